```python
import jax
import jax.numpy as jnp
from jax import lax
import numpy as np

D_MODEL = 1024
BATCH = 4
SEQ = 4096
DEPTH = 2

N_BRANCH = 4
BRANCH_W = D_MODEL // 2
EPS = 1e-6
MAX_POS_OFFSET = 1024
S5_CH_PER_GROUP = 16
S5_GROUPS = BRANCH_W // S5_CH_PER_GROUP
S5_STATE = 64
S5_MAX_REAL = -1e-4
HG_HEADS = 4
HG_DK = BRANCH_W // HG_HEADS
HG_DV = BRANCH_W // HG_HEADS
HG_CHUNK = 64
RET_HEADS = 4
RET_DK = 64
RET_DV = BRANCH_W // RET_HEADS
RET_CHUNK = 64
ROPE_BASE = 10000.0
M2_HEADS = 8
M2_HEADDIM = BRANCH_W // M2_HEADS
M2_GROUPS = 2
M2_STATE = 128
M2_CONV = 4
M2_CHUNK = 64
M2_XBC = BRANCH_W + 2 * M2_GROUPS * M2_STATE
MOE_GROUPS = 4
MOE_EXPERTS_PER_GROUP = 8
MOE_TOPK = 2
MOE_FF = 256
IN_SIZES = (BRANCH_W,
            HG_HEADS * HG_DK, HG_HEADS * HG_DK, HG_HEADS * HG_DV, HG_HEADS * HG_DV,
            RET_HEADS * RET_DK, RET_HEADS * RET_DK, RET_HEADS * RET_DV, RET_HEADS * RET_DV,
            BRANCH_W, M2_XBC, M2_HEADS)
IN_W = BRANCH_W + 2 * HG_HEADS * (HG_DK + HG_DV) + 2 * RET_HEADS * (RET_DK + RET_DV) + BRANCH_W + M2_XBC + M2_HEADS

kernel_name = 'hybrid_gated_ssm_retention_moe_trunk'


def _rms(t):
    t32 = t.astype(jnp.float32)
    return (t32 * lax.rsqrt(jnp.mean(t32 * t32, axis=-1, keepdims=True) + EPS)).astype(t.dtype)


def _rope(t, positions):
    half = t.shape[-1] // 2
    inv_freq = ROPE_BASE ** (-jnp.arange(half, dtype=jnp.float32) / half)
    ang = positions.astype(jnp.float32)[:, :, None, None] * inv_freq
    cos, sin = jnp.cos(ang), jnp.sin(ang)
    t1 = t[..., :half].astype(jnp.float32)
    t2 = t[..., half:].astype(jnp.float32)
    return jnp.concatenate([t1 * cos - t2 * sin, t2 * cos + t1 * sin], axis=-1).astype(t.dtype)


def _chunk_gated_linear_attention(q, k, v, log_f, chunk):
    out_dtype = v.dtype
    f32 = jnp.float32
    bsz, nh, seq, dk = q.shape
    dv = v.shape[-1]
    n_chunks = seq // chunk

    def to_chunks(t):
        t = t.astype(f32).reshape(bsz, nh, n_chunks, chunk, t.shape[-1])
        return jnp.moveaxis(t, 2, 0)

    causal = jnp.tril(jnp.ones((chunk, chunk), dtype=bool))[:, :, None]

    def step(state, inp):
        qc, kc, vc, fc = inp
        b = jnp.cumsum(fc, axis=-2)
        diff = b[..., :, None, :] - b[..., None, :, :]
        decay = jnp.where(causal, jnp.exp(jnp.minimum(diff, 0.0)), 0.0)
        scores = jnp.sum(qc[..., :, None, :] * kc[..., None, :, :] * decay, axis=-1)
        o = (jnp.einsum('bhts,bhsv->bhtv', scores, vc)
             + jnp.einsum('bhtk,bhkv->bhtv', qc * jnp.exp(b), state))
        b_last = b[..., -1:, :]
        state = (jnp.exp(b_last[..., 0, :])[..., None] * state
                 + jnp.einsum('bhsk,bhsv->bhkv', kc * jnp.exp(b_last - b), vc))
        return state, o

    state0 = jnp.zeros((bsz, nh, dk, dv), f32)
    _, o = lax.scan(step, state0, tuple(map(to_chunks, (q, k, v, log_f))))
    o = jnp.moveaxis(o, 0, 2).reshape(bsz, nh, seq, dv)
    return o.astype(out_dtype)


def _ssd_chunked(x, dt, a, bm, cm, chunk):
    f32 = jnp.float32
    bsz, seq, nh, hp = x.shape
    ns = bm.shape[-1]
    n_chunks = seq // chunk
    xd = (x.astype(f32) * dt[..., None]).reshape(bsz, n_chunks, chunk, nh, hp)
    bc = bm.astype(f32).reshape(bsz, n_chunks, chunk, nh, ns)
    cc = cm.astype(f32).reshape(bsz, n_chunks, chunk, nh, ns)
    a_cs = jnp.cumsum((dt * a).reshape(bsz, n_chunks, chunk, nh).transpose(0, 3, 1, 2), axis=-1)
    causal = jnp.tril(jnp.ones((chunk, chunk), dtype=bool))
    seg = a_cs[..., :, None] - a_cs[..., None, :]
    l_mat = jnp.where(causal, jnp.exp(jnp.minimum(seg, 0.0)), 0.0)
    scores = jnp.einsum('bclhn,bcshn->bhcls', cc, bc) * l_mat
    y_diag = jnp.einsum('bhcls,bcshp->bclhp', scores, xd)
    decay_to_end = jnp.exp(a_cs[..., -1:] - a_cs)
    chunk_states = jnp.einsum('bclhn,bhcl,bclhp->bchpn', bc, decay_to_end, xd)
    chunk_decay = jnp.exp(a_cs[..., -1])

    def step(state, inp):
        dec, st = inp
        return dec[..., None, None] * state + st, state

    _, prev = lax.scan(step, jnp.zeros((bsz, nh, hp, ns), f32),
                       (jnp.moveaxis(chunk_decay, 2, 0), jnp.moveaxis(chunk_states, 1, 0)))
    prev = jnp.moveaxis(prev, 0, 1)
    y_off = jnp.einsum('bclhn,bchpn,bhcl->bclhp', cc, prev, jnp.exp(a_cs))
    return (y_diag + y_off).reshape(bsz, seq, nh, hp)


def _s5_mixer(u, lam_re, lam_im, b_re, b_im, c_re, c_im, d_skip, log_dt, w_glu):
    f32 = jnp.float32
    bsz, seq, _ = u.shape
    ug = u.astype(f32).reshape(bsz, seq, S5_GROUPS, S5_CH_PER_GROUP)
    lam = lax.complex(jnp.minimum(lam_re.astype(f32), S5_MAX_REAL), lam_im.astype(f32))
    step = jnp.exp(log_dt.astype(f32))[:, None]
    lam_bar = jnp.exp(lam * step)
    b_bar = ((lam_bar - 1.0) / lam)[..., None] * lax.complex(b_re.astype(f32), b_im.astype(f32))
    c_mat = lax.complex(c_re.astype(f32), c_im.astype(f32))
    bu = jnp.einsum('blgh,gph->blgp', ug.astype(jnp.complex64), b_bar)
    a = jnp.broadcast_to(lam_bar, (1, seq, S5_GROUPS, S5_STATE))

    def combine(e1, e2):
        a1, b1 = e1
        a2, b2 = e2
        return a2 * a1, a2 * b1 + b2

    _, states = lax.associative_scan(combine, (a, bu), axis=1)
    y = (jnp.einsum('blgp,ghp->blgh', states, c_mat).real
         + d_skip.astype(f32).reshape(S5_GROUPS, S5_CH_PER_GROUP) * ug)
    y = jax.nn.gelu(y.reshape(bsz, seq, BRANCH_W)).astype(u.dtype)
    return y * jax.nn.sigmoid(y @ w_glu)


def _hgrn2_mixer(q, f, i, g, lower_bound, norm_w):
    bsz, seq, _ = q.shape

    def heads(t, dh):
        return t.reshape(bsz, seq, HG_HEADS, dh).transpose(0, 2, 1, 3)

    q = jax.nn.silu(q)
    log_f = jnp.logaddexp(jnp.log(lower_bound), jnp.log1p(-lower_bound) + jax.nn.log_sigmoid(f.astype(jnp.float32)))
    k = -jnp.expm1(log_f)
    o = _chunk_gated_linear_attention(heads(q, HG_DK), heads(k, HG_DK), heads(i, HG_DV),
                                      heads(log_f, HG_DK), HG_CHUNK)
    o = _rms(o.transpose(0, 2, 1, 3)) * norm_w
    return o.reshape(bsz, seq, BRANCH_W) * jax.nn.silu(g)


def _retention_mixer(q, k, v, g, positions):
    bsz, seq, _ = q.shape
    qh = _rope(q.reshape(bsz, seq, RET_HEADS, RET_DK), positions)
    kh = _rope(k.reshape(bsz, seq, RET_HEADS, RET_DK), positions) * (RET_DK ** -0.5)
    vh = v.reshape(bsz, seq, RET_HEADS, RET_DV)
    log_gamma = jnp.log1p(-jnp.exp2(-5.0 - jnp.arange(RET_HEADS, dtype=jnp.float32)))
    log_f = jnp.broadcast_to(log_gamma[None, :, None, None], (bsz, RET_HEADS, seq, 1))
    o = _chunk_gated_linear_attention(qh.transpose(0, 2, 1, 3), kh.transpose(0, 2, 1, 3),
                                      vh.transpose(0, 2, 1, 3), log_f, RET_CHUNK)
    o = _rms(o.transpose(0, 2, 1, 3)).reshape(bsz, seq, BRANCH_W)
    return o * jax.nn.silu(g)


def _mamba2_mixer(z, xbc, dt_raw, conv_w, conv_b, dt_bias, a_log, d_skip, norm_w):
    bsz, seq, _ = xbc.shape
    xbc = lax.conv_general_dilated(xbc, conv_w[:, None, :], window_strides=(1,),
                                   padding=((M2_CONV - 1, 0),),
                                   dimension_numbers=('NWC', 'WIO', 'NWC'),
                                   feature_group_count=M2_XBC)
    xbc = jax.nn.silu(xbc + conv_b)
    xs, bm, cm = jnp.split(xbc, [BRANCH_W, BRANCH_W + M2_GROUPS * M2_STATE], axis=-1)
    heads_per_group = M2_HEADS // M2_GROUPS
    xs = xs.reshape(bsz, seq, M2_HEADS, M2_HEADDIM)
    bm = jnp.repeat(bm.reshape(bsz, seq, M2_GROUPS, M2_STATE), heads_per_group, axis=2)
    cm = jnp.repeat(cm.reshape(bsz, seq, M2_GROUPS, M2_STATE), heads_per_group, axis=2)
    dt = jax.nn.softplus(dt_raw.astype(jnp.float32) + dt_bias.astype(jnp.float32))
    a = -jnp.exp(a_log.astype(jnp.float32))
    y = _ssd_chunked(xs, dt, a, bm, cm, M2_CHUNK) + d_skip.astype(jnp.float32)[:, None] * xs.astype(jnp.float32)
    y = y.reshape(bsz, seq, BRANCH_W).astype(z.dtype)
    return _rms(y * jax.nn.silu(z)) * norm_w


def _hier_moe(h, w_group, b_group, w_expert, b_expert, w1, w3, w2):
    bsz, seq, d = h.shape
    t = h.reshape(bsz * seq, d)
    g_logits = (t @ w_group + b_group).astype(jnp.float32)
    g_prob = jax.nn.softmax(g_logits, axis=-1)
    g_sel = jnp.argmax(g_logits, axis=-1)
    g_w = jnp.take_along_axis(g_prob, g_sel[:, None], axis=1)
    e_logits = (t @ w_expert + b_expert).astype(jnp.float32).reshape(-1, MOE_GROUPS, MOE_EXPERTS_PER_GROUP)
    e_in_group = jnp.take_along_axis(e_logits, g_sel[:, None, None], axis=1)[:, 0]
    top_v, top_i = lax.top_k(e_in_group, MOE_TOPK)
    top_w = jax.nn.softmax(top_v, axis=-1) * g_w
    comb_e = jnp.sum(jax.nn.one_hot(top_i, MOE_EXPERTS_PER_GROUP, dtype=jnp.float32) * top_w[..., None], axis=1)
    comb = (jax.nn.one_hot(g_sel, MOE_GROUPS, dtype=jnp.float32)[:, :, None] * comb_e[:, None, :]).astype(t.dtype)
    out = jnp.zeros_like(t)
    for grp in range(MOE_GROUPS):
        act = jax.nn.silu(jnp.einsum('td,edf->tef', t, w1[grp])) * jnp.einsum('td,edf->tef', t, w3[grp])
        out = out + jnp.einsum('tef,efd->td', act * comb[:, grp, :, None], w2[grp])
    return out.reshape(bsz, seq, d)


def setup_inputs(seed: int = 0) -> dict:
    key = jax.random.key(seed)
    keys = jax.random.split(key, 48)
    counter = [0]
    f32 = jnp.float32

    def nk():
        counter[0] += 1
        return keys[counter[0] - 1]

    def nrm(shape, scale):
        return scale * jax.random.normal(nk(), shape, f32)

    def unif(shape, lo, hi):
        return jax.random.uniform(nk(), shape, f32, lo, hi)

    L = DEPTH
    D = D_MODEL
    x = nrm((BATCH, SEQ, D), 1.0)
    c = nrm((BATCH, D), 1.0)
    start = jax.random.randint(nk(), (BATCH, 1), 0, MAX_POS_OFFSET, dtype=jnp.int32)
    positions = start + jnp.arange(SEQ, dtype=jnp.int32)[None, :]
    ada_w = nrm((L, D, 6 * D), 0.5 * D ** -0.5)
    ada_b = nrm((L, 6 * D), 0.02)
    w_in = nrm((L, D, IN_W), D ** -0.5)
    s5_lam_re = -0.5 + nrm((L, S5_GROUPS, S5_STATE), 0.01)
    s5_lam_im = np.pi * jnp.arange(S5_STATE, dtype=f32)[None, None, :] + nrm((L, S5_GROUPS, S5_STATE), 0.01)
    s5_b_re = nrm((L, S5_GROUPS, S5_STATE, S5_CH_PER_GROUP), (2 * S5_CH_PER_GROUP) ** -0.5)
    s5_b_im = nrm((L, S5_GROUPS, S5_STATE, S5_CH_PER_GROUP), (2 * S5_CH_PER_GROUP) ** -0.5)
    s5_c_re = nrm((L, S5_GROUPS, S5_CH_PER_GROUP, S5_STATE), (2 * S5_STATE) ** -0.5)
    s5_c_im = nrm((L, S5_GROUPS, S5_CH_PER_GROUP, S5_STATE), (2 * S5_STATE) ** -0.5)
    s5_d = nrm((L, BRANCH_W), 1.0)
    s5_log_dt = unif((L, S5_GROUPS), float(np.log(1e-3)), float(np.log(1e-1)))
    s5_w_glu = nrm((L, BRANCH_W, BRANCH_W), BRANCH_W ** -0.5)
    hg_lb_logits = nrm((L, HG_HEADS * HG_DK), 0.5)
    hg_norm_w = 1.0 + nrm((L, HG_DV), 0.01)
    m2_conv_w = nrm((L, M2_CONV, M2_XBC), M2_CONV ** -0.5)
    m2_conv_b = nrm((L, M2_XBC), 0.01)
    dt0 = jnp.exp(unif((L, M2_HEADS), float(np.log(1e-3)), float(np.log(1e-1))))
    m2_dt_bias = dt0 + jnp.log(-jnp.expm1(-dt0))
    m2_a_log = jnp.log(unif((L, M2_HEADS), 1.0, 16.0))
    m2_d = 1.0 + nrm((L, M2_HEADS), 0.01)
    m2_norm_w = 1.0 + nrm((L, BRANCH_W), 0.01)
    w_branch = nrm((L, N_BRANCH, BRANCH_W, D), BRANCH_W ** -0.5)
    w_gate = nrm((L, D, N_BRANCH * D), D ** -0.5)
    b_gate = nrm((L, N_BRANCH * D), 0.01)
    w_out = nrm((L, D, D), D ** -0.5)
    moe_w_group = nrm((L, D, MOE_GROUPS), D ** -0.5)
    moe_b_group = nrm((L, MOE_GROUPS), 0.01)
    moe_w_expert = nrm((L, D, MOE_GROUPS * MOE_EXPERTS_PER_GROUP), D ** -0.5)
    moe_b_expert = nrm((L, MOE_GROUPS * MOE_EXPERTS_PER_GROUP), 0.01)
    moe_w1 = nrm((L, MOE_GROUPS, MOE_EXPERTS_PER_GROUP, D, MOE_FF), D ** -0.5)
    moe_w3 = nrm((L, MOE_GROUPS, MOE_EXPERTS_PER_GROUP, D, MOE_FF), D ** -0.5)
    moe_w2 = nrm((L, MOE_GROUPS, MOE_EXPERTS_PER_GROUP, MOE_FF, D), MOE_FF ** -0.5)
    final_norm_w = 1.0 + nrm((D,), 0.01)
    return {'x': x, 'c': c, 'positions': positions, 'ada_w': ada_w, 'ada_b': ada_b, 'w_in': w_in,
            's5_lam_re': s5_lam_re, 's5_lam_im': s5_lam_im, 's5_b_re': s5_b_re, 's5_b_im': s5_b_im,
            's5_c_re': s5_c_re, 's5_c_im': s5_c_im, 's5_d': s5_d, 's5_log_dt': s5_log_dt, 's5_w_glu': s5_w_glu,
            'hg_lb_logits': hg_lb_logits, 'hg_norm_w': hg_norm_w,
            'm2_conv_w': m2_conv_w, 'm2_conv_b': m2_conv_b, 'm2_dt_bias': m2_dt_bias, 'm2_a_log': m2_a_log,
            'm2_d': m2_d, 'm2_norm_w': m2_norm_w,
            'w_branch': w_branch, 'w_gate': w_gate, 'b_gate': b_gate, 'w_out': w_out,
            'moe_w_group': moe_w_group, 'moe_b_group': moe_b_group, 'moe_w_expert': moe_w_expert,
            'moe_b_expert': moe_b_expert, 'moe_w1': moe_w1, 'moe_w3': moe_w3, 'moe_w2': moe_w2,
            'final_norm_w': final_norm_w}


def reference(x, c, positions, ada_w, ada_b, w_in, s5_lam_re, s5_lam_im, s5_b_re, s5_b_im, s5_c_re, s5_c_im,
              s5_d, s5_log_dt, s5_w_glu, hg_lb_logits, hg_norm_w, m2_conv_w, m2_conv_b, m2_dt_bias, m2_a_log,
              m2_d, m2_norm_w, w_branch, w_gate, b_gate, w_out, moe_w_group, moe_b_group, moe_w_expert,
              moe_b_expert, moe_w1, moe_w3, moe_w2, final_norm_w):
    bsz, seq, d = x.shape
    split_at = []
    acc = 0
    for size in IN_SIZES[:-1]:
        acc += size
        split_at.append(acc)
    lb_cum = jnp.cumsum(jax.nn.softmax(hg_lb_logits.astype(jnp.float32), axis=0), axis=0)
    hg_lower_bounds = lb_cum - lb_cum[:1]
    cond = jax.nn.silu(c)
    for layer in range(DEPTH):
        mod = (cond @ ada_w[layer] + ada_b[layer]).reshape(bsz, 6, 1, d)
        shift_mix, scale_mix, gate_mix, shift_ffn, scale_ffn, gate_ffn = (mod[:, j] for j in range(6))
        h = _rms(x) * (1.0 + scale_mix) + shift_mix
        (s5_u, hg_q, hg_f, hg_i, hg_g, ret_q, ret_k, ret_v, ret_g,
         m2_z, m2_xbc, m2_dt) = jnp.split(h @ w_in[layer], split_at, axis=-1)
        y_s5 = _s5_mixer(s5_u, s5_lam_re[layer], s5_lam_im[layer], s5_b_re[layer], s5_b_im[layer],
                         s5_c_re[layer], s5_c_im[layer], s5_d[layer], s5_log_dt[layer], s5_w_glu[layer])
        y_hg = _hgrn2_mixer(hg_q, hg_f, hg_i, hg_g, hg_lower_bounds[layer], hg_norm_w[layer])
        y_ret = _retention_mixer(ret_q, ret_k, ret_v, ret_g, positions)
        y_m2 = _mamba2_mixer(m2_z, m2_xbc, m2_dt, m2_conv_w[layer], m2_conv_b[layer], m2_dt_bias[layer],
                             m2_a_log[layer], m2_d[layer], m2_norm_w[layer])
        branches = jnp.einsum('blnw,nwd->blnd', jnp.stack([y_s5, y_hg, y_ret, y_m2], axis=2), w_branch[layer])
        gates = jax.nn.sigmoid(h @ w_gate[layer] + b_gate[layer]).reshape(bsz, seq, N_BRANCH, d)
        x = x + gate_mix * (jnp.sum(gates * branches, axis=2) @ w_out[layer])
        h = _rms(x) * (1.0 + scale_ffn) + shift_ffn
        x = x + gate_ffn * _hier_moe(h, moe_w_group[layer], moe_b_group[layer], moe_w_expert[layer],
                                     moe_b_expert[layer], moe_w1[layer], moe_w3[layer], moe_w2[layer])
    return _rms(x) * final_norm_w
```

```python
import functools
import math

import numpy as np
import jax
import jax.numpy as jnp
from jax import lax
from jax.experimental import pallas as pl
from jax.experimental.pallas import tpu as pltpu

F32 = jnp.float32
BF16 = jnp.bfloat16
HIGHEST = lax.Precision.HIGHEST

D_MODEL = 1024
DEPTH = 2
BRANCH_W = 512
EPS = 1e-6
S5_GROUPS = 32
S5_CH = 16
S5_STATE = 64
S5_MAX_REAL = -1e-4
S5_BLOCK = 16
HG_HEADS = 4
HG_DK = 128
RET_HEADS = 4
RET_DK = 64
RET_DV = 128
ROPE_BASE = 10000.0
M2_HEADS = 8
M2_HEADDIM = 64
M2_GROUPS = 2
M2_STATE = 128
M2_CONV = 4
MOE_GROUPS = 4
MOE_EPG = 8
MOE_EXPERTS = MOE_GROUPS * MOE_EPG
MOE_FF = 256

COL_S5, COL_HQ, COL_HF, COL_HI, COL_HG = 0, 512, 1024, 1536, 2048
COL_RQ, COL_RK, COL_RV, COL_RG = 2560, 2816, 3072, 3584
COL_MZ, COL_MXS, COL_MBC, COL_MDT = 4096, 4608, 5120, 5632
IN_W = 5640
IN_W_PAD = 5760

LANE = 128
VMEM_LIMIT = 56 * 1024 * 1024

TM_PROJ = 1024
TN_PROJ = 1152
TM_MERGE = 512
C_RET = 256
C_SSD = 256
C_HG = 128
TM_X = 256
TM_COMB = 512
TM_DISP = 2048


def _cparams(sem):
    return pltpu.CompilerParams(dimension_semantics=sem, vmem_limit_bytes=VMEM_LIMIT)


def _silu(v):
    return v * jax.nn.sigmoid(v)


def _dot_nt(a, b, **kw):
    return lax.dot_general(a, b, (((1,), (1,)), ((), ())), preferred_element_type=F32, **kw)


def _dot_tn(a, b, **kw):
    return lax.dot_general(a, b, (((0,), (0,)), ((), ())), preferred_element_type=F32, **kw)


def _ada_kernel(c_ref, w_ref, b_ref, o_ref):
    cond = _silu(c_ref[...])
    o_ref[...] = jnp.dot(cond, w_ref[...], preferred_element_type=F32, precision=HIGHEST) + b_ref[...]


def _ada_mod(c_pad, ada_w, ada_b):
    depth, d, n = ada_w.shape
    tn = 1536
    return pl.pallas_call(
        _ada_kernel,
        grid=(depth, n // tn),
        in_specs=[pl.BlockSpec((8, d), lambda l, j: (0, 0)),
                  pl.BlockSpec((None, d, tn), lambda l, j: (l, 0, j)),
                  pl.BlockSpec((None, 1, tn), lambda l, j: (l, 0, j))],
        out_specs=pl.BlockSpec((None, 8, tn), lambda l, j: (l, 0, j)),
        out_shape=jax.ShapeDtypeStruct((depth, 8, n), F32),
        compiler_params=_cparams(("parallel", "parallel")),
        name="ada_mod",
    )(c_pad, ada_w, ada_b.reshape(depth, 1, n))


def _modulated_norm(x, scale, shift):
    ms = jnp.mean(x * x, axis=-1, keepdims=True)
    return x * lax.rsqrt(ms + EPS) * (1.0 + scale) + shift


def _inproj_kernel(x_ref, sc_ref, sh_ref, w_ref, o_ref, h_scr):
    @pl.when(pl.program_id(1) == 0)
    def _():
        h_scr[...] = _modulated_norm(x_ref[...], sc_ref[...], sh_ref[...]).astype(BF16)

    o_ref[...] = jnp.dot(h_scr[...], w_ref[...], preferred_element_type=F32)


def _in_proj(x2, mod3, w_pad, seq):
    t, d = x2.shape
    tpb = seq // TM_PROJ
    return pl.pallas_call(
        _inproj_kernel,
        grid=(t // TM_PROJ, IN_W_PAD // TN_PROJ),
        in_specs=[pl.BlockSpec((TM_PROJ, d), lambda i, j: (i, 0)),
                  pl.BlockSpec((None, 1, d), lambda i, j: ((i // tpb) * 6 + 1, 0, 0)),
                  pl.BlockSpec((None, 1, d), lambda i, j: ((i // tpb) * 6 + 0, 0, 0)),
                  pl.BlockSpec((d, TN_PROJ), lambda i, j: (0, j))],
        out_specs=pl.BlockSpec((TM_PROJ, TN_PROJ), lambda i, j: (i, j)),
        out_shape=jax.ShapeDtypeStruct((t, IN_W_PAD), F32),
        scratch_shapes=[pltpu.VMEM((TM_PROJ, d), BF16)],
        compiler_params=_cparams(("parallel", "arbitrary")),
        name="in_proj",
    )(x2, mod3, mod3, w_pad)


def _rope_kernel(pos_ref, invf_ref, sgn_ref, cos_ref, sin_ref):
    ang = pos_ref[...].astype(F32) * invf_ref[...]
    cos_ref[...] = jnp.cos(ang)
    sin_ref[...] = jnp.sin(ang) * sgn_ref[...]


def _rope_tables(pos_col, invf_row, sgn_row):
    t = pos_col.shape[0]
    w = invf_row.shape[1]
    tm = 1024
    return pl.pallas_call(
        _rope_kernel,
        grid=(t // tm,),
        in_specs=[pl.BlockSpec((tm, 1), lambda i: (i, 0)),
                  pl.BlockSpec((1, w), lambda i: (0, 0)),
                  pl.BlockSpec((1, w), lambda i: (0, 0))],
        out_specs=[pl.BlockSpec((tm, w), lambda i: (i, 0))] * 2,
        out_shape=[jax.ShapeDtypeStruct((t, w), F32)] * 2,
        compiler_params=_cparams(("parallel",)),
        name="rope_tables",
    )(pos_col, invf_row, sgn_row)


def _ret_kernel(q_ref, k_ref, v_ref, g_ref, cos_ref, sin_ref, o_ref, st_ref, *, chunk):
    @pl.when(pl.program_id(1) == 0)
    def _():
        st_ref[...] = jnp.zeros_like(st_ref)

    cosf = cos_ref[...]
    sinf = sin_ref[...]
    width = RET_HEADS * RET_DK
    lane = lax.broadcasted_iota(jnp.int32, (chunk, width), 1)
    first_half = (lane % RET_DK) < (RET_DK // 2)

    def rope(t):
        partner = jnp.where(first_half, pltpu.roll(t, width - RET_DK // 2, 1), pltpu.roll(t, RET_DK // 2, 1))
        return t * cosf + partner * sinf

    q = rope(q_ref[...])
    k = rope(k_ref[...]) * (RET_DK ** -0.5)
    v = v_ref[...]
    g = g_ref[...]
    ti = lax.broadcasted_iota(jnp.int32, (chunk, chunk), 0)
    si = lax.broadcasted_iota(jnp.int32, (chunk, chunk), 1)
    lag = (ti - si).astype(F32)
    tcol = lax.broadcasted_iota(jnp.int32, (chunk, 1), 0).astype(F32)
    for h in range(RET_HEADS):
        log_gamma = math.log1p(-(2.0 ** (-5.0 - h)))
        qh = q[:, h * RET_DK:(h + 1) * RET_DK]
        kh = k[:, h * RET_DK:(h + 1) * RET_DK]
        vh = v[:, h * RET_DV:(h + 1) * RET_DV].astype(BF16)
        decay = jnp.where(ti >= si, jnp.exp(jnp.minimum(lag * log_gamma, 0.0)), 0.0)
        scores = _dot_nt(qh.astype(BF16), kh.astype(BF16)) * decay
        state = st_ref[h]
        q_in = qh * jnp.exp(log_gamma * (tcol + 1.0))
        o = (jnp.dot(scores.astype(BF16), vh, preferred_element_type=F32)
             + jnp.dot(q_in.astype(BF16), state.astype(BF16), preferred_element_type=F32))
        k_out = kh * jnp.exp(log_gamma * (chunk - 1.0 - tcol))
        st_ref[h] = math.exp(log_gamma * chunk) * state + _dot_tn(k_out.astype(BF16), vh)
        o = o * lax.rsqrt(jnp.mean(o * o, axis=-1, keepdims=True) + EPS)
        gh = g[:, h * RET_DV:(h + 1) * RET_DV]
        o_ref[:, h * RET_DV:(h + 1) * RET_DV] = o * _silu(gh)


def _retention(p3, cos3, sin3):
    b, seq, _ = p3.shape
    c = C_RET
    qk_w = RET_HEADS * RET_DK
    return pl.pallas_call(
        functools.partial(_ret_kernel, chunk=c),
        grid=(b, seq // c),
        in_specs=[pl.BlockSpec((None, c, qk_w), lambda i, j: (i, j, COL_RQ // qk_w)),
                  pl.BlockSpec((None, c, qk_w), lambda i, j: (i, j, COL_RK // qk_w)),
                  pl.BlockSpec((None, c, BRANCH_W), lambda i, j: (i, j, COL_RV // BRANCH_W)),
                  pl.BlockSpec((None, c, BRANCH_W), lambda i, j: (i, j, COL_RG // BRANCH_W)),
                  pl.BlockSpec((None, c, qk_w), lambda i, j: (i, j, 0)),
                  pl.BlockSpec((None, c, qk_w), lambda i, j: (i, j, 0))],
        out_specs=pl.BlockSpec((None, c, BRANCH_W), lambda i, j: (i, j, 0)),
        out_shape=jax.ShapeDtypeStruct((b, seq, BRANCH_W), F32),
        scratch_shapes=[pltpu.VMEM((RET_HEADS, RET_DK, RET_DV), F32)],
        compiler_params=_cparams(("parallel", "arbitrary")),
        name="retention",
    )(p3, p3, p3, p3, cos3, sin3)


def _ssd_kernel(z_ref, xs_ref, bc_ref, dt_ref, tri_ref, cw_ref, cb_ref, dtb_ref, alog_ref, dsk_ref, nw_ref,
                o_ref, xe_scr, st_ref, *, chunk):
    j = pl.program_id(1)
    width = 2 * BRANCH_W

    @pl.when(j == 0)
    def _():
        st_ref[...] = jnp.zeros_like(st_ref)
        xe_scr[0:8, :] = jnp.zeros((8, width), F32)

    @pl.when(j > 0)
    def _():
        xe_scr[0:8, :] = xe_scr[chunk:chunk + 8, :]

    xe_scr[8:, 0:BRANCH_W] = xs_ref[...]
    xe_scr[8:, BRANCH_W:] = bc_ref[...]
    conv = cb_ref[...] + cw_ref[M2_CONV - 1:M2_CONV, :] * xe_scr[8:, :]
    for tap in range(M2_CONV - 1):
        conv = conv + cw_ref[tap:tap + 1, :] * xe_scr[pl.ds(8 - (M2_CONV - 1) + tap, chunk), :]
    conv = _silu(conv)
    xs = conv[:, :BRANCH_W]
    bm = conv[:, BRANCH_W:BRANCH_W + M2_GROUPS * M2_STATE]
    cm = conv[:, BRANCH_W + M2_GROUPS * M2_STATE:]

    dt = jax.nn.softplus(dt_ref[...] + dtb_ref[...])
    da = dt * (-jnp.exp(alog_ref[...]))
    a_cs = jnp.dot(tri_ref[...], da, preferred_element_type=F32, precision=HIGHEST)
    a_cs_t = a_cs.T
    ti = lax.broadcasted_iota(jnp.int32, (chunk, chunk), 0)
    si = lax.broadcasted_iota(jnp.int32, (chunk, chunk), 1)
    causal = ti >= si
    hpg = M2_HEADS // M2_GROUPS
    ys = []
    for grp in range(M2_GROUPS):
        bm_g = bm[:, grp * M2_STATE:(grp + 1) * M2_STATE]
        cm_g = cm[:, grp * M2_STATE:(grp + 1) * M2_STATE]
        cb = _dot_nt(cm_g.astype(BF16), bm_g.astype(BF16))
        for hh in range(hpg):
            h = grp * hpg + hh
            col = a_cs[:, h:h + 1]
            row = a_cs_t[h:h + 1, :]
            lmat = jnp.where(causal, jnp.exp(jnp.minimum(col - row, 0.0)), 0.0)
            xd = xs[:, h * M2_HEADDIM:(h + 1) * M2_HEADDIM] * dt[:, h:h + 1]
            state = st_ref[h]
            y = (jnp.dot((cb * lmat).astype(BF16), xd.astype(BF16), preferred_element_type=F32)
                 + jnp.dot((cm_g * jnp.exp(col)).astype(BF16), state.astype(BF16), preferred_element_type=F32))
            a_last = a_cs[chunk - 1:chunk, h:h + 1]
            to_end = jnp.exp(a_last - col)
            st_ref[h] = jnp.exp(a_last) * state + _dot_tn(bm_g.astype(BF16), (xd * to_end).astype(BF16))
            ys.append(y)
    y = jnp.concatenate(ys, axis=-1) + dsk_ref[...] * xs
    y = y * _silu(z_ref[...])
    o_ref[...] = y * lax.rsqrt(jnp.mean(y * y, axis=-1, keepdims=True) + EPS) * nw_ref[...]


def _ssd(p3, tri, conv_w, conv_b, dt_bias_row, a_log_row, d_skip_row, norm_w_row):
    b, seq, _ = p3.shape
    c = C_SSD
    const = lambda shape: pl.BlockSpec(shape, lambda i, j: (0,) * len(shape))
    return pl.pallas_call(
        functools.partial(_ssd_kernel, chunk=c),
        grid=(b, seq // c),
        in_specs=[pl.BlockSpec((None, c, BRANCH_W), lambda i, j: (i, j, COL_MZ // BRANCH_W)),
                  pl.BlockSpec((None, c, BRANCH_W), lambda i, j: (i, j, COL_MXS // BRANCH_W)),
                  pl.BlockSpec((None, c, BRANCH_W), lambda i, j: (i, j, COL_MBC // BRANCH_W)),
                  pl.BlockSpec((None, c, LANE), lambda i, j: (i, j, COL_MDT // LANE)),
                  const((c, c)), const((M2_CONV, 2 * BRANCH_W)), const((1, 2 * BRANCH_W)),
                  const((1, LANE)), const((1, LANE)), const((1, BRANCH_W)), const((1, BRANCH_W))],
        out_specs=pl.BlockSpec((None, c, BRANCH_W), lambda i, j: (i, j, 0)),
        out_shape=jax.ShapeDtypeStruct((b, seq, BRANCH_W), F32),
        scratch_shapes=[pltpu.VMEM((c + 8, 2 * BRANCH_W), F32),
                        pltpu.VMEM((M2_HEADS, M2_STATE, M2_HEADDIM), F32)],
        compiler_params=_cparams(("parallel", "arbitrary")),
        name="ssd",
    )(p3, p3, p3, p3, tri, conv_w, conv_b, dt_bias_row, a_log_row, d_skip_row, norm_w_row)


def _hg_tables(chunk):
    n_lev = int(math.log2(chunk))
    r = np.arange(chunk)[:, None]
    jj = np.arange(chunk)[None, :]
    mats = [(jj <= r), (jj > r)]
    for lev in range(n_lev):
        m = 1 << lev
        mid = (r >> (lev + 1) << (lev + 1)) + m - 1
        second = ((r >> lev) & 1) == 1
        mats.append(np.where(second, (jj > mid) & (jj <= r), (jj > r) & (jj <= mid)))
    summat = np.concatenate(mats, axis=0).astype(np.float32)
    x = r ^ jj
    levmap = np.where(r > jj, np.floor(np.log2(x + 0.5)), np.where(r == jj, -1, -2)).astype(np.int32)
    return summat, levmap, n_lev


def _hg_kernel(q_ref, f_ref, i_ref, g_ref, sum_ref, lev_ref, llb_ref, l1m_ref, nw_ref, o_ref, st_ref,
               *, chunk, n_lev):
    @pl.when(pl.program_id(1) == 0)
    def _():
        st_ref[...] = jnp.zeros_like(st_ref)

    f = f_ref[...]
    log_sig = jnp.minimum(f, 0.0) - jnp.log1p(jnp.exp(-jnp.abs(f)))
    a = llb_ref[...]
    bb = l1m_ref[...] + log_sig
    log_f = jnp.maximum(a, bb) + jnp.log1p(jnp.exp(-jnp.abs(a - bb)))
    k_all = jnp.exp(l1m_ref[...]) * jax.nn.sigmoid(-f)
    q_all = _silu(q_ref[...])
    hi = log_f.astype(BF16)
    r1 = log_f - hi.astype(F32)
    mid = r1.astype(BF16)
    lo = (r1 - mid.astype(F32)).astype(BF16)
    summat = sum_ref[...]
    sums = (jnp.dot(summat, hi, preferred_element_type=F32)
            + jnp.dot(summat, mid, preferred_element_type=F32)
            + jnp.dot(summat, lo, preferred_element_type=F32))
    levmap = lev_ref[...]
    v_all = i_ref[...]
    g_all = g_ref[...]
    for h in range(HG_HEADS):
        sl = slice(h * HG_DK, (h + 1) * HG_DK)
        qh = q_all[:, sl]
        kh = k_all[:, sl]
        vh = v_all[:, sl].astype(BF16)
        b_h = sums[0:chunk, sl]
        to_end = sums[chunk:2 * chunk, sl]
        amat = jnp.where(levmap == -1, _dot_nt(qh.astype(BF16), kh.astype(BF16)), 0.0)
        for lev in range(n_lev):
            e = jnp.exp(sums[(2 + lev) * chunk:(3 + lev) * chunk, sl])
            a_l = _dot_nt((qh * e).astype(BF16), (kh * e).astype(BF16))
            amat = jnp.where(levmap == lev, a_l, amat)
        state_t = st_ref[h]
        o = (jnp.dot(amat.astype(BF16), vh, preferred_element_type=F32)
             + _dot_nt((qh * jnp.exp(b_h)).astype(BF16), state_t.astype(BF16)))
        k_end = kh * jnp.exp(to_end)
        st_ref[h] = jnp.exp(b_h[chunk - 1:chunk, :]) * state_t + _dot_tn(vh, k_end.astype(BF16))
        o = o * lax.rsqrt(jnp.mean(o * o, axis=-1, keepdims=True) + EPS) * nw_ref[...]
        o_ref[:, sl] = o * _silu(g_all[:, sl])


def _hgrn2(p3, log_lb_row, log1m_lb_row, norm_w_row):
    b, seq, _ = p3.shape
    c = C_HG
    summat, levmap, n_lev = _hg_tables(c)
    const = lambda shape: pl.BlockSpec(shape, lambda i, j: (0,) * len(shape))
    blk = lambda col: pl.BlockSpec((None, c, BRANCH_W), lambda i, j: (i, j, col // BRANCH_W))
    return pl.pallas_call(
        functools.partial(_hg_kernel, chunk=c, n_lev=n_lev),
        grid=(b, seq // c),
        in_specs=[blk(COL_HQ), blk(COL_HF), blk(COL_HI), blk(COL_HG),
                  const(summat.shape), const((c, c)),
                  const((1, BRANCH_W)), const((1, BRANCH_W)), const((1, HG_DK))],
        out_specs=pl.BlockSpec((None, c, BRANCH_W), lambda i, j: (i, j, 0)),
        out_shape=jax.ShapeDtypeStruct((b, seq, BRANCH_W), F32),
        scratch_shapes=[pltpu.VMEM((HG_HEADS, HG_DK, HG_DK), F32)],
        compiler_params=_cparams(("parallel", "arbitrary")),
        name="hgrn2",
    )(p3, p3, p3, p3, jnp.asarray(summat, BF16), jnp.asarray(levmap), log_lb_row, log1m_lb_row, norm_w_row)


def _s5_kernel(x_ref, tq_ref, bq_ref, cq_ref, lam_ref, o_ref, w_scr, s_scr, *, rows):
    x = x_ref[...]
    half = w_scr.shape[1] // 2
    w_scr[...] = jnp.dot(x, bq_ref[...], preferred_element_type=F32)
    lam_re = lam_ref[0:1, :]
    lam_im = lam_ref[1:2, :]

    def body(j, carry):
        s_re, s_im = carry
        s_scr[pl.ds(j, 1), 0:half] = s_re
        s_scr[pl.ds(j, 1), half:] = s_im
        w_re = w_scr[pl.ds(j, 1), 0:half]
        w_im = w_scr[pl.ds(j, 1), half:]
        return lam_re * s_re - lam_im * s_im + w_re, lam_re * s_im + lam_im * s_re + w_im

    zero = jnp.zeros((1, half), F32)
    lax.fori_loop(0, rows, body, (zero, zero))
    o_ref[...] = (jnp.dot(x, tq_ref[...], preferred_element_type=F32)
                  + jnp.dot(s_scr[...].astype(BF16), cq_ref[...], preferred_element_type=F32))


def _s5_scan(u6, tq, bq, cq, lam16, batch):
    nq, rows_all, kdim = u6.shape
    rows = rows_all // batch
    ncol = bq.shape[2]
    return pl.pallas_call(
        functools.partial(_s5_kernel, rows=rows),
        grid=(nq, batch),
        in_specs=[pl.BlockSpec((None, rows, kdim), lambda q, b: (q, b, 0)),
                  pl.BlockSpec((None, kdim, kdim), lambda q, b: (q, 0, 0)),
                  pl.BlockSpec((None, kdim, ncol), lambda q, b: (q, 0, 0)),
                  pl.BlockSpec((None, ncol, kdim), lambda q, b: (q, 0, 0)),
                  pl.BlockSpec((None, 2, ncol // 2), lambda q, b: (q, 0, 0))],
        out_specs=pl.BlockSpec((None, rows, kdim), lambda q, b: (q, b, 0)),
        out_shape=jax.ShapeDtypeStruct((nq, rows_all, kdim), F32),
        scratch_shapes=[pltpu.VMEM((rows, ncol), F32), pltpu.VMEM((rows, ncol), F32)],
        compiler_params=_cparams(("parallel", "parallel")),
        name="s5_scan",
    )(u6, tq, bq, cq, lam16)


def _s5_operators(lam_re, lam_im, b_re, b_im, c_re, c_im, d_skip, log_dt):
    nb = S5_BLOCK
    gq = LANE // S5_CH
    nq = S5_GROUPS // gq
    lam = lax.complex(jnp.minimum(lam_re.astype(F32), S5_MAX_REAL), lam_im.astype(F32))
    step = jnp.exp(log_dt.astype(F32))[:, None]
    z = lam * step
    lam_bar = jnp.exp(z)
    b_bar = ((lam_bar - 1.0) / lam)[..., None] * lax.complex(b_re.astype(F32), b_im.astype(F32))
    c_mat = lax.complex(c_re.astype(F32), c_im.astype(F32))
    pw = jnp.exp(z[..., None] * jnp.arange(nb + 1, dtype=F32))
    kern = jnp.einsum('gop,gpd,gpi->gdio', c_mat, pw[..., :nb], b_bar, precision=HIGHEST).real
    tt = np.arange(nb)[:, None]
    ss = np.arange(nb)[None, :]
    lagidx = np.clip(ss - tt, 0, nb - 1).reshape(-1)
    toe = jnp.take(kern, jnp.asarray(lagidx), axis=1).reshape(S5_GROUPS, nb, nb, S5_CH, S5_CH)
    toe = jnp.where(jnp.asarray(ss >= tt)[None, :, :, None, None], toe, 0.0)
    skip = (jnp.asarray(np.eye(nb, dtype=np.float32))[None, :, :, None, None]
            * jnp.asarray(np.eye(S5_CH, dtype=np.float32))[None, None, None, :, :]
            * d_skip.astype(F32).reshape(S5_GROUPS, 1, 1, S5_CH, 1))
    toe = (toe + skip).transpose(0, 1, 3, 2, 4).reshape(nq, gq, nb, S5_CH, nb, S5_CH)
    eye_g = jnp.asarray(np.eye(gq, dtype=np.float32))
    tq = (toe.transpose(0, 2, 1, 3, 4, 5)[:, :, :, :, :, None, :]
          * eye_g[None, None, :, None, None, :, None]).reshape(nq, nb * LANE, nb * LANE)
    pw_rev = jnp.exp(z[..., None] * jnp.asarray(np.arange(nb - 1, -1, -1), F32))
    binc = pw_rev[:, :, :, None] * b_bar[:, :, None, :]
    binc = binc.transpose(0, 2, 3, 1).reshape(nq, gq, nb, S5_CH, S5_STATE)
    binc = jnp.stack([binc.real, binc.imag], axis=0)
    bq = (binc.transpose(1, 3, 2, 4, 0, 5)[:, :, :, :, :, None, :]
          * eye_g[None, None, :, None, None, :, None]).reshape(nq, nb * LANE, 2 * gq * S5_STATE)
    cm = c_mat.transpose(0, 2, 1)[:, :, None, :] * pw[..., 1:][:, :, :, None]
    cm = cm.reshape(nq, gq, S5_STATE, nb, S5_CH)
    cm = jnp.stack([cm.real, -cm.imag], axis=0)
    cq = (cm.transpose(1, 0, 2, 3, 4, 5)[:, :, :, :, :, None, :]
          * eye_g[None, None, :, None, None, :, None]).reshape(nq, 2 * gq * S5_STATE, nb * LANE)
    lam_n = pw[..., nb].reshape(nq, gq * S5_STATE)
    lam16 = jnp.stack([lam_n.real, lam_n.imag], axis=1)
    return tq.astype(BF16), bq.astype(BF16), cq.astype(BF16), lam16


def _s5(p, batch, seq, ops):
    nb = S5_BLOCK
    nq = BRANCH_W // LANE
    u = p[:, COL_S5:COL_S5 + BRANCH_W].astype(BF16)
    u6 = u.reshape(batch, seq // nb, nb, nq, LANE).transpose(3, 0, 1, 2, 4).reshape(nq, batch * seq // nb, nb * LANE)
    y6 = _s5_scan(u6, *ops, batch)
    return y6.reshape(nq, batch, seq // nb, nb, LANE).transpose(1, 2, 3, 0, 4).reshape(batch * seq, BRANCH_W)


def _merge_kernel(x_ref, sc_ref, sh_ref, gm_ref, ys5_ref, yhg_ref, yret_ref, ym2_ref,
                  wglu_ref, wbr_ref, wg_ref, bg_ref, wout_ref, o_ref):
    x = x_ref[...]
    d = x.shape[1]
    h = _modulated_norm(x, sc_ref[...], sh_ref[...]).astype(BF16)
    y_s5 = jax.nn.gelu(ys5_ref[...])
    y_s5 = y_s5 * jax.nn.sigmoid(jnp.dot(y_s5.astype(BF16), wglu_ref[...], preferred_element_type=F32))
    acc = jnp.zeros(x.shape, F32)
    for n, y in enumerate((y_s5, yhg_ref[...], yret_ref[...], ym2_ref[...])):
        gate = jax.nn.sigmoid(jnp.dot(h, wg_ref[:, n * d:(n + 1) * d], preferred_element_type=F32)
                              + bg_ref[:, n * d:(n + 1) * d])
        acc = acc + gate * jnp.dot(y.astype(BF16), wbr_ref[n], preferred_element_type=F32)
    o_ref[...] = x + gm_ref[...] * jnp.dot(acc.astype(BF16), wout_ref[...], preferred_element_type=F32)


def _merge(x2, mod3, ys5, yhg, yret, ym2, w_glu, w_branch, w_gate, b_gate, w_out, seq):
    t, d = x2.shape
    tm = TM_MERGE
    tpb = seq // tm
    const = lambda shape: pl.BlockSpec(shape, lambda i: (0,) * len(shape))
    modspec = lambda k: pl.BlockSpec((None, 1, d), lambda i: ((i // tpb) * 6 + k, 0, 0))
    yspec = pl.BlockSpec((tm, BRANCH_W), lambda i: (i, 0))
    return pl.pallas_call(
        _merge_kernel,
        grid=(t // tm,),
        in_specs=[pl.BlockSpec((tm, d), lambda i: (i, 0)), modspec(1), modspec(0), modspec(2),
                  yspec, yspec, yspec, yspec,
                  const((BRANCH_W, BRANCH_W)), const((4, BRANCH_W, d)), const((d, 4 * d)), const((1, 4 * d)),
                  const((d, d))],
        out_specs=pl.BlockSpec((tm, d), lambda i: (i, 0)),
        out_shape=jax.ShapeDtypeStruct((t, d), F32),
        compiler_params=_cparams(("parallel",)),
        name="merge",
    )(x2, mod3, mod3, mod3, ys5, yhg, yret, ym2, w_glu, w_branch, w_gate, b_gate, w_out)


def _router_kernel(x_ref, sc_ref, sh_ref, wr_ref, br_ref, tri_ref, h_ref, ids_ref, wts_ref, cnt_ref, carry):
    i = pl.program_id(0)

    @pl.when(i == 0)
    def _():
        carry[...] = jnp.zeros_like(carry)

    h = _modulated_norm(x_ref[...], sc_ref[...], sh_ref[...])
    tm, d = h.shape
    for s in range(d // LANE):
        h_ref[:, s, :] = h[:, s * LANE:(s + 1) * LANE]
    logits = _dot_nt(wr_ref[...], h, precision=HIGHEST) + br_ref[:, 0:1]
    gl = [logits[g:g + 1, :] for g in range(MOE_GROUPS)]
    gmax = gl[0]
    gsel = jnp.zeros((1, tm), jnp.int32)
    for g in range(1, MOE_GROUPS):
        better = gl[g] > gmax
        gsel = jnp.where(better, g, gsel)
        gmax = jnp.where(better, gl[g], gmax)
    gden = gl[0] * 0.0
    for g in range(MOE_GROUPS):
        gden = gden + jnp.exp(gl[g] - gmax)
    g_w = 1.0 / gden
    el = []
    for e in range(MOE_EPG):
        v = logits[MOE_GROUPS + e:MOE_GROUPS + e + 1, :]
        for g in range(1, MOE_GROUPS):
            row = MOE_GROUPS + g * MOE_EPG + e
            v = jnp.where(gsel == g, logits[row:row + 1, :], v)
        el.append(v)
    v1 = el[0]
    i1 = jnp.zeros((1, tm), jnp.int32)
    for e in range(1, MOE_EPG):
        better = el[e] > v1
        i1 = jnp.where(better, e, i1)
        v1 = jnp.where(better, el[e], v1)
    v2 = jnp.full((1, tm), -jnp.inf, F32)
    i2 = jnp.zeros((1, tm), jnp.int32)
    for e in range(MOE_EPG):
        better = (el[e] > v2) & (i1 != e)
        i2 = jnp.where(better, e, i2)
        v2 = jnp.where(better, el[e], v2)
    ex = jnp.exp(v2 - v1)
    p1 = 1.0 / (1.0 + ex)
    e1 = gsel * MOE_EPG + i1
    e2 = gsel * MOE_EPG + i2
    erow = lax.broadcasted_iota(jnp.int32, (MOE_EXPERTS, tm), 0)
    oh1 = (erow == e1).astype(F32)
    oh2 = (erow == e2).astype(F32)
    both = oh1 + oh2
    prefix = jnp.dot(both.astype(BF16), tri_ref[...], preferred_element_type=F32) + carry[:, 0:1]
    rank1 = jnp.sum(oh1 * prefix, axis=0, keepdims=True).astype(jnp.int32)
    rank2 = jnp.sum(oh2 * prefix, axis=0, keepdims=True).astype(jnp.int32)
    carry[...] = carry[...] + jnp.sum(both, axis=1, keepdims=True)
    zi = jnp.zeros((1, tm), jnp.int32)
    ids_ref[...] = jnp.concatenate([e1, e2, rank1, rank2, zi, zi, zi, zi], axis=0)
    zf = jnp.zeros((1, tm), F32)
    wts_ref[...] = jnp.concatenate([p1 * g_w, ex * p1 * g_w, zf, zf, zf, zf, zf, zf], axis=0)
    cnt_ref[...] = carry[...]


def _router(x2, mod3, w_route, b_route, tri_excl, seq):
    t, d = x2.shape
    tm = TM_PROJ
    tpb = seq // tm
    nr = w_route.shape[0]
    const = lambda shape: pl.BlockSpec(shape, lambda i: (0,) * len(shape))
    modspec = lambda k: pl.BlockSpec((None, 1, d), lambda i: ((i // tpb) * 6 + k, 0, 0))
    return pl.pallas_call(
        _router_kernel,
        grid=(t // tm,),
        in_specs=[pl.BlockSpec((tm, d), lambda i: (i, 0)), modspec(4), modspec(3),
                  const((nr, d)), const((nr, LANE)), const((tm, tm))],
        out_specs=[pl.BlockSpec((tm, d // LANE, LANE), lambda i: (i, 0, 0)),
                   pl.BlockSpec((8, tm), lambda i: (0, i)),
                   pl.BlockSpec((8, tm), lambda i: (0, i)),
                   const((MOE_EXPERTS, LANE))],
        out_shape=[jax.ShapeDtypeStruct((t, d // LANE, LANE), F32),
                   jax.ShapeDtypeStruct((8, t), jnp.int32),
                   jax.ShapeDtypeStruct((8, t), F32),
                   jax.ShapeDtypeStruct((MOE_EXPERTS, LANE), F32)],
        scratch_shapes=[pltpu.VMEM((MOE_EXPERTS, LANE), F32)],
        compiler_params=_cparams(("arbitrary",)),
        name="moe_router",
    )(x2, mod3, mod3, w_route, b_route, tri_excl)


def _dispatch_kernel(s1_ref, s2_ref, h_hbm, zero_hbm, xs_hbm, sem, *, tm):
    del zero_hbm
    base = pl.program_id(0) * tm

    def row_copy(tok, slot):
        return pltpu.make_async_copy(h_hbm.at[tok], xs_hbm.at[slot], sem)

    def issue(r, c):
        row_copy(base + r, s1_ref[0, r]).start()
        row_copy(base + r, s2_ref[0, r]).start()
        return c

    lax.fori_loop(0, tm, issue, 0)

    def drain(r, c):
        row_copy(base + r, s1_ref[0, r]).wait()
        row_copy(base + r, s2_ref[0, r]).wait()
        return c

    lax.fori_loop(0, tm, drain, 0)


def _dispatch(slot1, slot2, h3, zeros3):
    t = h3.shape[0]
    tm = TM_DISP
    smem = lambda: pl.BlockSpec((None, 1, tm), lambda i: (i, 0, 0), memory_space=pltpu.SMEM)
    return pl.pallas_call(
        functools.partial(_dispatch_kernel, tm=tm),
        grid=(t // tm,),
        in_specs=[smem(), smem(), pl.BlockSpec(memory_space=pl.ANY), pl.BlockSpec(memory_space=pl.ANY)],
        out_specs=pl.BlockSpec(memory_space=pl.ANY),
        out_shape=jax.ShapeDtypeStruct(zeros3.shape, zeros3.dtype),
        scratch_shapes=[pltpu.SemaphoreType.DMA(())],
        input_output_aliases={3: 0},
        compiler_params=_cparams(("arbitrary",)),
        name="moe_dispatch",
    )(slot1.reshape(t // tm, 1, tm), slot2.reshape(t // tm, 1, tm), h3, zeros3)


def _expert_kernel(texp_ref, xs_ref, cw_ref, w1_ref, w3_ref, w2_ref, ys_ref, x_scr):
    del texp_ref
    nsl = xs_ref.shape[1]
    for s in range(nsl):
        x_scr[:, s * LANE:(s + 1) * LANE] = xs_ref[:, s, :].astype(BF16)
    x = x_scr[...]
    a = jnp.dot(x, w1_ref[...], preferred_element_type=F32)
    b = jnp.dot(x, w3_ref[...], preferred_element_type=F32)
    act = _silu(a) * b * cw_ref[:, 0:1]
    y = jnp.dot(act.astype(BF16), w2_ref[...], preferred_element_type=F32)
    for s in range(nsl):
        ys_ref[:, s, :] = y[:, s * LANE:(s + 1) * LANE]


def _experts(tile_expert, xs3, slot_w, w1, w3, w2):
    ns, nsl, _ = xs3.shape
    d = nsl * LANE
    ff = w1.shape[2]
    grid_spec = pltpu.PrefetchScalarGridSpec(
        num_scalar_prefetch=1,
        grid=(ns // TM_X,),
        in_specs=[pl.BlockSpec((TM_X, nsl, LANE), lambda i, te: (i, 0, 0)),
                  pl.BlockSpec((TM_X, 1), lambda i, te: (i, 0)),
                  pl.BlockSpec((None, d, ff), lambda i, te: (te[i], 0, 0)),
                  pl.BlockSpec((None, d, ff), lambda i, te: (te[i], 0, 0)),
                  pl.BlockSpec((None, ff, d), lambda i, te: (te[i], 0, 0))],
        out_specs=pl.BlockSpec((TM_X, nsl, LANE), lambda i, te: (i, 0, 0)),
        scratch_shapes=[pltpu.VMEM((TM_X, d), BF16)],
    )
    return pl.pallas_call(
        _expert_kernel,
        grid_spec=grid_spec,
        out_shape=jax.ShapeDtypeStruct((ns, nsl, LANE), F32),
        compiler_params=_cparams(("parallel",)),
        name="moe_experts",
    )(tile_expert, xs3, slot_w, w1, w3, w2)


def _combine_kernel(s1_ref, s2_ref, x_ref, gate_ref, fw_ref, ys_hbm, o_ref, buf, sem, *, tm, final):
    def row_copy(which, r, slot):
        return pltpu.make_async_copy(ys_hbm.at[slot], buf.at[which, r], sem)

    def issue(r, c):
        row_copy(0, r, s1_ref[0, r]).start()
        row_copy(1, r, s2_ref[0, r]).start()
        return c

    lax.fori_loop(0, tm, issue, 0)

    def drain(r, c):
        row_copy(0, r, s1_ref[0, r]).wait()
        row_copy(1, r, s2_ref[0, r]).wait()
        return c

    lax.fori_loop(0, tm, drain, 0)
    nsl = buf.shape[2]
    parts = [buf[0, :, s, :] + buf[1, :, s, :] for s in range(nsl)]
    moe = jnp.concatenate(parts, axis=-1)
    x = x_ref[...] + gate_ref[...] * moe
    if final:
        x = x * lax.rsqrt(jnp.mean(x * x, axis=-1, keepdims=True) + EPS) * fw_ref[...]
    o_ref[...] = x


def _combine(slot1, slot2, x2, mod3, final_w_row, ys3, seq, final):
    t, d = x2.shape
    tm = TM_COMB
    tpb = seq // tm
    nsl = ys3.shape[1]
    smem = lambda: pl.BlockSpec((None, 1, tm), lambda i: (i, 0, 0), memory_space=pltpu.SMEM)
    return pl.pallas_call(
        functools.partial(_combine_kernel, tm=tm, final=final),
        grid=(t // tm,),
        in_specs=[smem(), smem(), pl.BlockSpec((tm, d), lambda i: (i, 0)),
                  pl.BlockSpec((None, 1, d), lambda i: ((i // tpb) * 6 + 5, 0, 0)),
                  pl.BlockSpec((1, d), lambda i: (0, 0)),
                  pl.BlockSpec(memory_space=pl.ANY)],
        out_specs=pl.BlockSpec((tm, d), lambda i: (i, 0)),
        out_shape=jax.ShapeDtypeStruct((t, d), F32),
        scratch_shapes=[pltpu.VMEM((2, tm, nsl, LANE), F32), pltpu.SemaphoreType.DMA(())],
        compiler_params=_cparams(("arbitrary",)),
        name="moe_combine",
    )(slot1.reshape(t // tm, 1, tm), slot2.reshape(t // tm, 1, tm), x2, mod3, final_w_row, ys3)


def _moe(x2, mod3, final_w_row, w_route, b_route, tri_excl, w1, w3, w2, seq, final):
    t, d = x2.shape
    h3, ids, wts, counts = _router(x2, mod3, w_route, b_route, tri_excl, seq)
    cnt = counts[:, 0].astype(jnp.int32)
    padded = (cnt + TM_X - 1) // TM_X * TM_X
    ends = jnp.cumsum(padded)
    offs = ends - padded
    slot1 = offs[ids[0]] + ids[2]
    slot2 = offs[ids[1]] + ids[3]
    ns = 2 * t + MOE_EXPERTS * TM_X
    tile_start = jnp.arange(ns // TM_X, dtype=jnp.int32) * TM_X
    tile_expert = jnp.minimum(jnp.sum(tile_start[:, None] >= ends[None, :], axis=1), MOE_EXPERTS - 1).astype(jnp.int32)
    slot_w = jnp.zeros((ns, 1), F32).at[slot1, 0].set(wts[0]).at[slot2, 0].set(wts[1])
    xs3 = _dispatch(slot1, slot2, h3, jnp.zeros((ns, d // LANE, LANE), F32))
    ys3 = _experts(tile_expert, xs3, slot_w, w1, w3, w2)
    return _combine(slot1, slot2, x2, mod3, final_w_row, ys3, seq, final)


def kernel(x, c, positions, ada_w, ada_b, w_in, s5_lam_re, s5_lam_im, s5_b_re, s5_b_im, s5_c_re, s5_c_im, s5_d, s5_log_dt, s5_w_glu, hg_lb_logits, hg_norm_w, m2_conv_w, m2_conv_b, m2_dt_bias, m2_a_log, m2_d, m2_norm_w, w_branch, w_gate, b_gate, w_out, moe_w_group, moe_b_group, moe_w_expert, moe_b_expert, moe_w1, moe_w3, moe_w2, final_norm_w):
    bsz, seq, d = x.shape
    t = bsz * seq
    depth = ada_w.shape[0]
    assert seq % TM_PROJ == 0 and seq % C_RET == 0 and seq % C_SSD == 0 and seq % C_HG == 0
    x2 = x.reshape(t, d).astype(F32)

    c_pad = jnp.zeros((8, d), F32).at[:bsz].set(c.astype(F32))
    mod_all = _ada_mod(c_pad, ada_w.astype(F32), ada_b.astype(F32))

    half = RET_DK // 2
    inv_freq = ROPE_BASE ** (-jnp.arange(half, dtype=F32) / half)
    invf_row = jnp.tile(inv_freq, 2 * RET_HEADS)[None, :]
    sgn_row = jnp.asarray(np.tile(np.concatenate([-np.ones(half), np.ones(half)]), RET_HEADS)[None, :], F32)
    cos_t, sin_t = _rope_tables(positions.reshape(t, 1).astype(jnp.int32), invf_row, sgn_row)
    cos3 = cos_t.reshape(bsz, seq, -1)
    sin3 = sin_t.reshape(bsz, seq, -1)

    lb_cum = jnp.cumsum(jax.nn.softmax(hg_lb_logits.astype(F32), axis=0), axis=0)
    hg_lb = lb_cum - lb_cum[:1]
    tri_ssd = jnp.asarray(np.tril(np.ones((C_SSD, C_SSD), np.float32)))
    tri_excl = jnp.asarray(np.triu(np.ones((TM_PROJ, TM_PROJ), np.float32), 1), BF16)
    final_w_row = final_norm_w.astype(F32)[None, :]

    for layer in range(depth):
        mod3 = mod_all[layer, :bsz].reshape(bsz * 6, 1, d)
        w_pad = jnp.zeros((d, IN_W_PAD), BF16).at[:, :IN_W].set(w_in[layer].astype(BF16))
        p = _in_proj(x2, mod3, w_pad, seq)
        p3 = p.reshape(bsz, seq, IN_W_PAD)

        ops = _s5_operators(s5_lam_re[layer], s5_lam_im[layer], s5_b_re[layer], s5_b_im[layer],
                            s5_c_re[layer], s5_c_im[layer], s5_d[layer], s5_log_dt[layer])
        y_s5 = _s5(p, bsz, seq, ops)

        lb = hg_lb[layer][None, :]
        y_hg = _hgrn2(p3, jnp.log(lb), jnp.log1p(-lb), hg_norm_w[layer].astype(F32)[None, :]).reshape(t, BRANCH_W)

        y_ret = _retention(p3, cos3, sin3).reshape(t, BRANCH_W)

        pad8 = lambda v: jnp.zeros((1, LANE), F32).at[0, :M2_HEADS].set(v.astype(F32))
        y_m2 = _ssd(p3, tri_ssd, m2_conv_w[layer].astype(F32), m2_conv_b[layer].astype(F32)[None, :],
                    pad8(m2_dt_bias[layer]), pad8(m2_a_log[layer]),
                    jnp.repeat(m2_d[layer].astype(F32), M2_HEADDIM)[None, :],
                    m2_norm_w[layer].astype(F32)[None, :]).reshape(t, BRANCH_W)

        x2 = _merge(x2, mod3, y_s5, y_hg, y_ret, y_m2, s5_w_glu[layer].astype(BF16),
                    w_branch[layer].astype(BF16), w_gate[layer].astype(BF16), b_gate[layer].astype(F32)[None, :],
                    w_out[layer].astype(BF16), seq)

        nr = 40
        w_route = jnp.zeros((nr, d), F32).at[:MOE_GROUPS].set(moe_w_group[layer].astype(F32).T)
        w_route = w_route.at[MOE_GROUPS:MOE_GROUPS + MOE_EXPERTS].set(moe_w_expert[layer].astype(F32).T)
        b_route = jnp.zeros((nr, LANE), F32).at[:MOE_GROUPS, 0].set(moe_b_group[layer].astype(F32))
        b_route = b_route.at[MOE_GROUPS:MOE_GROUPS + MOE_EXPERTS, 0].set(moe_b_expert[layer].astype(F32))
        x2 = _moe(x2, mod3, final_w_row, w_route, b_route, tri_excl,
                  moe_w1[layer].reshape(MOE_EXPERTS, d, MOE_FF).astype(BF16),
                  moe_w3[layer].reshape(MOE_EXPERTS, d, MOE_FF).astype(BF16),
                  moe_w2[layer].reshape(MOE_EXPERTS, MOE_FF, d).astype(BF16),
                  seq, final=(layer == depth - 1))
    return x2.reshape(bsz, seq, d)
```

```python
import functools
import math

import numpy as np
import jax
import jax.numpy as jnp
from jax import lax
from jax.experimental import pallas as pl
from jax.experimental.pallas import tpu as pltpu

F32 = jnp.float32
BF16 = jnp.bfloat16
HIGHEST = lax.Precision.HIGHEST

D_MODEL = 1024
DEPTH = 2
BRANCH_W = 512
EPS = 1e-6
S5_GROUPS = 32
S5_CH = 16
S5_STATE = 64
S5_MAX_REAL = -1e-4
S5_BLOCK = 16
HG_HEADS = 4
HG_DK = 128
RET_HEADS = 4
RET_DK = 64
RET_DV = 128
ROPE_BASE = 10000.0
M2_HEADS = 8
M2_HEADDIM = 64
M2_GROUPS = 2
M2_STATE = 128
M2_CONV = 4
MOE_GROUPS = 4
MOE_EPG = 8
MOE_EXPERTS = MOE_GROUPS * MOE_EPG
MOE_FF = 256

COL_S5, COL_HQ, COL_HF, COL_HI, COL_HG = 0, 512, 1024, 1536, 2048
COL_RQ, COL_RK, COL_RV, COL_RG = 2560, 2816, 3072, 3584
COL_MZ, COL_MXS, COL_MBC, COL_MDT = 4096, 4608, 5120, 5632
IN_W = 5640
IN_W_PAD = 5760

LANE = 128
VMEM_LIMIT = 56 * 1024 * 1024

TM_PROJ = 1024
TN_PROJ = 1152
TM_MERGE = 512
C_RET = 256
C_SSD = 256
C_HG = 128
TM_X = 256
TM_COMB = 512
TM_DISP = 512


def _cparams(sem):
    return pltpu.CompilerParams(dimension_semantics=sem, vmem_limit_bytes=VMEM_LIMIT)


def _silu(v):
    return v * jax.nn.sigmoid(v)


def _dot_nt(a, b, **kw):
    return lax.dot_general(a, b, (((1,), (1,)), ((), ())), preferred_element_type=F32, **kw)


def _dot_tn(a, b, **kw):
    return lax.dot_general(a, b, (((0,), (0,)), ((), ())), preferred_element_type=F32, **kw)


def _ada_kernel(c_ref, w_ref, b_ref, o_ref):
    cond = _silu(c_ref[...])
    o_ref[...] = jnp.dot(cond, w_ref[...], preferred_element_type=F32, precision=HIGHEST) + b_ref[...]


def _ada_mod(c_pad, ada_w, ada_b):
    depth, d, n = ada_w.shape
    tn = 1536
    return pl.pallas_call(
        _ada_kernel,
        grid=(depth, n // tn),
        in_specs=[pl.BlockSpec((8, d), lambda l, j: (0, 0)),
                  pl.BlockSpec((None, d, tn), lambda l, j: (l, 0, j)),
                  pl.BlockSpec((None, 1, tn), lambda l, j: (l, 0, j))],
        out_specs=pl.BlockSpec((None, 8, tn), lambda l, j: (l, 0, j)),
        out_shape=jax.ShapeDtypeStruct((depth, 8, n), F32),
        compiler_params=_cparams(("parallel", "parallel")),
        name="ada_mod",
    )(c_pad, ada_w, ada_b.reshape(depth, 1, n))


def _modulated_norm(x, scale, shift):
    ms = jnp.mean(x * x, axis=-1, keepdims=True)
    return x * lax.rsqrt(ms + EPS) * (1.0 + scale) + shift


def _inproj_kernel(x_ref, sc_ref, sh_ref, w_ref, o_ref, h_scr):
    @pl.when(pl.program_id(1) == 0)
    def _():
        h_scr[...] = _modulated_norm(x_ref[...], sc_ref[...], sh_ref[...]).astype(BF16)

    o_ref[...] = jnp.dot(h_scr[...], w_ref[...], preferred_element_type=F32)


def _in_proj(x2, mod3, w_pad, seq):
    t, d = x2.shape
    tpb = seq // TM_PROJ
    return pl.pallas_call(
        _inproj_kernel,
        grid=(t // TM_PROJ, IN_W_PAD // TN_PROJ),
        in_specs=[pl.BlockSpec((TM_PROJ, d), lambda i, j: (i, 0)),
                  pl.BlockSpec((None, 1, d), lambda i, j: ((i // tpb) * 6 + 1, 0, 0)),
                  pl.BlockSpec((None, 1, d), lambda i, j: ((i // tpb) * 6 + 0, 0, 0)),
                  pl.BlockSpec((d, TN_PROJ), lambda i, j: (0, j))],
        out_specs=pl.BlockSpec((TM_PROJ, TN_PROJ), lambda i, j: (i, j)),
        out_shape=jax.ShapeDtypeStruct((t, IN_W_PAD), F32),
        scratch_shapes=[pltpu.VMEM((TM_PROJ, d), BF16)],
        compiler_params=_cparams(("parallel", "arbitrary")),
        name="in_proj",
    )(x2, mod3, mod3, w_pad)


def _rope_kernel(pos_ref, invf_ref, sgn_ref, cos_ref, sin_ref):
    ang = pos_ref[...].astype(F32) * invf_ref[...]
    cos_ref[...] = jnp.cos(ang)
    sin_ref[...] = jnp.sin(ang) * sgn_ref[...]


def _rope_tables(pos_col, invf_row, sgn_row):
    t = pos_col.shape[0]
    w = invf_row.shape[1]
    tm = 1024
    return pl.pallas_call(
        _rope_kernel,
        grid=(t // tm,),
        in_specs=[pl.BlockSpec((tm, 1), lambda i: (i, 0)),
                  pl.BlockSpec((1, w), lambda i: (0, 0)),
                  pl.BlockSpec((1, w), lambda i: (0, 0))],
        out_specs=[pl.BlockSpec((tm, w), lambda i: (i, 0))] * 2,
        out_shape=[jax.ShapeDtypeStruct((t, w), F32)] * 2,
        compiler_params=_cparams(("parallel",)),
        name="rope_tables",
    )(pos_col, invf_row, sgn_row)


def _ret_kernel(q_ref, k_ref, v_ref, g_ref, cos_ref, sin_ref, o_ref, st_ref, *, chunk):
    @pl.when(pl.program_id(1) == 0)
    def _():
        st_ref[...] = jnp.zeros_like(st_ref)

    cosf = cos_ref[...]
    sinf = sin_ref[...]
    width = RET_HEADS * RET_DK
    lane = lax.broadcasted_iota(jnp.int32, (chunk, width), 1)
    first_half = (lane % RET_DK) < (RET_DK // 2)

    def rope(t):
        partner = jnp.where(first_half, pltpu.roll(t, width - RET_DK // 2, 1), pltpu.roll(t, RET_DK // 2, 1))
        return t * cosf + partner * sinf

    q = rope(q_ref[...])
    k = rope(k_ref[...]) * (RET_DK ** -0.5)
    v = v_ref[...]
    g = g_ref[...]
    ti = lax.broadcasted_iota(jnp.int32, (chunk, chunk), 0)
    si = lax.broadcasted_iota(jnp.int32, (chunk, chunk), 1)
    lag = (ti - si).astype(F32)
    tcol = lax.broadcasted_iota(jnp.int32, (chunk, 1), 0).astype(F32)
    for h in range(RET_HEADS):
        log_gamma = math.log1p(-(2.0 ** (-5.0 - h)))
        qh = q[:, h * RET_DK:(h + 1) * RET_DK]
        kh = k[:, h * RET_DK:(h + 1) * RET_DK]
        vh = v[:, h * RET_DV:(h + 1) * RET_DV].astype(BF16)
        decay = jnp.where(ti >= si, jnp.exp(jnp.minimum(lag * log_gamma, 0.0)), 0.0)
        scores = _dot_nt(qh.astype(BF16), kh.astype(BF16)) * decay
        state = st_ref[h]
        q_in = qh * jnp.exp(log_gamma * (tcol + 1.0))
        o = (jnp.dot(scores.astype(BF16), vh, preferred_element_type=F32)
             + jnp.dot(q_in.astype(BF16), state.astype(BF16), preferred_element_type=F32))
        k_out = kh * jnp.exp(log_gamma * (chunk - 1.0 - tcol))
        st_ref[h] = math.exp(log_gamma * chunk) * state + _dot_tn(k_out.astype(BF16), vh)
        o = o * lax.rsqrt(jnp.mean(o * o, axis=-1, keepdims=True) + EPS)
        gh = g[:, h * RET_DV:(h + 1) * RET_DV]
        o_ref[:, h * RET_DV:(h + 1) * RET_DV] = o * _silu(gh)


def _retention(p3, cos3, sin3):
    b, seq, _ = p3.shape
    c = C_RET
    qk_w = RET_HEADS * RET_DK
    return pl.pallas_call(
        functools.partial(_ret_kernel, chunk=c),
        grid=(b, seq // c),
        in_specs=[pl.BlockSpec((None, c, qk_w), lambda i, j: (i, j, COL_RQ // qk_w)),
                  pl.BlockSpec((None, c, qk_w), lambda i, j: (i, j, COL_RK // qk_w)),
                  pl.BlockSpec((None, c, BRANCH_W), lambda i, j: (i, j, COL_RV // BRANCH_W)),
                  pl.BlockSpec((None, c, BRANCH_W), lambda i, j: (i, j, COL_RG // BRANCH_W)),
                  pl.BlockSpec((None, c, qk_w), lambda i, j: (i, j, 0)),
                  pl.BlockSpec((None, c, qk_w), lambda i, j: (i, j, 0))],
        out_specs=pl.BlockSpec((None, c, BRANCH_W), lambda i, j: (i, j, 0)),
        out_shape=jax.ShapeDtypeStruct((b, seq, BRANCH_W), F32),
        scratch_shapes=[pltpu.VMEM((RET_HEADS, RET_DK, RET_DV), F32)],
        compiler_params=_cparams(("parallel", "arbitrary")),
        name="retention",
    )(p3, p3, p3, p3, cos3, sin3)


def _ssd_kernel(z_ref, xs_ref, bc_ref, dt_ref, tri_ref, cw_ref, cb_ref, dtb_ref, alog_ref, dsk_ref, nw_ref,
                o_ref, xe_scr, st_ref, *, chunk):
    j = pl.program_id(1)
    width = 2 * BRANCH_W

    @pl.when(j == 0)
    def _():
        st_ref[...] = jnp.zeros_like(st_ref)
        xe_scr[0:8, :] = jnp.zeros((8, width), F32)

    @pl.when(j > 0)
    def _():
        xe_scr[0:8, :] = xe_scr[chunk:chunk + 8, :]

    xe_scr[8:, 0:BRANCH_W] = xs_ref[...]
    xe_scr[8:, BRANCH_W:] = bc_ref[...]
    conv = cb_ref[...] + cw_ref[M2_CONV - 1:M2_CONV, :] * xe_scr[8:, :]
    for tap in range(M2_CONV - 1):
        conv = conv + cw_ref[tap:tap + 1, :] * xe_scr[pl.ds(8 - (M2_CONV - 1) + tap, chunk), :]
    conv = _silu(conv)
    xs = conv[:, :BRANCH_W]
    bm = conv[:, BRANCH_W:BRANCH_W + M2_GROUPS * M2_STATE]
    cm = conv[:, BRANCH_W + M2_GROUPS * M2_STATE:]

    dt = jax.nn.softplus(dt_ref[...] + dtb_ref[...])
    da = dt * (-jnp.exp(alog_ref[...]))
    a_cs = jnp.dot(tri_ref[...], da, preferred_element_type=F32, precision=HIGHEST)
    a_cs_t = a_cs.T
    ti = lax.broadcasted_iota(jnp.int32, (chunk, chunk), 0)
    si = lax.broadcasted_iota(jnp.int32, (chunk, chunk), 1)
    causal = ti >= si
    hpg = M2_HEADS // M2_GROUPS
    ys = []
    for grp in range(M2_GROUPS):
        bm_g = bm[:, grp * M2_STATE:(grp + 1) * M2_STATE]
        cm_g = cm[:, grp * M2_STATE:(grp + 1) * M2_STATE]
        cb = _dot_nt(cm_g.astype(BF16), bm_g.astype(BF16))
        for hh in range(hpg):
            h = grp * hpg + hh
            col = a_cs[:, h:h + 1]
            row = a_cs_t[h:h + 1, :]
            lmat = jnp.where(causal, jnp.exp(jnp.minimum(col - row, 0.0)), 0.0)
            xd = xs[:, h * M2_HEADDIM:(h + 1) * M2_HEADDIM] * dt[:, h:h + 1]
            state = st_ref[h]
            y = (jnp.dot((cb * lmat).astype(BF16), xd.astype(BF16), preferred_element_type=F32)
                 + jnp.dot((cm_g * jnp.exp(col)).astype(BF16), state.astype(BF16), preferred_element_type=F32))
            a_last = a_cs[chunk - 1:chunk, h:h + 1]
            to_end = jnp.exp(a_last - col)
            st_ref[h] = jnp.exp(a_last) * state + _dot_tn(bm_g.astype(BF16), (xd * to_end).astype(BF16))
            ys.append(y)
    y = jnp.concatenate(ys, axis=-1) + dsk_ref[...] * xs
    y = y * _silu(z_ref[...])
    o_ref[...] = y * lax.rsqrt(jnp.mean(y * y, axis=-1, keepdims=True) + EPS) * nw_ref[...]


def _ssd(p3, tri, conv_w, conv_b, dt_bias_row, a_log_row, d_skip_row, norm_w_row):
    b, seq, _ = p3.shape
    c = C_SSD
    const = lambda shape: pl.BlockSpec(shape, lambda i, j: (0,) * len(shape))
    return pl.pallas_call(
        functools.partial(_ssd_kernel, chunk=c),
        grid=(b, seq // c),
        in_specs=[pl.BlockSpec((None, c, BRANCH_W), lambda i, j: (i, j, COL_MZ // BRANCH_W)),
                  pl.BlockSpec((None, c, BRANCH_W), lambda i, j: (i, j, COL_MXS // BRANCH_W)),
                  pl.BlockSpec((None, c, BRANCH_W), lambda i, j: (i, j, COL_MBC // BRANCH_W)),
                  pl.BlockSpec((None, c, LANE), lambda i, j: (i, j, COL_MDT // LANE)),
                  const((c, c)), const((M2_CONV, 2 * BRANCH_W)), const((1, 2 * BRANCH_W)),
                  const((1, LANE)), const((1, LANE)), const((1, BRANCH_W)), const((1, BRANCH_W))],
        out_specs=pl.BlockSpec((None, c, BRANCH_W), lambda i, j: (i, j, 0)),
        out_shape=jax.ShapeDtypeStruct((b, seq, BRANCH_W), F32),
        scratch_shapes=[pltpu.VMEM((c + 8, 2 * BRANCH_W), F32),
                        pltpu.VMEM((M2_HEADS, M2_STATE, M2_HEADDIM), F32)],
        compiler_params=_cparams(("parallel", "arbitrary")),
        name="ssd",
    )(p3, p3, p3, p3, tri, conv_w, conv_b, dt_bias_row, a_log_row, d_skip_row, norm_w_row)


def _hg_tables(chunk):
    n_lev = int(math.log2(chunk))
    r = np.arange(chunk)[:, None]
    jj = np.arange(chunk)[None, :]
    mats = [(jj <= r), (jj > r)]
    for lev in range(n_lev):
        m = 1 << lev
        mid = (r >> (lev + 1) << (lev + 1)) + m - 1
        second = ((r >> lev) & 1) == 1
        mats.append(np.where(second, (jj > mid) & (jj <= r), (jj > r) & (jj <= mid)))
    summat = np.concatenate(mats, axis=0).astype(np.float32)
    x = r ^ jj
    levmap = np.where(r > jj, np.floor(np.log2(x + 0.5)), np.where(r == jj, -1, -2)).astype(np.int32)
    return summat, levmap, n_lev


def _hg_kernel(q_ref, f_ref, i_ref, g_ref, sum_ref, lev_ref, llb_ref, l1m_ref, nw_ref, o_ref, st_ref,
               *, chunk, n_lev):
    @pl.when(pl.program_id(1) == 0)
    def _():
        st_ref[...] = jnp.zeros_like(st_ref)

    f = f_ref[...]
    log_sig = jnp.minimum(f, 0.0) - jnp.log1p(jnp.exp(-jnp.abs(f)))
    a = llb_ref[...]
    bb = l1m_ref[...] + log_sig
    log_f = jnp.maximum(a, bb) + jnp.log1p(jnp.exp(-jnp.abs(a - bb)))
    k_all = jnp.exp(l1m_ref[...]) * jax.nn.sigmoid(-f)
    q_all = _silu(q_ref[...])
    hi = log_f.astype(BF16)
    r1 = log_f - hi.astype(F32)
    mid = r1.astype(BF16)
    lo = (r1 - mid.astype(F32)).astype(BF16)
    summat = sum_ref[...]
    sums = (jnp.dot(summat, hi, preferred_element_type=F32)
            + jnp.dot(summat, mid, preferred_element_type=F32)
            + jnp.dot(summat, lo, preferred_element_type=F32))
    levmap = lev_ref[...]
    v_all = i_ref[...]
    g_all = g_ref[...]
    for h in range(HG_HEADS):
        sl = slice(h * HG_DK, (h + 1) * HG_DK)
        qh = q_all[:, sl]
        kh = k_all[:, sl]
        vh = v_all[:, sl].astype(BF16)
        b_h = sums[0:chunk, sl]
        to_end = sums[chunk:2 * chunk, sl]
        amat = jnp.where(levmap == -1, _dot_nt(qh.astype(BF16), kh.astype(BF16)), 0.0)
        for lev in range(n_lev):
            e = jnp.exp(sums[(2 + lev) * chunk:(3 + lev) * chunk, sl])
            a_l = _dot_nt((qh * e).astype(BF16), (kh * e).astype(BF16))
            amat = jnp.where(levmap == lev, a_l, amat)
        state_t = st_ref[h]
        o = (jnp.dot(amat.astype(BF16), vh, preferred_element_type=F32)
             + _dot_nt((qh * jnp.exp(b_h)).astype(BF16), state_t.astype(BF16)))
        k_end = kh * jnp.exp(to_end)
        st_ref[h] = jnp.exp(b_h[chunk - 1:chunk, :]) * state_t + _dot_tn(vh, k_end.astype(BF16))
        o = o * lax.rsqrt(jnp.mean(o * o, axis=-1, keepdims=True) + EPS) * nw_ref[...]
        o_ref[:, sl] = o * _silu(g_all[:, sl])


def _hgrn2(p3, log_lb_row, log1m_lb_row, norm_w_row):
    b, seq, _ = p3.shape
    c = C_HG
    summat, levmap, n_lev = _hg_tables(c)
    const = lambda shape: pl.BlockSpec(shape, lambda i, j: (0,) * len(shape))
    blk = lambda col: pl.BlockSpec((None, c, BRANCH_W), lambda i, j: (i, j, col // BRANCH_W))
    return pl.pallas_call(
        functools.partial(_hg_kernel, chunk=c, n_lev=n_lev),
        grid=(b, seq // c),
        in_specs=[blk(COL_HQ), blk(COL_HF), blk(COL_HI), blk(COL_HG),
                  const(summat.shape), const((c, c)),
                  const((1, BRANCH_W)), const((1, BRANCH_W)), const((1, HG_DK))],
        out_specs=pl.BlockSpec((None, c, BRANCH_W), lambda i, j: (i, j, 0)),
        out_shape=jax.ShapeDtypeStruct((b, seq, BRANCH_W), F32),
        scratch_shapes=[pltpu.VMEM((HG_HEADS, HG_DK, HG_DK), F32)],
        compiler_params=_cparams(("parallel", "arbitrary")),
        name="hgrn2",
    )(p3, p3, p3, p3, jnp.asarray(summat, BF16), jnp.asarray(levmap), log_lb_row, log1m_lb_row, norm_w_row)


def _expand_block_diag(comp_ref, e_ref, dst_ref, row_div, lane_div):
    gq = LANE // S5_CH
    rows, ncols = dst_ref.shape
    step = 512
    comp = comp_ref[...]
    row_grp = (lax.broadcasted_iota(jnp.int32, (rows, step), 0) // row_div) % gq
    for c0 in range(0, ncols, step):
        lane_grp = ((lax.broadcasted_iota(jnp.int32, (rows, step), 1) + c0) // lane_div) % gq
        full = jnp.dot(comp, e_ref[:, c0:c0 + step], preferred_element_type=F32)
        dst_ref[:, c0:c0 + step] = jnp.where(row_grp == lane_grp, full, 0.0).astype(dst_ref.dtype)


def _s5_kernel(x_ref, tc_ref, bc_ref, cc_ref, esc_ref, eb_ref, lam_ref, o_ref, tq_ref, bq_ref, cq_ref,
               w_scr, s_scr, *, rows):
    @pl.when(pl.program_id(1) == 0)
    def _():
        _expand_block_diag(tc_ref, esc_ref, tq_ref, S5_CH, S5_CH)
        _expand_block_diag(bc_ref, eb_ref, bq_ref, S5_CH, S5_STATE)
        _expand_block_diag(cc_ref, esc_ref, cq_ref, S5_STATE, S5_CH)

    x = x_ref[...]
    half = w_scr.shape[1] // 2
    w_scr[...] = jnp.dot(x, bq_ref[...], preferred_element_type=F32)
    lam_re = lam_ref[0:1, :]
    lam_im = lam_ref[1:2, :]

    def body(j, carry):
        s_re, s_im = carry
        s_scr[pl.ds(j, 1), 0:half] = s_re
        s_scr[pl.ds(j, 1), half:] = s_im
        w_re = w_scr[pl.ds(j, 1), 0:half]
        w_im = w_scr[pl.ds(j, 1), half:]
        return lam_re * s_re - lam_im * s_im + w_re, lam_re * s_im + lam_im * s_re + w_im

    zero = jnp.zeros((1, half), F32)
    lax.fori_loop(0, rows, body, (zero, zero))
    o_ref[...] = (jnp.dot(x, tq_ref[...], preferred_element_type=F32)
                  + jnp.dot(s_scr[...].astype(BF16), cq_ref[...], preferred_element_type=F32))


def _s5_scan(u6, tc, bc, cc, lam16, batch):
    nq, rows_all, kdim = u6.shape
    rows = rows_all // batch
    gq = LANE // S5_CH
    ncol = 2 * gq * S5_STATE
    nb = S5_BLOCK
    e_sc = (np.eye(nb)[:, None, :, None, None] * np.eye(S5_CH)[None, :, None, None, :] * np.ones((1, 1, 1, gq, 1)))
    e_sc = e_sc.reshape(nb * S5_CH, nb * gq * S5_CH)
    e_b = (np.eye(2)[:, None, :, None, None] * np.eye(S5_STATE)[None, :, None, None, :] * np.ones((1, 1, 1, gq, 1)))
    e_b = e_b.reshape(2 * S5_STATE, ncol)
    full = lambda shape: pl.BlockSpec(shape, lambda q, b: (0,) * len(shape))
    per_q = lambda r, c: pl.BlockSpec((None, r, c), lambda q, b: (q, 0, 0))
    return pl.pallas_call(
        functools.partial(_s5_kernel, rows=rows),
        grid=(nq, batch),
        in_specs=[pl.BlockSpec((None, rows, kdim), lambda q, b: (q, b, 0)),
                  per_q(kdim, nb * S5_CH), per_q(kdim, 2 * S5_STATE), per_q(ncol, nb * S5_CH),
                  full(e_sc.shape), full(e_b.shape), per_q(2, ncol // 2)],
        out_specs=pl.BlockSpec((None, rows, kdim), lambda q, b: (q, b, 0)),
        out_shape=jax.ShapeDtypeStruct((nq, rows_all, kdim), F32),
        scratch_shapes=[pltpu.VMEM((kdim, kdim), BF16), pltpu.VMEM((kdim, ncol), BF16), pltpu.VMEM((ncol, kdim), BF16),
                        pltpu.VMEM((rows, ncol), F32), pltpu.VMEM((rows, ncol), F32)],
        compiler_params=_cparams(("parallel", "arbitrary")),
        name="s5_scan",
    )(u6, tc, bc, cc, jnp.asarray(e_sc, BF16), jnp.asarray(e_b, BF16), lam16)


def _s5_operators(lam_re, lam_im, b_re, b_im, c_re, c_im, d_skip, log_dt):
    nb = S5_BLOCK
    gq = LANE // S5_CH
    nq = S5_GROUPS // gq
    lam = lax.complex(jnp.minimum(lam_re.astype(F32), S5_MAX_REAL), lam_im.astype(F32))
    step = jnp.exp(log_dt.astype(F32))[:, None]
    z = lam * step
    lam_bar = jnp.exp(z)
    b_bar = ((lam_bar - 1.0) / lam)[..., None] * lax.complex(b_re.astype(F32), b_im.astype(F32))
    c_mat = lax.complex(c_re.astype(F32), c_im.astype(F32))
    pw = jnp.exp(z[..., None] * jnp.arange(nb + 1, dtype=F32))
    kern = jnp.einsum('gop,gpd,gpi->gdio', c_mat, pw[..., :nb], b_bar, precision=HIGHEST).real
    tt = np.arange(nb)[None, :, None]
    ss = np.arange(nb)[None, None, :]
    shift = jnp.asarray((ss - tt == np.arange(nb)[:, None, None]).astype(np.float32))
    toe = jnp.einsum('dts,gdio->gtiso', shift, kern, precision=HIGHEST)
    skip = (jnp.asarray(np.eye(nb, dtype=np.float32))[None, :, None, :, None]
            * jnp.asarray(np.eye(S5_CH, dtype=np.float32))[None, None, :, None, :]
            * d_skip.astype(F32).reshape(S5_GROUPS, 1, S5_CH, 1, 1))
    tc = (toe + skip).reshape(nq, gq, nb, S5_CH, nb * S5_CH).transpose(0, 2, 1, 3, 4)
    tc = tc.reshape(nq, nb * LANE, nb * S5_CH)
    pw_rev = jnp.exp(z[..., None] * jnp.asarray(np.arange(nb - 1, -1, -1), F32))
    binc = pw_rev[:, :, :, None] * b_bar[:, :, None, :]
    binc = jnp.stack([binc.real, binc.imag], axis=0).reshape(2, nq, gq, S5_STATE, nb, S5_CH)
    bc = binc.transpose(1, 4, 2, 5, 0, 3).reshape(nq, nb * LANE, 2 * S5_STATE)
    cm = c_mat.transpose(0, 2, 1)[:, :, None, :] * pw[..., 1:][:, :, :, None]
    cm = jnp.stack([cm.real, -cm.imag], axis=0).reshape(2, nq, gq * S5_STATE, nb * S5_CH)
    cc = cm.transpose(1, 0, 2, 3).reshape(nq, 2 * gq * S5_STATE, nb * S5_CH)
    lam_n = pw[..., nb].reshape(nq, gq * S5_STATE)
    lam16 = jnp.stack([lam_n.real, lam_n.imag], axis=1)
    return tc.astype(BF16), bc.astype(BF16), cc.astype(BF16), lam16


def _s5(p, batch, seq, ops):
    nb = S5_BLOCK
    nq = BRANCH_W // LANE
    u = p[:, COL_S5:COL_S5 + BRANCH_W].astype(BF16)
    u6 = u.reshape(batch, seq // nb, nb, nq, LANE).transpose(3, 0, 1, 2, 4).reshape(nq, batch * seq // nb, nb * LANE)
    y6 = _s5_scan(u6, *ops, batch)
    return y6.reshape(nq, batch, seq // nb, nb, LANE).transpose(1, 2, 3, 0, 4).reshape(batch * seq, BRANCH_W)


def _merge_kernel(x_ref, sc_ref, sh_ref, gm_ref, ys5_ref, yhg_ref, yret_ref, ym2_ref,
                  wglu_ref, wbr_ref, wg_ref, bg_ref, wout_ref, o_ref):
    x = x_ref[...]
    d = x.shape[1]
    h = _modulated_norm(x, sc_ref[...], sh_ref[...]).astype(BF16)
    y_s5 = jax.nn.gelu(ys5_ref[...])
    y_s5 = y_s5 * jax.nn.sigmoid(jnp.dot(y_s5.astype(BF16), wglu_ref[...], preferred_element_type=F32))
    acc = jnp.zeros(x.shape, F32)
    for n, y in enumerate((y_s5, yhg_ref[...], yret_ref[...], ym2_ref[...])):
        gate = jax.nn.sigmoid(jnp.dot(h, wg_ref[:, n * d:(n + 1) * d], preferred_element_type=F32)
                              + bg_ref[:, n * d:(n + 1) * d])
        acc = acc + gate * jnp.dot(y.astype(BF16), wbr_ref[n], preferred_element_type=F32)
    o_ref[...] = x + gm_ref[...] * jnp.dot(acc.astype(BF16), wout_ref[...], preferred_element_type=F32)


def _merge(x2, mod3, ys5, yhg, yret, ym2, w_glu, w_branch, w_gate, b_gate, w_out, seq):
    t, d = x2.shape
    tm = TM_MERGE
    tpb = seq // tm
    const = lambda shape: pl.BlockSpec(shape, lambda i: (0,) * len(shape))
    modspec = lambda k: pl.BlockSpec((None, 1, d), lambda i: ((i // tpb) * 6 + k, 0, 0))
    yspec = pl.BlockSpec((tm, BRANCH_W), lambda i: (i, 0))
    return pl.pallas_call(
        _merge_kernel,
        grid=(t // tm,),
        in_specs=[pl.BlockSpec((tm, d), lambda i: (i, 0)), modspec(1), modspec(0), modspec(2),
                  yspec, yspec, yspec, yspec,
                  const((BRANCH_W, BRANCH_W)), const((4, BRANCH_W, d)), const((d, 4 * d)), const((1, 4 * d)),
                  const((d, d))],
        out_specs=pl.BlockSpec((tm, d), lambda i: (i, 0)),
        out_shape=jax.ShapeDtypeStruct((t, d), F32),
        compiler_params=_cparams(("parallel",)),
        name="merge",
    )(x2, mod3, mod3, mod3, ys5, yhg, yret, ym2, w_glu, w_branch, w_gate, b_gate, w_out)


def _router_kernel(x_ref, sc_ref, sh_ref, wr_ref, br_ref, tri_ref, h_ref, ids_ref, wts_ref, cnt_ref, carry):
    i = pl.program_id(0)

    @pl.when(i == 0)
    def _():
        carry[...] = jnp.zeros_like(carry)

    h = _modulated_norm(x_ref[...], sc_ref[...], sh_ref[...])
    tm, d = h.shape
    for s in range(d // LANE):
        h_ref[:, s, :] = h[:, s * LANE:(s + 1) * LANE]
    logits = _dot_nt(wr_ref[...], h, precision=HIGHEST) + br_ref[:, 0:1]
    gl = [logits[g:g + 1, :] for g in range(MOE_GROUPS)]
    gmax = gl[0]
    gsel = jnp.zeros((1, tm), jnp.int32)
    for g in range(1, MOE_GROUPS):
        better = gl[g] > gmax
        gsel = jnp.where(better, g, gsel)
        gmax = jnp.where(better, gl[g], gmax)
    gden = gl[0] * 0.0
    for g in range(MOE_GROUPS):
        gden = gden + jnp.exp(gl[g] - gmax)
    g_w = 1.0 / gden
    el = []
    for e in range(MOE_EPG):
        v = logits[MOE_GROUPS + e:MOE_GROUPS + e + 1, :]
        for g in range(1, MOE_GROUPS):
            row = MOE_GROUPS + g * MOE_EPG + e
            v = jnp.where(gsel == g, logits[row:row + 1, :], v)
        el.append(v)
    v1 = el[0]
    i1 = jnp.zeros((1, tm), jnp.int32)
    for e in range(1, MOE_EPG):
        better = el[e] > v1
        i1 = jnp.where(better, e, i1)
        v1 = jnp.where(better, el[e], v1)
    v2 = jnp.full((1, tm), -jnp.inf, F32)
    i2 = jnp.zeros((1, tm), jnp.int32)
    for e in range(MOE_EPG):
        better = (el[e] > v2) & (i1 != e)
        i2 = jnp.where(better, e, i2)
        v2 = jnp.where(better, el[e], v2)
    ex = jnp.exp(v2 - v1)
    p1 = 1.0 / (1.0 + ex)
    e1 = gsel * MOE_EPG + i1
    e2 = gsel * MOE_EPG + i2
    erow = lax.broadcasted_iota(jnp.int32, (MOE_EXPERTS, tm), 0)
    oh1 = (erow == e1).astype(F32)
    oh2 = (erow == e2).astype(F32)
    both = oh1 + oh2
    prefix = jnp.dot(both.astype(BF16), tri_ref[...], preferred_element_type=F32) + carry[:, 0:1]
    rank1 = jnp.sum(oh1 * prefix, axis=0, keepdims=True).astype(jnp.int32)
    rank2 = jnp.sum(oh2 * prefix, axis=0, keepdims=True).astype(jnp.int32)
    carry[...] = carry[...] + jnp.sum(both, axis=1, keepdims=True)
    zi = jnp.zeros((1, tm), jnp.int32)
    ids_ref[...] = jnp.concatenate([e1, e2, rank1, rank2, zi, zi, zi, zi], axis=0)
    wrow = lax.broadcasted_iota(jnp.int32, (LANE, tm), 0)
    wts_ref[...] = jnp.where(wrow == 0, p1 * g_w, jnp.where(wrow == 1, ex * p1 * g_w, 0.0)).T
    cnt_ref[...] = carry[...]


def _router(x2, mod3, w_route, b_route, tri_excl, seq):
    t, d = x2.shape
    tm = TM_PROJ
    tpb = seq // tm
    nr = w_route.shape[0]
    const = lambda shape: pl.BlockSpec(shape, lambda i: (0,) * len(shape))
    modspec = lambda k: pl.BlockSpec((None, 1, d), lambda i: ((i // tpb) * 6 + k, 0, 0))
    return pl.pallas_call(
        _router_kernel,
        grid=(t // tm,),
        in_specs=[pl.BlockSpec((tm, d), lambda i: (i, 0)), modspec(4), modspec(3),
                  const((nr, d)), const((nr, LANE)), const((tm, tm))],
        out_specs=[pl.BlockSpec((tm, d // LANE, LANE), lambda i: (i, 0, 0)),
                   pl.BlockSpec((8, tm), lambda i: (0, i)),
                   pl.BlockSpec((tm, LANE), lambda i: (i, 0)),
                   const((MOE_EXPERTS, LANE))],
        out_shape=[jax.ShapeDtypeStruct((t, d // LANE, LANE), F32),
                   jax.ShapeDtypeStruct((8, t), jnp.int32),
                   jax.ShapeDtypeStruct((t, LANE), F32),
                   jax.ShapeDtypeStruct((MOE_EXPERTS, LANE), F32)],
        scratch_shapes=[pltpu.VMEM((MOE_EXPERTS, LANE), F32)],
        compiler_params=_cparams(("arbitrary",)),
        name="moe_router",
    )(x2, mod3, mod3, w_route, b_route, tri_excl)


def _dispatch_kernel(s1_ref, s2_ref, h_ref, zero_hbm, xs_hbm, sem, *, tm):
    del zero_hbm

    def row_copy(r, slot):
        return pltpu.make_async_copy(h_ref.at[r], xs_hbm.at[slot], sem)

    def issue(r, c):
        row_copy(r, s1_ref[0, r]).start()
        row_copy(r, s2_ref[0, r]).start()
        return c

    lax.fori_loop(0, tm, issue, 0)

    def drain(r, c):
        row_copy(r, s1_ref[0, r]).wait()
        row_copy(r, s2_ref[0, r]).wait()
        return c

    lax.fori_loop(0, tm, drain, 0)


def _dispatch(slot1, slot2, h3, zeros3):
    t, nsl, _ = h3.shape
    tm = TM_DISP
    smem = lambda: pl.BlockSpec((None, 1, tm), lambda i: (i, 0, 0), memory_space=pltpu.SMEM)
    return pl.pallas_call(
        functools.partial(_dispatch_kernel, tm=tm),
        grid=(t // tm,),
        in_specs=[smem(), smem(), pl.BlockSpec((tm, nsl, LANE), lambda i: (i, 0, 0)),
                  pl.BlockSpec(memory_space=pl.ANY)],
        out_specs=pl.BlockSpec(memory_space=pl.ANY),
        out_shape=jax.ShapeDtypeStruct(zeros3.shape, zeros3.dtype),
        scratch_shapes=[pltpu.SemaphoreType.DMA(())],
        input_output_aliases={3: 0},
        compiler_params=_cparams(("arbitrary",)),
        name="moe_dispatch",
    )(slot1.reshape(t // tm, 1, tm), slot2.reshape(t // tm, 1, tm), h3, zeros3)


def _expert_kernel(texp_ref, xs_ref, w1_ref, w3_ref, w2_ref, ys_ref, x_scr):
    del texp_ref
    nsl = xs_ref.shape[1]
    for s in range(nsl):
        x_scr[:, s * LANE:(s + 1) * LANE] = xs_ref[:, s, :].astype(BF16)
    x = x_scr[...]
    a = jnp.dot(x, w1_ref[...], preferred_element_type=F32)
    b = jnp.dot(x, w3_ref[...], preferred_element_type=F32)
    act = _silu(a) * b
    y = jnp.dot(act.astype(BF16), w2_ref[...], preferred_element_type=F32)
    for s in range(nsl):
        ys_ref[:, s, :] = y[:, s * LANE:(s + 1) * LANE]


def _experts(tile_expert, xs3, w1, w3, w2):
    ns, nsl, _ = xs3.shape
    d = nsl * LANE
    ff = w1.shape[2]
    grid_spec = pltpu.PrefetchScalarGridSpec(
        num_scalar_prefetch=1,
        grid=(ns // TM_X,),
        in_specs=[pl.BlockSpec((TM_X, nsl, LANE), lambda i, te: (i, 0, 0)),
                  pl.BlockSpec((None, d, ff), lambda i, te: (te[i], 0, 0)),
                  pl.BlockSpec((None, d, ff), lambda i, te: (te[i], 0, 0)),
                  pl.BlockSpec((None, ff, d), lambda i, te: (te[i], 0, 0))],
        out_specs=pl.BlockSpec((TM_X, nsl, LANE), lambda i, te: (i, 0, 0)),
        scratch_shapes=[pltpu.VMEM((TM_X, d), BF16)],
    )
    return pl.pallas_call(
        _expert_kernel,
        grid_spec=grid_spec,
        out_shape=jax.ShapeDtypeStruct((ns, nsl, LANE), F32),
        compiler_params=_cparams(("parallel",)),
        name="moe_experts",
    )(tile_expert, xs3, w1, w3, w2)


def _combine_kernel(s1_ref, s2_ref, x_ref, gate_ref, fw_ref, wcol_ref, ys_hbm, o_ref, buf, sem, *, tm, final):
    def row_copy(which, r, slot):
        return pltpu.make_async_copy(ys_hbm.at[slot], buf.at[which, r], sem)

    def issue(r, c):
        row_copy(0, r, s1_ref[0, r]).start()
        row_copy(1, r, s2_ref[0, r]).start()
        return c

    lax.fori_loop(0, tm, issue, 0)

    def drain(r, c):
        row_copy(0, r, s1_ref[0, r]).wait()
        row_copy(1, r, s2_ref[0, r]).wait()
        return c

    lax.fori_loop(0, tm, drain, 0)
    nsl = buf.shape[2]
    w_first = wcol_ref[:, 0:1]
    w_second = wcol_ref[:, 1:2]
    parts = [w_first * buf[0, :, s, :] + w_second * buf[1, :, s, :] for s in range(nsl)]
    moe = jnp.concatenate(parts, axis=-1)
    x = x_ref[...] + gate_ref[...] * moe
    if final:
        x = x * lax.rsqrt(jnp.mean(x * x, axis=-1, keepdims=True) + EPS) * fw_ref[...]
    o_ref[...] = x


def _combine(slot1, slot2, x2, mod3, final_w_row, wcol, ys3, seq, final):
    t, d = x2.shape
    tm = TM_COMB
    tpb = seq // tm
    nsl = ys3.shape[1]
    smem = lambda: pl.BlockSpec((None, 1, tm), lambda i: (i, 0, 0), memory_space=pltpu.SMEM)
    return pl.pallas_call(
        functools.partial(_combine_kernel, tm=tm, final=final),
        grid=(t // tm,),
        in_specs=[smem(), smem(), pl.BlockSpec((tm, d), lambda i: (i, 0)),
                  pl.BlockSpec((None, 1, d), lambda i: ((i // tpb) * 6 + 5, 0, 0)),
                  pl.BlockSpec((1, d), lambda i: (0, 0)),
                  pl.BlockSpec((tm, LANE), lambda i: (i, 0)),
                  pl.BlockSpec(memory_space=pl.ANY)],
        out_specs=pl.BlockSpec((tm, d), lambda i: (i, 0)),
        out_shape=jax.ShapeDtypeStruct((t, d), F32),
        scratch_shapes=[pltpu.VMEM((2, tm, nsl, LANE), F32), pltpu.SemaphoreType.DMA(())],
        compiler_params=_cparams(("arbitrary",)),
        name="moe_combine",
    )(slot1.reshape(t // tm, 1, tm), slot2.reshape(t // tm, 1, tm), x2, mod3, final_w_row, wcol, ys3)


def _moe(x2, mod3, final_w_row, w_route, b_route, tri_excl, w1, w3, w2, seq, final):
    t, d = x2.shape
    h3, ids, wcol, counts = _router(x2, mod3, w_route, b_route, tri_excl, seq)
    cnt = counts[:, 0].astype(jnp.int32)
    padded = (cnt + TM_X - 1) // TM_X * TM_X
    ends = jnp.cumsum(padded)
    offs = ends - padded
    slot1 = offs[ids[0]] + ids[2]
    slot2 = offs[ids[1]] + ids[3]
    ns = 2 * t + MOE_EXPERTS * TM_X
    tile_start = jnp.arange(ns // TM_X, dtype=jnp.int32) * TM_X
    tile_expert = jnp.minimum(jnp.sum(tile_start[:, None] >= ends[None, :], axis=1), MOE_EXPERTS - 1).astype(jnp.int32)
    xs3 = _dispatch(slot1, slot2, h3, jnp.zeros((ns, d // LANE, LANE), F32))
    ys3 = _experts(tile_expert, xs3, w1, w3, w2)
    return _combine(slot1, slot2, x2, mod3, final_w_row, wcol, ys3, seq, final)


def kernel(x, c, positions, ada_w, ada_b, w_in, s5_lam_re, s5_lam_im, s5_b_re, s5_b_im, s5_c_re, s5_c_im, s5_d, s5_log_dt, s5_w_glu, hg_lb_logits, hg_norm_w, m2_conv_w, m2_conv_b, m2_dt_bias, m2_a_log, m2_d, m2_norm_w, w_branch, w_gate, b_gate, w_out, moe_w_group, moe_b_group, moe_w_expert, moe_b_expert, moe_w1, moe_w3, moe_w2, final_norm_w):
    bsz, seq, d = x.shape
    t = bsz * seq
    depth = ada_w.shape[0]
    assert seq % TM_PROJ == 0 and seq % C_RET == 0 and seq % C_SSD == 0 and seq % C_HG == 0
    x2 = x.reshape(t, d).astype(F32)

    c_pad = jnp.zeros((8, d), F32).at[:bsz].set(c.astype(F32))
    mod_all = _ada_mod(c_pad, ada_w.astype(F32), ada_b.astype(F32))

    half = RET_DK // 2
    inv_freq = ROPE_BASE ** (-jnp.arange(half, dtype=F32) / half)
    invf_row = jnp.tile(inv_freq, 2 * RET_HEADS)[None, :]
    sgn_row = jnp.asarray(np.tile(np.concatenate([-np.ones(half), np.ones(half)]), RET_HEADS)[None, :], F32)
    cos_t, sin_t = _rope_tables(positions.reshape(t, 1).astype(jnp.int32), invf_row, sgn_row)
    cos3 = cos_t.reshape(bsz, seq, -1)
    sin3 = sin_t.reshape(bsz, seq, -1)

    lb_cum = jnp.cumsum(jax.nn.softmax(hg_lb_logits.astype(F32), axis=0), axis=0)
    hg_lb = lb_cum - lb_cum[:1]
    tri_ssd = jnp.asarray(np.tril(np.ones((C_SSD, C_SSD), np.float32)))
    tri_excl = jnp.asarray(np.triu(np.ones((TM_PROJ, TM_PROJ), np.float32), 1), BF16)
    final_w_row = final_norm_w.astype(F32)[None, :]

    for layer in range(depth):
        mod3 = mod_all[layer, :bsz].reshape(bsz * 6, 1, d)
        w_pad = jnp.zeros((d, IN_W_PAD), BF16).at[:, :IN_W].set(w_in[layer].astype(BF16))
        p = _in_proj(x2, mod3, w_pad, seq)
        p3 = p.reshape(bsz, seq, IN_W_PAD)

        ops = _s5_operators(s5_lam_re[layer], s5_lam_im[layer], s5_b_re[layer], s5_b_im[layer],
                            s5_c_re[layer], s5_c_im[layer], s5_d[layer], s5_log_dt[layer])
        y_s5 = _s5(p, bsz, seq, ops)

        lb = hg_lb[layer][None, :]
        y_hg = _hgrn2(p3, jnp.log(lb), jnp.log1p(-lb), hg_norm_w[layer].astype(F32)[None, :]).reshape(t, BRANCH_W)

        y_ret = _retention(p3, cos3, sin3).reshape(t, BRANCH_W)

        pad8 = lambda v: jnp.zeros((1, LANE), F32).at[0, :M2_HEADS].set(v.astype(F32))
        y_m2 = _ssd(p3, tri_ssd, m2_conv_w[layer].astype(F32), m2_conv_b[layer].astype(F32)[None, :],
                    pad8(m2_dt_bias[layer]), pad8(m2_a_log[layer]),
                    jnp.repeat(m2_d[layer].astype(F32), M2_HEADDIM)[None, :],
                    m2_norm_w[layer].astype(F32)[None, :]).reshape(t, BRANCH_W)

        x2 = _merge(x2, mod3, y_s5, y_hg, y_ret, y_m2, s5_w_glu[layer].astype(BF16),
                    w_branch[layer].astype(BF16), w_gate[layer].astype(BF16), b_gate[layer].astype(F32)[None, :],
                    w_out[layer].astype(BF16), seq)

        nr = 40
        w_route = jnp.zeros((nr, d), F32).at[:MOE_GROUPS].set(moe_w_group[layer].astype(F32).T)
        w_route = w_route.at[MOE_GROUPS:MOE_GROUPS + MOE_EXPERTS].set(moe_w_expert[layer].astype(F32).T)
        b_route = jnp.zeros((nr, LANE), F32).at[:MOE_GROUPS, 0].set(moe_b_group[layer].astype(F32))
        b_route = b_route.at[MOE_GROUPS:MOE_GROUPS + MOE_EXPERTS, 0].set(moe_b_expert[layer].astype(F32))
        x2 = _moe(x2, mod3, final_w_row, w_route, b_route, tri_excl,
                  moe_w1[layer].reshape(MOE_EXPERTS, d, MOE_FF).astype(BF16),
                  moe_w3[layer].reshape(MOE_EXPERTS, d, MOE_FF).astype(BF16),
                  moe_w2[layer].reshape(MOE_EXPERTS, MOE_FF, d).astype(BF16),
                  seq, final=(layer == depth - 1))
    return x2.reshape(bsz, seq, d)
```

```python
import functools
import math

import numpy as np
import jax
import jax.numpy as jnp
from jax import lax
from jax.experimental import pallas as pl
from jax.experimental.pallas import tpu as pltpu

F32 = jnp.float32
BF16 = jnp.bfloat16
HIGHEST = lax.Precision.HIGHEST

D_MODEL = 1024
DEPTH = 2
BRANCH_W = 512
EPS = 1e-6
S5_GROUPS = 32
S5_CH = 16
S5_STATE = 64
S5_MAX_REAL = -1e-4
S5_BLOCK = 16
HG_HEADS = 4
HG_DK = 128
RET_HEADS = 4
RET_DK = 64
RET_DV = 128
ROPE_BASE = 10000.0
M2_HEADS = 8
M2_HEADDIM = 64
M2_GROUPS = 2
M2_STATE = 128
M2_CONV = 4
MOE_GROUPS = 4
MOE_EPG = 8
MOE_EXPERTS = MOE_GROUPS * MOE_EPG
MOE_FF = 256

COL_S5, COL_HQ, COL_HF, COL_HI, COL_HG = 0, 512, 1024, 1536, 2048
COL_RQ, COL_RK, COL_RV, COL_RG = 2560, 2816, 3072, 3584
COL_MZ, COL_MXS, COL_MBC, COL_MDT = 4096, 4608, 5120, 5632
IN_W = 5640
IN_W_PAD = 5760

LANE = 128
VMEM_LIMIT = 56 * 1024 * 1024

TM_PROJ = 1024
TN_PROJ = 1152
TM_MERGE = 512
C_RET = 256
C_SSD = 256
C_HG = 128
TM_X = 256
TM_COMB = 512
TM_DISP = 512


def _cparams(sem):
    return pltpu.CompilerParams(dimension_semantics=sem, vmem_limit_bytes=VMEM_LIMIT)


def _silu(v):
    return v * jax.nn.sigmoid(v)


def _dot_nt(a, b, **kw):
    return lax.dot_general(a, b, (((1,), (1,)), ((), ())), preferred_element_type=F32, **kw)


def _dot_tn(a, b, **kw):
    return lax.dot_general(a, b, (((0,), (0,)), ((), ())), preferred_element_type=F32, **kw)


def _ada_kernel(c_ref, w_ref, b_ref, o_ref):
    cond = _silu(c_ref[...])
    o_ref[...] = jnp.dot(cond, w_ref[...], preferred_element_type=F32, precision=HIGHEST) + b_ref[...]


def _ada_mod(c_pad, ada_w, ada_b):
    depth, d, n = ada_w.shape
    tn = 1536
    return pl.pallas_call(
        _ada_kernel,
        grid=(depth, n // tn),
        in_specs=[pl.BlockSpec((8, d), lambda l, j: (0, 0)),
                  pl.BlockSpec((None, d, tn), lambda l, j: (l, 0, j)),
                  pl.BlockSpec((None, 1, tn), lambda l, j: (l, 0, j))],
        out_specs=pl.BlockSpec((None, 8, tn), lambda l, j: (l, 0, j)),
        out_shape=jax.ShapeDtypeStruct((depth, 8, n), F32),
        compiler_params=_cparams(("parallel", "parallel")),
        name="ada_mod",
    )(c_pad, ada_w, ada_b.reshape(depth, 1, n))


def _modulated_norm(x, scale, shift):
    ms = jnp.mean(x * x, axis=-1, keepdims=True)
    return x * lax.rsqrt(ms + EPS) * (1.0 + scale) + shift


def _inproj_kernel(x_ref, sc_ref, sh_ref, w_ref, o_ref, h_scr):
    @pl.when(pl.program_id(1) == 0)
    def _():
        h_scr[...] = _modulated_norm(x_ref[...], sc_ref[...], sh_ref[...]).astype(BF16)

    o_ref[...] = jnp.dot(h_scr[...], w_ref[...], preferred_element_type=F32)


def _in_proj(x2, mod3, w_pad, seq):
    t, d = x2.shape
    tpb = seq // TM_PROJ
    return pl.pallas_call(
        _inproj_kernel,
        grid=(t // TM_PROJ, IN_W_PAD // TN_PROJ),
        in_specs=[pl.BlockSpec((TM_PROJ, d), lambda i, j: (i, 0)),
                  pl.BlockSpec((None, 1, d), lambda i, j: ((i // tpb) * 6 + 1, 0, 0)),
                  pl.BlockSpec((None, 1, d), lambda i, j: ((i // tpb) * 6 + 0, 0, 0)),
                  pl.BlockSpec((d, TN_PROJ), lambda i, j: (0, j))],
        out_specs=pl.BlockSpec((TM_PROJ, TN_PROJ), lambda i, j: (i, j)),
        out_shape=jax.ShapeDtypeStruct((t, IN_W_PAD), F32),
        scratch_shapes=[pltpu.VMEM((TM_PROJ, d), BF16)],
        compiler_params=_cparams(("parallel", "arbitrary")),
        name="in_proj",
    )(x2, mod3, mod3, w_pad)


def _rope_kernel(pos_ref, invf_ref, sgn_ref, cos_ref, sin_ref):
    ang = pos_ref[...].astype(F32) * invf_ref[...]
    cos_ref[...] = jnp.cos(ang)
    sin_ref[...] = jnp.sin(ang) * sgn_ref[...]


def _rope_tables(pos_col, invf_row, sgn_row):
    t = pos_col.shape[0]
    w = invf_row.shape[1]
    tm = 1024
    return pl.pallas_call(
        _rope_kernel,
        grid=(t // tm,),
        in_specs=[pl.BlockSpec((tm, 1), lambda i: (i, 0)),
                  pl.BlockSpec((1, w), lambda i: (0, 0)),
                  pl.BlockSpec((1, w), lambda i: (0, 0))],
        out_specs=[pl.BlockSpec((tm, w), lambda i: (i, 0))] * 2,
        out_shape=[jax.ShapeDtypeStruct((t, w), F32)] * 2,
        compiler_params=_cparams(("parallel",)),
        name="rope_tables",
    )(pos_col, invf_row, sgn_row)


def _ret_kernel(q_ref, k_ref, v_ref, g_ref, cos_ref, sin_ref, o_ref, st_ref, *, chunk):
    @pl.when(pl.program_id(1) == 0)
    def _():
        st_ref[...] = jnp.zeros_like(st_ref)

    cosf = cos_ref[...]
    sinf = sin_ref[...]
    width = RET_HEADS * RET_DK
    lane = lax.broadcasted_iota(jnp.int32, (chunk, width), 1)
    first_half = (lane % RET_DK) < (RET_DK // 2)

    def rope(t):
        partner = jnp.where(first_half, pltpu.roll(t, width - RET_DK // 2, 1), pltpu.roll(t, RET_DK // 2, 1))
        return t * cosf + partner * sinf

    q = rope(q_ref[...])
    k = rope(k_ref[...]) * (RET_DK ** -0.5)
    v = v_ref[...]
    g = g_ref[...]
    ti = lax.broadcasted_iota(jnp.int32, (chunk, chunk), 0)
    si = lax.broadcasted_iota(jnp.int32, (chunk, chunk), 1)
    lag = (ti - si).astype(F32)
    tcol = lax.broadcasted_iota(jnp.int32, (chunk, 1), 0).astype(F32)
    for h in range(RET_HEADS):
        log_gamma = math.log1p(-(2.0 ** (-5.0 - h)))
        qh = q[:, h * RET_DK:(h + 1) * RET_DK]
        kh = k[:, h * RET_DK:(h + 1) * RET_DK]
        vh = v[:, h * RET_DV:(h + 1) * RET_DV].astype(BF16)
        decay = jnp.where(ti >= si, jnp.exp(jnp.minimum(lag * log_gamma, 0.0)), 0.0)
        scores = _dot_nt(qh.astype(BF16), kh.astype(BF16)) * decay
        state = st_ref[h]
        q_in = qh * jnp.exp(log_gamma * (tcol + 1.0))
        o = (jnp.dot(scores.astype(BF16), vh, preferred_element_type=F32)
             + jnp.dot(q_in.astype(BF16), state.astype(BF16), preferred_element_type=F32))
        k_out = kh * jnp.exp(log_gamma * (chunk - 1.0 - tcol))
        st_ref[h] = math.exp(log_gamma * chunk) * state + _dot_tn(k_out.astype(BF16), vh)
        o = o * lax.rsqrt(jnp.mean(o * o, axis=-1, keepdims=True) + EPS)
        gh = g[:, h * RET_DV:(h + 1) * RET_DV]
        o_ref[:, h * RET_DV:(h + 1) * RET_DV] = o * _silu(gh)


def _retention(p3, cos3, sin3):
    b, seq, _ = p3.shape
    c = C_RET
    qk_w = RET_HEADS * RET_DK
    return pl.pallas_call(
        functools.partial(_ret_kernel, chunk=c),
        grid=(b, seq // c),
        in_specs=[pl.BlockSpec((None, c, qk_w), lambda i, j: (i, j, COL_RQ // qk_w)),
                  pl.BlockSpec((None, c, qk_w), lambda i, j: (i, j, COL_RK // qk_w)),
                  pl.BlockSpec((None, c, BRANCH_W), lambda i, j: (i, j, COL_RV // BRANCH_W)),
                  pl.BlockSpec((None, c, BRANCH_W), lambda i, j: (i, j, COL_RG // BRANCH_W)),
                  pl.BlockSpec((None, c, qk_w), lambda i, j: (i, j, 0)),
                  pl.BlockSpec((None, c, qk_w), lambda i, j: (i, j, 0))],
        out_specs=pl.BlockSpec((None, c, BRANCH_W), lambda i, j: (i, j, 0)),
        out_shape=jax.ShapeDtypeStruct((b, seq, BRANCH_W), F32),
        scratch_shapes=[pltpu.VMEM((RET_HEADS, RET_DK, RET_DV), F32)],
        compiler_params=_cparams(("parallel", "arbitrary")),
        name="retention",
    )(p3, p3, p3, p3, cos3, sin3)


def _ssd_kernel(z_ref, xs_ref, bc_ref, dt_ref, tri_ref, cw_ref, cb_ref, dtb_ref, alog_ref, dsk_ref, nw_ref,
                o_ref, xe_scr, st_ref, *, chunk):
    j = pl.program_id(1)
    width = 2 * BRANCH_W

    @pl.when(j == 0)
    def _():
        st_ref[...] = jnp.zeros_like(st_ref)
        xe_scr[0:8, :] = jnp.zeros((8, width), F32)

    @pl.when(j > 0)
    def _():
        xe_scr[0:8, :] = xe_scr[chunk:chunk + 8, :]

    xe_scr[8:, 0:BRANCH_W] = xs_ref[...]
    xe_scr[8:, BRANCH_W:] = bc_ref[...]
    conv = cb_ref[...] + cw_ref[M2_CONV - 1:M2_CONV, :] * xe_scr[8:, :]
    for tap in range(M2_CONV - 1):
        conv = conv + cw_ref[tap:tap + 1, :] * xe_scr[pl.ds(8 - (M2_CONV - 1) + tap, chunk), :]
    conv = _silu(conv)
    xs = conv[:, :BRANCH_W]
    bm = conv[:, BRANCH_W:BRANCH_W + M2_GROUPS * M2_STATE]
    cm = conv[:, BRANCH_W + M2_GROUPS * M2_STATE:]

    dt = jax.nn.softplus(dt_ref[...] + dtb_ref[...])
    da = dt * (-jnp.exp(alog_ref[...]))
    a_cs = jnp.dot(tri_ref[...], da, preferred_element_type=F32, precision=HIGHEST)
    a_cs_t = a_cs.T
    ti = lax.broadcasted_iota(jnp.int32, (chunk, chunk), 0)
    si = lax.broadcasted_iota(jnp.int32, (chunk, chunk), 1)
    causal = ti >= si
    hpg = M2_HEADS // M2_GROUPS
    ys = []
    for grp in range(M2_GROUPS):
        bm_g = bm[:, grp * M2_STATE:(grp + 1) * M2_STATE]
        cm_g = cm[:, grp * M2_STATE:(grp + 1) * M2_STATE]
        cb = _dot_nt(cm_g.astype(BF16), bm_g.astype(BF16))
        for hh in range(hpg):
            h = grp * hpg + hh
            col = a_cs[:, h:h + 1]
            row = a_cs_t[h:h + 1, :]
            lmat = jnp.where(causal, jnp.exp(jnp.minimum(col - row, 0.0)), 0.0)
            xd = xs[:, h * M2_HEADDIM:(h + 1) * M2_HEADDIM] * dt[:, h:h + 1]
            state = st_ref[h]
            y = (jnp.dot((cb * lmat).astype(BF16), xd.astype(BF16), preferred_element_type=F32)
                 + jnp.dot((cm_g * jnp.exp(col)).astype(BF16), state.astype(BF16), preferred_element_type=F32))
            a_last = a_cs[chunk - 1:chunk, h:h + 1]
            to_end = jnp.exp(a_last - col)
            st_ref[h] = jnp.exp(a_last) * state + _dot_tn(bm_g.astype(BF16), (xd * to_end).astype(BF16))
            ys.append(y)
    y = jnp.concatenate(ys, axis=-1) + dsk_ref[...] * xs
    y = y * _silu(z_ref[...])
    o_ref[...] = y * lax.rsqrt(jnp.mean(y * y, axis=-1, keepdims=True) + EPS) * nw_ref[...]


def _ssd(p3, tri, conv_w, conv_b, dt_bias_row, a_log_row, d_skip_row, norm_w_row):
    b, seq, _ = p3.shape
    c = C_SSD
    const = lambda shape: pl.BlockSpec(shape, lambda i, j: (0,) * len(shape))
    return pl.pallas_call(
        functools.partial(_ssd_kernel, chunk=c),
        grid=(b, seq // c),
        in_specs=[pl.BlockSpec((None, c, BRANCH_W), lambda i, j: (i, j, COL_MZ // BRANCH_W)),
                  pl.BlockSpec((None, c, BRANCH_W), lambda i, j: (i, j, COL_MXS // BRANCH_W)),
                  pl.BlockSpec((None, c, BRANCH_W), lambda i, j: (i, j, COL_MBC // BRANCH_W)),
                  pl.BlockSpec((None, c, LANE), lambda i, j: (i, j, COL_MDT // LANE)),
                  const((c, c)), const((M2_CONV, 2 * BRANCH_W)), const((1, 2 * BRANCH_W)),
                  const((1, LANE)), const((1, LANE)), const((1, BRANCH_W)), const((1, BRANCH_W))],
        out_specs=pl.BlockSpec((None, c, BRANCH_W), lambda i, j: (i, j, 0)),
        out_shape=jax.ShapeDtypeStruct((b, seq, BRANCH_W), F32),
        scratch_shapes=[pltpu.VMEM((c + 8, 2 * BRANCH_W), F32),
                        pltpu.VMEM((M2_HEADS, M2_STATE, M2_HEADDIM), F32)],
        compiler_params=_cparams(("parallel", "arbitrary")),
        name="ssd",
    )(p3, p3, p3, p3, tri, conv_w, conv_b, dt_bias_row, a_log_row, d_skip_row, norm_w_row)


def _hg_tables(chunk):
    n_lev = int(math.log2(chunk))
    r = np.arange(chunk)[:, None]
    jj = np.arange(chunk)[None, :]
    mats = [(jj <= r), (jj > r)]
    for lev in range(n_lev):
        m = 1 << lev
        mid = (r >> (lev + 1) << (lev + 1)) + m - 1
        second = ((r >> lev) & 1) == 1
        mats.append(np.where(second, (jj > mid) & (jj <= r), (jj > r) & (jj <= mid)))
    summat = np.concatenate(mats, axis=0).astype(np.float32)
    x = r ^ jj
    levmap = np.where(r > jj, np.floor(np.log2(x + 0.5)), np.where(r == jj, -1, -2)).astype(np.int32)
    return summat, levmap, n_lev


def _hg_kernel(q_ref, f_ref, i_ref, g_ref, sum_ref, lev_ref, llb_ref, l1m_ref, nw_ref, o_ref, st_ref,
               *, chunk, n_lev):
    @pl.when(pl.program_id(1) == 0)
    def _():
        st_ref[...] = jnp.zeros_like(st_ref)

    f = f_ref[...]
    log_sig = jnp.minimum(f, 0.0) - jnp.log1p(jnp.exp(-jnp.abs(f)))
    a = llb_ref[...]
    bb = l1m_ref[...] + log_sig
    log_f = jnp.maximum(a, bb) + jnp.log1p(jnp.exp(-jnp.abs(a - bb)))
    k_all = jnp.exp(l1m_ref[...]) * jax.nn.sigmoid(-f)
    q_all = _silu(q_ref[...])
    hi = log_f.astype(BF16)
    r1 = log_f - hi.astype(F32)
    mid = r1.astype(BF16)
    lo = (r1 - mid.astype(F32)).astype(BF16)
    summat = sum_ref[...]
    sums = (jnp.dot(summat, hi, preferred_element_type=F32)
            + jnp.dot(summat, mid, preferred_element_type=F32)
            + jnp.dot(summat, lo, preferred_element_type=F32))
    levmap = lev_ref[...]
    v_all = i_ref[...]
    g_all = g_ref[...]
    for h in range(HG_HEADS):
        sl = slice(h * HG_DK, (h + 1) * HG_DK)
        qh = q_all[:, sl]
        kh = k_all[:, sl]
        vh = v_all[:, sl].astype(BF16)
        b_h = sums[0:chunk, sl]
        to_end = sums[chunk:2 * chunk, sl]
        amat = jnp.where(levmap == -1, _dot_nt(qh.astype(BF16), kh.astype(BF16)), 0.0)
        for lev in range(n_lev):
            e = jnp.exp(sums[(2 + lev) * chunk:(3 + lev) * chunk, sl])
            a_l = _dot_nt((qh * e).astype(BF16), (kh * e).astype(BF16))
            amat = jnp.where(levmap == lev, a_l, amat)
        state_t = st_ref[h]
        o = (jnp.dot(amat.astype(BF16), vh, preferred_element_type=F32)
             + _dot_nt((qh * jnp.exp(b_h)).astype(BF16), state_t.astype(BF16)))
        k_end = kh * jnp.exp(to_end)
        st_ref[h] = jnp.exp(b_h[chunk - 1:chunk, :]) * state_t + _dot_tn(vh, k_end.astype(BF16))
        o = o * lax.rsqrt(jnp.mean(o * o, axis=-1, keepdims=True) + EPS) * nw_ref[...]
        o_ref[:, sl] = o * _silu(g_all[:, sl])


def _hgrn2(p3, log_lb_row, log1m_lb_row, norm_w_row):
    b, seq, _ = p3.shape
    c = C_HG
    summat, levmap, n_lev = _hg_tables(c)
    const = lambda shape: pl.BlockSpec(shape, lambda i, j: (0,) * len(shape))
    blk = lambda col: pl.BlockSpec((None, c, BRANCH_W), lambda i, j: (i, j, col // BRANCH_W))
    return pl.pallas_call(
        functools.partial(_hg_kernel, chunk=c, n_lev=n_lev),
        grid=(b, seq // c),
        in_specs=[blk(COL_HQ), blk(COL_HF), blk(COL_HI), blk(COL_HG),
                  const(summat.shape), const((c, c)),
                  const((1, BRANCH_W)), const((1, BRANCH_W)), const((1, HG_DK))],
        out_specs=pl.BlockSpec((None, c, BRANCH_W), lambda i, j: (i, j, 0)),
        out_shape=jax.ShapeDtypeStruct((b, seq, BRANCH_W), F32),
        scratch_shapes=[pltpu.VMEM((HG_HEADS, HG_DK, HG_DK), F32)],
        compiler_params=_cparams(("parallel", "arbitrary")),
        name="hgrn2",
    )(p3, p3, p3, p3, jnp.asarray(summat, BF16), jnp.asarray(levmap), log_lb_row, log1m_lb_row, norm_w_row)


def _expand_block_diag(comp_ref, e_ref, dst_ref, row_div, lane_div):
    gq = LANE // S5_CH
    rows, ncols = dst_ref.shape
    step = 512
    comp = comp_ref[...]
    row_grp = (lax.broadcasted_iota(jnp.int32, (rows, step), 0) // row_div) % gq
    for c0 in range(0, ncols, step):
        lane_grp = ((lax.broadcasted_iota(jnp.int32, (rows, step), 1) + c0) // lane_div) % gq
        full = jnp.dot(comp, e_ref[:, c0:c0 + step], preferred_element_type=F32)
        dst_ref[:, c0:c0 + step] = jnp.where(row_grp == lane_grp, full, 0.0).astype(dst_ref.dtype)


def _s5_kernel(x_ref, tc_ref, bc_ref, cc_ref, esc_ref, eb_ref, lam_ref, o_ref, tq_ref, bq_ref, cq_ref,
               w_scr, s_scr, *, rows):
    @pl.when(pl.program_id(1) == 0)
    def _():
        _expand_block_diag(tc_ref, esc_ref, tq_ref, S5_CH, S5_CH)
        _expand_block_diag(bc_ref, eb_ref, bq_ref, S5_CH, S5_STATE)
        _expand_block_diag(cc_ref, esc_ref, cq_ref, S5_STATE, S5_CH)

    x = x_ref[...]
    half = w_scr.shape[1] // 2
    w_scr[...] = jnp.dot(x, bq_ref[...], preferred_element_type=F32)
    lam_re = lam_ref[0:1, :]
    lam_im = lam_ref[1:2, :]

    def body(j, carry):
        s_re, s_im = carry
        s_scr[pl.ds(j, 1), 0:half] = s_re
        s_scr[pl.ds(j, 1), half:] = s_im
        w_re = w_scr[pl.ds(j, 1), 0:half]
        w_im = w_scr[pl.ds(j, 1), half:]
        return lam_re * s_re - lam_im * s_im + w_re, lam_re * s_im + lam_im * s_re + w_im

    zero = jnp.zeros((1, half), F32)
    lax.fori_loop(0, rows, body, (zero, zero))
    o_ref[...] = (jnp.dot(x, tq_ref[...], preferred_element_type=F32)
                  + jnp.dot(s_scr[...].astype(BF16), cq_ref[...], preferred_element_type=F32))


def _s5_scan(u6, tc, bc, cc, lam16, batch):
    nq, rows_all, kdim = u6.shape
    rows = rows_all // batch
    gq = LANE // S5_CH
    ncol = 2 * gq * S5_STATE
    nb = S5_BLOCK
    e_sc = (np.eye(nb)[:, None, :, None, None] * np.eye(S5_CH)[None, :, None, None, :] * np.ones((1, 1, 1, gq, 1)))
    e_sc = e_sc.reshape(nb * S5_CH, nb * gq * S5_CH)
    e_b = (np.eye(2)[:, None, :, None, None] * np.eye(S5_STATE)[None, :, None, None, :] * np.ones((1, 1, 1, gq, 1)))
    e_b = e_b.reshape(2 * S5_STATE, ncol)
    full = lambda shape: pl.BlockSpec(shape, lambda q, b: (0,) * len(shape))
    per_q = lambda r, c: pl.BlockSpec((None, r, c), lambda q, b: (q, 0, 0))
    return pl.pallas_call(
        functools.partial(_s5_kernel, rows=rows),
        grid=(nq, batch),
        in_specs=[pl.BlockSpec((None, rows, kdim), lambda q, b: (q, b, 0)),
                  per_q(kdim, nb * S5_CH), per_q(kdim, 2 * S5_STATE), per_q(ncol, nb * S5_CH),
                  full(e_sc.shape), full(e_b.shape), per_q(2, ncol // 2)],
        out_specs=pl.BlockSpec((None, rows, kdim), lambda q, b: (q, b, 0)),
        out_shape=jax.ShapeDtypeStruct((nq, rows_all, kdim), F32),
        scratch_shapes=[pltpu.VMEM((kdim, kdim), BF16), pltpu.VMEM((kdim, ncol), BF16), pltpu.VMEM((ncol, kdim), BF16),
                        pltpu.VMEM((rows, ncol), F32), pltpu.VMEM((rows, ncol), F32)],
        compiler_params=_cparams(("parallel", "arbitrary")),
        name="s5_scan",
    )(u6, tc, bc, cc, jnp.asarray(e_sc, BF16), jnp.asarray(e_b, BF16), lam16)


def _s5_operators(lam_re, lam_im, b_re, b_im, c_re, c_im, d_skip, log_dt):
    nb = S5_BLOCK
    gq = LANE // S5_CH
    nq = S5_GROUPS // gq
    lam = lax.complex(jnp.minimum(lam_re.astype(F32), S5_MAX_REAL), lam_im.astype(F32))
    step = jnp.exp(log_dt.astype(F32))[:, None]
    z = lam * step
    lam_bar = jnp.exp(z)
    b_bar = ((lam_bar - 1.0) / lam)[..., None] * lax.complex(b_re.astype(F32), b_im.astype(F32))
    c_mat = lax.complex(c_re.astype(F32), c_im.astype(F32))
    pw = jnp.exp(z[..., None] * jnp.arange(nb + 1, dtype=F32))
    kern = jnp.einsum('gop,gpd,gpi->gdio', c_mat, pw[..., :nb], b_bar, precision=HIGHEST).real
    tt = np.arange(nb)[None, :, None]
    ss = np.arange(nb)[None, None, :]
    shift = jnp.asarray((ss - tt == np.arange(nb)[:, None, None]).astype(np.float32))
    toe = jnp.einsum('dts,gdio->gtiso', shift, kern, precision=HIGHEST)
    skip = (jnp.asarray(np.eye(nb, dtype=np.float32))[None, :, None, :, None]
            * jnp.asarray(np.eye(S5_CH, dtype=np.float32))[None, None, :, None, :]
            * d_skip.astype(F32).reshape(S5_GROUPS, 1, S5_CH, 1, 1))
    tc = (toe + skip).reshape(nq, gq, nb, S5_CH, nb * S5_CH).transpose(0, 2, 1, 3, 4)
    tc = tc.reshape(nq, nb * LANE, nb * S5_CH)
    pw_rev = jnp.exp(z[..., None] * jnp.asarray(np.arange(nb - 1, -1, -1), F32))
    binc = pw_rev[:, :, :, None] * b_bar[:, :, None, :]
    binc = jnp.stack([binc.real, binc.imag], axis=0).reshape(2, nq, gq, S5_STATE, nb, S5_CH)
    bc = binc.transpose(1, 4, 2, 5, 0, 3).reshape(nq, nb * LANE, 2 * S5_STATE)
    cm = c_mat.transpose(0, 2, 1)[:, :, None, :] * pw[..., 1:][:, :, :, None]
    cm = jnp.stack([cm.real, -cm.imag], axis=0).reshape(2, nq, gq * S5_STATE, nb * S5_CH)
    cc = cm.transpose(1, 0, 2, 3).reshape(nq, 2 * gq * S5_STATE, nb * S5_CH)
    lam_n = pw[..., nb].reshape(nq, gq * S5_STATE)
    lam16 = jnp.stack([lam_n.real, lam_n.imag], axis=1)
    return tc.astype(BF16), bc.astype(BF16), cc.astype(BF16), lam16


def _s5(p, batch, seq, ops):
    nb = S5_BLOCK
    nq = BRANCH_W // LANE
    u = p[:, COL_S5:COL_S5 + BRANCH_W].astype(BF16)
    u6 = u.reshape(batch, seq // nb, nb, nq, LANE).transpose(3, 0, 1, 2, 4).reshape(nq, batch * seq // nb, nb * LANE)
    y6 = _s5_scan(u6, *ops, batch)
    return y6.reshape(nq, batch, seq // nb, nb, LANE).transpose(1, 2, 3, 0, 4).reshape(batch * seq, BRANCH_W)


def _merge_kernel(x_ref, sc_ref, sh_ref, gm_ref, ys5_ref, yhg_ref, yret_ref, ym2_ref,
                  wglu_ref, wbr_ref, wg_ref, bg_ref, wout_ref, o_ref):
    x = x_ref[...]
    d = x.shape[1]
    h = _modulated_norm(x, sc_ref[...], sh_ref[...]).astype(BF16)
    y_s5 = jax.nn.gelu(ys5_ref[...])
    y_s5 = y_s5 * jax.nn.sigmoid(jnp.dot(y_s5.astype(BF16), wglu_ref[...], preferred_element_type=F32))
    acc = jnp.zeros(x.shape, F32)
    for n, y in enumerate((y_s5, yhg_ref[...], yret_ref[...], ym2_ref[...])):
        gate = jax.nn.sigmoid(jnp.dot(h, wg_ref[:, n * d:(n + 1) * d], preferred_element_type=F32)
                              + bg_ref[:, n * d:(n + 1) * d])
        acc = acc + gate * jnp.dot(y.astype(BF16), wbr_ref[n], preferred_element_type=F32)
    o_ref[...] = x + gm_ref[...] * jnp.dot(acc.astype(BF16), wout_ref[...], preferred_element_type=F32)


def _merge(x2, mod3, ys5, yhg, yret, ym2, w_glu, w_branch, w_gate, b_gate, w_out, seq):
    t, d = x2.shape
    tm = TM_MERGE
    tpb = seq // tm
    const = lambda shape: pl.BlockSpec(shape, lambda i: (0,) * len(shape))
    modspec = lambda k: pl.BlockSpec((None, 1, d), lambda i: ((i // tpb) * 6 + k, 0, 0))
    yspec = pl.BlockSpec((tm, BRANCH_W), lambda i: (i, 0))
    return pl.pallas_call(
        _merge_kernel,
        grid=(t // tm,),
        in_specs=[pl.BlockSpec((tm, d), lambda i: (i, 0)), modspec(1), modspec(0), modspec(2),
                  yspec, yspec, yspec, yspec,
                  const((BRANCH_W, BRANCH_W)), const((4, BRANCH_W, d)), const((d, 4 * d)), const((1, 4 * d)),
                  const((d, d))],
        out_specs=pl.BlockSpec((tm, d), lambda i: (i, 0)),
        out_shape=jax.ShapeDtypeStruct((t, d), F32),
        compiler_params=_cparams(("parallel",)),
        name="merge",
    )(x2, mod3, mod3, mod3, ys5, yhg, yret, ym2, w_glu, w_branch, w_gate, b_gate, w_out)


def _router_kernel(x_ref, sc_ref, sh_ref, wr_ref, br_ref, tri_ref, h_ref, ids_ref, wts_ref, cnt_ref, carry):
    i = pl.program_id(0)

    @pl.when(i == 0)
    def _():
        carry[...] = jnp.zeros_like(carry)

    h = _modulated_norm(x_ref[...], sc_ref[...], sh_ref[...])
    tm, d = h.shape
    h_ref[...] = h
    logits = _dot_nt(wr_ref[...], h, precision=HIGHEST) + br_ref[:, 0:1]
    gl = [logits[g:g + 1, :] for g in range(MOE_GROUPS)]
    gmax = gl[0]
    gsel = jnp.zeros((1, tm), jnp.int32)
    for g in range(1, MOE_GROUPS):
        better = gl[g] > gmax
        gsel = jnp.where(better, g, gsel)
        gmax = jnp.where(better, gl[g], gmax)
    gden = gl[0] * 0.0
    for g in range(MOE_GROUPS):
        gden = gden + jnp.exp(gl[g] - gmax)
    g_w = 1.0 / gden
    el = []
    for e in range(MOE_EPG):
        v = logits[MOE_GROUPS + e:MOE_GROUPS + e + 1, :]
        for g in range(1, MOE_GROUPS):
            row = MOE_GROUPS + g * MOE_EPG + e
            v = jnp.where(gsel == g, logits[row:row + 1, :], v)
        el.append(v)
    v1 = el[0]
    i1 = jnp.zeros((1, tm), jnp.int32)
    for e in range(1, MOE_EPG):
        better = el[e] > v1
        i1 = jnp.where(better, e, i1)
        v1 = jnp.where(better, el[e], v1)
    v2 = jnp.full((1, tm), -jnp.inf, F32)
    i2 = jnp.zeros((1, tm), jnp.int32)
    for e in range(MOE_EPG):
        better = (el[e] > v2) & (i1 != e)
        i2 = jnp.where(better, e, i2)
        v2 = jnp.where(better, el[e], v2)
    ex = jnp.exp(v2 - v1)
    p1 = 1.0 / (1.0 + ex)
    e1 = gsel * MOE_EPG + i1
    e2 = gsel * MOE_EPG + i2
    erow = lax.broadcasted_iota(jnp.int32, (MOE_EXPERTS, tm), 0)
    oh1 = (erow == e1).astype(F32)
    oh2 = (erow == e2).astype(F32)
    both = oh1 + oh2
    prefix = jnp.dot(both.astype(BF16), tri_ref[...], preferred_element_type=F32) + carry[:, 0:1]
    rank1 = jnp.sum(oh1 * prefix, axis=0, keepdims=True).astype(jnp.int32)
    rank2 = jnp.sum(oh2 * prefix, axis=0, keepdims=True).astype(jnp.int32)
    carry[...] = carry[...] + jnp.sum(both, axis=1, keepdims=True)
    zi = jnp.zeros((1, tm), jnp.int32)
    ids_ref[...] = jnp.concatenate([e1, e2, rank1, rank2, zi, zi, zi, zi], axis=0)
    wrow = lax.broadcasted_iota(jnp.int32, (LANE, tm), 0)
    wts_ref[...] = jnp.where(wrow == 0, p1 * g_w, jnp.where(wrow == 1, ex * p1 * g_w, 0.0)).T
    cnt_ref[...] = carry[...]


def _router(x2, mod3, w_route, b_route, tri_excl, seq):
    t, d = x2.shape
    tm = TM_PROJ
    tpb = seq // tm
    nr = w_route.shape[0]
    const = lambda shape: pl.BlockSpec(shape, lambda i: (0,) * len(shape))
    modspec = lambda k: pl.BlockSpec((None, 1, d), lambda i: ((i // tpb) * 6 + k, 0, 0))
    return pl.pallas_call(
        _router_kernel,
        grid=(t // tm,),
        in_specs=[pl.BlockSpec((tm, d), lambda i: (i, 0)), modspec(4), modspec(3),
                  const((nr, d)), const((nr, LANE)), const((tm, tm))],
        out_specs=[pl.BlockSpec((tm, d), lambda i: (i, 0)),
                   pl.BlockSpec((8, tm), lambda i: (0, i)),
                   pl.BlockSpec((tm, LANE), lambda i: (i, 0)),
                   const((MOE_EXPERTS, LANE))],
        out_shape=[jax.ShapeDtypeStruct((t, d), F32),
                   jax.ShapeDtypeStruct((8, t), jnp.int32),
                   jax.ShapeDtypeStruct((t, LANE), F32),
                   jax.ShapeDtypeStruct((MOE_EXPERTS, LANE), F32)],
        scratch_shapes=[pltpu.VMEM((MOE_EXPERTS, LANE), F32)],
        compiler_params=_cparams(("arbitrary",)),
        name="moe_router",
    )(x2, mod3, mod3, w_route, b_route, tri_excl)


def _dispatch_kernel(s1_ref, s2_ref, h_ref, zero_hbm, xs_hbm, sem, *, tm):
    del zero_hbm

    def row_copy(r, slot):
        return pltpu.make_async_copy(h_ref.at[pl.ds(r, 1), :], xs_hbm.at[pl.ds(slot, 1), :], sem)

    def issue(r, c):
        row_copy(r, s1_ref[0, r]).start()
        row_copy(r, s2_ref[0, r]).start()
        return c

    lax.fori_loop(0, tm, issue, 0)

    def drain(r, c):
        row_copy(r, s1_ref[0, r]).wait()
        row_copy(r, s2_ref[0, r]).wait()
        return c

    lax.fori_loop(0, tm, drain, 0)


def _dispatch(slot1, slot2, h3, zeros3):
    t, d = h3.shape
    tm = TM_DISP
    smem = lambda: pl.BlockSpec((None, 1, tm), lambda i: (i, 0, 0), memory_space=pltpu.SMEM)
    return pl.pallas_call(
        functools.partial(_dispatch_kernel, tm=tm),
        grid=(t // tm,),
        in_specs=[smem(), smem(), pl.BlockSpec((tm, d), lambda i: (i, 0)),
                  pl.BlockSpec(memory_space=pl.ANY)],
        out_specs=pl.BlockSpec(memory_space=pl.ANY),
        out_shape=jax.ShapeDtypeStruct(zeros3.shape, zeros3.dtype),
        scratch_shapes=[pltpu.SemaphoreType.DMA(())],
        input_output_aliases={3: 0},
        compiler_params=_cparams(("arbitrary",)),
        name="moe_dispatch",
    )(slot1.reshape(t // tm, 1, tm), slot2.reshape(t // tm, 1, tm), h3, zeros3)


def _expert_kernel(texp_ref, xs_ref, w1_ref, w3_ref, w2_ref, ys_ref):
    del texp_ref
    x = xs_ref[...].astype(BF16)
    a = jnp.dot(x, w1_ref[...], preferred_element_type=F32)
    b = jnp.dot(x, w3_ref[...], preferred_element_type=F32)
    act = _silu(a) * b
    ys_ref[...] = jnp.dot(act.astype(BF16), w2_ref[...], preferred_element_type=F32)


def _experts(tile_expert, xs3, w1, w3, w2):
    ns, d = xs3.shape
    ff = w1.shape[2]
    grid_spec = pltpu.PrefetchScalarGridSpec(
        num_scalar_prefetch=1,
        grid=(ns // TM_X,),
        in_specs=[pl.BlockSpec((TM_X, d), lambda i, te: (i, 0)),
                  pl.BlockSpec((None, d, ff), lambda i, te: (te[i], 0, 0)),
                  pl.BlockSpec((None, d, ff), lambda i, te: (te[i], 0, 0)),
                  pl.BlockSpec((None, ff, d), lambda i, te: (te[i], 0, 0))],
        out_specs=pl.BlockSpec((TM_X, d), lambda i, te: (i, 0)),
    )
    return pl.pallas_call(
        _expert_kernel,
        grid_spec=grid_spec,
        out_shape=jax.ShapeDtypeStruct((ns, d), F32),
        compiler_params=_cparams(("parallel",)),
        name="moe_experts",
    )(tile_expert, xs3, w1, w3, w2)


def _combine_kernel(s1_ref, s2_ref, x_ref, gate_ref, fw_ref, wcol_ref, ys_hbm, o_ref, buf, sem, *, tm, final):
    def row_copy(which, r, slot):
        return pltpu.make_async_copy(ys_hbm.at[pl.ds(slot, 1), :], buf.at[which, pl.ds(r, 1), :], sem)

    def issue(r, c):
        row_copy(0, r, s1_ref[0, r]).start()
        row_copy(1, r, s2_ref[0, r]).start()
        return c

    lax.fori_loop(0, tm, issue, 0)

    def drain(r, c):
        row_copy(0, r, s1_ref[0, r]).wait()
        row_copy(1, r, s2_ref[0, r]).wait()
        return c

    lax.fori_loop(0, tm, drain, 0)
    moe = wcol_ref[:, 0:1] * buf[0] + wcol_ref[:, 1:2] * buf[1]
    x = x_ref[...] + gate_ref[...] * moe
    if final:
        x = x * lax.rsqrt(jnp.mean(x * x, axis=-1, keepdims=True) + EPS) * fw_ref[...]
    o_ref[...] = x


def _combine(slot1, slot2, x2, mod3, final_w_row, wcol, ys3, seq, final):
    t, d = x2.shape
    tm = TM_COMB
    tpb = seq // tm
    smem = lambda: pl.BlockSpec((None, 1, tm), lambda i: (i, 0, 0), memory_space=pltpu.SMEM)
    return pl.pallas_call(
        functools.partial(_combine_kernel, tm=tm, final=final),
        grid=(t // tm,),
        in_specs=[smem(), smem(), pl.BlockSpec((tm, d), lambda i: (i, 0)),
                  pl.BlockSpec((None, 1, d), lambda i: ((i // tpb) * 6 + 5, 0, 0)),
                  pl.BlockSpec((1, d), lambda i: (0, 0)),
                  pl.BlockSpec((tm, LANE), lambda i: (i, 0)),
                  pl.BlockSpec(memory_space=pl.ANY)],
        out_specs=pl.BlockSpec((tm, d), lambda i: (i, 0)),
        out_shape=jax.ShapeDtypeStruct((t, d), F32),
        scratch_shapes=[pltpu.VMEM((2, tm, d), F32), pltpu.SemaphoreType.DMA(())],
        compiler_params=_cparams(("arbitrary",)),
        name="moe_combine",
    )(slot1.reshape(t // tm, 1, tm), slot2.reshape(t // tm, 1, tm), x2, mod3, final_w_row, wcol, ys3)


def _moe(x2, mod3, final_w_row, w_route, b_route, tri_excl, w1, w3, w2, seq, final):
    t, d = x2.shape
    h3, ids, wcol, counts = _router(x2, mod3, w_route, b_route, tri_excl, seq)
    cnt = counts[:, 0].astype(jnp.int32)
    padded = (cnt + TM_X - 1) // TM_X * TM_X
    ends = jnp.cumsum(padded)
    offs = ends - padded
    slot1 = offs[ids[0]] + ids[2]
    slot2 = offs[ids[1]] + ids[3]
    ns = 2 * t + MOE_EXPERTS * TM_X
    tile_start = jnp.arange(ns // TM_X, dtype=jnp.int32) * TM_X
    tile_expert = jnp.minimum(jnp.sum(tile_start[:, None] >= ends[None, :], axis=1), MOE_EXPERTS - 1).astype(jnp.int32)
    xs3 = _dispatch(slot1, slot2, h3, jnp.zeros((ns, d), F32))
    ys3 = _experts(tile_expert, xs3, w1, w3, w2)
    return _combine(slot1, slot2, x2, mod3, final_w_row, wcol, ys3, seq, final)


def kernel(x, c, positions, ada_w, ada_b, w_in, s5_lam_re, s5_lam_im, s5_b_re, s5_b_im, s5_c_re, s5_c_im, s5_d, s5_log_dt, s5_w_glu, hg_lb_logits, hg_norm_w, m2_conv_w, m2_conv_b, m2_dt_bias, m2_a_log, m2_d, m2_norm_w, w_branch, w_gate, b_gate, w_out, moe_w_group, moe_b_group, moe_w_expert, moe_b_expert, moe_w1, moe_w3, moe_w2, final_norm_w):
    bsz, seq, d = x.shape
    t = bsz * seq
    depth = ada_w.shape[0]
    assert seq % TM_PROJ == 0 and seq % C_RET == 0 and seq % C_SSD == 0 and seq % C_HG == 0
    x2 = x.reshape(t, d).astype(F32)

    c_pad = jnp.zeros((8, d), F32).at[:bsz].set(c.astype(F32))
    mod_all = _ada_mod(c_pad, ada_w.astype(F32), ada_b.astype(F32))

    half = RET_DK // 2
    inv_freq = ROPE_BASE ** (-jnp.arange(half, dtype=F32) / half)
    invf_row = jnp.tile(inv_freq, 2 * RET_HEADS)[None, :]
    sgn_row = jnp.asarray(np.tile(np.concatenate([-np.ones(half), np.ones(half)]), RET_HEADS)[None, :], F32)
    cos_t, sin_t = _rope_tables(positions.reshape(t, 1).astype(jnp.int32), invf_row, sgn_row)
    cos3 = cos_t.reshape(bsz, seq, -1)
    sin3 = sin_t.reshape(bsz, seq, -1)

    lb_cum = jnp.cumsum(jax.nn.softmax(hg_lb_logits.astype(F32), axis=0), axis=0)
    hg_lb = lb_cum - lb_cum[:1]
    tri_ssd = jnp.asarray(np.tril(np.ones((C_SSD, C_SSD), np.float32)))
    tri_excl = jnp.asarray(np.triu(np.ones((TM_PROJ, TM_PROJ), np.float32), 1), BF16)
    final_w_row = final_norm_w.astype(F32)[None, :]

    for layer in range(depth):
        mod3 = mod_all[layer, :bsz].reshape(bsz * 6, 1, d)
        w_pad = jnp.zeros((d, IN_W_PAD), BF16).at[:, :IN_W].set(w_in[layer].astype(BF16))
        p = _in_proj(x2, mod3, w_pad, seq)
        p3 = p.reshape(bsz, seq, IN_W_PAD)

        ops = _s5_operators(s5_lam_re[layer], s5_lam_im[layer], s5_b_re[layer], s5_b_im[layer],
                            s5_c_re[layer], s5_c_im[layer], s5_d[layer], s5_log_dt[layer])
        y_s5 = _s5(p, bsz, seq, ops)

        lb = hg_lb[layer][None, :]
        y_hg = _hgrn2(p3, jnp.log(lb), jnp.log1p(-lb), hg_norm_w[layer].astype(F32)[None, :]).reshape(t, BRANCH_W)

        y_ret = _retention(p3, cos3, sin3).reshape(t, BRANCH_W)

        pad8 = lambda v: jnp.zeros((1, LANE), F32).at[0, :M2_HEADS].set(v.astype(F32))
        y_m2 = _ssd(p3, tri_ssd, m2_conv_w[layer].astype(F32), m2_conv_b[layer].astype(F32)[None, :],
                    pad8(m2_dt_bias[layer]), pad8(m2_a_log[layer]),
                    jnp.repeat(m2_d[layer].astype(F32), M2_HEADDIM)[None, :],
                    m2_norm_w[layer].astype(F32)[None, :]).reshape(t, BRANCH_W)

        x2 = _merge(x2, mod3, y_s5, y_hg, y_ret, y_m2, s5_w_glu[layer].astype(BF16),
                    w_branch[layer].astype(BF16), w_gate[layer].astype(BF16), b_gate[layer].astype(F32)[None, :],
                    w_out[layer].astype(BF16), seq)

        nr = 40
        w_route = jnp.zeros((nr, d), F32).at[:MOE_GROUPS].set(moe_w_group[layer].astype(F32).T)
        w_route = w_route.at[MOE_GROUPS:MOE_GROUPS + MOE_EXPERTS].set(moe_w_expert[layer].astype(F32).T)
        b_route = jnp.zeros((nr, LANE), F32).at[:MOE_GROUPS, 0].set(moe_b_group[layer].astype(F32))
        b_route = b_route.at[MOE_GROUPS:MOE_GROUPS + MOE_EXPERTS, 0].set(moe_b_expert[layer].astype(F32))
        x2 = _moe(x2, mod3, final_w_row, w_route, b_route, tri_excl,
                  moe_w1[layer].reshape(MOE_EXPERTS, d, MOE_FF).astype(BF16),
                  moe_w3[layer].reshape(MOE_EXPERTS, d, MOE_FF).astype(BF16),
                  moe_w2[layer].reshape(MOE_EXPERTS, MOE_FF, d).astype(BF16),
                  seq, final=(layer == depth - 1))
    return x2.reshape(bsz, seq, d)
```

```python
import functools
import math

import numpy as np
import jax
import jax.numpy as jnp
from jax import lax
from jax.experimental import pallas as pl
from jax.experimental.pallas import tpu as pltpu

F32 = jnp.float32
BF16 = jnp.bfloat16
HIGHEST = lax.Precision.HIGHEST

D_MODEL = 1024
DEPTH = 2
BRANCH_W = 512
EPS = 1e-6
S5_GROUPS = 32
S5_CH = 16
S5_STATE = 64
S5_MAX_REAL = -1e-4
S5_BLOCK = 16
HG_HEADS = 4
HG_DK = 128
RET_HEADS = 4
RET_DK = 64
RET_DV = 128
ROPE_BASE = 10000.0
M2_HEADS = 8
M2_HEADDIM = 64
M2_GROUPS = 2
M2_STATE = 128
M2_CONV = 4
MOE_GROUPS = 4
MOE_EPG = 8
MOE_EXPERTS = MOE_GROUPS * MOE_EPG
MOE_FF = 256

COL_S5, COL_HQ, COL_HF, COL_HI, COL_HG = 0, 512, 1024, 1536, 2048
COL_RQ, COL_RK, COL_RV, COL_RG = 2560, 2816, 3072, 3584
COL_MZ, COL_MXS, COL_MBC, COL_MDT = 4096, 4608, 5120, 5632
IN_W = 5640
IN_W_PAD = 5760

LANE = 128
VMEM_LIMIT = 56 * 1024 * 1024

TM_PROJ = 1024
TN_PROJ = 1152
TM_MERGE = 512
C_RET = 256
C_SSD = 256
C_HG = 128
TM_X = 256
TM_COMB = 512
TM_DISP = 512


def _cparams(sem):
    return pltpu.CompilerParams(dimension_semantics=sem, vmem_limit_bytes=VMEM_LIMIT)


def _silu(v):
    return v * jax.nn.sigmoid(v)


def _dot_nt(a, b, **kw):
    return lax.dot_general(a, b, (((1,), (1,)), ((), ())), preferred_element_type=F32, **kw)


def _dot_tn(a, b, **kw):
    return lax.dot_general(a, b, (((0,), (0,)), ((), ())), preferred_element_type=F32, **kw)


def _ada_kernel(c_ref, w_ref, b_ref, o_ref):
    cond = _silu(c_ref[...])
    o_ref[...] = jnp.dot(cond, w_ref[...], preferred_element_type=F32, precision=HIGHEST) + b_ref[...]


def _ada_mod(c_pad, ada_w, ada_b):
    depth, d, n = ada_w.shape
    tn = 1536
    return pl.pallas_call(
        _ada_kernel,
        grid=(depth, n // tn),
        in_specs=[pl.BlockSpec((8, d), lambda l, j: (0, 0)),
                  pl.BlockSpec((None, d, tn), lambda l, j: (l, 0, j)),
                  pl.BlockSpec((None, 1, tn), lambda l, j: (l, 0, j))],
        out_specs=pl.BlockSpec((None, 8, tn), lambda l, j: (l, 0, j)),
        out_shape=jax.ShapeDtypeStruct((depth, 8, n), F32),
        compiler_params=_cparams(("parallel", "parallel")),
        name="ada_mod",
    )(c_pad, ada_w, ada_b.reshape(depth, 1, n))


def _modulated_norm(x, scale, shift):
    ms = jnp.mean(x * x, axis=-1, keepdims=True)
    return x * lax.rsqrt(ms + EPS) * (1.0 + scale) + shift


def _inproj_kernel(x_ref, sc_ref, sh_ref, w_ref, o_ref, h_scr):
    @pl.when(pl.program_id(1) == 0)
    def _():
        h_scr[...] = _modulated_norm(x_ref[...], sc_ref[...], sh_ref[...]).astype(BF16)

    o_ref[...] = jnp.dot(h_scr[...], w_ref[...], preferred_element_type=F32)


def _in_proj(x2, mod3, w_pad, layer, seq):
    t, d = x2.shape
    tpb = seq // TM_PROJ
    return pl.pallas_call(
        _inproj_kernel,
        grid=(t // TM_PROJ, IN_W_PAD // TN_PROJ),
        in_specs=[pl.BlockSpec((TM_PROJ, d), lambda i, j: (i, 0)),
                  pl.BlockSpec((None, 1, d), lambda i, j: ((i // tpb) * 6 + 1, 0, 0)),
                  pl.BlockSpec((None, 1, d), lambda i, j: ((i // tpb) * 6 + 0, 0, 0)),
                  pl.BlockSpec((None, d, TN_PROJ), lambda i, j: (layer, 0, j))],
        out_specs=pl.BlockSpec((TM_PROJ, TN_PROJ), lambda i, j: (i, j)),
        out_shape=jax.ShapeDtypeStruct((t, IN_W_PAD), F32),
        scratch_shapes=[pltpu.VMEM((TM_PROJ, d), BF16)],
        compiler_params=_cparams(("parallel", "arbitrary")),
        name="in_proj",
    )(x2, mod3, mod3, w_pad)


def _rope_kernel(pos_ref, invf_ref, sgn_ref, cos_ref, sin_ref):
    ang = pos_ref[...].astype(F32) * invf_ref[...]
    cos_ref[...] = jnp.cos(ang)
    sin_ref[...] = jnp.sin(ang) * sgn_ref[...]


def _rope_tables(pos_col, invf_row, sgn_row):
    t = pos_col.shape[0]
    w = invf_row.shape[1]
    tm = 1024
    return pl.pallas_call(
        _rope_kernel,
        grid=(t // tm,),
        in_specs=[pl.BlockSpec((tm, 1), lambda i: (i, 0)),
                  pl.BlockSpec((1, w), lambda i: (0, 0)),
                  pl.BlockSpec((1, w), lambda i: (0, 0))],
        out_specs=[pl.BlockSpec((tm, w), lambda i: (i, 0))] * 2,
        out_shape=[jax.ShapeDtypeStruct((t, w), F32)] * 2,
        compiler_params=_cparams(("parallel",)),
        name="rope_tables",
    )(pos_col, invf_row, sgn_row)


def _ret_kernel(q_ref, k_ref, v_ref, g_ref, cos_ref, sin_ref, o_ref, st_ref, *, chunk):
    @pl.when(pl.program_id(1) == 0)
    def _():
        st_ref[...] = jnp.zeros_like(st_ref)

    cosf = cos_ref[...]
    sinf = sin_ref[...]
    width = RET_HEADS * RET_DK
    lane = lax.broadcasted_iota(jnp.int32, (chunk, width), 1)
    first_half = (lane % RET_DK) < (RET_DK // 2)

    def rope(t):
        partner = jnp.where(first_half, pltpu.roll(t, width - RET_DK // 2, 1), pltpu.roll(t, RET_DK // 2, 1))
        return t * cosf + partner * sinf

    q = rope(q_ref[...])
    k = rope(k_ref[...]) * (RET_DK ** -0.5)
    v = v_ref[...]
    g = g_ref[...]
    ti = lax.broadcasted_iota(jnp.int32, (chunk, chunk), 0)
    si = lax.broadcasted_iota(jnp.int32, (chunk, chunk), 1)
    lag = (ti - si).astype(F32)
    tcol = lax.broadcasted_iota(jnp.int32, (chunk, 1), 0).astype(F32)
    for h in range(RET_HEADS):
        log_gamma = math.log1p(-(2.0 ** (-5.0 - h)))
        qh = q[:, h * RET_DK:(h + 1) * RET_DK]
        kh = k[:, h * RET_DK:(h + 1) * RET_DK]
        vh = v[:, h * RET_DV:(h + 1) * RET_DV].astype(BF16)
        decay = jnp.where(ti >= si, jnp.exp(jnp.minimum(lag * log_gamma, 0.0)), 0.0)
        scores = _dot_nt(qh.astype(BF16), kh.astype(BF16)) * decay
        state = st_ref[h]
        q_in = qh * jnp.exp(log_gamma * (tcol + 1.0))
        o = (jnp.dot(scores.astype(BF16), vh, preferred_element_type=F32)
             + jnp.dot(q_in.astype(BF16), state.astype(BF16), preferred_element_type=F32))
        k_out = kh * jnp.exp(log_gamma * (chunk - 1.0 - tcol))
        st_ref[h] = math.exp(log_gamma * chunk) * state + _dot_tn(k_out.astype(BF16), vh)
        o = o * lax.rsqrt(jnp.mean(o * o, axis=-1, keepdims=True) + EPS)
        gh = g[:, h * RET_DV:(h + 1) * RET_DV]
        o_ref[:, h * RET_DV:(h + 1) * RET_DV] = o * _silu(gh)


def _retention(p3, cos3, sin3):
    b, seq, _ = p3.shape
    c = C_RET
    qk_w = RET_HEADS * RET_DK
    return pl.pallas_call(
        functools.partial(_ret_kernel, chunk=c),
        grid=(b, seq // c),
        in_specs=[pl.BlockSpec((None, c, qk_w), lambda i, j: (i, j, COL_RQ // qk_w)),
                  pl.BlockSpec((None, c, qk_w), lambda i, j: (i, j, COL_RK // qk_w)),
                  pl.BlockSpec((None, c, BRANCH_W), lambda i, j: (i, j, COL_RV // BRANCH_W)),
                  pl.BlockSpec((None, c, BRANCH_W), lambda i, j: (i, j, COL_RG // BRANCH_W)),
                  pl.BlockSpec((None, c, qk_w), lambda i, j: (i, j, 0)),
                  pl.BlockSpec((None, c, qk_w), lambda i, j: (i, j, 0))],
        out_specs=pl.BlockSpec((None, c, BRANCH_W), lambda i, j: (i, j, 0)),
        out_shape=jax.ShapeDtypeStruct((b, seq, BRANCH_W), F32),
        scratch_shapes=[pltpu.VMEM((RET_HEADS, RET_DK, RET_DV), F32)],
        compiler_params=_cparams(("parallel", "arbitrary")),
        name="retention",
    )(p3, p3, p3, p3, cos3, sin3)


def _ssd_kernel(z_ref, xs_ref, bc_ref, dt_ref, tri_ref, cw_ref, cb_ref, dtb_ref, alog_ref, dsk_ref, nw_ref,
                o_ref, xe_scr, st_ref, *, chunk):
    j = pl.program_id(1)
    width = 2 * BRANCH_W

    @pl.when(j == 0)
    def _():
        st_ref[...] = jnp.zeros_like(st_ref)
        xe_scr[0:8, :] = jnp.zeros((8, width), F32)

    @pl.when(j > 0)
    def _():
        xe_scr[0:8, :] = xe_scr[chunk:chunk + 8, :]

    xe_scr[8:, 0:BRANCH_W] = xs_ref[...]
    xe_scr[8:, BRANCH_W:] = bc_ref[...]
    conv = cb_ref[...] + cw_ref[M2_CONV - 1:M2_CONV, :] * xe_scr[8:, :]
    for tap in range(M2_CONV - 1):
        conv = conv + cw_ref[tap:tap + 1, :] * xe_scr[pl.ds(8 - (M2_CONV - 1) + tap, chunk), :]
    conv = _silu(conv)
    xs = conv[:, :BRANCH_W]
    bm = conv[:, BRANCH_W:BRANCH_W + M2_GROUPS * M2_STATE]
    cm = conv[:, BRANCH_W + M2_GROUPS * M2_STATE:]

    dt = jax.nn.softplus(dt_ref[...] + dtb_ref[...])
    da = dt * (-jnp.exp(alog_ref[...]))
    a_cs = jnp.dot(tri_ref[...], da, preferred_element_type=F32, precision=HIGHEST)
    a_cs_t = a_cs.T
    ti = lax.broadcasted_iota(jnp.int32, (chunk, chunk), 0)
    si = lax.broadcasted_iota(jnp.int32, (chunk, chunk), 1)
    causal = ti >= si
    hpg = M2_HEADS // M2_GROUPS
    ys = []
    for grp in range(M2_GROUPS):
        bm_g = bm[:, grp * M2_STATE:(grp + 1) * M2_STATE]
        cm_g = cm[:, grp * M2_STATE:(grp + 1) * M2_STATE]
        cb = _dot_nt(cm_g.astype(BF16), bm_g.astype(BF16))
        for hh in range(hpg):
            h = grp * hpg + hh
            col = a_cs[:, h:h + 1]
            row = a_cs_t[h:h + 1, :]
            lmat = jnp.where(causal, jnp.exp(jnp.minimum(col - row, 0.0)), 0.0)
            xd = xs[:, h * M2_HEADDIM:(h + 1) * M2_HEADDIM] * dt[:, h:h + 1]
            state = st_ref[h]
            y = (jnp.dot((cb * lmat).astype(BF16), xd.astype(BF16), preferred_element_type=F32)
                 + jnp.dot((cm_g * jnp.exp(col)).astype(BF16), state.astype(BF16), preferred_element_type=F32))
            a_last = a_cs[chunk - 1:chunk, h:h + 1]
            to_end = jnp.exp(a_last - col)
            st_ref[h] = jnp.exp(a_last) * state + _dot_tn(bm_g.astype(BF16), (xd * to_end).astype(BF16))
            ys.append(y)
    y = jnp.concatenate(ys, axis=-1) + dsk_ref[...] * xs
    y = y * _silu(z_ref[...])
    o_ref[...] = y * lax.rsqrt(jnp.mean(y * y, axis=-1, keepdims=True) + EPS) * nw_ref[...]


def _ssd(p3, tri, conv_w, conv_b, dt_bias_row, a_log_row, d_skip_row, norm_w_row):
    b, seq, _ = p3.shape
    c = C_SSD
    const = lambda shape: pl.BlockSpec(shape, lambda i, j: (0,) * len(shape))
    return pl.pallas_call(
        functools.partial(_ssd_kernel, chunk=c),
        grid=(b, seq // c),
        in_specs=[pl.BlockSpec((None, c, BRANCH_W), lambda i, j: (i, j, COL_MZ // BRANCH_W)),
                  pl.BlockSpec((None, c, BRANCH_W), lambda i, j: (i, j, COL_MXS // BRANCH_W)),
                  pl.BlockSpec((None, c, BRANCH_W), lambda i, j: (i, j, COL_MBC // BRANCH_W)),
                  pl.BlockSpec((None, c, LANE), lambda i, j: (i, j, COL_MDT // LANE)),
                  const((c, c)), const((M2_CONV, 2 * BRANCH_W)), const((1, 2 * BRANCH_W)),
                  const((1, LANE)), const((1, LANE)), const((1, BRANCH_W)), const((1, BRANCH_W))],
        out_specs=pl.BlockSpec((None, c, BRANCH_W), lambda i, j: (i, j, 0)),
        out_shape=jax.ShapeDtypeStruct((b, seq, BRANCH_W), F32),
        scratch_shapes=[pltpu.VMEM((c + 8, 2 * BRANCH_W), F32),
                        pltpu.VMEM((M2_HEADS, M2_STATE, M2_HEADDIM), F32)],
        compiler_params=_cparams(("parallel", "arbitrary")),
        name="ssd",
    )(p3, p3, p3, p3, tri, conv_w, conv_b, dt_bias_row, a_log_row, d_skip_row, norm_w_row)


def _hg_tables(chunk):
    n_lev = int(math.log2(chunk))
    r = np.arange(chunk)[:, None]
    jj = np.arange(chunk)[None, :]
    mats = [(jj <= r), (jj > r)]
    for lev in range(n_lev):
        m = 1 << lev
        mid = (r >> (lev + 1) << (lev + 1)) + m - 1
        second = ((r >> lev) & 1) == 1
        mats.append(np.where(second, (jj > mid) & (jj <= r), (jj > r) & (jj <= mid)))
    summat = np.concatenate(mats, axis=0).astype(np.float32)
    x = r ^ jj
    levmap = np.where(r > jj, np.floor(np.log2(x + 0.5)), np.where(r == jj, -1, -2)).astype(np.int32)
    return summat, levmap, n_lev


def _hg_kernel(q_ref, f_ref, i_ref, g_ref, sum_ref, lev_ref, llb_ref, l1m_ref, nw_ref, o_ref, st_ref,
               *, chunk, n_lev):
    @pl.when(pl.program_id(1) == 0)
    def _():
        st_ref[...] = jnp.zeros_like(st_ref)

    f = f_ref[...]
    log_sig = jnp.minimum(f, 0.0) - jnp.log1p(jnp.exp(-jnp.abs(f)))
    a = llb_ref[...]
    bb = l1m_ref[...] + log_sig
    log_f = jnp.maximum(a, bb) + jnp.log1p(jnp.exp(-jnp.abs(a - bb)))
    k_all = jnp.exp(l1m_ref[...]) * jax.nn.sigmoid(-f)
    q_all = _silu(q_ref[...])
    hi = log_f.astype(BF16)
    r1 = log_f - hi.astype(F32)
    mid = r1.astype(BF16)
    lo = (r1 - mid.astype(F32)).astype(BF16)
    summat = sum_ref[...]
    sums = (jnp.dot(summat, hi, preferred_element_type=F32)
            + jnp.dot(summat, mid, preferred_element_type=F32)
            + jnp.dot(summat, lo, preferred_element_type=F32))
    levmap = lev_ref[...]
    v_all = i_ref[...]
    g_all = g_ref[...]
    for h in range(HG_HEADS):
        sl = slice(h * HG_DK, (h + 1) * HG_DK)
        qh = q_all[:, sl]
        kh = k_all[:, sl]
        vh = v_all[:, sl].astype(BF16)
        b_h = sums[0:chunk, sl]
        to_end = sums[chunk:2 * chunk, sl]
        amat = jnp.where(levmap == -1, _dot_nt(qh.astype(BF16), kh.astype(BF16)), 0.0)
        for lev in range(n_lev):
            e = jnp.exp(sums[(2 + lev) * chunk:(3 + lev) * chunk, sl])
            a_l = _dot_nt((qh * e).astype(BF16), (kh * e).astype(BF16))
            amat = jnp.where(levmap == lev, a_l, amat)
        state_t = st_ref[h]
        o = (jnp.dot(amat.astype(BF16), vh, preferred_element_type=F32)
             + _dot_nt((qh * jnp.exp(b_h)).astype(BF16), state_t.astype(BF16)))
        k_end = kh * jnp.exp(to_end)
        st_ref[h] = jnp.exp(b_h[chunk - 1:chunk, :]) * state_t + _dot_tn(vh, k_end.astype(BF16))
        o = o * lax.rsqrt(jnp.mean(o * o, axis=-1, keepdims=True) + EPS) * nw_ref[...]
        o_ref[:, sl] = o * _silu(g_all[:, sl])


def _hgrn2(p3, log_lb_row, log1m_lb_row, norm_w_row):
    b, seq, _ = p3.shape
    c = C_HG
    summat, levmap, n_lev = _hg_tables(c)
    const = lambda shape: pl.BlockSpec(shape, lambda i, j: (0,) * len(shape))
    blk = lambda col: pl.BlockSpec((None, c, BRANCH_W), lambda i, j: (i, j, col // BRANCH_W))
    return pl.pallas_call(
        functools.partial(_hg_kernel, chunk=c, n_lev=n_lev),
        grid=(b, seq // c),
        in_specs=[blk(COL_HQ), blk(COL_HF), blk(COL_HI), blk(COL_HG),
                  const(summat.shape), const((c, c)),
                  const((1, BRANCH_W)), const((1, BRANCH_W)), const((1, HG_DK))],
        out_specs=pl.BlockSpec((None, c, BRANCH_W), lambda i, j: (i, j, 0)),
        out_shape=jax.ShapeDtypeStruct((b, seq, BRANCH_W), F32),
        scratch_shapes=[pltpu.VMEM((HG_HEADS, HG_DK, HG_DK), F32)],
        compiler_params=_cparams(("parallel", "arbitrary")),
        name="hgrn2",
    )(p3, p3, p3, p3, jnp.asarray(summat, BF16), jnp.asarray(levmap), log_lb_row, log1m_lb_row, norm_w_row)


def _expand_block_diag(comp_ref, e_ref, dst_ref, row_div, lane_div):
    gq = LANE // S5_CH
    rows, ncols = dst_ref.shape
    step = 512
    comp = comp_ref[...]
    row_grp = (lax.broadcasted_iota(jnp.int32, (rows, step), 0) // row_div) % gq
    for c0 in range(0, ncols, step):
        lane_grp = ((lax.broadcasted_iota(jnp.int32, (rows, step), 1) + c0) // lane_div) % gq
        full = jnp.dot(comp, e_ref[:, c0:c0 + step], preferred_element_type=F32)
        dst_ref[:, c0:c0 + step] = jnp.where(row_grp == lane_grp, full, 0.0).astype(dst_ref.dtype)


def _s5_kernel(u_ref, k2_ref, bc_ref, cc_ref, esc_ref, eb_ref, lam_ref, o_ref, tc_scr, tq_ref, bq_ref, cq_ref,
               x_scr, w_scr, s_scr, *, rows):
    nb = S5_BLOCK

    @pl.when(pl.program_id(1) == 0)
    def _():
        k2 = k2_ref[...]
        lane = lax.broadcasted_iota(jnp.int32, k2.shape, 1)
        for t in range(nb):
            shifted = k2 if t == 0 else jnp.where(lane >= t * S5_CH, pltpu.roll(k2, t * S5_CH, 1), 0.0)
            tc_scr[t * LANE:(t + 1) * LANE, :] = shifted.astype(tc_scr.dtype)
        _expand_block_diag(tc_scr, esc_ref, tq_ref, S5_CH, S5_CH)
        _expand_block_diag(bc_ref, eb_ref, bq_ref, S5_CH, S5_STATE)
        _expand_block_diag(cc_ref, esc_ref, cq_ref, S5_STATE, S5_CH)

    for t in range(nb):
        x_scr[:, t * LANE:(t + 1) * LANE] = u_ref[pl.ds(t, rows, stride=nb), :].astype(x_scr.dtype)
    x = x_scr[...]
    half = w_scr.shape[1] // 2
    w_scr[...] = jnp.dot(x, bq_ref[...], preferred_element_type=F32)
    lam_re = lam_ref[0:1, :]
    lam_im = lam_ref[1:2, :]

    def body(j, carry):
        s_re, s_im = carry
        s_scr[pl.ds(j, 1), 0:half] = s_re
        s_scr[pl.ds(j, 1), half:] = s_im
        w_re = w_scr[pl.ds(j, 1), 0:half]
        w_im = w_scr[pl.ds(j, 1), half:]
        return lam_re * s_re - lam_im * s_im + w_re, lam_re * s_im + lam_im * s_re + w_im

    zero = jnp.zeros((1, half), F32)
    lax.fori_loop(0, rows, body, (zero, zero))
    y = (jnp.dot(x, tq_ref[...], preferred_element_type=F32)
         + jnp.dot(s_scr[...].astype(BF16), cq_ref[...], preferred_element_type=F32))
    for t in range(nb):
        o_ref[pl.ds(t, rows, stride=nb), :] = y[:, t * LANE:(t + 1) * LANE]


def _s5_scan(p3, k2, bc, cc, lam16):
    batch, seq, _ = p3.shape
    nb = S5_BLOCK
    nq = BRANCH_W // LANE
    rows = seq // nb
    kdim = nb * LANE
    gq = LANE // S5_CH
    ncol = 2 * gq * S5_STATE
    e_sc = (np.eye(nb)[:, None, :, None, None] * np.eye(S5_CH)[None, :, None, None, :] * np.ones((1, 1, 1, gq, 1)))
    e_sc = e_sc.reshape(nb * S5_CH, nb * gq * S5_CH)
    e_b = (np.eye(2)[:, None, :, None, None] * np.eye(S5_STATE)[None, :, None, None, :] * np.ones((1, 1, 1, gq, 1)))
    e_b = e_b.reshape(2 * S5_STATE, ncol)
    full = lambda shape: pl.BlockSpec(shape, lambda q, b: (0,) * len(shape))
    per_q = lambda r, c: pl.BlockSpec((None, r, c), lambda q, b: (q, 0, 0))
    return pl.pallas_call(
        functools.partial(_s5_kernel, rows=rows),
        grid=(nq, batch),
        in_specs=[pl.BlockSpec((None, seq, LANE), lambda q, b: (b, 0, COL_S5 // LANE + q)),
                  per_q(LANE, nb * S5_CH), per_q(kdim, 2 * S5_STATE), per_q(ncol, nb * S5_CH),
                  full(e_sc.shape), full(e_b.shape), per_q(2, ncol // 2)],
        out_specs=pl.BlockSpec((None, seq, LANE), lambda q, b: (b, 0, q)),
        out_shape=jax.ShapeDtypeStruct((batch, seq, BRANCH_W), F32),
        scratch_shapes=[pltpu.VMEM((kdim, nb * S5_CH), BF16),
                        pltpu.VMEM((kdim, kdim), BF16), pltpu.VMEM((kdim, ncol), BF16), pltpu.VMEM((ncol, kdim), BF16),
                        pltpu.VMEM((rows, kdim), BF16), pltpu.VMEM((rows, ncol), F32), pltpu.VMEM((rows, ncol), F32)],
        compiler_params=_cparams(("parallel", "arbitrary")),
        name="s5_scan",
    )(p3, k2, bc, cc, jnp.asarray(e_sc, BF16), jnp.asarray(e_b, BF16), lam16)


def _s5_operators(lam_re, lam_im, b_re, b_im, c_re, c_im, d_skip, log_dt):
    nb = S5_BLOCK
    gq = LANE // S5_CH
    nq = S5_GROUPS // gq
    lam = lax.complex(jnp.minimum(lam_re.astype(F32), S5_MAX_REAL), lam_im.astype(F32))
    step = jnp.exp(log_dt.astype(F32))[:, None]
    z = lam * step
    lam_bar = jnp.exp(z)
    b_bar = ((lam_bar - 1.0) / lam)[..., None] * lax.complex(b_re.astype(F32), b_im.astype(F32))
    c_mat = lax.complex(c_re.astype(F32), c_im.astype(F32))
    pw = jnp.exp(z[..., None] * jnp.arange(nb + 1, dtype=F32))
    kern = jnp.einsum('gop,gpd,gpi->gdio', c_mat, pw[..., :nb], b_bar, precision=HIGHEST).real
    skip = (jnp.asarray((np.arange(nb) == 0).astype(np.float32))[None, :, None, None]
            * jnp.asarray(np.eye(S5_CH, dtype=np.float32))[None, None, :, :]
            * d_skip.astype(F32).reshape(S5_GROUPS, 1, S5_CH, 1))
    k2 = (kern + skip).transpose(0, 2, 1, 3).reshape(nq, gq * S5_CH, nb * S5_CH)
    pw_rev = jnp.exp(z[..., None] * jnp.asarray(np.arange(nb - 1, -1, -1), F32))
    binc = pw_rev[:, :, :, None] * b_bar[:, :, None, :]
    binc = jnp.stack([binc.real, binc.imag], axis=0).reshape(2, nq, gq, S5_STATE, nb, S5_CH)
    bc = binc.transpose(1, 4, 2, 5, 0, 3).reshape(nq, nb * LANE, 2 * S5_STATE)
    cm = c_mat.transpose(0, 2, 1)[:, :, None, :] * pw[..., 1:][:, :, :, None]
    cm = jnp.stack([cm.real, -cm.imag], axis=0).reshape(2, nq, gq * S5_STATE, nb * S5_CH)
    cc = cm.transpose(1, 0, 2, 3).reshape(nq, 2 * gq * S5_STATE, nb * S5_CH)
    lam_n = pw[..., nb].reshape(nq, gq * S5_STATE)
    lam16 = jnp.stack([lam_n.real, lam_n.imag], axis=1)
    return k2, bc.astype(BF16), cc.astype(BF16), lam16


def _merge_kernel(x_ref, sc_ref, sh_ref, gm_ref, ys5_ref, yhg_ref, yret_ref, ym2_ref,
                  wglu_ref, wbr_ref, wg_ref, bg_ref, wout_ref, o_ref):
    x = x_ref[...]
    d = x.shape[1]
    h = _modulated_norm(x, sc_ref[...], sh_ref[...]).astype(BF16)
    y_s5 = jax.nn.gelu(ys5_ref[...])
    y_s5 = y_s5 * jax.nn.sigmoid(jnp.dot(y_s5.astype(BF16), wglu_ref[...], preferred_element_type=F32))
    acc = jnp.zeros(x.shape, F32)
    for n, y in enumerate((y_s5, yhg_ref[...], yret_ref[...], ym2_ref[...])):
        gate = jax.nn.sigmoid(jnp.dot(h, wg_ref[:, n * d:(n + 1) * d], preferred_element_type=F32)
                              + bg_ref[:, n * d:(n + 1) * d])
        acc = acc + gate * jnp.dot(y.astype(BF16), wbr_ref[n], preferred_element_type=F32)
    o_ref[...] = x + gm_ref[...] * jnp.dot(acc.astype(BF16), wout_ref[...], preferred_element_type=F32)


def _merge(x2, mod3, ys5, yhg, yret, ym2, w_glu, w_branch, w_gate, b_gate, w_out, layer, seq):
    t, d = x2.shape
    tm = TM_MERGE
    tpb = seq // tm
    const = lambda shape: pl.BlockSpec((None,) + shape, lambda i: (layer,) + (0,) * len(shape))
    modspec = lambda k: pl.BlockSpec((None, 1, d), lambda i: ((i // tpb) * 6 + k, 0, 0))
    yspec = pl.BlockSpec((tm, BRANCH_W), lambda i: (i, 0))
    return pl.pallas_call(
        _merge_kernel,
        grid=(t // tm,),
        in_specs=[pl.BlockSpec((tm, d), lambda i: (i, 0)), modspec(1), modspec(0), modspec(2),
                  yspec, yspec, yspec, yspec,
                  const((BRANCH_W, BRANCH_W)), const((4, BRANCH_W, d)), const((d, 4 * d)), const((1, 4 * d)),
                  const((d, d))],
        out_specs=pl.BlockSpec((tm, d), lambda i: (i, 0)),
        out_shape=jax.ShapeDtypeStruct((t, d), F32),
        compiler_params=_cparams(("parallel",)),
        name="merge",
    )(x2, mod3, mod3, mod3, ys5, yhg, yret, ym2, w_glu, w_branch, w_gate, b_gate, w_out)


def _router_kernel(x_ref, sc_ref, sh_ref, wr_ref, br_ref, tri_ref, h_ref, ids_ref, wts_ref, cnt_ref, carry):
    i = pl.program_id(0)

    @pl.when(i == 0)
    def _():
        carry[...] = jnp.zeros_like(carry)

    h = _modulated_norm(x_ref[...], sc_ref[...], sh_ref[...])
    tm, d = h.shape
    h_ref[...] = h
    logits = _dot_nt(wr_ref[...], h, precision=HIGHEST) + br_ref[:, 0:1]
    gl = [logits[g:g + 1, :] for g in range(MOE_GROUPS)]
    gmax = gl[0]
    gsel = jnp.zeros((1, tm), jnp.int32)
    for g in range(1, MOE_GROUPS):
        better = gl[g] > gmax
        gsel = jnp.where(better, g, gsel)
        gmax = jnp.where(better, gl[g], gmax)
    gden = gl[0] * 0.0
    for g in range(MOE_GROUPS):
        gden = gden + jnp.exp(gl[g] - gmax)
    g_w = 1.0 / gden
    el = []
    for e in range(MOE_EPG):
        v = logits[MOE_GROUPS + e:MOE_GROUPS + e + 1, :]
        for g in range(1, MOE_GROUPS):
            row = MOE_GROUPS + g * MOE_EPG + e
            v = jnp.where(gsel == g, logits[row:row + 1, :], v)
        el.append(v)
    v1 = el[0]
    i1 = jnp.zeros((1, tm), jnp.int32)
    for e in range(1, MOE_EPG):
        better = el[e] > v1
        i1 = jnp.where(better, e, i1)
        v1 = jnp.where(better, el[e], v1)
    v2 = jnp.full((1, tm), -jnp.inf, F32)
    i2 = jnp.zeros((1, tm), jnp.int32)
    for e in range(MOE_EPG):
        better = (el[e] > v2) & (i1 != e)
        i2 = jnp.where(better, e, i2)
        v2 = jnp.where(better, el[e], v2)
    ex = jnp.exp(v2 - v1)
    p1 = 1.0 / (1.0 + ex)
    e1 = gsel * MOE_EPG + i1
    e2 = gsel * MOE_EPG + i2
    erow = lax.broadcasted_iota(jnp.int32, (MOE_EXPERTS, tm), 0)
    oh1 = (erow == e1).astype(F32)
    oh2 = (erow == e2).astype(F32)
    both = oh1 + oh2
    prefix = jnp.dot(both.astype(BF16), tri_ref[...], preferred_element_type=F32) + carry[:, 0:1]
    rank1 = jnp.sum(oh1 * prefix, axis=0, keepdims=True).astype(jnp.int32)
    rank2 = jnp.sum(oh2 * prefix, axis=0, keepdims=True).astype(jnp.int32)
    carry[...] = carry[...] + jnp.sum(both, axis=1, keepdims=True)
    zi = jnp.zeros((1, tm), jnp.int32)
    ids_ref[...] = jnp.concatenate([e1, e2, rank1, rank2, zi, zi, zi, zi], axis=0)
    wrow = lax.broadcasted_iota(jnp.int32, (LANE, tm), 0)
    wts_ref[...] = jnp.where(wrow == 0, p1 * g_w, jnp.where(wrow == 1, ex * p1 * g_w, 0.0)).T
    cnt_ref[...] = carry[...]


def _router(x2, mod3, w_route, b_route, tri_excl, seq):
    t, d = x2.shape
    tm = TM_PROJ
    tpb = seq // tm
    nr = w_route.shape[0]
    const = lambda shape: pl.BlockSpec(shape, lambda i: (0,) * len(shape))
    modspec = lambda k: pl.BlockSpec((None, 1, d), lambda i: ((i // tpb) * 6 + k, 0, 0))
    return pl.pallas_call(
        _router_kernel,
        grid=(t // tm,),
        in_specs=[pl.BlockSpec((tm, d), lambda i: (i, 0)), modspec(4), modspec(3),
                  const((nr, d)), const((nr, LANE)), const((tm, tm))],
        out_specs=[pl.BlockSpec((tm, d), lambda i: (i, 0)),
                   pl.BlockSpec((8, tm), lambda i: (0, i)),
                   pl.BlockSpec((tm, LANE), lambda i: (i, 0)),
                   const((MOE_EXPERTS, LANE))],
        out_shape=[jax.ShapeDtypeStruct((t, d), F32),
                   jax.ShapeDtypeStruct((8, t), jnp.int32),
                   jax.ShapeDtypeStruct((t, LANE), F32),
                   jax.ShapeDtypeStruct((MOE_EXPERTS, LANE), F32)],
        scratch_shapes=[pltpu.VMEM((MOE_EXPERTS, LANE), F32)],
        compiler_params=_cparams(("arbitrary",)),
        name="moe_router",
    )(x2, mod3, mod3, w_route, b_route, tri_excl)


def _dispatch_kernel(s1_ref, s2_ref, h_ref, xs_hbm, sem, *, tm):
    def row_copy(r, slot):
        return pltpu.make_async_copy(h_ref.at[pl.ds(r, 1), :], xs_hbm.at[pl.ds(slot, 1), :], sem)

    def issue(r, c):
        row_copy(r, s1_ref[0, r]).start()
        row_copy(r, s2_ref[0, r]).start()
        return c

    lax.fori_loop(0, tm, issue, 0)

    def drain(r, c):
        row_copy(r, s1_ref[0, r]).wait()
        row_copy(r, s2_ref[0, r]).wait()
        return c

    lax.fori_loop(0, tm, drain, 0)


def _dispatch(slot1, slot2, h2):
    t, d = h2.shape
    tm = TM_DISP
    smem = lambda: pl.BlockSpec((None, 1, tm), lambda i: (i, 0, 0), memory_space=pltpu.SMEM)
    return pl.pallas_call(
        functools.partial(_dispatch_kernel, tm=tm),
        grid=(t // tm,),
        in_specs=[smem(), smem(), pl.BlockSpec((tm, d), lambda i: (i, 0))],
        out_specs=pl.BlockSpec(memory_space=pl.ANY),
        out_shape=jax.ShapeDtypeStruct((2 * t, d), h2.dtype),
        scratch_shapes=[pltpu.SemaphoreType.DMA(())],
        compiler_params=_cparams(("arbitrary",)),
        name="moe_dispatch",
    )(slot1.reshape(t // tm, 1, tm), slot2.reshape(t // tm, 1, tm), h2)


def _expert_kernel(tile_ref, exp_ref, lo_ref, hi_ref, xs_ref, w1_ref, w3_ref, w2_ref, ys_ref, w1_scr, w3_scr, w2_scr):
    s = pl.program_id(0)
    prev = jnp.maximum(s - 1, 0)
    new_expert = (s == 0) | (exp_ref[s] != exp_ref[prev])
    new_tile = (s == 0) | (tile_ref[s] != tile_ref[prev])

    @pl.when(new_expert)
    def _():
        w1_scr[...] = w1_ref[...].astype(BF16)
        w3_scr[...] = w3_ref[...].astype(BF16)
        w2_scr[...] = w2_ref[...].astype(BF16)

    x = xs_ref[...].astype(BF16)
    a = jnp.dot(x, w1_scr[...], preferred_element_type=F32)
    b = jnp.dot(x, w3_scr[...], preferred_element_type=F32)
    act = _silu(a) * b
    y = jnp.dot(act.astype(BF16), w2_scr[...], preferred_element_type=F32)
    row = lax.broadcasted_iota(jnp.int32, y.shape, 0)
    mine = (row >= lo_ref[s]) & (row < hi_ref[s])

    @pl.when(new_tile)
    def _():
        ys_ref[...] = jnp.where(mine, y, 0.0)

    @pl.when(jnp.logical_not(new_tile))
    def _():
        ys_ref[...] = jnp.where(mine, y, ys_ref[...])


def _experts(step_tile, step_expert, step_lo, step_hi, xs, w1, w3, w2, layer):
    ns, d = xs.shape
    ff = w1.shape[2]
    n_steps = step_tile.shape[0]
    base = layer * MOE_EXPERTS
    grid_spec = pltpu.PrefetchScalarGridSpec(
        num_scalar_prefetch=4,
        grid=(n_steps,),
        in_specs=[pl.BlockSpec((TM_X, d), lambda s, tl, ex, lo, hi: (tl[s], 0)),
                  pl.BlockSpec((None, d, ff), lambda s, tl, ex, lo, hi: (base + ex[s], 0, 0)),
                  pl.BlockSpec((None, d, ff), lambda s, tl, ex, lo, hi: (base + ex[s], 0, 0)),
                  pl.BlockSpec((None, ff, d), lambda s, tl, ex, lo, hi: (base + ex[s], 0, 0))],
        out_specs=pl.BlockSpec((TM_X, d), lambda s, tl, ex, lo, hi: (tl[s], 0)),
        scratch_shapes=[pltpu.VMEM((d, ff), BF16), pltpu.VMEM((d, ff), BF16), pltpu.VMEM((ff, d), BF16)],
    )
    return pl.pallas_call(
        _expert_kernel,
        grid_spec=grid_spec,
        out_shape=jax.ShapeDtypeStruct((ns, d), F32),
        compiler_params=_cparams(("arbitrary",)),
        name="moe_experts",
    )(step_tile, step_expert, step_lo, step_hi, xs, w1, w3, w2)


def _combine_kernel(s1_ref, s2_ref, x_ref, gate_ref, fw_ref, wcol_ref, ys_hbm, o_ref, buf, sem, *, tm, final):
    def row_copy(which, r, slot):
        return pltpu.make_async_copy(ys_hbm.at[pl.ds(slot, 1), :], buf.at[which, pl.ds(r, 1), :], sem)

    def issue(r, c):
        row_copy(0, r, s1_ref[0, r]).start()
        row_copy(1, r, s2_ref[0, r]).start()
        return c

    lax.fori_loop(0, tm, issue, 0)

    def drain(r, c):
        row_copy(0, r, s1_ref[0, r]).wait()
        row_copy(1, r, s2_ref[0, r]).wait()
        return c

    lax.fori_loop(0, tm, drain, 0)
    moe = wcol_ref[:, 0:1] * buf[0] + wcol_ref[:, 1:2] * buf[1]
    x = x_ref[...] + gate_ref[...] * moe
    if final:
        x = x * lax.rsqrt(jnp.mean(x * x, axis=-1, keepdims=True) + EPS) * fw_ref[...]
    o_ref[...] = x


def _combine(slot1, slot2, x2, mod3, final_w_row, wcol, ys3, seq, final):
    t, d = x2.shape
    tm = TM_COMB
    tpb = seq // tm
    smem = lambda: pl.BlockSpec((None, 1, tm), lambda i: (i, 0, 0), memory_space=pltpu.SMEM)
    return pl.pallas_call(
        functools.partial(_combine_kernel, tm=tm, final=final),
        grid=(t // tm,),
        in_specs=[smem(), smem(), pl.BlockSpec((tm, d), lambda i: (i, 0)),
                  pl.BlockSpec((None, 1, d), lambda i: ((i // tpb) * 6 + 5, 0, 0)),
                  pl.BlockSpec((1, d), lambda i: (0, 0)),
                  pl.BlockSpec((tm, LANE), lambda i: (i, 0)),
                  pl.BlockSpec(memory_space=pl.ANY)],
        out_specs=pl.BlockSpec((tm, d), lambda i: (i, 0)),
        out_shape=jax.ShapeDtypeStruct((t, d), F32),
        scratch_shapes=[pltpu.VMEM((2, tm, d), F32), pltpu.SemaphoreType.DMA(())],
        compiler_params=_cparams(("arbitrary",)),
        name="moe_combine",
    )(slot1.reshape(t // tm, 1, tm), slot2.reshape(t // tm, 1, tm), x2, mod3, final_w_row, wcol, ys3)


def _moe(x2, mod3, final_w_row, w_route, b_route, tri_excl, w1, w3, w2, layer, seq, final):
    t, d = x2.shape
    h3, ids, wcol, counts = _router(x2, mod3, w_route, b_route, tri_excl, seq)
    cnt = counts[:, 0].astype(jnp.int32)
    ends = jnp.cumsum(cnt)
    offs = ends - cnt
    slot1 = offs[ids[0]] + ids[2]
    slot2 = offs[ids[1]] + ids[3]
    n_tiles = 2 * t // TM_X
    first_tile = offs // TM_X
    n_vis = jnp.where(cnt > 0, (ends - 1) // TM_X - first_tile + 1, 0)
    cum = jnp.cumsum(n_vis)
    step = jnp.arange(n_tiles + MOE_EXPERTS, dtype=jnp.int32)
    step_expert = jnp.minimum(jnp.sum(step[:, None] >= cum[None, :], axis=1), MOE_EXPERTS - 1).astype(jnp.int32)
    valid = step < cum[-1]
    step_tile = jnp.where(valid, first_tile[step_expert] + step - (cum - n_vis)[step_expert], n_tiles - 1)
    step_lo = jnp.where(valid, jnp.clip(offs[step_expert] - step_tile * TM_X, 0, TM_X), 0)
    step_hi = jnp.where(valid, jnp.clip(ends[step_expert] - step_tile * TM_X, 0, TM_X), 0)
    xs = _dispatch(slot1, slot2, h3)
    ys = _experts(step_tile.astype(jnp.int32), step_expert, step_lo.astype(jnp.int32), step_hi.astype(jnp.int32),
                  xs, w1, w3, w2, layer)
    return _combine(slot1, slot2, x2, mod3, final_w_row, wcol, ys, seq, final)


def kernel(x, c, positions, ada_w, ada_b, w_in, s5_lam_re, s5_lam_im, s5_b_re, s5_b_im, s5_c_re, s5_c_im, s5_d, s5_log_dt, s5_w_glu, hg_lb_logits, hg_norm_w, m2_conv_w, m2_conv_b, m2_dt_bias, m2_a_log, m2_d, m2_norm_w, w_branch, w_gate, b_gate, w_out, moe_w_group, moe_b_group, moe_w_expert, moe_b_expert, moe_w1, moe_w3, moe_w2, final_norm_w):
    bsz, seq, d = x.shape
    t = bsz * seq
    depth = ada_w.shape[0]
    assert seq % TM_PROJ == 0 and seq % C_RET == 0 and seq % C_SSD == 0 and seq % C_HG == 0
    x2 = x.reshape(t, d).astype(F32)

    c_pad = jnp.zeros((8, d), F32).at[:bsz].set(c.astype(F32))
    mod_all = _ada_mod(c_pad, ada_w.astype(F32), ada_b.astype(F32))

    half = RET_DK // 2
    inv_freq = ROPE_BASE ** (-jnp.arange(half, dtype=F32) / half)
    invf_row = jnp.tile(inv_freq, 2 * RET_HEADS)[None, :]
    sgn_row = jnp.asarray(np.tile(np.concatenate([-np.ones(half), np.ones(half)]), RET_HEADS)[None, :], F32)
    cos_t, sin_t = _rope_tables(positions.reshape(t, 1).astype(jnp.int32), invf_row, sgn_row)
    cos3 = cos_t.reshape(bsz, seq, -1)
    sin3 = sin_t.reshape(bsz, seq, -1)

    lb_cum = jnp.cumsum(jax.nn.softmax(hg_lb_logits.astype(F32), axis=0), axis=0)
    hg_lb = lb_cum - lb_cum[:1]
    tri_ssd = jnp.asarray(np.tril(np.ones((C_SSD, C_SSD), np.float32)))
    tri_excl = jnp.asarray(np.triu(np.ones((TM_PROJ, TM_PROJ), np.float32), 1), BF16)
    final_w_row = final_norm_w.astype(F32)[None, :]
    w_pad = jnp.zeros((depth, d, IN_W_PAD), BF16).at[:, :, :IN_W].set(w_in.astype(BF16))
    w_glu_bf = s5_w_glu.astype(BF16)
    w_branch_bf = w_branch.astype(BF16)
    w_gate_bf = w_gate.astype(BF16)
    w_out_bf = w_out.astype(BF16)
    b_gate3 = b_gate.astype(F32).reshape(depth, 1, -1)
    moe_w1_all = moe_w1.astype(F32).reshape(depth * MOE_EXPERTS, d, MOE_FF)
    moe_w3_all = moe_w3.astype(F32).reshape(depth * MOE_EXPERTS, d, MOE_FF)
    moe_w2_all = moe_w2.astype(F32).reshape(depth * MOE_EXPERTS, MOE_FF, d)

    for layer in range(depth):
        mod3 = mod_all[layer, :bsz].reshape(bsz * 6, 1, d)
        p = _in_proj(x2, mod3, w_pad, layer, seq)
        p3 = p.reshape(bsz, seq, IN_W_PAD)

        ops = _s5_operators(s5_lam_re[layer], s5_lam_im[layer], s5_b_re[layer], s5_b_im[layer],
                            s5_c_re[layer], s5_c_im[layer], s5_d[layer], s5_log_dt[layer])
        y_s5 = _s5_scan(p3, *ops).reshape(t, BRANCH_W)

        lb = hg_lb[layer][None, :]
        y_hg = _hgrn2(p3, jnp.log(lb), jnp.log1p(-lb), hg_norm_w[layer].astype(F32)[None, :]).reshape(t, BRANCH_W)

        y_ret = _retention(p3, cos3, sin3).reshape(t, BRANCH_W)

        pad8 = lambda v: jnp.zeros((1, LANE), F32).at[0, :M2_HEADS].set(v.astype(F32))
        y_m2 = _ssd(p3, tri_ssd, m2_conv_w[layer].astype(F32), m2_conv_b[layer].astype(F32)[None, :],
                    pad8(m2_dt_bias[layer]), pad8(m2_a_log[layer]),
                    jnp.repeat(m2_d[layer].astype(F32), M2_HEADDIM)[None, :],
                    m2_norm_w[layer].astype(F32)[None, :]).reshape(t, BRANCH_W)

        x2 = _merge(x2, mod3, y_s5, y_hg, y_ret, y_m2, w_glu_bf, w_branch_bf, w_gate_bf, b_gate3, w_out_bf,
                    layer, seq)

        nr = 40
        w_route = jnp.zeros((nr, d), F32).at[:MOE_GROUPS].set(moe_w_group[layer].astype(F32).T)
        w_route = w_route.at[MOE_GROUPS:MOE_GROUPS + MOE_EXPERTS].set(moe_w_expert[layer].astype(F32).T)
        b_route = jnp.zeros((nr, LANE), F32).at[:MOE_GROUPS, 0].set(moe_b_group[layer].astype(F32))
        b_route = b_route.at[MOE_GROUPS:MOE_GROUPS + MOE_EXPERTS, 0].set(moe_b_expert[layer].astype(F32))
        x2 = _moe(x2, mod3, final_w_row, w_route, b_route, tri_excl, moe_w1_all, moe_w3_all, moe_w2_all,
                  layer, seq, final=(layer == depth - 1))
    return x2.reshape(bsz, seq, d)
```

```python
import functools
import math

import numpy as np
import jax
import jax.numpy as jnp
from jax import lax
from jax.experimental import pallas as pl
from jax.experimental.pallas import tpu as pltpu
from jax.experimental.pallas import tpu_sc as plsc

F32 = jnp.float32
BF16 = jnp.bfloat16
HIGHEST = lax.Precision.HIGHEST

D_MODEL = 1024
DEPTH = 2
BRANCH_W = 512
EPS = 1e-6
S5_GROUPS = 32
S5_CH = 16
S5_STATE = 64
S5_MAX_REAL = -1e-4
S5_BLOCK = 16
HG_HEADS = 4
HG_DK = 128
RET_HEADS = 4
RET_DK = 64
RET_DV = 128
ROPE_BASE = 10000.0
M2_HEADS = 8
M2_HEADDIM = 64
M2_GROUPS = 2
M2_STATE = 128
M2_CONV = 4
MOE_GROUPS = 4
MOE_EPG = 8
MOE_EXPERTS = MOE_GROUPS * MOE_EPG
MOE_FF = 256

COL_S5, COL_HQ, COL_HF, COL_HI, COL_HG = 0, 512, 1024, 1536, 2048
COL_RQ, COL_RK, COL_RV, COL_RG = 2560, 2816, 3072, 3584
COL_MZ, COL_MXS, COL_MBC, COL_MDT = 4096, 4608, 5120, 5632
IN_W = 5640
IN_W_PAD = 5760

LANE = 128
VMEM_LIMIT = 56 * 1024 * 1024

TM_PROJ = 1024
TN_PROJ = 1152
TM_MERGE = 512
C_RET = 256
C_SSD = 256
C_HG = 128
TM_X = 256
TM_COMB = 512
SC_WINDOW = 128
SLAB = 256
N_SLAB = D_MODEL // SLAB


def _cparams(sem):
    return pltpu.CompilerParams(dimension_semantics=sem, vmem_limit_bytes=VMEM_LIMIT)


def _silu(v):
    return v * jax.nn.sigmoid(v)


def _dot_nt(a, b, **kw):
    return lax.dot_general(a, b, (((1,), (1,)), ((), ())), preferred_element_type=F32, **kw)


def _dot_tn(a, b, **kw):
    return lax.dot_general(a, b, (((0,), (0,)), ((), ())), preferred_element_type=F32, **kw)


def _ada_kernel(c_ref, w_ref, b_ref, o_ref):
    cond = _silu(c_ref[...])
    o_ref[...] = jnp.dot(cond, w_ref[...], preferred_element_type=F32, precision=HIGHEST) + b_ref[...]


def _ada_mod(c_pad, ada_w, ada_b):
    depth, d, n = ada_w.shape
    tn = 1536
    return pl.pallas_call(
        _ada_kernel,
        grid=(depth, n // tn),
        in_specs=[pl.BlockSpec((8, d), lambda l, j: (0, 0)),
                  pl.BlockSpec((None, d, tn), lambda l, j: (l, 0, j)),
                  pl.BlockSpec((None, 1, tn), lambda l, j: (l, 0, j))],
        out_specs=pl.BlockSpec((None, 8, tn), lambda l, j: (l, 0, j)),
        out_shape=jax.ShapeDtypeStruct((depth, 8, n), F32),
        compiler_params=_cparams(("parallel", "parallel")),
        name="ada_mod",
    )(c_pad, ada_w, ada_b.reshape(depth, 1, n))


def _modulated_norm(x, scale, shift):
    ms = jnp.mean(x * x, axis=-1, keepdims=True)
    return x * lax.rsqrt(ms + EPS) * (1.0 + scale) + shift


def _inproj_kernel(x_ref, sc_ref, sh_ref, w_ref, o_ref, h_scr):
    @pl.when(pl.program_id(1) == 0)
    def _():
        h_scr[...] = _modulated_norm(x_ref[...], sc_ref[...], sh_ref[...]).astype(BF16)

    o_ref[...] = jnp.dot(h_scr[...], w_ref[...], preferred_element_type=F32)


def _in_proj(x2, mod3, w_pad, layer, seq):
    t, d = x2.shape
    tpb = seq // TM_PROJ
    return pl.pallas_call(
        _inproj_kernel,
        grid=(t // TM_PROJ, IN_W_PAD // TN_PROJ),
        in_specs=[pl.BlockSpec((TM_PROJ, d), lambda i, j: (i, 0)),
                  pl.BlockSpec((None, 1, d), lambda i, j: ((i // tpb) * 6 + 1, 0, 0)),
                  pl.BlockSpec((None, 1, d), lambda i, j: ((i // tpb) * 6 + 0, 0, 0)),
                  pl.BlockSpec((None, d, TN_PROJ), lambda i, j: (layer, 0, j))],
        out_specs=pl.BlockSpec((TM_PROJ, TN_PROJ), lambda i, j: (i, j)),
        out_shape=jax.ShapeDtypeStruct((t, IN_W_PAD), F32),
        scratch_shapes=[pltpu.VMEM((TM_PROJ, d), BF16)],
        compiler_params=_cparams(("parallel", "arbitrary")),
        name="in_proj",
    )(x2, mod3, mod3, w_pad)


def _rope_kernel(pos_ref, invf_ref, sgn_ref, cos_ref, sin_ref):
    ang = pos_ref[...].astype(F32) * invf_ref[...]
    cos_ref[...] = jnp.cos(ang)
    sin_ref[...] = jnp.sin(ang) * sgn_ref[...]


def _rope_tables(pos_col, invf_row, sgn_row):
    t = pos_col.shape[0]
    w = invf_row.shape[1]
    tm = 1024
    return pl.pallas_call(
        _rope_kernel,
        grid=(t // tm,),
        in_specs=[pl.BlockSpec((tm, 1), lambda i: (i, 0)),
                  pl.BlockSpec((1, w), lambda i: (0, 0)),
                  pl.BlockSpec((1, w), lambda i: (0, 0))],
        out_specs=[pl.BlockSpec((tm, w), lambda i: (i, 0))] * 2,
        out_shape=[jax.ShapeDtypeStruct((t, w), F32)] * 2,
        compiler_params=_cparams(("parallel",)),
        name="rope_tables",
    )(pos_col, invf_row, sgn_row)


def _ret_kernel(q_ref, k_ref, v_ref, g_ref, cos_ref, sin_ref, o_ref, st_ref, *, chunk):
    @pl.when(pl.program_id(1) == 0)
    def _():
        st_ref[...] = jnp.zeros_like(st_ref)

    cosf = cos_ref[...]
    sinf = sin_ref[...]
    width = RET_HEADS * RET_DK
    lane = lax.broadcasted_iota(jnp.int32, (chunk, width), 1)
    first_half = (lane % RET_DK) < (RET_DK // 2)

    def rope(t):
        partner = jnp.where(first_half, pltpu.roll(t, width - RET_DK // 2, 1), pltpu.roll(t, RET_DK // 2, 1))
        return t * cosf + partner * sinf

    q = rope(q_ref[...])
    k = rope(k_ref[...]) * (RET_DK ** -0.5)
    v = v_ref[...]
    g = g_ref[...]
    ti = lax.broadcasted_iota(jnp.int32, (chunk, chunk), 0)
    si = lax.broadcasted_iota(jnp.int32, (chunk, chunk), 1)
    lag = (ti - si).astype(F32)
    tcol = lax.broadcasted_iota(jnp.int32, (chunk, 1), 0).astype(F32)
    for h in range(RET_HEADS):
        log_gamma = math.log1p(-(2.0 ** (-5.0 - h)))
        qh = q[:, h * RET_DK:(h + 1) * RET_DK]
        kh = k[:, h * RET_DK:(h + 1) * RET_DK]
        vh = v[:, h * RET_DV:(h + 1) * RET_DV].astype(BF16)
        decay = jnp.where(ti >= si, jnp.exp(jnp.minimum(lag * log_gamma, 0.0)), 0.0)
        scores = _dot_nt(qh.astype(BF16), kh.astype(BF16)) * decay
        state = st_ref[h]
        q_in = qh * jnp.exp(log_gamma * (tcol + 1.0))
        o = (jnp.dot(scores.astype(BF16), vh, preferred_element_type=F32)
             + jnp.dot(q_in.astype(BF16), state.astype(BF16), preferred_element_type=F32))
        k_out = kh * jnp.exp(log_gamma * (chunk - 1.0 - tcol))
        st_ref[h] = math.exp(log_gamma * chunk) * state + _dot_tn(k_out.astype(BF16), vh)
        o = o * lax.rsqrt(jnp.mean(o * o, axis=-1, keepdims=True) + EPS)
        gh = g[:, h * RET_DV:(h + 1) * RET_DV]
        o_ref[:, h * RET_DV:(h + 1) * RET_DV] = o * _silu(gh)


def _retention(p3, cos3, sin3):
    b, seq, _ = p3.shape
    c = C_RET
    qk_w = RET_HEADS * RET_DK
    return pl.pallas_call(
        functools.partial(_ret_kernel, chunk=c),
        grid=(b, seq // c),
        in_specs=[pl.BlockSpec((None, c, qk_w), lambda i, j: (i, j, COL_RQ // qk_w)),
                  pl.BlockSpec((None, c, qk_w), lambda i, j: (i, j, COL_RK // qk_w)),
                  pl.BlockSpec((None, c, BRANCH_W), lambda i, j: (i, j, COL_RV // BRANCH_W)),
                  pl.BlockSpec((None, c, BRANCH_W), lambda i, j: (i, j, COL_RG // BRANCH_W)),
                  pl.BlockSpec((None, c, qk_w), lambda i, j: (i, j, 0)),
                  pl.BlockSpec((None, c, qk_w), lambda i, j: (i, j, 0))],
        out_specs=pl.BlockSpec((None, c, BRANCH_W), lambda i, j: (i, j, 0)),
        out_shape=jax.ShapeDtypeStruct((b, seq, BRANCH_W), F32),
        scratch_shapes=[pltpu.VMEM((RET_HEADS, RET_DK, RET_DV), F32)],
        compiler_params=_cparams(("parallel", "arbitrary")),
        name="retention",
    )(p3, p3, p3, p3, cos3, sin3)


def _ssd_kernel(z_ref, xs_ref, bc_ref, dt_ref, tri_ref, cw_ref, cb_ref, dtb_ref, alog_ref, dsk_ref, nw_ref,
                o_ref, xe_scr, st_ref, *, chunk):
    j = pl.program_id(1)
    width = 2 * BRANCH_W

    @pl.when(j == 0)
    def _():
        st_ref[...] = jnp.zeros_like(st_ref)
        xe_scr[0:8, :] = jnp.zeros((8, width), F32)

    @pl.when(j > 0)
    def _():
        xe_scr[0:8, :] = xe_scr[chunk:chunk + 8, :]

    xe_scr[8:, 0:BRANCH_W] = xs_ref[...]
    xe_scr[8:, BRANCH_W:] = bc_ref[...]
    conv = cb_ref[...] + cw_ref[M2_CONV - 1:M2_CONV, :] * xe_scr[8:, :]
    for tap in range(M2_CONV - 1):
        conv = conv + cw_ref[tap:tap + 1, :] * xe_scr[pl.ds(8 - (M2_CONV - 1) + tap, chunk), :]
    conv = _silu(conv)
    xs = conv[:, :BRANCH_W]
    bm = conv[:, BRANCH_W:BRANCH_W + M2_GROUPS * M2_STATE]
    cm = conv[:, BRANCH_W + M2_GROUPS * M2_STATE:]

    dt = jax.nn.softplus(dt_ref[...] + dtb_ref[...])
    da = dt * (-jnp.exp(alog_ref[...]))
    a_cs = jnp.dot(tri_ref[...], da, preferred_element_type=F32, precision=HIGHEST)
    a_cs_t = a_cs.T
    ti = lax.broadcasted_iota(jnp.int32, (chunk, chunk), 0)
    si = lax.broadcasted_iota(jnp.int32, (chunk, chunk), 1)
    causal = ti >= si
    hpg = M2_HEADS // M2_GROUPS
    ys = []
    for grp in range(M2_GROUPS):
        bm_g = bm[:, grp * M2_STATE:(grp + 1) * M2_STATE]
        cm_g = cm[:, grp * M2_STATE:(grp + 1) * M2_STATE]
        cb = _dot_nt(cm_g.astype(BF16), bm_g.astype(BF16))
        for hh in range(hpg):
            h = grp * hpg + hh
            col = a_cs[:, h:h + 1]
            row = a_cs_t[h:h + 1, :]
            lmat = jnp.where(causal, jnp.exp(jnp.minimum(col - row, 0.0)), 0.0)
            xd = xs[:, h * M2_HEADDIM:(h + 1) * M2_HEADDIM] * dt[:, h:h + 1]
            state = st_ref[h]
            y = (jnp.dot((cb * lmat).astype(BF16), xd.astype(BF16), preferred_element_type=F32)
                 + jnp.dot((cm_g * jnp.exp(col)).astype(BF16), state.astype(BF16), preferred_element_type=F32))
            a_last = a_cs[chunk - 1:chunk, h:h + 1]
            to_end = jnp.exp(a_last - col)
            st_ref[h] = jnp.exp(a_last) * state + _dot_tn(bm_g.astype(BF16), (xd * to_end).astype(BF16))
            ys.append(y)
    y = jnp.concatenate(ys, axis=-1) + dsk_ref[...] * xs
    y = y * _silu(z_ref[...])
    o_ref[...] = y * lax.rsqrt(jnp.mean(y * y, axis=-1, keepdims=True) + EPS) * nw_ref[...]


def _ssd(p3, tri, conv_w, conv_b, dt_bias_row, a_log_row, d_skip_row, norm_w_row):
    b, seq, _ = p3.shape
    c = C_SSD
    const = lambda shape: pl.BlockSpec(shape, lambda i, j: (0,) * len(shape))
    return pl.pallas_call(
        functools.partial(_ssd_kernel, chunk=c),
        grid=(b, seq // c),
        in_specs=[pl.BlockSpec((None, c, BRANCH_W), lambda i, j: (i, j, COL_MZ // BRANCH_W)),
                  pl.BlockSpec((None, c, BRANCH_W), lambda i, j: (i, j, COL_MXS // BRANCH_W)),
                  pl.BlockSpec((None, c, BRANCH_W), lambda i, j: (i, j, COL_MBC // BRANCH_W)),
                  pl.BlockSpec((None, c, LANE), lambda i, j: (i, j, COL_MDT // LANE)),
                  const((c, c)), const((M2_CONV, 2 * BRANCH_W)), const((1, 2 * BRANCH_W)),
                  const((1, LANE)), const((1, LANE)), const((1, BRANCH_W)), const((1, BRANCH_W))],
        out_specs=pl.BlockSpec((None, c, BRANCH_W), lambda i, j: (i, j, 0)),
        out_shape=jax.ShapeDtypeStruct((b, seq, BRANCH_W), F32),
        scratch_shapes=[pltpu.VMEM((c + 8, 2 * BRANCH_W), F32),
                        pltpu.VMEM((M2_HEADS, M2_STATE, M2_HEADDIM), F32)],
        compiler_params=_cparams(("parallel", "arbitrary")),
        name="ssd",
    )(p3, p3, p3, p3, tri, conv_w, conv_b, dt_bias_row, a_log_row, d_skip_row, norm_w_row)


def _hg_tables(chunk):
    n_lev = int(math.log2(chunk))
    r = np.arange(chunk)[:, None]
    jj = np.arange(chunk)[None, :]
    mats = [(jj <= r), (jj > r)]
    for lev in range(n_lev):
        m = 1 << lev
        mid = (r >> (lev + 1) << (lev + 1)) + m - 1
        second = ((r >> lev) & 1) == 1
        mats.append(np.where(second, (jj > mid) & (jj <= r), (jj > r) & (jj <= mid)))
    summat = np.concatenate(mats, axis=0).astype(np.float32)
    x = r ^ jj
    levmap = np.where(r > jj, np.floor(np.log2(x + 0.5)), np.where(r == jj, -1, -2)).astype(np.int32)
    return summat, levmap, n_lev


def _hg_kernel(q_ref, f_ref, i_ref, g_ref, sum_ref, lev_ref, llb_ref, l1m_ref, nw_ref, o_ref, st_ref,
               *, chunk, n_lev):
    @pl.when(pl.program_id(1) == 0)
    def _():
        st_ref[...] = jnp.zeros_like(st_ref)

    f = f_ref[...]
    log_sig = jnp.minimum(f, 0.0) - jnp.log1p(jnp.exp(-jnp.abs(f)))
    a = llb_ref[...]
    bb = l1m_ref[...] + log_sig
    log_f = jnp.maximum(a, bb) + jnp.log1p(jnp.exp(-jnp.abs(a - bb)))
    k_all = jnp.exp(l1m_ref[...]) * jax.nn.sigmoid(-f)
    q_all = _silu(q_ref[...])
    hi = log_f.astype(BF16)
    r1 = log_f - hi.astype(F32)
    mid = r1.astype(BF16)
    lo = (r1 - mid.astype(F32)).astype(BF16)
    summat = sum_ref[...]
    sums = (jnp.dot(summat, hi, preferred_element_type=F32)
            + jnp.dot(summat, mid, preferred_element_type=F32)
            + jnp.dot(summat, lo, preferred_element_type=F32))
    levmap = lev_ref[...]
    v_all = i_ref[...]
    g_all = g_ref[...]
    for h in range(HG_HEADS):
        sl = slice(h * HG_DK, (h + 1) * HG_DK)
        qh = q_all[:, sl]
        kh = k_all[:, sl]
        vh = v_all[:, sl].astype(BF16)
        b_h = sums[0:chunk, sl]
        to_end = sums[chunk:2 * chunk, sl]
        amat = jnp.where(levmap == -1, _dot_nt(qh.astype(BF16), kh.astype(BF16)), 0.0)
        for lev in range(n_lev):
            e = jnp.exp(sums[(2 + lev) * chunk:(3 + lev) * chunk, sl])
            a_l = _dot_nt((qh * e).astype(BF16), (kh * e).astype(BF16))
            amat = jnp.where(levmap == lev, a_l, amat)
        state_t = st_ref[h]
        o = (jnp.dot(amat.astype(BF16), vh, preferred_element_type=F32)
             + _dot_nt((qh * jnp.exp(b_h)).astype(BF16), state_t.astype(BF16)))
        k_end = kh * jnp.exp(to_end)
        st_ref[h] = jnp.exp(b_h[chunk - 1:chunk, :]) * state_t + _dot_tn(vh, k_end.astype(BF16))
        o = o * lax.rsqrt(jnp.mean(o * o, axis=-1, keepdims=True) + EPS) * nw_ref[...]
        o_ref[:, sl] = o * _silu(g_all[:, sl])


def _hgrn2(p3, log_lb_row, log1m_lb_row, norm_w_row):
    b, seq, _ = p3.shape
    c = C_HG
    summat, levmap, n_lev = _hg_tables(c)
    const = lambda shape: pl.BlockSpec(shape, lambda i, j: (0,) * len(shape))
    blk = lambda col: pl.BlockSpec((None, c, BRANCH_W), lambda i, j: (i, j, col // BRANCH_W))
    return pl.pallas_call(
        functools.partial(_hg_kernel, chunk=c, n_lev=n_lev),
        grid=(b, seq // c),
        in_specs=[blk(COL_HQ), blk(COL_HF), blk(COL_HI), blk(COL_HG),
                  const(summat.shape), const((c, c)),
                  const((1, BRANCH_W)), const((1, BRANCH_W)), const((1, HG_DK))],
        out_specs=pl.BlockSpec((None, c, BRANCH_W), lambda i, j: (i, j, 0)),
        out_shape=jax.ShapeDtypeStruct((b, seq, BRANCH_W), F32),
        scratch_shapes=[pltpu.VMEM((HG_HEADS, HG_DK, HG_DK), F32)],
        compiler_params=_cparams(("parallel", "arbitrary")),
        name="hgrn2",
    )(p3, p3, p3, p3, jnp.asarray(summat, BF16), jnp.asarray(levmap), log_lb_row, log1m_lb_row, norm_w_row)


def _expand_block_diag(comp_ref, e_ref, dst_ref, row_div, lane_div):
    gq = LANE // S5_CH
    rows, ncols = dst_ref.shape
    step = 512
    comp = comp_ref[...]
    row_grp = (lax.broadcasted_iota(jnp.int32, (rows, step), 0) // row_div) % gq
    for c0 in range(0, ncols, step):
        lane_grp = ((lax.broadcasted_iota(jnp.int32, (rows, step), 1) + c0) // lane_div) % gq
        full = jnp.dot(comp, e_ref[:, c0:c0 + step], preferred_element_type=F32)
        dst_ref[:, c0:c0 + step] = jnp.where(row_grp == lane_grp, full, 0.0).astype(dst_ref.dtype)


def _s5_kernel(u_ref, k2_ref, bc_ref, cc_ref, esc_ref, eb_ref, lam_ref, o_ref, tc_scr, tq_ref, bq_ref, cq_ref,
               x_scr, w_scr, s_scr, *, rows):
    nb = S5_BLOCK

    @pl.when(pl.program_id(1) == 0)
    def _():
        k2 = k2_ref[...]
        lane = lax.broadcasted_iota(jnp.int32, k2.shape, 1)
        for t in range(nb):
            shifted = k2 if t == 0 else jnp.where(lane >= t * S5_CH, pltpu.roll(k2, t * S5_CH, 1), 0.0)
            tc_scr[t * LANE:(t + 1) * LANE, :] = shifted.astype(tc_scr.dtype)
        _expand_block_diag(tc_scr, esc_ref, tq_ref, S5_CH, S5_CH)
        _expand_block_diag(bc_ref, eb_ref, bq_ref, S5_CH, S5_STATE)
        _expand_block_diag(cc_ref, esc_ref, cq_ref, S5_STATE, S5_CH)

    for t in range(nb):
        x_scr[:, t * LANE:(t + 1) * LANE] = u_ref[pl.ds(t, rows, stride=nb), :].astype(x_scr.dtype)
    x = x_scr[...]
    half = w_scr.shape[1] // 2
    w_scr[...] = jnp.dot(x, bq_ref[...], preferred_element_type=F32)
    lam_re = lam_ref[0:1, :]
    lam_im = lam_ref[1:2, :]

    def body(j, carry):
        s_re, s_im = carry
        s_scr[pl.ds(j, 1), 0:half] = s_re
        s_scr[pl.ds(j, 1), half:] = s_im
        w_re = w_scr[pl.ds(j, 1), 0:half]
        w_im = w_scr[pl.ds(j, 1), half:]
        return lam_re * s_re - lam_im * s_im + w_re, lam_re * s_im + lam_im * s_re + w_im

    zero = jnp.zeros((1, half), F32)
    lax.fori_loop(0, rows, body, (zero, zero))
    y = (jnp.dot(x, tq_ref[...], preferred_element_type=F32)
         + jnp.dot(s_scr[...].astype(BF16), cq_ref[...], preferred_element_type=F32))
    for t in range(nb):
        o_ref[pl.ds(t, rows, stride=nb), :] = y[:, t * LANE:(t + 1) * LANE]


def _s5_scan(p3, k2, bc, cc, lam16):
    batch, seq, _ = p3.shape
    nb = S5_BLOCK
    nq = BRANCH_W // LANE
    rows = seq // nb
    kdim = nb * LANE
    gq = LANE // S5_CH
    ncol = 2 * gq * S5_STATE
    e_sc = (np.eye(nb)[:, None, :, None, None] * np.eye(S5_CH)[None, :, None, None, :] * np.ones((1, 1, 1, gq, 1)))
    e_sc = e_sc.reshape(nb * S5_CH, nb * gq * S5_CH)
    e_b = (np.eye(2)[:, None, :, None, None] * np.eye(S5_STATE)[None, :, None, None, :] * np.ones((1, 1, 1, gq, 1)))
    e_b = e_b.reshape(2 * S5_STATE, ncol)
    full = lambda shape: pl.BlockSpec(shape, lambda q, b: (0,) * len(shape))
    per_q = lambda r, c: pl.BlockSpec((None, r, c), lambda q, b: (q, 0, 0))
    return pl.pallas_call(
        functools.partial(_s5_kernel, rows=rows),
        grid=(nq, batch),
        in_specs=[pl.BlockSpec((None, seq, LANE), lambda q, b: (b, 0, COL_S5 // LANE + q)),
                  per_q(LANE, nb * S5_CH), per_q(kdim, 2 * S5_STATE), per_q(ncol, nb * S5_CH),
                  full(e_sc.shape), full(e_b.shape), per_q(2, ncol // 2)],
        out_specs=pl.BlockSpec((None, seq, LANE), lambda q, b: (b, 0, q)),
        out_shape=jax.ShapeDtypeStruct((batch, seq, BRANCH_W), F32),
        scratch_shapes=[pltpu.VMEM((kdim, nb * S5_CH), BF16),
                        pltpu.VMEM((kdim, kdim), BF16), pltpu.VMEM((kdim, ncol), BF16), pltpu.VMEM((ncol, kdim), BF16),
                        pltpu.VMEM((rows, kdim), BF16), pltpu.VMEM((rows, ncol), F32), pltpu.VMEM((rows, ncol), F32)],
        compiler_params=_cparams(("parallel", "arbitrary")),
        name="s5_scan",
    )(p3, k2, bc, cc, jnp.asarray(e_sc, BF16), jnp.asarray(e_b, BF16), lam16)


def _s5_operators(lam_re, lam_im, b_re, b_im, c_re, c_im, d_skip, log_dt):
    nb = S5_BLOCK
    gq = LANE // S5_CH
    nq = S5_GROUPS // gq
    lam = lax.complex(jnp.minimum(lam_re.astype(F32), S5_MAX_REAL), lam_im.astype(F32))
    step = jnp.exp(log_dt.astype(F32))[:, None]
    z = lam * step
    lam_bar = jnp.exp(z)
    b_bar = ((lam_bar - 1.0) / lam)[..., None] * lax.complex(b_re.astype(F32), b_im.astype(F32))
    c_mat = lax.complex(c_re.astype(F32), c_im.astype(F32))
    pw = jnp.exp(z[..., None] * jnp.arange(nb + 1, dtype=F32))
    cp = c_mat[:, None, :, :] * pw[..., :nb].transpose(0, 2, 1)[:, :, None, :]
    cp = jnp.concatenate([cp.real, -cp.imag], axis=-1).reshape(S5_GROUPS, nb * S5_CH, 2 * S5_STATE)
    bri = jnp.concatenate([b_bar.real, b_bar.imag], axis=1)
    kern = jnp.einsum('gnk,gki->gin', cp, bri, precision=HIGHEST)
    skip = (jnp.asarray(np.concatenate([np.eye(S5_CH), np.zeros((S5_CH, (nb - 1) * S5_CH))], axis=1), F32)[None]
            * d_skip.astype(F32).reshape(S5_GROUPS, S5_CH, 1))
    k2 = (kern + skip).reshape(nq, gq * S5_CH, nb * S5_CH)
    pw_rev = jnp.exp(z[..., None] * jnp.asarray(np.arange(nb - 1, -1, -1), F32))
    binc = pw_rev[:, :, :, None] * b_bar[:, :, None, :]
    binc = jnp.stack([binc.real, binc.imag], axis=0).reshape(2, nq, gq, S5_STATE, nb, S5_CH)
    bc = binc.transpose(1, 4, 2, 5, 0, 3).reshape(nq, nb * LANE, 2 * S5_STATE)
    cm = c_mat.transpose(0, 2, 1)[:, :, None, :] * pw[..., 1:][:, :, :, None]
    cm = jnp.stack([cm.real, -cm.imag], axis=0).reshape(2, nq, gq * S5_STATE, nb * S5_CH)
    cc = cm.transpose(1, 0, 2, 3).reshape(nq, 2 * gq * S5_STATE, nb * S5_CH)
    lam_n = pw[..., nb].reshape(nq, gq * S5_STATE)
    lam16 = jnp.stack([lam_n.real, lam_n.imag], axis=1)
    return k2, bc.astype(BF16), cc.astype(BF16), lam16


def _merge_kernel(x_ref, sc_ref, sh_ref, gm_ref, ys5_ref, yhg_ref, yret_ref, ym2_ref,
                  wglu_ref, wbr_ref, wg_ref, bg_ref, wout_ref, o_ref):
    x = x_ref[...]
    d = x.shape[1]
    h = _modulated_norm(x, sc_ref[...], sh_ref[...]).astype(BF16)
    y_s5 = jax.nn.gelu(ys5_ref[...])
    y_s5 = y_s5 * jax.nn.sigmoid(jnp.dot(y_s5.astype(BF16), wglu_ref[...], preferred_element_type=F32))
    acc = jnp.zeros(x.shape, F32)
    for n, y in enumerate((y_s5, yhg_ref[...], yret_ref[...], ym2_ref[...])):
        gate = jax.nn.sigmoid(jnp.dot(h, wg_ref[:, n * d:(n + 1) * d], preferred_element_type=F32)
                              + bg_ref[:, n * d:(n + 1) * d])
        acc = acc + gate * jnp.dot(y.astype(BF16), wbr_ref[n], preferred_element_type=F32)
    o_ref[...] = x + gm_ref[...] * jnp.dot(acc.astype(BF16), wout_ref[...], preferred_element_type=F32)


def _merge(x2, mod3, ys5, yhg, yret, ym2, w_glu, w_branch, w_gate, b_gate, w_out, layer, seq):
    t, d = x2.shape
    tm = TM_MERGE
    tpb = seq // tm
    const = lambda shape: pl.BlockSpec((None,) + shape, lambda i: (layer,) + (0,) * len(shape))
    modspec = lambda k: pl.BlockSpec((None, 1, d), lambda i: ((i // tpb) * 6 + k, 0, 0))
    yspec = pl.BlockSpec((tm, BRANCH_W), lambda i: (i, 0))
    return pl.pallas_call(
        _merge_kernel,
        grid=(t // tm,),
        in_specs=[pl.BlockSpec((tm, d), lambda i: (i, 0)), modspec(1), modspec(0), modspec(2),
                  yspec, yspec, yspec, yspec,
                  const((BRANCH_W, BRANCH_W)), const((4, BRANCH_W, d)), const((d, 4 * d)), const((1, 4 * d)),
                  const((d, d))],
        out_specs=pl.BlockSpec((tm, d), lambda i: (i, 0)),
        out_shape=jax.ShapeDtypeStruct((t, d), F32),
        compiler_params=_cparams(("parallel",)),
        name="merge",
    )(x2, mod3, mod3, mod3, ys5, yhg, yret, ym2, w_glu, w_branch, w_gate, b_gate, w_out)


def _router_kernel(x_ref, sc_ref, sh_ref, wr_ref, br_ref, tri_ref, h_ref, ids_ref, wts_ref, cnt_ref, carry):
    i = pl.program_id(0)

    @pl.when(i == 0)
    def _():
        carry[...] = jnp.zeros_like(carry)

    h = _modulated_norm(x_ref[...], sc_ref[...], sh_ref[...])
    tm, d = h.shape
    for k in range(N_SLAB):
        h_ref[k] = h[:, k * SLAB:(k + 1) * SLAB]
    logits = _dot_nt(wr_ref[...], h, precision=HIGHEST) + br_ref[:, 0:1]
    gl = [logits[g:g + 1, :] for g in range(MOE_GROUPS)]
    gmax = gl[0]
    gsel = jnp.zeros((1, tm), jnp.int32)
    for g in range(1, MOE_GROUPS):
        better = gl[g] > gmax
        gsel = jnp.where(better, g, gsel)
        gmax = jnp.where(better, gl[g], gmax)
    gden = gl[0] * 0.0
    for g in range(MOE_GROUPS):
        gden = gden + jnp.exp(gl[g] - gmax)
    g_w = 1.0 / gden
    el = []
    for e in range(MOE_EPG):
        v = logits[MOE_GROUPS + e:MOE_GROUPS + e + 1, :]
        for g in range(1, MOE_GROUPS):
            row = MOE_GROUPS + g * MOE_EPG + e
            v = jnp.where(gsel == g, logits[row:row + 1, :], v)
        el.append(v)
    v1 = el[0]
    i1 = jnp.zeros((1, tm), jnp.int32)
    for e in range(1, MOE_EPG):
        better = el[e] > v1
        i1 = jnp.where(better, e, i1)
        v1 = jnp.where(better, el[e], v1)
    v2 = jnp.full((1, tm), -jnp.inf, F32)
    i2 = jnp.zeros((1, tm), jnp.int32)
    for e in range(MOE_EPG):
        better = (el[e] > v2) & (i1 != e)
        i2 = jnp.where(better, e, i2)
        v2 = jnp.where(better, el[e], v2)
    ex = jnp.exp(v2 - v1)
    p1 = 1.0 / (1.0 + ex)
    e1 = gsel * MOE_EPG + i1
    e2 = gsel * MOE_EPG + i2
    erow = lax.broadcasted_iota(jnp.int32, (MOE_EXPERTS, tm), 0)
    oh1 = (erow == e1).astype(F32)
    oh2 = (erow == e2).astype(F32)
    both = oh1 + oh2
    prefix = jnp.dot(both.astype(BF16), tri_ref[...], preferred_element_type=F32) + carry[:, 0:1]
    rank1 = jnp.sum(oh1 * prefix, axis=0, keepdims=True).astype(jnp.int32)
    rank2 = jnp.sum(oh2 * prefix, axis=0, keepdims=True).astype(jnp.int32)
    carry[...] = carry[...] + jnp.sum(both, axis=1, keepdims=True)
    zi = jnp.zeros((1, tm), jnp.int32)
    ids_ref[...] = jnp.concatenate([e1, e2, rank1, rank2, zi, zi, zi, zi], axis=0)
    wrow = lax.broadcasted_iota(jnp.int32, (LANE, tm), 0)
    wts_ref[...] = jnp.where(wrow == 0, p1 * g_w, jnp.where(wrow == 1, ex * p1 * g_w, 0.0)).T
    cnt_ref[...] = carry[...]


def _router(x2, mod3, w_route, b_route, tri_excl, seq):
    t, d = x2.shape
    tm = TM_PROJ
    tpb = seq // tm
    nr = w_route.shape[0]
    const = lambda shape: pl.BlockSpec(shape, lambda i: (0,) * len(shape))
    modspec = lambda k: pl.BlockSpec((None, 1, d), lambda i: ((i // tpb) * 6 + k, 0, 0))
    return pl.pallas_call(
        _router_kernel,
        grid=(t // tm,),
        in_specs=[pl.BlockSpec((tm, d), lambda i: (i, 0)), modspec(4), modspec(3),
                  const((nr, d)), const((nr, LANE)), const((tm, tm))],
        out_specs=[pl.BlockSpec((N_SLAB, tm, SLAB), lambda i: (0, i, 0)),
                   pl.BlockSpec((8, tm), lambda i: (0, i)),
                   pl.BlockSpec((tm, LANE), lambda i: (i, 0)),
                   const((MOE_EXPERTS, LANE))],
        out_shape=[jax.ShapeDtypeStruct((N_SLAB, t, SLAB), F32),
                   jax.ShapeDtypeStruct((8, t), jnp.int32),
                   jax.ShapeDtypeStruct((t, LANE), F32),
                   jax.ShapeDtypeStruct((MOE_EXPERTS, LANE), F32)],
        scratch_shapes=[pltpu.VMEM((MOE_EXPERTS, LANE), F32)],
        compiler_params=_cparams(("arbitrary",)),
        name="moe_router",
    )(x2, mod3, mod3, w_route, b_route, tri_excl)


def _sc_mesh():
    return plsc.VectorSubcoreMesh(core_axis_name="core", subcore_axis_name="subcore")


def _slab_rows(idx, n_rows):
    return (idx[None, :] + (jnp.arange(N_SLAB, dtype=jnp.int32) * n_rows)[:, None]).reshape(-1)


def _dispatch(slot1, slot2, h_slabs):
    n_slab, t, d = h_slabs.shape
    n_out = 2 * t
    xs = _scatter_rows(h_slabs.reshape(n_slab * t, d), _slab_rows(slot1, n_out), _slab_rows(slot2, n_out),
                       n_slab * n_out)
    return xs.reshape(n_slab, n_out, d)


def _scatter_rows(src, idx1, idx2, n_out):
    t, d = src.shape
    win = SC_WINDOW

    @pl.kernel(out_type=jax.ShapeDtypeStruct((n_out, d), src.dtype), mesh=_sc_mesh(), name="moe_dispatch_sc")
    def scatter_rows(x_hbm, i1_hbm, i2_hbm, o_hbm):
        def body(x_vmem, i1_vmem, i2_vmem):
            pltpu.sync_copy(x_vmem, o_hbm.at[i1_vmem.at[0]])
            pltpu.sync_copy(x_vmem, o_hbm.at[i2_vmem.at[0]])

        pltpu.emit_pipeline(
            body,
            grid=(t // win,),
            in_specs=[pl.BlockSpec((win, d), lambda i: (i, 0)),
                      pl.BlockSpec((1, win), lambda i: (0, i)),
                      pl.BlockSpec((1, win), lambda i: (0, i))],
            out_specs=[],
            core_axis_name=("core", "subcore"),
            dimension_semantics=(pltpu.PARALLEL,),
        )(x_hbm, i1_hbm, i2_hbm)

    return scatter_rows(src, idx1.reshape(1, t), idx2.reshape(1, t))


def _gather_rows(src, idx):
    m = idx.shape[0]
    d = src.shape[1]
    win = SC_WINDOW

    @pl.kernel(out_type=jax.ShapeDtypeStruct((m, d), src.dtype), mesh=_sc_mesh(), name="moe_gather_sc")
    def gather(x_hbm, i_hbm, o_hbm):
        def body(i_vmem, o_vmem):
            pltpu.sync_copy(x_hbm.at[i_vmem.at[0]], o_vmem)

        pltpu.emit_pipeline(
            body,
            grid=(m // win,),
            in_specs=[pl.BlockSpec((1, win), lambda i: (0, i))],
            out_specs=[pl.BlockSpec((win, d), lambda i: (i, 0))],
            core_axis_name=("core", "subcore"),
            dimension_semantics=(pltpu.PARALLEL,),
        )(i_hbm, o_hbm)

    return gather(src, idx.reshape(1, m))


def _expert_kernel(tile_ref, exp_ref, lo_ref, hi_ref, xs_ref, w1_ref, w3_ref, w2_ref, ys_ref, w1_scr, w3_scr, w2_scr):
    s = pl.program_id(0)
    prev = jnp.maximum(s - 1, 0)
    new_expert = (s == 0) | (exp_ref[s] != exp_ref[prev])
    new_tile = (s == 0) | (tile_ref[s] != tile_ref[prev])

    @pl.when(new_expert)
    def _():
        w1_scr[...] = w1_ref[...].astype(BF16)
        w3_scr[...] = w3_ref[...].astype(BF16)
        w2_scr[...] = w2_ref[...].astype(BF16)

    x = jnp.concatenate([xs_ref[k] for k in range(N_SLAB)], axis=-1).astype(BF16)
    a = jnp.dot(x, w1_scr[...], preferred_element_type=F32)
    b = jnp.dot(x, w3_scr[...], preferred_element_type=F32)
    act = _silu(a) * b
    y = jnp.dot(act.astype(BF16), w2_scr[...], preferred_element_type=F32)
    row = lax.broadcasted_iota(jnp.int32, (y.shape[0], SLAB), 0)
    mine = (row >= lo_ref[s]) & (row < hi_ref[s])

    @pl.when(new_tile)
    def _():
        for k in range(N_SLAB):
            ys_ref[k] = jnp.where(mine, y[:, k * SLAB:(k + 1) * SLAB], 0.0)

    @pl.when(jnp.logical_not(new_tile))
    def _():
        for k in range(N_SLAB):
            ys_ref[k] = jnp.where(mine, y[:, k * SLAB:(k + 1) * SLAB], ys_ref[k])


def _experts(step_tile, step_expert, step_lo, step_hi, xs, w1, w3, w2, layer):
    n_slab, ns, slab = xs.shape
    d = n_slab * slab
    ff = w1.shape[2]
    n_steps = step_tile.shape[0]
    base = layer * MOE_EXPERTS
    grid_spec = pltpu.PrefetchScalarGridSpec(
        num_scalar_prefetch=4,
        grid=(n_steps,),
        in_specs=[pl.BlockSpec((n_slab, TM_X, slab), lambda s, tl, ex, lo, hi: (0, tl[s], 0)),
                  pl.BlockSpec((None, d, ff), lambda s, tl, ex, lo, hi: (base + ex[s], 0, 0)),
                  pl.BlockSpec((None, d, ff), lambda s, tl, ex, lo, hi: (base + ex[s], 0, 0)),
                  pl.BlockSpec((None, ff, d), lambda s, tl, ex, lo, hi: (base + ex[s], 0, 0))],
        out_specs=pl.BlockSpec((n_slab, TM_X, slab), lambda s, tl, ex, lo, hi: (0, tl[s], 0)),
        scratch_shapes=[pltpu.VMEM((d, ff), BF16), pltpu.VMEM((d, ff), BF16), pltpu.VMEM((ff, d), BF16)],
    )
    return pl.pallas_call(
        _expert_kernel,
        grid_spec=grid_spec,
        out_shape=jax.ShapeDtypeStruct((n_slab, ns, slab), F32),
        compiler_params=_cparams(("arbitrary",)),
        name="moe_experts",
    )(step_tile, step_expert, step_lo, step_hi, xs, w1, w3, w2)


def _combine_kernel(x_ref, gate_ref, fw_ref, wcol_ref, y1_ref, y2_ref, o_ref, *, final):
    w_first = wcol_ref[:, 0:1]
    w_second = wcol_ref[:, 1:2]
    moe = jnp.concatenate([w_first * y1_ref[k] + w_second * y2_ref[k] for k in range(N_SLAB)], axis=-1)
    x = x_ref[...] + gate_ref[...] * moe
    if final:
        x = x * lax.rsqrt(jnp.mean(x * x, axis=-1, keepdims=True) + EPS) * fw_ref[...]
    o_ref[...] = x


def _combine(x2, mod3, final_w_row, wcol, gathered, seq, final):
    t, d = x2.shape
    tm = TM_COMB
    tpb = seq // tm
    nblk = t // tm
    yspec = lambda off: pl.BlockSpec((N_SLAB, tm, SLAB), lambda i: (0, i + off, 0))
    return pl.pallas_call(
        functools.partial(_combine_kernel, final=final),
        grid=(nblk,),
        in_specs=[pl.BlockSpec((tm, d), lambda i: (i, 0)),
                  pl.BlockSpec((None, 1, d), lambda i: ((i // tpb) * 6 + 5, 0, 0)),
                  pl.BlockSpec((1, d), lambda i: (0, 0)),
                  pl.BlockSpec((tm, LANE), lambda i: (i, 0)),
                  yspec(0), yspec(nblk)],
        out_specs=pl.BlockSpec((tm, d), lambda i: (i, 0)),
        out_shape=jax.ShapeDtypeStruct((t, d), F32),
        compiler_params=_cparams(("parallel",)),
        name="moe_combine",
    )(x2, mod3, final_w_row, wcol, gathered, gathered)


def _moe(x2, mod3, final_w_row, w_route, b_route, tri_excl, w1, w3, w2, layer, seq, final):
    t, d = x2.shape
    h3, ids, wcol, counts = _router(x2, mod3, w_route, b_route, tri_excl, seq)
    cnt = counts[:, 0].astype(jnp.int32)
    ends = jnp.cumsum(cnt)
    offs = ends - cnt
    experts = jnp.arange(MOE_EXPERTS, dtype=jnp.int32)
    pick = lambda table, idx: jnp.sum(jnp.where(idx[:, None] == experts[None, :], table[None, :], 0), axis=1)
    slot1 = pick(offs, ids[0]) + ids[2]
    slot2 = pick(offs, ids[1]) + ids[3]
    n_tiles = 2 * t // TM_X
    first_tile = offs // TM_X
    n_vis = jnp.where(cnt > 0, (ends - 1) // TM_X - first_tile + 1, 0)
    cum = jnp.cumsum(n_vis)
    step = jnp.arange(n_tiles + MOE_EXPERTS, dtype=jnp.int32)
    step_expert = jnp.minimum(jnp.sum(step[:, None] >= cum[None, :], axis=1), MOE_EXPERTS - 1).astype(jnp.int32)
    valid = step < cum[-1]
    step_tile = jnp.where(valid, pick(first_tile - (cum - n_vis), step_expert) + step, n_tiles - 1)
    step_lo = jnp.where(valid, jnp.clip(pick(offs, step_expert) - step_tile * TM_X, 0, TM_X), 0)
    step_hi = jnp.where(valid, jnp.clip(pick(ends, step_expert) - step_tile * TM_X, 0, TM_X), 0)
    xs = _dispatch(slot1, slot2, h3)
    ys = _experts(step_tile.astype(jnp.int32), step_expert, step_lo.astype(jnp.int32), step_hi.astype(jnp.int32),
                  xs, w1, w3, w2, layer)
    n_sorted = ys.shape[1]
    gathered = _gather_rows(ys.reshape(N_SLAB * n_sorted, SLAB), _slab_rows(jnp.concatenate([slot1, slot2]), n_sorted))
    gathered = gathered.reshape(N_SLAB, n_sorted, SLAB)
    return _combine(x2, mod3, final_w_row, wcol, gathered, seq, final)


def kernel(x, c, positions, ada_w, ada_b, w_in, s5_lam_re, s5_lam_im, s5_b_re, s5_b_im, s5_c_re, s5_c_im, s5_d, s5_log_dt, s5_w_glu, hg_lb_logits, hg_norm_w, m2_conv_w, m2_conv_b, m2_dt_bias, m2_a_log, m2_d, m2_norm_w, w_branch, w_gate, b_gate, w_out, moe_w_group, moe_b_group, moe_w_expert, moe_b_expert, moe_w1, moe_w3, moe_w2, final_norm_w):
    bsz, seq, d = x.shape
    t = bsz * seq
    depth = ada_w.shape[0]
    assert seq % TM_PROJ == 0 and seq % C_RET == 0 and seq % C_SSD == 0 and seq % C_HG == 0
    x2 = x.reshape(t, d).astype(F32)

    c_pad = jnp.zeros((8, d), F32).at[:bsz].set(c.astype(F32))
    mod_all = _ada_mod(c_pad, ada_w.astype(F32), ada_b.astype(F32))

    half = RET_DK // 2
    inv_freq = ROPE_BASE ** (-jnp.arange(half, dtype=F32) / half)
    invf_row = jnp.tile(inv_freq, 2 * RET_HEADS)[None, :]
    sgn_row = jnp.asarray(np.tile(np.concatenate([-np.ones(half), np.ones(half)]), RET_HEADS)[None, :], F32)
    cos_t, sin_t = _rope_tables(positions.reshape(t, 1).astype(jnp.int32), invf_row, sgn_row)
    cos3 = cos_t.reshape(bsz, seq, -1)
    sin3 = sin_t.reshape(bsz, seq, -1)

    lb_cum = jnp.cumsum(jax.nn.softmax(hg_lb_logits.astype(F32), axis=0), axis=0)
    hg_lb = lb_cum - lb_cum[:1]
    tri_ssd = jnp.asarray(np.tril(np.ones((C_SSD, C_SSD), np.float32)))
    tri_excl = jnp.asarray(np.triu(np.ones((TM_PROJ, TM_PROJ), np.float32), 1), BF16)
    final_w_row = final_norm_w.astype(F32)[None, :]
    w_pad = jnp.zeros((depth, d, IN_W_PAD), BF16).at[:, :, :IN_W].set(w_in.astype(BF16))
    w_glu_bf = s5_w_glu.astype(BF16)
    w_branch_bf = w_branch.astype(BF16)
    w_gate_bf = w_gate.astype(BF16)
    w_out_bf = w_out.astype(BF16)
    b_gate3 = b_gate.astype(F32).reshape(depth, 1, -1)
    moe_w1_all = moe_w1.astype(F32).reshape(depth * MOE_EXPERTS, d, MOE_FF)
    moe_w3_all = moe_w3.astype(F32).reshape(depth * MOE_EXPERTS, d, MOE_FF)
    moe_w2_all = moe_w2.astype(F32).reshape(depth * MOE_EXPERTS, MOE_FF, d)

    for layer in range(depth):
        mod3 = mod_all[layer, :bsz].reshape(bsz * 6, 1, d)
        p = _in_proj(x2, mod3, w_pad, layer, seq)
        p3 = p.reshape(bsz, seq, IN_W_PAD)

        ops = _s5_operators(s5_lam_re[layer], s5_lam_im[layer], s5_b_re[layer], s5_b_im[layer],
                            s5_c_re[layer], s5_c_im[layer], s5_d[layer], s5_log_dt[layer])
        y_s5 = _s5_scan(p3, *ops).reshape(t, BRANCH_W)

        lb = hg_lb[layer][None, :]
        y_hg = _hgrn2(p3, jnp.log(lb), jnp.log1p(-lb), hg_norm_w[layer].astype(F32)[None, :]).reshape(t, BRANCH_W)

        y_ret = _retention(p3, cos3, sin3).reshape(t, BRANCH_W)

        pad8 = lambda v: jnp.zeros((1, LANE), F32).at[0, :M2_HEADS].set(v.astype(F32))
        y_m2 = _ssd(p3, tri_ssd, m2_conv_w[layer].astype(F32), m2_conv_b[layer].astype(F32)[None, :],
                    pad8(m2_dt_bias[layer]), pad8(m2_a_log[layer]),
                    jnp.repeat(m2_d[layer].astype(F32), M2_HEADDIM)[None, :],
                    m2_norm_w[layer].astype(F32)[None, :]).reshape(t, BRANCH_W)

        x2 = _merge(x2, mod3, y_s5, y_hg, y_ret, y_m2, w_glu_bf, w_branch_bf, w_gate_bf, b_gate3, w_out_bf,
                    layer, seq)

        nr = 40
        w_route = jnp.zeros((nr, d), F32).at[:MOE_GROUPS].set(moe_w_group[layer].astype(F32).T)
        w_route = w_route.at[MOE_GROUPS:MOE_GROUPS + MOE_EXPERTS].set(moe_w_expert[layer].astype(F32).T)
        b_route = jnp.zeros((nr, LANE), F32).at[:MOE_GROUPS, 0].set(moe_b_group[layer].astype(F32))
        b_route = b_route.at[MOE_GROUPS:MOE_GROUPS + MOE_EXPERTS, 0].set(moe_b_expert[layer].astype(F32))
        x2 = _moe(x2, mod3, final_w_row, w_route, b_route, tri_excl, moe_w1_all, moe_w3_all, moe_w2_all,
                  layer, seq, final=(layer == depth - 1))
    return x2.reshape(bsz, seq, d)
```

```python
import functools
import math

import numpy as np
import jax
import jax.numpy as jnp
from jax import lax
from jax.experimental import pallas as pl
from jax.experimental.pallas import tpu as pltpu
from jax.experimental.pallas import tpu_sc as plsc

F32 = jnp.float32
BF16 = jnp.bfloat16
HIGHEST = lax.Precision.HIGHEST

D_MODEL = 1024
DEPTH = 2
BRANCH_W = 512
EPS = 1e-6
S5_GROUPS = 32
S5_CH = 16
S5_STATE = 64
S5_MAX_REAL = -1e-4
S5_BLOCK = 16
HG_HEADS = 4
HG_DK = 128
RET_HEADS = 4
RET_DK = 64
RET_DV = 128
ROPE_BASE = 10000.0
M2_HEADS = 8
M2_HEADDIM = 64
M2_GROUPS = 2
M2_STATE = 128
M2_CONV = 4
MOE_GROUPS = 4
MOE_EPG = 8
MOE_EXPERTS = MOE_GROUPS * MOE_EPG
MOE_FF = 256

COL_S5, COL_HQ, COL_HF, COL_HI, COL_HG = 0, 512, 1024, 1536, 2048
COL_RQ, COL_RK, COL_RV, COL_RG = 2560, 2816, 3072, 3584
COL_MZ, COL_MXS, COL_MBC, COL_MDT = 4096, 4608, 5120, 5632
IN_W = 5640
IN_W_PAD = 5760

LANE = 128
VMEM_LIMIT = 56 * 1024 * 1024

TM_PROJ = 1024
TN_PROJ = 1152
TM_MERGE = 512
C_RET = 256
C_SSD = 256
C_HG = 128
TM_X = 256
TM_COMB = 512
SC_WINDOW = 128
SLAB = 256
N_SLAB = D_MODEL // 2 // SLAB


def _cparams(sem):
    return pltpu.CompilerParams(dimension_semantics=sem, vmem_limit_bytes=VMEM_LIMIT)


def _silu(v):
    return v * jax.nn.sigmoid(v)


def _dot_nt(a, b, **kw):
    return lax.dot_general(a, b, (((1,), (1,)), ((), ())), preferred_element_type=F32, **kw)


def _dot_tn(a, b, **kw):
    return lax.dot_general(a, b, (((0,), (0,)), ((), ())), preferred_element_type=F32, **kw)


def _ada_kernel(c_ref, w_ref, b_ref, o_ref):
    cond = _silu(c_ref[...])
    o_ref[...] = jnp.dot(cond, w_ref[...], preferred_element_type=F32, precision=HIGHEST) + b_ref[...]


def _ada_mod(c_pad, ada_w, ada_b):
    depth, d, n = ada_w.shape
    tn = 1536
    return pl.pallas_call(
        _ada_kernel,
        grid=(depth, n // tn),
        in_specs=[pl.BlockSpec((8, d), lambda l, j: (0, 0)),
                  pl.BlockSpec((None, d, tn), lambda l, j: (l, 0, j)),
                  pl.BlockSpec((None, 1, tn), lambda l, j: (l, 0, j))],
        out_specs=pl.BlockSpec((None, 8, tn), lambda l, j: (l, 0, j)),
        out_shape=jax.ShapeDtypeStruct((depth, 8, n), F32),
        compiler_params=_cparams(("parallel", "parallel")),
        name="ada_mod",
    )(c_pad, ada_w, ada_b.reshape(depth, 1, n))


def _pack_bf16_pairs(x):
    n = x.shape[1] // 2
    lo = pltpu.bitcast(x[:, :n].astype(BF16).astype(F32), jnp.uint32) >> 16
    hi = pltpu.bitcast(x[:, n:].astype(BF16).astype(F32), jnp.uint32)
    return hi | lo


def _unpack_bf16_pairs(w):
    lo = pltpu.bitcast(w << 16, F32)
    hi = pltpu.bitcast(w & jnp.uint32(0xFFFF0000), F32)
    return jnp.concatenate([lo, hi], axis=-1)


def _modulated_norm(x, scale, shift):
    ms = jnp.mean(x * x, axis=-1, keepdims=True)
    return x * lax.rsqrt(ms + EPS) * (1.0 + scale) + shift


def _inproj_kernel(x_ref, sc_ref, sh_ref, w_ref, o_ref, u_ref, h_scr):
    first = pl.program_id(1) == 0

    @pl.when(first)
    def _():
        h_scr[...] = _modulated_norm(x_ref[...], sc_ref[...], sh_ref[...]).astype(BF16)

    p = jnp.dot(h_scr[...], w_ref[...], preferred_element_type=F32)
    o_ref[...] = p.astype(o_ref.dtype)

    @pl.when(first)
    def _():
        u_ref[...] = p[:, COL_S5:COL_S5 + BRANCH_W]


def _in_proj(x2, mod3, w_pad, layer, seq):
    t, d = x2.shape
    tpb = seq // TM_PROJ
    return pl.pallas_call(
        _inproj_kernel,
        grid=(t // TM_PROJ, IN_W_PAD // TN_PROJ),
        in_specs=[pl.BlockSpec((TM_PROJ, d), lambda i, j: (i, 0)),
                  pl.BlockSpec((None, 1, d), lambda i, j: ((i // tpb) * 6 + 1, 0, 0)),
                  pl.BlockSpec((None, 1, d), lambda i, j: ((i // tpb) * 6 + 0, 0, 0)),
                  pl.BlockSpec((None, d, TN_PROJ), lambda i, j: (layer, 0, j))],
        out_specs=[pl.BlockSpec((TM_PROJ, TN_PROJ), lambda i, j: (i, j)),
                   pl.BlockSpec((TM_PROJ, BRANCH_W), lambda i, j: (i, 0))],
        out_shape=[jax.ShapeDtypeStruct((t, IN_W_PAD), BF16), jax.ShapeDtypeStruct((t, BRANCH_W), F32)],
        scratch_shapes=[pltpu.VMEM((TM_PROJ, d), BF16)],
        compiler_params=_cparams(("parallel", "arbitrary")),
        name="in_proj",
    )(x2, mod3, mod3, w_pad)


def _rope_kernel(pos_ref, invf_ref, ecos_ref, esin_ref, cos_ref, sin_ref):
    ang = invf_ref[:, 0:1] * pos_ref[...].astype(F32)
    cos_ref[...] = _dot_tn(jnp.cos(ang), ecos_ref[...], precision=HIGHEST)
    sin_ref[...] = _dot_tn(jnp.sin(ang), esin_ref[...], precision=HIGHEST)


def _rope_tables(pos_row, invf_col, expand_cos, expand_sin):
    t = pos_row.shape[1]
    half, w = expand_cos.shape
    tm = 1024
    const = lambda shape: pl.BlockSpec(shape, lambda i: (0, 0))
    return pl.pallas_call(
        _rope_kernel,
        grid=(t // tm,),
        in_specs=[pl.BlockSpec((1, tm), lambda i: (0, i)), const((half, LANE)), const((half, w)), const((half, w))],
        out_specs=[pl.BlockSpec((tm, w), lambda i: (i, 0))] * 2,
        out_shape=[jax.ShapeDtypeStruct((t, w), F32)] * 2,
        compiler_params=_cparams(("parallel",)),
        name="rope_tables",
    )(pos_row, invf_col, expand_cos, expand_sin)


def _ret_kernel(q_ref, k_ref, v_ref, g_ref, cos_ref, sin_ref, o_ref, st_ref, *, chunk):
    @pl.when(pl.program_id(1) == 0)
    def _():
        st_ref[...] = jnp.zeros_like(st_ref)

    cosf = cos_ref[...]
    sinf = sin_ref[...]
    width = RET_HEADS * RET_DK
    lane = lax.broadcasted_iota(jnp.int32, (chunk, width), 1)
    first_half = (lane % RET_DK) < (RET_DK // 2)

    def rope(t):
        partner = jnp.where(first_half, pltpu.roll(t, width - RET_DK // 2, 1), pltpu.roll(t, RET_DK // 2, 1))
        return t * cosf + partner * sinf

    q = rope(q_ref[...].astype(F32))
    k = rope(k_ref[...].astype(F32)) * (RET_DK ** -0.5)
    v = v_ref[...]
    g = g_ref[...].astype(F32)
    ti = lax.broadcasted_iota(jnp.int32, (chunk, chunk), 0)
    si = lax.broadcasted_iota(jnp.int32, (chunk, chunk), 1)
    lag = (ti - si).astype(F32)
    tcol = lax.broadcasted_iota(jnp.int32, (chunk, 1), 0).astype(F32)
    for h in range(RET_HEADS):
        log_gamma = math.log1p(-(2.0 ** (-5.0 - h)))
        qh = q[:, h * RET_DK:(h + 1) * RET_DK]
        kh = k[:, h * RET_DK:(h + 1) * RET_DK]
        vh = v[:, h * RET_DV:(h + 1) * RET_DV].astype(BF16)
        decay = jnp.where(ti >= si, jnp.exp(jnp.minimum(lag * log_gamma, 0.0)), 0.0)
        scores = _dot_nt(qh.astype(BF16), kh.astype(BF16)) * decay
        state = st_ref[h]
        q_in = qh * jnp.exp(log_gamma * (tcol + 1.0))
        o = (jnp.dot(scores.astype(BF16), vh, preferred_element_type=F32)
             + jnp.dot(q_in.astype(BF16), state.astype(BF16), preferred_element_type=F32))
        k_out = kh * jnp.exp(log_gamma * (chunk - 1.0 - tcol))
        st_ref[h] = math.exp(log_gamma * chunk) * state + _dot_tn(k_out.astype(BF16), vh)
        o = o * lax.rsqrt(jnp.mean(o * o, axis=-1, keepdims=True) + EPS)
        gh = g[:, h * RET_DV:(h + 1) * RET_DV]
        o_ref[:, h * RET_DV:(h + 1) * RET_DV] = (o * _silu(gh)).astype(o_ref.dtype)


def _retention(p3, cos3, sin3):
    b, seq, _ = p3.shape
    c = C_RET
    qk_w = RET_HEADS * RET_DK
    return pl.pallas_call(
        functools.partial(_ret_kernel, chunk=c),
        grid=(b, seq // c),
        in_specs=[pl.BlockSpec((None, c, qk_w), lambda i, j: (i, j, COL_RQ // qk_w)),
                  pl.BlockSpec((None, c, qk_w), lambda i, j: (i, j, COL_RK // qk_w)),
                  pl.BlockSpec((None, c, BRANCH_W), lambda i, j: (i, j, COL_RV // BRANCH_W)),
                  pl.BlockSpec((None, c, BRANCH_W), lambda i, j: (i, j, COL_RG // BRANCH_W)),
                  pl.BlockSpec((None, c, qk_w), lambda i, j: (i, j, 0)),
                  pl.BlockSpec((None, c, qk_w), lambda i, j: (i, j, 0))],
        out_specs=pl.BlockSpec((None, c, BRANCH_W), lambda i, j: (i, j, 0)),
        out_shape=jax.ShapeDtypeStruct((b, seq, BRANCH_W), BF16),
        scratch_shapes=[pltpu.VMEM((RET_HEADS, RET_DK, RET_DV), F32)],
        compiler_params=_cparams(("parallel", "arbitrary")),
        name="retention",
    )(p3, p3, p3, p3, cos3, sin3)


def _ssd_kernel(z_ref, xs_ref, bc_ref, dt_ref, tri_ref, cw_ref, cb_ref, dtb_ref, alog_ref, dsk_ref, nw_ref,
                o_ref, xe_scr, st_ref, *, chunk):
    j = pl.program_id(1)
    width = 2 * BRANCH_W

    @pl.when(j == 0)
    def _():
        st_ref[...] = jnp.zeros_like(st_ref)
        xe_scr[0:8, :] = jnp.zeros((8, width), F32)

    @pl.when(j > 0)
    def _():
        xe_scr[0:8, :] = xe_scr[chunk:chunk + 8, :]

    xe_scr[8:, 0:BRANCH_W] = xs_ref[...].astype(F32)
    xe_scr[8:, BRANCH_W:] = bc_ref[...].astype(F32)
    conv = cb_ref[...] + cw_ref[M2_CONV - 1:M2_CONV, :] * xe_scr[8:, :]
    for tap in range(M2_CONV - 1):
        conv = conv + cw_ref[tap:tap + 1, :] * xe_scr[pl.ds(8 - (M2_CONV - 1) + tap, chunk), :]
    conv = _silu(conv)
    xs = conv[:, :BRANCH_W]
    bm = conv[:, BRANCH_W:BRANCH_W + M2_GROUPS * M2_STATE]
    cm = conv[:, BRANCH_W + M2_GROUPS * M2_STATE:]

    dt = jax.nn.softplus(dt_ref[...].astype(F32) + dtb_ref[...])
    da = dt * (-jnp.exp(alog_ref[...]))
    a_cs = jnp.dot(tri_ref[...], da, preferred_element_type=F32, precision=HIGHEST)
    a_cs_t = a_cs.T
    ti = lax.broadcasted_iota(jnp.int32, (chunk, chunk), 0)
    si = lax.broadcasted_iota(jnp.int32, (chunk, chunk), 1)
    causal = ti >= si
    hpg = M2_HEADS // M2_GROUPS
    ys = []
    for grp in range(M2_GROUPS):
        bm_g = bm[:, grp * M2_STATE:(grp + 1) * M2_STATE]
        cm_g = cm[:, grp * M2_STATE:(grp + 1) * M2_STATE]
        cb = _dot_nt(cm_g.astype(BF16), bm_g.astype(BF16))
        for hh in range(hpg):
            h = grp * hpg + hh
            col = a_cs[:, h:h + 1]
            row = a_cs_t[h:h + 1, :]
            lmat = jnp.where(causal, jnp.exp(jnp.minimum(col - row, 0.0)), 0.0)
            xd = xs[:, h * M2_HEADDIM:(h + 1) * M2_HEADDIM] * dt[:, h:h + 1]
            state = st_ref[h]
            y = (jnp.dot((cb * lmat).astype(BF16), xd.astype(BF16), preferred_element_type=F32)
                 + jnp.dot((cm_g * jnp.exp(col)).astype(BF16), state.astype(BF16), preferred_element_type=F32))
            a_last = a_cs[chunk - 1:chunk, h:h + 1]
            to_end = jnp.exp(a_last - col)
            st_ref[h] = jnp.exp(a_last) * state + _dot_tn(bm_g.astype(BF16), (xd * to_end).astype(BF16))
            ys.append(y)
    y = jnp.concatenate(ys, axis=-1) + dsk_ref[...] * xs
    y = y * _silu(z_ref[...].astype(F32))
    o_ref[...] = (y * lax.rsqrt(jnp.mean(y * y, axis=-1, keepdims=True) + EPS) * nw_ref[...]).astype(o_ref.dtype)


def _ssd(p3, tri, conv_w, conv_b, dt_bias_row, a_log_row, d_skip_row, norm_w_row):
    b, seq, _ = p3.shape
    c = C_SSD
    const = lambda shape: pl.BlockSpec(shape, lambda i, j: (0,) * len(shape))
    return pl.pallas_call(
        functools.partial(_ssd_kernel, chunk=c),
        grid=(b, seq // c),
        in_specs=[pl.BlockSpec((None, c, BRANCH_W), lambda i, j: (i, j, COL_MZ // BRANCH_W)),
                  pl.BlockSpec((None, c, BRANCH_W), lambda i, j: (i, j, COL_MXS // BRANCH_W)),
                  pl.BlockSpec((None, c, BRANCH_W), lambda i, j: (i, j, COL_MBC // BRANCH_W)),
                  pl.BlockSpec((None, c, LANE), lambda i, j: (i, j, COL_MDT // LANE)),
                  const((c, c)), const((M2_CONV, 2 * BRANCH_W)), const((1, 2 * BRANCH_W)),
                  const((1, LANE)), const((1, LANE)), const((1, BRANCH_W)), const((1, BRANCH_W))],
        out_specs=pl.BlockSpec((None, c, BRANCH_W), lambda i, j: (i, j, 0)),
        out_shape=jax.ShapeDtypeStruct((b, seq, BRANCH_W), BF16),
        scratch_shapes=[pltpu.VMEM((c + 8, 2 * BRANCH_W), F32),
                        pltpu.VMEM((M2_HEADS, M2_STATE, M2_HEADDIM), F32)],
        compiler_params=_cparams(("parallel", "arbitrary")),
        name="ssd",
    )(p3, p3, p3, p3, tri, conv_w, conv_b, dt_bias_row, a_log_row, d_skip_row, norm_w_row)


def _hg_tables(chunk):
    n_lev = int(math.log2(chunk))
    r = np.arange(chunk)[:, None]
    jj = np.arange(chunk)[None, :]
    mats = [(jj <= r), (jj > r)]
    for lev in range(n_lev):
        m = 1 << lev
        mid = (r >> (lev + 1) << (lev + 1)) + m - 1
        second = ((r >> lev) & 1) == 1
        mats.append(np.where(second, (jj > mid) & (jj <= r), (jj > r) & (jj <= mid)))
    summat = np.concatenate(mats, axis=0).astype(np.float32)
    x = r ^ jj
    levmap = np.where(r > jj, np.floor(np.log2(x + 0.5)), np.where(r == jj, -1, -2)).astype(np.int32)
    return summat, levmap, n_lev


def _hg_kernel(q_ref, f_ref, i_ref, g_ref, sum_ref, lev_ref, llb_ref, l1m_ref, nw_ref, o_ref, st_ref,
               *, chunk, n_lev):
    @pl.when(pl.program_id(1) == 0)
    def _():
        st_ref[...] = jnp.zeros_like(st_ref)

    f = f_ref[...].astype(F32)
    log_sig = jnp.minimum(f, 0.0) - jnp.log1p(jnp.exp(-jnp.abs(f)))
    a = llb_ref[...]
    bb = l1m_ref[...] + log_sig
    log_f = jnp.maximum(a, bb) + jnp.log1p(jnp.exp(-jnp.abs(a - bb)))
    k_all = jnp.exp(l1m_ref[...]) * jax.nn.sigmoid(-f)
    q_all = _silu(q_ref[...].astype(F32))
    hi = log_f.astype(BF16)
    r1 = log_f - hi.astype(F32)
    mid = r1.astype(BF16)
    lo = (r1 - mid.astype(F32)).astype(BF16)
    summat = sum_ref[...]
    sums = (jnp.dot(summat, hi, preferred_element_type=F32)
            + jnp.dot(summat, mid, preferred_element_type=F32)
            + jnp.dot(summat, lo, preferred_element_type=F32))
    levmap = lev_ref[...]
    v_all = i_ref[...]
    g_all = g_ref[...].astype(F32)
    for h in range(HG_HEADS):
        sl = slice(h * HG_DK, (h + 1) * HG_DK)
        qh = q_all[:, sl]
        kh = k_all[:, sl]
        vh = v_all[:, sl].astype(BF16)
        b_h = sums[0:chunk, sl]
        to_end = sums[chunk:2 * chunk, sl]
        amat = jnp.where(levmap == -1, _dot_nt(qh.astype(BF16), kh.astype(BF16)), 0.0)
        for lev in range(n_lev):
            e = jnp.exp(sums[(2 + lev) * chunk:(3 + lev) * chunk, sl])
            a_l = _dot_nt((qh * e).astype(BF16), (kh * e).astype(BF16))
            amat = jnp.where(levmap == lev, a_l, amat)
        state_t = st_ref[h]
        o = (jnp.dot(amat.astype(BF16), vh, preferred_element_type=F32)
             + _dot_nt((qh * jnp.exp(b_h)).astype(BF16), state_t.astype(BF16)))
        k_end = kh * jnp.exp(to_end)
        st_ref[h] = jnp.exp(b_h[chunk - 1:chunk, :]) * state_t + _dot_tn(vh, k_end.astype(BF16))
        o = o * lax.rsqrt(jnp.mean(o * o, axis=-1, keepdims=True) + EPS) * nw_ref[...]
        o_ref[:, sl] = (o * _silu(g_all[:, sl])).astype(o_ref.dtype)


def _hgrn2(p3, log_lb_row, log1m_lb_row, norm_w_row):
    b, seq, _ = p3.shape
    c = C_HG
    summat, levmap, n_lev = _hg_tables(c)
    const = lambda shape: pl.BlockSpec(shape, lambda i, j: (0,) * len(shape))
    blk = lambda col: pl.BlockSpec((None, c, BRANCH_W), lambda i, j: (i, j, col // BRANCH_W))
    return pl.pallas_call(
        functools.partial(_hg_kernel, chunk=c, n_lev=n_lev),
        grid=(b, seq // c),
        in_specs=[blk(COL_HQ), blk(COL_HF), blk(COL_HI), blk(COL_HG),
                  const(summat.shape), const((c, c)),
                  const((1, BRANCH_W)), const((1, BRANCH_W)), const((1, HG_DK))],
        out_specs=pl.BlockSpec((None, c, BRANCH_W), lambda i, j: (i, j, 0)),
        out_shape=jax.ShapeDtypeStruct((b, seq, BRANCH_W), BF16),
        scratch_shapes=[pltpu.VMEM((HG_HEADS, HG_DK, HG_DK), F32)],
        compiler_params=_cparams(("parallel", "arbitrary")),
        name="hgrn2",
    )(p3, p3, p3, p3, jnp.asarray(summat, BF16), jnp.asarray(levmap), log_lb_row, log1m_lb_row, norm_w_row)


def _expand_block_diag(comp_ref, e_ref, dst_ref, row_div, lane_div):
    gq = LANE // S5_CH
    rows, ncols = dst_ref.shape
    step = 512
    comp = comp_ref[...]
    row_grp = (lax.broadcasted_iota(jnp.int32, (rows, step), 0) // row_div) % gq
    for c0 in range(0, ncols, step):
        lane_grp = ((lax.broadcasted_iota(jnp.int32, (rows, step), 1) + c0) // lane_div) % gq
        full = jnp.dot(comp, e_ref[:, c0:c0 + step], preferred_element_type=F32)
        dst_ref[:, c0:c0 + step] = jnp.where(row_grp == lane_grp, full, 0.0).astype(dst_ref.dtype)


def _s5_kernel(u_ref, k2_ref, bc_ref, cc_ref, esc_ref, eb_ref, lam_ref, o_ref, tc_scr, tq_ref, bq_ref, cq_ref,
               x_scr, w_scr, s_scr, *, rows):
    nb = S5_BLOCK

    @pl.when(pl.program_id(1) == 0)
    def _():
        k2 = k2_ref[...]
        lane = lax.broadcasted_iota(jnp.int32, k2.shape, 1)
        for t in range(nb):
            shifted = k2 if t == 0 else jnp.where(lane >= t * S5_CH, pltpu.roll(k2, t * S5_CH, 1), 0.0)
            tc_scr[t * LANE:(t + 1) * LANE, :] = shifted.astype(tc_scr.dtype)
        _expand_block_diag(tc_scr, esc_ref, tq_ref, S5_CH, S5_CH)
        _expand_block_diag(bc_ref, eb_ref, bq_ref, S5_CH, S5_STATE)
        _expand_block_diag(cc_ref, esc_ref, cq_ref, S5_STATE, S5_CH)

    for t in range(nb):
        x_scr[:, t * LANE:(t + 1) * LANE] = u_ref[pl.ds(t, rows, stride=nb), :].astype(x_scr.dtype)
    x = x_scr[...]
    half = w_scr.shape[1] // 2
    w_scr[...] = jnp.dot(x, bq_ref[...], preferred_element_type=F32)
    lam_re = lam_ref[0:1, :]
    lam_im = lam_ref[1:2, :]

    def body(j, carry):
        s_re, s_im = carry
        s_scr[pl.ds(j, 1), 0:half] = s_re
        s_scr[pl.ds(j, 1), half:] = s_im
        w_re = w_scr[pl.ds(j, 1), 0:half]
        w_im = w_scr[pl.ds(j, 1), half:]
        return lam_re * s_re - lam_im * s_im + w_re, lam_re * s_im + lam_im * s_re + w_im

    zero = jnp.zeros((1, half), F32)
    lax.fori_loop(0, rows, body, (zero, zero))
    y = (jnp.dot(x, tq_ref[...], preferred_element_type=F32)
         + jnp.dot(s_scr[...].astype(BF16), cq_ref[...], preferred_element_type=F32))
    for t in range(nb):
        o_ref[pl.ds(t, rows, stride=nb), :] = y[:, t * LANE:(t + 1) * LANE]


def _s5_scan(p3, k2, bc, cc, lam16):
    batch, seq, _ = p3.shape
    nb = S5_BLOCK
    nq = BRANCH_W // LANE
    rows = seq // nb
    kdim = nb * LANE
    gq = LANE // S5_CH
    ncol = 2 * gq * S5_STATE
    e_sc = (np.eye(nb)[:, None, :, None, None] * np.eye(S5_CH)[None, :, None, None, :] * np.ones((1, 1, 1, gq, 1)))
    e_sc = e_sc.reshape(nb * S5_CH, nb * gq * S5_CH)
    e_b = (np.eye(2)[:, None, :, None, None] * np.eye(S5_STATE)[None, :, None, None, :] * np.ones((1, 1, 1, gq, 1)))
    e_b = e_b.reshape(2 * S5_STATE, ncol)
    full = lambda shape: pl.BlockSpec(shape, lambda q, b: (0,) * len(shape))
    per_q = lambda r, c: pl.BlockSpec((None, r, c), lambda q, b: (q, 0, 0))
    return pl.pallas_call(
        functools.partial(_s5_kernel, rows=rows),
        grid=(nq, batch),
        in_specs=[pl.BlockSpec((None, seq, LANE), lambda q, b: (b, 0, q)),
                  per_q(LANE, nb * S5_CH), per_q(kdim, 2 * S5_STATE), per_q(ncol, nb * S5_CH),
                  full(e_sc.shape), full(e_b.shape), per_q(2, ncol // 2)],
        out_specs=pl.BlockSpec((None, seq, LANE), lambda q, b: (b, 0, q)),
        out_shape=jax.ShapeDtypeStruct((batch, seq, BRANCH_W), F32),
        scratch_shapes=[pltpu.VMEM((kdim, nb * S5_CH), BF16),
                        pltpu.VMEM((kdim, kdim), BF16), pltpu.VMEM((kdim, ncol), BF16), pltpu.VMEM((ncol, kdim), BF16),
                        pltpu.VMEM((rows, kdim), BF16), pltpu.VMEM((rows, ncol), F32), pltpu.VMEM((rows, ncol), F32)],
        compiler_params=_cparams(("parallel", "arbitrary")),
        name="s5_scan",
    )(p3, k2, bc, cc, jnp.asarray(e_sc, BF16), jnp.asarray(e_b, BF16), lam16)


def _s5_operators(lam_re, lam_im, b_re, b_im, c_re, c_im, d_skip, log_dt):
    nb = S5_BLOCK
    gq = LANE // S5_CH
    nq = S5_GROUPS // gq
    lam = lax.complex(jnp.minimum(lam_re.astype(F32), S5_MAX_REAL), lam_im.astype(F32))
    step = jnp.exp(log_dt.astype(F32))[:, None]
    z = lam * step
    lam_bar = jnp.exp(z)
    b_bar = ((lam_bar - 1.0) / lam)[..., None] * lax.complex(b_re.astype(F32), b_im.astype(F32))
    c_mat = lax.complex(c_re.astype(F32), c_im.astype(F32))
    pw = jnp.exp(z[..., None] * jnp.arange(nb + 1, dtype=F32))
    cp = c_mat[:, None, :, :] * pw[..., :nb].transpose(0, 2, 1)[:, :, None, :]
    cp = jnp.concatenate([cp.real, -cp.imag], axis=-1).reshape(S5_GROUPS, nb * S5_CH, 2 * S5_STATE)
    bri = jnp.concatenate([b_bar.real, b_bar.imag], axis=1)
    kern = jnp.einsum('gnk,gki->gin', cp, bri, precision=HIGHEST)
    skip = (jnp.asarray(np.concatenate([np.eye(S5_CH), np.zeros((S5_CH, (nb - 1) * S5_CH))], axis=1), F32)[None]
            * d_skip.astype(F32).reshape(S5_GROUPS, S5_CH, 1))
    k2 = (kern + skip).reshape(nq, gq * S5_CH, nb * S5_CH)
    pw_rev = jnp.exp(z[..., None] * jnp.asarray(np.arange(nb - 1, -1, -1), F32))
    binc = pw_rev[:, :, :, None] * b_bar[:, :, None, :]
    binc = jnp.stack([binc.real, binc.imag], axis=0).reshape(2, nq, gq, S5_STATE, nb, S5_CH)
    bc = binc.transpose(1, 4, 2, 5, 0, 3).reshape(nq, nb * LANE, 2 * S5_STATE)
    cm = c_mat.transpose(0, 2, 1)[:, :, None, :] * pw[..., 1:][:, :, :, None]
    cm = jnp.stack([cm.real, -cm.imag], axis=0).reshape(2, nq, gq * S5_STATE, nb * S5_CH)
    cc = cm.transpose(1, 0, 2, 3).reshape(nq, 2 * gq * S5_STATE, nb * S5_CH)
    lam_n = pw[..., nb].reshape(nq, gq * S5_STATE)
    lam16 = jnp.stack([lam_n.real, lam_n.imag], axis=1)
    return k2, bc.astype(BF16), cc.astype(BF16), lam16


def _merge_kernel(x_ref, sc_ref, sh_ref, gm_ref, ys5_ref, yhg_ref, yret_ref, ym2_ref,
                  wglu_ref, wbr_ref, wg_ref, bg_ref, wout_ref, o_ref):
    x = x_ref[...]
    d = x.shape[1]
    h = _modulated_norm(x, sc_ref[...], sh_ref[...]).astype(BF16)
    y_s5 = jax.nn.gelu(ys5_ref[...])
    y_s5 = y_s5 * jax.nn.sigmoid(jnp.dot(y_s5.astype(BF16), wglu_ref[...], preferred_element_type=F32))
    acc = jnp.zeros(x.shape, F32)
    for n, y in enumerate((y_s5, yhg_ref[...], yret_ref[...], ym2_ref[...])):
        gate = jax.nn.sigmoid(jnp.dot(h, wg_ref[:, n * d:(n + 1) * d], preferred_element_type=F32)
                              + bg_ref[:, n * d:(n + 1) * d])
        acc = acc + gate * jnp.dot(y.astype(BF16), wbr_ref[n], preferred_element_type=F32)
    o_ref[...] = x + gm_ref[...] * jnp.dot(acc.astype(BF16), wout_ref[...], preferred_element_type=F32)


def _merge(x2, mod3, ys5, yhg, yret, ym2, w_glu, w_branch, w_gate, b_gate, w_out, layer, seq):
    t, d = x2.shape
    tm = TM_MERGE
    tpb = seq // tm
    const = lambda shape: pl.BlockSpec((None,) + shape, lambda i: (layer,) + (0,) * len(shape))
    modspec = lambda k: pl.BlockSpec((None, 1, d), lambda i: ((i // tpb) * 6 + k, 0, 0))
    yspec = pl.BlockSpec((tm, BRANCH_W), lambda i: (i, 0))
    return pl.pallas_call(
        _merge_kernel,
        grid=(t // tm,),
        in_specs=[pl.BlockSpec((tm, d), lambda i: (i, 0)), modspec(1), modspec(0), modspec(2),
                  yspec, yspec, yspec, yspec,
                  const((BRANCH_W, BRANCH_W)), const((4, BRANCH_W, d)), const((d, 4 * d)), const((1, 4 * d)),
                  const((d, d))],
        out_specs=pl.BlockSpec((tm, d), lambda i: (i, 0)),
        out_shape=jax.ShapeDtypeStruct((t, d), F32),
        compiler_params=_cparams(("parallel",)),
        name="merge",
    )(x2, mod3, mod3, mod3, ys5, yhg, yret, ym2, w_glu, w_branch, w_gate, b_gate, w_out)


def _router_kernel(x_ref, sc_ref, sh_ref, wr_ref, br_ref, tri_ref, h_ref, ids_ref, wts_ref, cnt_ref, carry):
    i = pl.program_id(0)

    @pl.when(i == 0)
    def _():
        carry[...] = jnp.zeros_like(carry)

    h = _modulated_norm(x_ref[...], sc_ref[...], sh_ref[...])
    tm, d = h.shape
    packed = _pack_bf16_pairs(h)
    for k in range(N_SLAB):
        h_ref[k] = packed[:, k * SLAB:(k + 1) * SLAB]
    logits = _dot_nt(wr_ref[...], h, precision=HIGHEST) + br_ref[:, 0:1]
    gl = [logits[g:g + 1, :] for g in range(MOE_GROUPS)]
    gmax = gl[0]
    gsel = jnp.zeros((1, tm), jnp.int32)
    for g in range(1, MOE_GROUPS):
        better = gl[g] > gmax
        gsel = jnp.where(better, g, gsel)
        gmax = jnp.where(better, gl[g], gmax)
    gden = gl[0] * 0.0
    for g in range(MOE_GROUPS):
        gden = gden + jnp.exp(gl[g] - gmax)
    g_w = 1.0 / gden
    el = []
    for e in range(MOE_EPG):
        v = logits[MOE_GROUPS + e:MOE_GROUPS + e + 1, :]
        for g in range(1, MOE_GROUPS):
            row = MOE_GROUPS + g * MOE_EPG + e
            v = jnp.where(gsel == g, logits[row:row + 1, :], v)
        el.append(v)
    v1 = el[0]
    i1 = jnp.zeros((1, tm), jnp.int32)
    for e in range(1, MOE_EPG):
        better = el[e] > v1
        i1 = jnp.where(better, e, i1)
        v1 = jnp.where(better, el[e], v1)
    v2 = jnp.full((1, tm), -jnp.inf, F32)
    i2 = jnp.zeros((1, tm), jnp.int32)
    for e in range(MOE_EPG):
        better = (el[e] > v2) & (i1 != e)
        i2 = jnp.where(better, e, i2)
        v2 = jnp.where(better, el[e], v2)
    ex = jnp.exp(v2 - v1)
    p1 = 1.0 / (1.0 + ex)
    e1 = gsel * MOE_EPG + i1
    e2 = gsel * MOE_EPG + i2
    erow = lax.broadcasted_iota(jnp.int32, (MOE_EXPERTS, tm), 0)
    oh1 = (erow == e1).astype(F32)
    oh2 = (erow == e2).astype(F32)
    both = oh1 + oh2
    prefix = jnp.dot(both.astype(BF16), tri_ref[...], preferred_element_type=F32) + carry[:, 0:1]
    rank1 = jnp.sum(oh1 * prefix, axis=0, keepdims=True).astype(jnp.int32)
    rank2 = jnp.sum(oh2 * prefix, axis=0, keepdims=True).astype(jnp.int32)
    carry[...] = carry[...] + jnp.sum(both, axis=1, keepdims=True)
    zi = jnp.zeros((1, tm), jnp.int32)
    ids_ref[...] = jnp.concatenate([e1, e2, rank1, rank2, zi, zi, zi, zi], axis=0)
    wrow = lax.broadcasted_iota(jnp.int32, (LANE, tm), 0)
    wts_ref[...] = jnp.where(wrow == 0, p1 * g_w, jnp.where(wrow == 1, ex * p1 * g_w, 0.0)).T
    cnt_ref[...] = carry[...]


def _router(x2, mod3, w_route, b_route, tri_excl, seq):
    t, d = x2.shape
    tm = TM_PROJ
    tpb = seq // tm
    nr = w_route.shape[0]
    const = lambda shape: pl.BlockSpec(shape, lambda i: (0,) * len(shape))
    modspec = lambda k: pl.BlockSpec((None, 1, d), lambda i: ((i // tpb) * 6 + k, 0, 0))
    return pl.pallas_call(
        _router_kernel,
        grid=(t // tm,),
        in_specs=[pl.BlockSpec((tm, d), lambda i: (i, 0)), modspec(4), modspec(3),
                  const((nr, d)), const((nr, LANE)), const((tm, tm))],
        out_specs=[pl.BlockSpec((N_SLAB, tm, SLAB), lambda i: (0, i, 0)),
                   pl.BlockSpec((8, tm), lambda i: (0, i)),
                   pl.BlockSpec((tm, LANE), lambda i: (i, 0)),
                   const((MOE_EXPERTS, LANE))],
        out_shape=[jax.ShapeDtypeStruct((N_SLAB, t, SLAB), jnp.uint32),
                   jax.ShapeDtypeStruct((8, t), jnp.int32),
                   jax.ShapeDtypeStruct((t, LANE), F32),
                   jax.ShapeDtypeStruct((MOE_EXPERTS, LANE), F32)],
        scratch_shapes=[pltpu.VMEM((MOE_EXPERTS, LANE), F32)],
        compiler_params=_cparams(("arbitrary",)),
        name="moe_router",
    )(x2, mod3, mod3, w_route, b_route, tri_excl)


def _sc_mesh():
    return plsc.VectorSubcoreMesh(core_axis_name="core", subcore_axis_name="subcore")


def _slab_rows(idx, n_rows):
    return (idx[None, :] + (jnp.arange(N_SLAB, dtype=jnp.int32) * n_rows)[:, None]).reshape(-1)


def _dispatch(slot1, slot2, h_slabs):
    n_slab, t, d = h_slabs.shape
    n_out = 2 * t
    xs = _scatter_rows(h_slabs.reshape(n_slab * t, d), _slab_rows(slot1, n_out), _slab_rows(slot2, n_out),
                       n_slab * n_out)
    return xs.reshape(n_slab, n_out, d)


def _scatter_rows(src, idx1, idx2, n_out):
    t, d = src.shape
    win = SC_WINDOW

    @pl.kernel(out_type=jax.ShapeDtypeStruct((n_out, d), src.dtype), mesh=_sc_mesh(), name="moe_dispatch_sc")
    def scatter_rows(x_hbm, i1_hbm, i2_hbm, o_hbm):
        def body(x_vmem, i1_vmem, i2_vmem):
            pltpu.sync_copy(x_vmem, o_hbm.at[i1_vmem.at[0]])
            pltpu.sync_copy(x_vmem, o_hbm.at[i2_vmem.at[0]])

        pltpu.emit_pipeline(
            body,
            grid=(t // win,),
            in_specs=[pl.BlockSpec((win, d), lambda i: (i, 0)),
                      pl.BlockSpec((1, win), lambda i: (0, i)),
                      pl.BlockSpec((1, win), lambda i: (0, i))],
            out_specs=[],
            core_axis_name=("core", "subcore"),
            dimension_semantics=(pltpu.PARALLEL,),
        )(x_hbm, i1_hbm, i2_hbm)

    return scatter_rows(src, idx1.reshape(1, t), idx2.reshape(1, t))


def _gather_rows(src, idx):
    m = idx.shape[0]
    d = src.shape[1]
    win = SC_WINDOW

    @pl.kernel(out_type=jax.ShapeDtypeStruct((m, d), src.dtype), mesh=_sc_mesh(), name="moe_gather_sc")
    def gather(x_hbm, i_hbm, o_hbm):
        def body(i_vmem, o_vmem):
            pltpu.sync_copy(x_hbm.at[i_vmem.at[0]], o_vmem)

        pltpu.emit_pipeline(
            body,
            grid=(m // win,),
            in_specs=[pl.BlockSpec((1, win), lambda i: (0, i))],
            out_specs=[pl.BlockSpec((win, d), lambda i: (i, 0))],
            core_axis_name=("core", "subcore"),
            dimension_semantics=(pltpu.PARALLEL,),
        )(i_hbm, o_hbm)

    return gather(src, idx.reshape(1, m))


def _expert_kernel(tile_ref, exp_ref, lo_ref, hi_ref, xs_ref, w1_ref, w3_ref, w2_ref, ys_ref, w1_scr, w3_scr, w2_scr):
    s = pl.program_id(0)
    prev = jnp.maximum(s - 1, 0)
    new_expert = (s == 0) | (exp_ref[s] != exp_ref[prev])
    new_tile = (s == 0) | (tile_ref[s] != tile_ref[prev])

    @pl.when(new_expert)
    def _():
        w1_scr[...] = w1_ref[...].astype(BF16)
        w3_scr[...] = w3_ref[...].astype(BF16)
        w2_scr[...] = w2_ref[...].astype(BF16)

    x = _unpack_bf16_pairs(jnp.concatenate([xs_ref[k] for k in range(N_SLAB)], axis=-1)).astype(BF16)
    a = jnp.dot(x, w1_scr[...], preferred_element_type=F32)
    b = jnp.dot(x, w3_scr[...], preferred_element_type=F32)
    act = _silu(a) * b
    y = _pack_bf16_pairs(jnp.dot(act.astype(BF16), w2_scr[...], preferred_element_type=F32))
    row = lax.broadcasted_iota(jnp.int32, (y.shape[0], SLAB), 0)
    mine = (row >= lo_ref[s]) & (row < hi_ref[s])

    @pl.when(new_tile)
    def _():
        for k in range(N_SLAB):
            ys_ref[k] = jnp.where(mine, y[:, k * SLAB:(k + 1) * SLAB], jnp.uint32(0))

    @pl.when(jnp.logical_not(new_tile))
    def _():
        for k in range(N_SLAB):
            ys_ref[k] = jnp.where(mine, y[:, k * SLAB:(k + 1) * SLAB], ys_ref[k])


def _experts(step_tile, step_expert, step_lo, step_hi, xs, w1, w3, w2, layer):
    n_slab, ns, slab = xs.shape
    d = w1.shape[1]
    ff = w1.shape[2]
    n_steps = step_tile.shape[0]
    base = layer * MOE_EXPERTS
    grid_spec = pltpu.PrefetchScalarGridSpec(
        num_scalar_prefetch=4,
        grid=(n_steps,),
        in_specs=[pl.BlockSpec((n_slab, TM_X, slab), lambda s, tl, ex, lo, hi: (0, tl[s], 0)),
                  pl.BlockSpec((None, d, ff), lambda s, tl, ex, lo, hi: (base + ex[s], 0, 0)),
                  pl.BlockSpec((None, d, ff), lambda s, tl, ex, lo, hi: (base + ex[s], 0, 0)),
                  pl.BlockSpec((None, ff, d), lambda s, tl, ex, lo, hi: (base + ex[s], 0, 0))],
        out_specs=pl.BlockSpec((n_slab, TM_X, slab), lambda s, tl, ex, lo, hi: (0, tl[s], 0)),
        scratch_shapes=[pltpu.VMEM((d, ff), BF16), pltpu.VMEM((d, ff), BF16), pltpu.VMEM((ff, d), BF16)],
    )
    return pl.pallas_call(
        _expert_kernel,
        grid_spec=grid_spec,
        out_shape=jax.ShapeDtypeStruct((n_slab, ns, slab), xs.dtype),
        compiler_params=_cparams(("arbitrary",)),
        name="moe_experts",
    )(step_tile, step_expert, step_lo, step_hi, xs, w1, w3, w2)


def _combine_kernel(x_ref, gate_ref, fw_ref, wcol_ref, y1_ref, y2_ref, o_ref, *, final):
    w_first = wcol_ref[:, 0:1]
    w_second = wcol_ref[:, 1:2]
    y_first = _unpack_bf16_pairs(jnp.concatenate([y1_ref[k] for k in range(N_SLAB)], axis=-1))
    y_second = _unpack_bf16_pairs(jnp.concatenate([y2_ref[k] for k in range(N_SLAB)], axis=-1))
    moe = w_first * y_first + w_second * y_second
    x = x_ref[...] + gate_ref[...] * moe
    if final:
        x = x * lax.rsqrt(jnp.mean(x * x, axis=-1, keepdims=True) + EPS) * fw_ref[...]
    o_ref[...] = x


def _combine(x2, mod3, final_w_row, wcol, gathered, seq, final):
    t, d = x2.shape
    tm = TM_COMB
    tpb = seq // tm
    nblk = t // tm
    yspec = lambda off: pl.BlockSpec((N_SLAB, tm, SLAB), lambda i: (0, i + off, 0))
    return pl.pallas_call(
        functools.partial(_combine_kernel, final=final),
        grid=(nblk,),
        in_specs=[pl.BlockSpec((tm, d), lambda i: (i, 0)),
                  pl.BlockSpec((None, 1, d), lambda i: ((i // tpb) * 6 + 5, 0, 0)),
                  pl.BlockSpec((1, d), lambda i: (0, 0)),
                  pl.BlockSpec((tm, LANE), lambda i: (i, 0)),
                  yspec(0), yspec(nblk)],
        out_specs=pl.BlockSpec((tm, d), lambda i: (i, 0)),
        out_shape=jax.ShapeDtypeStruct((t, d), F32),
        compiler_params=_cparams(("parallel",)),
        name="moe_combine",
    )(x2, mod3, final_w_row, wcol, gathered, gathered)


def _moe(x2, mod3, final_w_row, w_route, b_route, tri_excl, w1, w3, w2, layer, seq, final):
    t, d = x2.shape
    h3, ids, wcol, counts = _router(x2, mod3, w_route, b_route, tri_excl, seq)
    cnt = counts[:, 0].astype(jnp.int32)
    ends = jnp.cumsum(cnt)
    offs = ends - cnt
    experts = jnp.arange(MOE_EXPERTS, dtype=jnp.int32)
    pick = lambda table, idx: jnp.sum(jnp.where(idx[:, None] == experts[None, :], table[None, :], 0), axis=1)
    slot1 = pick(offs, ids[0]) + ids[2]
    slot2 = pick(offs, ids[1]) + ids[3]
    n_tiles = 2 * t // TM_X
    first_tile = offs // TM_X
    n_vis = jnp.where(cnt > 0, (ends - 1) // TM_X - first_tile + 1, 0)
    cum = jnp.cumsum(n_vis)
    step = jnp.arange(n_tiles + MOE_EXPERTS, dtype=jnp.int32)
    step_expert = jnp.minimum(jnp.sum(step[:, None] >= cum[None, :], axis=1), MOE_EXPERTS - 1).astype(jnp.int32)
    valid = step < cum[-1]
    step_tile = jnp.where(valid, pick(first_tile - (cum - n_vis), step_expert) + step, n_tiles - 1)
    step_lo = jnp.where(valid, jnp.clip(pick(offs, step_expert) - step_tile * TM_X, 0, TM_X), 0)
    step_hi = jnp.where(valid, jnp.clip(pick(ends, step_expert) - step_tile * TM_X, 0, TM_X), 0)
    xs = _dispatch(slot1, slot2, h3)
    ys = _experts(step_tile.astype(jnp.int32), step_expert, step_lo.astype(jnp.int32), step_hi.astype(jnp.int32),
                  xs, w1, w3, w2, layer)
    n_sorted = ys.shape[1]
    gathered = _gather_rows(ys.reshape(N_SLAB * n_sorted, SLAB), _slab_rows(jnp.concatenate([slot1, slot2]), n_sorted))
    gathered = gathered.reshape(N_SLAB, n_sorted, SLAB)
    return _combine(x2, mod3, final_w_row, wcol, gathered, seq, final)


def kernel(x, c, positions, ada_w, ada_b, w_in, s5_lam_re, s5_lam_im, s5_b_re, s5_b_im, s5_c_re, s5_c_im, s5_d, s5_log_dt, s5_w_glu, hg_lb_logits, hg_norm_w, m2_conv_w, m2_conv_b, m2_dt_bias, m2_a_log, m2_d, m2_norm_w, w_branch, w_gate, b_gate, w_out, moe_w_group, moe_b_group, moe_w_expert, moe_b_expert, moe_w1, moe_w3, moe_w2, final_norm_w):
    bsz, seq, d = x.shape
    t = bsz * seq
    depth = ada_w.shape[0]
    assert seq % TM_PROJ == 0 and seq % C_RET == 0 and seq % C_SSD == 0 and seq % C_HG == 0
    x2 = x.reshape(t, d).astype(F32)

    c_pad = jnp.zeros((8, d), F32).at[:bsz].set(c.astype(F32))
    mod_all = _ada_mod(c_pad, ada_w.astype(F32), ada_b.astype(F32))

    half = RET_DK // 2
    inv_freq = ROPE_BASE ** (-jnp.arange(half, dtype=F32) / half)
    invf_col = jnp.broadcast_to(inv_freq[:, None], (half, LANE))
    expand = np.tile(np.eye(half, dtype=np.float32), (1, 2 * RET_HEADS))
    sign = np.tile(np.concatenate([-np.ones(half), np.ones(half)]), RET_HEADS)[None, :].astype(np.float32)
    cos_t, sin_t = _rope_tables(positions.reshape(1, t).astype(jnp.int32), invf_col,
                                jnp.asarray(expand), jnp.asarray(expand * sign))
    cos3 = cos_t.reshape(bsz, seq, -1)
    sin3 = sin_t.reshape(bsz, seq, -1)

    lb_cum = jnp.cumsum(jax.nn.softmax(hg_lb_logits.astype(F32), axis=0), axis=0)
    hg_lb = lb_cum - lb_cum[:1]
    tri_ssd = jnp.asarray(np.tril(np.ones((C_SSD, C_SSD), np.float32)))
    tri_excl = jnp.asarray(np.triu(np.ones((TM_PROJ, TM_PROJ), np.float32), 1), BF16)
    final_w_row = final_norm_w.astype(F32)[None, :]
    w_pad = jnp.zeros((depth, d, IN_W_PAD), BF16).at[:, :, :IN_W].set(w_in.astype(BF16))
    w_glu_bf = s5_w_glu.astype(BF16)
    w_branch_bf = w_branch.astype(BF16)
    w_gate_bf = w_gate.astype(BF16)
    w_out_bf = w_out.astype(BF16)
    b_gate3 = b_gate.astype(F32).reshape(depth, 1, -1)
    moe_w1_all = moe_w1.astype(F32).reshape(depth * MOE_EXPERTS, d, MOE_FF)
    moe_w3_all = moe_w3.astype(F32).reshape(depth * MOE_EXPERTS, d, MOE_FF)
    moe_w2_all = moe_w2.astype(F32).reshape(depth * MOE_EXPERTS, MOE_FF, d)

    for layer in range(depth):
        mod3 = mod_all[layer, :bsz].reshape(bsz * 6, 1, d)
        p, u_s5 = _in_proj(x2, mod3, w_pad, layer, seq)
        p3 = p.reshape(bsz, seq, IN_W_PAD)

        ops = _s5_operators(s5_lam_re[layer], s5_lam_im[layer], s5_b_re[layer], s5_b_im[layer],
                            s5_c_re[layer], s5_c_im[layer], s5_d[layer], s5_log_dt[layer])
        y_s5 = _s5_scan(u_s5.reshape(bsz, seq, BRANCH_W), *ops).reshape(t, BRANCH_W)

        lb = hg_lb[layer][None, :]
        y_hg = _hgrn2(p3, jnp.log(lb), jnp.log1p(-lb), hg_norm_w[layer].astype(F32)[None, :]).reshape(t, BRANCH_W)

        y_ret = _retention(p3, cos3, sin3).reshape(t, BRANCH_W)

        pad8 = lambda v: jnp.zeros((1, LANE), F32).at[0, :M2_HEADS].set(v.astype(F32))
        y_m2 = _ssd(p3, tri_ssd, m2_conv_w[layer].astype(F32), m2_conv_b[layer].astype(F32)[None, :],
                    pad8(m2_dt_bias[layer]), pad8(m2_a_log[layer]),
                    jnp.repeat(m2_d[layer].astype(F32), M2_HEADDIM)[None, :],
                    m2_norm_w[layer].astype(F32)[None, :]).reshape(t, BRANCH_W)

        x2 = _merge(x2, mod3, y_s5, y_hg, y_ret, y_m2, w_glu_bf, w_branch_bf, w_gate_bf, b_gate3, w_out_bf,
                    layer, seq)

        nr = 40
        w_route = jnp.zeros((nr, d), F32).at[:MOE_GROUPS].set(moe_w_group[layer].astype(F32).T)
        w_route = w_route.at[MOE_GROUPS:MOE_GROUPS + MOE_EXPERTS].set(moe_w_expert[layer].astype(F32).T)
        b_route = jnp.zeros((nr, LANE), F32).at[:MOE_GROUPS, 0].set(moe_b_group[layer].astype(F32))
        b_route = b_route.at[MOE_GROUPS:MOE_GROUPS + MOE_EXPERTS, 0].set(moe_b_expert[layer].astype(F32))
        x2 = _moe(x2, mod3, final_w_row, w_route, b_route, tri_excl, moe_w1_all, moe_w3_all, moe_w2_all,
                  layer, seq, final=(layer == depth - 1))
    return x2.reshape(bsz, seq, d)
```

```python
import functools
import math

import numpy as np
import jax
import jax.numpy as jnp
from jax import lax
from jax.experimental import pallas as pl
from jax.experimental.pallas import tpu as pltpu
from jax.experimental.pallas import tpu_sc as plsc

F32 = jnp.float32
BF16 = jnp.bfloat16
HIGHEST = lax.Precision.HIGHEST

D_MODEL = 1024
DEPTH = 2
BRANCH_W = 512
EPS = 1e-6
S5_GROUPS = 32
S5_CH = 16
S5_STATE = 64
S5_MAX_REAL = -1e-4
S5_BLOCK = 16
HG_HEADS = 4
HG_DK = 128
RET_HEADS = 4
RET_DK = 64
RET_DV = 128
ROPE_BASE = 10000.0
M2_HEADS = 8
M2_HEADDIM = 64
M2_GROUPS = 2
M2_STATE = 128
M2_CONV = 4
MOE_GROUPS = 4
MOE_EPG = 8
MOE_EXPERTS = MOE_GROUPS * MOE_EPG
MOE_FF = 256

COL_S5, COL_HQ, COL_HF, COL_HI, COL_HG = 0, 512, 1024, 1536, 2048
COL_RQ, COL_RK, COL_RV, COL_RG = 2560, 2816, 3072, 3584
COL_MZ, COL_MXS, COL_MBC, COL_MDT = 4096, 4608, 5120, 5632
IN_W = 5640
IN_W_PAD = 5888

LANE = 128
VMEM_LIMIT = 56 * 1024 * 1024

TM_PROJ = 1024
TN_PROJ = 1024
TM_MERGE = 512
C_RET = 256
C_SSD = 256
C_HG = 128
TM_X = 256
TM_COMB = 512
SC_WINDOW = 128
SLAB = 256
N_SLAB = D_MODEL // 2 // SLAB


def _cparams(sem):
    return pltpu.CompilerParams(dimension_semantics=sem, vmem_limit_bytes=VMEM_LIMIT)


def _silu(v):
    return v * jax.nn.sigmoid(v)


def _dot_nt(a, b, **kw):
    return lax.dot_general(a, b, (((1,), (1,)), ((), ())), preferred_element_type=F32, **kw)


def _dot_tn(a, b, **kw):
    return lax.dot_general(a, b, (((0,), (0,)), ((), ())), preferred_element_type=F32, **kw)


def _ada_kernel(c_ref, w_ref, b_ref, o_ref):
    cond = _silu(c_ref[...])
    o_ref[...] = jnp.dot(cond, w_ref[...], preferred_element_type=F32, precision=HIGHEST) + b_ref[...]


def _ada_mod(c_pad, ada_w, ada_b):
    depth, d, n = ada_w.shape
    tn = 1536
    return pl.pallas_call(
        _ada_kernel,
        grid=(depth, n // tn),
        in_specs=[pl.BlockSpec((8, d), lambda l, j: (0, 0)),
                  pl.BlockSpec((None, d, tn), lambda l, j: (l, 0, j)),
                  pl.BlockSpec((None, 1, tn), lambda l, j: (l, 0, j))],
        out_specs=pl.BlockSpec((None, 8, tn), lambda l, j: (l, 0, j)),
        out_shape=jax.ShapeDtypeStruct((depth, 8, n), F32),
        compiler_params=_cparams(("parallel", "parallel")),
        name="ada_mod",
    )(c_pad, ada_w, ada_b.reshape(depth, 1, n))


def _pack_bf16_pairs(x):
    n = x.shape[1] // 2
    lo = pltpu.bitcast(x[:, :n].astype(BF16).astype(F32), jnp.uint32) >> 16
    hi = pltpu.bitcast(x[:, n:].astype(BF16).astype(F32), jnp.uint32)
    return hi | lo


def _unpack_bf16_pairs(w):
    lo = pltpu.bitcast(w << 16, F32)
    hi = pltpu.bitcast(w & jnp.uint32(0xFFFF0000), F32)
    return jnp.concatenate([lo, hi], axis=-1)


def _modulated_norm(x, scale, shift):
    ms = jnp.mean(x * x, axis=-1, keepdims=True)
    return x * lax.rsqrt(ms + EPS) * (1.0 + scale) + shift


def _inproj_kernel(x_ref, sc_ref, sh_ref, w_ref, o_ref, u_ref):
    h = _modulated_norm(x_ref[...], sc_ref[...], sh_ref[...]).astype(BF16)
    n_total = o_ref.shape[1]
    for n0 in range(0, n_total, TN_PROJ):
        n1 = min(n0 + TN_PROJ, n_total)
        p = jnp.dot(h, w_ref[:, n0:n1], preferred_element_type=F32)
        o_ref[:, n0:n1] = p.astype(o_ref.dtype)
        if n0 == 0:
            u_ref[...] = p[:, COL_S5:COL_S5 + BRANCH_W]


def _in_proj(x2, mod3, w_pad, layer, seq):
    t, d = x2.shape
    tm = TM_MERGE
    tpb = seq // tm
    assert COL_S5 + BRANCH_W <= TN_PROJ
    return pl.pallas_call(
        _inproj_kernel,
        grid=(t // tm,),
        in_specs=[pl.BlockSpec((tm, d), lambda i: (i, 0)),
                  pl.BlockSpec((None, 1, d), lambda i: ((i // tpb) * 6 + 1, 0, 0)),
                  pl.BlockSpec((None, 1, d), lambda i: ((i // tpb) * 6 + 0, 0, 0)),
                  pl.BlockSpec((None, d, IN_W_PAD), lambda i: (layer, 0, 0))],
        out_specs=[pl.BlockSpec((tm, IN_W_PAD), lambda i: (i, 0)),
                   pl.BlockSpec((tm, BRANCH_W), lambda i: (i, 0))],
        out_shape=[jax.ShapeDtypeStruct((t, IN_W_PAD), BF16), jax.ShapeDtypeStruct((t, BRANCH_W), F32)],
        compiler_params=_cparams(("parallel",)),
        name="in_proj",
    )(x2, mod3, mod3, w_pad)


def _rope_kernel(pos_ref, invf_ref, ecos_ref, esin_ref, cos_ref, sin_ref):
    ang = invf_ref[:, 0:1] * pos_ref[...].astype(F32)
    cos_ref[...] = _dot_tn(jnp.cos(ang), ecos_ref[...], precision=HIGHEST)
    sin_ref[...] = _dot_tn(jnp.sin(ang), esin_ref[...], precision=HIGHEST)


def _rope_tables(pos_row, invf_col, expand_cos, expand_sin):
    t = pos_row.shape[1]
    half, w = expand_cos.shape
    tm = 1024
    const = lambda shape: pl.BlockSpec(shape, lambda i: (0, 0))
    return pl.pallas_call(
        _rope_kernel,
        grid=(t // tm,),
        in_specs=[pl.BlockSpec((1, tm), lambda i: (0, i)), const((half, LANE)), const((half, w)), const((half, w))],
        out_specs=[pl.BlockSpec((tm, w), lambda i: (i, 0))] * 2,
        out_shape=[jax.ShapeDtypeStruct((t, w), F32)] * 2,
        compiler_params=_cparams(("parallel",)),
        name="rope_tables",
    )(pos_row, invf_col, expand_cos, expand_sin)


def _ret_kernel(q_ref, k_ref, v_ref, g_ref, cos_ref, sin_ref, o_ref, st_ref, *, chunk):
    @pl.when(pl.program_id(1) == 0)
    def _():
        st_ref[...] = jnp.zeros_like(st_ref)

    cosf = cos_ref[...]
    sinf = sin_ref[...]
    width = RET_HEADS * RET_DK
    lane = lax.broadcasted_iota(jnp.int32, (chunk, width), 1)
    first_half = (lane % RET_DK) < (RET_DK // 2)

    def rope(t):
        partner = jnp.where(first_half, pltpu.roll(t, width - RET_DK // 2, 1), pltpu.roll(t, RET_DK // 2, 1))
        return t * cosf + partner * sinf

    q = rope(q_ref[...].astype(F32))
    k = rope(k_ref[...].astype(F32)) * (RET_DK ** -0.5)
    v = v_ref[...]
    g = g_ref[...].astype(F32)
    ti = lax.broadcasted_iota(jnp.int32, (chunk, chunk), 0)
    si = lax.broadcasted_iota(jnp.int32, (chunk, chunk), 1)
    lag = (ti - si).astype(F32)
    tcol = lax.broadcasted_iota(jnp.int32, (chunk, 1), 0).astype(F32)
    for h in range(RET_HEADS):
        log_gamma = math.log1p(-(2.0 ** (-5.0 - h)))
        qh = q[:, h * RET_DK:(h + 1) * RET_DK]
        kh = k[:, h * RET_DK:(h + 1) * RET_DK]
        vh = v[:, h * RET_DV:(h + 1) * RET_DV].astype(BF16)
        decay = jnp.where(ti >= si, jnp.exp(jnp.minimum(lag * log_gamma, 0.0)), 0.0)
        scores = _dot_nt(qh.astype(BF16), kh.astype(BF16)) * decay
        state = st_ref[h]
        q_in = qh * jnp.exp(log_gamma * (tcol + 1.0))
        o = (jnp.dot(scores.astype(BF16), vh, preferred_element_type=F32)
             + jnp.dot(q_in.astype(BF16), state.astype(BF16), preferred_element_type=F32))
        k_out = kh * jnp.exp(log_gamma * (chunk - 1.0 - tcol))
        st_ref[h] = math.exp(log_gamma * chunk) * state + _dot_tn(k_out.astype(BF16), vh)
        o = o * lax.rsqrt(jnp.mean(o * o, axis=-1, keepdims=True) + EPS)
        gh = g[:, h * RET_DV:(h + 1) * RET_DV]
        o_ref[:, h * RET_DV:(h + 1) * RET_DV] = (o * _silu(gh)).astype(o_ref.dtype)


def _retention(p3, cos3, sin3):
    b, seq, _ = p3.shape
    c = C_RET
    qk_w = RET_HEADS * RET_DK
    return pl.pallas_call(
        functools.partial(_ret_kernel, chunk=c),
        grid=(b, seq // c),
        in_specs=[pl.BlockSpec((None, c, qk_w), lambda i, j: (i, j, COL_RQ // qk_w)),
                  pl.BlockSpec((None, c, qk_w), lambda i, j: (i, j, COL_RK // qk_w)),
                  pl.BlockSpec((None, c, BRANCH_W), lambda i, j: (i, j, COL_RV // BRANCH_W)),
                  pl.BlockSpec((None, c, BRANCH_W), lambda i, j: (i, j, COL_RG // BRANCH_W)),
                  pl.BlockSpec((None, c, qk_w), lambda i, j: (i, j, 0)),
                  pl.BlockSpec((None, c, qk_w), lambda i, j: (i, j, 0))],
        out_specs=pl.BlockSpec((None, c, BRANCH_W), lambda i, j: (i, j, 0)),
        out_shape=jax.ShapeDtypeStruct((b, seq, BRANCH_W), BF16),
        scratch_shapes=[pltpu.VMEM((RET_HEADS, RET_DK, RET_DV), F32)],
        compiler_params=_cparams(("parallel", "arbitrary")),
        name="retention",
    )(p3, p3, p3, p3, cos3, sin3)


def _ssd_kernel(z_ref, xs_ref, bc_ref, dt_ref, tri_ref, cw_ref, cb_ref, dtb_ref, alog_ref, dsk_ref, nw_ref,
                o_ref, xe_scr, st_ref, *, chunk):
    j = pl.program_id(1)
    width = 2 * BRANCH_W

    @pl.when(j == 0)
    def _():
        st_ref[...] = jnp.zeros_like(st_ref)
        xe_scr[0:8, :] = jnp.zeros((8, width), F32)

    @pl.when(j > 0)
    def _():
        xe_scr[0:8, :] = xe_scr[chunk:chunk + 8, :]

    xe_scr[8:, 0:BRANCH_W] = xs_ref[...].astype(F32)
    xe_scr[8:, BRANCH_W:] = bc_ref[...].astype(F32)
    conv = cb_ref[...] + cw_ref[M2_CONV - 1:M2_CONV, :] * xe_scr[8:, :]
    for tap in range(M2_CONV - 1):
        conv = conv + cw_ref[tap:tap + 1, :] * xe_scr[pl.ds(8 - (M2_CONV - 1) + tap, chunk), :]
    conv = _silu(conv)
    xs = conv[:, :BRANCH_W]
    bm = conv[:, BRANCH_W:BRANCH_W + M2_GROUPS * M2_STATE]
    cm = conv[:, BRANCH_W + M2_GROUPS * M2_STATE:]

    dt = jax.nn.softplus(dt_ref[...].astype(F32) + dtb_ref[...])
    da = dt * (-jnp.exp(alog_ref[...]))
    a_cs = jnp.dot(tri_ref[...], da, preferred_element_type=F32, precision=HIGHEST)
    a_cs_t = a_cs.T
    ti = lax.broadcasted_iota(jnp.int32, (chunk, chunk), 0)
    si = lax.broadcasted_iota(jnp.int32, (chunk, chunk), 1)
    causal = ti >= si
    hpg = M2_HEADS // M2_GROUPS
    ys = []
    for grp in range(M2_GROUPS):
        bm_g = bm[:, grp * M2_STATE:(grp + 1) * M2_STATE]
        cm_g = cm[:, grp * M2_STATE:(grp + 1) * M2_STATE]
        cb = _dot_nt(cm_g.astype(BF16), bm_g.astype(BF16))
        for hh in range(hpg):
            h = grp * hpg + hh
            col = a_cs[:, h:h + 1]
            row = a_cs_t[h:h + 1, :]
            lmat = jnp.where(causal, jnp.exp(jnp.minimum(col - row, 0.0)), 0.0)
            xd = xs[:, h * M2_HEADDIM:(h + 1) * M2_HEADDIM] * dt[:, h:h + 1]
            state = st_ref[h]
            y = (jnp.dot((cb * lmat).astype(BF16), xd.astype(BF16), preferred_element_type=F32)
                 + jnp.dot((cm_g * jnp.exp(col)).astype(BF16), state.astype(BF16), preferred_element_type=F32))
            a_last = a_cs[chunk - 1:chunk, h:h + 1]
            to_end = jnp.exp(a_last - col)
            st_ref[h] = jnp.exp(a_last) * state + _dot_tn(bm_g.astype(BF16), (xd * to_end).astype(BF16))
            ys.append(y)
    y = jnp.concatenate(ys, axis=-1) + dsk_ref[...] * xs
    y = y * _silu(z_ref[...].astype(F32))
    o_ref[...] = (y * lax.rsqrt(jnp.mean(y * y, axis=-1, keepdims=True) + EPS) * nw_ref[...]).astype(o_ref.dtype)


def _ssd(p3, tri, conv_w, conv_b, dt_bias_row, a_log_row, d_skip_row, norm_w_row):
    b, seq, _ = p3.shape
    c = C_SSD
    const = lambda shape: pl.BlockSpec(shape, lambda i, j: (0,) * len(shape))
    return pl.pallas_call(
        functools.partial(_ssd_kernel, chunk=c),
        grid=(b, seq // c),
        in_specs=[pl.BlockSpec((None, c, BRANCH_W), lambda i, j: (i, j, COL_MZ // BRANCH_W)),
                  pl.BlockSpec((None, c, BRANCH_W), lambda i, j: (i, j, COL_MXS // BRANCH_W)),
                  pl.BlockSpec((None, c, BRANCH_W), lambda i, j: (i, j, COL_MBC // BRANCH_W)),
                  pl.BlockSpec((None, c, LANE), lambda i, j: (i, j, COL_MDT // LANE)),
                  const((c, c)), const((M2_CONV, 2 * BRANCH_W)), const((1, 2 * BRANCH_W)),
                  const((1, LANE)), const((1, LANE)), const((1, BRANCH_W)), const((1, BRANCH_W))],
        out_specs=pl.BlockSpec((None, c, BRANCH_W), lambda i, j: (i, j, 0)),
        out_shape=jax.ShapeDtypeStruct((b, seq, BRANCH_W), BF16),
        scratch_shapes=[pltpu.VMEM((c + 8, 2 * BRANCH_W), F32),
                        pltpu.VMEM((M2_HEADS, M2_STATE, M2_HEADDIM), F32)],
        compiler_params=_cparams(("parallel", "arbitrary")),
        name="ssd",
    )(p3, p3, p3, p3, tri, conv_w, conv_b, dt_bias_row, a_log_row, d_skip_row, norm_w_row)


def _hg_tables(chunk):
    n_lev = int(math.log2(chunk))
    r = np.arange(chunk)[:, None]
    jj = np.arange(chunk)[None, :]
    tri = (jj <= r).astype(np.float32)
    x = r ^ jj
    levmap = np.where(r > jj, np.floor(np.log2(x + 0.5)), np.where(r == jj, -1, -2)).astype(np.int32)
    return tri, levmap, n_lev


def _hg_level_exponent(b, lev):
    rows, width = b.shape
    m = 1 << lev
    sub = 8
    if 2 * m >= sub:
        blocks = b.reshape(rows // (2 * m), 2 * m, width)
        mid = jnp.broadcast_to(blocks[:, m - 1:m, :], blocks.shape).reshape(rows, width)
    else:
        groups = b.reshape(rows // sub, sub, width)
        row_in_group = lax.broadcasted_iota(jnp.int32, groups.shape, 1)
        mid = None
        for start in range(0, sub, 2 * m):
            picked = jnp.broadcast_to(groups[:, start + m - 1:start + m, :], groups.shape)
            mid = picked if mid is None else jnp.where(row_in_group >= start, picked, mid)
        mid = mid.reshape(rows, width)
    return -jnp.abs(b - mid)


def _hg_kernel(q_ref, f_ref, i_ref, g_ref, sum_ref, lev_ref, llb_ref, l1m_ref, nw_ref, o_ref, st_ref,
               *, chunk, n_lev):
    @pl.when(pl.program_id(1) == 0)
    def _():
        st_ref[...] = jnp.zeros_like(st_ref)

    f = f_ref[...].astype(F32)
    log_sig = jnp.minimum(f, 0.0) - jnp.log1p(jnp.exp(-jnp.abs(f)))
    a = llb_ref[...]
    bb = l1m_ref[...] + log_sig
    log_f = jnp.maximum(a, bb) + jnp.log1p(jnp.exp(-jnp.abs(a - bb)))
    k_all = jnp.exp(l1m_ref[...]) * jax.nn.sigmoid(-f)
    q_all = _silu(q_ref[...].astype(F32))
    hi = log_f.astype(BF16)
    r1 = log_f - hi.astype(F32)
    mid = r1.astype(BF16)
    lo = (r1 - mid.astype(F32)).astype(BF16)
    tri = sum_ref[...]
    b_all = (jnp.dot(tri, hi, preferred_element_type=F32)
             + jnp.dot(tri, mid, preferred_element_type=F32)
             + jnp.dot(tri, lo, preferred_element_type=F32))
    to_end_all = b_all[chunk - 1:chunk, :] - b_all
    level_decay = [jnp.exp(_hg_level_exponent(b_all, lev)) for lev in range(n_lev)]
    levmap = lev_ref[...]
    v_all = i_ref[...]
    g_all = g_ref[...].astype(F32)
    for h in range(HG_HEADS):
        sl = slice(h * HG_DK, (h + 1) * HG_DK)
        qh = q_all[:, sl]
        kh = k_all[:, sl]
        vh = v_all[:, sl].astype(BF16)
        b_h = b_all[:, sl]
        to_end = to_end_all[:, sl]
        amat = jnp.where(levmap == -1, _dot_nt(qh.astype(BF16), kh.astype(BF16)), 0.0)
        for lev in range(n_lev):
            e = level_decay[lev][:, sl]
            a_l = _dot_nt((qh * e).astype(BF16), (kh * e).astype(BF16))
            amat = jnp.where(levmap == lev, a_l, amat)
        state_t = st_ref[h]
        o = (jnp.dot(amat.astype(BF16), vh, preferred_element_type=F32)
             + _dot_nt((qh * jnp.exp(b_h)).astype(BF16), state_t.astype(BF16)))
        k_end = kh * jnp.exp(to_end)
        st_ref[h] = jnp.exp(b_h[chunk - 1:chunk, :]) * state_t + _dot_tn(vh, k_end.astype(BF16))
        o = o * lax.rsqrt(jnp.mean(o * o, axis=-1, keepdims=True) + EPS) * nw_ref[...]
        o_ref[:, sl] = (o * _silu(g_all[:, sl])).astype(o_ref.dtype)


def _hgrn2(p3, log_lb_row, log1m_lb_row, norm_w_row):
    b, seq, _ = p3.shape
    c = C_HG
    tri, levmap, n_lev = _hg_tables(c)
    const = lambda shape: pl.BlockSpec(shape, lambda i, j: (0,) * len(shape))
    blk = lambda col: pl.BlockSpec((None, c, BRANCH_W), lambda i, j: (i, j, col // BRANCH_W))
    return pl.pallas_call(
        functools.partial(_hg_kernel, chunk=c, n_lev=n_lev),
        grid=(b, seq // c),
        in_specs=[blk(COL_HQ), blk(COL_HF), blk(COL_HI), blk(COL_HG),
                  const((c, c)), const((c, c)),
                  const((1, BRANCH_W)), const((1, BRANCH_W)), const((1, HG_DK))],
        out_specs=pl.BlockSpec((None, c, BRANCH_W), lambda i, j: (i, j, 0)),
        out_shape=jax.ShapeDtypeStruct((b, seq, BRANCH_W), BF16),
        scratch_shapes=[pltpu.VMEM((HG_HEADS, HG_DK, HG_DK), F32)],
        compiler_params=_cparams(("parallel", "arbitrary")),
        name="hgrn2",
    )(p3, p3, p3, p3, jnp.asarray(tri, BF16), jnp.asarray(levmap), log_lb_row, log1m_lb_row, norm_w_row)


def _expand_block_diag(comp_ref, e_ref, dst_ref, row_div, lane_div):
    gq = LANE // S5_CH
    rows, ncols = dst_ref.shape
    step = 512
    comp = comp_ref[...]
    row_grp = (lax.broadcasted_iota(jnp.int32, (rows, step), 0) // row_div) % gq
    for c0 in range(0, ncols, step):
        lane_grp = ((lax.broadcasted_iota(jnp.int32, (rows, step), 1) + c0) // lane_div) % gq
        full = jnp.dot(comp, e_ref[:, c0:c0 + step], preferred_element_type=F32)
        dst_ref[:, c0:c0 + step] = jnp.where(row_grp == lane_grp, full, 0.0).astype(dst_ref.dtype)


def _s5_kernel(u_ref, k2_ref, bc_ref, cc_ref, esc_ref, eb_ref, lam_ref, o_ref, tc_scr, tq_ref, bq_ref, cq_ref,
               x_scr, w_scr, s_scr, *, rows):
    nb = S5_BLOCK

    @pl.when(pl.program_id(1) == 0)
    def _():
        k2 = k2_ref[...]
        lane = lax.broadcasted_iota(jnp.int32, k2.shape, 1)
        for t in range(nb):
            shifted = k2 if t == 0 else jnp.where(lane >= t * S5_CH, pltpu.roll(k2, t * S5_CH, 1), 0.0)
            tc_scr[t * LANE:(t + 1) * LANE, :] = shifted.astype(tc_scr.dtype)
        _expand_block_diag(tc_scr, esc_ref, tq_ref, S5_CH, S5_CH)
        _expand_block_diag(bc_ref, eb_ref, bq_ref, S5_CH, S5_STATE)
        _expand_block_diag(cc_ref, esc_ref, cq_ref, S5_STATE, S5_CH)

    for t in range(nb):
        x_scr[:, t * LANE:(t + 1) * LANE] = u_ref[pl.ds(t, rows, stride=nb), :].astype(x_scr.dtype)
    x = x_scr[...]
    half = w_scr.shape[1] // 2
    w_scr[...] = jnp.dot(x, bq_ref[...], preferred_element_type=F32)
    lam_re = lam_ref[0:1, :]
    lam_im = lam_ref[1:2, :]

    def body(j, carry):
        s_re, s_im = carry
        s_scr[pl.ds(j, 1), 0:half] = s_re
        s_scr[pl.ds(j, 1), half:] = s_im
        w_re = w_scr[pl.ds(j, 1), 0:half]
        w_im = w_scr[pl.ds(j, 1), half:]
        return lam_re * s_re - lam_im * s_im + w_re, lam_re * s_im + lam_im * s_re + w_im

    zero = jnp.zeros((1, half), F32)
    lax.fori_loop(0, rows, body, (zero, zero))
    y = (jnp.dot(x, tq_ref[...], preferred_element_type=F32)
         + jnp.dot(s_scr[...].astype(BF16), cq_ref[...], preferred_element_type=F32))
    for t in range(nb):
        o_ref[pl.ds(t, rows, stride=nb), :] = y[:, t * LANE:(t + 1) * LANE]


def _s5_scan(p3, k2, bc, cc, lam16):
    batch, seq, _ = p3.shape
    nb = S5_BLOCK
    nq = BRANCH_W // LANE
    rows = seq // nb
    kdim = nb * LANE
    gq = LANE // S5_CH
    ncol = 2 * gq * S5_STATE
    e_sc = (np.eye(nb)[:, None, :, None, None] * np.eye(S5_CH)[None, :, None, None, :] * np.ones((1, 1, 1, gq, 1)))
    e_sc = e_sc.reshape(nb * S5_CH, nb * gq * S5_CH)
    e_b = (np.eye(2)[:, None, :, None, None] * np.eye(S5_STATE)[None, :, None, None, :] * np.ones((1, 1, 1, gq, 1)))
    e_b = e_b.reshape(2 * S5_STATE, ncol)
    full = lambda shape: pl.BlockSpec(shape, lambda q, b: (0,) * len(shape))
    per_q = lambda r, c: pl.BlockSpec((None, r, c), lambda q, b: (q, 0, 0))
    return pl.pallas_call(
        functools.partial(_s5_kernel, rows=rows),
        grid=(nq, batch),
        in_specs=[pl.BlockSpec((None, seq, LANE), lambda q, b: (b, 0, q)),
                  per_q(LANE, nb * S5_CH), per_q(kdim, 2 * S5_STATE), per_q(ncol, nb * S5_CH),
                  full(e_sc.shape), full(e_b.shape), per_q(2, ncol // 2)],
        out_specs=pl.BlockSpec((None, seq, LANE), lambda q, b: (b, 0, q)),
        out_shape=jax.ShapeDtypeStruct((batch, seq, BRANCH_W), F32),
        scratch_shapes=[pltpu.VMEM((kdim, nb * S5_CH), BF16),
                        pltpu.VMEM((kdim, kdim), BF16), pltpu.VMEM((kdim, ncol), BF16), pltpu.VMEM((ncol, kdim), BF16),
                        pltpu.VMEM((rows, kdim), BF16), pltpu.VMEM((rows, ncol), F32), pltpu.VMEM((rows, ncol), F32)],
        compiler_params=_cparams(("parallel", "arbitrary")),
        name="s5_scan",
    )(p3, k2, bc, cc, jnp.asarray(e_sc, BF16), jnp.asarray(e_b, BF16), lam16)


def _s5_operators(lam_re, lam_im, b_re, b_im, c_re, c_im, d_skip, log_dt):
    nb = S5_BLOCK
    gq = LANE // S5_CH
    nq = S5_GROUPS // gq
    lam = lax.complex(jnp.minimum(lam_re.astype(F32), S5_MAX_REAL), lam_im.astype(F32))
    step = jnp.exp(log_dt.astype(F32))[:, None]
    z = lam * step
    lam_bar = jnp.exp(z)
    b_bar = ((lam_bar - 1.0) / lam)[..., None] * lax.complex(b_re.astype(F32), b_im.astype(F32))
    c_mat = lax.complex(c_re.astype(F32), c_im.astype(F32))
    pw = jnp.exp(z[..., None] * jnp.arange(nb + 1, dtype=F32))
    cp = c_mat[:, None, :, :] * pw[..., :nb].transpose(0, 2, 1)[:, :, None, :]
    cp = jnp.concatenate([cp.real, -cp.imag], axis=-1).reshape(S5_GROUPS, nb * S5_CH, 2 * S5_STATE)
    bri = jnp.concatenate([b_bar.real, b_bar.imag], axis=1)
    kern = jnp.einsum('gnk,gki->gin', cp, bri, precision=HIGHEST)
    skip = (jnp.asarray(np.concatenate([np.eye(S5_CH), np.zeros((S5_CH, (nb - 1) * S5_CH))], axis=1), F32)[None]
            * d_skip.astype(F32).reshape(S5_GROUPS, S5_CH, 1))
    k2 = (kern + skip).reshape(nq, gq * S5_CH, nb * S5_CH)
    pw_rev = jnp.exp(z[..., None] * jnp.asarray(np.arange(nb - 1, -1, -1), F32))
    binc = pw_rev[:, :, :, None] * b_bar[:, :, None, :]
    binc = jnp.stack([binc.real, binc.imag], axis=0).reshape(2, nq, gq, S5_STATE, nb, S5_CH)
    bc = binc.transpose(1, 4, 2, 5, 0, 3).reshape(nq, nb * LANE, 2 * S5_STATE)
    cm = c_mat.transpose(0, 2, 1)[:, :, None, :] * pw[..., 1:][:, :, :, None]
    cm = jnp.stack([cm.real, -cm.imag], axis=0).reshape(2, nq, gq * S5_STATE, nb * S5_CH)
    cc = cm.transpose(1, 0, 2, 3).reshape(nq, 2 * gq * S5_STATE, nb * S5_CH)
    lam_n = pw[..., nb].reshape(nq, gq * S5_STATE)
    lam16 = jnp.stack([lam_n.real, lam_n.imag], axis=1)
    return k2, bc.astype(BF16), cc.astype(BF16), lam16


def _merge_kernel(x_ref, sc_ref, sh_ref, gm_ref, ys5_ref, yhg_ref, yret_ref, ym2_ref,
                  wglu_ref, wbr_ref, wg_ref, bg_ref, wout_ref, o_ref):
    x = x_ref[...]
    d = x.shape[1]
    h = _modulated_norm(x, sc_ref[...], sh_ref[...]).astype(BF16)
    y_s5 = jax.nn.gelu(ys5_ref[...])
    y_s5 = y_s5 * jax.nn.sigmoid(jnp.dot(y_s5.astype(BF16), wglu_ref[...], preferred_element_type=F32))
    acc = jnp.zeros(x.shape, F32)
    for n, y in enumerate((y_s5, yhg_ref[...], yret_ref[...], ym2_ref[...])):
        gate = jax.nn.sigmoid(jnp.dot(h, wg_ref[:, n * d:(n + 1) * d], preferred_element_type=F32)
                              + bg_ref[:, n * d:(n + 1) * d])
        acc = acc + gate * jnp.dot(y.astype(BF16), wbr_ref[n], preferred_element_type=F32)
    o_ref[...] = x + gm_ref[...] * jnp.dot(acc.astype(BF16), wout_ref[...], preferred_element_type=F32)


def _merge(x2, mod3, ys5, yhg, yret, ym2, w_glu, w_branch, w_gate, b_gate, w_out, layer, seq):
    t, d = x2.shape
    tm = TM_MERGE
    tpb = seq // tm
    const = lambda shape: pl.BlockSpec((None,) + shape, lambda i: (layer,) + (0,) * len(shape))
    modspec = lambda k: pl.BlockSpec((None, 1, d), lambda i: ((i // tpb) * 6 + k, 0, 0))
    yspec = pl.BlockSpec((tm, BRANCH_W), lambda i: (i, 0))
    return pl.pallas_call(
        _merge_kernel,
        grid=(t // tm,),
        in_specs=[pl.BlockSpec((tm, d), lambda i: (i, 0)), modspec(1), modspec(0), modspec(2),
                  yspec, yspec, yspec, yspec,
                  const((BRANCH_W, BRANCH_W)), const((4, BRANCH_W, d)), const((d, 4 * d)), const((1, 4 * d)),
                  const((d, d))],
        out_specs=pl.BlockSpec((tm, d), lambda i: (i, 0)),
        out_shape=jax.ShapeDtypeStruct((t, d), F32),
        compiler_params=_cparams(("parallel",)),
        name="merge",
    )(x2, mod3, mod3, mod3, ys5, yhg, yret, ym2, w_glu, w_branch, w_gate, b_gate, w_out)


def _router_kernel(x_ref, sc_ref, sh_ref, wr_ref, br_ref, tri_ref, h_ref, ids_ref, wts_ref, cnt_ref, carry):
    i = pl.program_id(0)

    @pl.when(i == 0)
    def _():
        carry[...] = jnp.zeros_like(carry)

    h = _modulated_norm(x_ref[...], sc_ref[...], sh_ref[...])
    tm, d = h.shape
    packed = _pack_bf16_pairs(h)
    for k in range(N_SLAB):
        h_ref[k] = packed[:, k * SLAB:(k + 1) * SLAB]
    logits = _dot_nt(wr_ref[...], h, precision=HIGHEST) + br_ref[:, 0:1]
    gl = [logits[g:g + 1, :] for g in range(MOE_GROUPS)]
    gmax = gl[0]
    gsel = jnp.zeros((1, tm), jnp.int32)
    for g in range(1, MOE_GROUPS):
        better = gl[g] > gmax
        gsel = jnp.where(better, g, gsel)
        gmax = jnp.where(better, gl[g], gmax)
    gden = gl[0] * 0.0
    for g in range(MOE_GROUPS):
        gden = gden + jnp.exp(gl[g] - gmax)
    g_w = 1.0 / gden
    el = []
    for e in range(MOE_EPG):
        v = logits[MOE_GROUPS + e:MOE_GROUPS + e + 1, :]
        for g in range(1, MOE_GROUPS):
            row = MOE_GROUPS + g * MOE_EPG + e
            v = jnp.where(gsel == g, logits[row:row + 1, :], v)
        el.append(v)
    v1 = el[0]
    i1 = jnp.zeros((1, tm), jnp.int32)
    for e in range(1, MOE_EPG):
        better = el[e] > v1
        i1 = jnp.where(better, e, i1)
        v1 = jnp.where(better, el[e], v1)
    v2 = jnp.full((1, tm), -jnp.inf, F32)
    i2 = jnp.zeros((1, tm), jnp.int32)
    for e in range(MOE_EPG):
        better = (el[e] > v2) & (i1 != e)
        i2 = jnp.where(better, e, i2)
        v2 = jnp.where(better, el[e], v2)
    ex = jnp.exp(v2 - v1)
    p1 = 1.0 / (1.0 + ex)
    e1 = gsel * MOE_EPG + i1
    e2 = gsel * MOE_EPG + i2
    erow = lax.broadcasted_iota(jnp.int32, (MOE_EXPERTS, tm), 0)
    oh1 = (erow == e1).astype(F32)
    oh2 = (erow == e2).astype(F32)
    both = oh1 + oh2
    prefix = jnp.dot(both.astype(BF16), tri_ref[...], preferred_element_type=F32) + carry[:, 0:1]
    rank1 = jnp.sum(oh1 * prefix, axis=0, keepdims=True).astype(jnp.int32)
    rank2 = jnp.sum(oh2 * prefix, axis=0, keepdims=True).astype(jnp.int32)
    carry[...] = carry[...] + jnp.sum(both, axis=1, keepdims=True)
    zi = jnp.zeros((1, tm), jnp.int32)
    ids_ref[...] = jnp.concatenate([e1, e2, rank1, rank2, zi, zi, zi, zi], axis=0)
    wrow = lax.broadcasted_iota(jnp.int32, (LANE, tm), 0)
    wts_ref[...] = jnp.where(wrow == 0, p1 * g_w, jnp.where(wrow == 1, ex * p1 * g_w, 0.0)).T
    cnt_ref[...] = carry[...]


def _router(x2, mod3, w_route, b_route, tri_excl, seq):
    t, d = x2.shape
    tm = TM_PROJ
    tpb = seq // tm
    nr = w_route.shape[0]
    const = lambda shape: pl.BlockSpec(shape, lambda i: (0,) * len(shape))
    modspec = lambda k: pl.BlockSpec((None, 1, d), lambda i: ((i // tpb) * 6 + k, 0, 0))
    return pl.pallas_call(
        _router_kernel,
        grid=(t // tm,),
        in_specs=[pl.BlockSpec((tm, d), lambda i: (i, 0)), modspec(4), modspec(3),
                  const((nr, d)), const((nr, LANE)), const((tm, tm))],
        out_specs=[pl.BlockSpec((N_SLAB, tm, SLAB), lambda i: (0, i, 0)),
                   pl.BlockSpec((8, tm), lambda i: (0, i)),
                   pl.BlockSpec((tm, LANE), lambda i: (i, 0)),
                   const((MOE_EXPERTS, LANE))],
        out_shape=[jax.ShapeDtypeStruct((N_SLAB, t, SLAB), jnp.uint32),
                   jax.ShapeDtypeStruct((8, t), jnp.int32),
                   jax.ShapeDtypeStruct((t, LANE), F32),
                   jax.ShapeDtypeStruct((MOE_EXPERTS, LANE), F32)],
        scratch_shapes=[pltpu.VMEM((MOE_EXPERTS, LANE), F32)],
        compiler_params=_cparams(("arbitrary",)),
        name="moe_router",
    )(x2, mod3, mod3, w_route, b_route, tri_excl)


def _sc_mesh():
    return plsc.VectorSubcoreMesh(core_axis_name="core", subcore_axis_name="subcore")


def _slab_rows(idx, n_rows):
    return (idx[None, :] + (jnp.arange(N_SLAB, dtype=jnp.int32) * n_rows)[:, None]).reshape(-1)


def _dispatch(slot1, slot2, h_slabs):
    n_slab, t, d = h_slabs.shape
    n_out = 2 * t
    xs = _scatter_rows(h_slabs.reshape(n_slab * t, d), _slab_rows(slot1, n_out), _slab_rows(slot2, n_out),
                       n_slab * n_out)
    return xs.reshape(n_slab, n_out, d)


def _scatter_rows(src, idx1, idx2, n_out):
    t, d = src.shape
    win = SC_WINDOW

    @pl.kernel(out_type=jax.ShapeDtypeStruct((n_out, d), src.dtype), mesh=_sc_mesh(), name="moe_dispatch_sc")
    def scatter_rows(x_hbm, i1_hbm, i2_hbm, o_hbm):
        def body(x_vmem, i1_vmem, i2_vmem):
            pltpu.sync_copy(x_vmem, o_hbm.at[i1_vmem.at[0]])
            pltpu.sync_copy(x_vmem, o_hbm.at[i2_vmem.at[0]])

        pltpu.emit_pipeline(
            body,
            grid=(t // win,),
            in_specs=[pl.BlockSpec((win, d), lambda i: (i, 0)),
                      pl.BlockSpec((1, win), lambda i: (0, i)),
                      pl.BlockSpec((1, win), lambda i: (0, i))],
            out_specs=[],
            core_axis_name=("core", "subcore"),
            dimension_semantics=(pltpu.PARALLEL,),
        )(x_hbm, i1_hbm, i2_hbm)

    return scatter_rows(src, idx1.reshape(1, t), idx2.reshape(1, t))


def _gather_rows(src, idx):
    m = idx.shape[0]
    d = src.shape[1]
    win = SC_WINDOW

    @pl.kernel(out_type=jax.ShapeDtypeStruct((m, d), src.dtype), mesh=_sc_mesh(), name="moe_gather_sc")
    def gather(x_hbm, i_hbm, o_hbm):
        def body(i_vmem, o_vmem):
            pltpu.sync_copy(x_hbm.at[i_vmem.at[0]], o_vmem)

        pltpu.emit_pipeline(
            body,
            grid=(m // win,),
            in_specs=[pl.BlockSpec((1, win), lambda i: (0, i))],
            out_specs=[pl.BlockSpec((win, d), lambda i: (i, 0))],
            core_axis_name=("core", "subcore"),
            dimension_semantics=(pltpu.PARALLEL,),
        )(i_hbm, o_hbm)

    return gather(src, idx.reshape(1, m))


def _expert_kernel(tile_ref, exp_ref, lo_ref, hi_ref, xs_ref, w1_ref, w3_ref, w2_ref, ys_ref, w1_scr, w3_scr, w2_scr):
    s = pl.program_id(0)
    prev = jnp.maximum(s - 1, 0)
    new_expert = (s == 0) | (exp_ref[s] != exp_ref[prev])
    new_tile = (s == 0) | (tile_ref[s] != tile_ref[prev])

    @pl.when(new_expert)
    def _():
        w1_scr[...] = w1_ref[...].astype(BF16)
        w3_scr[...] = w3_ref[...].astype(BF16)
        w2_scr[...] = w2_ref[...].astype(BF16)

    x = _unpack_bf16_pairs(jnp.concatenate([xs_ref[k] for k in range(N_SLAB)], axis=-1)).astype(BF16)
    a = jnp.dot(x, w1_scr[...], preferred_element_type=F32)
    b = jnp.dot(x, w3_scr[...], preferred_element_type=F32)
    act = _silu(a) * b
    y = _pack_bf16_pairs(jnp.dot(act.astype(BF16), w2_scr[...], preferred_element_type=F32))
    row = lax.broadcasted_iota(jnp.int32, (y.shape[0], SLAB), 0)
    mine = (row >= lo_ref[s]) & (row < hi_ref[s])

    @pl.when(new_tile)
    def _():
        for k in range(N_SLAB):
            ys_ref[k] = jnp.where(mine, y[:, k * SLAB:(k + 1) * SLAB], jnp.uint32(0))

    @pl.when(jnp.logical_not(new_tile))
    def _():
        for k in range(N_SLAB):
            ys_ref[k] = jnp.where(mine, y[:, k * SLAB:(k + 1) * SLAB], ys_ref[k])


def _experts(step_tile, step_expert, step_lo, step_hi, xs, w1, w3, w2, layer):
    n_slab, ns, slab = xs.shape
    d = w1.shape[1]
    ff = w1.shape[2]
    n_steps = step_tile.shape[0]
    base = layer * MOE_EXPERTS
    grid_spec = pltpu.PrefetchScalarGridSpec(
        num_scalar_prefetch=4,
        grid=(n_steps,),
        in_specs=[pl.BlockSpec((n_slab, TM_X, slab), lambda s, tl, ex, lo, hi: (0, tl[s], 0)),
                  pl.BlockSpec((None, d, ff), lambda s, tl, ex, lo, hi: (base + ex[s], 0, 0)),
                  pl.BlockSpec((None, d, ff), lambda s, tl, ex, lo, hi: (base + ex[s], 0, 0)),
                  pl.BlockSpec((None, ff, d), lambda s, tl, ex, lo, hi: (base + ex[s], 0, 0))],
        out_specs=pl.BlockSpec((n_slab, TM_X, slab), lambda s, tl, ex, lo, hi: (0, tl[s], 0)),
        scratch_shapes=[pltpu.VMEM((d, ff), BF16), pltpu.VMEM((d, ff), BF16), pltpu.VMEM((ff, d), BF16)],
    )
    return pl.pallas_call(
        _expert_kernel,
        grid_spec=grid_spec,
        out_shape=jax.ShapeDtypeStruct((n_slab, ns, slab), xs.dtype),
        compiler_params=_cparams(("arbitrary",)),
        name="moe_experts",
    )(step_tile, step_expert, step_lo, step_hi, xs, w1, w3, w2)


def _combine_kernel(x_ref, gate_ref, fw_ref, wcol_ref, y1_ref, y2_ref, o_ref, *, final):
    w_first = wcol_ref[:, 0:1]
    w_second = wcol_ref[:, 1:2]
    y_first = _unpack_bf16_pairs(jnp.concatenate([y1_ref[k] for k in range(N_SLAB)], axis=-1))
    y_second = _unpack_bf16_pairs(jnp.concatenate([y2_ref[k] for k in range(N_SLAB)], axis=-1))
    moe = w_first * y_first + w_second * y_second
    x = x_ref[...] + gate_ref[...] * moe
    if final:
        x = x * lax.rsqrt(jnp.mean(x * x, axis=-1, keepdims=True) + EPS) * fw_ref[...]
    o_ref[...] = x


def _combine(x2, mod3, final_w_row, wcol, gathered, seq, final):
    t, d = x2.shape
    tm = TM_COMB
    tpb = seq // tm
    nblk = t // tm
    yspec = lambda off: pl.BlockSpec((N_SLAB, tm, SLAB), lambda i: (0, i + off, 0))
    return pl.pallas_call(
        functools.partial(_combine_kernel, final=final),
        grid=(nblk,),
        in_specs=[pl.BlockSpec((tm, d), lambda i: (i, 0)),
                  pl.BlockSpec((None, 1, d), lambda i: ((i // tpb) * 6 + 5, 0, 0)),
                  pl.BlockSpec((1, d), lambda i: (0, 0)),
                  pl.BlockSpec((tm, LANE), lambda i: (i, 0)),
                  yspec(0), yspec(nblk)],
        out_specs=pl.BlockSpec((tm, d), lambda i: (i, 0)),
        out_shape=jax.ShapeDtypeStruct((t, d), F32),
        compiler_params=_cparams(("parallel",)),
        name="moe_combine",
    )(x2, mod3, final_w_row, wcol, gathered, gathered)


def _moe(x2, mod3, final_w_row, w_route, b_route, tri_excl, w1, w3, w2, layer, seq, final):
    t, d = x2.shape
    h3, ids, wcol, counts = _router(x2, mod3, w_route, b_route, tri_excl, seq)
    cnt = counts[:, 0].astype(jnp.int32)
    ends = jnp.cumsum(cnt)
    offs = ends - cnt
    experts = jnp.arange(MOE_EXPERTS, dtype=jnp.int32)
    pick = lambda table, idx: jnp.sum(jnp.where(idx[:, None] == experts[None, :], table[None, :], 0), axis=1)
    slot1 = pick(offs, ids[0]) + ids[2]
    slot2 = pick(offs, ids[1]) + ids[3]
    n_tiles = 2 * t // TM_X
    first_tile = offs // TM_X
    n_vis = jnp.where(cnt > 0, (ends - 1) // TM_X - first_tile + 1, 0)
    cum = jnp.cumsum(n_vis)
    step = jnp.arange(n_tiles + MOE_EXPERTS, dtype=jnp.int32)
    step_expert = jnp.minimum(jnp.sum(step[:, None] >= cum[None, :], axis=1), MOE_EXPERTS - 1).astype(jnp.int32)
    valid = step < cum[-1]
    step_tile = jnp.where(valid, pick(first_tile - (cum - n_vis), step_expert) + step, n_tiles - 1)
    step_lo = jnp.where(valid, jnp.clip(pick(offs, step_expert) - step_tile * TM_X, 0, TM_X), 0)
    step_hi = jnp.where(valid, jnp.clip(pick(ends, step_expert) - step_tile * TM_X, 0, TM_X), 0)
    xs = _dispatch(slot1, slot2, h3)
    ys = _experts(step_tile.astype(jnp.int32), step_expert, step_lo.astype(jnp.int32), step_hi.astype(jnp.int32),
                  xs, w1, w3, w2, layer)
    n_sorted = ys.shape[1]
    gathered = _gather_rows(ys.reshape(N_SLAB * n_sorted, SLAB), _slab_rows(jnp.concatenate([slot1, slot2]), n_sorted))
    gathered = gathered.reshape(N_SLAB, n_sorted, SLAB)
    return _combine(x2, mod3, final_w_row, wcol, gathered, seq, final)


def kernel(x, c, positions, ada_w, ada_b, w_in, s5_lam_re, s5_lam_im, s5_b_re, s5_b_im, s5_c_re, s5_c_im, s5_d, s5_log_dt, s5_w_glu, hg_lb_logits, hg_norm_w, m2_conv_w, m2_conv_b, m2_dt_bias, m2_a_log, m2_d, m2_norm_w, w_branch, w_gate, b_gate, w_out, moe_w_group, moe_b_group, moe_w_expert, moe_b_expert, moe_w1, moe_w3, moe_w2, final_norm_w):
    bsz, seq, d = x.shape
    t = bsz * seq
    depth = ada_w.shape[0]
    assert seq % TM_PROJ == 0 and seq % C_RET == 0 and seq % C_SSD == 0 and seq % C_HG == 0
    x2 = x.reshape(t, d).astype(F32)

    c_pad = jnp.zeros((8, d), F32).at[:bsz].set(c.astype(F32))
    mod_all = _ada_mod(c_pad, ada_w.astype(F32), ada_b.astype(F32))

    half = RET_DK // 2
    inv_freq = ROPE_BASE ** (-jnp.arange(half, dtype=F32) / half)
    invf_col = jnp.broadcast_to(inv_freq[:, None], (half, LANE))
    expand = np.tile(np.eye(half, dtype=np.float32), (1, 2 * RET_HEADS))
    sign = np.tile(np.concatenate([-np.ones(half), np.ones(half)]), RET_HEADS)[None, :].astype(np.float32)
    cos_t, sin_t = _rope_tables(positions.reshape(1, t).astype(jnp.int32), invf_col,
                                jnp.asarray(expand), jnp.asarray(expand * sign))
    cos3 = cos_t.reshape(bsz, seq, -1)
    sin3 = sin_t.reshape(bsz, seq, -1)

    lb_cum = jnp.cumsum(jax.nn.softmax(hg_lb_logits.astype(F32), axis=0), axis=0)
    hg_lb = lb_cum - lb_cum[:1]
    tri_ssd = jnp.asarray(np.tril(np.ones((C_SSD, C_SSD), np.float32)))
    tri_excl = jnp.asarray(np.triu(np.ones((TM_PROJ, TM_PROJ), np.float32), 1), BF16)
    final_w_row = final_norm_w.astype(F32)[None, :]
    w_pad = jnp.zeros((depth, d, IN_W_PAD), BF16).at[:, :, :IN_W].set(w_in.astype(BF16))
    w_glu_bf = s5_w_glu.astype(BF16)
    w_branch_bf = w_branch.astype(BF16)
    w_gate_bf = w_gate.astype(BF16)
    w_out_bf = w_out.astype(BF16)
    b_gate3 = b_gate.astype(F32).reshape(depth, 1, -1)
    moe_w1_all = moe_w1.astype(F32).reshape(depth * MOE_EXPERTS, d, MOE_FF)
    moe_w3_all = moe_w3.astype(F32).reshape(depth * MOE_EXPERTS, d, MOE_FF)
    moe_w2_all = moe_w2.astype(F32).reshape(depth * MOE_EXPERTS, MOE_FF, d)

    for layer in range(depth):
        mod3 = mod_all[layer, :bsz].reshape(bsz * 6, 1, d)
        p, u_s5 = _in_proj(x2, mod3, w_pad, layer, seq)
        p3 = p.reshape(bsz, seq, IN_W_PAD)

        ops = _s5_operators(s5_lam_re[layer], s5_lam_im[layer], s5_b_re[layer], s5_b_im[layer],
                            s5_c_re[layer], s5_c_im[layer], s5_d[layer], s5_log_dt[layer])
        y_s5 = _s5_scan(u_s5.reshape(bsz, seq, BRANCH_W), *ops).reshape(t, BRANCH_W)

        lb = hg_lb[layer][None, :]
        y_hg = _hgrn2(p3, jnp.log(lb), jnp.log1p(-lb), hg_norm_w[layer].astype(F32)[None, :]).reshape(t, BRANCH_W)

        y_ret = _retention(p3, cos3, sin3).reshape(t, BRANCH_W)

        pad8 = lambda v: jnp.zeros((1, LANE), F32).at[0, :M2_HEADS].set(v.astype(F32))
        y_m2 = _ssd(p3, tri_ssd, m2_conv_w[layer].astype(F32), m2_conv_b[layer].astype(F32)[None, :],
                    pad8(m2_dt_bias[layer]), pad8(m2_a_log[layer]),
                    jnp.repeat(m2_d[layer].astype(F32), M2_HEADDIM)[None, :],
                    m2_norm_w[layer].astype(F32)[None, :]).reshape(t, BRANCH_W)

        x2 = _merge(x2, mod3, y_s5, y_hg, y_ret, y_m2, w_glu_bf, w_branch_bf, w_gate_bf, b_gate3, w_out_bf,
                    layer, seq)

        nr = 40
        w_route = jnp.zeros((nr, d), F32).at[:MOE_GROUPS].set(moe_w_group[layer].astype(F32).T)
        w_route = w_route.at[MOE_GROUPS:MOE_GROUPS + MOE_EXPERTS].set(moe_w_expert[layer].astype(F32).T)
        b_route = jnp.zeros((nr, LANE), F32).at[:MOE_GROUPS, 0].set(moe_b_group[layer].astype(F32))
        b_route = b_route.at[MOE_GROUPS:MOE_GROUPS + MOE_EXPERTS, 0].set(moe_b_expert[layer].astype(F32))
        x2 = _moe(x2, mod3, final_w_row, w_route, b_route, tri_excl, moe_w1_all, moe_w3_all, moe_w2_all,
                  layer, seq, final=(layer == depth - 1))
    return x2.reshape(bsz, seq, d)
```

```python
import functools
import math

import numpy as np
import jax
import jax.numpy as jnp
from jax import lax
from jax.experimental import pallas as pl
from jax.experimental.pallas import tpu as pltpu
from jax.experimental.pallas import tpu_sc as plsc

F32 = jnp.float32
BF16 = jnp.bfloat16
HIGHEST = lax.Precision.HIGHEST

D_MODEL = 1024
DEPTH = 2
BRANCH_W = 512
EPS = 1e-6
S5_GROUPS = 32
S5_CH = 16
S5_STATE = 64
S5_MAX_REAL = -1e-4
S5_BLOCK = 16
HG_HEADS = 4
HG_DK = 128
RET_HEADS = 4
RET_DK = 64
RET_DV = 128
ROPE_BASE = 10000.0
M2_HEADS = 8
M2_HEADDIM = 64
M2_GROUPS = 2
M2_STATE = 128
M2_CONV = 4
MOE_GROUPS = 4
MOE_EPG = 8
MOE_EXPERTS = MOE_GROUPS * MOE_EPG
MOE_FF = 256

COL_S5, COL_HQ, COL_HF, COL_HI, COL_HG = 0, 512, 1024, 1536, 2048
COL_RQ, COL_RK, COL_RV, COL_RG = 2560, 2816, 3072, 3584
COL_MZ, COL_MXS, COL_MBC, COL_MDT = 4096, 4608, 5120, 5632
IN_W = 5640
IN_W_PAD = 5888

LANE = 128
VMEM_LIMIT = 56 * 1024 * 1024

TM_PROJ = 1024
TN_PROJ = 1024
TM_MERGE = 1024
TM_INPROJ = 512
C_RET = 256
C_SSD = 256
C_HG = 128
TM_X = 512
TM_COMB = 512
SC_WINDOW = 128
SLAB = 256
N_SLAB = D_MODEL // 2 // SLAB


def _cparams(sem):
    return pltpu.CompilerParams(dimension_semantics=sem, vmem_limit_bytes=VMEM_LIMIT)


def _silu(v):
    return v * jax.nn.sigmoid(v)


def _dot_nt(a, b, **kw):
    return lax.dot_general(a, b, (((1,), (1,)), ((), ())), preferred_element_type=F32, **kw)


def _dot_tn(a, b, **kw):
    return lax.dot_general(a, b, (((0,), (0,)), ((), ())), preferred_element_type=F32, **kw)


def _ada_kernel(c_ref, w_ref, b_ref, o_ref):
    cond = _silu(c_ref[...])
    o_ref[...] = jnp.dot(cond, w_ref[...], preferred_element_type=F32, precision=HIGHEST) + b_ref[...]


def _ada_mod(c_pad, ada_w, ada_b):
    depth, d, n = ada_w.shape
    tn = 1536
    return pl.pallas_call(
        _ada_kernel,
        grid=(depth, n // tn),
        in_specs=[pl.BlockSpec((8, d), lambda l, j: (0, 0)),
                  pl.BlockSpec((None, d, tn), lambda l, j: (l, 0, j)),
                  pl.BlockSpec((None, 1, tn), lambda l, j: (l, 0, j))],
        out_specs=pl.BlockSpec((None, 8, tn), lambda l, j: (l, 0, j)),
        out_shape=jax.ShapeDtypeStruct((depth, 8, n), F32),
        compiler_params=_cparams(("parallel", "parallel")),
        name="ada_mod",
    )(c_pad, ada_w, ada_b.reshape(depth, 1, n))


def _pack_bf16_pairs(x):
    n = x.shape[1] // 2
    lo = pltpu.bitcast(x[:, :n].astype(BF16).astype(F32), jnp.uint32) >> 16
    hi = pltpu.bitcast(x[:, n:].astype(BF16).astype(F32), jnp.uint32)
    return hi | lo


def _unpack_bf16_pairs(w):
    lo = pltpu.bitcast(w << 16, F32)
    hi = pltpu.bitcast(w & jnp.uint32(0xFFFF0000), F32)
    return jnp.concatenate([lo, hi], axis=-1)


def _modulated_norm(x, scale, shift):
    ms = jnp.mean(x * x, axis=-1, keepdims=True)
    return x * lax.rsqrt(ms + EPS) * (1.0 + scale) + shift


def _inproj_kernel(x_ref, sc_ref, sh_ref, w_ref, o_ref, u_ref):
    h = _modulated_norm(x_ref[...], sc_ref[...], sh_ref[...]).astype(BF16)
    n_total = o_ref.shape[1]
    for n0 in range(0, n_total, TN_PROJ):
        n1 = min(n0 + TN_PROJ, n_total)
        p = jnp.dot(h, w_ref[:, n0:n1], preferred_element_type=F32)
        o_ref[:, n0:n1] = p.astype(o_ref.dtype)
        if n0 == 0:
            u_ref[...] = p[:, COL_S5:COL_S5 + BRANCH_W]


def _in_proj(x2, mod3, w_pad, layer, seq):
    t, d = x2.shape
    tm = TM_INPROJ
    tpb = seq // tm
    assert COL_S5 + BRANCH_W <= TN_PROJ
    return pl.pallas_call(
        _inproj_kernel,
        grid=(t // tm,),
        in_specs=[pl.BlockSpec((tm, d), lambda i: (i, 0)),
                  pl.BlockSpec((None, 1, d), lambda i: ((i // tpb) * 6 + 1, 0, 0)),
                  pl.BlockSpec((None, 1, d), lambda i: ((i // tpb) * 6 + 0, 0, 0)),
                  pl.BlockSpec((None, d, IN_W_PAD), lambda i: (layer, 0, 0), pipeline_mode=pl.Buffered(1))],
        out_specs=[pl.BlockSpec((tm, IN_W_PAD), lambda i: (i, 0)),
                   pl.BlockSpec((tm, BRANCH_W), lambda i: (i, 0))],
        out_shape=[jax.ShapeDtypeStruct((t, IN_W_PAD), BF16), jax.ShapeDtypeStruct((t, BRANCH_W), F32)],
        compiler_params=_cparams(("parallel",)),
        name="in_proj",
    )(x2, mod3, mod3, w_pad)


def _rope_kernel(pos_ref, invf_ref, ecos_ref, esin_ref, cos_ref, sin_ref):
    ang = invf_ref[:, 0:1] * pos_ref[...].astype(F32)
    cos_ref[...] = _dot_tn(jnp.cos(ang), ecos_ref[...], precision=HIGHEST)
    sin_ref[...] = _dot_tn(jnp.sin(ang), esin_ref[...], precision=HIGHEST)


def _rope_tables(pos_row, invf_col, expand_cos, expand_sin):
    t = pos_row.shape[1]
    half, w = expand_cos.shape
    tm = 1024
    const = lambda shape: pl.BlockSpec(shape, lambda i: (0, 0))
    return pl.pallas_call(
        _rope_kernel,
        grid=(t // tm,),
        in_specs=[pl.BlockSpec((1, tm), lambda i: (0, i)), const((half, LANE)), const((half, w)), const((half, w))],
        out_specs=[pl.BlockSpec((tm, w), lambda i: (i, 0))] * 2,
        out_shape=[jax.ShapeDtypeStruct((t, w), F32)] * 2,
        compiler_params=_cparams(("parallel",)),
        name="rope_tables",
    )(pos_row, invf_col, expand_cos, expand_sin)


def _ret_kernel(q_ref, k_ref, v_ref, g_ref, cos_ref, sin_ref, o_ref, st_ref, dec_ref, *, chunk):
    @pl.when(pl.program_id(1) == 0)
    def _():
        st_ref[...] = jnp.zeros_like(st_ref)
        ti = lax.broadcasted_iota(jnp.int32, (chunk, chunk), 0)
        si = lax.broadcasted_iota(jnp.int32, (chunk, chunk), 1)
        lag = (ti - si).astype(F32)
        for h in range(RET_HEADS):
            log_gamma = math.log1p(-(2.0 ** (-5.0 - h)))
            dec_ref[h] = jnp.where(ti >= si, jnp.exp(jnp.minimum(lag * log_gamma, 0.0)), 0.0)

    cosf = cos_ref[...]
    sinf = sin_ref[...]
    width = RET_HEADS * RET_DK
    lane = lax.broadcasted_iota(jnp.int32, (chunk, width), 1)
    first_half = (lane % RET_DK) < (RET_DK // 2)

    def rope(t):
        partner = jnp.where(first_half, pltpu.roll(t, width - RET_DK // 2, 1), pltpu.roll(t, RET_DK // 2, 1))
        return t * cosf + partner * sinf

    q = rope(q_ref[...].astype(F32))
    k = rope(k_ref[...].astype(F32)) * (RET_DK ** -0.5)
    v = v_ref[...]
    g = g_ref[...].astype(F32)
    tcol = lax.broadcasted_iota(jnp.int32, (chunk, 1), 0).astype(F32)
    for h in range(RET_HEADS):
        log_gamma = math.log1p(-(2.0 ** (-5.0 - h)))
        qh = q[:, h * RET_DK:(h + 1) * RET_DK]
        kh = k[:, h * RET_DK:(h + 1) * RET_DK]
        vh = v[:, h * RET_DV:(h + 1) * RET_DV].astype(BF16)
        scores = _dot_nt(qh.astype(BF16), kh.astype(BF16)) * dec_ref[h]
        state = st_ref[h]
        q_in = qh * jnp.exp(log_gamma * (tcol + 1.0))
        o = (jnp.dot(scores.astype(BF16), vh, preferred_element_type=F32)
             + jnp.dot(q_in.astype(BF16), state.astype(BF16), preferred_element_type=F32))
        k_out = kh * jnp.exp(log_gamma * (chunk - 1.0 - tcol))
        st_ref[h] = math.exp(log_gamma * chunk) * state + _dot_tn(k_out.astype(BF16), vh)
        o = o * lax.rsqrt(jnp.mean(o * o, axis=-1, keepdims=True) + EPS)
        gh = g[:, h * RET_DV:(h + 1) * RET_DV]
        o_ref[:, h * RET_DV:(h + 1) * RET_DV] = (o * _silu(gh)).astype(o_ref.dtype)


def _retention(p3, cos3, sin3):
    b, seq, _ = p3.shape
    c = C_RET
    qk_w = RET_HEADS * RET_DK
    return pl.pallas_call(
        functools.partial(_ret_kernel, chunk=c),
        grid=(b, seq // c),
        in_specs=[pl.BlockSpec((None, c, qk_w), lambda i, j: (i, j, COL_RQ // qk_w)),
                  pl.BlockSpec((None, c, qk_w), lambda i, j: (i, j, COL_RK // qk_w)),
                  pl.BlockSpec((None, c, BRANCH_W), lambda i, j: (i, j, COL_RV // BRANCH_W)),
                  pl.BlockSpec((None, c, BRANCH_W), lambda i, j: (i, j, COL_RG // BRANCH_W)),
                  pl.BlockSpec((None, c, qk_w), lambda i, j: (i, j, 0)),
                  pl.BlockSpec((None, c, qk_w), lambda i, j: (i, j, 0))],
        out_specs=pl.BlockSpec((None, c, BRANCH_W), lambda i, j: (i, j, 0)),
        out_shape=jax.ShapeDtypeStruct((b, seq, BRANCH_W), BF16),
        scratch_shapes=[pltpu.VMEM((RET_HEADS, RET_DK, RET_DV), F32), pltpu.VMEM((RET_HEADS, c, c), F32)],
        compiler_params=_cparams(("parallel", "arbitrary")),
        name="retention",
    )(p3, p3, p3, p3, cos3, sin3)


def _ssd_kernel(z_ref, xs_ref, bc_ref, dt_ref, tri_ref, cw_ref, cb_ref, dtb_ref, alog_ref, dsk_ref, nw_ref,
                o_ref, xe_scr, st_ref, *, chunk):
    j = pl.program_id(1)
    width = 2 * BRANCH_W

    @pl.when(j == 0)
    def _():
        st_ref[...] = jnp.zeros_like(st_ref)
        xe_scr[0:8, :] = jnp.zeros((8, width), F32)

    @pl.when(j > 0)
    def _():
        xe_scr[0:8, :] = xe_scr[chunk:chunk + 8, :]

    xe_scr[8:, 0:BRANCH_W] = xs_ref[...].astype(F32)
    xe_scr[8:, BRANCH_W:] = bc_ref[...].astype(F32)
    conv = cb_ref[...] + cw_ref[M2_CONV - 1:M2_CONV, :] * xe_scr[8:, :]
    for tap in range(M2_CONV - 1):
        conv = conv + cw_ref[tap:tap + 1, :] * xe_scr[pl.ds(8 - (M2_CONV - 1) + tap, chunk), :]
    conv = _silu(conv)
    xs = conv[:, :BRANCH_W]
    bm = conv[:, BRANCH_W:BRANCH_W + M2_GROUPS * M2_STATE]
    cm = conv[:, BRANCH_W + M2_GROUPS * M2_STATE:]

    dt = jax.nn.softplus(dt_ref[...].astype(F32) + dtb_ref[...])
    da = dt * (-jnp.exp(alog_ref[...]))
    da_hi = da.astype(BF16)
    da_r = da - da_hi.astype(F32)
    da_mid = da_r.astype(BF16)
    da_lo = (da_r - da_mid.astype(F32)).astype(BF16)
    tri = tri_ref[...]
    a_cs = (jnp.dot(tri, da_hi, preferred_element_type=F32) + jnp.dot(tri, da_mid, preferred_element_type=F32)
            + jnp.dot(tri, da_lo, preferred_element_type=F32))
    a_cs_t = a_cs.T
    ti = lax.broadcasted_iota(jnp.int32, (chunk, chunk), 0)
    si = lax.broadcasted_iota(jnp.int32, (chunk, chunk), 1)
    causal = ti >= si
    hpg = M2_HEADS // M2_GROUPS
    ys = []
    for grp in range(M2_GROUPS):
        bm_g = bm[:, grp * M2_STATE:(grp + 1) * M2_STATE]
        cm_g = cm[:, grp * M2_STATE:(grp + 1) * M2_STATE]
        cb = _dot_nt(cm_g.astype(BF16), bm_g.astype(BF16))
        for hh in range(hpg):
            h = grp * hpg + hh
            col = a_cs[:, h:h + 1]
            row = a_cs_t[h:h + 1, :]
            lmat = jnp.where(causal, jnp.exp(jnp.minimum(col - row, 0.0)), 0.0)
            xd = xs[:, h * M2_HEADDIM:(h + 1) * M2_HEADDIM] * dt[:, h:h + 1]
            state = st_ref[h]
            y = (jnp.dot((cb * lmat).astype(BF16), xd.astype(BF16), preferred_element_type=F32)
                 + jnp.dot((cm_g * jnp.exp(col)).astype(BF16), state.astype(BF16), preferred_element_type=F32))
            a_last = a_cs[chunk - 1:chunk, h:h + 1]
            to_end = jnp.exp(a_last - col)
            st_ref[h] = jnp.exp(a_last) * state + _dot_tn(bm_g.astype(BF16), (xd * to_end).astype(BF16))
            ys.append(y)
    y = jnp.concatenate(ys, axis=-1) + dsk_ref[...] * xs
    y = y * _silu(z_ref[...].astype(F32))
    o_ref[...] = (y * lax.rsqrt(jnp.mean(y * y, axis=-1, keepdims=True) + EPS) * nw_ref[...]).astype(o_ref.dtype)


def _ssd(p3, tri, conv_w, conv_b, dt_bias_row, a_log_row, d_skip_row, norm_w_row):
    b, seq, _ = p3.shape
    c = C_SSD
    const = lambda shape: pl.BlockSpec(shape, lambda i, j: (0,) * len(shape))
    return pl.pallas_call(
        functools.partial(_ssd_kernel, chunk=c),
        grid=(b, seq // c),
        in_specs=[pl.BlockSpec((None, c, BRANCH_W), lambda i, j: (i, j, COL_MZ // BRANCH_W)),
                  pl.BlockSpec((None, c, BRANCH_W), lambda i, j: (i, j, COL_MXS // BRANCH_W)),
                  pl.BlockSpec((None, c, BRANCH_W), lambda i, j: (i, j, COL_MBC // BRANCH_W)),
                  pl.BlockSpec((None, c, LANE), lambda i, j: (i, j, COL_MDT // LANE)),
                  const((c, c)), const((M2_CONV, 2 * BRANCH_W)), const((1, 2 * BRANCH_W)),
                  const((1, LANE)), const((1, LANE)), const((1, BRANCH_W)), const((1, BRANCH_W))],
        out_specs=pl.BlockSpec((None, c, BRANCH_W), lambda i, j: (i, j, 0)),
        out_shape=jax.ShapeDtypeStruct((b, seq, BRANCH_W), BF16),
        scratch_shapes=[pltpu.VMEM((c + 8, 2 * BRANCH_W), F32),
                        pltpu.VMEM((M2_HEADS, M2_STATE, M2_HEADDIM), F32)],
        compiler_params=_cparams(("parallel", "arbitrary")),
        name="ssd",
    )(p3, p3, p3, p3, tri, conv_w, conv_b, dt_bias_row, a_log_row, d_skip_row, norm_w_row)


def _hg_tables(chunk):
    n_lev = int(math.log2(chunk))
    r = np.arange(chunk)[:, None]
    jj = np.arange(chunk)[None, :]
    tri = (jj <= r).astype(np.float32)
    x = r ^ jj
    levmap = np.where(r > jj, np.floor(np.log2(x + 0.5)), np.where(r == jj, -1, -2)).astype(np.int32)
    return tri, levmap, n_lev


def _hg_level_exponent(b, lev):
    rows, width = b.shape
    m = 1 << lev
    sub = 8
    if 2 * m >= sub:
        blocks = b.reshape(rows // (2 * m), 2 * m, width)
        mid = jnp.broadcast_to(blocks[:, m - 1:m, :], blocks.shape).reshape(rows, width)
    else:
        groups = b.reshape(rows // sub, sub, width)
        row_in_group = lax.broadcasted_iota(jnp.int32, groups.shape, 1)
        mid = None
        for start in range(0, sub, 2 * m):
            picked = jnp.broadcast_to(groups[:, start + m - 1:start + m, :], groups.shape)
            mid = picked if mid is None else jnp.where(row_in_group >= start, picked, mid)
        mid = mid.reshape(rows, width)
    return -jnp.abs(b - mid)


def _hg_kernel(q_ref, f_ref, i_ref, g_ref, sum_ref, lev_ref, llb_ref, l1m_ref, nw_ref, o_ref, st_ref,
               *, chunk, n_lev):
    @pl.when(pl.program_id(1) == 0)
    def _():
        st_ref[...] = jnp.zeros_like(st_ref)

    f = f_ref[...].astype(F32)
    log_sig = jnp.minimum(f, 0.0) - jnp.log1p(jnp.exp(-jnp.abs(f)))
    a = llb_ref[...]
    bb = l1m_ref[...] + log_sig
    log_f = jnp.maximum(a, bb) + jnp.log1p(jnp.exp(-jnp.abs(a - bb)))
    k_all = jnp.exp(l1m_ref[...]) * jax.nn.sigmoid(-f)
    q_all = _silu(q_ref[...].astype(F32))
    hi = log_f.astype(BF16)
    r1 = log_f - hi.astype(F32)
    mid = r1.astype(BF16)
    lo = (r1 - mid.astype(F32)).astype(BF16)
    tri = sum_ref[...]
    b_all = (jnp.dot(tri, hi, preferred_element_type=F32)
             + jnp.dot(tri, mid, preferred_element_type=F32)
             + jnp.dot(tri, lo, preferred_element_type=F32))
    to_end_all = b_all[chunk - 1:chunk, :] - b_all
    level_decay = [jnp.exp(_hg_level_exponent(b_all, lev)) for lev in range(n_lev)]
    levmap = lev_ref[...]
    v_all = i_ref[...]
    g_all = g_ref[...].astype(F32)
    for h in range(HG_HEADS):
        sl = slice(h * HG_DK, (h + 1) * HG_DK)
        qh = q_all[:, sl]
        kh = k_all[:, sl]
        vh = v_all[:, sl].astype(BF16)
        b_h = b_all[:, sl]
        to_end = to_end_all[:, sl]
        amat = jnp.where(levmap == -1, _dot_nt(qh.astype(BF16), kh.astype(BF16)), 0.0)
        for lev in range(n_lev):
            e = level_decay[lev][:, sl]
            a_l = _dot_nt((qh * e).astype(BF16), (kh * e).astype(BF16))
            amat = jnp.where(levmap == lev, a_l, amat)
        state_t = st_ref[h]
        o = (jnp.dot(amat.astype(BF16), vh, preferred_element_type=F32)
             + _dot_nt((qh * jnp.exp(b_h)).astype(BF16), state_t.astype(BF16)))
        k_end = kh * jnp.exp(to_end)
        st_ref[h] = jnp.exp(b_h[chunk - 1:chunk, :]) * state_t + _dot_tn(vh, k_end.astype(BF16))
        o = o * lax.rsqrt(jnp.mean(o * o, axis=-1, keepdims=True) + EPS) * nw_ref[...]
        o_ref[:, sl] = (o * _silu(g_all[:, sl])).astype(o_ref.dtype)


def _hgrn2(p3, log_lb_row, log1m_lb_row, norm_w_row):
    b, seq, _ = p3.shape
    c = C_HG
    tri, levmap, n_lev = _hg_tables(c)
    const = lambda shape: pl.BlockSpec(shape, lambda i, j: (0,) * len(shape))
    blk = lambda col: pl.BlockSpec((None, c, BRANCH_W), lambda i, j: (i, j, col // BRANCH_W))
    return pl.pallas_call(
        functools.partial(_hg_kernel, chunk=c, n_lev=n_lev),
        grid=(b, seq // c),
        in_specs=[blk(COL_HQ), blk(COL_HF), blk(COL_HI), blk(COL_HG),
                  const((c, c)), const((c, c)),
                  const((1, BRANCH_W)), const((1, BRANCH_W)), const((1, HG_DK))],
        out_specs=pl.BlockSpec((None, c, BRANCH_W), lambda i, j: (i, j, 0)),
        out_shape=jax.ShapeDtypeStruct((b, seq, BRANCH_W), BF16),
        scratch_shapes=[pltpu.VMEM((HG_HEADS, HG_DK, HG_DK), F32)],
        compiler_params=_cparams(("parallel", "arbitrary")),
        name="hgrn2",
    )(p3, p3, p3, p3, jnp.asarray(tri, BF16), jnp.asarray(levmap), log_lb_row, log1m_lb_row, norm_w_row)


def _expand_block_diag(comp_ref, e_ref, dst_ref, row_div, lane_div):
    gq = LANE // S5_CH
    rows, ncols = dst_ref.shape
    step = 512
    comp = comp_ref[...]
    row_grp = (lax.broadcasted_iota(jnp.int32, (rows, step), 0) // row_div) % gq
    for c0 in range(0, ncols, step):
        lane_grp = ((lax.broadcasted_iota(jnp.int32, (rows, step), 1) + c0) // lane_div) % gq
        full = jnp.dot(comp, e_ref[:, c0:c0 + step], preferred_element_type=F32)
        dst_ref[:, c0:c0 + step] = jnp.where(row_grp == lane_grp, full, 0.0).astype(dst_ref.dtype)


def _s5_kernel(u_ref, k2_ref, bc_ref, cc_ref, esc_ref, eb_ref, lam_ref, o_ref, tc_scr, tq_ref, bq_ref, cq_ref,
               x_scr, w_scr, s_scr, *, rows):
    nb = S5_BLOCK

    @pl.when(pl.program_id(1) == 0)
    def _():
        k2 = k2_ref[...]
        lane = lax.broadcasted_iota(jnp.int32, k2.shape, 1)
        for t in range(nb):
            shifted = k2 if t == 0 else jnp.where(lane >= t * S5_CH, pltpu.roll(k2, t * S5_CH, 1), 0.0)
            tc_scr[t * LANE:(t + 1) * LANE, :] = shifted.astype(tc_scr.dtype)
        _expand_block_diag(tc_scr, esc_ref, tq_ref, S5_CH, S5_CH)
        _expand_block_diag(bc_ref, eb_ref, bq_ref, S5_CH, S5_STATE)
        _expand_block_diag(cc_ref, esc_ref, cq_ref, S5_STATE, S5_CH)

    for t in range(nb):
        x_scr[:, t * LANE:(t + 1) * LANE] = u_ref[pl.ds(t, rows, stride=nb), :].astype(x_scr.dtype)
    x = x_scr[...]
    half = w_scr.shape[1] // 2
    w_scr[...] = jnp.dot(x, bq_ref[...], preferred_element_type=F32)
    lam_re = lam_ref[0:1, :]
    lam_im = lam_ref[1:2, :]

    def body(j, carry):
        s_re, s_im = carry
        s_scr[pl.ds(j, 1), 0:half] = s_re
        s_scr[pl.ds(j, 1), half:] = s_im
        w_re = w_scr[pl.ds(j, 1), 0:half]
        w_im = w_scr[pl.ds(j, 1), half:]
        return lam_re * s_re - lam_im * s_im + w_re, lam_re * s_im + lam_im * s_re + w_im

    zero = jnp.zeros((1, half), F32)
    lax.fori_loop(0, rows, body, (zero, zero))
    y = (jnp.dot(x, tq_ref[...], preferred_element_type=F32)
         + jnp.dot(s_scr[...].astype(BF16), cq_ref[...], preferred_element_type=F32))
    for t in range(nb):
        o_ref[pl.ds(t, rows, stride=nb), :] = y[:, t * LANE:(t + 1) * LANE]


def _s5_scan(p3, k2, bc, cc, lam16):
    batch, seq, _ = p3.shape
    nb = S5_BLOCK
    nq = BRANCH_W // LANE
    rows = seq // nb
    kdim = nb * LANE
    gq = LANE // S5_CH
    ncol = 2 * gq * S5_STATE
    e_sc = (np.eye(nb)[:, None, :, None, None] * np.eye(S5_CH)[None, :, None, None, :] * np.ones((1, 1, 1, gq, 1)))
    e_sc = e_sc.reshape(nb * S5_CH, nb * gq * S5_CH)
    e_b = (np.eye(2)[:, None, :, None, None] * np.eye(S5_STATE)[None, :, None, None, :] * np.ones((1, 1, 1, gq, 1)))
    e_b = e_b.reshape(2 * S5_STATE, ncol)
    full = lambda shape: pl.BlockSpec(shape, lambda q, b: (0,) * len(shape))
    per_q = lambda r, c: pl.BlockSpec((None, r, c), lambda q, b: (q, 0, 0))
    return pl.pallas_call(
        functools.partial(_s5_kernel, rows=rows),
        grid=(nq, batch),
        in_specs=[pl.BlockSpec((None, seq, LANE), lambda q, b: (b, 0, q)),
                  per_q(LANE, nb * S5_CH), per_q(kdim, 2 * S5_STATE), per_q(ncol, nb * S5_CH),
                  full(e_sc.shape), full(e_b.shape), per_q(2, ncol // 2)],
        out_specs=pl.BlockSpec((None, seq, LANE), lambda q, b: (b, 0, q)),
        out_shape=jax.ShapeDtypeStruct((batch, seq, BRANCH_W), F32),
        scratch_shapes=[pltpu.VMEM((kdim, nb * S5_CH), BF16),
                        pltpu.VMEM((kdim, kdim), BF16), pltpu.VMEM((kdim, ncol), BF16), pltpu.VMEM((ncol, kdim), BF16),
                        pltpu.VMEM((rows, kdim), BF16), pltpu.VMEM((rows, ncol), F32), pltpu.VMEM((rows, ncol), F32)],
        compiler_params=_cparams(("parallel", "arbitrary")),
        name="s5_scan",
    )(p3, k2, bc, cc, jnp.asarray(e_sc, BF16), jnp.asarray(e_b, BF16), lam16)


def _s5_operators(lam_re, lam_im, b_re, b_im, c_re, c_im, d_skip, log_dt):
    nb = S5_BLOCK
    gq = LANE // S5_CH
    nq = S5_GROUPS // gq
    lam = lax.complex(jnp.minimum(lam_re.astype(F32), S5_MAX_REAL), lam_im.astype(F32))
    step = jnp.exp(log_dt.astype(F32))[:, None]
    z = lam * step
    lam_bar = jnp.exp(z)
    b_bar = ((lam_bar - 1.0) / lam)[..., None] * lax.complex(b_re.astype(F32), b_im.astype(F32))
    c_mat = lax.complex(c_re.astype(F32), c_im.astype(F32))
    pw = jnp.exp(z[..., None] * jnp.arange(nb + 1, dtype=F32))
    cp = c_mat[:, None, :, :] * pw[..., :nb].transpose(0, 2, 1)[:, :, None, :]
    cp = jnp.concatenate([cp.real, -cp.imag], axis=-1).reshape(S5_GROUPS, nb * S5_CH, 2 * S5_STATE)
    bri = jnp.concatenate([b_bar.real, b_bar.imag], axis=1)
    kern = jnp.einsum('gnk,gki->gin', cp, bri, precision=HIGHEST)
    skip = (jnp.asarray(np.concatenate([np.eye(S5_CH), np.zeros((S5_CH, (nb - 1) * S5_CH))], axis=1), F32)[None]
            * d_skip.astype(F32).reshape(S5_GROUPS, S5_CH, 1))
    k2 = (kern + skip).reshape(nq, gq * S5_CH, nb * S5_CH)
    pw_rev = jnp.exp(z[..., None] * jnp.asarray(np.arange(nb - 1, -1, -1), F32))
    binc = pw_rev[:, :, :, None] * b_bar[:, :, None, :]
    binc = jnp.stack([binc.real, binc.imag], axis=0).reshape(2, nq, gq, S5_STATE, nb, S5_CH)
    bc = binc.transpose(1, 4, 2, 5, 0, 3).reshape(nq, nb * LANE, 2 * S5_STATE)
    cm = c_mat.transpose(0, 2, 1)[:, :, None, :] * pw[..., 1:][:, :, :, None]
    cm = jnp.stack([cm.real, -cm.imag], axis=0).reshape(2, nq, gq * S5_STATE, nb * S5_CH)
    cc = cm.transpose(1, 0, 2, 3).reshape(nq, 2 * gq * S5_STATE, nb * S5_CH)
    lam_n = pw[..., nb].reshape(nq, gq * S5_STATE)
    lam16 = jnp.stack([lam_n.real, lam_n.imag], axis=1)
    return k2, bc.astype(BF16), cc.astype(BF16), lam16


def _merge_kernel(x_ref, sc_ref, sh_ref, gm_ref, ys5_ref, yhg_ref, yret_ref, ym2_ref,
                  wglu_ref, wbr_ref, wg_ref, bg_ref, wout_ref, o_ref):
    x = x_ref[...]
    d = x.shape[1]
    h = _modulated_norm(x, sc_ref[...], sh_ref[...]).astype(BF16)
    y_s5 = jax.nn.gelu(ys5_ref[...])
    y_s5 = y_s5 * jax.nn.sigmoid(jnp.dot(y_s5.astype(BF16), wglu_ref[...], preferred_element_type=F32))
    acc = jnp.zeros(x.shape, F32)
    for n, y in enumerate((y_s5, yhg_ref[...], yret_ref[...], ym2_ref[...])):
        gate = jax.nn.sigmoid(jnp.dot(h, wg_ref[:, n * d:(n + 1) * d], preferred_element_type=F32)
                              + bg_ref[:, n * d:(n + 1) * d])
        acc = acc + gate * jnp.dot(y.astype(BF16), wbr_ref[n], preferred_element_type=F32)
    o_ref[...] = x + gm_ref[...] * jnp.dot(acc.astype(BF16), wout_ref[...], preferred_element_type=F32)


def _merge(x2, mod3, ys5, yhg, yret, ym2, w_glu, w_branch, w_gate, b_gate, w_out, layer, seq):
    t, d = x2.shape
    tm = TM_MERGE
    tpb = seq // tm
    const = lambda shape: pl.BlockSpec((None,) + shape, lambda i: (layer,) + (0,) * len(shape),
                                       pipeline_mode=pl.Buffered(1))
    modspec = lambda k: pl.BlockSpec((None, 1, d), lambda i: ((i // tpb) * 6 + k, 0, 0))
    yspec = pl.BlockSpec((tm, BRANCH_W), lambda i: (i, 0))
    return pl.pallas_call(
        _merge_kernel,
        grid=(t // tm,),
        in_specs=[pl.BlockSpec((tm, d), lambda i: (i, 0)), modspec(1), modspec(0), modspec(2),
                  yspec, yspec, yspec, yspec,
                  const((BRANCH_W, BRANCH_W)), const((4, BRANCH_W, d)), const((d, 4 * d)), const((1, 4 * d)),
                  const((d, d))],
        out_specs=pl.BlockSpec((tm, d), lambda i: (i, 0)),
        out_shape=jax.ShapeDtypeStruct((t, d), F32),
        compiler_params=_cparams(("parallel",)),
        name="merge",
    )(x2, mod3, mod3, mod3, ys5, yhg, yret, ym2, w_glu, w_branch, w_gate, b_gate, w_out)


def _router_kernel(x_ref, sc_ref, sh_ref, wr_ref, br_ref, tri_ref, h_ref, ids_ref, wts_ref, cnt_ref, carry):
    i = pl.program_id(0)

    @pl.when(i == 0)
    def _():
        carry[...] = jnp.zeros_like(carry)

    h = _modulated_norm(x_ref[...], sc_ref[...], sh_ref[...])
    tm, d = h.shape
    packed = _pack_bf16_pairs(h)
    for k in range(N_SLAB):
        h_ref[k] = packed[:, k * SLAB:(k + 1) * SLAB]
    h_hi = h.astype(BF16)
    h_lo = (h - h_hi.astype(F32)).astype(BF16)
    w_r = wr_ref[...]
    w_hi = w_r.astype(BF16)
    w_lo = (w_r - w_hi.astype(F32)).astype(BF16)
    logits = _dot_nt(w_hi, h_hi) + _dot_nt(w_hi, h_lo) + _dot_nt(w_lo, h_hi) + br_ref[:, 0:1]
    gl = [logits[g:g + 1, :] for g in range(MOE_GROUPS)]
    gmax = gl[0]
    gsel = jnp.zeros((1, tm), jnp.int32)
    for g in range(1, MOE_GROUPS):
        better = gl[g] > gmax
        gsel = jnp.where(better, g, gsel)
        gmax = jnp.where(better, gl[g], gmax)
    gden = gl[0] * 0.0
    for g in range(MOE_GROUPS):
        gden = gden + jnp.exp(gl[g] - gmax)
    g_w = 1.0 / gden
    el = []
    for e in range(MOE_EPG):
        v = logits[MOE_GROUPS + e:MOE_GROUPS + e + 1, :]
        for g in range(1, MOE_GROUPS):
            row = MOE_GROUPS + g * MOE_EPG + e
            v = jnp.where(gsel == g, logits[row:row + 1, :], v)
        el.append(v)
    v1 = el[0]
    i1 = jnp.zeros((1, tm), jnp.int32)
    for e in range(1, MOE_EPG):
        better = el[e] > v1
        i1 = jnp.where(better, e, i1)
        v1 = jnp.where(better, el[e], v1)
    v2 = jnp.full((1, tm), -jnp.inf, F32)
    i2 = jnp.zeros((1, tm), jnp.int32)
    for e in range(MOE_EPG):
        better = (el[e] > v2) & (i1 != e)
        i2 = jnp.where(better, e, i2)
        v2 = jnp.where(better, el[e], v2)
    ex = jnp.exp(v2 - v1)
    p1 = 1.0 / (1.0 + ex)
    e1 = gsel * MOE_EPG + i1
    e2 = gsel * MOE_EPG + i2
    erow = lax.broadcasted_iota(jnp.int32, (MOE_EXPERTS, tm), 0)
    oh1 = (erow == e1).astype(F32)
    oh2 = (erow == e2).astype(F32)
    both = oh1 + oh2
    prefix = jnp.dot(both.astype(BF16), tri_ref[...], preferred_element_type=F32) + carry[:, 0:1]
    rank1 = jnp.sum(oh1 * prefix, axis=0, keepdims=True).astype(jnp.int32)
    rank2 = jnp.sum(oh2 * prefix, axis=0, keepdims=True).astype(jnp.int32)
    carry[...] = carry[...] + jnp.sum(both, axis=1, keepdims=True)
    zi = jnp.zeros((1, tm), jnp.int32)
    ids_ref[...] = jnp.concatenate([e1, e2, rank1, rank2, zi, zi, zi, zi], axis=0)
    wrow = lax.broadcasted_iota(jnp.int32, (LANE, tm), 0)
    wts_ref[...] = jnp.where(wrow == 0, p1 * g_w, jnp.where(wrow == 1, ex * p1 * g_w, 0.0)).T
    cnt_ref[...] = carry[...]


def _router(x2, mod3, w_route, b_route, tri_excl, seq):
    t, d = x2.shape
    tm = TM_PROJ
    tpb = seq // tm
    nr = w_route.shape[0]
    const = lambda shape: pl.BlockSpec(shape, lambda i: (0,) * len(shape))
    modspec = lambda k: pl.BlockSpec((None, 1, d), lambda i: ((i // tpb) * 6 + k, 0, 0))
    return pl.pallas_call(
        _router_kernel,
        grid=(t // tm,),
        in_specs=[pl.BlockSpec((tm, d), lambda i: (i, 0)), modspec(4), modspec(3),
                  const((nr, d)), const((nr, LANE)), const((tm, tm))],
        out_specs=[pl.BlockSpec((N_SLAB, tm, SLAB), lambda i: (0, i, 0)),
                   pl.BlockSpec((8, tm), lambda i: (0, i)),
                   pl.BlockSpec((tm, LANE), lambda i: (i, 0)),
                   const((MOE_EXPERTS, LANE))],
        out_shape=[jax.ShapeDtypeStruct((N_SLAB, t, SLAB), jnp.uint32),
                   jax.ShapeDtypeStruct((8, t), jnp.int32),
                   jax.ShapeDtypeStruct((t, LANE), F32),
                   jax.ShapeDtypeStruct((MOE_EXPERTS, LANE), F32)],
        scratch_shapes=[pltpu.VMEM((MOE_EXPERTS, LANE), F32)],
        compiler_params=_cparams(("arbitrary",)),
        name="moe_router",
    )(x2, mod3, mod3, w_route, b_route, tri_excl)


def _sc_mesh():
    return plsc.VectorSubcoreMesh(core_axis_name="core", subcore_axis_name="subcore")


def _slab_rows(idx, n_rows):
    return (idx[None, :] + (jnp.arange(N_SLAB, dtype=jnp.int32) * n_rows)[:, None]).reshape(-1)


def _dispatch(slot1, slot2, h_slabs):
    n_slab, t, d = h_slabs.shape
    n_out = 2 * t
    xs = _scatter_rows(h_slabs.reshape(n_slab * t, d), _slab_rows(slot1, n_out), _slab_rows(slot2, n_out),
                       n_slab * n_out)
    return xs.reshape(n_slab, n_out, d)


def _scatter_rows(src, idx1, idx2, n_out):
    t, d = src.shape
    win = SC_WINDOW

    @pl.kernel(out_type=jax.ShapeDtypeStruct((n_out, d), src.dtype), mesh=_sc_mesh(), name="moe_dispatch_sc")
    def scatter_rows(x_hbm, i1_hbm, i2_hbm, o_hbm):
        def body(x_vmem, i1_vmem, i2_vmem):
            pltpu.sync_copy(x_vmem, o_hbm.at[i1_vmem.at[0]])
            pltpu.sync_copy(x_vmem, o_hbm.at[i2_vmem.at[0]])

        pltpu.emit_pipeline(
            body,
            grid=(t // win,),
            in_specs=[pl.BlockSpec((win, d), lambda i: (i, 0)),
                      pl.BlockSpec((1, win), lambda i: (0, i)),
                      pl.BlockSpec((1, win), lambda i: (0, i))],
            out_specs=[],
            core_axis_name=("core", "subcore"),
            dimension_semantics=(pltpu.PARALLEL,),
        )(x_hbm, i1_hbm, i2_hbm)

    return scatter_rows(src, idx1.reshape(1, t), idx2.reshape(1, t))


def _gather_rows(src, idx):
    m = idx.shape[0]
    d = src.shape[1]
    win = SC_WINDOW

    @pl.kernel(out_type=jax.ShapeDtypeStruct((m, d), src.dtype), mesh=_sc_mesh(), name="moe_gather_sc")
    def gather(x_hbm, i_hbm, o_hbm):
        def body(i_vmem, o_vmem):
            pltpu.sync_copy(x_hbm.at[i_vmem.at[0]], o_vmem)

        pltpu.emit_pipeline(
            body,
            grid=(m // win,),
            in_specs=[pl.BlockSpec((1, win), lambda i: (0, i))],
            out_specs=[pl.BlockSpec((win, d), lambda i: (i, 0))],
            core_axis_name=("core", "subcore"),
            dimension_semantics=(pltpu.PARALLEL,),
        )(i_hbm, o_hbm)

    return gather(src, idx.reshape(1, m))


def _expert_kernel(tile_ref, exp_ref, lo_ref, hi_ref, xs_ref, w1_ref, w3_ref, w2_ref, ys_ref, w1_scr, w3_scr, w2_scr):
    s = pl.program_id(0)
    prev = jnp.maximum(s - 1, 0)
    new_expert = (s == 0) | (exp_ref[s] != exp_ref[prev])
    new_tile = (s == 0) | (tile_ref[s] != tile_ref[prev])

    @pl.when(new_expert)
    def _():
        w1_scr[...] = w1_ref[...].astype(BF16)
        w3_scr[...] = w3_ref[...].astype(BF16)
        w2_scr[...] = w2_ref[...].astype(BF16)

    x = _unpack_bf16_pairs(jnp.concatenate([xs_ref[k] for k in range(N_SLAB)], axis=-1)).astype(BF16)
    a = jnp.dot(x, w1_scr[...], preferred_element_type=F32)
    b = jnp.dot(x, w3_scr[...], preferred_element_type=F32)
    act = _silu(a) * b
    y = _pack_bf16_pairs(jnp.dot(act.astype(BF16), w2_scr[...], preferred_element_type=F32))
    row = lax.broadcasted_iota(jnp.int32, (y.shape[0], SLAB), 0)
    mine = (row >= lo_ref[s]) & (row < hi_ref[s])

    @pl.when(new_tile)
    def _():
        for k in range(N_SLAB):
            ys_ref[k] = jnp.where(mine, y[:, k * SLAB:(k + 1) * SLAB], jnp.uint32(0))

    @pl.when(jnp.logical_not(new_tile))
    def _():
        for k in range(N_SLAB):
            ys_ref[k] = jnp.where(mine, y[:, k * SLAB:(k + 1) * SLAB], ys_ref[k])


def _experts(step_tile, step_expert, step_lo, step_hi, xs, w1, w3, w2, layer):
    n_slab, ns, slab = xs.shape
    d = w1.shape[1]
    ff = w1.shape[2]
    n_steps = step_tile.shape[0]
    base = layer * MOE_EXPERTS
    grid_spec = pltpu.PrefetchScalarGridSpec(
        num_scalar_prefetch=4,
        grid=(n_steps,),
        in_specs=[pl.BlockSpec((n_slab, TM_X, slab), lambda s, tl, ex, lo, hi: (0, tl[s], 0)),
                  pl.BlockSpec((None, d, ff), lambda s, tl, ex, lo, hi: (base + ex[s], 0, 0)),
                  pl.BlockSpec((None, d, ff), lambda s, tl, ex, lo, hi: (base + ex[s], 0, 0)),
                  pl.BlockSpec((None, ff, d), lambda s, tl, ex, lo, hi: (base + ex[s], 0, 0))],
        out_specs=pl.BlockSpec((n_slab, TM_X, slab), lambda s, tl, ex, lo, hi: (0, tl[s], 0)),
        scratch_shapes=[pltpu.VMEM((d, ff), BF16), pltpu.VMEM((d, ff), BF16), pltpu.VMEM((ff, d), BF16)],
    )
    return pl.pallas_call(
        _expert_kernel,
        grid_spec=grid_spec,
        out_shape=jax.ShapeDtypeStruct((n_slab, ns, slab), xs.dtype),
        compiler_params=_cparams(("arbitrary",)),
        name="moe_experts",
    )(step_tile, step_expert, step_lo, step_hi, xs, w1, w3, w2)


def _combine_kernel(x_ref, gate_ref, fw_ref, wcol_ref, y1_ref, y2_ref, o_ref, *, final):
    w_first = wcol_ref[:, 0:1]
    w_second = wcol_ref[:, 1:2]
    y_first = _unpack_bf16_pairs(jnp.concatenate([y1_ref[k] for k in range(N_SLAB)], axis=-1))
    y_second = _unpack_bf16_pairs(jnp.concatenate([y2_ref[k] for k in range(N_SLAB)], axis=-1))
    moe = w_first * y_first + w_second * y_second
    x = x_ref[...] + gate_ref[...] * moe
    if final:
        x = x * lax.rsqrt(jnp.mean(x * x, axis=-1, keepdims=True) + EPS) * fw_ref[...]
    o_ref[...] = x


def _combine(x2, mod3, final_w_row, wcol, gathered, seq, final):
    t, d = x2.shape
    tm = TM_COMB
    tpb = seq // tm
    nblk = t // tm
    yspec = lambda off: pl.BlockSpec((N_SLAB, tm, SLAB), lambda i: (0, i + off, 0))
    return pl.pallas_call(
        functools.partial(_combine_kernel, final=final),
        grid=(nblk,),
        in_specs=[pl.BlockSpec((tm, d), lambda i: (i, 0)),
                  pl.BlockSpec((None, 1, d), lambda i: ((i // tpb) * 6 + 5, 0, 0)),
                  pl.BlockSpec((1, d), lambda i: (0, 0)),
                  pl.BlockSpec((tm, LANE), lambda i: (i, 0)),
                  yspec(0), yspec(nblk)],
        out_specs=pl.BlockSpec((tm, d), lambda i: (i, 0)),
        out_shape=jax.ShapeDtypeStruct((t, d), F32),
        compiler_params=_cparams(("parallel",)),
        name="moe_combine",
    )(x2, mod3, final_w_row, wcol, gathered, gathered)


def _moe(x2, mod3, final_w_row, w_route, b_route, tri_excl, w1, w3, w2, layer, seq, final):
    t, d = x2.shape
    h3, ids, wcol, counts = _router(x2, mod3, w_route, b_route, tri_excl, seq)
    cnt = counts[:, 0].astype(jnp.int32)
    ends = jnp.cumsum(cnt)
    offs = ends - cnt
    experts = jnp.arange(MOE_EXPERTS, dtype=jnp.int32)
    pick = lambda table, idx: jnp.sum(jnp.where(idx[:, None] == experts[None, :], table[None, :], 0), axis=1)
    slot1 = pick(offs, ids[0]) + ids[2]
    slot2 = pick(offs, ids[1]) + ids[3]
    n_tiles = 2 * t // TM_X
    first_tile = offs // TM_X
    n_vis = jnp.where(cnt > 0, (ends - 1) // TM_X - first_tile + 1, 0)
    cum = jnp.cumsum(n_vis)
    step = jnp.arange(n_tiles + MOE_EXPERTS, dtype=jnp.int32)
    step_expert = jnp.minimum(jnp.sum(step[:, None] >= cum[None, :], axis=1), MOE_EXPERTS - 1).astype(jnp.int32)
    valid = step < cum[-1]
    step_tile = jnp.where(valid, pick(first_tile - (cum - n_vis), step_expert) + step, n_tiles - 1)
    step_lo = jnp.where(valid, jnp.clip(pick(offs, step_expert) - step_tile * TM_X, 0, TM_X), 0)
    step_hi = jnp.where(valid, jnp.clip(pick(ends, step_expert) - step_tile * TM_X, 0, TM_X), 0)
    xs = _dispatch(slot1, slot2, h3)
    ys = _experts(step_tile.astype(jnp.int32), step_expert, step_lo.astype(jnp.int32), step_hi.astype(jnp.int32),
                  xs, w1, w3, w2, layer)
    n_sorted = ys.shape[1]
    gathered = _gather_rows(ys.reshape(N_SLAB * n_sorted, SLAB), _slab_rows(jnp.concatenate([slot1, slot2]), n_sorted))
    gathered = gathered.reshape(N_SLAB, n_sorted, SLAB)
    return _combine(x2, mod3, final_w_row, wcol, gathered, seq, final)


def kernel(x, c, positions, ada_w, ada_b, w_in, s5_lam_re, s5_lam_im, s5_b_re, s5_b_im, s5_c_re, s5_c_im, s5_d, s5_log_dt, s5_w_glu, hg_lb_logits, hg_norm_w, m2_conv_w, m2_conv_b, m2_dt_bias, m2_a_log, m2_d, m2_norm_w, w_branch, w_gate, b_gate, w_out, moe_w_group, moe_b_group, moe_w_expert, moe_b_expert, moe_w1, moe_w3, moe_w2, final_norm_w):
    bsz, seq, d = x.shape
    t = bsz * seq
    depth = ada_w.shape[0]
    assert seq % TM_PROJ == 0 and seq % C_RET == 0 and seq % C_SSD == 0 and seq % C_HG == 0
    x2 = x.reshape(t, d).astype(F32)

    c_pad = jnp.zeros((8, d), F32).at[:bsz].set(c.astype(F32))
    mod_all = _ada_mod(c_pad, ada_w.astype(F32), ada_b.astype(F32))

    half = RET_DK // 2
    inv_freq = ROPE_BASE ** (-jnp.arange(half, dtype=F32) / half)
    invf_col = jnp.broadcast_to(inv_freq[:, None], (half, LANE))
    expand = np.tile(np.eye(half, dtype=np.float32), (1, 2 * RET_HEADS))
    sign = np.tile(np.concatenate([-np.ones(half), np.ones(half)]), RET_HEADS)[None, :].astype(np.float32)
    cos_t, sin_t = _rope_tables(positions.reshape(1, t).astype(jnp.int32), invf_col,
                                jnp.asarray(expand), jnp.asarray(expand * sign))
    cos3 = cos_t.reshape(bsz, seq, -1)
    sin3 = sin_t.reshape(bsz, seq, -1)

    lb_cum = jnp.cumsum(jax.nn.softmax(hg_lb_logits.astype(F32), axis=0), axis=0)
    hg_lb = lb_cum - lb_cum[:1]
    tri_ssd = jnp.asarray(np.tril(np.ones((C_SSD, C_SSD), np.float32)), BF16)
    tri_excl = jnp.asarray(np.triu(np.ones((TM_PROJ, TM_PROJ), np.float32), 1), BF16)
    final_w_row = final_norm_w.astype(F32)[None, :]
    w_pad = jnp.zeros((depth, d, IN_W_PAD), BF16).at[:, :, :IN_W].set(w_in.astype(BF16))
    w_glu_bf = s5_w_glu.astype(BF16)
    w_branch_bf = w_branch.astype(BF16)
    w_gate_bf = w_gate.astype(BF16)
    w_out_bf = w_out.astype(BF16)
    b_gate3 = b_gate.astype(F32).reshape(depth, 1, -1)
    moe_w1_all = moe_w1.astype(F32).reshape(depth * MOE_EXPERTS, d, MOE_FF)
    moe_w3_all = moe_w3.astype(F32).reshape(depth * MOE_EXPERTS, d, MOE_FF)
    moe_w2_all = moe_w2.astype(F32).reshape(depth * MOE_EXPERTS, MOE_FF, d)

    for layer in range(depth):
        mod3 = mod_all[layer, :bsz].reshape(bsz * 6, 1, d)
        p, u_s5 = _in_proj(x2, mod3, w_pad, layer, seq)
        p3 = p.reshape(bsz, seq, IN_W_PAD)

        ops = _s5_operators(s5_lam_re[layer], s5_lam_im[layer], s5_b_re[layer], s5_b_im[layer],
                            s5_c_re[layer], s5_c_im[layer], s5_d[layer], s5_log_dt[layer])
        y_s5 = _s5_scan(u_s5.reshape(bsz, seq, BRANCH_W), *ops).reshape(t, BRANCH_W)

        lb = hg_lb[layer][None, :]
        y_hg = _hgrn2(p3, jnp.log(lb), jnp.log1p(-lb), hg_norm_w[layer].astype(F32)[None, :]).reshape(t, BRANCH_W)

        y_ret = _retention(p3, cos3, sin3).reshape(t, BRANCH_W)

        pad8 = lambda v: jnp.zeros((1, LANE), F32).at[0, :M2_HEADS].set(v.astype(F32))
        y_m2 = _ssd(p3, tri_ssd, m2_conv_w[layer].astype(F32), m2_conv_b[layer].astype(F32)[None, :],
                    pad8(m2_dt_bias[layer]), pad8(m2_a_log[layer]),
                    jnp.repeat(m2_d[layer].astype(F32), M2_HEADDIM)[None, :],
                    m2_norm_w[layer].astype(F32)[None, :]).reshape(t, BRANCH_W)

        x2 = _merge(x2, mod3, y_s5, y_hg, y_ret, y_m2, w_glu_bf, w_branch_bf, w_gate_bf, b_gate3, w_out_bf,
                    layer, seq)

        nr = 40
        w_route = jnp.zeros((nr, d), F32).at[:MOE_GROUPS].set(moe_w_group[layer].astype(F32).T)
        w_route = w_route.at[MOE_GROUPS:MOE_GROUPS + MOE_EXPERTS].set(moe_w_expert[layer].astype(F32).T)
        b_route = jnp.zeros((nr, LANE), F32).at[:MOE_GROUPS, 0].set(moe_b_group[layer].astype(F32))
        b_route = b_route.at[MOE_GROUPS:MOE_GROUPS + MOE_EXPERTS, 0].set(moe_b_expert[layer].astype(F32))
        x2 = _moe(x2, mod3, final_w_row, w_route, b_route, tri_excl, moe_w1_all, moe_w3_all, moe_w2_all,
                  layer, seq, final=(layer == depth - 1))
    return x2.reshape(bsz, seq, d)
```

```python
import functools
import math

import numpy as np
import jax
import jax.numpy as jnp
from jax import lax
from jax.experimental import pallas as pl
from jax.experimental.pallas import tpu as pltpu
from jax.experimental.pallas import tpu_sc as plsc

F32 = jnp.float32
BF16 = jnp.bfloat16
HIGHEST = lax.Precision.HIGHEST

D_MODEL = 1024
DEPTH = 2
BRANCH_W = 512
EPS = 1e-6
S5_GROUPS = 32
S5_CH = 16
S5_STATE = 64
S5_MAX_REAL = -1e-4
S5_BLOCK = 16
S5_SEQ_PER_STEP = 2
HG_HEADS = 4
HG_DK = 128
RET_HEADS = 4
RET_DK = 64
RET_DV = 128
ROPE_BASE = 10000.0
M2_HEADS = 8
M2_HEADDIM = 64
M2_GROUPS = 2
M2_STATE = 128
M2_CONV = 4
MOE_GROUPS = 4
MOE_EPG = 8
MOE_EXPERTS = MOE_GROUPS * MOE_EPG
MOE_FF = 256

COL_S5, COL_HQ, COL_HF, COL_HI, COL_HG = 0, 512, 1024, 1536, 2048
COL_RQ, COL_RK, COL_RV, COL_RG = 2560, 2816, 3072, 3584
COL_MZ, COL_MXS, COL_MBC, COL_MDT = 4096, 4608, 5120, 5632
IN_W = 5640
IN_W_PAD = 5888

LANE = 128
VMEM_LIMIT = 56 * 1024 * 1024

TM_PROJ = 1024
TN_PROJ = 1024
TM_MERGE = 1024
TM_INPROJ = 512
C_RET = 256
C_SSD = 256
C_HG = 128
TM_X = 512
TM_COMB = 512
SC_WINDOW = 128
SLAB = 256
N_SLAB = D_MODEL // 2 // SLAB


def _cparams(sem):
    return pltpu.CompilerParams(dimension_semantics=sem, vmem_limit_bytes=VMEM_LIMIT)


def _silu(v):
    return v * jax.nn.sigmoid(v)


def _dot_nt(a, b, **kw):
    return lax.dot_general(a, b, (((1,), (1,)), ((), ())), preferred_element_type=F32, **kw)


def _dot_tn(a, b, **kw):
    return lax.dot_general(a, b, (((0,), (0,)), ((), ())), preferred_element_type=F32, **kw)


def _ada_kernel(c_ref, w_ref, b_ref, o_ref):
    cond = _silu(c_ref[...])
    o_ref[...] = jnp.dot(cond, w_ref[...], preferred_element_type=F32, precision=HIGHEST) + b_ref[...]


def _ada_mod(c_pad, ada_w, ada_b):
    depth, d, n = ada_w.shape
    tn = 1536
    return pl.pallas_call(
        _ada_kernel,
        grid=(depth, n // tn),
        in_specs=[pl.BlockSpec((8, d), lambda l, j: (0, 0)),
                  pl.BlockSpec((None, d, tn), lambda l, j: (l, 0, j)),
                  pl.BlockSpec((None, 1, tn), lambda l, j: (l, 0, j))],
        out_specs=pl.BlockSpec((None, 8, tn), lambda l, j: (l, 0, j)),
        out_shape=jax.ShapeDtypeStruct((depth, 8, n), F32),
        compiler_params=_cparams(("parallel", "parallel")),
        name="ada_mod",
    )(c_pad, ada_w, ada_b.reshape(depth, 1, n))


def _pack_bf16_pairs(x):
    n = x.shape[1] // 2
    lo = pltpu.bitcast(x[:, :n].astype(BF16).astype(F32), jnp.uint32) >> 16
    hi = pltpu.bitcast(x[:, n:].astype(BF16).astype(F32), jnp.uint32)
    return hi | lo


def _unpack_bf16_pairs(w):
    lo = pltpu.bitcast(w << 16, F32)
    hi = pltpu.bitcast(w & jnp.uint32(0xFFFF0000), F32)
    return jnp.concatenate([lo, hi], axis=-1)


def _modulated_norm(x, scale, shift):
    ms = jnp.mean(x * x, axis=-1, keepdims=True)
    return x * lax.rsqrt(ms + EPS) * (1.0 + scale) + shift


def _inproj_kernel(x_ref, sc_ref, sh_ref, w_ref, o_ref, u_ref):
    h = _modulated_norm(x_ref[...], sc_ref[...], sh_ref[...]).astype(BF16)
    n_total = o_ref.shape[1]
    for n0 in range(0, n_total, TN_PROJ):
        n1 = min(n0 + TN_PROJ, n_total)
        p = jnp.dot(h, w_ref[:, n0:n1], preferred_element_type=F32)
        o_ref[:, n0:n1] = p.astype(o_ref.dtype)
        if n0 == 0:
            u_ref[...] = p[:, COL_S5:COL_S5 + BRANCH_W]


def _in_proj(x2, mod3, w_pad, layer, seq):
    t, d = x2.shape
    tm = TM_INPROJ
    tpb = seq // tm
    assert COL_S5 + BRANCH_W <= TN_PROJ
    return pl.pallas_call(
        _inproj_kernel,
        grid=(t // tm,),
        in_specs=[pl.BlockSpec((tm, d), lambda i: (i, 0)),
                  pl.BlockSpec((None, 1, d), lambda i: ((i // tpb) * 6 + 1, 0, 0)),
                  pl.BlockSpec((None, 1, d), lambda i: ((i // tpb) * 6 + 0, 0, 0)),
                  pl.BlockSpec((None, d, IN_W_PAD), lambda i: (layer, 0, 0), pipeline_mode=pl.Buffered(1))],
        out_specs=[pl.BlockSpec((tm, IN_W_PAD), lambda i: (i, 0)),
                   pl.BlockSpec((tm, BRANCH_W), lambda i: (i, 0))],
        out_shape=[jax.ShapeDtypeStruct((t, IN_W_PAD), BF16), jax.ShapeDtypeStruct((t, BRANCH_W), F32)],
        compiler_params=_cparams(("parallel",)),
        name="in_proj",
    )(x2, mod3, mod3, w_pad)


def _rope_kernel(pos_ref, invf_ref, ecos_ref, esin_ref, cos_ref, sin_ref):
    ang = invf_ref[:, 0:1] * pos_ref[...].astype(F32)
    def spread(values, e_ref):
        hi = values.astype(BF16)
        rest = values - hi.astype(F32)
        mid = rest.astype(BF16)
        lo = (rest - mid.astype(F32)).astype(BF16)
        e = e_ref[...]
        return _dot_tn(hi, e) + _dot_tn(mid, e) + _dot_tn(lo, e)

    cos_ref[...] = spread(jnp.cos(ang), ecos_ref)
    sin_ref[...] = spread(jnp.sin(ang), esin_ref)


def _rope_tables(pos_row, invf_col, expand_cos, expand_sin):
    t = pos_row.shape[1]
    half, w = expand_cos.shape
    tm = 1024
    const = lambda shape: pl.BlockSpec(shape, lambda i: (0, 0))
    return pl.pallas_call(
        _rope_kernel,
        grid=(t // tm,),
        in_specs=[pl.BlockSpec((1, tm), lambda i: (0, i)), const((half, LANE)), const((half, w)), const((half, w))],
        out_specs=[pl.BlockSpec((tm, w), lambda i: (i, 0))] * 2,
        out_shape=[jax.ShapeDtypeStruct((t, w), F32)] * 2,
        compiler_params=_cparams(("parallel",)),
        name="rope_tables",
    )(pos_row, invf_col, expand_cos, expand_sin)


def _ret_kernel(q_ref, k_ref, v_ref, g_ref, cos_ref, sin_ref, o_ref, st_ref, dec_ref, *, chunk):
    @pl.when(pl.program_id(1) == 0)
    def _():
        st_ref[...] = jnp.zeros_like(st_ref)
        ti = lax.broadcasted_iota(jnp.int32, (chunk, chunk), 0)
        si = lax.broadcasted_iota(jnp.int32, (chunk, chunk), 1)
        lag = (ti - si).astype(F32)
        for h in range(RET_HEADS):
            log_gamma = math.log1p(-(2.0 ** (-5.0 - h)))
            dec_ref[h] = jnp.where(ti >= si, jnp.exp(jnp.minimum(lag * log_gamma, 0.0)), 0.0)

    cosf = cos_ref[...]
    sinf = sin_ref[...]
    width = RET_HEADS * RET_DK
    lane = lax.broadcasted_iota(jnp.int32, (chunk, width), 1)
    first_half = (lane % RET_DK) < (RET_DK // 2)

    def rope(t):
        partner = jnp.where(first_half, pltpu.roll(t, width - RET_DK // 2, 1), pltpu.roll(t, RET_DK // 2, 1))
        return t * cosf + partner * sinf

    q = rope(q_ref[...].astype(F32))
    k = rope(k_ref[...].astype(F32)) * (RET_DK ** -0.5)
    v = v_ref[...]
    g = g_ref[...].astype(F32)
    tcol = lax.broadcasted_iota(jnp.int32, (chunk, 1), 0).astype(F32)
    for h in range(RET_HEADS):
        log_gamma = math.log1p(-(2.0 ** (-5.0 - h)))
        qh = q[:, h * RET_DK:(h + 1) * RET_DK]
        kh = k[:, h * RET_DK:(h + 1) * RET_DK]
        vh = v[:, h * RET_DV:(h + 1) * RET_DV].astype(BF16)
        scores = _dot_nt(qh.astype(BF16), kh.astype(BF16)) * dec_ref[h]
        state = st_ref[h]
        q_in = qh * jnp.exp(log_gamma * (tcol + 1.0))
        o = (jnp.dot(scores.astype(BF16), vh, preferred_element_type=F32)
             + jnp.dot(q_in.astype(BF16), state.astype(BF16), preferred_element_type=F32))
        k_out = kh * jnp.exp(log_gamma * (chunk - 1.0 - tcol))
        st_ref[h] = math.exp(log_gamma * chunk) * state + _dot_tn(k_out.astype(BF16), vh)
        o = o * lax.rsqrt(jnp.mean(o * o, axis=-1, keepdims=True) + EPS)
        gh = g[:, h * RET_DV:(h + 1) * RET_DV]
        o_ref[:, h * RET_DV:(h + 1) * RET_DV] = (o * _silu(gh)).astype(o_ref.dtype)


def _retention(p3, cos3, sin3):
    b, seq, _ = p3.shape
    c = C_RET
    qk_w = RET_HEADS * RET_DK
    return pl.pallas_call(
        functools.partial(_ret_kernel, chunk=c),
        grid=(b, seq // c),
        in_specs=[pl.BlockSpec((None, c, qk_w), lambda i, j: (i, j, COL_RQ // qk_w)),
                  pl.BlockSpec((None, c, qk_w), lambda i, j: (i, j, COL_RK // qk_w)),
                  pl.BlockSpec((None, c, BRANCH_W), lambda i, j: (i, j, COL_RV // BRANCH_W)),
                  pl.BlockSpec((None, c, BRANCH_W), lambda i, j: (i, j, COL_RG // BRANCH_W)),
                  pl.BlockSpec((None, c, qk_w), lambda i, j: (i, j, 0)),
                  pl.BlockSpec((None, c, qk_w), lambda i, j: (i, j, 0))],
        out_specs=pl.BlockSpec((None, c, BRANCH_W), lambda i, j: (i, j, 0)),
        out_shape=jax.ShapeDtypeStruct((b, seq, BRANCH_W), BF16),
        scratch_shapes=[pltpu.VMEM((RET_HEADS, RET_DK, RET_DV), F32), pltpu.VMEM((RET_HEADS, c, c), F32)],
        compiler_params=_cparams(("parallel", "arbitrary")),
        name="retention",
    )(p3, p3, p3, p3, cos3, sin3)


def _ssd_kernel(z_ref, xs_ref, bc_ref, dt_ref, tri_ref, cw_ref, cb_ref, dtb_ref, alog_ref, dsk_ref, nw_ref,
                o_ref, xe_scr, st_ref, *, chunk):
    j = pl.program_id(1)
    width = 2 * BRANCH_W

    @pl.when(j == 0)
    def _():
        st_ref[...] = jnp.zeros_like(st_ref)
        xe_scr[0:8, :] = jnp.zeros((8, width), F32)

    @pl.when(j > 0)
    def _():
        xe_scr[0:8, :] = xe_scr[chunk:chunk + 8, :]

    xe_scr[8:, 0:BRANCH_W] = xs_ref[...].astype(F32)
    xe_scr[8:, BRANCH_W:] = bc_ref[...].astype(F32)
    conv = cb_ref[...] + cw_ref[M2_CONV - 1:M2_CONV, :] * xe_scr[8:, :]
    for tap in range(M2_CONV - 1):
        conv = conv + cw_ref[tap:tap + 1, :] * xe_scr[pl.ds(8 - (M2_CONV - 1) + tap, chunk), :]
    conv = _silu(conv)
    xs = conv[:, :BRANCH_W]
    bm = conv[:, BRANCH_W:BRANCH_W + M2_GROUPS * M2_STATE]
    cm = conv[:, BRANCH_W + M2_GROUPS * M2_STATE:]

    dt = jax.nn.softplus(dt_ref[...].astype(F32) + dtb_ref[...])
    da = dt * (-jnp.exp(alog_ref[...]))
    da_hi = da.astype(BF16)
    da_r = da - da_hi.astype(F32)
    da_mid = da_r.astype(BF16)
    da_lo = (da_r - da_mid.astype(F32)).astype(BF16)
    tri = tri_ref[...]
    a_cs = (jnp.dot(tri, da_hi, preferred_element_type=F32) + jnp.dot(tri, da_mid, preferred_element_type=F32)
            + jnp.dot(tri, da_lo, preferred_element_type=F32))
    a_cs_t = a_cs.T
    ti = lax.broadcasted_iota(jnp.int32, (chunk, chunk), 0)
    si = lax.broadcasted_iota(jnp.int32, (chunk, chunk), 1)
    causal = ti >= si
    hpg = M2_HEADS // M2_GROUPS
    ys = []
    for grp in range(M2_GROUPS):
        bm_g = bm[:, grp * M2_STATE:(grp + 1) * M2_STATE]
        cm_g = cm[:, grp * M2_STATE:(grp + 1) * M2_STATE]
        cb = _dot_nt(cm_g.astype(BF16), bm_g.astype(BF16))
        for hh in range(hpg):
            h = grp * hpg + hh
            col = a_cs[:, h:h + 1]
            row = a_cs_t[h:h + 1, :]
            lmat = jnp.where(causal, jnp.exp(jnp.minimum(col - row, 0.0)), 0.0)
            xd = xs[:, h * M2_HEADDIM:(h + 1) * M2_HEADDIM] * dt[:, h:h + 1]
            state = st_ref[h]
            y = (jnp.dot((cb * lmat).astype(BF16), xd.astype(BF16), preferred_element_type=F32)
                 + jnp.dot((cm_g * jnp.exp(col)).astype(BF16), state.astype(BF16), preferred_element_type=F32))
            a_last = a_cs[chunk - 1:chunk, h:h + 1]
            to_end = jnp.exp(a_last - col)
            st_ref[h] = jnp.exp(a_last) * state + _dot_tn(bm_g.astype(BF16), (xd * to_end).astype(BF16))
            ys.append(y)
    y = jnp.concatenate(ys, axis=-1) + dsk_ref[...] * xs
    y = y * _silu(z_ref[...].astype(F32))
    o_ref[...] = (y * lax.rsqrt(jnp.mean(y * y, axis=-1, keepdims=True) + EPS) * nw_ref[...]).astype(o_ref.dtype)


def _ssd(p3, tri, conv_w, conv_b, dt_bias_row, a_log_row, d_skip_row, norm_w_row):
    b, seq, _ = p3.shape
    c = C_SSD
    const = lambda shape: pl.BlockSpec(shape, lambda i, j: (0,) * len(shape))
    return pl.pallas_call(
        functools.partial(_ssd_kernel, chunk=c),
        grid=(b, seq // c),
        in_specs=[pl.BlockSpec((None, c, BRANCH_W), lambda i, j: (i, j, COL_MZ // BRANCH_W)),
                  pl.BlockSpec((None, c, BRANCH_W), lambda i, j: (i, j, COL_MXS // BRANCH_W)),
                  pl.BlockSpec((None, c, BRANCH_W), lambda i, j: (i, j, COL_MBC // BRANCH_W)),
                  pl.BlockSpec((None, c, LANE), lambda i, j: (i, j, COL_MDT // LANE)),
                  const((c, c)), const((M2_CONV, 2 * BRANCH_W)), const((1, 2 * BRANCH_W)),
                  const((1, LANE)), const((1, LANE)), const((1, BRANCH_W)), const((1, BRANCH_W))],
        out_specs=pl.BlockSpec((None, c, BRANCH_W), lambda i, j: (i, j, 0)),
        out_shape=jax.ShapeDtypeStruct((b, seq, BRANCH_W), BF16),
        scratch_shapes=[pltpu.VMEM((c + 8, 2 * BRANCH_W), F32),
                        pltpu.VMEM((M2_HEADS, M2_STATE, M2_HEADDIM), F32)],
        compiler_params=_cparams(("parallel", "arbitrary")),
        name="ssd",
    )(p3, p3, p3, p3, tri, conv_w, conv_b, dt_bias_row, a_log_row, d_skip_row, norm_w_row)


def _hg_tables(chunk):
    n_lev = int(math.log2(chunk))
    r = np.arange(chunk)[:, None]
    jj = np.arange(chunk)[None, :]
    tri = (jj <= r).astype(np.float32)
    x = r ^ jj
    levmap = np.where(r > jj, np.floor(np.log2(x + 0.5)), np.where(r == jj, -1, -2)).astype(np.int32)
    return tri, levmap, n_lev


def _hg_level_exponent(b, lev):
    rows, width = b.shape
    m = 1 << lev
    sub = 8
    if 2 * m >= sub:
        blocks = b.reshape(rows // (2 * m), 2 * m, width)
        mid = jnp.broadcast_to(blocks[:, m - 1:m, :], blocks.shape).reshape(rows, width)
    else:
        groups = b.reshape(rows // sub, sub, width)
        row_in_group = lax.broadcasted_iota(jnp.int32, groups.shape, 1)
        mid = None
        for start in range(0, sub, 2 * m):
            picked = jnp.broadcast_to(groups[:, start + m - 1:start + m, :], groups.shape)
            mid = picked if mid is None else jnp.where(row_in_group >= start, picked, mid)
        mid = mid.reshape(rows, width)
    return -jnp.abs(b - mid)


def _hg_kernel(q_ref, f_ref, i_ref, g_ref, sum_ref, lev_ref, llb_ref, l1m_ref, nw_ref, o_ref, st_ref,
               *, chunk, n_lev):
    @pl.when(pl.program_id(1) == 0)
    def _():
        st_ref[...] = jnp.zeros_like(st_ref)

    f = f_ref[...].astype(F32)
    log_sig = jnp.minimum(f, 0.0) - jnp.log1p(jnp.exp(-jnp.abs(f)))
    a = llb_ref[...]
    bb = l1m_ref[...] + log_sig
    log_f = jnp.maximum(a, bb) + jnp.log1p(jnp.exp(-jnp.abs(a - bb)))
    k_all = jnp.exp(l1m_ref[...]) * jax.nn.sigmoid(-f)
    q_all = _silu(q_ref[...].astype(F32))
    hi = log_f.astype(BF16)
    r1 = log_f - hi.astype(F32)
    mid = r1.astype(BF16)
    lo = (r1 - mid.astype(F32)).astype(BF16)
    tri = sum_ref[...]
    b_all = (jnp.dot(tri, hi, preferred_element_type=F32)
             + jnp.dot(tri, mid, preferred_element_type=F32)
             + jnp.dot(tri, lo, preferred_element_type=F32))
    to_end_all = b_all[chunk - 1:chunk, :] - b_all
    level_decay = [jnp.exp(_hg_level_exponent(b_all, lev)) for lev in range(n_lev)]
    levmap = lev_ref[...]
    on_diag = levmap == -1
    on_level = [levmap == lev for lev in range(n_lev)]
    v_all = i_ref[...]
    g_all = g_ref[...].astype(F32)
    for h in range(HG_HEADS):
        sl = slice(h * HG_DK, (h + 1) * HG_DK)
        qh = q_all[:, sl]
        kh = k_all[:, sl]
        vh = v_all[:, sl].astype(BF16)
        b_h = b_all[:, sl]
        to_end = to_end_all[:, sl]
        amat = jnp.where(on_diag, _dot_nt(qh.astype(BF16), kh.astype(BF16)), 0.0)
        for lev in range(n_lev):
            e = level_decay[lev][:, sl]
            a_l = _dot_nt((qh * e).astype(BF16), (kh * e).astype(BF16))
            amat = jnp.where(on_level[lev], a_l, amat)
        state_t = st_ref[h]
        o = (jnp.dot(amat.astype(BF16), vh, preferred_element_type=F32)
             + _dot_nt((qh * jnp.exp(b_h)).astype(BF16), state_t.astype(BF16)))
        k_end = kh * jnp.exp(to_end)
        st_ref[h] = jnp.exp(b_h[chunk - 1:chunk, :]) * state_t + _dot_tn(vh, k_end.astype(BF16))
        o = o * lax.rsqrt(jnp.mean(o * o, axis=-1, keepdims=True) + EPS) * nw_ref[...]
        o_ref[:, sl] = (o * _silu(g_all[:, sl])).astype(o_ref.dtype)


def _hgrn2(p3, log_lb_row, log1m_lb_row, norm_w_row):
    b, seq, _ = p3.shape
    c = C_HG
    tri, levmap, n_lev = _hg_tables(c)
    const = lambda shape: pl.BlockSpec(shape, lambda i, j: (0,) * len(shape))
    blk = lambda col: pl.BlockSpec((None, c, BRANCH_W), lambda i, j: (i, j, col // BRANCH_W))
    return pl.pallas_call(
        functools.partial(_hg_kernel, chunk=c, n_lev=n_lev),
        grid=(b, seq // c),
        in_specs=[blk(COL_HQ), blk(COL_HF), blk(COL_HI), blk(COL_HG),
                  const((c, c)), const((c, c)),
                  const((1, BRANCH_W)), const((1, BRANCH_W)), const((1, HG_DK))],
        out_specs=pl.BlockSpec((None, c, BRANCH_W), lambda i, j: (i, j, 0)),
        out_shape=jax.ShapeDtypeStruct((b, seq, BRANCH_W), BF16),
        scratch_shapes=[pltpu.VMEM((HG_HEADS, HG_DK, HG_DK), F32)],
        compiler_params=_cparams(("parallel", "arbitrary")),
        name="hgrn2",
    )(p3, p3, p3, p3, jnp.asarray(tri, BF16), jnp.asarray(levmap), log_lb_row, log1m_lb_row, norm_w_row)


def _expand_block_diag(comp_ref, e_ref, dst_ref, row_div, lane_div):
    gq = LANE // S5_CH
    rows, ncols = dst_ref.shape
    step = 512
    comp = comp_ref[...]
    row_grp = (lax.broadcasted_iota(jnp.int32, (rows, step), 0) // row_div) % gq
    for c0 in range(0, ncols, step):
        lane_grp = ((lax.broadcasted_iota(jnp.int32, (rows, step), 1) + c0) // lane_div) % gq
        full = jnp.dot(comp, e_ref[:, c0:c0 + step], preferred_element_type=F32)
        dst_ref[:, c0:c0 + step] = jnp.where(row_grp == lane_grp, full, 0.0).astype(dst_ref.dtype)


def _s5_kernel(u_ref, k2_ref, bc_ref, cc_ref, esc_ref, eb_ref, lam_ref, o_ref, tc_scr, tq_ref, bq_ref, cq_ref,
               x_scr, w_scr, s_scr, *, rows):
    nb = S5_BLOCK

    @pl.when(pl.program_id(1) == 0)
    def _():
        k2 = k2_ref[...]
        lane = lax.broadcasted_iota(jnp.int32, k2.shape, 1)
        for t in range(nb):
            shifted = k2 if t == 0 else jnp.where(lane >= t * S5_CH, pltpu.roll(k2, t * S5_CH, 1), 0.0)
            tc_scr[t * LANE:(t + 1) * LANE, :] = shifted.astype(tc_scr.dtype)
        _expand_block_diag(tc_scr, esc_ref, tq_ref, S5_CH, S5_CH)
        _expand_block_diag(bc_ref, eb_ref, bq_ref, S5_CH, S5_STATE)
        _expand_block_diag(cc_ref, esc_ref, cq_ref, S5_STATE, S5_CH)

    n_seq = u_ref.shape[0]
    for b in range(n_seq):
        for t in range(nb):
            x_scr[b * rows:(b + 1) * rows, t * LANE:(t + 1) * LANE] = (
                u_ref[b, pl.ds(t, rows, stride=nb), :].astype(x_scr.dtype))
    x = x_scr[...]
    half = w_scr.shape[1] // 2
    w_scr[...] = jnp.dot(x, bq_ref[...], preferred_element_type=F32)
    lam_re = lam_ref[0:1, :]
    lam_im = lam_ref[1:2, :]

    def body(j, carry):
        out = []
        for b in range(n_seq):
            s_re, s_im = carry[2 * b], carry[2 * b + 1]
            r = b * rows + j
            s_scr[pl.ds(r, 1), 0:half] = s_re
            s_scr[pl.ds(r, 1), half:] = s_im
            w_re = w_scr[pl.ds(r, 1), 0:half]
            w_im = w_scr[pl.ds(r, 1), half:]
            out += [lam_re * s_re - lam_im * s_im + w_re, lam_re * s_im + lam_im * s_re + w_im]
        return tuple(out)

    zero = jnp.zeros((1, half), F32)
    lax.fori_loop(0, rows, body, (zero,) * (2 * n_seq))
    y = (jnp.dot(x, tq_ref[...], preferred_element_type=F32)
         + jnp.dot(s_scr[...].astype(BF16), cq_ref[...], preferred_element_type=F32))
    for b in range(n_seq):
        for t in range(nb):
            o_ref[b, pl.ds(t, rows, stride=nb), :] = y[b * rows:(b + 1) * rows, t * LANE:(t + 1) * LANE]


def _s5_scan(p3, k2, bc, cc, lam16):
    batch, seq, _ = p3.shape
    nb = S5_BLOCK
    nq = BRANCH_W // LANE
    rows = seq // nb
    kdim = nb * LANE
    gq = LANE // S5_CH
    ncol = 2 * gq * S5_STATE
    e_sc = (np.eye(nb)[:, None, :, None, None] * np.eye(S5_CH)[None, :, None, None, :] * np.ones((1, 1, 1, gq, 1)))
    e_sc = e_sc.reshape(nb * S5_CH, nb * gq * S5_CH)
    e_b = (np.eye(2)[:, None, :, None, None] * np.eye(S5_STATE)[None, :, None, None, :] * np.ones((1, 1, 1, gq, 1)))
    e_b = e_b.reshape(2 * S5_STATE, ncol)
    full = lambda shape: pl.BlockSpec(shape, lambda q, b: (0,) * len(shape))
    per_q = lambda r, c: pl.BlockSpec((None, r, c), lambda q, b: (q, 0, 0))
    n_seq = S5_SEQ_PER_STEP if batch % S5_SEQ_PER_STEP == 0 else 1
    return pl.pallas_call(
        functools.partial(_s5_kernel, rows=rows),
        grid=(nq, batch // n_seq),
        in_specs=[pl.BlockSpec((n_seq, seq, LANE), lambda q, b: (b, 0, q)),
                  per_q(LANE, nb * S5_CH), per_q(kdim, 2 * S5_STATE), per_q(ncol, nb * S5_CH),
                  full(e_sc.shape), full(e_b.shape), per_q(2, ncol // 2)],
        out_specs=pl.BlockSpec((n_seq, seq, LANE), lambda q, b: (b, 0, q)),
        out_shape=jax.ShapeDtypeStruct((batch, seq, BRANCH_W), F32),
        scratch_shapes=[pltpu.VMEM((kdim, nb * S5_CH), BF16),
                        pltpu.VMEM((kdim, kdim), BF16), pltpu.VMEM((kdim, ncol), BF16), pltpu.VMEM((ncol, kdim), BF16),
                        pltpu.VMEM((n_seq * rows, kdim), BF16), pltpu.VMEM((n_seq * rows, ncol), F32),
                        pltpu.VMEM((n_seq * rows, ncol), F32)],
        compiler_params=_cparams(("parallel", "arbitrary")),
        name="s5_scan",
    )(p3, k2, bc, cc, jnp.asarray(e_sc, BF16), jnp.asarray(e_b, BF16), lam16)


def _s5_operators(lam_re, lam_im, b_re, b_im, c_re, c_im, d_skip, log_dt):
    nb = S5_BLOCK
    gq = LANE // S5_CH
    nq = S5_GROUPS // gq
    lam = lax.complex(jnp.minimum(lam_re.astype(F32), S5_MAX_REAL), lam_im.astype(F32))
    step = jnp.exp(log_dt.astype(F32))[:, None]
    z = lam * step
    lam_bar = jnp.exp(z)
    b_bar = ((lam_bar - 1.0) / lam)[..., None] * lax.complex(b_re.astype(F32), b_im.astype(F32))
    c_mat = lax.complex(c_re.astype(F32), c_im.astype(F32))
    pw = jnp.exp(z[..., None] * jnp.arange(nb + 1, dtype=F32))
    cp = c_mat[:, None, :, :] * pw[..., :nb].transpose(0, 2, 1)[:, :, None, :]
    cp = jnp.concatenate([cp.real, -cp.imag], axis=-1).reshape(S5_GROUPS, nb * S5_CH, 2 * S5_STATE)
    bri = jnp.concatenate([b_bar.real, b_bar.imag], axis=1)
    kern = jnp.einsum('gnk,gki->gin', cp, bri, precision=HIGHEST)
    skip = (jnp.asarray(np.concatenate([np.eye(S5_CH), np.zeros((S5_CH, (nb - 1) * S5_CH))], axis=1), F32)[None]
            * d_skip.astype(F32).reshape(S5_GROUPS, S5_CH, 1))
    k2 = (kern + skip).reshape(nq, gq * S5_CH, nb * S5_CH)
    pw_rev = jnp.exp(z[..., None] * jnp.asarray(np.arange(nb - 1, -1, -1), F32))
    binc = pw_rev[:, :, :, None] * b_bar[:, :, None, :]
    binc = jnp.stack([binc.real, binc.imag], axis=0).reshape(2, nq, gq, S5_STATE, nb, S5_CH)
    bc = binc.transpose(1, 4, 2, 5, 0, 3).reshape(nq, nb * LANE, 2 * S5_STATE)
    cm = c_mat.transpose(0, 2, 1)[:, :, None, :] * pw[..., 1:][:, :, :, None]
    cm = jnp.stack([cm.real, -cm.imag], axis=0).reshape(2, nq, gq * S5_STATE, nb * S5_CH)
    cc = cm.transpose(1, 0, 2, 3).reshape(nq, 2 * gq * S5_STATE, nb * S5_CH)
    lam_n = pw[..., nb].reshape(nq, gq * S5_STATE)
    lam16 = jnp.stack([lam_n.real, lam_n.imag], axis=1)
    return k2, bc.astype(BF16), cc.astype(BF16), lam16


def _merge_kernel(x_ref, sc_ref, sh_ref, gm_ref, ys5_ref, yhg_ref, yret_ref, ym2_ref,
                  wglu_ref, wbr_ref, wg_ref, bg_ref, wout_ref, o_ref):
    x = x_ref[...]
    d = x.shape[1]
    h = _modulated_norm(x, sc_ref[...], sh_ref[...]).astype(BF16)
    y_s5 = jax.nn.gelu(ys5_ref[...])
    y_s5 = y_s5 * jax.nn.sigmoid(jnp.dot(y_s5.astype(BF16), wglu_ref[...], preferred_element_type=F32))
    acc = jnp.zeros(x.shape, F32)
    for n, y in enumerate((y_s5, yhg_ref[...], yret_ref[...], ym2_ref[...])):
        gate = jax.nn.sigmoid(jnp.dot(h, wg_ref[:, n * d:(n + 1) * d], preferred_element_type=F32)
                              + bg_ref[:, n * d:(n + 1) * d])
        acc = acc + gate * jnp.dot(y.astype(BF16), wbr_ref[n], preferred_element_type=F32)
    o_ref[...] = x + gm_ref[...] * jnp.dot(acc.astype(BF16), wout_ref[...], preferred_element_type=F32)


def _merge(x2, mod3, ys5, yhg, yret, ym2, w_glu, w_branch, w_gate, b_gate, w_out, layer, seq):
    t, d = x2.shape
    tm = TM_MERGE
    tpb = seq // tm
    const = lambda shape: pl.BlockSpec((None,) + shape, lambda i: (layer,) + (0,) * len(shape),
                                       pipeline_mode=pl.Buffered(1))
    modspec = lambda k: pl.BlockSpec((None, 1, d), lambda i: ((i // tpb) * 6 + k, 0, 0))
    yspec = pl.BlockSpec((tm, BRANCH_W), lambda i: (i, 0))
    return pl.pallas_call(
        _merge_kernel,
        grid=(t // tm,),
        in_specs=[pl.BlockSpec((tm, d), lambda i: (i, 0)), modspec(1), modspec(0), modspec(2),
                  yspec, yspec, yspec, yspec,
                  const((BRANCH_W, BRANCH_W)), const((4, BRANCH_W, d)), const((d, 4 * d)), const((1, 4 * d)),
                  const((d, d))],
        out_specs=pl.BlockSpec((tm, d), lambda i: (i, 0)),
        out_shape=jax.ShapeDtypeStruct((t, d), F32),
        compiler_params=_cparams(("parallel",)),
        name="merge",
    )(x2, mod3, mod3, mod3, ys5, yhg, yret, ym2, w_glu, w_branch, w_gate, b_gate, w_out)


def _router_kernel(x_ref, sc_ref, sh_ref, wr_ref, br_ref, tri_ref, h_ref, ids_ref, wts_ref, cnt_ref, carry):
    i = pl.program_id(0)

    @pl.when(i == 0)
    def _():
        carry[...] = jnp.zeros_like(carry)

    h = _modulated_norm(x_ref[...], sc_ref[...], sh_ref[...])
    tm, d = h.shape
    packed = _pack_bf16_pairs(h)
    for k in range(N_SLAB):
        h_ref[k] = packed[:, k * SLAB:(k + 1) * SLAB]
    h_hi = h.astype(BF16)
    h_lo = (h - h_hi.astype(F32)).astype(BF16)
    w_r = wr_ref[...]
    w_hi = w_r.astype(BF16)
    w_lo = (w_r - w_hi.astype(F32)).astype(BF16)
    logits = _dot_nt(w_hi, h_hi) + _dot_nt(w_hi, h_lo) + _dot_nt(w_lo, h_hi) + br_ref[:, 0:1]
    gl = [logits[g:g + 1, :] for g in range(MOE_GROUPS)]
    gmax = gl[0]
    gsel = jnp.zeros((1, tm), jnp.int32)
    for g in range(1, MOE_GROUPS):
        better = gl[g] > gmax
        gsel = jnp.where(better, g, gsel)
        gmax = jnp.where(better, gl[g], gmax)
    gden = gl[0] * 0.0
    for g in range(MOE_GROUPS):
        gden = gden + jnp.exp(gl[g] - gmax)
    g_w = 1.0 / gden
    el = []
    for e in range(MOE_EPG):
        v = logits[MOE_GROUPS + e:MOE_GROUPS + e + 1, :]
        for g in range(1, MOE_GROUPS):
            row = MOE_GROUPS + g * MOE_EPG + e
            v = jnp.where(gsel == g, logits[row:row + 1, :], v)
        el.append(v)
    v1 = el[0]
    i1 = jnp.zeros((1, tm), jnp.int32)
    for e in range(1, MOE_EPG):
        better = el[e] > v1
        i1 = jnp.where(better, e, i1)
        v1 = jnp.where(better, el[e], v1)
    v2 = jnp.full((1, tm), -jnp.inf, F32)
    i2 = jnp.zeros((1, tm), jnp.int32)
    for e in range(MOE_EPG):
        better = (el[e] > v2) & (i1 != e)
        i2 = jnp.where(better, e, i2)
        v2 = jnp.where(better, el[e], v2)
    ex = jnp.exp(v2 - v1)
    p1 = 1.0 / (1.0 + ex)
    e1 = gsel * MOE_EPG + i1
    e2 = gsel * MOE_EPG + i2
    erow = lax.broadcasted_iota(jnp.int32, (MOE_EXPERTS, tm), 0)
    oh1 = (erow == e1).astype(F32)
    oh2 = (erow == e2).astype(F32)
    both = oh1 + oh2
    prefix = jnp.dot(both.astype(BF16), tri_ref[...], preferred_element_type=F32) + carry[:, 0:1]
    rank1 = jnp.sum(oh1 * prefix, axis=0, keepdims=True).astype(jnp.int32)
    rank2 = jnp.sum(oh2 * prefix, axis=0, keepdims=True).astype(jnp.int32)
    carry[...] = carry[...] + jnp.sum(both, axis=1, keepdims=True)
    zi = jnp.zeros((1, tm), jnp.int32)
    ids_ref[...] = jnp.concatenate([e1, e2, rank1, rank2, zi, zi, zi, zi], axis=0)
    wrow = lax.broadcasted_iota(jnp.int32, (LANE, tm), 0)
    wts_ref[...] = jnp.where(wrow == 0, p1 * g_w, jnp.where(wrow == 1, ex * p1 * g_w, 0.0)).T
    cnt_ref[...] = carry[...]


def _router(x2, mod3, w_route, b_route, tri_excl, seq):
    t, d = x2.shape
    tm = TM_PROJ
    tpb = seq // tm
    nr = w_route.shape[0]
    const = lambda shape: pl.BlockSpec(shape, lambda i: (0,) * len(shape))
    modspec = lambda k: pl.BlockSpec((None, 1, d), lambda i: ((i // tpb) * 6 + k, 0, 0))
    return pl.pallas_call(
        _router_kernel,
        grid=(t // tm,),
        in_specs=[pl.BlockSpec((tm, d), lambda i: (i, 0)), modspec(4), modspec(3),
                  const((nr, d)), const((nr, LANE)), const((tm, tm))],
        out_specs=[pl.BlockSpec((N_SLAB, tm, SLAB), lambda i: (0, i, 0)),
                   pl.BlockSpec((8, tm), lambda i: (0, i)),
                   pl.BlockSpec((tm, LANE), lambda i: (i, 0)),
                   const((MOE_EXPERTS, LANE))],
        out_shape=[jax.ShapeDtypeStruct((N_SLAB, t, SLAB), jnp.uint32),
                   jax.ShapeDtypeStruct((8, t), jnp.int32),
                   jax.ShapeDtypeStruct((t, LANE), F32),
                   jax.ShapeDtypeStruct((MOE_EXPERTS, LANE), F32)],
        scratch_shapes=[pltpu.VMEM((MOE_EXPERTS, LANE), F32)],
        compiler_params=_cparams(("arbitrary",)),
        name="moe_router",
    )(x2, mod3, mod3, w_route, b_route, tri_excl)


def _sc_mesh():
    return plsc.VectorSubcoreMesh(core_axis_name="core", subcore_axis_name="subcore")


def _slab_rows(idx, n_rows):
    return (idx[None, :] + (jnp.arange(N_SLAB, dtype=jnp.int32) * n_rows)[:, None]).reshape(-1)


def _dispatch(slot1, slot2, h_slabs):
    n_slab, t, d = h_slabs.shape
    n_out = 2 * t
    xs = _scatter_rows(h_slabs.reshape(n_slab * t, d), _slab_rows(slot1, n_out), _slab_rows(slot2, n_out),
                       n_slab * n_out)
    return xs.reshape(n_slab, n_out, d)


def _scatter_rows(src, idx1, idx2, n_out):
    t, d = src.shape
    win = SC_WINDOW

    @pl.kernel(out_type=jax.ShapeDtypeStruct((n_out, d), src.dtype), mesh=_sc_mesh(), name="moe_dispatch_sc")
    def scatter_rows(x_hbm, i1_hbm, i2_hbm, o_hbm):
        def body(x_vmem, i1_vmem, i2_vmem):
            pltpu.sync_copy(x_vmem, o_hbm.at[i1_vmem.at[0]])
            pltpu.sync_copy(x_vmem, o_hbm.at[i2_vmem.at[0]])

        pltpu.emit_pipeline(
            body,
            grid=(t // win,),
            in_specs=[pl.BlockSpec((win, d), lambda i: (i, 0)),
                      pl.BlockSpec((1, win), lambda i: (0, i)),
                      pl.BlockSpec((1, win), lambda i: (0, i))],
            out_specs=[],
            core_axis_name=("core", "subcore"),
            dimension_semantics=(pltpu.PARALLEL,),
        )(x_hbm, i1_hbm, i2_hbm)

    return scatter_rows(src, idx1.reshape(1, t), idx2.reshape(1, t))


def _gather_rows(src, idx):
    m = idx.shape[0]
    d = src.shape[1]
    win = SC_WINDOW

    @pl.kernel(out_type=jax.ShapeDtypeStruct((m, d), src.dtype), mesh=_sc_mesh(), name="moe_gather_sc")
    def gather(x_hbm, i_hbm, o_hbm):
        def body(i_vmem, o_vmem):
            pltpu.sync_copy(x_hbm.at[i_vmem.at[0]], o_vmem)

        pltpu.emit_pipeline(
            body,
            grid=(m // win,),
            in_specs=[pl.BlockSpec((1, win), lambda i: (0, i))],
            out_specs=[pl.BlockSpec((win, d), lambda i: (i, 0))],
            core_axis_name=("core", "subcore"),
            dimension_semantics=(pltpu.PARALLEL,),
        )(i_hbm, o_hbm)

    return gather(src, idx.reshape(1, m))


def _expert_kernel(tile_ref, exp_ref, lo_ref, hi_ref, xs_ref, w1_ref, w3_ref, w2_ref, ys_ref, w1_scr, w3_scr, w2_scr):
    s = pl.program_id(0)
    prev = jnp.maximum(s - 1, 0)
    new_expert = (s == 0) | (exp_ref[s] != exp_ref[prev])
    new_tile = (s == 0) | (tile_ref[s] != tile_ref[prev])

    @pl.when(new_expert)
    def _():
        w1_scr[...] = w1_ref[...].astype(BF16)
        w3_scr[...] = w3_ref[...].astype(BF16)
        w2_scr[...] = w2_ref[...].astype(BF16)

    x = _unpack_bf16_pairs(jnp.concatenate([xs_ref[k] for k in range(N_SLAB)], axis=-1)).astype(BF16)
    a = jnp.dot(x, w1_scr[...], preferred_element_type=F32)
    b = jnp.dot(x, w3_scr[...], preferred_element_type=F32)
    act = _silu(a) * b
    y = _pack_bf16_pairs(jnp.dot(act.astype(BF16), w2_scr[...], preferred_element_type=F32))
    row = lax.broadcasted_iota(jnp.int32, (y.shape[0], SLAB), 0)
    mine = (row >= lo_ref[s]) & (row < hi_ref[s])

    @pl.when(new_tile)
    def _():
        for k in range(N_SLAB):
            ys_ref[k] = jnp.where(mine, y[:, k * SLAB:(k + 1) * SLAB], jnp.uint32(0))

    @pl.when(jnp.logical_not(new_tile))
    def _():
        for k in range(N_SLAB):
            ys_ref[k] = jnp.where(mine, y[:, k * SLAB:(k + 1) * SLAB], ys_ref[k])


def _experts(step_tile, step_expert, step_lo, step_hi, xs, w1, w3, w2, layer):
    n_slab, ns, slab = xs.shape
    d = w1.shape[1]
    ff = w1.shape[2]
    n_steps = step_tile.shape[0]
    base = layer * MOE_EXPERTS
    grid_spec = pltpu.PrefetchScalarGridSpec(
        num_scalar_prefetch=4,
        grid=(n_steps,),
        in_specs=[pl.BlockSpec((n_slab, TM_X, slab), lambda s, tl, ex, lo, hi: (0, tl[s], 0)),
                  pl.BlockSpec((None, d, ff), lambda s, tl, ex, lo, hi: (base + ex[s], 0, 0)),
                  pl.BlockSpec((None, d, ff), lambda s, tl, ex, lo, hi: (base + ex[s], 0, 0)),
                  pl.BlockSpec((None, ff, d), lambda s, tl, ex, lo, hi: (base + ex[s], 0, 0))],
        out_specs=pl.BlockSpec((n_slab, TM_X, slab), lambda s, tl, ex, lo, hi: (0, tl[s], 0)),
        scratch_shapes=[pltpu.VMEM((d, ff), BF16), pltpu.VMEM((d, ff), BF16), pltpu.VMEM((ff, d), BF16)],
    )
    return pl.pallas_call(
        _expert_kernel,
        grid_spec=grid_spec,
        out_shape=jax.ShapeDtypeStruct((n_slab, ns, slab), xs.dtype),
        compiler_params=_cparams(("arbitrary",)),
        name="moe_experts",
    )(step_tile, step_expert, step_lo, step_hi, xs, w1, w3, w2)


def _combine_kernel(x_ref, gate_ref, fw_ref, wcol_ref, y1_ref, y2_ref, o_ref, *, final):
    w_first = wcol_ref[:, 0:1]
    w_second = wcol_ref[:, 1:2]
    y_first = _unpack_bf16_pairs(jnp.concatenate([y1_ref[k] for k in range(N_SLAB)], axis=-1))
    y_second = _unpack_bf16_pairs(jnp.concatenate([y2_ref[k] for k in range(N_SLAB)], axis=-1))
    moe = w_first * y_first + w_second * y_second
    x = x_ref[...] + gate_ref[...] * moe
    if final:
        x = x * lax.rsqrt(jnp.mean(x * x, axis=-1, keepdims=True) + EPS) * fw_ref[...]
    o_ref[...] = x


def _combine(x2, mod3, final_w_row, wcol, gathered, seq, final):
    t, d = x2.shape
    tm = TM_COMB
    tpb = seq // tm
    nblk = t // tm
    yspec = lambda off: pl.BlockSpec((N_SLAB, tm, SLAB), lambda i: (0, i + off, 0))
    return pl.pallas_call(
        functools.partial(_combine_kernel, final=final),
        grid=(nblk,),
        in_specs=[pl.BlockSpec((tm, d), lambda i: (i, 0)),
                  pl.BlockSpec((None, 1, d), lambda i: ((i // tpb) * 6 + 5, 0, 0)),
                  pl.BlockSpec((1, d), lambda i: (0, 0)),
                  pl.BlockSpec((tm, LANE), lambda i: (i, 0)),
                  yspec(0), yspec(nblk)],
        out_specs=pl.BlockSpec((tm, d), lambda i: (i, 0)),
        out_shape=jax.ShapeDtypeStruct((t, d), F32),
        compiler_params=_cparams(("parallel",)),
        name="moe_combine",
    )(x2, mod3, final_w_row, wcol, gathered, gathered)


def _moe(x2, mod3, final_w_row, w_route, b_route, tri_excl, w1, w3, w2, layer, seq, final):
    t, d = x2.shape
    h3, ids, wcol, counts = _router(x2, mod3, w_route, b_route, tri_excl, seq)
    cnt = counts[:, 0].astype(jnp.int32)
    ends = jnp.cumsum(cnt)
    offs = ends - cnt
    experts = jnp.arange(MOE_EXPERTS, dtype=jnp.int32)
    pick = lambda table, idx: jnp.sum(jnp.where(idx[:, None] == experts[None, :], table[None, :], 0), axis=1)
    slot1 = pick(offs, ids[0]) + ids[2]
    slot2 = pick(offs, ids[1]) + ids[3]
    n_tiles = 2 * t // TM_X
    first_tile = offs // TM_X
    n_vis = jnp.where(cnt > 0, (ends - 1) // TM_X - first_tile + 1, 0)
    cum = jnp.cumsum(n_vis)
    step = jnp.arange(n_tiles + MOE_EXPERTS, dtype=jnp.int32)
    step_expert = jnp.minimum(jnp.sum(step[:, None] >= cum[None, :], axis=1), MOE_EXPERTS - 1).astype(jnp.int32)
    valid = step < cum[-1]
    step_tile = jnp.where(valid, pick(first_tile - (cum - n_vis), step_expert) + step, n_tiles - 1)
    step_lo = jnp.where(valid, jnp.clip(pick(offs, step_expert) - step_tile * TM_X, 0, TM_X), 0)
    step_hi = jnp.where(valid, jnp.clip(pick(ends, step_expert) - step_tile * TM_X, 0, TM_X), 0)
    xs = _dispatch(slot1, slot2, h3)
    ys = _experts(step_tile.astype(jnp.int32), step_expert, step_lo.astype(jnp.int32), step_hi.astype(jnp.int32),
                  xs, w1, w3, w2, layer)
    n_sorted = ys.shape[1]
    gathered = _gather_rows(ys.reshape(N_SLAB * n_sorted, SLAB), _slab_rows(jnp.concatenate([slot1, slot2]), n_sorted))
    gathered = gathered.reshape(N_SLAB, n_sorted, SLAB)
    return _combine(x2, mod3, final_w_row, wcol, gathered, seq, final)


def kernel(x, c, positions, ada_w, ada_b, w_in, s5_lam_re, s5_lam_im, s5_b_re, s5_b_im, s5_c_re, s5_c_im, s5_d, s5_log_dt, s5_w_glu, hg_lb_logits, hg_norm_w, m2_conv_w, m2_conv_b, m2_dt_bias, m2_a_log, m2_d, m2_norm_w, w_branch, w_gate, b_gate, w_out, moe_w_group, moe_b_group, moe_w_expert, moe_b_expert, moe_w1, moe_w3, moe_w2, final_norm_w):
    bsz, seq, d = x.shape
    t = bsz * seq
    depth = ada_w.shape[0]
    assert seq % TM_PROJ == 0 and seq % C_RET == 0 and seq % C_SSD == 0 and seq % C_HG == 0
    x2 = x.reshape(t, d).astype(F32)

    c_pad = jnp.zeros((8, d), F32).at[:bsz].set(c.astype(F32))
    mod_all = _ada_mod(c_pad, ada_w.astype(F32), ada_b.astype(F32))

    half = RET_DK // 2
    inv_freq = ROPE_BASE ** (-jnp.arange(half, dtype=F32) / half)
    invf_col = jnp.broadcast_to(inv_freq[:, None], (half, LANE))
    expand = np.tile(np.eye(half, dtype=np.float32), (1, 2 * RET_HEADS))
    sign = np.tile(np.concatenate([-np.ones(half), np.ones(half)]), RET_HEADS)[None, :].astype(np.float32)
    cos_t, sin_t = _rope_tables(positions.reshape(1, t).astype(jnp.int32), invf_col,
                                jnp.asarray(expand, BF16), jnp.asarray(expand * sign, BF16))
    cos3 = cos_t.reshape(bsz, seq, -1)
    sin3 = sin_t.reshape(bsz, seq, -1)

    lb_cum = jnp.cumsum(jax.nn.softmax(hg_lb_logits.astype(F32), axis=0), axis=0)
    hg_lb = lb_cum - lb_cum[:1]
    tri_ssd = jnp.asarray(np.tril(np.ones((C_SSD, C_SSD), np.float32)), BF16)
    tri_excl = jnp.asarray(np.triu(np.ones((TM_PROJ, TM_PROJ), np.float32), 1), BF16)
    final_w_row = final_norm_w.astype(F32)[None, :]
    w_pad = jnp.pad(w_in.astype(BF16), ((0, 0), (0, 0), (0, IN_W_PAD - IN_W)))
    w_glu_bf = s5_w_glu.astype(BF16)
    w_branch_bf = w_branch.astype(BF16)
    w_gate_bf = w_gate.astype(BF16)
    w_out_bf = w_out.astype(BF16)
    b_gate3 = b_gate.astype(F32).reshape(depth, 1, -1)
    moe_w1_all = moe_w1.astype(F32).reshape(depth * MOE_EXPERTS, d, MOE_FF)
    moe_w3_all = moe_w3.astype(F32).reshape(depth * MOE_EXPERTS, d, MOE_FF)
    moe_w2_all = moe_w2.astype(F32).reshape(depth * MOE_EXPERTS, MOE_FF, d)

    for layer in range(depth):
        mod3 = mod_all[layer, :bsz].reshape(bsz * 6, 1, d)
        p, u_s5 = _in_proj(x2, mod3, w_pad, layer, seq)
        p3 = p.reshape(bsz, seq, IN_W_PAD)

        ops = _s5_operators(s5_lam_re[layer], s5_lam_im[layer], s5_b_re[layer], s5_b_im[layer],
                            s5_c_re[layer], s5_c_im[layer], s5_d[layer], s5_log_dt[layer])
        y_s5 = _s5_scan(u_s5.reshape(bsz, seq, BRANCH_W), *ops).reshape(t, BRANCH_W)

        lb = hg_lb[layer][None, :]
        y_hg = _hgrn2(p3, jnp.log(lb), jnp.log1p(-lb), hg_norm_w[layer].astype(F32)[None, :]).reshape(t, BRANCH_W)

        y_ret = _retention(p3, cos3, sin3).reshape(t, BRANCH_W)

        pad8 = lambda v: jnp.zeros((1, LANE), F32).at[0, :M2_HEADS].set(v.astype(F32))
        y_m2 = _ssd(p3, tri_ssd, m2_conv_w[layer].astype(F32), m2_conv_b[layer].astype(F32)[None, :],
                    pad8(m2_dt_bias[layer]), pad8(m2_a_log[layer]),
                    jnp.repeat(m2_d[layer].astype(F32), M2_HEADDIM)[None, :],
                    m2_norm_w[layer].astype(F32)[None, :]).reshape(t, BRANCH_W)

        x2 = _merge(x2, mod3, y_s5, y_hg, y_ret, y_m2, w_glu_bf, w_branch_bf, w_gate_bf, b_gate3, w_out_bf,
                    layer, seq)

        nr = 40
        w_route = jnp.zeros((nr, d), F32).at[:MOE_GROUPS].set(moe_w_group[layer].astype(F32).T)
        w_route = w_route.at[MOE_GROUPS:MOE_GROUPS + MOE_EXPERTS].set(moe_w_expert[layer].astype(F32).T)
        b_route = jnp.zeros((nr, LANE), F32).at[:MOE_GROUPS, 0].set(moe_b_group[layer].astype(F32))
        b_route = b_route.at[MOE_GROUPS:MOE_GROUPS + MOE_EXPERTS, 0].set(moe_b_expert[layer].astype(F32))
        x2 = _moe(x2, mod3, final_w_row, w_route, b_route, tri_excl, moe_w1_all, moe_w3_all, moe_w2_all,
                  layer, seq, final=(layer == depth - 1))
    return x2.reshape(bsz, seq, d)
```

```python
import functools
import math

import numpy as np
import jax
import jax.numpy as jnp
from jax import lax
from jax.experimental import pallas as pl
from jax.experimental.pallas import tpu as pltpu
from jax.experimental.pallas import tpu_sc as plsc

F32 = jnp.float32
BF16 = jnp.bfloat16
HIGHEST = lax.Precision.HIGHEST

D_MODEL = 1024
DEPTH = 2
BRANCH_W = 512
EPS = 1e-6
S5_GROUPS = 32
S5_CH = 16
S5_STATE = 64
S5_MAX_REAL = -1e-4
S5_BLOCK = 16
S5_SEQ_PER_STEP = 2
HG_HEADS = 4
HG_DK = 128
RET_HEADS = 4
RET_DK = 64
RET_DV = 128
ROPE_BASE = 10000.0
M2_HEADS = 8
M2_HEADDIM = 64
M2_GROUPS = 2
M2_STATE = 128
M2_CONV = 4
MOE_GROUPS = 4
MOE_EPG = 8
MOE_EXPERTS = MOE_GROUPS * MOE_EPG
MOE_FF = 256

COL_S5, COL_HQ, COL_HF, COL_HI, COL_HG = 0, 512, 1024, 1536, 2048
COL_RQ, COL_RK, COL_RV, COL_RG = 2560, 2816, 3072, 3584
COL_MZ, COL_MXS, COL_MBC, COL_MDT = 4096, 4608, 5120, 5632
IN_W = 5640
IN_W_PAD = 5888

LANE = 128
VMEM_LIMIT = 56 * 1024 * 1024

TM_PROJ = 512
TN_PROJ = 1024
TM_INPROJ = 512
LOG2_E = 1.4426950408889634
C_RET = 256
C_SSD = 256
C_HG = 128
TM_X = 512
TM_COMB = 512
SC_WINDOW = 128
SLAB = 256
N_SLAB = D_MODEL // 2 // SLAB


def _cparams(sem):
    return pltpu.CompilerParams(dimension_semantics=sem, vmem_limit_bytes=VMEM_LIMIT)


def _silu(v):
    return v * jax.nn.sigmoid(v)


def _dot_nt(a, b, **kw):
    return lax.dot_general(a, b, (((1,), (1,)), ((), ())), preferred_element_type=F32, **kw)


def _dot_tn(a, b, **kw):
    return lax.dot_general(a, b, (((0,), (0,)), ((), ())), preferred_element_type=F32, **kw)


def _ada_kernel(c_ref, w_ref, b_ref, o_ref):
    cond = _silu(c_ref[...])
    o_ref[...] = jnp.dot(cond, w_ref[...], preferred_element_type=F32, precision=HIGHEST) + b_ref[...]


def _ada_mod(c_pad, ada_w, ada_b):
    depth, d, n = ada_w.shape
    tn = 1536
    return pl.pallas_call(
        _ada_kernel,
        grid=(depth, n // tn),
        in_specs=[pl.BlockSpec((8, d), lambda l, j: (0, 0)),
                  pl.BlockSpec((None, d, tn), lambda l, j: (l, 0, j)),
                  pl.BlockSpec((None, 1, tn), lambda l, j: (l, 0, j))],
        out_specs=pl.BlockSpec((None, 8, tn), lambda l, j: (l, 0, j)),
        out_shape=jax.ShapeDtypeStruct((depth, 8, n), F32),
        compiler_params=_cparams(("parallel", "parallel")),
        name="ada_mod",
    )(c_pad, ada_w, ada_b.reshape(depth, 1, n))


def _pack_bf16_pairs(x):
    n = x.shape[1] // 2
    lo = pltpu.bitcast(x[:, :n].astype(BF16).astype(F32), jnp.uint32) >> 16
    hi = pltpu.bitcast(x[:, n:].astype(BF16).astype(F32), jnp.uint32)
    return hi | lo


def _unpack_bf16_pairs(w):
    lo = pltpu.bitcast(w << 16, F32)
    hi = pltpu.bitcast(w & jnp.uint32(0xFFFF0000), F32)
    return jnp.concatenate([lo, hi], axis=-1)


def _modulated_norm(x, scale, shift):
    ms = jnp.mean(x * x, axis=-1, keepdims=True)
    return x * lax.rsqrt(ms + EPS) * (1.0 + scale) + shift


def _inproj_kernel(x_ref, sc_ref, sh_ref, w_ref, o_ref, u_ref):
    h = _modulated_norm(x_ref[...], sc_ref[...], sh_ref[...]).astype(BF16)
    n_total = o_ref.shape[1]
    for n0 in range(0, n_total, TN_PROJ):
        n1 = min(n0 + TN_PROJ, n_total)
        p = jnp.dot(h, w_ref[:, n0:n1], preferred_element_type=F32)
        o_ref[:, n0:n1] = p.astype(o_ref.dtype)
        if n0 == 0:
            u_ref[...] = p[:, COL_S5:COL_S5 + BRANCH_W]


def _in_proj(x2, mod3, w_pad, layer, seq):
    t, d = x2.shape
    tm = TM_INPROJ
    tpb = seq // tm
    assert COL_S5 + BRANCH_W <= TN_PROJ
    return pl.pallas_call(
        _inproj_kernel,
        grid=(t // tm,),
        in_specs=[pl.BlockSpec((tm, d), lambda i: (i, 0)),
                  pl.BlockSpec((None, 1, d), lambda i: ((i // tpb) * 6 + 1, 0, 0)),
                  pl.BlockSpec((None, 1, d), lambda i: ((i // tpb) * 6 + 0, 0, 0)),
                  pl.BlockSpec((None, d, IN_W_PAD), lambda i: (layer, 0, 0), pipeline_mode=pl.Buffered(1))],
        out_specs=[pl.BlockSpec((tm, IN_W_PAD), lambda i: (i, 0)),
                   pl.BlockSpec((tm, BRANCH_W), lambda i: (i, 0))],
        out_shape=[jax.ShapeDtypeStruct((t, IN_W_PAD), BF16), jax.ShapeDtypeStruct((t, BRANCH_W), F32)],
        compiler_params=_cparams(("parallel",)),
        name="in_proj",
    )(x2, mod3, mod3, w_pad)


def _rope_kernel(pos_ref, invf_ref, ecos_ref, esin_ref, cos_ref, sin_ref):
    ang = invf_ref[:, 0:1] * pos_ref[...].astype(F32)
    def spread(values, e_ref):
        hi = values.astype(BF16)
        rest = values - hi.astype(F32)
        mid = rest.astype(BF16)
        lo = (rest - mid.astype(F32)).astype(BF16)
        e = e_ref[...]
        return _dot_tn(hi, e) + _dot_tn(mid, e) + _dot_tn(lo, e)

    cos_ref[...] = spread(jnp.cos(ang), ecos_ref)
    sin_ref[...] = spread(jnp.sin(ang), esin_ref)


def _rope_tables(pos_row, invf_col, expand_cos, expand_sin):
    t = pos_row.shape[1]
    half, w = expand_cos.shape
    tm = 1024
    const = lambda shape: pl.BlockSpec(shape, lambda i: (0, 0))
    return pl.pallas_call(
        _rope_kernel,
        grid=(t // tm,),
        in_specs=[pl.BlockSpec((1, tm), lambda i: (0, i)), const((half, LANE)), const((half, w)), const((half, w))],
        out_specs=[pl.BlockSpec((tm, w), lambda i: (i, 0))] * 2,
        out_shape=[jax.ShapeDtypeStruct((t, w), F32)] * 2,
        compiler_params=_cparams(("parallel",)),
        name="rope_tables",
    )(pos_row, invf_col, expand_cos, expand_sin)


def _ret_kernel(q_ref, k_ref, v_ref, g_ref, cos_ref, sin_ref, o_ref, st_ref, dec_ref, *, chunk):
    @pl.when(pl.program_id(1) == 0)
    def _():
        st_ref[...] = jnp.zeros_like(st_ref)
        ti = lax.broadcasted_iota(jnp.int32, (chunk, chunk), 0)
        si = lax.broadcasted_iota(jnp.int32, (chunk, chunk), 1)
        lag = (ti - si).astype(F32)
        for h in range(RET_HEADS):
            log_gamma = math.log1p(-(2.0 ** (-5.0 - h)))
            dec_ref[h] = jnp.where(ti >= si, jnp.exp(jnp.minimum(lag * log_gamma, 0.0)), 0.0)

    cosf = cos_ref[...]
    sinf = sin_ref[...]
    width = RET_HEADS * RET_DK
    lane = lax.broadcasted_iota(jnp.int32, (chunk, width), 1)
    first_half = (lane % RET_DK) < (RET_DK // 2)

    def rope(t):
        partner = jnp.where(first_half, pltpu.roll(t, width - RET_DK // 2, 1), pltpu.roll(t, RET_DK // 2, 1))
        return t * cosf + partner * sinf

    q = rope(q_ref[...].astype(F32))
    k = rope(k_ref[...].astype(F32)) * (RET_DK ** -0.5)
    v = v_ref[...]
    g = g_ref[...].astype(F32)
    tcol = lax.broadcasted_iota(jnp.int32, (chunk, 1), 0).astype(F32)
    for h in range(RET_HEADS):
        log_gamma = math.log1p(-(2.0 ** (-5.0 - h)))
        qh = q[:, h * RET_DK:(h + 1) * RET_DK]
        kh = k[:, h * RET_DK:(h + 1) * RET_DK]
        vh = v[:, h * RET_DV:(h + 1) * RET_DV].astype(BF16)
        scores = _dot_nt(qh.astype(BF16), kh.astype(BF16)) * dec_ref[h]
        state = st_ref[h]
        q_in = qh * jnp.exp(log_gamma * (tcol + 1.0))
        o = (jnp.dot(scores.astype(BF16), vh, preferred_element_type=F32)
             + jnp.dot(q_in.astype(BF16), state.astype(BF16), preferred_element_type=F32))
        k_out = kh * jnp.exp(log_gamma * (chunk - 1.0 - tcol))
        st_ref[h] = math.exp(log_gamma * chunk) * state + _dot_tn(k_out.astype(BF16), vh)
        o = o * lax.rsqrt(jnp.mean(o * o, axis=-1, keepdims=True) + EPS)
        gh = g[:, h * RET_DV:(h + 1) * RET_DV]
        o_ref[:, h * RET_DV:(h + 1) * RET_DV] = (o * _silu(gh)).astype(o_ref.dtype)


def _retention(p3, cos3, sin3):
    b, seq, _ = p3.shape
    c = C_RET
    qk_w = RET_HEADS * RET_DK
    return pl.pallas_call(
        functools.partial(_ret_kernel, chunk=c),
        grid=(b, seq // c),
        in_specs=[pl.BlockSpec((None, c, qk_w), lambda i, j: (i, j, COL_RQ // qk_w)),
                  pl.BlockSpec((None, c, qk_w), lambda i, j: (i, j, COL_RK // qk_w)),
                  pl.BlockSpec((None, c, BRANCH_W), lambda i, j: (i, j, COL_RV // BRANCH_W)),
                  pl.BlockSpec((None, c, BRANCH_W), lambda i, j: (i, j, COL_RG // BRANCH_W)),
                  pl.BlockSpec((None, c, qk_w), lambda i, j: (i, j, 0)),
                  pl.BlockSpec((None, c, qk_w), lambda i, j: (i, j, 0))],
        out_specs=pl.BlockSpec((None, c, BRANCH_W), lambda i, j: (i, j, 0)),
        out_shape=jax.ShapeDtypeStruct((b, seq, BRANCH_W), BF16),
        scratch_shapes=[pltpu.VMEM((RET_HEADS, RET_DK, RET_DV), F32), pltpu.VMEM((RET_HEADS, c, c), F32)],
        compiler_params=_cparams(("parallel", "arbitrary")),
        name="retention",
    )(p3, p3, p3, p3, cos3, sin3)


def _ssd_kernel(z_ref, xs_ref, bc_ref, dt_ref, tri_ref, cw_ref, cb_ref, dtb_ref, alog_ref, dsk_ref, nw_ref,
                o_ref, xe_scr, st_ref, *, chunk):
    j = pl.program_id(1)
    width = 2 * BRANCH_W

    @pl.when(j == 0)
    def _():
        st_ref[...] = jnp.zeros_like(st_ref)
        xe_scr[0:8, :] = jnp.zeros((8, width), F32)

    @pl.when(j > 0)
    def _():
        xe_scr[0:8, :] = xe_scr[chunk:chunk + 8, :]

    xe_scr[8:, 0:BRANCH_W] = xs_ref[...].astype(F32)
    xe_scr[8:, BRANCH_W:] = bc_ref[...].astype(F32)
    conv = cb_ref[...] + cw_ref[M2_CONV - 1:M2_CONV, :] * xe_scr[8:, :]
    for tap in range(M2_CONV - 1):
        conv = conv + cw_ref[tap:tap + 1, :] * xe_scr[pl.ds(8 - (M2_CONV - 1) + tap, chunk), :]
    conv = _silu(conv)
    xs = conv[:, :BRANCH_W]
    bm = conv[:, BRANCH_W:BRANCH_W + M2_GROUPS * M2_STATE]
    cm = conv[:, BRANCH_W + M2_GROUPS * M2_STATE:]

    dt = jax.nn.softplus(dt_ref[...].astype(F32) + dtb_ref[...])
    da = dt * (-jnp.exp(alog_ref[...]))
    da_hi = da.astype(BF16)
    da_r = da - da_hi.astype(F32)
    da_mid = da_r.astype(BF16)
    da_lo = (da_r - da_mid.astype(F32)).astype(BF16)
    tri = tri_ref[...]
    a_cs = (jnp.dot(tri, da_hi, preferred_element_type=F32) + jnp.dot(tri, da_mid, preferred_element_type=F32)
            + jnp.dot(tri, da_lo, preferred_element_type=F32))
    a_cs = a_cs * LOG2_E
    a_cs_t = a_cs.T
    ti = lax.broadcasted_iota(jnp.int32, (chunk, chunk), 0)
    si = lax.broadcasted_iota(jnp.int32, (chunk, chunk), 1)
    causal = ti >= si
    hpg = M2_HEADS // M2_GROUPS
    ys = []
    for grp in range(M2_GROUPS):
        bm_g = bm[:, grp * M2_STATE:(grp + 1) * M2_STATE]
        cm_g = cm[:, grp * M2_STATE:(grp + 1) * M2_STATE]
        cb = _dot_nt(cm_g.astype(BF16), bm_g.astype(BF16))
        for hh in range(hpg):
            h = grp * hpg + hh
            col = a_cs[:, h:h + 1]
            row = a_cs_t[h:h + 1, :]
            lmat = jnp.where(causal, jnp.exp2(col - row), 0.0)
            xd = xs[:, h * M2_HEADDIM:(h + 1) * M2_HEADDIM] * dt[:, h:h + 1]
            state = st_ref[h]
            y = (jnp.dot((cb * lmat).astype(BF16), xd.astype(BF16), preferred_element_type=F32)
                 + jnp.dot((cm_g * jnp.exp2(col)).astype(BF16), state.astype(BF16), preferred_element_type=F32))
            a_last = a_cs[chunk - 1:chunk, h:h + 1]
            to_end = jnp.exp2(a_last - col)
            st_ref[h] = jnp.exp2(a_last) * state + _dot_tn(bm_g.astype(BF16), (xd * to_end).astype(BF16))
            ys.append(y)
    y = jnp.concatenate(ys, axis=-1) + dsk_ref[...] * xs
    y = y * _silu(z_ref[...].astype(F32))
    o_ref[...] = (y * lax.rsqrt(jnp.mean(y * y, axis=-1, keepdims=True) + EPS) * nw_ref[...]).astype(o_ref.dtype)


def _ssd(p3, tri, conv_w, conv_b, dt_bias_row, a_log_row, d_skip_row, norm_w_row):
    b, seq, _ = p3.shape
    c = C_SSD
    const = lambda shape: pl.BlockSpec(shape, lambda i, j: (0,) * len(shape))
    return pl.pallas_call(
        functools.partial(_ssd_kernel, chunk=c),
        grid=(b, seq // c),
        in_specs=[pl.BlockSpec((None, c, BRANCH_W), lambda i, j: (i, j, COL_MZ // BRANCH_W)),
                  pl.BlockSpec((None, c, BRANCH_W), lambda i, j: (i, j, COL_MXS // BRANCH_W)),
                  pl.BlockSpec((None, c, BRANCH_W), lambda i, j: (i, j, COL_MBC // BRANCH_W)),
                  pl.BlockSpec((None, c, LANE), lambda i, j: (i, j, COL_MDT // LANE)),
                  const((c, c)), const((M2_CONV, 2 * BRANCH_W)), const((1, 2 * BRANCH_W)),
                  const((1, LANE)), const((1, LANE)), const((1, BRANCH_W)), const((1, BRANCH_W))],
        out_specs=pl.BlockSpec((None, c, BRANCH_W), lambda i, j: (i, j, 0)),
        out_shape=jax.ShapeDtypeStruct((b, seq, BRANCH_W), BF16),
        scratch_shapes=[pltpu.VMEM((c + 8, 2 * BRANCH_W), F32),
                        pltpu.VMEM((M2_HEADS, M2_STATE, M2_HEADDIM), F32)],
        compiler_params=_cparams(("parallel", "arbitrary")),
        name="ssd",
    )(p3, p3, p3, p3, tri, conv_w, conv_b, dt_bias_row, a_log_row, d_skip_row, norm_w_row)


def _hg_tables(chunk):
    n_lev = int(math.log2(chunk))
    r = np.arange(chunk)[:, None]
    jj = np.arange(chunk)[None, :]
    tri = (jj <= r).astype(np.float32)
    x = r ^ jj
    levmap = np.where(r > jj, np.floor(np.log2(x + 0.5)), np.where(r == jj, -1, -2)).astype(np.int32)
    return tri, levmap, n_lev


def _hg_level_exponent(b, lev):
    rows, width = b.shape
    m = 1 << lev
    sub = 8
    if 2 * m >= sub:
        blocks = b.reshape(rows // (2 * m), 2 * m, width)
        mid = jnp.broadcast_to(blocks[:, m - 1:m, :], blocks.shape).reshape(rows, width)
    else:
        groups = b.reshape(rows // sub, sub, width)
        row_in_group = lax.broadcasted_iota(jnp.int32, groups.shape, 1)
        mid = None
        for start in range(0, sub, 2 * m):
            picked = jnp.broadcast_to(groups[:, start + m - 1:start + m, :], groups.shape)
            mid = picked if mid is None else jnp.where(row_in_group >= start, picked, mid)
        mid = mid.reshape(rows, width)
    return -jnp.abs(b - mid)


def _hg_kernel(q_ref, f_ref, i_ref, g_ref, sum_ref, lev_ref, llb_ref, l1m_ref, nw_ref, o_ref, st_ref,
               *, chunk, n_lev):
    @pl.when(pl.program_id(1) == 0)
    def _():
        st_ref[...] = jnp.zeros_like(st_ref)

    f = f_ref[...].astype(F32)
    log_sig = jnp.minimum(f, 0.0) - jnp.log1p(jnp.exp(-jnp.abs(f)))
    a = llb_ref[...]
    bb = l1m_ref[...] + log_sig
    log_f = jnp.maximum(a, bb) + jnp.log1p(jnp.exp(-jnp.abs(a - bb)))
    k_all = jnp.exp(l1m_ref[...]) * jax.nn.sigmoid(-f)
    q_all = _silu(q_ref[...].astype(F32))
    hi = log_f.astype(BF16)
    r1 = log_f - hi.astype(F32)
    mid = r1.astype(BF16)
    lo = (r1 - mid.astype(F32)).astype(BF16)
    tri = sum_ref[...]
    b_all = (jnp.dot(tri, hi, preferred_element_type=F32)
             + jnp.dot(tri, mid, preferred_element_type=F32)
             + jnp.dot(tri, lo, preferred_element_type=F32))
    b_all = b_all * LOG2_E
    to_end_all = b_all[chunk - 1:chunk, :] - b_all
    level_decay = [jnp.exp2(_hg_level_exponent(b_all, lev)) for lev in range(n_lev)]
    levmap = lev_ref[...]
    on_diag = levmap == -1
    on_level = [levmap == lev for lev in range(n_lev)]
    v_all = i_ref[...]
    g_all = g_ref[...].astype(F32)
    for h in range(HG_HEADS):
        sl = slice(h * HG_DK, (h + 1) * HG_DK)
        qh = q_all[:, sl]
        kh = k_all[:, sl]
        vh = v_all[:, sl].astype(BF16)
        b_h = b_all[:, sl]
        to_end = to_end_all[:, sl]
        amat = jnp.where(on_diag, _dot_nt(qh.astype(BF16), kh.astype(BF16)), 0.0)
        for lev in range(n_lev):
            e = level_decay[lev][:, sl]
            a_l = _dot_nt((qh * e).astype(BF16), (kh * e).astype(BF16))
            amat = jnp.where(on_level[lev], a_l, amat)
        state_t = st_ref[h]
        o = (jnp.dot(amat.astype(BF16), vh, preferred_element_type=F32)
             + _dot_nt((qh * jnp.exp2(b_h)).astype(BF16), state_t.astype(BF16)))
        k_end = kh * jnp.exp2(to_end)
        st_ref[h] = jnp.exp2(b_h[chunk - 1:chunk, :]) * state_t + _dot_tn(vh, k_end.astype(BF16))
        o = o * lax.rsqrt(jnp.mean(o * o, axis=-1, keepdims=True) + EPS) * nw_ref[...]
        o_ref[:, sl] = (o * _silu(g_all[:, sl])).astype(o_ref.dtype)


def _hgrn2(p3, log_lb_row, log1m_lb_row, norm_w_row):
    b, seq, _ = p3.shape
    c = C_HG
    tri, levmap, n_lev = _hg_tables(c)
    const = lambda shape: pl.BlockSpec(shape, lambda i, j: (0,) * len(shape))
    blk = lambda col: pl.BlockSpec((None, c, BRANCH_W), lambda i, j: (i, j, col // BRANCH_W))
    return pl.pallas_call(
        functools.partial(_hg_kernel, chunk=c, n_lev=n_lev),
        grid=(b, seq // c),
        in_specs=[blk(COL_HQ), blk(COL_HF), blk(COL_HI), blk(COL_HG),
                  const((c, c)), const((c, c)),
                  const((1, BRANCH_W)), const((1, BRANCH_W)), const((1, HG_DK))],
        out_specs=pl.BlockSpec((None, c, BRANCH_W), lambda i, j: (i, j, 0)),
        out_shape=jax.ShapeDtypeStruct((b, seq, BRANCH_W), BF16),
        scratch_shapes=[pltpu.VMEM((HG_HEADS, HG_DK, HG_DK), F32)],
        compiler_params=_cparams(("parallel", "arbitrary")),
        name="hgrn2",
    )(p3, p3, p3, p3, jnp.asarray(tri, BF16), jnp.asarray(levmap), log_lb_row, log1m_lb_row, norm_w_row)


def _expand_block_diag(comp_ref, e_ref, dst_ref, row_div, lane_div):
    gq = LANE // S5_CH
    rows, ncols = dst_ref.shape
    step = 512
    comp = comp_ref[...]
    row_grp = (lax.broadcasted_iota(jnp.int32, (rows, step), 0) // row_div) % gq
    for c0 in range(0, ncols, step):
        lane_grp = ((lax.broadcasted_iota(jnp.int32, (rows, step), 1) + c0) // lane_div) % gq
        full = jnp.dot(comp, e_ref[:, c0:c0 + step], preferred_element_type=F32)
        dst_ref[:, c0:c0 + step] = jnp.where(row_grp == lane_grp, full, 0.0).astype(dst_ref.dtype)


def _s5_kernel(u_ref, k2_ref, bc_ref, cc_ref, esc_ref, eb_ref, lam_ref, o_ref, tc_scr, tq_ref, bq_ref, cq_ref,
               x_scr, w_scr, s_scr, *, rows):
    nb = S5_BLOCK

    @pl.when(pl.program_id(1) == 0)
    def _():
        k2 = k2_ref[...]
        lane = lax.broadcasted_iota(jnp.int32, k2.shape, 1)
        for t in range(nb):
            shifted = k2 if t == 0 else jnp.where(lane >= t * S5_CH, pltpu.roll(k2, t * S5_CH, 1), 0.0)
            tc_scr[t * LANE:(t + 1) * LANE, :] = shifted.astype(tc_scr.dtype)
        _expand_block_diag(tc_scr, esc_ref, tq_ref, S5_CH, S5_CH)
        _expand_block_diag(bc_ref, eb_ref, bq_ref, S5_CH, S5_STATE)
        _expand_block_diag(cc_ref, esc_ref, cq_ref, S5_STATE, S5_CH)

    n_seq = u_ref.shape[0]
    for b in range(n_seq):
        for t in range(nb):
            x_scr[b * rows:(b + 1) * rows, t * LANE:(t + 1) * LANE] = (
                u_ref[b, pl.ds(t, rows, stride=nb), :].astype(x_scr.dtype))
    x = x_scr[...]
    half = w_scr.shape[1] // 2
    w_scr[...] = jnp.dot(x, bq_ref[...], preferred_element_type=F32)
    lam_re = lam_ref[0:1, :]
    lam_im = lam_ref[1:2, :]

    def body(j, carry):
        out = []
        for b in range(n_seq):
            s_re, s_im = carry[2 * b], carry[2 * b + 1]
            r = b * rows + j
            s_scr[pl.ds(r, 1), 0:half] = s_re
            s_scr[pl.ds(r, 1), half:] = s_im
            w_re = w_scr[pl.ds(r, 1), 0:half]
            w_im = w_scr[pl.ds(r, 1), half:]
            out += [lam_re * s_re - lam_im * s_im + w_re, lam_re * s_im + lam_im * s_re + w_im]
        return tuple(out)

    zero = jnp.zeros((1, half), F32)
    lax.fori_loop(0, rows, body, (zero,) * (2 * n_seq))
    y = (jnp.dot(x, tq_ref[...], preferred_element_type=F32)
         + jnp.dot(s_scr[...].astype(BF16), cq_ref[...], preferred_element_type=F32))
    for b in range(n_seq):
        for t in range(nb):
            o_ref[b, pl.ds(t, rows, stride=nb), :] = y[b * rows:(b + 1) * rows, t * LANE:(t + 1) * LANE]


def _s5_scan(p3, k2, bc, cc, lam16):
    batch, seq, _ = p3.shape
    nb = S5_BLOCK
    nq = BRANCH_W // LANE
    rows = seq // nb
    kdim = nb * LANE
    gq = LANE // S5_CH
    ncol = 2 * gq * S5_STATE
    e_sc = (np.eye(nb)[:, None, :, None, None] * np.eye(S5_CH)[None, :, None, None, :] * np.ones((1, 1, 1, gq, 1)))
    e_sc = e_sc.reshape(nb * S5_CH, nb * gq * S5_CH)
    e_b = (np.eye(2)[:, None, :, None, None] * np.eye(S5_STATE)[None, :, None, None, :] * np.ones((1, 1, 1, gq, 1)))
    e_b = e_b.reshape(2 * S5_STATE, ncol)
    full = lambda shape: pl.BlockSpec(shape, lambda q, b: (0,) * len(shape))
    per_q = lambda r, c: pl.BlockSpec((None, r, c), lambda q, b: (q, 0, 0))
    n_seq = S5_SEQ_PER_STEP if batch % S5_SEQ_PER_STEP == 0 else 1
    return pl.pallas_call(
        functools.partial(_s5_kernel, rows=rows),
        grid=(nq, batch // n_seq),
        in_specs=[pl.BlockSpec((n_seq, seq, LANE), lambda q, b: (b, 0, q)),
                  per_q(LANE, nb * S5_CH), per_q(kdim, 2 * S5_STATE), per_q(ncol, nb * S5_CH),
                  full(e_sc.shape), full(e_b.shape), per_q(2, ncol // 2)],
        out_specs=pl.BlockSpec((n_seq, seq, LANE), lambda q, b: (b, 0, q)),
        out_shape=jax.ShapeDtypeStruct((batch, seq, BRANCH_W), F32),
        scratch_shapes=[pltpu.VMEM((kdim, nb * S5_CH), BF16),
                        pltpu.VMEM((kdim, kdim), BF16), pltpu.VMEM((kdim, ncol), BF16), pltpu.VMEM((ncol, kdim), BF16),
                        pltpu.VMEM((n_seq * rows, kdim), BF16), pltpu.VMEM((n_seq * rows, ncol), F32),
                        pltpu.VMEM((n_seq * rows, ncol), F32)],
        compiler_params=_cparams(("parallel", "arbitrary")),
        name="s5_scan",
    )(p3, k2, bc, cc, jnp.asarray(e_sc, BF16), jnp.asarray(e_b, BF16), lam16)


def _s5_operators(lam_re, lam_im, b_re, b_im, c_re, c_im, d_skip, log_dt):
    nb = S5_BLOCK
    gq = LANE // S5_CH
    nq = S5_GROUPS // gq
    lam = lax.complex(jnp.minimum(lam_re.astype(F32), S5_MAX_REAL), lam_im.astype(F32))
    step = jnp.exp(log_dt.astype(F32))[:, None]
    z = lam * step
    lam_bar = jnp.exp(z)
    b_bar = ((lam_bar - 1.0) / lam)[..., None] * lax.complex(b_re.astype(F32), b_im.astype(F32))
    c_mat = lax.complex(c_re.astype(F32), c_im.astype(F32))
    pw = jnp.exp(z[..., None] * jnp.arange(nb + 1, dtype=F32))
    cp = c_mat[:, None, :, :] * pw[..., :nb].transpose(0, 2, 1)[:, :, None, :]
    cp = jnp.concatenate([cp.real, -cp.imag], axis=-1).reshape(S5_GROUPS, nb * S5_CH, 2 * S5_STATE)
    bri = jnp.concatenate([b_bar.real, b_bar.imag], axis=1)
    kern = jnp.einsum('gnk,gki->gin', cp, bri, precision=HIGHEST)
    skip = (jnp.asarray(np.concatenate([np.eye(S5_CH), np.zeros((S5_CH, (nb - 1) * S5_CH))], axis=1), F32)[None]
            * d_skip.astype(F32).reshape(S5_GROUPS, S5_CH, 1))
    k2 = (kern + skip).reshape(nq, gq * S5_CH, nb * S5_CH)
    pw_rev = jnp.exp(z[..., None] * jnp.asarray(np.arange(nb - 1, -1, -1), F32))
    binc = pw_rev[:, :, :, None] * b_bar[:, :, None, :]
    binc = jnp.stack([binc.real, binc.imag], axis=0).reshape(2, nq, gq, S5_STATE, nb, S5_CH)
    bc = binc.transpose(1, 4, 2, 5, 0, 3).reshape(nq, nb * LANE, 2 * S5_STATE)
    cm = c_mat.transpose(0, 2, 1)[:, :, None, :] * pw[..., 1:][:, :, :, None]
    cm = jnp.stack([cm.real, -cm.imag], axis=0).reshape(2, nq, gq * S5_STATE, nb * S5_CH)
    cc = cm.transpose(1, 0, 2, 3).reshape(nq, 2 * gq * S5_STATE, nb * S5_CH)
    lam_n = pw[..., nb].reshape(nq, gq * S5_STATE)
    lam16 = jnp.stack([lam_n.real, lam_n.imag], axis=1)
    return k2, bc.astype(BF16), cc.astype(BF16), lam16


def _merge_kernel(x_ref, sc_ref, sh_ref, gm_ref, ys5_ref, yhg_ref, yret_ref, ym2_ref,
                  wglu_ref, wbr_ref, wg_ref, bg_ref, wout_ref,
                  scf_ref, shf_ref, wr_ref, br_ref, tri_ref,
                  o_ref, h_ref, ids_ref, wts_ref, cnt_ref, carry):
    x = x_ref[...]
    d = x.shape[1]
    h = _modulated_norm(x, sc_ref[...], sh_ref[...]).astype(BF16)
    y_s5 = jax.nn.gelu(ys5_ref[...])
    y_s5 = y_s5 * jax.nn.sigmoid(jnp.dot(y_s5.astype(BF16), wglu_ref[...], preferred_element_type=F32))
    acc = jnp.zeros(x.shape, F32)
    for n, y in enumerate((y_s5, yhg_ref[...], yret_ref[...], ym2_ref[...])):
        gate = jax.nn.sigmoid(jnp.dot(h, wg_ref[:, n * d:(n + 1) * d], preferred_element_type=F32)
                              + bg_ref[:, n * d:(n + 1) * d])
        acc = acc + gate * jnp.dot(y.astype(BF16), wbr_ref[n], preferred_element_type=F32)
    x_mix = x + gm_ref[...] * jnp.dot(acc.astype(BF16), wout_ref[...], preferred_element_type=F32)
    o_ref[...] = x_mix
    _route_tile(x_mix, scf_ref, shf_ref, wr_ref, br_ref, tri_ref, h_ref, ids_ref, wts_ref, cnt_ref, carry)


def _merge(x2, mod3, ys5, yhg, yret, ym2, w_glu, w_branch, w_gate, b_gate, w_out, w_route, b_route, tri_excl,
           layer, seq):
    t, d = x2.shape
    tm = TM_PROJ
    tpb = seq // tm
    nr = w_route.shape[0]
    whole = lambda shape: pl.BlockSpec(shape, lambda i: (0,) * len(shape), pipeline_mode=pl.Buffered(1))
    const = lambda shape: pl.BlockSpec((None,) + shape, lambda i: (layer,) + (0,) * len(shape),
                                       pipeline_mode=pl.Buffered(1))
    modspec = lambda k: pl.BlockSpec((None, 1, d), lambda i: ((i // tpb) * 6 + k, 0, 0))
    yspec = pl.BlockSpec((tm, BRANCH_W), lambda i: (i, 0))
    return pl.pallas_call(
        _merge_kernel,
        grid=(t // tm,),
        in_specs=[pl.BlockSpec((tm, d), lambda i: (i, 0)), modspec(1), modspec(0), modspec(2),
                  yspec, yspec, yspec, yspec,
                  const((BRANCH_W, BRANCH_W)), const((4, BRANCH_W, d)), const((d, 4 * d)), const((1, 4 * d)),
                  const((d, d)),
                  modspec(4), modspec(3), whole((nr, d)), whole((nr, LANE)), whole((tm, tm))],
        out_specs=[pl.BlockSpec((tm, d), lambda i: (i, 0)),
                   pl.BlockSpec((N_SLAB, tm, SLAB), lambda i: (0, i, 0)),
                   pl.BlockSpec((8, tm), lambda i: (0, i)),
                   pl.BlockSpec((tm, LANE), lambda i: (i, 0)),
                   pl.BlockSpec((MOE_EXPERTS, LANE), lambda i: (0, 0))],
        out_shape=[jax.ShapeDtypeStruct((t, d), F32),
                   jax.ShapeDtypeStruct((N_SLAB, t, SLAB), jnp.uint32),
                   jax.ShapeDtypeStruct((8, t), jnp.int32),
                   jax.ShapeDtypeStruct((t, LANE), F32),
                   jax.ShapeDtypeStruct((MOE_EXPERTS, LANE), F32)],
        scratch_shapes=[pltpu.VMEM((MOE_EXPERTS, LANE), F32)],
        compiler_params=_cparams(("arbitrary",)),
        name="merge_route",
    )(x2, mod3, mod3, mod3, ys5, yhg, yret, ym2, w_glu, w_branch, w_gate, b_gate, w_out,
      mod3, mod3, w_route, b_route, tri_excl)


def _route_tile(x, sc_ref, sh_ref, wr_ref, br_ref, tri_ref, h_ref, ids_ref, wts_ref, cnt_ref, carry):
    i = pl.program_id(0)

    @pl.when(i == 0)
    def _():
        carry[...] = jnp.zeros_like(carry)

    h = _modulated_norm(x, sc_ref[...], sh_ref[...])
    tm, d = h.shape
    packed = _pack_bf16_pairs(h)
    for k in range(N_SLAB):
        h_ref[k] = packed[:, k * SLAB:(k + 1) * SLAB]
    h_hi = h.astype(BF16)
    h_lo = (h - h_hi.astype(F32)).astype(BF16)
    w_r = wr_ref[...]
    w_hi = w_r.astype(BF16)
    w_lo = (w_r - w_hi.astype(F32)).astype(BF16)
    logits = _dot_nt(w_hi, h_hi) + _dot_nt(w_hi, h_lo) + _dot_nt(w_lo, h_hi) + br_ref[:, 0:1]
    gl = [logits[g:g + 1, :] for g in range(MOE_GROUPS)]
    gmax = gl[0]
    gsel = jnp.zeros((1, tm), jnp.int32)
    for g in range(1, MOE_GROUPS):
        better = gl[g] > gmax
        gsel = jnp.where(better, g, gsel)
        gmax = jnp.where(better, gl[g], gmax)
    gden = gl[0] * 0.0
    for g in range(MOE_GROUPS):
        gden = gden + jnp.exp(gl[g] - gmax)
    g_w = 1.0 / gden
    el = []
    for e in range(MOE_EPG):
        v = logits[MOE_GROUPS + e:MOE_GROUPS + e + 1, :]
        for g in range(1, MOE_GROUPS):
            row = MOE_GROUPS + g * MOE_EPG + e
            v = jnp.where(gsel == g, logits[row:row + 1, :], v)
        el.append(v)
    v1 = el[0]
    i1 = jnp.zeros((1, tm), jnp.int32)
    for e in range(1, MOE_EPG):
        better = el[e] > v1
        i1 = jnp.where(better, e, i1)
        v1 = jnp.where(better, el[e], v1)
    v2 = jnp.full((1, tm), -jnp.inf, F32)
    i2 = jnp.zeros((1, tm), jnp.int32)
    for e in range(MOE_EPG):
        better = (el[e] > v2) & (i1 != e)
        i2 = jnp.where(better, e, i2)
        v2 = jnp.where(better, el[e], v2)
    ex = jnp.exp(v2 - v1)
    p1 = 1.0 / (1.0 + ex)
    e1 = gsel * MOE_EPG + i1
    e2 = gsel * MOE_EPG + i2
    erow = lax.broadcasted_iota(jnp.int32, (MOE_EXPERTS, tm), 0)
    oh1 = (erow == e1).astype(F32)
    oh2 = (erow == e2).astype(F32)
    both = oh1 + oh2
    prefix = jnp.dot(both.astype(BF16), tri_ref[...], preferred_element_type=F32) + carry[:, 0:1]
    rank1 = jnp.sum(oh1 * prefix, axis=0, keepdims=True).astype(jnp.int32)
    rank2 = jnp.sum(oh2 * prefix, axis=0, keepdims=True).astype(jnp.int32)
    carry[...] = carry[...] + jnp.sum(both, axis=1, keepdims=True)
    zi = jnp.zeros((1, tm), jnp.int32)
    ids_ref[...] = jnp.concatenate([e1, e2, rank1, rank2, zi, zi, zi, zi], axis=0)
    wrow = lax.broadcasted_iota(jnp.int32, (LANE, tm), 0)
    wts_ref[...] = jnp.where(wrow == 0, p1 * g_w, jnp.where(wrow == 1, ex * p1 * g_w, 0.0)).T
    cnt_ref[...] = carry[...]


def _sc_mesh():
    return plsc.VectorSubcoreMesh(core_axis_name="core", subcore_axis_name="subcore")


def _slab_rows(idx, n_rows):
    return (idx[None, :] + (jnp.arange(N_SLAB, dtype=jnp.int32) * n_rows)[:, None]).reshape(-1)


def _dispatch(slot1, slot2, h_slabs):
    n_slab, t, d = h_slabs.shape
    n_out = 2 * t
    xs = _scatter_rows(h_slabs.reshape(n_slab * t, d), _slab_rows(slot1, n_out), _slab_rows(slot2, n_out),
                       n_slab * n_out)
    return xs.reshape(n_slab, n_out, d)


def _scatter_rows(src, idx1, idx2, n_out):
    t, d = src.shape
    win = SC_WINDOW

    @pl.kernel(out_type=jax.ShapeDtypeStruct((n_out, d), src.dtype), mesh=_sc_mesh(), name="moe_dispatch_sc")
    def scatter_rows(x_hbm, i1_hbm, i2_hbm, o_hbm):
        def body(x_vmem, i1_vmem, i2_vmem):
            pltpu.sync_copy(x_vmem, o_hbm.at[i1_vmem.at[0]])
            pltpu.sync_copy(x_vmem, o_hbm.at[i2_vmem.at[0]])

        pltpu.emit_pipeline(
            body,
            grid=(t // win,),
            in_specs=[pl.BlockSpec((win, d), lambda i: (i, 0)),
                      pl.BlockSpec((1, win), lambda i: (0, i)),
                      pl.BlockSpec((1, win), lambda i: (0, i))],
            out_specs=[],
            core_axis_name=("core", "subcore"),
            dimension_semantics=(pltpu.PARALLEL,),
        )(x_hbm, i1_hbm, i2_hbm)

    return scatter_rows(src, idx1.reshape(1, t), idx2.reshape(1, t))


def _gather_rows(src, idx):
    m = idx.shape[0]
    d = src.shape[1]
    win = SC_WINDOW

    @pl.kernel(out_type=jax.ShapeDtypeStruct((m, d), src.dtype), mesh=_sc_mesh(), name="moe_gather_sc")
    def gather(x_hbm, i_hbm, o_hbm):
        def body(i_vmem, o_vmem):
            pltpu.sync_copy(x_hbm.at[i_vmem.at[0]], o_vmem)

        pltpu.emit_pipeline(
            body,
            grid=(m // win,),
            in_specs=[pl.BlockSpec((1, win), lambda i: (0, i))],
            out_specs=[pl.BlockSpec((win, d), lambda i: (i, 0))],
            core_axis_name=("core", "subcore"),
            dimension_semantics=(pltpu.PARALLEL,),
        )(i_hbm, o_hbm)

    return gather(src, idx.reshape(1, m))


def _expert_kernel(tile_ref, exp_ref, lo_ref, hi_ref, xs_ref, w1_ref, w3_ref, w2_ref, ys_ref, w1_scr, w3_scr, w2_scr):
    s = pl.program_id(0)
    prev = jnp.maximum(s - 1, 0)
    new_expert = (s == 0) | (exp_ref[s] != exp_ref[prev])
    new_tile = (s == 0) | (tile_ref[s] != tile_ref[prev])

    @pl.when(new_expert)
    def _():
        w1_scr[...] = w1_ref[...].astype(BF16)
        w3_scr[...] = w3_ref[...].astype(BF16)
        w2_scr[...] = w2_ref[...].astype(BF16)

    x = _unpack_bf16_pairs(jnp.concatenate([xs_ref[k] for k in range(N_SLAB)], axis=-1)).astype(BF16)
    a = jnp.dot(x, w1_scr[...], preferred_element_type=F32)
    b = jnp.dot(x, w3_scr[...], preferred_element_type=F32)
    act = _silu(a) * b
    y = _pack_bf16_pairs(jnp.dot(act.astype(BF16), w2_scr[...], preferred_element_type=F32))
    row = lax.broadcasted_iota(jnp.int32, (y.shape[0], SLAB), 0)
    mine = (row >= lo_ref[s]) & (row < hi_ref[s])

    @pl.when(new_tile)
    def _():
        for k in range(N_SLAB):
            ys_ref[k] = jnp.where(mine, y[:, k * SLAB:(k + 1) * SLAB], jnp.uint32(0))

    @pl.when(jnp.logical_not(new_tile))
    def _():
        for k in range(N_SLAB):
            ys_ref[k] = jnp.where(mine, y[:, k * SLAB:(k + 1) * SLAB], ys_ref[k])


def _experts(step_tile, step_expert, step_lo, step_hi, xs, w1, w3, w2, layer):
    n_slab, ns, slab = xs.shape
    d = w1.shape[1]
    ff = w1.shape[2]
    n_steps = step_tile.shape[0]
    base = layer * MOE_EXPERTS
    grid_spec = pltpu.PrefetchScalarGridSpec(
        num_scalar_prefetch=4,
        grid=(n_steps,),
        in_specs=[pl.BlockSpec((n_slab, TM_X, slab), lambda s, tl, ex, lo, hi: (0, tl[s], 0)),
                  pl.BlockSpec((None, d, ff), lambda s, tl, ex, lo, hi: (base + ex[s], 0, 0)),
                  pl.BlockSpec((None, d, ff), lambda s, tl, ex, lo, hi: (base + ex[s], 0, 0)),
                  pl.BlockSpec((None, ff, d), lambda s, tl, ex, lo, hi: (base + ex[s], 0, 0))],
        out_specs=pl.BlockSpec((n_slab, TM_X, slab), lambda s, tl, ex, lo, hi: (0, tl[s], 0)),
        scratch_shapes=[pltpu.VMEM((d, ff), BF16), pltpu.VMEM((d, ff), BF16), pltpu.VMEM((ff, d), BF16)],
    )
    return pl.pallas_call(
        _expert_kernel,
        grid_spec=grid_spec,
        out_shape=jax.ShapeDtypeStruct((n_slab, ns, slab), xs.dtype),
        compiler_params=_cparams(("arbitrary",)),
        name="moe_experts",
    )(step_tile, step_expert, step_lo, step_hi, xs, w1, w3, w2)


def _combine_kernel(x_ref, gate_ref, fw_ref, wcol_ref, y1_ref, y2_ref, o_ref, *, final):
    w_first = wcol_ref[:, 0:1]
    w_second = wcol_ref[:, 1:2]
    y_first = _unpack_bf16_pairs(jnp.concatenate([y1_ref[k] for k in range(N_SLAB)], axis=-1))
    y_second = _unpack_bf16_pairs(jnp.concatenate([y2_ref[k] for k in range(N_SLAB)], axis=-1))
    moe = w_first * y_first + w_second * y_second
    x = x_ref[...] + gate_ref[...] * moe
    if final:
        x = x * lax.rsqrt(jnp.mean(x * x, axis=-1, keepdims=True) + EPS) * fw_ref[...]
    o_ref[...] = x


def _combine(x2, mod3, final_w_row, wcol, gathered, seq, final):
    t, d = x2.shape
    tm = TM_COMB
    tpb = seq // tm
    nblk = t // tm
    yspec = lambda off: pl.BlockSpec((N_SLAB, tm, SLAB), lambda i: (0, i + off, 0))
    return pl.pallas_call(
        functools.partial(_combine_kernel, final=final),
        grid=(nblk,),
        in_specs=[pl.BlockSpec((tm, d), lambda i: (i, 0)),
                  pl.BlockSpec((None, 1, d), lambda i: ((i // tpb) * 6 + 5, 0, 0)),
                  pl.BlockSpec((1, d), lambda i: (0, 0)),
                  pl.BlockSpec((tm, LANE), lambda i: (i, 0)),
                  yspec(0), yspec(nblk)],
        out_specs=pl.BlockSpec((tm, d), lambda i: (i, 0)),
        out_shape=jax.ShapeDtypeStruct((t, d), F32),
        compiler_params=_cparams(("parallel",)),
        name="moe_combine",
    )(x2, mod3, final_w_row, wcol, gathered, gathered)


def _moe(x2, h3, ids, wcol, counts, mod3, final_w_row, w1, w3, w2, layer, seq, final):
    t, d = x2.shape
    cnt = counts[:, 0].astype(jnp.int32)
    ends = jnp.cumsum(cnt)
    offs = ends - cnt
    experts = jnp.arange(MOE_EXPERTS, dtype=jnp.int32)
    pick = lambda table, idx: jnp.sum(jnp.where(idx[:, None] == experts[None, :], table[None, :], 0), axis=1)
    slot1 = pick(offs, ids[0]) + ids[2]
    slot2 = pick(offs, ids[1]) + ids[3]
    n_tiles = 2 * t // TM_X
    first_tile = offs // TM_X
    n_vis = jnp.where(cnt > 0, (ends - 1) // TM_X - first_tile + 1, 0)
    cum = jnp.cumsum(n_vis)
    step = jnp.arange(n_tiles + MOE_EXPERTS, dtype=jnp.int32)
    step_expert = jnp.minimum(jnp.sum(step[:, None] >= cum[None, :], axis=1), MOE_EXPERTS - 1).astype(jnp.int32)
    valid = step < cum[-1]
    step_tile = jnp.where(valid, pick(first_tile - (cum - n_vis), step_expert) + step, n_tiles - 1)
    step_lo = jnp.where(valid, jnp.clip(pick(offs, step_expert) - step_tile * TM_X, 0, TM_X), 0)
    step_hi = jnp.where(valid, jnp.clip(pick(ends, step_expert) - step_tile * TM_X, 0, TM_X), 0)
    xs = _dispatch(slot1, slot2, h3)
    ys = _experts(step_tile.astype(jnp.int32), step_expert, step_lo.astype(jnp.int32), step_hi.astype(jnp.int32),
                  xs, w1, w3, w2, layer)
    n_sorted = ys.shape[1]
    gathered = _gather_rows(ys.reshape(N_SLAB * n_sorted, SLAB), _slab_rows(jnp.concatenate([slot1, slot2]), n_sorted))
    gathered = gathered.reshape(N_SLAB, n_sorted, SLAB)
    return _combine(x2, mod3, final_w_row, wcol, gathered, seq, final)


def kernel(x, c, positions, ada_w, ada_b, w_in, s5_lam_re, s5_lam_im, s5_b_re, s5_b_im, s5_c_re, s5_c_im, s5_d, s5_log_dt, s5_w_glu, hg_lb_logits, hg_norm_w, m2_conv_w, m2_conv_b, m2_dt_bias, m2_a_log, m2_d, m2_norm_w, w_branch, w_gate, b_gate, w_out, moe_w_group, moe_b_group, moe_w_expert, moe_b_expert, moe_w1, moe_w3, moe_w2, final_norm_w):
    bsz, seq, d = x.shape
    t = bsz * seq
    depth = ada_w.shape[0]
    assert seq % TM_PROJ == 0 and seq % C_RET == 0 and seq % C_SSD == 0 and seq % C_HG == 0
    x2 = x.reshape(t, d).astype(F32)

    c_pad = jnp.zeros((8, d), F32).at[:bsz].set(c.astype(F32))
    mod_all = _ada_mod(c_pad, ada_w.astype(F32), ada_b.astype(F32))

    half = RET_DK // 2
    inv_freq = ROPE_BASE ** (-jnp.arange(half, dtype=F32) / half)
    invf_col = jnp.broadcast_to(inv_freq[:, None], (half, LANE))
    expand = np.tile(np.eye(half, dtype=np.float32), (1, 2 * RET_HEADS))
    sign = np.tile(np.concatenate([-np.ones(half), np.ones(half)]), RET_HEADS)[None, :].astype(np.float32)
    cos_t, sin_t = _rope_tables(positions.reshape(1, t).astype(jnp.int32), invf_col,
                                jnp.asarray(expand, BF16), jnp.asarray(expand * sign, BF16))
    cos3 = cos_t.reshape(bsz, seq, -1)
    sin3 = sin_t.reshape(bsz, seq, -1)

    lb_cum = jnp.cumsum(jax.nn.softmax(hg_lb_logits.astype(F32), axis=0), axis=0)
    hg_lb = lb_cum - lb_cum[:1]
    tri_ssd = jnp.asarray(np.tril(np.ones((C_SSD, C_SSD), np.float32)), BF16)
    tri_excl = jnp.asarray(np.triu(np.ones((TM_PROJ, TM_PROJ), np.float32), 1), BF16)
    final_w_row = final_norm_w.astype(F32)[None, :]
    w_pad = jnp.pad(w_in.astype(BF16), ((0, 0), (0, 0), (0, IN_W_PAD - IN_W)))
    w_glu_bf = s5_w_glu.astype(BF16)
    w_branch_bf = w_branch.astype(BF16)
    w_gate_bf = w_gate.astype(BF16)
    w_out_bf = w_out.astype(BF16)
    b_gate3 = b_gate.astype(F32).reshape(depth, 1, -1)
    moe_w1_all = moe_w1.astype(F32).reshape(depth * MOE_EXPERTS, d, MOE_FF)
    moe_w3_all = moe_w3.astype(F32).reshape(depth * MOE_EXPERTS, d, MOE_FF)
    moe_w2_all = moe_w2.astype(F32).reshape(depth * MOE_EXPERTS, MOE_FF, d)

    for layer in range(depth):
        mod3 = mod_all[layer, :bsz].reshape(bsz * 6, 1, d)
        p, u_s5 = _in_proj(x2, mod3, w_pad, layer, seq)
        p3 = p.reshape(bsz, seq, IN_W_PAD)

        ops = _s5_operators(s5_lam_re[layer], s5_lam_im[layer], s5_b_re[layer], s5_b_im[layer],
                            s5_c_re[layer], s5_c_im[layer], s5_d[layer], s5_log_dt[layer])
        y_s5 = _s5_scan(u_s5.reshape(bsz, seq, BRANCH_W), *ops).reshape(t, BRANCH_W)

        lb = hg_lb[layer][None, :]
        y_hg = _hgrn2(p3, jnp.log(lb), jnp.log1p(-lb), hg_norm_w[layer].astype(F32)[None, :]).reshape(t, BRANCH_W)

        y_ret = _retention(p3, cos3, sin3).reshape(t, BRANCH_W)

        pad8 = lambda v: jnp.zeros((1, LANE), F32).at[0, :M2_HEADS].set(v.astype(F32))
        y_m2 = _ssd(p3, tri_ssd, m2_conv_w[layer].astype(F32), m2_conv_b[layer].astype(F32)[None, :],
                    pad8(m2_dt_bias[layer]), pad8(m2_a_log[layer]),
                    jnp.repeat(m2_d[layer].astype(F32), M2_HEADDIM)[None, :],
                    m2_norm_w[layer].astype(F32)[None, :]).reshape(t, BRANCH_W)

        nr = 40
        w_route = jnp.zeros((nr, d), F32).at[:MOE_GROUPS].set(moe_w_group[layer].astype(F32).T)
        w_route = w_route.at[MOE_GROUPS:MOE_GROUPS + MOE_EXPERTS].set(moe_w_expert[layer].astype(F32).T)
        b_route = jnp.zeros((nr, LANE), F32).at[:MOE_GROUPS, 0].set(moe_b_group[layer].astype(F32))
        b_route = b_route.at[MOE_GROUPS:MOE_GROUPS + MOE_EXPERTS, 0].set(moe_b_expert[layer].astype(F32))
        x2, h_slabs, ids, wcol, counts = _merge(x2, mod3, y_s5, y_hg, y_ret, y_m2, w_glu_bf, w_branch_bf, w_gate_bf,
                                                b_gate3, w_out_bf, w_route, b_route, tri_excl, layer, seq)
        x2 = _moe(x2, h_slabs, ids, wcol, counts, mod3, final_w_row, moe_w1_all, moe_w3_all, moe_w2_all,
                  layer, seq, final=(layer == depth - 1))
    return x2.reshape(bsz, seq, d)
```

```python
import functools
import math

import numpy as np
import jax
import jax.numpy as jnp
from jax import lax
from jax.experimental import pallas as pl
from jax.experimental.pallas import tpu as pltpu
from jax.experimental.pallas import tpu_sc as plsc

F32 = jnp.float32
BF16 = jnp.bfloat16
HIGHEST = lax.Precision.HIGHEST

D_MODEL = 1024
DEPTH = 2
BRANCH_W = 512
EPS = 1e-6
S5_GROUPS = 32
S5_CH = 16
S5_STATE = 64
S5_MAX_REAL = -1e-4
S5_BLOCK = 16
S5_SEQ_PER_STEP = 2
HG_HEADS = 4
HG_DK = 128
RET_HEADS = 4
RET_DK = 64
RET_DV = 128
ROPE_BASE = 10000.0
M2_HEADS = 8
M2_HEADDIM = 64
M2_GROUPS = 2
M2_STATE = 128
M2_CONV = 4
MOE_GROUPS = 4
MOE_EPG = 8
MOE_EXPERTS = MOE_GROUPS * MOE_EPG
MOE_FF = 256

COL_S5, COL_HQ, COL_HF, COL_HI, COL_HG = 0, 512, 1024, 1536, 2048
COL_RQ, COL_RK, COL_RV, COL_RG = 2560, 2816, 3072, 3584
COL_MZ, COL_MXS, COL_MBC, COL_MDT = 4096, 4608, 5120, 5632
IN_W = 5640
IN_W_PAD = 5888

LANE = 128
VMEM_LIMIT = 56 * 1024 * 1024

TM_PROJ = 1024
TN_PROJ = 1024
TM_INPROJ = 512
LOG2_E = 1.4426950408889634
C_RET = 256
C_SSD = 256
C_HG = 128
TM_X = 512
TM_COMB = 512
SC_WINDOW = 128
SLAB = 256
N_SLAB = D_MODEL // 2 // SLAB


def _cparams(sem):
    return pltpu.CompilerParams(dimension_semantics=sem, vmem_limit_bytes=VMEM_LIMIT)


def _silu(v):
    return v * jax.nn.sigmoid(v)


def _dot_nt(a, b, **kw):
    return lax.dot_general(a, b, (((1,), (1,)), ((), ())), preferred_element_type=F32, **kw)


def _dot_tn(a, b, **kw):
    return lax.dot_general(a, b, (((0,), (0,)), ((), ())), preferred_element_type=F32, **kw)


def _ada_kernel(ct_ref, w_ref, b_ref, o_ref, *, n_rows):
    cond_t = _silu(ct_ref[...])
    w = w_ref[...]
    rows = [jnp.sum(w * cond_t[:, b:b + 1], axis=0, keepdims=True) for b in range(n_rows)]
    rows += [jnp.zeros_like(rows[0])] * (cond_t.shape[1] - n_rows)
    o_ref[...] = jnp.concatenate(rows, axis=0) + b_ref[...]


def _ada_mod(c_pad_t, ada_w, ada_b, n_rows):
    depth, d, n = ada_w.shape
    tn = 1536
    return pl.pallas_call(
        functools.partial(_ada_kernel, n_rows=n_rows),
        grid=(depth, n // tn),
        in_specs=[pl.BlockSpec((d, 8), lambda l, j: (0, 0)),
                  pl.BlockSpec((None, d, tn), lambda l, j: (l, 0, j)),
                  pl.BlockSpec((None, 1, tn), lambda l, j: (l, 0, j))],
        out_specs=pl.BlockSpec((None, 8, tn), lambda l, j: (l, 0, j)),
        out_shape=jax.ShapeDtypeStruct((depth, 8, n), F32),
        compiler_params=_cparams(("parallel", "parallel")),
        name="ada_mod",
    )(c_pad_t, ada_w, ada_b.reshape(depth, 1, n))


def _pack_bf16_pairs(x):
    n = x.shape[1] // 2
    lo = pltpu.bitcast(x[:, :n].astype(BF16).astype(F32), jnp.uint32) >> 16
    hi = pltpu.bitcast(x[:, n:].astype(BF16).astype(F32), jnp.uint32)
    return hi | lo


def _unpack_bf16_pairs(w):
    lo = pltpu.bitcast(w << 16, F32)
    hi = pltpu.bitcast(w & jnp.uint32(0xFFFF0000), F32)
    return jnp.concatenate([lo, hi], axis=-1)


def _modulated_norm(x, scale, shift):
    ms = jnp.mean(x * x, axis=-1, keepdims=True)
    return x * lax.rsqrt(ms + EPS) * (1.0 + scale) + shift


def _inproj_kernel(x_ref, sc_ref, sh_ref, w_ref, o_ref, u_ref):
    h = _modulated_norm(x_ref[...], sc_ref[...], sh_ref[...]).astype(BF16)
    n_total = o_ref.shape[1]
    for n0 in range(0, n_total, TN_PROJ):
        n1 = min(n0 + TN_PROJ, n_total)
        p = jnp.dot(h, w_ref[:, n0:n1], preferred_element_type=F32)
        o_ref[:, n0:n1] = p.astype(o_ref.dtype)
        if n0 == 0:
            u_ref[...] = p[:, COL_S5:COL_S5 + BRANCH_W]


def _in_proj(x2, mod3, w_pad, layer, seq):
    t, d = x2.shape
    tm = TM_INPROJ
    tpb = seq // tm
    assert COL_S5 + BRANCH_W <= TN_PROJ
    return pl.pallas_call(
        _inproj_kernel,
        grid=(t // tm,),
        in_specs=[pl.BlockSpec((tm, d), lambda i: (i, 0)),
                  pl.BlockSpec((None, 1, d), lambda i: ((i // tpb) * 6 + 1, 0, 0)),
                  pl.BlockSpec((None, 1, d), lambda i: ((i // tpb) * 6 + 0, 0, 0)),
                  pl.BlockSpec((None, d, IN_W_PAD), lambda i: (layer, 0, 0), pipeline_mode=pl.Buffered(1))],
        out_specs=[pl.BlockSpec((tm, IN_W_PAD), lambda i: (i, 0)),
                   pl.BlockSpec((tm, BRANCH_W), lambda i: (i, 0))],
        out_shape=[jax.ShapeDtypeStruct((t, IN_W_PAD), BF16), jax.ShapeDtypeStruct((t, BRANCH_W), F32)],
        compiler_params=_cparams(("parallel",)),
        name="in_proj",
    )(x2, mod3, mod3, w_pad)


def _rope_kernel(pos_ref, invf_ref, ecos_ref, esin_ref, cos_ref, sin_ref):
    ang = invf_ref[:, 0:1] * pos_ref[...].astype(F32)
    def spread(values, e_ref):
        hi = values.astype(BF16)
        rest = values - hi.astype(F32)
        mid = rest.astype(BF16)
        lo = (rest - mid.astype(F32)).astype(BF16)
        e = e_ref[...]
        return _dot_tn(hi, e) + _dot_tn(mid, e) + _dot_tn(lo, e)

    cos_ref[...] = spread(jnp.cos(ang), ecos_ref)
    sin_ref[...] = spread(jnp.sin(ang), esin_ref)


def _rope_tables(pos_row, invf_col, expand_cos, expand_sin):
    t = pos_row.shape[1]
    half, w = expand_cos.shape
    tm = 1024
    const = lambda shape: pl.BlockSpec(shape, lambda i: (0, 0))
    return pl.pallas_call(
        _rope_kernel,
        grid=(t // tm,),
        in_specs=[pl.BlockSpec((1, tm), lambda i: (0, i)), const((half, LANE)), const((half, w)), const((half, w))],
        out_specs=[pl.BlockSpec((tm, w), lambda i: (i, 0))] * 2,
        out_shape=[jax.ShapeDtypeStruct((t, w), F32)] * 2,
        compiler_params=_cparams(("parallel",)),
        name="rope_tables",
    )(pos_row, invf_col, expand_cos, expand_sin)


def _ret_kernel(q_ref, k_ref, v_ref, g_ref, cos_ref, sin_ref, o_ref, st_ref, dec_ref, *, chunk):
    @pl.when(pl.program_id(1) == 0)
    def _():
        st_ref[...] = jnp.zeros_like(st_ref)
        ti = lax.broadcasted_iota(jnp.int32, (chunk, chunk), 0)
        si = lax.broadcasted_iota(jnp.int32, (chunk, chunk), 1)
        lag = (ti - si).astype(F32)
        for h in range(RET_HEADS):
            log_gamma = math.log1p(-(2.0 ** (-5.0 - h)))
            dec_ref[h] = jnp.where(ti >= si, jnp.exp(jnp.minimum(lag * log_gamma, 0.0)), 0.0)

    cosf = cos_ref[...]
    sinf = sin_ref[...]
    width = RET_HEADS * RET_DK
    lane = lax.broadcasted_iota(jnp.int32, (chunk, width), 1)
    first_half = (lane % RET_DK) < (RET_DK // 2)

    def rope(t):
        partner = jnp.where(first_half, pltpu.roll(t, width - RET_DK // 2, 1), pltpu.roll(t, RET_DK // 2, 1))
        return t * cosf + partner * sinf

    q = rope(q_ref[...].astype(F32))
    k = rope(k_ref[...].astype(F32)) * (RET_DK ** -0.5)
    v = v_ref[...]
    g = g_ref[...].astype(F32)
    tcol = lax.broadcasted_iota(jnp.int32, (chunk, 1), 0).astype(F32)
    for h in range(RET_HEADS):
        log_gamma = math.log1p(-(2.0 ** (-5.0 - h)))
        qh = q[:, h * RET_DK:(h + 1) * RET_DK]
        kh = k[:, h * RET_DK:(h + 1) * RET_DK]
        vh = v[:, h * RET_DV:(h + 1) * RET_DV].astype(BF16)
        scores = _dot_nt(qh.astype(BF16), kh.astype(BF16)) * dec_ref[h]
        state = st_ref[h]
        q_in = qh * jnp.exp(log_gamma * (tcol + 1.0))
        o = (jnp.dot(scores.astype(BF16), vh, preferred_element_type=F32)
             + jnp.dot(q_in.astype(BF16), state.astype(BF16), preferred_element_type=F32))
        k_out = kh * jnp.exp(log_gamma * (chunk - 1.0 - tcol))
        st_ref[h] = math.exp(log_gamma * chunk) * state + _dot_tn(k_out.astype(BF16), vh)
        o = o * lax.rsqrt(jnp.mean(o * o, axis=-1, keepdims=True) + EPS)
        gh = g[:, h * RET_DV:(h + 1) * RET_DV]
        o_ref[:, h * RET_DV:(h + 1) * RET_DV] = (o * _silu(gh)).astype(o_ref.dtype)


def _retention(p3, cos3, sin3):
    b, seq, _ = p3.shape
    c = C_RET
    qk_w = RET_HEADS * RET_DK
    return pl.pallas_call(
        functools.partial(_ret_kernel, chunk=c),
        grid=(b, seq // c),
        in_specs=[pl.BlockSpec((None, c, qk_w), lambda i, j: (i, j, COL_RQ // qk_w)),
                  pl.BlockSpec((None, c, qk_w), lambda i, j: (i, j, COL_RK // qk_w)),
                  pl.BlockSpec((None, c, BRANCH_W), lambda i, j: (i, j, COL_RV // BRANCH_W)),
                  pl.BlockSpec((None, c, BRANCH_W), lambda i, j: (i, j, COL_RG // BRANCH_W)),
                  pl.BlockSpec((None, c, qk_w), lambda i, j: (i, j, 0)),
                  pl.BlockSpec((None, c, qk_w), lambda i, j: (i, j, 0))],
        out_specs=pl.BlockSpec((None, c, BRANCH_W), lambda i, j: (i, j, 0)),
        out_shape=jax.ShapeDtypeStruct((b, seq, BRANCH_W), BF16),
        scratch_shapes=[pltpu.VMEM((RET_HEADS, RET_DK, RET_DV), F32), pltpu.VMEM((RET_HEADS, c, c), F32)],
        compiler_params=_cparams(("parallel", "arbitrary")),
        name="retention",
    )(p3, p3, p3, p3, cos3, sin3)


def _ssd_kernel(z_ref, xs_ref, bc_ref, dt_ref, tri_ref, cw_ref, cb_ref, dtb_ref, alog_ref, dsk_ref, nw_ref,
                o_ref, xe_scr, st_ref, *, chunk):
    j = pl.program_id(1)
    width = 2 * BRANCH_W

    @pl.when(j == 0)
    def _():
        st_ref[...] = jnp.zeros_like(st_ref)
        xe_scr[0:8, :] = jnp.zeros((8, width), F32)

    @pl.when(j > 0)
    def _():
        xe_scr[0:8, :] = xe_scr[chunk:chunk + 8, :]

    xe_scr[8:, 0:BRANCH_W] = xs_ref[...].astype(F32)
    xe_scr[8:, BRANCH_W:] = bc_ref[...].astype(F32)
    conv = cb_ref[...] + cw_ref[M2_CONV - 1:M2_CONV, :] * xe_scr[8:, :]
    for tap in range(M2_CONV - 1):
        conv = conv + cw_ref[tap:tap + 1, :] * xe_scr[pl.ds(8 - (M2_CONV - 1) + tap, chunk), :]
    conv = _silu(conv)
    xs = conv[:, :BRANCH_W]
    bm = conv[:, BRANCH_W:BRANCH_W + M2_GROUPS * M2_STATE]
    cm = conv[:, BRANCH_W + M2_GROUPS * M2_STATE:]

    dt = jax.nn.softplus(dt_ref[...].astype(F32) + dtb_ref[...])
    da = dt * (-jnp.exp(alog_ref[...]))
    da_hi = da.astype(BF16)
    da_r = da - da_hi.astype(F32)
    da_mid = da_r.astype(BF16)
    da_lo = (da_r - da_mid.astype(F32)).astype(BF16)
    tri = tri_ref[...]
    a_cs = (jnp.dot(tri, da_hi, preferred_element_type=F32) + jnp.dot(tri, da_mid, preferred_element_type=F32)
            + jnp.dot(tri, da_lo, preferred_element_type=F32))
    a_cs = a_cs * LOG2_E
    a_cs_t = a_cs.T
    ti = lax.broadcasted_iota(jnp.int32, (chunk, chunk), 0)
    si = lax.broadcasted_iota(jnp.int32, (chunk, chunk), 1)
    causal = ti >= si
    hpg = M2_HEADS // M2_GROUPS
    ys = []
    for grp in range(M2_GROUPS):
        bm_g = bm[:, grp * M2_STATE:(grp + 1) * M2_STATE]
        cm_g = cm[:, grp * M2_STATE:(grp + 1) * M2_STATE]
        cb = _dot_nt(cm_g.astype(BF16), bm_g.astype(BF16))
        for hh in range(hpg):
            h = grp * hpg + hh
            col = a_cs[:, h:h + 1]
            row = a_cs_t[h:h + 1, :]
            lmat = jnp.where(causal, jnp.exp2(col - row), 0.0)
            xd = xs[:, h * M2_HEADDIM:(h + 1) * M2_HEADDIM] * dt[:, h:h + 1]
            state = st_ref[h]
            y = (jnp.dot((cb * lmat).astype(BF16), xd.astype(BF16), preferred_element_type=F32)
                 + jnp.dot((cm_g * jnp.exp2(col)).astype(BF16), state.astype(BF16), preferred_element_type=F32))
            a_last = a_cs[chunk - 1:chunk, h:h + 1]
            to_end = jnp.exp2(a_last - col)
            st_ref[h] = jnp.exp2(a_last) * state + _dot_tn(bm_g.astype(BF16), (xd * to_end).astype(BF16))
            ys.append(y)
    y = jnp.concatenate(ys, axis=-1) + dsk_ref[...] * xs
    y = y * _silu(z_ref[...].astype(F32))
    o_ref[...] = (y * lax.rsqrt(jnp.mean(y * y, axis=-1, keepdims=True) + EPS) * nw_ref[...]).astype(o_ref.dtype)


def _ssd(p3, tri, conv_w, conv_b, dt_bias_row, a_log_row, d_skip_row, norm_w_row):
    b, seq, _ = p3.shape
    c = C_SSD
    const = lambda shape: pl.BlockSpec(shape, lambda i, j: (0,) * len(shape))
    return pl.pallas_call(
        functools.partial(_ssd_kernel, chunk=c),
        grid=(b, seq // c),
        in_specs=[pl.BlockSpec((None, c, BRANCH_W), lambda i, j: (i, j, COL_MZ // BRANCH_W)),
                  pl.BlockSpec((None, c, BRANCH_W), lambda i, j: (i, j, COL_MXS // BRANCH_W)),
                  pl.BlockSpec((None, c, BRANCH_W), lambda i, j: (i, j, COL_MBC // BRANCH_W)),
                  pl.BlockSpec((None, c, LANE), lambda i, j: (i, j, COL_MDT // LANE)),
                  const((c, c)), const((M2_CONV, 2 * BRANCH_W)), const((1, 2 * BRANCH_W)),
                  const((1, LANE)), const((1, LANE)), const((1, BRANCH_W)), const((1, BRANCH_W))],
        out_specs=pl.BlockSpec((None, c, BRANCH_W), lambda i, j: (i, j, 0)),
        out_shape=jax.ShapeDtypeStruct((b, seq, BRANCH_W), BF16),
        scratch_shapes=[pltpu.VMEM((c + 8, 2 * BRANCH_W), F32),
                        pltpu.VMEM((M2_HEADS, M2_STATE, M2_HEADDIM), F32)],
        compiler_params=_cparams(("parallel", "arbitrary")),
        name="ssd",
    )(p3, p3, p3, p3, tri, conv_w, conv_b, dt_bias_row, a_log_row, d_skip_row, norm_w_row)


def _hg_tables(chunk):
    n_lev = int(math.log2(chunk))
    r = np.arange(chunk)[:, None]
    jj = np.arange(chunk)[None, :]
    tri = (jj <= r).astype(np.float32)
    x = r ^ jj
    levmap = np.where(r > jj, np.floor(np.log2(x + 0.5)), np.where(r == jj, -1, -2)).astype(np.int32)
    return tri, levmap, n_lev


def _hg_level_exponent(b, lev):
    rows, width = b.shape
    m = 1 << lev
    sub = 8
    if 2 * m >= sub:
        blocks = b.reshape(rows // (2 * m), 2 * m, width)
        mid = jnp.broadcast_to(blocks[:, m - 1:m, :], blocks.shape).reshape(rows, width)
    else:
        groups = b.reshape(rows // sub, sub, width)
        row_in_group = lax.broadcasted_iota(jnp.int32, groups.shape, 1)
        mid = None
        for start in range(0, sub, 2 * m):
            picked = jnp.broadcast_to(groups[:, start + m - 1:start + m, :], groups.shape)
            mid = picked if mid is None else jnp.where(row_in_group >= start, picked, mid)
        mid = mid.reshape(rows, width)
    return -jnp.abs(b - mid)


def _hg_kernel(q_ref, f_ref, i_ref, g_ref, sum_ref, lev_ref, llb_ref, l1m_ref, nw_ref, o_ref, st_ref,
               *, chunk, n_lev):
    @pl.when(pl.program_id(1) == 0)
    def _():
        st_ref[...] = jnp.zeros_like(st_ref)

    f = f_ref[...].astype(F32)
    log_sig = jnp.minimum(f, 0.0) - jnp.log1p(jnp.exp(-jnp.abs(f)))
    a = llb_ref[...]
    bb = l1m_ref[...] + log_sig
    log_f = jnp.maximum(a, bb) + jnp.log1p(jnp.exp(-jnp.abs(a - bb)))
    k_all = jnp.exp(l1m_ref[...]) * jax.nn.sigmoid(-f)
    q_all = _silu(q_ref[...].astype(F32))
    hi = log_f.astype(BF16)
    r1 = log_f - hi.astype(F32)
    mid = r1.astype(BF16)
    lo = (r1 - mid.astype(F32)).astype(BF16)
    tri = sum_ref[...]
    b_all = (jnp.dot(tri, hi, preferred_element_type=F32)
             + jnp.dot(tri, mid, preferred_element_type=F32)
             + jnp.dot(tri, lo, preferred_element_type=F32))
    b_all = b_all * LOG2_E
    to_end_all = b_all[chunk - 1:chunk, :] - b_all
    level_decay = [jnp.exp2(_hg_level_exponent(b_all, lev)) for lev in range(n_lev)]
    levmap = lev_ref[...]
    on_diag = levmap == -1
    on_level = [levmap == lev for lev in range(n_lev)]
    v_all = i_ref[...]
    g_all = g_ref[...].astype(F32)
    for h in range(HG_HEADS):
        sl = slice(h * HG_DK, (h + 1) * HG_DK)
        qh = q_all[:, sl]
        kh = k_all[:, sl]
        vh = v_all[:, sl].astype(BF16)
        b_h = b_all[:, sl]
        to_end = to_end_all[:, sl]
        amat = jnp.where(on_diag, _dot_nt(qh.astype(BF16), kh.astype(BF16)), 0.0)
        for lev in range(n_lev):
            e = level_decay[lev][:, sl]
            a_l = _dot_nt((qh * e).astype(BF16), (kh * e).astype(BF16))
            amat = jnp.where(on_level[lev], a_l, amat)
        state_t = st_ref[h]
        o = (jnp.dot(amat.astype(BF16), vh, preferred_element_type=F32)
             + _dot_nt((qh * jnp.exp2(b_h)).astype(BF16), state_t.astype(BF16)))
        k_end = kh * jnp.exp2(to_end)
        st_ref[h] = jnp.exp2(b_h[chunk - 1:chunk, :]) * state_t + _dot_tn(vh, k_end.astype(BF16))
        o = o * lax.rsqrt(jnp.mean(o * o, axis=-1, keepdims=True) + EPS) * nw_ref[...]
        o_ref[:, sl] = (o * _silu(g_all[:, sl])).astype(o_ref.dtype)


def _hgrn2(p3, log_lb_row, log1m_lb_row, norm_w_row):
    b, seq, _ = p3.shape
    c = C_HG
    tri, levmap, n_lev = _hg_tables(c)
    const = lambda shape: pl.BlockSpec(shape, lambda i, j: (0,) * len(shape))
    blk = lambda col: pl.BlockSpec((None, c, BRANCH_W), lambda i, j: (i, j, col // BRANCH_W))
    return pl.pallas_call(
        functools.partial(_hg_kernel, chunk=c, n_lev=n_lev),
        grid=(b, seq // c),
        in_specs=[blk(COL_HQ), blk(COL_HF), blk(COL_HI), blk(COL_HG),
                  const((c, c)), const((c, c)),
                  const((1, BRANCH_W)), const((1, BRANCH_W)), const((1, HG_DK))],
        out_specs=pl.BlockSpec((None, c, BRANCH_W), lambda i, j: (i, j, 0)),
        out_shape=jax.ShapeDtypeStruct((b, seq, BRANCH_W), BF16),
        scratch_shapes=[pltpu.VMEM((HG_HEADS, HG_DK, HG_DK), F32)],
        compiler_params=_cparams(("parallel", "arbitrary")),
        name="hgrn2",
    )(p3, p3, p3, p3, jnp.asarray(tri, BF16), jnp.asarray(levmap), log_lb_row, log1m_lb_row, norm_w_row)


def _expand_block_diag(comp_ref, e_ref, dst_ref, row_div, lane_div):
    gq = LANE // S5_CH
    rows, ncols = dst_ref.shape
    step = 512
    comp = comp_ref[...]
    row_grp = (lax.broadcasted_iota(jnp.int32, (rows, step), 0) // row_div) % gq
    for c0 in range(0, ncols, step):
        lane_grp = ((lax.broadcasted_iota(jnp.int32, (rows, step), 1) + c0) // lane_div) % gq
        full = jnp.dot(comp, e_ref[:, c0:c0 + step], preferred_element_type=F32)
        dst_ref[:, c0:c0 + step] = jnp.where(row_grp == lane_grp, full, 0.0).astype(dst_ref.dtype)


def _s5_kernel(u_ref, k2_ref, bc_ref, cc_ref, esc_ref, eb_ref, lam_ref, o_ref, tc_scr, tq_ref, bq_ref, cq_ref,
               x_scr, w_scr, s_scr, *, rows):
    nb = S5_BLOCK

    @pl.when(pl.program_id(1) == 0)
    def _():
        k2 = k2_ref[...]
        lane = lax.broadcasted_iota(jnp.int32, k2.shape, 1)
        for t in range(nb):
            shifted = k2 if t == 0 else jnp.where(lane >= t * S5_CH, pltpu.roll(k2, t * S5_CH, 1), 0.0)
            tc_scr[t * LANE:(t + 1) * LANE, :] = shifted.astype(tc_scr.dtype)
        _expand_block_diag(tc_scr, esc_ref, tq_ref, S5_CH, S5_CH)
        _expand_block_diag(bc_ref, eb_ref, bq_ref, S5_CH, S5_STATE)
        _expand_block_diag(cc_ref, esc_ref, cq_ref, S5_STATE, S5_CH)

    n_seq = u_ref.shape[0]
    for b in range(n_seq):
        for t in range(nb):
            x_scr[b * rows:(b + 1) * rows, t * LANE:(t + 1) * LANE] = (
                u_ref[b, pl.ds(t, rows, stride=nb), :].astype(x_scr.dtype))
    x = x_scr[...]
    half = w_scr.shape[1] // 2
    w_scr[...] = jnp.dot(x, bq_ref[...], preferred_element_type=F32)
    lam_re = lam_ref[0:1, :]
    lam_im = lam_ref[1:2, :]

    def body(j, carry):
        out = []
        for b in range(n_seq):
            s_re, s_im = carry[2 * b], carry[2 * b + 1]
            r = b * rows + j
            s_scr[pl.ds(r, 1), 0:half] = s_re
            s_scr[pl.ds(r, 1), half:] = s_im
            w_re = w_scr[pl.ds(r, 1), 0:half]
            w_im = w_scr[pl.ds(r, 1), half:]
            out += [lam_re * s_re - lam_im * s_im + w_re, lam_re * s_im + lam_im * s_re + w_im]
        return tuple(out)

    zero = jnp.zeros((1, half), F32)
    lax.fori_loop(0, rows, body, (zero,) * (2 * n_seq))
    s_bf = s_scr[...].astype(BF16)
    pair = 2 * LANE
    for c0 in range(0, nb * LANE, pair):
        k_rows = c0 + pair
        y = (jnp.dot(x[:, :k_rows], tq_ref[0:k_rows, c0:c0 + pair], preferred_element_type=F32)
             + jnp.dot(s_bf, cq_ref[:, c0:c0 + pair], preferred_element_type=F32))
        for b in range(n_seq):
            for t in range(c0 // LANE, (c0 + pair) // LANE):
                o_ref[b, pl.ds(t, rows, stride=nb), :] = y[b * rows:(b + 1) * rows, t * LANE - c0:(t + 1) * LANE - c0]


def _s5_scan(p3, k2, bc, cc, lam16):
    batch, seq, _ = p3.shape
    nb = S5_BLOCK
    nq = BRANCH_W // LANE
    rows = seq // nb
    kdim = nb * LANE
    gq = LANE // S5_CH
    ncol = 2 * gq * S5_STATE
    e_sc = (np.eye(nb)[:, None, :, None, None] * np.eye(S5_CH)[None, :, None, None, :] * np.ones((1, 1, 1, gq, 1)))
    e_sc = e_sc.reshape(nb * S5_CH, nb * gq * S5_CH)
    e_b = (np.eye(2)[:, None, :, None, None] * np.eye(S5_STATE)[None, :, None, None, :] * np.ones((1, 1, 1, gq, 1)))
    e_b = e_b.reshape(2 * S5_STATE, ncol)
    full = lambda shape: pl.BlockSpec(shape, lambda q, b: (0,) * len(shape))
    per_q = lambda r, c: pl.BlockSpec((None, r, c), lambda q, b: (q, 0, 0))
    n_seq = S5_SEQ_PER_STEP if batch % S5_SEQ_PER_STEP == 0 else 1
    return pl.pallas_call(
        functools.partial(_s5_kernel, rows=rows),
        grid=(nq, batch // n_seq),
        in_specs=[pl.BlockSpec((n_seq, seq, LANE), lambda q, b: (b, 0, q)),
                  per_q(LANE, nb * S5_CH), per_q(kdim, 2 * S5_STATE), per_q(ncol, nb * S5_CH),
                  full(e_sc.shape), full(e_b.shape), per_q(2, ncol // 2)],
        out_specs=pl.BlockSpec((n_seq, seq, LANE), lambda q, b: (b, 0, q)),
        out_shape=jax.ShapeDtypeStruct((batch, seq, BRANCH_W), F32),
        scratch_shapes=[pltpu.VMEM((kdim, nb * S5_CH), BF16),
                        pltpu.VMEM((kdim, kdim), BF16), pltpu.VMEM((kdim, ncol), BF16), pltpu.VMEM((ncol, kdim), BF16),
                        pltpu.VMEM((n_seq * rows, kdim), BF16), pltpu.VMEM((n_seq * rows, ncol), F32),
                        pltpu.VMEM((n_seq * rows, ncol), F32)],
        compiler_params=_cparams(("parallel", "arbitrary")),
        name="s5_scan",
    )(p3, k2, bc, cc, jnp.asarray(e_sc, BF16), jnp.asarray(e_b, BF16), lam16)


def _s5_operators(lam_re, lam_im, b_re, b_im, c_re, c_im, d_skip, log_dt):
    nb = S5_BLOCK
    gq = LANE // S5_CH
    nq = S5_GROUPS // gq
    lam = lax.complex(jnp.minimum(lam_re.astype(F32), S5_MAX_REAL), lam_im.astype(F32))
    step = jnp.exp(log_dt.astype(F32))[:, None]
    z = lam * step
    lam_bar = jnp.exp(z)
    b_bar = ((lam_bar - 1.0) / lam)[..., None] * lax.complex(b_re.astype(F32), b_im.astype(F32))
    c_mat = lax.complex(c_re.astype(F32), c_im.astype(F32))
    pw = jnp.exp(z[..., None] * jnp.arange(nb + 1, dtype=F32))
    cp = c_mat[:, None, :, :] * pw[..., :nb].transpose(0, 2, 1)[:, :, None, :]
    cp = jnp.concatenate([cp.real, -cp.imag], axis=-1).reshape(S5_GROUPS, nb * S5_CH, 2 * S5_STATE)
    bri = jnp.concatenate([b_bar.real, b_bar.imag], axis=1)
    kern = jnp.einsum('gnk,gki->gin', cp, bri, precision=HIGHEST)
    skip = (jnp.asarray(np.concatenate([np.eye(S5_CH), np.zeros((S5_CH, (nb - 1) * S5_CH))], axis=1), F32)[None]
            * d_skip.astype(F32).reshape(S5_GROUPS, S5_CH, 1))
    k2 = (kern + skip).reshape(nq, gq * S5_CH, nb * S5_CH)
    pw_rev = jnp.exp(z[..., None] * jnp.asarray(np.arange(nb - 1, -1, -1), F32))
    binc = pw_rev[:, :, :, None] * b_bar[:, :, None, :]
    binc = jnp.stack([binc.real, binc.imag], axis=0).reshape(2, nq, gq, S5_STATE, nb, S5_CH)
    bc = binc.transpose(1, 4, 2, 5, 0, 3).reshape(nq, nb * LANE, 2 * S5_STATE)
    cm = c_mat.transpose(0, 2, 1)[:, :, None, :] * pw[..., 1:][:, :, :, None]
    cm = jnp.stack([cm.real, -cm.imag], axis=0).reshape(2, nq, gq * S5_STATE, nb * S5_CH)
    cc = cm.transpose(1, 0, 2, 3).reshape(nq, 2 * gq * S5_STATE, nb * S5_CH)
    lam_n = pw[..., nb].reshape(nq, gq * S5_STATE)
    lam16 = jnp.stack([lam_n.real, lam_n.imag], axis=1)
    return k2, bc.astype(BF16), cc.astype(BF16), lam16


def _merge_kernel(x_ref, sc_ref, sh_ref, gm_ref, ys5_ref, yhg_ref, yret_ref, ym2_ref,
                  wglu_ref, wbr_ref, wg_ref, bg_ref, wout_ref, o_ref):
    x = x_ref[...]
    d = x.shape[1]
    h = _modulated_norm(x, sc_ref[...], sh_ref[...]).astype(BF16)
    y_s5 = jax.nn.gelu(ys5_ref[...])
    y_s5 = y_s5 * jax.nn.sigmoid(jnp.dot(y_s5.astype(BF16), wglu_ref[...], preferred_element_type=F32))
    acc = jnp.zeros(x.shape, F32)
    for n, y in enumerate((y_s5, yhg_ref[...], yret_ref[...], ym2_ref[...])):
        gate = jax.nn.sigmoid(jnp.dot(h, wg_ref[:, n * d:(n + 1) * d], preferred_element_type=F32)
                              + bg_ref[:, n * d:(n + 1) * d])
        acc = acc + gate * jnp.dot(y.astype(BF16), wbr_ref[n], preferred_element_type=F32)
    o_ref[...] = x + gm_ref[...] * jnp.dot(acc.astype(BF16), wout_ref[...], preferred_element_type=F32)


def _merge(x2, mod3, ys5, yhg, yret, ym2, w_glu, w_branch, w_gate, b_gate, w_out, layer, seq):
    t, d = x2.shape
    tm = TM_PROJ
    tpb = seq // tm
    const = lambda shape: pl.BlockSpec((None,) + shape, lambda i: (layer,) + (0,) * len(shape),
                                       pipeline_mode=pl.Buffered(1))
    modspec = lambda k: pl.BlockSpec((None, 1, d), lambda i: ((i // tpb) * 6 + k, 0, 0))
    yspec = pl.BlockSpec((tm, BRANCH_W), lambda i: (i, 0))
    return pl.pallas_call(
        _merge_kernel,
        grid=(t // tm,),
        in_specs=[pl.BlockSpec((tm, d), lambda i: (i, 0)), modspec(1), modspec(0), modspec(2),
                  yspec, yspec, yspec, yspec,
                  const((BRANCH_W, BRANCH_W)), const((4, BRANCH_W, d)), const((d, 4 * d)), const((1, 4 * d)),
                  const((d, d))],
        out_specs=pl.BlockSpec((tm, d), lambda i: (i, 0)),
        out_shape=jax.ShapeDtypeStruct((t, d), F32),
        compiler_params=_cparams(("parallel",)),
        name="merge",
    )(x2, mod3, mod3, mod3, ys5, yhg, yret, ym2, w_glu, w_branch, w_gate, b_gate, w_out)


def _router_kernel(x_ref, sc_ref, sh_ref, wr_ref, br_ref, tri_ref, h_ref, ids_ref, wts_ref, cnt_ref, carry):
    i = pl.program_id(0)

    @pl.when(i == 0)
    def _():
        carry[...] = jnp.zeros_like(carry)

    h = _modulated_norm(x_ref[...], sc_ref[...], sh_ref[...])
    tm, d = h.shape
    packed = _pack_bf16_pairs(h)
    for k in range(N_SLAB):
        h_ref[k] = packed[:, k * SLAB:(k + 1) * SLAB]
    h_hi = h.astype(BF16)
    h_lo = (h - h_hi.astype(F32)).astype(BF16)
    w_r = wr_ref[...]
    w_hi = w_r.astype(BF16)
    w_lo = (w_r - w_hi.astype(F32)).astype(BF16)
    logits = _dot_nt(w_hi, h_hi) + _dot_nt(w_hi, h_lo) + _dot_nt(w_lo, h_hi) + br_ref[:, 0:1]
    gl = [logits[g:g + 1, :] for g in range(MOE_GROUPS)]
    gmax = gl[0]
    gsel = jnp.zeros((1, tm), jnp.int32)
    for g in range(1, MOE_GROUPS):
        better = gl[g] > gmax
        gsel = jnp.where(better, g, gsel)
        gmax = jnp.where(better, gl[g], gmax)
    gden = gl[0] * 0.0
    for g in range(MOE_GROUPS):
        gden = gden + jnp.exp(gl[g] - gmax)
    g_w = 1.0 / gden
    el = []
    for e in range(MOE_EPG):
        v = logits[MOE_GROUPS + e:MOE_GROUPS + e + 1, :]
        for g in range(1, MOE_GROUPS):
            row = MOE_GROUPS + g * MOE_EPG + e
            v = jnp.where(gsel == g, logits[row:row + 1, :], v)
        el.append(v)
    v1 = el[0]
    i1 = jnp.zeros((1, tm), jnp.int32)
    for e in range(1, MOE_EPG):
        better = el[e] > v1
        i1 = jnp.where(better, e, i1)
        v1 = jnp.where(better, el[e], v1)
    v2 = jnp.full((1, tm), -jnp.inf, F32)
    i2 = jnp.zeros((1, tm), jnp.int32)
    for e in range(MOE_EPG):
        better = (el[e] > v2) & (i1 != e)
        i2 = jnp.where(better, e, i2)
        v2 = jnp.where(better, el[e], v2)
    ex = jnp.exp(v2 - v1)
    p1 = 1.0 / (1.0 + ex)
    e1 = gsel * MOE_EPG + i1
    e2 = gsel * MOE_EPG + i2
    erow = lax.broadcasted_iota(jnp.int32, (MOE_EXPERTS, tm), 0)
    oh1 = (erow == e1).astype(F32)
    oh2 = (erow == e2).astype(F32)
    both = oh1 + oh2
    prefix = jnp.dot(both.astype(BF16), tri_ref[...], preferred_element_type=F32) + carry[:, 0:1]
    rank1 = jnp.sum(oh1 * prefix, axis=0, keepdims=True).astype(jnp.int32)
    rank2 = jnp.sum(oh2 * prefix, axis=0, keepdims=True).astype(jnp.int32)
    carry[...] = carry[...] + jnp.sum(both, axis=1, keepdims=True)
    zi = jnp.zeros((1, tm), jnp.int32)
    ids_ref[...] = jnp.concatenate([e1, e2, rank1, rank2, zi, zi, zi, zi], axis=0)
    wrow = lax.broadcasted_iota(jnp.int32, (LANE, tm), 0)
    wts_ref[...] = jnp.where(wrow == 0, p1 * g_w, jnp.where(wrow == 1, ex * p1 * g_w, 0.0)).T
    cnt_ref[...] = carry[...]


def _router(x2, mod3, w_route, b_route, tri_excl, seq):
    t, d = x2.shape
    tm = TM_PROJ
    tpb = seq // tm
    nr = w_route.shape[0]
    const = lambda shape: pl.BlockSpec(shape, lambda i: (0,) * len(shape))
    modspec = lambda k: pl.BlockSpec((None, 1, d), lambda i: ((i // tpb) * 6 + k, 0, 0))
    return pl.pallas_call(
        _router_kernel,
        grid=(t // tm,),
        in_specs=[pl.BlockSpec((tm, d), lambda i: (i, 0)), modspec(4), modspec(3),
                  const((nr, d)), const((nr, LANE)), const((tm, tm))],
        out_specs=[pl.BlockSpec((N_SLAB, tm, SLAB), lambda i: (0, i, 0)),
                   pl.BlockSpec((8, tm), lambda i: (0, i)),
                   pl.BlockSpec((tm, LANE), lambda i: (i, 0)),
                   const((MOE_EXPERTS, LANE))],
        out_shape=[jax.ShapeDtypeStruct((N_SLAB, t, SLAB), jnp.uint32),
                   jax.ShapeDtypeStruct((8, t), jnp.int32),
                   jax.ShapeDtypeStruct((t, LANE), F32),
                   jax.ShapeDtypeStruct((MOE_EXPERTS, LANE), F32)],
        scratch_shapes=[pltpu.VMEM((MOE_EXPERTS, LANE), F32)],
        compiler_params=_cparams(("arbitrary",)),
        name="moe_router",
    )(x2, mod3, mod3, w_route, b_route, tri_excl)


def _sc_mesh():
    return plsc.VectorSubcoreMesh(core_axis_name="core", subcore_axis_name="subcore")


def _slab_rows(idx, n_rows):
    return (idx[None, :] + (jnp.arange(N_SLAB, dtype=jnp.int32) * n_rows)[:, None]).reshape(-1)


def _dispatch(slot1, slot2, h_slabs):
    n_slab, t, d = h_slabs.shape
    n_out = 2 * t
    xs = _scatter_rows(h_slabs.reshape(n_slab * t, d), _slab_rows(slot1, n_out), _slab_rows(slot2, n_out),
                       n_slab * n_out)
    return xs.reshape(n_slab, n_out, d)


def _scatter_rows(src, idx1, idx2, n_out):
    t, d = src.shape
    win = SC_WINDOW

    @pl.kernel(out_type=jax.ShapeDtypeStruct((n_out, d), src.dtype), mesh=_sc_mesh(), name="moe_dispatch_sc")
    def scatter_rows(x_hbm, i1_hbm, i2_hbm, o_hbm):
        def body(x_vmem, i1_vmem, i2_vmem):
            pltpu.sync_copy(x_vmem, o_hbm.at[i1_vmem.at[0]])
            pltpu.sync_copy(x_vmem, o_hbm.at[i2_vmem.at[0]])

        pltpu.emit_pipeline(
            body,
            grid=(t // win,),
            in_specs=[pl.BlockSpec((win, d), lambda i: (i, 0)),
                      pl.BlockSpec((1, win), lambda i: (0, i)),
                      pl.BlockSpec((1, win), lambda i: (0, i))],
            out_specs=[],
            core_axis_name=("core", "subcore"),
            dimension_semantics=(pltpu.PARALLEL,),
        )(x_hbm, i1_hbm, i2_hbm)

    return scatter_rows(src, idx1.reshape(1, t), idx2.reshape(1, t))


def _gather_rows(src, idx):
    m = idx.shape[0]
    d = src.shape[1]
    win = SC_WINDOW

    @pl.kernel(out_type=jax.ShapeDtypeStruct((m, d), src.dtype), mesh=_sc_mesh(), name="moe_gather_sc")
    def gather(x_hbm, i_hbm, o_hbm):
        def body(i_vmem, o_vmem):
            pltpu.sync_copy(x_hbm.at[i_vmem.at[0]], o_vmem)

        pltpu.emit_pipeline(
            body,
            grid=(m // win,),
            in_specs=[pl.BlockSpec((1, win), lambda i: (0, i))],
            out_specs=[pl.BlockSpec((win, d), lambda i: (i, 0))],
            core_axis_name=("core", "subcore"),
            dimension_semantics=(pltpu.PARALLEL,),
        )(i_hbm, o_hbm)

    return gather(src, idx.reshape(1, m))


def _expert_kernel(tile_ref, exp_ref, lo_ref, hi_ref, xs_ref, w1_ref, w3_ref, w2_ref, ys_ref, w1_scr, w3_scr, w2_scr):
    s = pl.program_id(0)
    prev = jnp.maximum(s - 1, 0)
    new_expert = (s == 0) | (exp_ref[s] != exp_ref[prev])
    new_tile = (s == 0) | (tile_ref[s] != tile_ref[prev])

    @pl.when(new_expert)
    def _():
        w1_scr[...] = w1_ref[...].astype(BF16)
        w3_scr[...] = w3_ref[...].astype(BF16)
        w2_scr[...] = w2_ref[...].astype(BF16)

    x = _unpack_bf16_pairs(jnp.concatenate([xs_ref[k] for k in range(N_SLAB)], axis=-1)).astype(BF16)
    a = jnp.dot(x, w1_scr[...], preferred_element_type=F32)
    b = jnp.dot(x, w3_scr[...], preferred_element_type=F32)
    act = _silu(a) * b
    y = _pack_bf16_pairs(jnp.dot(act.astype(BF16), w2_scr[...], preferred_element_type=F32))
    row = lax.broadcasted_iota(jnp.int32, (y.shape[0], SLAB), 0)
    mine = (row >= lo_ref[s]) & (row < hi_ref[s])

    @pl.when(new_tile)
    def _():
        for k in range(N_SLAB):
            ys_ref[k] = jnp.where(mine, y[:, k * SLAB:(k + 1) * SLAB], jnp.uint32(0))

    @pl.when(jnp.logical_not(new_tile))
    def _():
        for k in range(N_SLAB):
            ys_ref[k] = jnp.where(mine, y[:, k * SLAB:(k + 1) * SLAB], ys_ref[k])


def _experts(step_tile, step_expert, step_lo, step_hi, xs, w1, w3, w2, layer):
    n_slab, ns, slab = xs.shape
    d = w1.shape[1]
    ff = w1.shape[2]
    n_steps = step_tile.shape[0]
    base = layer * MOE_EXPERTS
    grid_spec = pltpu.PrefetchScalarGridSpec(
        num_scalar_prefetch=4,
        grid=(n_steps,),
        in_specs=[pl.BlockSpec((n_slab, TM_X, slab), lambda s, tl, ex, lo, hi: (0, tl[s], 0)),
                  pl.BlockSpec((None, d, ff), lambda s, tl, ex, lo, hi: (base + ex[s], 0, 0)),
                  pl.BlockSpec((None, d, ff), lambda s, tl, ex, lo, hi: (base + ex[s], 0, 0)),
                  pl.BlockSpec((None, ff, d), lambda s, tl, ex, lo, hi: (base + ex[s], 0, 0))],
        out_specs=pl.BlockSpec((n_slab, TM_X, slab), lambda s, tl, ex, lo, hi: (0, tl[s], 0)),
        scratch_shapes=[pltpu.VMEM((d, ff), BF16), pltpu.VMEM((d, ff), BF16), pltpu.VMEM((ff, d), BF16)],
    )
    return pl.pallas_call(
        _expert_kernel,
        grid_spec=grid_spec,
        out_shape=jax.ShapeDtypeStruct((n_slab, ns, slab), xs.dtype),
        compiler_params=_cparams(("arbitrary",)),
        name="moe_experts",
    )(step_tile, step_expert, step_lo, step_hi, xs, w1, w3, w2)


def _combine_kernel(x_ref, gate_ref, fw_ref, wcol_ref, y1_ref, y2_ref, o_ref, *, final):
    w_first = wcol_ref[:, 0:1]
    w_second = wcol_ref[:, 1:2]
    y_first = _unpack_bf16_pairs(jnp.concatenate([y1_ref[k] for k in range(N_SLAB)], axis=-1))
    y_second = _unpack_bf16_pairs(jnp.concatenate([y2_ref[k] for k in range(N_SLAB)], axis=-1))
    moe = w_first * y_first + w_second * y_second
    x = x_ref[...] + gate_ref[...] * moe
    if final:
        x = x * lax.rsqrt(jnp.mean(x * x, axis=-1, keepdims=True) + EPS) * fw_ref[...]
    o_ref[...] = x


def _combine(x2, mod3, final_w_row, wcol, gathered, seq, final):
    t, d = x2.shape
    tm = TM_COMB
    tpb = seq // tm
    nblk = t // tm
    yspec = lambda off: pl.BlockSpec((N_SLAB, tm, SLAB), lambda i: (0, i + off, 0))
    return pl.pallas_call(
        functools.partial(_combine_kernel, final=final),
        grid=(nblk,),
        in_specs=[pl.BlockSpec((tm, d), lambda i: (i, 0)),
                  pl.BlockSpec((None, 1, d), lambda i: ((i // tpb) * 6 + 5, 0, 0)),
                  pl.BlockSpec((1, d), lambda i: (0, 0)),
                  pl.BlockSpec((tm, LANE), lambda i: (i, 0)),
                  yspec(0), yspec(nblk)],
        out_specs=pl.BlockSpec((tm, d), lambda i: (i, 0)),
        out_shape=jax.ShapeDtypeStruct((t, d), F32),
        compiler_params=_cparams(("parallel",)),
        name="moe_combine",
    )(x2, mod3, final_w_row, wcol, gathered, gathered)


def _moe(x2, mod3, final_w_row, w_route, b_route, tri_excl, w1, w3, w2, layer, seq, final):
    t, d = x2.shape
    h3, ids, wcol, counts = _router(x2, mod3, w_route, b_route, tri_excl, seq)
    cnt = counts[:, 0].astype(jnp.int32)
    ends = jnp.cumsum(cnt)
    offs = ends - cnt
    experts = jnp.arange(MOE_EXPERTS, dtype=jnp.int32)
    pick = lambda table, idx: jnp.sum(jnp.where(idx[:, None] == experts[None, :], table[None, :], 0), axis=1)
    slot1 = pick(offs, ids[0]) + ids[2]
    slot2 = pick(offs, ids[1]) + ids[3]
    n_tiles = 2 * t // TM_X
    first_tile = offs // TM_X
    n_vis = jnp.where(cnt > 0, (ends - 1) // TM_X - first_tile + 1, 0)
    cum = jnp.cumsum(n_vis)
    step = jnp.arange(n_tiles + MOE_EXPERTS, dtype=jnp.int32)
    step_expert = jnp.minimum(jnp.sum(step[:, None] >= cum[None, :], axis=1), MOE_EXPERTS - 1).astype(jnp.int32)
    valid = step < cum[-1]
    step_tile = jnp.where(valid, pick(first_tile - (cum - n_vis), step_expert) + step, n_tiles - 1)
    step_lo = jnp.where(valid, jnp.clip(pick(offs, step_expert) - step_tile * TM_X, 0, TM_X), 0)
    step_hi = jnp.where(valid, jnp.clip(pick(ends, step_expert) - step_tile * TM_X, 0, TM_X), 0)
    xs = _dispatch(slot1, slot2, h3)
    ys = _experts(step_tile.astype(jnp.int32), step_expert, step_lo.astype(jnp.int32), step_hi.astype(jnp.int32),
                  xs, w1, w3, w2, layer)
    n_sorted = ys.shape[1]
    gathered = _gather_rows(ys.reshape(N_SLAB * n_sorted, SLAB), _slab_rows(jnp.concatenate([slot1, slot2]), n_sorted))
    gathered = gathered.reshape(N_SLAB, n_sorted, SLAB)
    return _combine(x2, mod3, final_w_row, wcol, gathered, seq, final)


def kernel(x, c, positions, ada_w, ada_b, w_in, s5_lam_re, s5_lam_im, s5_b_re, s5_b_im, s5_c_re, s5_c_im, s5_d, s5_log_dt, s5_w_glu, hg_lb_logits, hg_norm_w, m2_conv_w, m2_conv_b, m2_dt_bias, m2_a_log, m2_d, m2_norm_w, w_branch, w_gate, b_gate, w_out, moe_w_group, moe_b_group, moe_w_expert, moe_b_expert, moe_w1, moe_w3, moe_w2, final_norm_w):
    bsz, seq, d = x.shape
    t = bsz * seq
    depth = ada_w.shape[0]
    assert seq % TM_PROJ == 0 and seq % C_RET == 0 and seq % C_SSD == 0 and seq % C_HG == 0
    x2 = x.reshape(t, d).astype(F32)

    c_pad = jnp.zeros((8, d), F32).at[:bsz].set(c.astype(F32))
    mod_all = _ada_mod(c_pad.T, ada_w.astype(F32), ada_b.astype(F32), bsz)

    half = RET_DK // 2
    inv_freq = ROPE_BASE ** (-jnp.arange(half, dtype=F32) / half)
    invf_col = jnp.broadcast_to(inv_freq[:, None], (half, LANE))
    expand = np.tile(np.eye(half, dtype=np.float32), (1, 2 * RET_HEADS))
    sign = np.tile(np.concatenate([-np.ones(half), np.ones(half)]), RET_HEADS)[None, :].astype(np.float32)
    cos_t, sin_t = _rope_tables(positions.reshape(1, t).astype(jnp.int32), invf_col,
                                jnp.asarray(expand, BF16), jnp.asarray(expand * sign, BF16))
    cos3 = cos_t.reshape(bsz, seq, -1)
    sin3 = sin_t.reshape(bsz, seq, -1)

    lb_cum = jnp.cumsum(jax.nn.softmax(hg_lb_logits.astype(F32), axis=0), axis=0)
    hg_lb = lb_cum - lb_cum[:1]
    tri_ssd = jnp.asarray(np.tril(np.ones((C_SSD, C_SSD), np.float32)), BF16)
    tri_excl = jnp.asarray(np.triu(np.ones((TM_PROJ, TM_PROJ), np.float32), 1), BF16)
    final_w_row = final_norm_w.astype(F32)[None, :]
    w_pad = jnp.pad(w_in.astype(BF16), ((0, 0), (0, 0), (0, IN_W_PAD - IN_W)))
    w_glu_bf = s5_w_glu.astype(BF16)
    w_branch_bf = w_branch.astype(BF16)
    w_gate_bf = w_gate.astype(BF16)
    w_out_bf = w_out.astype(BF16)
    b_gate3 = b_gate.astype(F32).reshape(depth, 1, -1)
    moe_w1_all = moe_w1.astype(F32).reshape(depth * MOE_EXPERTS, d, MOE_FF)
    moe_w3_all = moe_w3.astype(F32).reshape(depth * MOE_EXPERTS, d, MOE_FF)
    moe_w2_all = moe_w2.astype(F32).reshape(depth * MOE_EXPERTS, MOE_FF, d)

    for layer in range(depth):
        mod3 = mod_all[layer, :bsz].reshape(bsz * 6, 1, d)
        p, u_s5 = _in_proj(x2, mod3, w_pad, layer, seq)
        p3 = p.reshape(bsz, seq, IN_W_PAD)

        ops = _s5_operators(s5_lam_re[layer], s5_lam_im[layer], s5_b_re[layer], s5_b_im[layer],
                            s5_c_re[layer], s5_c_im[layer], s5_d[layer], s5_log_dt[layer])
        y_s5 = _s5_scan(u_s5.reshape(bsz, seq, BRANCH_W), *ops).reshape(t, BRANCH_W)

        lb = hg_lb[layer][None, :]
        y_hg = _hgrn2(p3, jnp.log(lb), jnp.log1p(-lb), hg_norm_w[layer].astype(F32)[None, :]).reshape(t, BRANCH_W)

        y_ret = _retention(p3, cos3, sin3).reshape(t, BRANCH_W)

        pad8 = lambda v: jnp.zeros((1, LANE), F32).at[0, :M2_HEADS].set(v.astype(F32))
        y_m2 = _ssd(p3, tri_ssd, m2_conv_w[layer].astype(F32), m2_conv_b[layer].astype(F32)[None, :],
                    pad8(m2_dt_bias[layer]), pad8(m2_a_log[layer]),
                    jnp.repeat(m2_d[layer].astype(F32), M2_HEADDIM)[None, :],
                    m2_norm_w[layer].astype(F32)[None, :]).reshape(t, BRANCH_W)

        nr = 40
        w_route = jnp.zeros((nr, d), F32).at[:MOE_GROUPS].set(moe_w_group[layer].astype(F32).T)
        w_route = w_route.at[MOE_GROUPS:MOE_GROUPS + MOE_EXPERTS].set(moe_w_expert[layer].astype(F32).T)
        b_route = jnp.zeros((nr, LANE), F32).at[:MOE_GROUPS, 0].set(moe_b_group[layer].astype(F32))
        b_route = b_route.at[MOE_GROUPS:MOE_GROUPS + MOE_EXPERTS, 0].set(moe_b_expert[layer].astype(F32))
        x2 = _merge(x2, mod3, y_s5, y_hg, y_ret, y_m2, w_glu_bf, w_branch_bf, w_gate_bf, b_gate3, w_out_bf,
                    layer, seq)
        x2 = _moe(x2, mod3, final_w_row, w_route, b_route, tri_excl, moe_w1_all, moe_w3_all, moe_w2_all,
                  layer, seq, final=(layer == depth - 1))
    return x2.reshape(bsz, seq, d)
```

```python
import functools
import math

import numpy as np
import jax
import jax.numpy as jnp
from jax import lax
from jax.experimental import pallas as pl
from jax.experimental.pallas import tpu as pltpu
from jax.experimental.pallas import tpu_sc as plsc

F32 = jnp.float32
BF16 = jnp.bfloat16
HIGHEST = lax.Precision.HIGHEST

D_MODEL = 1024
DEPTH = 2
BRANCH_W = 512
EPS = 1e-6
S5_GROUPS = 32
S5_CH = 16
S5_STATE = 64
S5_MAX_REAL = -1e-4
S5_BLOCK = 16
S5_SEQ_PER_STEP = 2
HG_HEADS = 4
HG_DK = 128
RET_HEADS = 4
RET_DK = 64
RET_DV = 128
ROPE_BASE = 10000.0
M2_HEADS = 8
M2_HEADDIM = 64
M2_GROUPS = 2
M2_STATE = 128
M2_CONV = 4
MOE_GROUPS = 4
MOE_EPG = 8
MOE_EXPERTS = MOE_GROUPS * MOE_EPG
MOE_FF = 256

COL_S5, COL_HQ, COL_HF, COL_HI, COL_HG = 0, 512, 1024, 1536, 2048
COL_RQ, COL_RK, COL_RV, COL_RG = 2560, 2816, 3072, 3584
COL_MZ, COL_MXS, COL_MBC, COL_MDT = 4096, 4608, 5120, 5632
IN_W = 5640
IN_W_PAD = 5888

LANE = 128
VMEM_LIMIT = 56 * 1024 * 1024

TM_PROJ = 1024
TN_PROJ = 1024
TM_INPROJ = 512
LOG2_E = 1.4426950408889634
C_RET = 512
C_SSD = 256
C_HG = 128
TM_X = 512
TM_COMB = 512
SC_WINDOW = 128
SLAB = 256
N_SLAB = D_MODEL // 2 // SLAB


def _cparams(sem):
    return pltpu.CompilerParams(dimension_semantics=sem, vmem_limit_bytes=VMEM_LIMIT)


def _silu(v):
    return v * jax.nn.sigmoid(v)


def _dot_nt(a, b, **kw):
    return lax.dot_general(a, b, (((1,), (1,)), ((), ())), preferred_element_type=F32, **kw)


def _dot_tn(a, b, **kw):
    return lax.dot_general(a, b, (((0,), (0,)), ((), ())), preferred_element_type=F32, **kw)


def _ada_kernel(ct_ref, w_ref, b_ref, o_ref, *, n_rows):
    cond_t = _silu(ct_ref[...])
    w = w_ref[...]
    rows = [jnp.sum(w * cond_t[:, b:b + 1], axis=0, keepdims=True) for b in range(n_rows)]
    rows += [jnp.zeros_like(rows[0])] * (cond_t.shape[1] - n_rows)
    o_ref[...] = jnp.concatenate(rows, axis=0) + b_ref[...]


def _ada_mod(c_pad_t, ada_w, ada_b, n_rows):
    depth, d, n = ada_w.shape
    tn = 1536
    return pl.pallas_call(
        functools.partial(_ada_kernel, n_rows=n_rows),
        grid=(depth, n // tn),
        in_specs=[pl.BlockSpec((d, 8), lambda l, j: (0, 0)),
                  pl.BlockSpec((None, d, tn), lambda l, j: (l, 0, j)),
                  pl.BlockSpec((None, 1, tn), lambda l, j: (l, 0, j))],
        out_specs=pl.BlockSpec((None, 8, tn), lambda l, j: (l, 0, j)),
        out_shape=jax.ShapeDtypeStruct((depth, 8, n), F32),
        compiler_params=_cparams(("parallel", "parallel")),
        name="ada_mod",
    )(c_pad_t, ada_w, ada_b.reshape(depth, 1, n))


def _pack_bf16_pairs(x):
    n = x.shape[1] // 2
    lo = pltpu.bitcast(x[:, :n].astype(BF16).astype(F32), jnp.uint32) >> 16
    hi = pltpu.bitcast(x[:, n:].astype(BF16).astype(F32), jnp.uint32)
    return hi | lo


def _unpack_bf16_pairs(w):
    lo = pltpu.bitcast(w << 16, F32)
    hi = pltpu.bitcast(w & jnp.uint32(0xFFFF0000), F32)
    return jnp.concatenate([lo, hi], axis=-1)


def _modulated_norm(x, scale, shift):
    ms = jnp.mean(x * x, axis=-1, keepdims=True)
    return x * lax.rsqrt(ms + EPS) * (1.0 + scale) + shift


def _inproj_kernel(x_ref, sc_ref, sh_ref, w_ref, o_ref, u_ref):
    h = _modulated_norm(x_ref[...], sc_ref[...], sh_ref[...]).astype(BF16)
    n_total = o_ref.shape[1]
    for n0 in range(0, n_total, TN_PROJ):
        n1 = min(n0 + TN_PROJ, n_total)
        p = jnp.dot(h, w_ref[:, n0:n1], preferred_element_type=F32)
        o_ref[:, n0:n1] = p.astype(o_ref.dtype)
        if n0 == 0:
            u_ref[...] = p[:, COL_S5:COL_S5 + BRANCH_W]


def _in_proj(x2, mod3, w_pad, layer, seq):
    t, d = x2.shape
    tm = TM_INPROJ
    tpb = seq // tm
    assert COL_S5 + BRANCH_W <= TN_PROJ
    return pl.pallas_call(
        _inproj_kernel,
        grid=(t // tm,),
        in_specs=[pl.BlockSpec((tm, d), lambda i: (i, 0)),
                  pl.BlockSpec((None, 1, d), lambda i: ((i // tpb) * 6 + 1, 0, 0)),
                  pl.BlockSpec((None, 1, d), lambda i: ((i // tpb) * 6 + 0, 0, 0)),
                  pl.BlockSpec((None, d, IN_W_PAD), lambda i: (layer, 0, 0), pipeline_mode=pl.Buffered(1))],
        out_specs=[pl.BlockSpec((tm, IN_W_PAD), lambda i: (i, 0)),
                   pl.BlockSpec((tm, BRANCH_W), lambda i: (i, 0))],
        out_shape=[jax.ShapeDtypeStruct((t, IN_W_PAD), BF16), jax.ShapeDtypeStruct((t, BRANCH_W), F32)],
        compiler_params=_cparams(("parallel",)),
        name="in_proj",
    )(x2, mod3, mod3, w_pad)


def _rope_kernel(pos_ref, invf_ref, ecos_ref, esin_ref, cos_ref, sin_ref):
    ang = invf_ref[:, 0:1] * pos_ref[...].astype(F32)
    def spread(values, e_ref):
        hi = values.astype(BF16)
        rest = values - hi.astype(F32)
        mid = rest.astype(BF16)
        lo = (rest - mid.astype(F32)).astype(BF16)
        e = e_ref[...]
        return _dot_tn(hi, e) + _dot_tn(mid, e) + _dot_tn(lo, e)

    cos_ref[...] = spread(jnp.cos(ang), ecos_ref)
    sin_ref[...] = spread(jnp.sin(ang), esin_ref)


def _rope_tables(pos_row, invf_col, expand_cos, expand_sin):
    t = pos_row.shape[1]
    half, w = expand_cos.shape
    tm = 1024
    const = lambda shape: pl.BlockSpec(shape, lambda i: (0, 0))
    return pl.pallas_call(
        _rope_kernel,
        grid=(t // tm,),
        in_specs=[pl.BlockSpec((1, tm), lambda i: (0, i)), const((half, LANE)), const((half, w)), const((half, w))],
        out_specs=[pl.BlockSpec((tm, w), lambda i: (i, 0))] * 2,
        out_shape=[jax.ShapeDtypeStruct((t, w), F32)] * 2,
        compiler_params=_cparams(("parallel",)),
        name="rope_tables",
    )(pos_row, invf_col, expand_cos, expand_sin)


def _ret_kernel(q_ref, k_ref, v_ref, g_ref, cos_ref, sin_ref, o_ref, st_ref, dec_ref, *, chunk):
    @pl.when(pl.program_id(1) == 0)
    def _():
        st_ref[...] = jnp.zeros_like(st_ref)
        ti = lax.broadcasted_iota(jnp.int32, (chunk, chunk), 0)
        si = lax.broadcasted_iota(jnp.int32, (chunk, chunk), 1)
        lag = (ti - si).astype(F32)
        for h in range(RET_HEADS):
            log_gamma = math.log1p(-(2.0 ** (-5.0 - h)))
            dec_ref[h] = jnp.where(ti >= si, jnp.exp(jnp.minimum(lag * log_gamma, 0.0)), 0.0)

    cosf = cos_ref[...]
    sinf = sin_ref[...]
    width = RET_HEADS * RET_DK
    lane = lax.broadcasted_iota(jnp.int32, (chunk, width), 1)
    first_half = (lane % RET_DK) < (RET_DK // 2)

    def rope(t):
        partner = jnp.where(first_half, pltpu.roll(t, width - RET_DK // 2, 1), pltpu.roll(t, RET_DK // 2, 1))
        return t * cosf + partner * sinf

    q = rope(q_ref[...].astype(F32))
    k = rope(k_ref[...].astype(F32)) * (RET_DK ** -0.5)
    v = v_ref[...]
    g = g_ref[...].astype(F32)
    tcol = lax.broadcasted_iota(jnp.int32, (chunk, 1), 0).astype(F32)
    for h in range(RET_HEADS):
        log_gamma = math.log1p(-(2.0 ** (-5.0 - h)))
        qh = q[:, h * RET_DK:(h + 1) * RET_DK]
        kh = k[:, h * RET_DK:(h + 1) * RET_DK]
        vh = v[:, h * RET_DV:(h + 1) * RET_DV].astype(BF16)
        scores = _dot_nt(qh.astype(BF16), kh.astype(BF16)) * dec_ref[h]
        state = st_ref[h]
        q_in = qh * jnp.exp(log_gamma * (tcol + 1.0))
        o = (jnp.dot(scores.astype(BF16), vh, preferred_element_type=F32)
             + jnp.dot(q_in.astype(BF16), state.astype(BF16), preferred_element_type=F32))
        k_out = kh * jnp.exp(log_gamma * (chunk - 1.0 - tcol))
        st_ref[h] = math.exp(log_gamma * chunk) * state + _dot_tn(k_out.astype(BF16), vh)
        o = o * lax.rsqrt(jnp.mean(o * o, axis=-1, keepdims=True) + EPS)
        gh = g[:, h * RET_DV:(h + 1) * RET_DV]
        o_ref[:, h * RET_DV:(h + 1) * RET_DV] = (o * _silu(gh)).astype(o_ref.dtype)


def _retention(p3, cos3, sin3):
    b, seq, _ = p3.shape
    c = C_RET
    qk_w = RET_HEADS * RET_DK
    return pl.pallas_call(
        functools.partial(_ret_kernel, chunk=c),
        grid=(b, seq // c),
        in_specs=[pl.BlockSpec((None, c, qk_w), lambda i, j: (i, j, COL_RQ // qk_w)),
                  pl.BlockSpec((None, c, qk_w), lambda i, j: (i, j, COL_RK // qk_w)),
                  pl.BlockSpec((None, c, BRANCH_W), lambda i, j: (i, j, COL_RV // BRANCH_W)),
                  pl.BlockSpec((None, c, BRANCH_W), lambda i, j: (i, j, COL_RG // BRANCH_W)),
                  pl.BlockSpec((None, c, qk_w), lambda i, j: (i, j, 0)),
                  pl.BlockSpec((None, c, qk_w), lambda i, j: (i, j, 0))],
        out_specs=pl.BlockSpec((None, c, BRANCH_W), lambda i, j: (i, j, 0)),
        out_shape=jax.ShapeDtypeStruct((b, seq, BRANCH_W), BF16),
        scratch_shapes=[pltpu.VMEM((RET_HEADS, RET_DK, RET_DV), F32), pltpu.VMEM((RET_HEADS, c, c), F32)],
        compiler_params=_cparams(("parallel", "arbitrary")),
        name="retention",
    )(p3, p3, p3, p3, cos3, sin3)


def _ssd_kernel(z_ref, xs_ref, bc_ref, dt_ref, tri_ref, cw_ref, cb_ref, dtb_ref, alog_ref, dsk_ref, nw_ref,
                o_ref, xe_scr, st_ref, *, chunk):
    j = pl.program_id(1)
    width = 2 * BRANCH_W

    @pl.when(j == 0)
    def _():
        st_ref[...] = jnp.zeros_like(st_ref)
        xe_scr[0:8, :] = jnp.zeros((8, width), F32)

    @pl.when(j > 0)
    def _():
        xe_scr[0:8, :] = xe_scr[chunk:chunk + 8, :]

    xe_scr[8:, 0:BRANCH_W] = xs_ref[...].astype(F32)
    xe_scr[8:, BRANCH_W:] = bc_ref[...].astype(F32)
    conv = cb_ref[...] + cw_ref[M2_CONV - 1:M2_CONV, :] * xe_scr[8:, :]
    for tap in range(M2_CONV - 1):
        conv = conv + cw_ref[tap:tap + 1, :] * xe_scr[pl.ds(8 - (M2_CONV - 1) + tap, chunk), :]
    conv = _silu(conv)
    xs = conv[:, :BRANCH_W]
    bm = conv[:, BRANCH_W:BRANCH_W + M2_GROUPS * M2_STATE]
    cm = conv[:, BRANCH_W + M2_GROUPS * M2_STATE:]

    dt = jax.nn.softplus(dt_ref[...].astype(F32) + dtb_ref[...])
    da = dt * (-jnp.exp(alog_ref[...]))
    da_hi = da.astype(BF16)
    da_r = da - da_hi.astype(F32)
    da_mid = da_r.astype(BF16)
    da_lo = (da_r - da_mid.astype(F32)).astype(BF16)
    tri = tri_ref[...]
    a_cs = (jnp.dot(tri, da_hi, preferred_element_type=F32) + jnp.dot(tri, da_mid, preferred_element_type=F32)
            + jnp.dot(tri, da_lo, preferred_element_type=F32))
    a_cs = a_cs * LOG2_E
    a_cs_t = a_cs.T
    ti = lax.broadcasted_iota(jnp.int32, (chunk, chunk), 0)
    si = lax.broadcasted_iota(jnp.int32, (chunk, chunk), 1)
    causal = ti >= si
    hpg = M2_HEADS // M2_GROUPS
    ys = []
    for grp in range(M2_GROUPS):
        bm_g = bm[:, grp * M2_STATE:(grp + 1) * M2_STATE]
        cm_g = cm[:, grp * M2_STATE:(grp + 1) * M2_STATE]
        cb = _dot_nt(cm_g.astype(BF16), bm_g.astype(BF16))
        for hh in range(hpg):
            h = grp * hpg + hh
            col = a_cs[:, h:h + 1]
            row = a_cs_t[h:h + 1, :]
            lmat = jnp.where(causal, jnp.exp2(col - row), 0.0)
            xd = xs[:, h * M2_HEADDIM:(h + 1) * M2_HEADDIM] * dt[:, h:h + 1]
            state = st_ref[h]
            y = (jnp.dot((cb * lmat).astype(BF16), xd.astype(BF16), preferred_element_type=F32)
                 + jnp.dot((cm_g * jnp.exp2(col)).astype(BF16), state.astype(BF16), preferred_element_type=F32))
            a_last = a_cs[chunk - 1:chunk, h:h + 1]
            to_end = jnp.exp2(a_last - col)
            st_ref[h] = jnp.exp2(a_last) * state + _dot_tn(bm_g.astype(BF16), (xd * to_end).astype(BF16))
            ys.append(y)
    y = jnp.concatenate(ys, axis=-1) + dsk_ref[...] * xs
    y = y * _silu(z_ref[...].astype(F32))
    o_ref[...] = (y * lax.rsqrt(jnp.mean(y * y, axis=-1, keepdims=True) + EPS) * nw_ref[...]).astype(o_ref.dtype)


def _ssd(p3, tri, conv_w, conv_b, dt_bias_row, a_log_row, d_skip_row, norm_w_row):
    b, seq, _ = p3.shape
    c = C_SSD
    const = lambda shape: pl.BlockSpec(shape, lambda i, j: (0,) * len(shape))
    return pl.pallas_call(
        functools.partial(_ssd_kernel, chunk=c),
        grid=(b, seq // c),
        in_specs=[pl.BlockSpec((None, c, BRANCH_W), lambda i, j: (i, j, COL_MZ // BRANCH_W)),
                  pl.BlockSpec((None, c, BRANCH_W), lambda i, j: (i, j, COL_MXS // BRANCH_W)),
                  pl.BlockSpec((None, c, BRANCH_W), lambda i, j: (i, j, COL_MBC // BRANCH_W)),
                  pl.BlockSpec((None, c, LANE), lambda i, j: (i, j, COL_MDT // LANE)),
                  const((c, c)), const((M2_CONV, 2 * BRANCH_W)), const((1, 2 * BRANCH_W)),
                  const((1, LANE)), const((1, LANE)), const((1, BRANCH_W)), const((1, BRANCH_W))],
        out_specs=pl.BlockSpec((None, c, BRANCH_W), lambda i, j: (i, j, 0)),
        out_shape=jax.ShapeDtypeStruct((b, seq, BRANCH_W), BF16),
        scratch_shapes=[pltpu.VMEM((c + 8, 2 * BRANCH_W), F32),
                        pltpu.VMEM((M2_HEADS, M2_STATE, M2_HEADDIM), F32)],
        compiler_params=_cparams(("parallel", "arbitrary")),
        name="ssd",
    )(p3, p3, p3, p3, tri, conv_w, conv_b, dt_bias_row, a_log_row, d_skip_row, norm_w_row)


def _hg_tables(chunk):
    n_lev = int(math.log2(chunk))
    r = np.arange(chunk)[:, None]
    jj = np.arange(chunk)[None, :]
    tri = (jj <= r).astype(np.float32)
    x = r ^ jj
    levmap = np.where(r > jj, np.floor(np.log2(x + 0.5)), np.where(r == jj, -1, -2)).astype(np.int32)
    return tri, levmap, n_lev


def _hg_level_exponent(b, lev):
    rows, width = b.shape
    m = 1 << lev
    sub = 8
    if 2 * m >= sub:
        blocks = b.reshape(rows // (2 * m), 2 * m, width)
        mid = jnp.broadcast_to(blocks[:, m - 1:m, :], blocks.shape).reshape(rows, width)
    else:
        groups = b.reshape(rows // sub, sub, width)
        row_in_group = lax.broadcasted_iota(jnp.int32, groups.shape, 1)
        mid = None
        for start in range(0, sub, 2 * m):
            picked = jnp.broadcast_to(groups[:, start + m - 1:start + m, :], groups.shape)
            mid = picked if mid is None else jnp.where(row_in_group >= start, picked, mid)
        mid = mid.reshape(rows, width)
    return -jnp.abs(b - mid)


def _hg_kernel(q_ref, f_ref, i_ref, g_ref, sum_ref, lev_ref, llb_ref, l1m_ref, nw_ref, o_ref, st_ref,
               *, chunk, n_lev):
    @pl.when(pl.program_id(1) == 0)
    def _():
        st_ref[...] = jnp.zeros_like(st_ref)

    f = f_ref[...].astype(F32)
    log_sig = jnp.minimum(f, 0.0) - jnp.log(1.0 + jnp.exp(-jnp.abs(f)))
    a = llb_ref[...]
    bb = l1m_ref[...] + log_sig
    log_f = jnp.maximum(a, bb) + jnp.log(1.0 + jnp.exp(-jnp.abs(a - bb)))
    k_all = jnp.exp(l1m_ref[...]) * jax.nn.sigmoid(-f)
    q_all = _silu(q_ref[...].astype(F32))
    hi = log_f.astype(BF16)
    r1 = log_f - hi.astype(F32)
    mid = r1.astype(BF16)
    lo = (r1 - mid.astype(F32)).astype(BF16)
    tri = sum_ref[...]
    b_all = (jnp.dot(tri, hi, preferred_element_type=F32)
             + jnp.dot(tri, mid, preferred_element_type=F32)
             + jnp.dot(tri, lo, preferred_element_type=F32))
    b_all = b_all * LOG2_E
    to_end_all = b_all[chunk - 1:chunk, :] - b_all
    level_decay = [jnp.exp2(_hg_level_exponent(b_all, lev)) for lev in range(n_lev)]
    levmap = lev_ref[...]
    on_diag = levmap == -1
    on_level = [levmap == lev for lev in range(n_lev)]
    v_all = i_ref[...]
    g_all = g_ref[...].astype(F32)
    for h in range(HG_HEADS):
        sl = slice(h * HG_DK, (h + 1) * HG_DK)
        qh = q_all[:, sl]
        kh = k_all[:, sl]
        vh = v_all[:, sl].astype(BF16)
        b_h = b_all[:, sl]
        to_end = to_end_all[:, sl]
        amat = jnp.where(on_diag, _dot_nt(qh.astype(BF16), kh.astype(BF16)), 0.0)
        for lev in range(n_lev):
            e = level_decay[lev][:, sl]
            a_l = _dot_nt((qh * e).astype(BF16), (kh * e).astype(BF16))
            amat = jnp.where(on_level[lev], a_l, amat)
        state_t = st_ref[h]
        o = (jnp.dot(amat.astype(BF16), vh, preferred_element_type=F32)
             + _dot_nt((qh * jnp.exp2(b_h)).astype(BF16), state_t.astype(BF16)))
        k_end = kh * jnp.exp2(to_end)
        st_ref[h] = jnp.exp2(b_h[chunk - 1:chunk, :]) * state_t + _dot_tn(vh, k_end.astype(BF16))
        o = o * lax.rsqrt(jnp.mean(o * o, axis=-1, keepdims=True) + EPS) * nw_ref[...]
        o_ref[:, sl] = (o * _silu(g_all[:, sl])).astype(o_ref.dtype)


def _hgrn2(p3, log_lb_row, log1m_lb_row, norm_w_row):
    b, seq, _ = p3.shape
    c = C_HG
    tri, levmap, n_lev = _hg_tables(c)
    const = lambda shape: pl.BlockSpec(shape, lambda i, j: (0,) * len(shape))
    blk = lambda col: pl.BlockSpec((None, c, BRANCH_W), lambda i, j: (i, j, col // BRANCH_W))
    return pl.pallas_call(
        functools.partial(_hg_kernel, chunk=c, n_lev=n_lev),
        grid=(b, seq // c),
        in_specs=[blk(COL_HQ), blk(COL_HF), blk(COL_HI), blk(COL_HG),
                  const((c, c)), const((c, c)),
                  const((1, BRANCH_W)), const((1, BRANCH_W)), const((1, HG_DK))],
        out_specs=pl.BlockSpec((None, c, BRANCH_W), lambda i, j: (i, j, 0)),
        out_shape=jax.ShapeDtypeStruct((b, seq, BRANCH_W), BF16),
        scratch_shapes=[pltpu.VMEM((HG_HEADS, HG_DK, HG_DK), F32)],
        compiler_params=_cparams(("parallel", "arbitrary")),
        name="hgrn2",
    )(p3, p3, p3, p3, jnp.asarray(tri, BF16), jnp.asarray(levmap), log_lb_row, log1m_lb_row, norm_w_row)


def _expand_block_diag(comp_ref, e_ref, dst_ref, row_div, lane_div):
    gq = LANE // S5_CH
    rows, ncols = dst_ref.shape
    step = 512
    comp = comp_ref[...]
    row_grp = (lax.broadcasted_iota(jnp.int32, (rows, step), 0) // row_div) % gq
    for c0 in range(0, ncols, step):
        lane_grp = ((lax.broadcasted_iota(jnp.int32, (rows, step), 1) + c0) // lane_div) % gq
        full = jnp.dot(comp, e_ref[:, c0:c0 + step], preferred_element_type=F32)
        dst_ref[:, c0:c0 + step] = jnp.where(row_grp == lane_grp, full, 0.0).astype(dst_ref.dtype)


def _s5_kernel(u_ref, k2_ref, bc_ref, cc_ref, esc_ref, eb_ref, lam_ref, o_ref, tc_scr, tq_ref, bq_ref, cq_ref,
               x_scr, w_scr, s_scr, *, rows):
    nb = S5_BLOCK

    @pl.when(pl.program_id(1) == 0)
    def _():
        k2 = k2_ref[...]
        lane = lax.broadcasted_iota(jnp.int32, k2.shape, 1)
        for t in range(nb):
            shifted = k2 if t == 0 else jnp.where(lane >= t * S5_CH, pltpu.roll(k2, t * S5_CH, 1), 0.0)
            tc_scr[t * LANE:(t + 1) * LANE, :] = shifted.astype(tc_scr.dtype)
        _expand_block_diag(tc_scr, esc_ref, tq_ref, S5_CH, S5_CH)
        _expand_block_diag(bc_ref, eb_ref, bq_ref, S5_CH, S5_STATE)
        _expand_block_diag(cc_ref, esc_ref, cq_ref, S5_STATE, S5_CH)

    n_seq = u_ref.shape[0]
    for b in range(n_seq):
        for t in range(nb):
            x_scr[b * rows:(b + 1) * rows, t * LANE:(t + 1) * LANE] = (
                u_ref[b, pl.ds(t, rows, stride=nb), :].astype(x_scr.dtype))
    x = x_scr[...]
    half = w_scr.shape[1] // 2
    w_scr[...] = jnp.dot(x, bq_ref[...], preferred_element_type=F32)
    lam_re = lam_ref[0:1, :]
    lam_im = lam_ref[1:2, :]

    def body(j, carry):
        out = []
        for b in range(n_seq):
            s_re, s_im = carry[2 * b], carry[2 * b + 1]
            r = b * rows + j
            s_scr[pl.ds(r, 1), 0:half] = s_re
            s_scr[pl.ds(r, 1), half:] = s_im
            w_re = w_scr[pl.ds(r, 1), 0:half]
            w_im = w_scr[pl.ds(r, 1), half:]
            out += [lam_re * s_re - lam_im * s_im + w_re, lam_re * s_im + lam_im * s_re + w_im]
        return tuple(out)

    zero = jnp.zeros((1, half), F32)
    lax.fori_loop(0, rows, body, (zero,) * (2 * n_seq))
    s_bf = s_scr[...].astype(BF16)
    pair = 2 * LANE
    for c0 in range(0, nb * LANE, pair):
        k_rows = c0 + pair
        y = (jnp.dot(x[:, :k_rows], tq_ref[0:k_rows, c0:c0 + pair], preferred_element_type=F32)
             + jnp.dot(s_bf, cq_ref[:, c0:c0 + pair], preferred_element_type=F32))
        for b in range(n_seq):
            for t in range(c0 // LANE, (c0 + pair) // LANE):
                o_ref[b, pl.ds(t, rows, stride=nb), :] = y[b * rows:(b + 1) * rows, t * LANE - c0:(t + 1) * LANE - c0]


def _s5_scan(p3, k2, bc, cc, lam16):
    batch, seq, _ = p3.shape
    nb = S5_BLOCK
    nq = BRANCH_W // LANE
    rows = seq // nb
    kdim = nb * LANE
    gq = LANE // S5_CH
    ncol = 2 * gq * S5_STATE
    e_sc = (np.eye(nb)[:, None, :, None, None] * np.eye(S5_CH)[None, :, None, None, :] * np.ones((1, 1, 1, gq, 1)))
    e_sc = e_sc.reshape(nb * S5_CH, nb * gq * S5_CH)
    e_b = (np.eye(2)[:, None, :, None, None] * np.eye(S5_STATE)[None, :, None, None, :] * np.ones((1, 1, 1, gq, 1)))
    e_b = e_b.reshape(2 * S5_STATE, ncol)
    full = lambda shape: pl.BlockSpec(shape, lambda q, b: (0,) * len(shape))
    per_q = lambda r, c: pl.BlockSpec((None, r, c), lambda q, b: (q, 0, 0))
    n_seq = S5_SEQ_PER_STEP if batch % S5_SEQ_PER_STEP == 0 else 1
    return pl.pallas_call(
        functools.partial(_s5_kernel, rows=rows),
        grid=(nq, batch // n_seq),
        in_specs=[pl.BlockSpec((n_seq, seq, LANE), lambda q, b: (b, 0, q)),
                  per_q(LANE, nb * S5_CH), per_q(kdim, 2 * S5_STATE), per_q(ncol, nb * S5_CH),
                  full(e_sc.shape), full(e_b.shape), per_q(2, ncol // 2)],
        out_specs=pl.BlockSpec((n_seq, seq, LANE), lambda q, b: (b, 0, q)),
        out_shape=jax.ShapeDtypeStruct((batch, seq, BRANCH_W), F32),
        scratch_shapes=[pltpu.VMEM((kdim, nb * S5_CH), BF16),
                        pltpu.VMEM((kdim, kdim), BF16), pltpu.VMEM((kdim, ncol), BF16), pltpu.VMEM((ncol, kdim), BF16),
                        pltpu.VMEM((n_seq * rows, kdim), BF16), pltpu.VMEM((n_seq * rows, ncol), F32),
                        pltpu.VMEM((n_seq * rows, ncol), F32)],
        compiler_params=_cparams(("parallel", "arbitrary")),
        name="s5_scan",
    )(p3, k2, bc, cc, jnp.asarray(e_sc, BF16), jnp.asarray(e_b, BF16), lam16)


def _s5_operators(lam_re, lam_im, b_re, b_im, c_re, c_im, d_skip, log_dt):
    nb = S5_BLOCK
    gq = LANE // S5_CH
    nq = S5_GROUPS // gq
    lam = lax.complex(jnp.minimum(lam_re.astype(F32), S5_MAX_REAL), lam_im.astype(F32))
    step = jnp.exp(log_dt.astype(F32))[:, None]
    z = lam * step
    lam_bar = jnp.exp(z)
    b_bar = ((lam_bar - 1.0) / lam)[..., None] * lax.complex(b_re.astype(F32), b_im.astype(F32))
    c_mat = lax.complex(c_re.astype(F32), c_im.astype(F32))
    pw = jnp.exp(z[..., None] * jnp.arange(nb + 1, dtype=F32))
    cp = c_mat[:, None, :, :] * pw[..., :nb].transpose(0, 2, 1)[:, :, None, :]
    cp = jnp.concatenate([cp.real, -cp.imag], axis=-1).reshape(S5_GROUPS, nb * S5_CH, 2 * S5_STATE)
    bri = jnp.concatenate([b_bar.real, b_bar.imag], axis=1)
    kern = jnp.einsum('gnk,gki->gin', cp, bri, precision=HIGHEST)
    skip = (jnp.asarray(np.concatenate([np.eye(S5_CH), np.zeros((S5_CH, (nb - 1) * S5_CH))], axis=1), F32)[None]
            * d_skip.astype(F32).reshape(S5_GROUPS, S5_CH, 1))
    k2 = (kern + skip).reshape(nq, gq * S5_CH, nb * S5_CH)
    pw_rev = jnp.exp(z[..., None] * jnp.asarray(np.arange(nb - 1, -1, -1), F32))
    binc = pw_rev[:, :, :, None] * b_bar[:, :, None, :]
    binc = jnp.stack([binc.real, binc.imag], axis=0).reshape(2, nq, gq, S5_STATE, nb, S5_CH)
    bc = binc.transpose(1, 4, 2, 5, 0, 3).reshape(nq, nb * LANE, 2 * S5_STATE)
    cm = c_mat.transpose(0, 2, 1)[:, :, None, :] * pw[..., 1:][:, :, :, None]
    cm = jnp.stack([cm.real, -cm.imag], axis=0).reshape(2, nq, gq * S5_STATE, nb * S5_CH)
    cc = cm.transpose(1, 0, 2, 3).reshape(nq, 2 * gq * S5_STATE, nb * S5_CH)
    lam_n = pw[..., nb].reshape(nq, gq * S5_STATE)
    lam16 = jnp.stack([lam_n.real, lam_n.imag], axis=1)
    return k2, bc.astype(BF16), cc.astype(BF16), lam16


def _merge_kernel(x_ref, sc_ref, sh_ref, gm_ref, ys5_ref, yhg_ref, yret_ref, ym2_ref,
                  wglu_ref, wbr_ref, wg_ref, bg_ref, wout_ref, o_ref):
    x = x_ref[...]
    d = x.shape[1]
    h = _modulated_norm(x, sc_ref[...], sh_ref[...]).astype(BF16)
    y_s5 = jax.nn.gelu(ys5_ref[...])
    y_s5 = y_s5 * jax.nn.sigmoid(jnp.dot(y_s5.astype(BF16), wglu_ref[...], preferred_element_type=F32))
    acc = jnp.zeros(x.shape, F32)
    for n, y in enumerate((y_s5, yhg_ref[...], yret_ref[...], ym2_ref[...])):
        gate = jax.nn.sigmoid(jnp.dot(h, wg_ref[:, n * d:(n + 1) * d], preferred_element_type=F32)
                              + bg_ref[:, n * d:(n + 1) * d])
        acc = acc + gate * jnp.dot(y.astype(BF16), wbr_ref[n], preferred_element_type=F32)
    o_ref[...] = x + gm_ref[...] * jnp.dot(acc.astype(BF16), wout_ref[...], preferred_element_type=F32)


def _merge(x2, mod3, ys5, yhg, yret, ym2, w_glu, w_branch, w_gate, b_gate, w_out, layer, seq):
    t, d = x2.shape
    tm = TM_PROJ
    tpb = seq // tm
    const = lambda shape: pl.BlockSpec((None,) + shape, lambda i: (layer,) + (0,) * len(shape),
                                       pipeline_mode=pl.Buffered(1))
    modspec = lambda k: pl.BlockSpec((None, 1, d), lambda i: ((i // tpb) * 6 + k, 0, 0))
    yspec = pl.BlockSpec((tm, BRANCH_W), lambda i: (i, 0))
    return pl.pallas_call(
        _merge_kernel,
        grid=(t // tm,),
        in_specs=[pl.BlockSpec((tm, d), lambda i: (i, 0)), modspec(1), modspec(0), modspec(2),
                  yspec, yspec, yspec, yspec,
                  const((BRANCH_W, BRANCH_W)), const((4, BRANCH_W, d)), const((d, 4 * d)), const((1, 4 * d)),
                  const((d, d))],
        out_specs=pl.BlockSpec((tm, d), lambda i: (i, 0)),
        out_shape=jax.ShapeDtypeStruct((t, d), F32),
        compiler_params=_cparams(("parallel",)),
        name="merge",
    )(x2, mod3, mod3, mod3, ys5, yhg, yret, ym2, w_glu, w_branch, w_gate, b_gate, w_out)


def _router_kernel(x_ref, sc_ref, sh_ref, wr_ref, br_ref, tri_ref, h_ref, ids_ref, wts_ref, cnt_ref, carry):
    i = pl.program_id(0)

    @pl.when(i == 0)
    def _():
        carry[...] = jnp.zeros_like(carry)

    h = _modulated_norm(x_ref[...], sc_ref[...], sh_ref[...])
    tm, d = h.shape
    packed = _pack_bf16_pairs(h)
    for k in range(N_SLAB):
        h_ref[k] = packed[:, k * SLAB:(k + 1) * SLAB]
    h_hi = h.astype(BF16)
    h_lo = (h - h_hi.astype(F32)).astype(BF16)
    w_r = wr_ref[...]
    w_hi = w_r.astype(BF16)
    w_lo = (w_r - w_hi.astype(F32)).astype(BF16)
    logits = _dot_nt(w_hi, h_hi) + _dot_nt(w_hi, h_lo) + _dot_nt(w_lo, h_hi) + br_ref[:, 0:1]
    gl = [logits[g:g + 1, :] for g in range(MOE_GROUPS)]
    gmax = gl[0]
    gsel = jnp.zeros((1, tm), jnp.int32)
    for g in range(1, MOE_GROUPS):
        better = gl[g] > gmax
        gsel = jnp.where(better, g, gsel)
        gmax = jnp.where(better, gl[g], gmax)
    gden = gl[0] * 0.0
    for g in range(MOE_GROUPS):
        gden = gden + jnp.exp(gl[g] - gmax)
    g_w = 1.0 / gden
    el = []
    for e in range(MOE_EPG):
        v = logits[MOE_GROUPS + e:MOE_GROUPS + e + 1, :]
        for g in range(1, MOE_GROUPS):
            row = MOE_GROUPS + g * MOE_EPG + e
            v = jnp.where(gsel == g, logits[row:row + 1, :], v)
        el.append(v)
    v1 = el[0]
    i1 = jnp.zeros((1, tm), jnp.int32)
    for e in range(1, MOE_EPG):
        better = el[e] > v1
        i1 = jnp.where(better, e, i1)
        v1 = jnp.where(better, el[e], v1)
    v2 = jnp.full((1, tm), -jnp.inf, F32)
    i2 = jnp.zeros((1, tm), jnp.int32)
    for e in range(MOE_EPG):
        better = (el[e] > v2) & (i1 != e)
        i2 = jnp.where(better, e, i2)
        v2 = jnp.where(better, el[e], v2)
    ex = jnp.exp(v2 - v1)
    p1 = 1.0 / (1.0 + ex)
    e1 = gsel * MOE_EPG + i1
    e2 = gsel * MOE_EPG + i2
    erow = lax.broadcasted_iota(jnp.int32, (MOE_EXPERTS, tm), 0)
    oh1 = (erow == e1).astype(F32)
    oh2 = (erow == e2).astype(F32)
    both = oh1 + oh2
    prefix = jnp.dot(both.astype(BF16), tri_ref[...], preferred_element_type=F32) + carry[:, 0:1]
    rank1 = jnp.sum(oh1 * prefix, axis=0, keepdims=True).astype(jnp.int32)
    rank2 = jnp.sum(oh2 * prefix, axis=0, keepdims=True).astype(jnp.int32)
    carry[...] = carry[...] + jnp.sum(both, axis=1, keepdims=True)
    zi = jnp.zeros((1, tm), jnp.int32)
    ids_ref[...] = jnp.concatenate([e1, e2, rank1, rank2, zi, zi, zi, zi], axis=0)
    wrow = lax.broadcasted_iota(jnp.int32, (LANE, tm), 0)
    wts_ref[...] = jnp.where(wrow == 0, p1 * g_w, jnp.where(wrow == 1, ex * p1 * g_w, 0.0)).T
    cnt_ref[...] = carry[...]


def _router(x2, mod3, w_route, b_route, tri_excl, seq):
    t, d = x2.shape
    tm = TM_PROJ
    tpb = seq // tm
    nr = w_route.shape[0]
    const = lambda shape: pl.BlockSpec(shape, lambda i: (0,) * len(shape))
    modspec = lambda k: pl.BlockSpec((None, 1, d), lambda i: ((i // tpb) * 6 + k, 0, 0))
    return pl.pallas_call(
        _router_kernel,
        grid=(t // tm,),
        in_specs=[pl.BlockSpec((tm, d), lambda i: (i, 0)), modspec(4), modspec(3),
                  const((nr, d)), const((nr, LANE)), const((tm, tm))],
        out_specs=[pl.BlockSpec((N_SLAB, tm, SLAB), lambda i: (0, i, 0)),
                   pl.BlockSpec((8, tm), lambda i: (0, i)),
                   pl.BlockSpec((tm, LANE), lambda i: (i, 0)),
                   const((MOE_EXPERTS, LANE))],
        out_shape=[jax.ShapeDtypeStruct((N_SLAB, t, SLAB), jnp.uint32),
                   jax.ShapeDtypeStruct((8, t), jnp.int32),
                   jax.ShapeDtypeStruct((t, LANE), F32),
                   jax.ShapeDtypeStruct((MOE_EXPERTS, LANE), F32)],
        scratch_shapes=[pltpu.VMEM((MOE_EXPERTS, LANE), F32)],
        compiler_params=_cparams(("arbitrary",)),
        name="moe_router",
    )(x2, mod3, mod3, w_route, b_route, tri_excl)


def _sc_mesh():
    return plsc.VectorSubcoreMesh(core_axis_name="core", subcore_axis_name="subcore")


def _slab_rows(idx, n_rows):
    return (idx[None, :] + (jnp.arange(N_SLAB, dtype=jnp.int32) * n_rows)[:, None]).reshape(-1)


def _dispatch(slot1, slot2, h_slabs):
    n_slab, t, d = h_slabs.shape
    n_out = 2 * t
    xs = _scatter_rows(h_slabs.reshape(n_slab * t, d), _slab_rows(slot1, n_out), _slab_rows(slot2, n_out),
                       n_slab * n_out)
    return xs.reshape(n_slab, n_out, d)


def _scatter_rows(src, idx1, idx2, n_out):
    t, d = src.shape
    win = SC_WINDOW

    @pl.kernel(out_type=jax.ShapeDtypeStruct((n_out, d), src.dtype), mesh=_sc_mesh(), name="moe_dispatch_sc")
    def scatter_rows(x_hbm, i1_hbm, i2_hbm, o_hbm):
        def body(x_vmem, i1_vmem, i2_vmem):
            pltpu.sync_copy(x_vmem, o_hbm.at[i1_vmem.at[0]])
            pltpu.sync_copy(x_vmem, o_hbm.at[i2_vmem.at[0]])

        pltpu.emit_pipeline(
            body,
            grid=(t // win,),
            in_specs=[pl.BlockSpec((win, d), lambda i: (i, 0)),
                      pl.BlockSpec((1, win), lambda i: (0, i)),
                      pl.BlockSpec((1, win), lambda i: (0, i))],
            out_specs=[],
            core_axis_name=("core", "subcore"),
            dimension_semantics=(pltpu.PARALLEL,),
        )(x_hbm, i1_hbm, i2_hbm)

    return scatter_rows(src, idx1.reshape(1, t), idx2.reshape(1, t))


def _gather_rows(src, idx):
    m = idx.shape[0]
    d = src.shape[1]
    win = SC_WINDOW

    @pl.kernel(out_type=jax.ShapeDtypeStruct((m, d), src.dtype), mesh=_sc_mesh(), name="moe_gather_sc")
    def gather(x_hbm, i_hbm, o_hbm):
        def body(i_vmem, o_vmem):
            pltpu.sync_copy(x_hbm.at[i_vmem.at[0]], o_vmem)

        pltpu.emit_pipeline(
            body,
            grid=(m // win,),
            in_specs=[pl.BlockSpec((1, win), lambda i: (0, i))],
            out_specs=[pl.BlockSpec((win, d), lambda i: (i, 0))],
            core_axis_name=("core", "subcore"),
            dimension_semantics=(pltpu.PARALLEL,),
        )(i_hbm, o_hbm)

    return gather(src, idx.reshape(1, m))


def _expert_kernel(tile_ref, exp_ref, lo_ref, hi_ref, xs_ref, w1_ref, w3_ref, w2_ref, ys_ref, w1_scr, w3_scr, w2_scr):
    s = pl.program_id(0)
    prev = jnp.maximum(s - 1, 0)
    new_expert = (s == 0) | (exp_ref[s] != exp_ref[prev])
    new_tile = (s == 0) | (tile_ref[s] != tile_ref[prev])

    @pl.when(new_expert)
    def _():
        w1_scr[...] = w1_ref[...].astype(BF16)
        w3_scr[...] = w3_ref[...].astype(BF16)
        w2_scr[...] = w2_ref[...].astype(BF16)

    x = _unpack_bf16_pairs(jnp.concatenate([xs_ref[k] for k in range(N_SLAB)], axis=-1)).astype(BF16)
    a = jnp.dot(x, w1_scr[...], preferred_element_type=F32)
    b = jnp.dot(x, w3_scr[...], preferred_element_type=F32)
    act = _silu(a) * b
    y = _pack_bf16_pairs(jnp.dot(act.astype(BF16), w2_scr[...], preferred_element_type=F32))
    row = lax.broadcasted_iota(jnp.int32, (y.shape[0], SLAB), 0)
    mine = (row >= lo_ref[s]) & (row < hi_ref[s])

    @pl.when(new_tile)
    def _():
        for k in range(N_SLAB):
            ys_ref[k] = jnp.where(mine, y[:, k * SLAB:(k + 1) * SLAB], jnp.uint32(0))

    @pl.when(jnp.logical_not(new_tile))
    def _():
        for k in range(N_SLAB):
            ys_ref[k] = jnp.where(mine, y[:, k * SLAB:(k + 1) * SLAB], ys_ref[k])


def _experts(step_tile, step_expert, step_lo, step_hi, xs, w1, w3, w2, layer):
    n_slab, ns, slab = xs.shape
    d = w1.shape[1]
    ff = w1.shape[2]
    n_steps = step_tile.shape[0]
    base = layer * MOE_EXPERTS
    grid_spec = pltpu.PrefetchScalarGridSpec(
        num_scalar_prefetch=4,
        grid=(n_steps,),
        in_specs=[pl.BlockSpec((n_slab, TM_X, slab), lambda s, tl, ex, lo, hi: (0, tl[s], 0)),
                  pl.BlockSpec((None, d, ff), lambda s, tl, ex, lo, hi: (base + ex[s], 0, 0)),
                  pl.BlockSpec((None, d, ff), lambda s, tl, ex, lo, hi: (base + ex[s], 0, 0)),
                  pl.BlockSpec((None, ff, d), lambda s, tl, ex, lo, hi: (base + ex[s], 0, 0))],
        out_specs=pl.BlockSpec((n_slab, TM_X, slab), lambda s, tl, ex, lo, hi: (0, tl[s], 0)),
        scratch_shapes=[pltpu.VMEM((d, ff), BF16), pltpu.VMEM((d, ff), BF16), pltpu.VMEM((ff, d), BF16)],
    )
    return pl.pallas_call(
        _expert_kernel,
        grid_spec=grid_spec,
        out_shape=jax.ShapeDtypeStruct((n_slab, ns, slab), xs.dtype),
        compiler_params=_cparams(("arbitrary",)),
        name="moe_experts",
    )(step_tile, step_expert, step_lo, step_hi, xs, w1, w3, w2)


def _combine_kernel(x_ref, gate_ref, fw_ref, wcol_ref, y1_ref, y2_ref, o_ref, *, final):
    w_first = wcol_ref[:, 0:1]
    w_second = wcol_ref[:, 1:2]
    y_first = _unpack_bf16_pairs(jnp.concatenate([y1_ref[k] for k in range(N_SLAB)], axis=-1))
    y_second = _unpack_bf16_pairs(jnp.concatenate([y2_ref[k] for k in range(N_SLAB)], axis=-1))
    moe = w_first * y_first + w_second * y_second
    x = x_ref[...] + gate_ref[...] * moe
    if final:
        x = x * lax.rsqrt(jnp.mean(x * x, axis=-1, keepdims=True) + EPS) * fw_ref[...]
    o_ref[...] = x


def _combine(x2, mod3, final_w_row, wcol, gathered, seq, final):
    t, d = x2.shape
    tm = TM_COMB
    tpb = seq // tm
    nblk = t // tm
    yspec = lambda off: pl.BlockSpec((N_SLAB, tm, SLAB), lambda i: (0, i + off, 0))
    return pl.pallas_call(
        functools.partial(_combine_kernel, final=final),
        grid=(nblk,),
        in_specs=[pl.BlockSpec((tm, d), lambda i: (i, 0)),
                  pl.BlockSpec((None, 1, d), lambda i: ((i // tpb) * 6 + 5, 0, 0)),
                  pl.BlockSpec((1, d), lambda i: (0, 0)),
                  pl.BlockSpec((tm, LANE), lambda i: (i, 0)),
                  yspec(0), yspec(nblk)],
        out_specs=pl.BlockSpec((tm, d), lambda i: (i, 0)),
        out_shape=jax.ShapeDtypeStruct((t, d), F32),
        compiler_params=_cparams(("parallel",)),
        name="moe_combine",
    )(x2, mod3, final_w_row, wcol, gathered, gathered)


def _moe(x2, mod3, final_w_row, w_route, b_route, tri_excl, w1, w3, w2, layer, seq, final):
    t, d = x2.shape
    h3, ids, wcol, counts = _router(x2, mod3, w_route, b_route, tri_excl, seq)
    cnt = counts[:, 0].astype(jnp.int32)
    ends = jnp.cumsum(cnt)
    offs = ends - cnt
    experts = jnp.arange(MOE_EXPERTS, dtype=jnp.int32)
    pick = lambda table, idx: jnp.sum(jnp.where(idx[:, None] == experts[None, :], table[None, :], 0), axis=1)
    slot1 = pick(offs, ids[0]) + ids[2]
    slot2 = pick(offs, ids[1]) + ids[3]
    n_tiles = 2 * t // TM_X
    first_tile = offs // TM_X
    n_vis = jnp.where(cnt > 0, (ends - 1) // TM_X - first_tile + 1, 0)
    cum = jnp.cumsum(n_vis)
    step = jnp.arange(n_tiles + MOE_EXPERTS, dtype=jnp.int32)
    step_expert = jnp.minimum(jnp.sum(step[:, None] >= cum[None, :], axis=1), MOE_EXPERTS - 1).astype(jnp.int32)
    valid = step < cum[-1]
    step_tile = jnp.where(valid, pick(first_tile - (cum - n_vis), step_expert) + step, n_tiles - 1)
    step_lo = jnp.where(valid, jnp.clip(pick(offs, step_expert) - step_tile * TM_X, 0, TM_X), 0)
    step_hi = jnp.where(valid, jnp.clip(pick(ends, step_expert) - step_tile * TM_X, 0, TM_X), 0)
    xs = _dispatch(slot1, slot2, h3)
    ys = _experts(step_tile.astype(jnp.int32), step_expert, step_lo.astype(jnp.int32), step_hi.astype(jnp.int32),
                  xs, w1, w3, w2, layer)
    n_sorted = ys.shape[1]
    gathered = _gather_rows(ys.reshape(N_SLAB * n_sorted, SLAB), _slab_rows(jnp.concatenate([slot1, slot2]), n_sorted))
    gathered = gathered.reshape(N_SLAB, n_sorted, SLAB)
    return _combine(x2, mod3, final_w_row, wcol, gathered, seq, final)


def kernel(x, c, positions, ada_w, ada_b, w_in, s5_lam_re, s5_lam_im, s5_b_re, s5_b_im, s5_c_re, s5_c_im, s5_d, s5_log_dt, s5_w_glu, hg_lb_logits, hg_norm_w, m2_conv_w, m2_conv_b, m2_dt_bias, m2_a_log, m2_d, m2_norm_w, w_branch, w_gate, b_gate, w_out, moe_w_group, moe_b_group, moe_w_expert, moe_b_expert, moe_w1, moe_w3, moe_w2, final_norm_w):
    bsz, seq, d = x.shape
    t = bsz * seq
    depth = ada_w.shape[0]
    assert seq % TM_PROJ == 0 and seq % C_RET == 0 and seq % C_SSD == 0 and seq % C_HG == 0
    x2 = x.reshape(t, d).astype(F32)

    c_pad = jnp.zeros((8, d), F32).at[:bsz].set(c.astype(F32))
    mod_all = _ada_mod(c_pad.T, ada_w.astype(F32), ada_b.astype(F32), bsz)

    half = RET_DK // 2
    inv_freq = ROPE_BASE ** (-jnp.arange(half, dtype=F32) / half)
    invf_col = jnp.broadcast_to(inv_freq[:, None], (half, LANE))
    expand = np.tile(np.eye(half, dtype=np.float32), (1, 2 * RET_HEADS))
    sign = np.tile(np.concatenate([-np.ones(half), np.ones(half)]), RET_HEADS)[None, :].astype(np.float32)
    cos_t, sin_t = _rope_tables(positions.reshape(1, t).astype(jnp.int32), invf_col,
                                jnp.asarray(expand, BF16), jnp.asarray(expand * sign, BF16))
    cos3 = cos_t.reshape(bsz, seq, -1)
    sin3 = sin_t.reshape(bsz, seq, -1)

    lb_cum = jnp.cumsum(jax.nn.softmax(hg_lb_logits.astype(F32), axis=0), axis=0)
    hg_lb = lb_cum - lb_cum[:1]
    tri_ssd = jnp.asarray(np.tril(np.ones((C_SSD, C_SSD), np.float32)), BF16)
    tri_excl = jnp.asarray(np.triu(np.ones((TM_PROJ, TM_PROJ), np.float32), 1), BF16)
    final_w_row = final_norm_w.astype(F32)[None, :]
    w_pad = jnp.pad(w_in.astype(BF16), ((0, 0), (0, 0), (0, IN_W_PAD - IN_W)))
    w_glu_bf = s5_w_glu.astype(BF16)
    w_branch_bf = w_branch.astype(BF16)
    w_gate_bf = w_gate.astype(BF16)
    w_out_bf = w_out.astype(BF16)
    b_gate3 = b_gate.astype(F32).reshape(depth, 1, -1)
    moe_w1_all = moe_w1.astype(F32).reshape(depth * MOE_EXPERTS, d, MOE_FF)
    moe_w3_all = moe_w3.astype(F32).reshape(depth * MOE_EXPERTS, d, MOE_FF)
    moe_w2_all = moe_w2.astype(F32).reshape(depth * MOE_EXPERTS, MOE_FF, d)

    for layer in range(depth):
        mod3 = mod_all[layer, :bsz].reshape(bsz * 6, 1, d)
        p, u_s5 = _in_proj(x2, mod3, w_pad, layer, seq)
        p3 = p.reshape(bsz, seq, IN_W_PAD)

        ops = _s5_operators(s5_lam_re[layer], s5_lam_im[layer], s5_b_re[layer], s5_b_im[layer],
                            s5_c_re[layer], s5_c_im[layer], s5_d[layer], s5_log_dt[layer])
        y_s5 = _s5_scan(u_s5.reshape(bsz, seq, BRANCH_W), *ops).reshape(t, BRANCH_W)

        lb = hg_lb[layer][None, :]
        y_hg = _hgrn2(p3, jnp.log(lb), jnp.log1p(-lb), hg_norm_w[layer].astype(F32)[None, :]).reshape(t, BRANCH_W)

        y_ret = _retention(p3, cos3, sin3).reshape(t, BRANCH_W)

        pad8 = lambda v: jnp.zeros((1, LANE), F32).at[0, :M2_HEADS].set(v.astype(F32))
        y_m2 = _ssd(p3, tri_ssd, m2_conv_w[layer].astype(F32), m2_conv_b[layer].astype(F32)[None, :],
                    pad8(m2_dt_bias[layer]), pad8(m2_a_log[layer]),
                    jnp.repeat(m2_d[layer].astype(F32), M2_HEADDIM)[None, :],
                    m2_norm_w[layer].astype(F32)[None, :]).reshape(t, BRANCH_W)

        nr = 40
        w_route = jnp.zeros((nr, d), F32).at[:MOE_GROUPS].set(moe_w_group[layer].astype(F32).T)
        w_route = w_route.at[MOE_GROUPS:MOE_GROUPS + MOE_EXPERTS].set(moe_w_expert[layer].astype(F32).T)
        b_route = jnp.zeros((nr, LANE), F32).at[:MOE_GROUPS, 0].set(moe_b_group[layer].astype(F32))
        b_route = b_route.at[MOE_GROUPS:MOE_GROUPS + MOE_EXPERTS, 0].set(moe_b_expert[layer].astype(F32))
        x2 = _merge(x2, mod3, y_s5, y_hg, y_ret, y_m2, w_glu_bf, w_branch_bf, w_gate_bf, b_gate3, w_out_bf,
                    layer, seq)
        x2 = _moe(x2, mod3, final_w_row, w_route, b_route, tri_excl, moe_w1_all, moe_w3_all, moe_w2_all,
                  layer, seq, final=(layer == depth - 1))
    return x2.reshape(bsz, seq, d)
```

```python
import functools
import math

import numpy as np
import jax
import jax.numpy as jnp
from jax import lax
from jax.experimental import pallas as pl
from jax.experimental.pallas import tpu as pltpu
from jax.experimental.pallas import tpu_sc as plsc

F32 = jnp.float32
BF16 = jnp.bfloat16
HIGHEST = lax.Precision.HIGHEST

D_MODEL = 1024
DEPTH = 2
BRANCH_W = 512
EPS = 1e-6
S5_GROUPS = 32
S5_CH = 16
S5_STATE = 64
S5_MAX_REAL = -1e-4
S5_BLOCK = 16
S5_SEQ_PER_STEP = 2
HG_HEADS = 4
HG_DK = 128
RET_HEADS = 4
RET_DK = 64
RET_DV = 128
ROPE_BASE = 10000.0
M2_HEADS = 8
M2_HEADDIM = 64
M2_GROUPS = 2
M2_STATE = 128
M2_CONV = 4
MOE_GROUPS = 4
MOE_EPG = 8
MOE_EXPERTS = MOE_GROUPS * MOE_EPG
MOE_FF = 256

COL_S5, COL_HQ, COL_HF, COL_HI, COL_HG = 0, 512, 1024, 1536, 2048
COL_RQ, COL_RK, COL_RV, COL_RG = 2560, 2816, 3072, 3584
COL_MZ, COL_MXS, COL_MBC, COL_MDT = 4096, 4608, 5120, 5632
IN_W = 5640
IN_W_PAD = 5888

LANE = 128
VMEM_LIMIT = 56 * 1024 * 1024

TM_PROJ = 1024
TN_PROJ = 1024
TM_INPROJ = 512
LOG2_E = 1.4426950408889634
C_RET = 512
C_SSD = 256
C_HG = 128
TM_X = 512
TM_COMB = 1024
SC_WINDOW = 128
SLAB = 256
N_SLAB = D_MODEL // 2 // SLAB


def _cparams(sem):
    return pltpu.CompilerParams(dimension_semantics=sem, vmem_limit_bytes=VMEM_LIMIT)


def _silu(v):
    return v * jax.nn.sigmoid(v)


def _dot_nt(a, b, **kw):
    return lax.dot_general(a, b, (((1,), (1,)), ((), ())), preferred_element_type=F32, **kw)


def _dot_tn(a, b, **kw):
    return lax.dot_general(a, b, (((0,), (0,)), ((), ())), preferred_element_type=F32, **kw)


def _ada_kernel(ct_ref, w_ref, b_ref, o_ref, *, n_rows):
    cond_t = _silu(ct_ref[...])
    w = w_ref[...]
    rows = [jnp.sum(w * cond_t[:, b:b + 1], axis=0, keepdims=True) for b in range(n_rows)]
    rows += [jnp.zeros_like(rows[0])] * (cond_t.shape[1] - n_rows)
    o_ref[...] = jnp.concatenate(rows, axis=0) + b_ref[...]


def _ada_mod(c_pad_t, ada_w, ada_b, n_rows):
    depth, d, n = ada_w.shape
    tn = 1536
    return pl.pallas_call(
        functools.partial(_ada_kernel, n_rows=n_rows),
        grid=(depth, n // tn),
        in_specs=[pl.BlockSpec((d, 8), lambda l, j: (0, 0)),
                  pl.BlockSpec((None, d, tn), lambda l, j: (l, 0, j)),
                  pl.BlockSpec((None, 1, tn), lambda l, j: (l, 0, j))],
        out_specs=pl.BlockSpec((None, 8, tn), lambda l, j: (l, 0, j)),
        out_shape=jax.ShapeDtypeStruct((depth, 8, n), F32),
        compiler_params=_cparams(("parallel", "parallel")),
        name="ada_mod",
    )(c_pad_t, ada_w, ada_b.reshape(depth, 1, n))


def _pack_bf16_pairs(x):
    n = x.shape[1] // 2
    lo = pltpu.bitcast(x[:, :n].astype(BF16).astype(F32), jnp.uint32) >> 16
    hi = pltpu.bitcast(x[:, n:].astype(BF16).astype(F32), jnp.uint32)
    return hi | lo


def _unpack_bf16_pairs(w):
    lo = pltpu.bitcast(w << 16, F32)
    hi = pltpu.bitcast(w & jnp.uint32(0xFFFF0000), F32)
    return jnp.concatenate([lo, hi], axis=-1)


def _modulated_norm(x, scale, shift):
    ms = jnp.mean(x * x, axis=-1, keepdims=True)
    return x * lax.rsqrt(ms + EPS) * (1.0 + scale) + shift


def _inproj_kernel(x_ref, sc_ref, sh_ref, w_ref, o_ref, u_ref):
    h = _modulated_norm(x_ref[...], sc_ref[...], sh_ref[...]).astype(BF16)
    n_total = o_ref.shape[1]
    for n0 in range(0, n_total, TN_PROJ):
        n1 = min(n0 + TN_PROJ, n_total)
        p = jnp.dot(h, w_ref[:, n0:n1], preferred_element_type=F32)
        o_ref[:, n0:n1] = p.astype(o_ref.dtype)
        if n0 == 0:
            u_ref[...] = p[:, COL_S5:COL_S5 + BRANCH_W]


def _in_proj(x2, mod3, w_pad, layer, seq):
    t, d = x2.shape
    tm = TM_INPROJ
    tpb = seq // tm
    assert COL_S5 + BRANCH_W <= TN_PROJ
    return pl.pallas_call(
        _inproj_kernel,
        grid=(t // tm,),
        in_specs=[pl.BlockSpec((tm, d), lambda i: (i, 0)),
                  pl.BlockSpec((None, 1, d), lambda i: ((i // tpb) * 6 + 1, 0, 0)),
                  pl.BlockSpec((None, 1, d), lambda i: ((i // tpb) * 6 + 0, 0, 0)),
                  pl.BlockSpec((None, d, IN_W_PAD), lambda i: (layer, 0, 0), pipeline_mode=pl.Buffered(1))],
        out_specs=[pl.BlockSpec((tm, IN_W_PAD), lambda i: (i, 0)),
                   pl.BlockSpec((tm, BRANCH_W), lambda i: (i, 0))],
        out_shape=[jax.ShapeDtypeStruct((t, IN_W_PAD), BF16), jax.ShapeDtypeStruct((t, BRANCH_W), F32)],
        compiler_params=_cparams(("parallel",)),
        name="in_proj",
    )(x2, mod3, mod3, w_pad)


def _rope_kernel(pos_ref, invf_ref, ecos_ref, esin_ref, cos_ref, sin_ref):
    ang = invf_ref[:, 0:1] * pos_ref[...].astype(F32)
    def spread(values, e_ref):
        hi = values.astype(BF16)
        rest = values - hi.astype(F32)
        mid = rest.astype(BF16)
        lo = (rest - mid.astype(F32)).astype(BF16)
        e = e_ref[...]
        return _dot_tn(hi, e) + _dot_tn(mid, e) + _dot_tn(lo, e)

    cos_ref[...] = spread(jnp.cos(ang), ecos_ref)
    sin_ref[...] = spread(jnp.sin(ang), esin_ref)


def _rope_tables(pos_row, invf_col, expand_cos, expand_sin):
    t = pos_row.shape[1]
    half, w = expand_cos.shape
    tm = 1024
    const = lambda shape: pl.BlockSpec(shape, lambda i: (0, 0))
    return pl.pallas_call(
        _rope_kernel,
        grid=(t // tm,),
        in_specs=[pl.BlockSpec((1, tm), lambda i: (0, i)), const((half, LANE)), const((half, w)), const((half, w))],
        out_specs=[pl.BlockSpec((tm, w), lambda i: (i, 0))] * 2,
        out_shape=[jax.ShapeDtypeStruct((t, w), F32)] * 2,
        compiler_params=_cparams(("parallel",)),
        name="rope_tables",
    )(pos_row, invf_col, expand_cos, expand_sin)


def _ret_kernel(q_ref, k_ref, v_ref, g_ref, cos_ref, sin_ref, o_ref, st_ref, dec_ref, *, chunk):
    @pl.when(pl.program_id(1) == 0)
    def _():
        st_ref[...] = jnp.zeros_like(st_ref)
        ti = lax.broadcasted_iota(jnp.int32, (chunk, chunk), 0)
        si = lax.broadcasted_iota(jnp.int32, (chunk, chunk), 1)
        lag = (ti - si).astype(F32)
        for h in range(RET_HEADS):
            log_gamma = math.log1p(-(2.0 ** (-5.0 - h)))
            dec_ref[h] = jnp.where(ti >= si, jnp.exp(jnp.minimum(lag * log_gamma, 0.0)), 0.0)

    cosf = cos_ref[...]
    sinf = sin_ref[...]
    width = RET_HEADS * RET_DK
    lane = lax.broadcasted_iota(jnp.int32, (chunk, width), 1)
    first_half = (lane % RET_DK) < (RET_DK // 2)

    def rope(t):
        partner = jnp.where(first_half, pltpu.roll(t, width - RET_DK // 2, 1), pltpu.roll(t, RET_DK // 2, 1))
        return t * cosf + partner * sinf

    q = rope(q_ref[...].astype(F32))
    k = rope(k_ref[...].astype(F32)) * (RET_DK ** -0.5)
    v = v_ref[...]
    g = g_ref[...].astype(F32)
    tcol = lax.broadcasted_iota(jnp.int32, (chunk, 1), 0).astype(F32)
    for h in range(RET_HEADS):
        log_gamma = math.log1p(-(2.0 ** (-5.0 - h)))
        qh = q[:, h * RET_DK:(h + 1) * RET_DK]
        kh = k[:, h * RET_DK:(h + 1) * RET_DK]
        vh = v[:, h * RET_DV:(h + 1) * RET_DV].astype(BF16)
        scores = _dot_nt(qh.astype(BF16), kh.astype(BF16)) * dec_ref[h]
        state = st_ref[h]
        q_in = qh * jnp.exp(log_gamma * (tcol + 1.0))
        o = (jnp.dot(scores.astype(BF16), vh, preferred_element_type=F32)
             + jnp.dot(q_in.astype(BF16), state.astype(BF16), preferred_element_type=F32))
        k_out = kh * jnp.exp(log_gamma * (chunk - 1.0 - tcol))
        st_ref[h] = math.exp(log_gamma * chunk) * state + _dot_tn(k_out.astype(BF16), vh)
        o = o * lax.rsqrt(jnp.mean(o * o, axis=-1, keepdims=True) + EPS)
        gh = g[:, h * RET_DV:(h + 1) * RET_DV]
        o_ref[:, h * RET_DV:(h + 1) * RET_DV] = (o * _silu(gh)).astype(o_ref.dtype)


def _retention(p3, cos3, sin3):
    b, seq, _ = p3.shape
    c = C_RET
    qk_w = RET_HEADS * RET_DK
    return pl.pallas_call(
        functools.partial(_ret_kernel, chunk=c),
        grid=(b, seq // c),
        in_specs=[pl.BlockSpec((None, c, qk_w), lambda i, j: (i, j, COL_RQ // qk_w)),
                  pl.BlockSpec((None, c, qk_w), lambda i, j: (i, j, COL_RK // qk_w)),
                  pl.BlockSpec((None, c, BRANCH_W), lambda i, j: (i, j, COL_RV // BRANCH_W)),
                  pl.BlockSpec((None, c, BRANCH_W), lambda i, j: (i, j, COL_RG // BRANCH_W)),
                  pl.BlockSpec((None, c, qk_w), lambda i, j: (i, j, 0)),
                  pl.BlockSpec((None, c, qk_w), lambda i, j: (i, j, 0))],
        out_specs=pl.BlockSpec((None, c, BRANCH_W), lambda i, j: (i, j, 0)),
        out_shape=jax.ShapeDtypeStruct((b, seq, BRANCH_W), BF16),
        scratch_shapes=[pltpu.VMEM((RET_HEADS, RET_DK, RET_DV), F32), pltpu.VMEM((RET_HEADS, c, c), F32)],
        compiler_params=_cparams(("parallel", "arbitrary")),
        name="retention",
    )(p3, p3, p3, p3, cos3, sin3)


def _ssd_kernel(z_ref, xs_ref, bc_ref, dt_ref, tri_ref, cw_ref, cb_ref, dtb_ref, alog_ref, dsk_ref, nw_ref,
                o_ref, xe_scr, st_ref, *, chunk):
    j = pl.program_id(1)
    width = 2 * BRANCH_W

    @pl.when(j == 0)
    def _():
        st_ref[...] = jnp.zeros_like(st_ref)
        xe_scr[0:8, :] = jnp.zeros((8, width), F32)

    @pl.when(j > 0)
    def _():
        xe_scr[0:8, :] = xe_scr[chunk:chunk + 8, :]

    xe_scr[8:, 0:BRANCH_W] = xs_ref[...].astype(F32)
    xe_scr[8:, BRANCH_W:] = bc_ref[...].astype(F32)
    conv = cb_ref[...] + cw_ref[M2_CONV - 1:M2_CONV, :] * xe_scr[8:, :]
    for tap in range(M2_CONV - 1):
        conv = conv + cw_ref[tap:tap + 1, :] * xe_scr[pl.ds(8 - (M2_CONV - 1) + tap, chunk), :]
    conv = _silu(conv)
    xs = conv[:, :BRANCH_W]
    bm = conv[:, BRANCH_W:BRANCH_W + M2_GROUPS * M2_STATE]
    cm = conv[:, BRANCH_W + M2_GROUPS * M2_STATE:]

    dt = jax.nn.softplus(dt_ref[...].astype(F32) + dtb_ref[...])
    da = dt * (-jnp.exp(alog_ref[...]))
    da_hi = da.astype(BF16)
    da_r = da - da_hi.astype(F32)
    da_mid = da_r.astype(BF16)
    da_lo = (da_r - da_mid.astype(F32)).astype(BF16)
    tri = tri_ref[...]
    a_cs = (jnp.dot(tri, da_hi, preferred_element_type=F32) + jnp.dot(tri, da_mid, preferred_element_type=F32)
            + jnp.dot(tri, da_lo, preferred_element_type=F32))
    a_cs = a_cs * LOG2_E
    a_cs_t = a_cs.T
    ti = lax.broadcasted_iota(jnp.int32, (chunk, chunk), 0)
    si = lax.broadcasted_iota(jnp.int32, (chunk, chunk), 1)
    causal = ti >= si
    hpg = M2_HEADS // M2_GROUPS
    ys = []
    for grp in range(M2_GROUPS):
        bm_g = bm[:, grp * M2_STATE:(grp + 1) * M2_STATE]
        cm_g = cm[:, grp * M2_STATE:(grp + 1) * M2_STATE]
        cb = _dot_nt(cm_g.astype(BF16), bm_g.astype(BF16))
        for hh in range(hpg):
            h = grp * hpg + hh
            col = a_cs[:, h:h + 1]
            row = a_cs_t[h:h + 1, :]
            lmat = jnp.where(causal, jnp.exp2(col - row), 0.0)
            xd = xs[:, h * M2_HEADDIM:(h + 1) * M2_HEADDIM] * dt[:, h:h + 1]
            state = st_ref[h]
            y = (jnp.dot((cb * lmat).astype(BF16), xd.astype(BF16), preferred_element_type=F32)
                 + jnp.dot((cm_g * jnp.exp2(col)).astype(BF16), state.astype(BF16), preferred_element_type=F32))
            a_last = a_cs[chunk - 1:chunk, h:h + 1]
            to_end = jnp.exp2(a_last - col)
            st_ref[h] = jnp.exp2(a_last) * state + _dot_tn(bm_g.astype(BF16), (xd * to_end).astype(BF16))
            ys.append(y)
    y = jnp.concatenate(ys, axis=-1) + dsk_ref[...] * xs
    y = y * _silu(z_ref[...].astype(F32))
    o_ref[...] = (y * lax.rsqrt(jnp.mean(y * y, axis=-1, keepdims=True) + EPS) * nw_ref[...]).astype(o_ref.dtype)


def _ssd(p3, tri, conv_w, conv_b, dt_bias_row, a_log_row, d_skip_row, norm_w_row):
    b, seq, _ = p3.shape
    c = C_SSD
    const = lambda shape: pl.BlockSpec(shape, lambda i, j: (0,) * len(shape))
    return pl.pallas_call(
        functools.partial(_ssd_kernel, chunk=c),
        grid=(b, seq // c),
        in_specs=[pl.BlockSpec((None, c, BRANCH_W), lambda i, j: (i, j, COL_MZ // BRANCH_W)),
                  pl.BlockSpec((None, c, BRANCH_W), lambda i, j: (i, j, COL_MXS // BRANCH_W)),
                  pl.BlockSpec((None, c, BRANCH_W), lambda i, j: (i, j, COL_MBC // BRANCH_W)),
                  pl.BlockSpec((None, c, LANE), lambda i, j: (i, j, COL_MDT // LANE)),
                  const((c, c)), const((M2_CONV, 2 * BRANCH_W)), const((1, 2 * BRANCH_W)),
                  const((1, LANE)), const((1, LANE)), const((1, BRANCH_W)), const((1, BRANCH_W))],
        out_specs=pl.BlockSpec((None, c, BRANCH_W), lambda i, j: (i, j, 0)),
        out_shape=jax.ShapeDtypeStruct((b, seq, BRANCH_W), BF16),
        scratch_shapes=[pltpu.VMEM((c + 8, 2 * BRANCH_W), F32),
                        pltpu.VMEM((M2_HEADS, M2_STATE, M2_HEADDIM), F32)],
        compiler_params=_cparams(("parallel", "arbitrary")),
        name="ssd",
    )(p3, p3, p3, p3, tri, conv_w, conv_b, dt_bias_row, a_log_row, d_skip_row, norm_w_row)


def _hg_tables(chunk):
    n_lev = int(math.log2(chunk))
    r = np.arange(chunk)[:, None]
    jj = np.arange(chunk)[None, :]
    tri = (jj <= r).astype(np.float32)
    x = r ^ jj
    levmap = np.where(r > jj, np.floor(np.log2(x + 0.5)), np.where(r == jj, -1, -2)).astype(np.int32)
    return tri, levmap, n_lev


def _hg_level_exponent(b, lev):
    rows, width = b.shape
    m = 1 << lev
    sub = 8
    if 2 * m >= sub:
        blocks = b.reshape(rows // (2 * m), 2 * m, width)
        mid = jnp.broadcast_to(blocks[:, m - 1:m, :], blocks.shape).reshape(rows, width)
    else:
        groups = b.reshape(rows // sub, sub, width)
        row_in_group = lax.broadcasted_iota(jnp.int32, groups.shape, 1)
        mid = None
        for start in range(0, sub, 2 * m):
            picked = jnp.broadcast_to(groups[:, start + m - 1:start + m, :], groups.shape)
            mid = picked if mid is None else jnp.where(row_in_group >= start, picked, mid)
        mid = mid.reshape(rows, width)
    return -jnp.abs(b - mid)


def _hg_kernel(q_ref, f_ref, i_ref, g_ref, sum_ref, lev_ref, llb_ref, l1m_ref, nw_ref, o_ref, st_ref,
               *, chunk, n_lev):
    @pl.when(pl.program_id(1) == 0)
    def _():
        st_ref[...] = jnp.zeros_like(st_ref)

    f = f_ref[...].astype(F32)
    y = jnp.exp(-jnp.abs(f))
    one_plus_y = 1.0 + y
    log_sig = jnp.minimum(f, 0.0) - jnp.log(one_plus_y)
    a = llb_ref[...]
    bb = l1m_ref[...] + log_sig
    log_f = jnp.maximum(a, bb) + jnp.log(1.0 + jnp.exp(-jnp.abs(a - bb)))
    k_all = jnp.exp(l1m_ref[...]) * (jnp.where(f >= 0.0, y, 1.0) / one_plus_y)
    q_all = _silu(q_ref[...].astype(F32))
    hi = log_f.astype(BF16)
    r1 = log_f - hi.astype(F32)
    mid = r1.astype(BF16)
    lo = (r1 - mid.astype(F32)).astype(BF16)
    tri = sum_ref[...]
    b_all = (jnp.dot(tri, hi, preferred_element_type=F32)
             + jnp.dot(tri, mid, preferred_element_type=F32)
             + jnp.dot(tri, lo, preferred_element_type=F32))
    b_all = b_all * LOG2_E
    to_end_all = b_all[chunk - 1:chunk, :] - b_all
    level_decay = [jnp.exp2(_hg_level_exponent(b_all, lev)) for lev in range(n_lev)]
    levmap = lev_ref[...]
    on_diag = levmap == -1
    on_level = [levmap == lev for lev in range(n_lev)]
    v_all = i_ref[...]
    g_all = g_ref[...].astype(F32)
    for h in range(HG_HEADS):
        sl = slice(h * HG_DK, (h + 1) * HG_DK)
        qh = q_all[:, sl]
        kh = k_all[:, sl]
        vh = v_all[:, sl].astype(BF16)
        b_h = b_all[:, sl]
        to_end = to_end_all[:, sl]
        amat = jnp.where(on_diag, _dot_nt(qh.astype(BF16), kh.astype(BF16)), 0.0)
        for lev in range(n_lev):
            e = level_decay[lev][:, sl]
            a_l = _dot_nt((qh * e).astype(BF16), (kh * e).astype(BF16))
            amat = jnp.where(on_level[lev], a_l, amat)
        state_t = st_ref[h]
        o = (jnp.dot(amat.astype(BF16), vh, preferred_element_type=F32)
             + _dot_nt((qh * jnp.exp2(b_h)).astype(BF16), state_t.astype(BF16)))
        k_end = kh * jnp.exp2(to_end)
        st_ref[h] = jnp.exp2(b_h[chunk - 1:chunk, :]) * state_t + _dot_tn(vh, k_end.astype(BF16))
        o = o * lax.rsqrt(jnp.mean(o * o, axis=-1, keepdims=True) + EPS) * nw_ref[...]
        o_ref[:, sl] = (o * _silu(g_all[:, sl])).astype(o_ref.dtype)


def _hgrn2(p3, log_lb_row, log1m_lb_row, norm_w_row):
    b, seq, _ = p3.shape
    c = C_HG
    tri, levmap, n_lev = _hg_tables(c)
    const = lambda shape: pl.BlockSpec(shape, lambda i, j: (0,) * len(shape))
    blk = lambda col: pl.BlockSpec((None, c, BRANCH_W), lambda i, j: (i, j, col // BRANCH_W))
    return pl.pallas_call(
        functools.partial(_hg_kernel, chunk=c, n_lev=n_lev),
        grid=(b, seq // c),
        in_specs=[blk(COL_HQ), blk(COL_HF), blk(COL_HI), blk(COL_HG),
                  const((c, c)), const((c, c)),
                  const((1, BRANCH_W)), const((1, BRANCH_W)), const((1, HG_DK))],
        out_specs=pl.BlockSpec((None, c, BRANCH_W), lambda i, j: (i, j, 0)),
        out_shape=jax.ShapeDtypeStruct((b, seq, BRANCH_W), BF16),
        scratch_shapes=[pltpu.VMEM((HG_HEADS, HG_DK, HG_DK), F32)],
        compiler_params=_cparams(("parallel", "arbitrary")),
        name="hgrn2",
    )(p3, p3, p3, p3, jnp.asarray(tri, BF16), jnp.asarray(levmap), log_lb_row, log1m_lb_row, norm_w_row)


def _expand_block_diag(comp_ref, e_ref, dst_ref, row_div, lane_div):
    gq = LANE // S5_CH
    rows, ncols = dst_ref.shape
    step = 512
    comp = comp_ref[...]
    row_grp = (lax.broadcasted_iota(jnp.int32, (rows, step), 0) // row_div) % gq
    for c0 in range(0, ncols, step):
        lane_grp = ((lax.broadcasted_iota(jnp.int32, (rows, step), 1) + c0) // lane_div) % gq
        full = jnp.dot(comp, e_ref[:, c0:c0 + step], preferred_element_type=F32)
        dst_ref[:, c0:c0 + step] = jnp.where(row_grp == lane_grp, full, 0.0).astype(dst_ref.dtype)


def _s5_kernel(u_ref, k2_ref, bc_ref, cc_ref, esc_ref, eb_ref, lam_ref, o_ref, tc_scr, tq_ref, bq_ref, cq_ref,
               x_scr, w_scr, s_scr, *, rows):
    nb = S5_BLOCK

    @pl.when(pl.program_id(1) == 0)
    def _():
        k2 = k2_ref[...]
        lane = lax.broadcasted_iota(jnp.int32, k2.shape, 1)
        for t in range(nb):
            shifted = k2 if t == 0 else jnp.where(lane >= t * S5_CH, pltpu.roll(k2, t * S5_CH, 1), 0.0)
            tc_scr[t * LANE:(t + 1) * LANE, :] = shifted.astype(tc_scr.dtype)
        _expand_block_diag(tc_scr, esc_ref, tq_ref, S5_CH, S5_CH)
        _expand_block_diag(bc_ref, eb_ref, bq_ref, S5_CH, S5_STATE)
        _expand_block_diag(cc_ref, esc_ref, cq_ref, S5_STATE, S5_CH)

    n_seq = u_ref.shape[0]
    for b in range(n_seq):
        for t in range(nb):
            x_scr[b * rows:(b + 1) * rows, t * LANE:(t + 1) * LANE] = (
                u_ref[b, pl.ds(t, rows, stride=nb), :].astype(x_scr.dtype))
    x = x_scr[...]
    half = w_scr.shape[1] // 2
    w_scr[...] = jnp.dot(x, bq_ref[...], preferred_element_type=F32)
    lam_re = lam_ref[0:1, :]
    lam_im = lam_ref[1:2, :]

    def body(j, carry):
        out = []
        for b in range(n_seq):
            s_re, s_im = carry[2 * b], carry[2 * b + 1]
            r = b * rows + j
            s_scr[pl.ds(r, 1), 0:half] = s_re
            s_scr[pl.ds(r, 1), half:] = s_im
            w_re = w_scr[pl.ds(r, 1), 0:half]
            w_im = w_scr[pl.ds(r, 1), half:]
            out += [lam_re * s_re - lam_im * s_im + w_re, lam_re * s_im + lam_im * s_re + w_im]
        return tuple(out)

    zero = jnp.zeros((1, half), F32)
    lax.fori_loop(0, rows, body, (zero,) * (2 * n_seq))
    s_bf = s_scr[...].astype(BF16)
    pair = 2 * LANE
    for c0 in range(0, nb * LANE, pair):
        k_rows = c0 + pair
        y = (jnp.dot(x[:, :k_rows], tq_ref[0:k_rows, c0:c0 + pair], preferred_element_type=F32)
             + jnp.dot(s_bf, cq_ref[:, c0:c0 + pair], preferred_element_type=F32))
        for b in range(n_seq):
            for t in range(c0 // LANE, (c0 + pair) // LANE):
                o_ref[b, pl.ds(t, rows, stride=nb), :] = y[b * rows:(b + 1) * rows, t * LANE - c0:(t + 1) * LANE - c0]


def _s5_scan(p3, k2, bc, cc, lam16):
    batch, seq, _ = p3.shape
    nb = S5_BLOCK
    nq = BRANCH_W // LANE
    rows = seq // nb
    kdim = nb * LANE
    gq = LANE // S5_CH
    ncol = 2 * gq * S5_STATE
    e_sc = (np.eye(nb)[:, None, :, None, None] * np.eye(S5_CH)[None, :, None, None, :] * np.ones((1, 1, 1, gq, 1)))
    e_sc = e_sc.reshape(nb * S5_CH, nb * gq * S5_CH)
    e_b = (np.eye(2)[:, None, :, None, None] * np.eye(S5_STATE)[None, :, None, None, :] * np.ones((1, 1, 1, gq, 1)))
    e_b = e_b.reshape(2 * S5_STATE, ncol)
    full = lambda shape: pl.BlockSpec(shape, lambda q, b: (0,) * len(shape))
    per_q = lambda r, c: pl.BlockSpec((None, r, c), lambda q, b: (q, 0, 0))
    n_seq = S5_SEQ_PER_STEP if batch % S5_SEQ_PER_STEP == 0 else 1
    return pl.pallas_call(
        functools.partial(_s5_kernel, rows=rows),
        grid=(nq, batch // n_seq),
        in_specs=[pl.BlockSpec((n_seq, seq, LANE), lambda q, b: (b, 0, q)),
                  per_q(LANE, nb * S5_CH), per_q(kdim, 2 * S5_STATE), per_q(ncol, nb * S5_CH),
                  full(e_sc.shape), full(e_b.shape), per_q(2, ncol // 2)],
        out_specs=pl.BlockSpec((n_seq, seq, LANE), lambda q, b: (b, 0, q)),
        out_shape=jax.ShapeDtypeStruct((batch, seq, BRANCH_W), F32),
        scratch_shapes=[pltpu.VMEM((kdim, nb * S5_CH), BF16),
                        pltpu.VMEM((kdim, kdim), BF16), pltpu.VMEM((kdim, ncol), BF16), pltpu.VMEM((ncol, kdim), BF16),
                        pltpu.VMEM((n_seq * rows, kdim), BF16), pltpu.VMEM((n_seq * rows, ncol), F32),
                        pltpu.VMEM((n_seq * rows, ncol), F32)],
        compiler_params=_cparams(("parallel", "arbitrary")),
        name="s5_scan",
    )(p3, k2, bc, cc, jnp.asarray(e_sc, BF16), jnp.asarray(e_b, BF16), lam16)


def _s5_operators(lam_re, lam_im, b_re, b_im, c_re, c_im, d_skip, log_dt):
    nb = S5_BLOCK
    gq = LANE // S5_CH
    nq = S5_GROUPS // gq
    lam = lax.complex(jnp.minimum(lam_re.astype(F32), S5_MAX_REAL), lam_im.astype(F32))
    step = jnp.exp(log_dt.astype(F32))[:, None]
    z = lam * step
    lam_bar = jnp.exp(z)
    b_bar = ((lam_bar - 1.0) / lam)[..., None] * lax.complex(b_re.astype(F32), b_im.astype(F32))
    c_mat = lax.complex(c_re.astype(F32), c_im.astype(F32))
    pw = jnp.exp(z[..., None] * jnp.arange(nb + 1, dtype=F32))
    cp = c_mat[:, None, :, :] * pw[..., :nb].transpose(0, 2, 1)[:, :, None, :]
    cp = jnp.concatenate([cp.real, -cp.imag], axis=-1).reshape(S5_GROUPS, nb * S5_CH, 2 * S5_STATE)
    bri = jnp.concatenate([b_bar.real, b_bar.imag], axis=1)
    kern = jnp.einsum('gnk,gki->gin', cp, bri, precision=HIGHEST)
    skip = (jnp.asarray(np.concatenate([np.eye(S5_CH), np.zeros((S5_CH, (nb - 1) * S5_CH))], axis=1), F32)[None]
            * d_skip.astype(F32).reshape(S5_GROUPS, S5_CH, 1))
    k2 = (kern + skip).reshape(nq, gq * S5_CH, nb * S5_CH)
    pw_rev = jnp.exp(z[..., None] * jnp.asarray(np.arange(nb - 1, -1, -1), F32))
    binc = pw_rev[:, :, :, None] * b_bar[:, :, None, :]
    binc = jnp.stack([binc.real, binc.imag], axis=0).reshape(2, nq, gq, S5_STATE, nb, S5_CH)
    bc = binc.transpose(1, 4, 2, 5, 0, 3).reshape(nq, nb * LANE, 2 * S5_STATE)
    cm = c_mat.transpose(0, 2, 1)[:, :, None, :] * pw[..., 1:][:, :, :, None]
    cm = jnp.stack([cm.real, -cm.imag], axis=0).reshape(2, nq, gq * S5_STATE, nb * S5_CH)
    cc = cm.transpose(1, 0, 2, 3).reshape(nq, 2 * gq * S5_STATE, nb * S5_CH)
    lam_n = pw[..., nb].reshape(nq, gq * S5_STATE)
    lam16 = jnp.stack([lam_n.real, lam_n.imag], axis=1)
    return k2, bc.astype(BF16), cc.astype(BF16), lam16


def _merge_kernel(x_ref, sc_ref, sh_ref, gm_ref, ys5_ref, yhg_ref, yret_ref, ym2_ref,
                  wglu_ref, wbr_ref, wg_ref, bg_ref, wout_ref, o_ref):
    x = x_ref[...]
    d = x.shape[1]
    h = _modulated_norm(x, sc_ref[...], sh_ref[...]).astype(BF16)
    y_s5 = jax.nn.gelu(ys5_ref[...])
    y_s5 = y_s5 * jax.nn.sigmoid(jnp.dot(y_s5.astype(BF16), wglu_ref[...], preferred_element_type=F32))
    acc = jnp.zeros(x.shape, F32)
    for n, y in enumerate((y_s5, yhg_ref[...], yret_ref[...], ym2_ref[...])):
        gate = jax.nn.sigmoid(jnp.dot(h, wg_ref[:, n * d:(n + 1) * d], preferred_element_type=F32)
                              + bg_ref[:, n * d:(n + 1) * d])
        acc = acc + gate * jnp.dot(y.astype(BF16), wbr_ref[n], preferred_element_type=F32)
    o_ref[...] = x + gm_ref[...] * jnp.dot(acc.astype(BF16), wout_ref[...], preferred_element_type=F32)


def _merge(x2, mod3, ys5, yhg, yret, ym2, w_glu, w_branch, w_gate, b_gate, w_out, layer, seq):
    t, d = x2.shape
    tm = TM_PROJ
    tpb = seq // tm
    const = lambda shape: pl.BlockSpec((None,) + shape, lambda i: (layer,) + (0,) * len(shape),
                                       pipeline_mode=pl.Buffered(1))
    modspec = lambda k: pl.BlockSpec((None, 1, d), lambda i: ((i // tpb) * 6 + k, 0, 0))
    yspec = pl.BlockSpec((tm, BRANCH_W), lambda i: (i, 0))
    return pl.pallas_call(
        _merge_kernel,
        grid=(t // tm,),
        in_specs=[pl.BlockSpec((tm, d), lambda i: (i, 0)), modspec(1), modspec(0), modspec(2),
                  yspec, yspec, yspec, yspec,
                  const((BRANCH_W, BRANCH_W)), const((4, BRANCH_W, d)), const((d, 4 * d)), const((1, 4 * d)),
                  const((d, d))],
        out_specs=pl.BlockSpec((tm, d), lambda i: (i, 0)),
        out_shape=jax.ShapeDtypeStruct((t, d), F32),
        compiler_params=_cparams(("parallel",)),
        name="merge",
    )(x2, mod3, mod3, mod3, ys5, yhg, yret, ym2, w_glu, w_branch, w_gate, b_gate, w_out)


def _router_kernel(x_ref, sc_ref, sh_ref, wr_ref, br_ref, tri_ref, h_ref, ids_ref, wts_ref, cnt_ref, carry):
    i = pl.program_id(0)

    @pl.when(i == 0)
    def _():
        carry[...] = jnp.zeros_like(carry)

    h = _modulated_norm(x_ref[...], sc_ref[...], sh_ref[...])
    tm, d = h.shape
    packed = _pack_bf16_pairs(h)
    for k in range(N_SLAB):
        h_ref[k] = packed[:, k * SLAB:(k + 1) * SLAB]
    h_hi = h.astype(BF16)
    h_lo = (h - h_hi.astype(F32)).astype(BF16)
    w_r = wr_ref[...]
    w_hi = w_r.astype(BF16)
    w_lo = (w_r - w_hi.astype(F32)).astype(BF16)
    logits = _dot_nt(w_hi, h_hi) + _dot_nt(w_hi, h_lo) + _dot_nt(w_lo, h_hi) + br_ref[:, 0:1]
    gl = [logits[g:g + 1, :] for g in range(MOE_GROUPS)]
    gmax = gl[0]
    gsel = jnp.zeros((1, tm), jnp.int32)
    for g in range(1, MOE_GROUPS):
        better = gl[g] > gmax
        gsel = jnp.where(better, g, gsel)
        gmax = jnp.where(better, gl[g], gmax)
    gden = gl[0] * 0.0
    for g in range(MOE_GROUPS):
        gden = gden + jnp.exp(gl[g] - gmax)
    g_w = 1.0 / gden
    el = []
    for e in range(MOE_EPG):
        v = logits[MOE_GROUPS + e:MOE_GROUPS + e + 1, :]
        for g in range(1, MOE_GROUPS):
            row = MOE_GROUPS + g * MOE_EPG + e
            v = jnp.where(gsel == g, logits[row:row + 1, :], v)
        el.append(v)
    v1 = el[0]
    i1 = jnp.zeros((1, tm), jnp.int32)
    for e in range(1, MOE_EPG):
        better = el[e] > v1
        i1 = jnp.where(better, e, i1)
        v1 = jnp.where(better, el[e], v1)
    v2 = jnp.full((1, tm), -jnp.inf, F32)
    i2 = jnp.zeros((1, tm), jnp.int32)
    for e in range(MOE_EPG):
        better = (el[e] > v2) & (i1 != e)
        i2 = jnp.where(better, e, i2)
        v2 = jnp.where(better, el[e], v2)
    ex = jnp.exp(v2 - v1)
    p1 = 1.0 / (1.0 + ex)
    e1 = gsel * MOE_EPG + i1
    e2 = gsel * MOE_EPG + i2
    erow = lax.broadcasted_iota(jnp.int32, (MOE_EXPERTS, tm), 0)
    oh1 = (erow == e1).astype(F32)
    oh2 = (erow == e2).astype(F32)
    both = oh1 + oh2
    prefix = jnp.dot(both.astype(BF16), tri_ref[...], preferred_element_type=F32) + carry[:, 0:1]
    rank1 = jnp.sum(oh1 * prefix, axis=0, keepdims=True).astype(jnp.int32)
    rank2 = jnp.sum(oh2 * prefix, axis=0, keepdims=True).astype(jnp.int32)
    carry[...] = carry[...] + jnp.sum(both, axis=1, keepdims=True)
    zi = jnp.zeros((1, tm), jnp.int32)
    ids_ref[...] = jnp.concatenate([e1, e2, rank1, rank2, zi, zi, zi, zi], axis=0)
    wrow = lax.broadcasted_iota(jnp.int32, (LANE, tm), 0)
    wts_ref[...] = jnp.where(wrow == 0, p1 * g_w, jnp.where(wrow == 1, ex * p1 * g_w, 0.0)).T
    cnt_ref[...] = carry[...]


def _router(x2, mod3, w_route, b_route, tri_excl, seq):
    t, d = x2.shape
    tm = TM_PROJ
    tpb = seq // tm
    nr = w_route.shape[0]
    const = lambda shape: pl.BlockSpec(shape, lambda i: (0,) * len(shape))
    modspec = lambda k: pl.BlockSpec((None, 1, d), lambda i: ((i // tpb) * 6 + k, 0, 0))
    return pl.pallas_call(
        _router_kernel,
        grid=(t // tm,),
        in_specs=[pl.BlockSpec((tm, d), lambda i: (i, 0)), modspec(4), modspec(3),
                  const((nr, d)), const((nr, LANE)), const((tm, tm))],
        out_specs=[pl.BlockSpec((N_SLAB, tm, SLAB), lambda i: (0, i, 0)),
                   pl.BlockSpec((8, tm), lambda i: (0, i)),
                   pl.BlockSpec((tm, LANE), lambda i: (i, 0)),
                   const((MOE_EXPERTS, LANE))],
        out_shape=[jax.ShapeDtypeStruct((N_SLAB, t, SLAB), jnp.uint32),
                   jax.ShapeDtypeStruct((8, t), jnp.int32),
                   jax.ShapeDtypeStruct((t, LANE), F32),
                   jax.ShapeDtypeStruct((MOE_EXPERTS, LANE), F32)],
        scratch_shapes=[pltpu.VMEM((MOE_EXPERTS, LANE), F32)],
        compiler_params=_cparams(("arbitrary",)),
        name="moe_router",
    )(x2, mod3, mod3, w_route, b_route, tri_excl)


def _sc_mesh():
    return plsc.VectorSubcoreMesh(core_axis_name="core", subcore_axis_name="subcore")


def _slab_rows(idx, n_rows):
    return (idx[None, :] + (jnp.arange(N_SLAB, dtype=jnp.int32) * n_rows)[:, None]).reshape(-1)


def _dispatch(slot1, slot2, h_slabs):
    n_slab, t, d = h_slabs.shape
    n_out = 2 * t
    xs = _scatter_rows(h_slabs.reshape(n_slab * t, d), _slab_rows(slot1, n_out), _slab_rows(slot2, n_out),
                       n_slab * n_out)
    return xs.reshape(n_slab, n_out, d)


def _scatter_rows(src, idx1, idx2, n_out):
    t, d = src.shape
    win = SC_WINDOW

    @pl.kernel(out_type=jax.ShapeDtypeStruct((n_out, d), src.dtype), mesh=_sc_mesh(), name="moe_dispatch_sc")
    def scatter_rows(x_hbm, i1_hbm, i2_hbm, o_hbm):
        def body(x_vmem, i1_vmem, i2_vmem):
            pltpu.sync_copy(x_vmem, o_hbm.at[i1_vmem.at[0]])
            pltpu.sync_copy(x_vmem, o_hbm.at[i2_vmem.at[0]])

        pltpu.emit_pipeline(
            body,
            grid=(t // win,),
            in_specs=[pl.BlockSpec((win, d), lambda i: (i, 0)),
                      pl.BlockSpec((1, win), lambda i: (0, i)),
                      pl.BlockSpec((1, win), lambda i: (0, i))],
            out_specs=[],
            core_axis_name=("core", "subcore"),
            dimension_semantics=(pltpu.PARALLEL,),
        )(x_hbm, i1_hbm, i2_hbm)

    return scatter_rows(src, idx1.reshape(1, t), idx2.reshape(1, t))


def _gather_rows(src, idx):
    m = idx.shape[0]
    d = src.shape[1]
    win = SC_WINDOW

    @pl.kernel(out_type=jax.ShapeDtypeStruct((m, d), src.dtype), mesh=_sc_mesh(), name="moe_gather_sc")
    def gather(x_hbm, i_hbm, o_hbm):
        def body(i_vmem, o_vmem):
            pltpu.sync_copy(x_hbm.at[i_vmem.at[0]], o_vmem)

        pltpu.emit_pipeline(
            body,
            grid=(m // win,),
            in_specs=[pl.BlockSpec((1, win), lambda i: (0, i))],
            out_specs=[pl.BlockSpec((win, d), lambda i: (i, 0))],
            core_axis_name=("core", "subcore"),
            dimension_semantics=(pltpu.PARALLEL,),
        )(i_hbm, o_hbm)

    return gather(src, idx.reshape(1, m))


def _expert_kernel(tile_ref, exp_ref, lo_ref, hi_ref, xs_ref, w1_ref, w3_ref, w2_ref, ys_ref, w1_scr, w3_scr, w2_scr):
    s = pl.program_id(0)
    prev = jnp.maximum(s - 1, 0)
    new_expert = (s == 0) | (exp_ref[s] != exp_ref[prev])
    new_tile = (s == 0) | (tile_ref[s] != tile_ref[prev])

    @pl.when(new_expert)
    def _():
        w1_scr[...] = w1_ref[...].astype(BF16)
        w3_scr[...] = w3_ref[...].astype(BF16)
        w2_scr[...] = w2_ref[...].astype(BF16)

    x = _unpack_bf16_pairs(jnp.concatenate([xs_ref[k] for k in range(N_SLAB)], axis=-1)).astype(BF16)
    a = jnp.dot(x, w1_scr[...], preferred_element_type=F32)
    b = jnp.dot(x, w3_scr[...], preferred_element_type=F32)
    act = _silu(a) * b
    y = _pack_bf16_pairs(jnp.dot(act.astype(BF16), w2_scr[...], preferred_element_type=F32))
    row = lax.broadcasted_iota(jnp.int32, (y.shape[0], SLAB), 0)
    mine = (row >= lo_ref[s]) & (row < hi_ref[s])

    @pl.when(new_tile)
    def _():
        for k in range(N_SLAB):
            ys_ref[k] = jnp.where(mine, y[:, k * SLAB:(k + 1) * SLAB], jnp.uint32(0))

    @pl.when(jnp.logical_not(new_tile))
    def _():
        for k in range(N_SLAB):
            ys_ref[k] = jnp.where(mine, y[:, k * SLAB:(k + 1) * SLAB], ys_ref[k])


def _experts(step_tile, step_expert, step_lo, step_hi, xs, w1, w3, w2, layer):
    n_slab, ns, slab = xs.shape
    d = w1.shape[1]
    ff = w1.shape[2]
    n_steps = step_tile.shape[0]
    base = layer * MOE_EXPERTS
    grid_spec = pltpu.PrefetchScalarGridSpec(
        num_scalar_prefetch=4,
        grid=(n_steps,),
        in_specs=[pl.BlockSpec((n_slab, TM_X, slab), lambda s, tl, ex, lo, hi: (0, tl[s], 0)),
                  pl.BlockSpec((None, d, ff), lambda s, tl, ex, lo, hi: (base + ex[s], 0, 0)),
                  pl.BlockSpec((None, d, ff), lambda s, tl, ex, lo, hi: (base + ex[s], 0, 0)),
                  pl.BlockSpec((None, ff, d), lambda s, tl, ex, lo, hi: (base + ex[s], 0, 0))],
        out_specs=pl.BlockSpec((n_slab, TM_X, slab), lambda s, tl, ex, lo, hi: (0, tl[s], 0)),
        scratch_shapes=[pltpu.VMEM((d, ff), BF16), pltpu.VMEM((d, ff), BF16), pltpu.VMEM((ff, d), BF16)],
    )
    return pl.pallas_call(
        _expert_kernel,
        grid_spec=grid_spec,
        out_shape=jax.ShapeDtypeStruct((n_slab, ns, slab), xs.dtype),
        compiler_params=_cparams(("arbitrary",)),
        name="moe_experts",
    )(step_tile, step_expert, step_lo, step_hi, xs, w1, w3, w2)


def _combine_kernel(x_ref, gate_ref, fw_ref, wcol_ref, y1_ref, y2_ref, o_ref, *, final):
    w_first = wcol_ref[:, 0:1]
    w_second = wcol_ref[:, 1:2]
    y_first = _unpack_bf16_pairs(jnp.concatenate([y1_ref[k] for k in range(N_SLAB)], axis=-1))
    y_second = _unpack_bf16_pairs(jnp.concatenate([y2_ref[k] for k in range(N_SLAB)], axis=-1))
    moe = w_first * y_first + w_second * y_second
    x = x_ref[...] + gate_ref[...] * moe
    if final:
        x = x * lax.rsqrt(jnp.mean(x * x, axis=-1, keepdims=True) + EPS) * fw_ref[...]
    o_ref[...] = x


def _combine(x2, mod3, final_w_row, wcol, gathered, seq, final):
    t, d = x2.shape
    tm = TM_COMB
    tpb = seq // tm
    nblk = t // tm
    yspec = lambda off: pl.BlockSpec((N_SLAB, tm, SLAB), lambda i: (0, i + off, 0))
    return pl.pallas_call(
        functools.partial(_combine_kernel, final=final),
        grid=(nblk,),
        in_specs=[pl.BlockSpec((tm, d), lambda i: (i, 0)),
                  pl.BlockSpec((None, 1, d), lambda i: ((i // tpb) * 6 + 5, 0, 0)),
                  pl.BlockSpec((1, d), lambda i: (0, 0)),
                  pl.BlockSpec((tm, LANE), lambda i: (i, 0)),
                  yspec(0), yspec(nblk)],
        out_specs=pl.BlockSpec((tm, d), lambda i: (i, 0)),
        out_shape=jax.ShapeDtypeStruct((t, d), F32),
        compiler_params=_cparams(("parallel",)),
        name="moe_combine",
    )(x2, mod3, final_w_row, wcol, gathered, gathered)


def _moe(x2, mod3, final_w_row, w_route, b_route, tri_excl, w1, w3, w2, layer, seq, final):
    t, d = x2.shape
    h3, ids, wcol, counts = _router(x2, mod3, w_route, b_route, tri_excl, seq)
    cnt = counts[:, 0].astype(jnp.int32)
    ends = jnp.cumsum(cnt)
    offs = ends - cnt
    experts = jnp.arange(MOE_EXPERTS, dtype=jnp.int32)
    pick = lambda table, idx: jnp.sum(jnp.where(idx[:, None] == experts[None, :], table[None, :], 0), axis=1)
    slot1 = pick(offs, ids[0]) + ids[2]
    slot2 = pick(offs, ids[1]) + ids[3]
    n_tiles = 2 * t // TM_X
    first_tile = offs // TM_X
    n_vis = jnp.where(cnt > 0, (ends - 1) // TM_X - first_tile + 1, 0)
    cum = jnp.cumsum(n_vis)
    step = jnp.arange(n_tiles + MOE_EXPERTS, dtype=jnp.int32)
    step_expert = jnp.minimum(jnp.sum(step[:, None] >= cum[None, :], axis=1), MOE_EXPERTS - 1).astype(jnp.int32)
    valid = step < cum[-1]
    step_tile = jnp.where(valid, pick(first_tile - (cum - n_vis), step_expert) + step, n_tiles - 1)
    step_lo = jnp.where(valid, jnp.clip(pick(offs, step_expert) - step_tile * TM_X, 0, TM_X), 0)
    step_hi = jnp.where(valid, jnp.clip(pick(ends, step_expert) - step_tile * TM_X, 0, TM_X), 0)
    xs = _dispatch(slot1, slot2, h3)
    ys = _experts(step_tile.astype(jnp.int32), step_expert, step_lo.astype(jnp.int32), step_hi.astype(jnp.int32),
                  xs, w1, w3, w2, layer)
    n_sorted = ys.shape[1]
    gathered = _gather_rows(ys.reshape(N_SLAB * n_sorted, SLAB), _slab_rows(jnp.concatenate([slot1, slot2]), n_sorted))
    gathered = gathered.reshape(N_SLAB, n_sorted, SLAB)
    return _combine(x2, mod3, final_w_row, wcol, gathered, seq, final)


def kernel(x, c, positions, ada_w, ada_b, w_in, s5_lam_re, s5_lam_im, s5_b_re, s5_b_im, s5_c_re, s5_c_im, s5_d, s5_log_dt, s5_w_glu, hg_lb_logits, hg_norm_w, m2_conv_w, m2_conv_b, m2_dt_bias, m2_a_log, m2_d, m2_norm_w, w_branch, w_gate, b_gate, w_out, moe_w_group, moe_b_group, moe_w_expert, moe_b_expert, moe_w1, moe_w3, moe_w2, final_norm_w):
    bsz, seq, d = x.shape
    t = bsz * seq
    depth = ada_w.shape[0]
    assert seq % TM_PROJ == 0 and seq % C_RET == 0 and seq % C_SSD == 0 and seq % C_HG == 0
    x2 = x.reshape(t, d).astype(F32)

    c_pad = jnp.zeros((8, d), F32).at[:bsz].set(c.astype(F32))
    mod_all = _ada_mod(c_pad.T, ada_w.astype(F32), ada_b.astype(F32), bsz)

    half = RET_DK // 2
    inv_freq = ROPE_BASE ** (-jnp.arange(half, dtype=F32) / half)
    invf_col = jnp.broadcast_to(inv_freq[:, None], (half, LANE))
    expand = np.tile(np.eye(half, dtype=np.float32), (1, 2 * RET_HEADS))
    sign = np.tile(np.concatenate([-np.ones(half), np.ones(half)]), RET_HEADS)[None, :].astype(np.float32)
    cos_t, sin_t = _rope_tables(positions.reshape(1, t).astype(jnp.int32), invf_col,
                                jnp.asarray(expand, BF16), jnp.asarray(expand * sign, BF16))
    cos3 = cos_t.reshape(bsz, seq, -1)
    sin3 = sin_t.reshape(bsz, seq, -1)

    lb_cum = jnp.cumsum(jax.nn.softmax(hg_lb_logits.astype(F32), axis=0), axis=0)
    hg_lb = lb_cum - lb_cum[:1]
    tri_ssd = jnp.asarray(np.tril(np.ones((C_SSD, C_SSD), np.float32)), BF16)
    tri_excl = jnp.asarray(np.triu(np.ones((TM_PROJ, TM_PROJ), np.float32), 1), BF16)
    final_w_row = final_norm_w.astype(F32)[None, :]
    w_pad = jnp.concatenate([w_in.astype(BF16), jnp.zeros((depth, d, IN_W_PAD - IN_W), BF16)], axis=-1)
    w_glu_bf = s5_w_glu.astype(BF16)
    w_branch_bf = w_branch.astype(BF16)
    w_gate_bf = w_gate.astype(BF16)
    w_out_bf = w_out.astype(BF16)
    b_gate3 = b_gate.astype(F32).reshape(depth, 1, -1)
    moe_w1_all = moe_w1.astype(F32).reshape(depth * MOE_EXPERTS, d, MOE_FF)
    moe_w3_all = moe_w3.astype(F32).reshape(depth * MOE_EXPERTS, d, MOE_FF)
    moe_w2_all = moe_w2.astype(F32).reshape(depth * MOE_EXPERTS, MOE_FF, d)

    for layer in range(depth):
        mod3 = mod_all[layer, :bsz].reshape(bsz * 6, 1, d)
        p, u_s5 = _in_proj(x2, mod3, w_pad, layer, seq)
        p3 = p.reshape(bsz, seq, IN_W_PAD)

        ops = _s5_operators(s5_lam_re[layer], s5_lam_im[layer], s5_b_re[layer], s5_b_im[layer],
                            s5_c_re[layer], s5_c_im[layer], s5_d[layer], s5_log_dt[layer])
        y_s5 = _s5_scan(u_s5.reshape(bsz, seq, BRANCH_W), *ops).reshape(t, BRANCH_W)

        lb = hg_lb[layer][None, :]
        y_hg = _hgrn2(p3, jnp.log(lb), jnp.log1p(-lb), hg_norm_w[layer].astype(F32)[None, :]).reshape(t, BRANCH_W)

        y_ret = _retention(p3, cos3, sin3).reshape(t, BRANCH_W)

        pad8 = lambda v: jnp.zeros((1, LANE), F32).at[0, :M2_HEADS].set(v.astype(F32))
        y_m2 = _ssd(p3, tri_ssd, m2_conv_w[layer].astype(F32), m2_conv_b[layer].astype(F32)[None, :],
                    pad8(m2_dt_bias[layer]), pad8(m2_a_log[layer]),
                    jnp.repeat(m2_d[layer].astype(F32), M2_HEADDIM)[None, :],
                    m2_norm_w[layer].astype(F32)[None, :]).reshape(t, BRANCH_W)

        nr = 40
        w_route = jnp.zeros((nr, d), F32).at[:MOE_GROUPS].set(moe_w_group[layer].astype(F32).T)
        w_route = w_route.at[MOE_GROUPS:MOE_GROUPS + MOE_EXPERTS].set(moe_w_expert[layer].astype(F32).T)
        b_route = jnp.zeros((nr, LANE), F32).at[:MOE_GROUPS, 0].set(moe_b_group[layer].astype(F32))
        b_route = b_route.at[MOE_GROUPS:MOE_GROUPS + MOE_EXPERTS, 0].set(moe_b_expert[layer].astype(F32))
        x2 = _merge(x2, mod3, y_s5, y_hg, y_ret, y_m2, w_glu_bf, w_branch_bf, w_gate_bf, b_gate3, w_out_bf,
                    layer, seq)
        x2 = _moe(x2, mod3, final_w_row, w_route, b_route, tri_excl, moe_w1_all, moe_w3_all, moe_w2_all,
                  layer, seq, final=(layer == depth - 1))
    return x2.reshape(bsz, seq, d)
```

```python
import functools
import math

import numpy as np
import jax
import jax.numpy as jnp
from jax import lax
from jax.experimental import pallas as pl
from jax.experimental.pallas import tpu as pltpu
from jax.experimental.pallas import tpu_sc as plsc

F32 = jnp.float32
BF16 = jnp.bfloat16
HIGHEST = lax.Precision.HIGHEST

D_MODEL = 1024
DEPTH = 2
BRANCH_W = 512
EPS = 1e-6
S5_GROUPS = 32
S5_CH = 16
S5_STATE = 64
S5_MAX_REAL = -1e-4
S5_BLOCK = 16
S5_SEQ_PER_STEP = 2
HG_HEADS = 4
HG_DK = 128
RET_HEADS = 4
RET_DK = 64
RET_DV = 128
ROPE_BASE = 10000.0
M2_HEADS = 8
M2_HEADDIM = 64
M2_GROUPS = 2
M2_STATE = 128
M2_CONV = 4
MOE_GROUPS = 4
MOE_EPG = 8
MOE_EXPERTS = MOE_GROUPS * MOE_EPG
MOE_FF = 256

COL_S5, COL_HQ, COL_HF, COL_HI, COL_HG = 0, 512, 1024, 1536, 2048
COL_RQ, COL_RK, COL_RV, COL_RG = 2560, 2816, 3072, 3584
COL_MZ, COL_MXS, COL_MBC, COL_MDT = 4096, 4608, 5120, 5632
IN_W = 5640
IN_W_PAD = 5888

LANE = 128
VMEM_LIMIT = 56 * 1024 * 1024

TM_PROJ = 1024
TN_PROJ = 1024
TM_INPROJ = 512
LOG2_E = 1.4426950408889634
C_RET = 512
C_SSD = 256
C_HG = 128
TM_X = 512
TM_COMB = 1024
SC_WINDOW = 128
SLAB = 256
N_SLAB = D_MODEL // 2 // SLAB


def _cparams(sem):
    return pltpu.CompilerParams(dimension_semantics=sem, vmem_limit_bytes=VMEM_LIMIT)


def _silu(v):
    return v * jax.nn.sigmoid(v)


def _dot_nt(a, b, **kw):
    return lax.dot_general(a, b, (((1,), (1,)), ((), ())), preferred_element_type=F32, **kw)


def _dot_tn(a, b, **kw):
    return lax.dot_general(a, b, (((0,), (0,)), ((), ())), preferred_element_type=F32, **kw)


def _ada_kernel(ct_ref, w_ref, b_ref, o_ref, *, n_rows):
    cond_t = _silu(ct_ref[...])
    w = w_ref[...]
    rows = [jnp.sum(w * cond_t[:, b:b + 1], axis=0, keepdims=True) for b in range(n_rows)]
    rows += [jnp.zeros_like(rows[0])] * (cond_t.shape[1] - n_rows)
    o_ref[...] = jnp.concatenate(rows, axis=0) + b_ref[...]


def _ada_mod(c_pad_t, ada_w, ada_b, n_rows):
    depth, d, n = ada_w.shape
    tn = 1536
    return pl.pallas_call(
        functools.partial(_ada_kernel, n_rows=n_rows),
        grid=(depth, n // tn),
        in_specs=[pl.BlockSpec((d, 8), lambda l, j: (0, 0)),
                  pl.BlockSpec((None, d, tn), lambda l, j: (l, 0, j)),
                  pl.BlockSpec((None, 1, tn), lambda l, j: (l, 0, j))],
        out_specs=pl.BlockSpec((None, 8, tn), lambda l, j: (l, 0, j)),
        out_shape=jax.ShapeDtypeStruct((depth, 8, n), F32),
        compiler_params=_cparams(("parallel", "parallel")),
        name="ada_mod",
    )(c_pad_t, ada_w, ada_b.reshape(depth, 1, n))


def _pack_bf16_pairs(x):
    n = x.shape[1] // 2
    lo = pltpu.bitcast(x[:, :n].astype(BF16).astype(F32), jnp.uint32) >> 16
    hi = pltpu.bitcast(x[:, n:].astype(BF16).astype(F32), jnp.uint32)
    return hi | lo


def _unpack_bf16_pairs(w):
    lo = pltpu.bitcast(w << 16, F32)
    hi = pltpu.bitcast(w & jnp.uint32(0xFFFF0000), F32)
    return jnp.concatenate([lo, hi], axis=-1)


def _modulated_norm(x, scale, shift):
    ms = jnp.mean(x * x, axis=-1, keepdims=True)
    return x * lax.rsqrt(ms + EPS) * (1.0 + scale) + shift


def _inproj_kernel(x_ref, sc_ref, sh_ref, w_ref, o_ref, u_ref):
    h = _modulated_norm(x_ref[...], sc_ref[...], sh_ref[...]).astype(BF16)
    n_total = o_ref.shape[1]
    for n0 in range(0, n_total, TN_PROJ):
        n1 = min(n0 + TN_PROJ, n_total)
        p = jnp.dot(h, w_ref[:, n0:n1], preferred_element_type=F32)
        o_ref[:, n0:n1] = p.astype(o_ref.dtype)
        if n0 == 0:
            u_ref[...] = p[:, COL_S5:COL_S5 + BRANCH_W]


def _in_proj(x2, mod3, w_pad, layer, seq):
    t, d = x2.shape
    tm = TM_INPROJ
    tpb = seq // tm
    assert COL_S5 + BRANCH_W <= TN_PROJ
    return pl.pallas_call(
        _inproj_kernel,
        grid=(t // tm,),
        in_specs=[pl.BlockSpec((tm, d), lambda i: (i, 0)),
                  pl.BlockSpec((None, 1, d), lambda i: ((i // tpb) * 6 + 1, 0, 0)),
                  pl.BlockSpec((None, 1, d), lambda i: ((i // tpb) * 6 + 0, 0, 0)),
                  pl.BlockSpec((None, d, IN_W_PAD), lambda i: (layer, 0, 0), pipeline_mode=pl.Buffered(1))],
        out_specs=[pl.BlockSpec((tm, IN_W_PAD), lambda i: (i, 0)),
                   pl.BlockSpec((tm, BRANCH_W), lambda i: (i, 0))],
        out_shape=[jax.ShapeDtypeStruct((t, IN_W_PAD), BF16), jax.ShapeDtypeStruct((t, BRANCH_W), F32)],
        compiler_params=_cparams(("parallel",)),
        name="in_proj",
    )(x2, mod3, mod3, w_pad)


def _rope_kernel(pos_ref, invf_ref, ecos_ref, esin_ref, cos_ref, sin_ref):
    ang = invf_ref[:, 0:1] * pos_ref[...].astype(F32)
    def spread(values, e_ref):
        hi = values.astype(BF16)
        rest = values - hi.astype(F32)
        mid = rest.astype(BF16)
        lo = (rest - mid.astype(F32)).astype(BF16)
        e = e_ref[...]
        return _dot_tn(hi, e) + _dot_tn(mid, e) + _dot_tn(lo, e)

    cos_ref[...] = spread(jnp.cos(ang), ecos_ref)
    sin_ref[...] = spread(jnp.sin(ang), esin_ref)


def _rope_tables(pos_row, invf_col, expand_cos, expand_sin):
    t = pos_row.shape[1]
    half, w = expand_cos.shape
    tm = 1024
    const = lambda shape: pl.BlockSpec(shape, lambda i: (0, 0))
    return pl.pallas_call(
        _rope_kernel,
        grid=(t // tm,),
        in_specs=[pl.BlockSpec((1, tm), lambda i: (0, i)), const((half, LANE)), const((half, w)), const((half, w))],
        out_specs=[pl.BlockSpec((tm, w), lambda i: (i, 0))] * 2,
        out_shape=[jax.ShapeDtypeStruct((t, w), F32)] * 2,
        compiler_params=_cparams(("parallel",)),
        name="rope_tables",
    )(pos_row, invf_col, expand_cos, expand_sin)


def _ret_kernel(q_ref, k_ref, v_ref, g_ref, cos_ref, sin_ref, o_ref, st_ref, dec_ref, *, chunk):
    @pl.when(pl.program_id(1) == 0)
    def _():
        st_ref[...] = jnp.zeros_like(st_ref)
        ti = lax.broadcasted_iota(jnp.int32, (chunk, chunk), 0)
        si = lax.broadcasted_iota(jnp.int32, (chunk, chunk), 1)
        lag = (ti - si).astype(F32)
        for h in range(RET_HEADS):
            log_gamma = math.log1p(-(2.0 ** (-5.0 - h)))
            dec_ref[h] = jnp.where(ti >= si, jnp.exp(jnp.minimum(lag * log_gamma, 0.0)), 0.0)

    cosf = cos_ref[...]
    sinf = sin_ref[...]
    width = RET_HEADS * RET_DK
    lane = lax.broadcasted_iota(jnp.int32, (chunk, width), 1)
    first_half = (lane % RET_DK) < (RET_DK // 2)

    def rope(t):
        partner = jnp.where(first_half, pltpu.roll(t, width - RET_DK // 2, 1), pltpu.roll(t, RET_DK // 2, 1))
        return t * cosf + partner * sinf

    q = rope(q_ref[...].astype(F32))
    k = rope(k_ref[...].astype(F32)) * (RET_DK ** -0.5)
    v = v_ref[...]
    g = g_ref[...].astype(F32)
    tcol = lax.broadcasted_iota(jnp.int32, (chunk, 1), 0).astype(F32)
    for h in range(RET_HEADS):
        log_gamma = math.log1p(-(2.0 ** (-5.0 - h)))
        qh = q[:, h * RET_DK:(h + 1) * RET_DK]
        kh = k[:, h * RET_DK:(h + 1) * RET_DK]
        vh = v[:, h * RET_DV:(h + 1) * RET_DV].astype(BF16)
        scores = _dot_nt(qh.astype(BF16), kh.astype(BF16)) * dec_ref[h]
        state = st_ref[h]
        q_in = qh * jnp.exp(log_gamma * (tcol + 1.0))
        o = (jnp.dot(scores.astype(BF16), vh, preferred_element_type=F32)
             + jnp.dot(q_in.astype(BF16), state.astype(BF16), preferred_element_type=F32))
        k_out = kh * jnp.exp(log_gamma * (chunk - 1.0 - tcol))
        st_ref[h] = math.exp(log_gamma * chunk) * state + _dot_tn(k_out.astype(BF16), vh)
        o = o * lax.rsqrt(jnp.mean(o * o, axis=-1, keepdims=True) + EPS)
        gh = g[:, h * RET_DV:(h + 1) * RET_DV]
        o_ref[:, h * RET_DV:(h + 1) * RET_DV] = (o * _silu(gh)).astype(o_ref.dtype)


def _retention(p3, cos3, sin3):
    b, seq, _ = p3.shape
    c = C_RET
    qk_w = RET_HEADS * RET_DK
    return pl.pallas_call(
        functools.partial(_ret_kernel, chunk=c),
        grid=(b, seq // c),
        in_specs=[pl.BlockSpec((None, c, qk_w), lambda i, j: (i, j, COL_RQ // qk_w)),
                  pl.BlockSpec((None, c, qk_w), lambda i, j: (i, j, COL_RK // qk_w)),
                  pl.BlockSpec((None, c, BRANCH_W), lambda i, j: (i, j, COL_RV // BRANCH_W)),
                  pl.BlockSpec((None, c, BRANCH_W), lambda i, j: (i, j, COL_RG // BRANCH_W)),
                  pl.BlockSpec((None, c, qk_w), lambda i, j: (i, j, 0)),
                  pl.BlockSpec((None, c, qk_w), lambda i, j: (i, j, 0))],
        out_specs=pl.BlockSpec((None, c, BRANCH_W), lambda i, j: (i, j, 0)),
        out_shape=jax.ShapeDtypeStruct((b, seq, BRANCH_W), BF16),
        scratch_shapes=[pltpu.VMEM((RET_HEADS, RET_DK, RET_DV), F32), pltpu.VMEM((RET_HEADS, c, c), F32)],
        compiler_params=_cparams(("parallel", "arbitrary")),
        name="retention",
    )(p3, p3, p3, p3, cos3, sin3)


def _ssd_kernel(z_ref, xs_ref, bc_ref, dt_ref, tri_ref, cw_ref, cb_ref, dtb_ref, alog_ref, dsk_ref, nw_ref,
                o_ref, xe_scr, st_ref, *, chunk):
    j = pl.program_id(1)
    width = 2 * BRANCH_W

    @pl.when(j == 0)
    def _():
        st_ref[...] = jnp.zeros_like(st_ref)
        xe_scr[0:8, :] = jnp.zeros((8, width), F32)

    @pl.when(j > 0)
    def _():
        xe_scr[0:8, :] = xe_scr[chunk:chunk + 8, :]

    xe_scr[8:, 0:BRANCH_W] = xs_ref[...].astype(F32)
    xe_scr[8:, BRANCH_W:] = bc_ref[...].astype(F32)
    conv = cb_ref[...] + cw_ref[M2_CONV - 1:M2_CONV, :] * xe_scr[8:, :]
    for tap in range(M2_CONV - 1):
        conv = conv + cw_ref[tap:tap + 1, :] * xe_scr[pl.ds(8 - (M2_CONV - 1) + tap, chunk), :]
    conv = _silu(conv)
    xs = conv[:, :BRANCH_W]
    bm = conv[:, BRANCH_W:BRANCH_W + M2_GROUPS * M2_STATE]
    cm = conv[:, BRANCH_W + M2_GROUPS * M2_STATE:]

    dt = jax.nn.softplus(dt_ref[...].astype(F32) + dtb_ref[...])
    da = dt * (-jnp.exp(alog_ref[...]))
    da_hi = da.astype(BF16)
    da_r = da - da_hi.astype(F32)
    da_mid = da_r.astype(BF16)
    da_lo = (da_r - da_mid.astype(F32)).astype(BF16)
    tri = tri_ref[...]
    a_cs = (jnp.dot(tri, da_hi, preferred_element_type=F32) + jnp.dot(tri, da_mid, preferred_element_type=F32)
            + jnp.dot(tri, da_lo, preferred_element_type=F32))
    a_cs = a_cs * LOG2_E
    a_cs_t = a_cs.T
    ti = lax.broadcasted_iota(jnp.int32, (chunk, chunk), 0)
    si = lax.broadcasted_iota(jnp.int32, (chunk, chunk), 1)
    causal = ti >= si
    hpg = M2_HEADS // M2_GROUPS
    ys = []
    for grp in range(M2_GROUPS):
        bm_g = bm[:, grp * M2_STATE:(grp + 1) * M2_STATE]
        cm_g = cm[:, grp * M2_STATE:(grp + 1) * M2_STATE]
        cb = _dot_nt(cm_g.astype(BF16), bm_g.astype(BF16))
        for hh in range(hpg):
            h = grp * hpg + hh
            col = a_cs[:, h:h + 1]
            row = a_cs_t[h:h + 1, :]
            lmat = jnp.where(causal, jnp.exp2(col - row), 0.0)
            xd = xs[:, h * M2_HEADDIM:(h + 1) * M2_HEADDIM] * dt[:, h:h + 1]
            state = st_ref[h]
            y = (jnp.dot((cb * lmat).astype(BF16), xd.astype(BF16), preferred_element_type=F32)
                 + jnp.dot((cm_g * jnp.exp2(col)).astype(BF16), state.astype(BF16), preferred_element_type=F32))
            a_last = a_cs[chunk - 1:chunk, h:h + 1]
            to_end = jnp.exp2(a_last - col)
            st_ref[h] = jnp.exp2(a_last) * state + _dot_tn(bm_g.astype(BF16), (xd * to_end).astype(BF16))
            ys.append(y)
    y = jnp.concatenate(ys, axis=-1) + dsk_ref[...] * xs
    y = y * _silu(z_ref[...].astype(F32))
    o_ref[...] = (y * lax.rsqrt(jnp.mean(y * y, axis=-1, keepdims=True) + EPS) * nw_ref[...]).astype(o_ref.dtype)


def _ssd(p3, tri, conv_w, conv_b, dt_bias_row, a_log_row, d_skip_row, norm_w_row):
    b, seq, _ = p3.shape
    c = C_SSD
    const = lambda shape: pl.BlockSpec(shape, lambda i, j: (0,) * len(shape))
    return pl.pallas_call(
        functools.partial(_ssd_kernel, chunk=c),
        grid=(b, seq // c),
        in_specs=[pl.BlockSpec((None, c, BRANCH_W), lambda i, j: (i, j, COL_MZ // BRANCH_W)),
                  pl.BlockSpec((None, c, BRANCH_W), lambda i, j: (i, j, COL_MXS // BRANCH_W)),
                  pl.BlockSpec((None, c, BRANCH_W), lambda i, j: (i, j, COL_MBC // BRANCH_W)),
                  pl.BlockSpec((None, c, LANE), lambda i, j: (i, j, COL_MDT // LANE)),
                  const((c, c)), const((M2_CONV, 2 * BRANCH_W)), const((1, 2 * BRANCH_W)),
                  const((1, LANE)), const((1, LANE)), const((1, BRANCH_W)), const((1, BRANCH_W))],
        out_specs=pl.BlockSpec((None, c, BRANCH_W), lambda i, j: (i, j, 0)),
        out_shape=jax.ShapeDtypeStruct((b, seq, BRANCH_W), BF16),
        scratch_shapes=[pltpu.VMEM((c + 8, 2 * BRANCH_W), F32),
                        pltpu.VMEM((M2_HEADS, M2_STATE, M2_HEADDIM), F32)],
        compiler_params=_cparams(("parallel", "arbitrary")),
        name="ssd",
    )(p3, p3, p3, p3, tri, conv_w, conv_b, dt_bias_row, a_log_row, d_skip_row, norm_w_row)


def _hg_tables(chunk):
    n_lev = int(math.log2(chunk))
    r = np.arange(chunk)[:, None]
    jj = np.arange(chunk)[None, :]
    tri = (jj <= r).astype(np.float32)
    x = r ^ jj
    levmap = np.where(r > jj, np.floor(np.log2(x + 0.5)), np.where(r == jj, -1, -2)).astype(np.int32)
    return tri, levmap, n_lev


def _hg_level_exponent(b, lev):
    rows, width = b.shape
    m = 1 << lev
    sub = 8
    if 2 * m >= sub:
        blocks = b.reshape(rows // (2 * m), 2 * m, width)
        mid = jnp.broadcast_to(blocks[:, m - 1:m, :], blocks.shape).reshape(rows, width)
    else:
        groups = b.reshape(rows // sub, sub, width)
        row_in_group = lax.broadcasted_iota(jnp.int32, groups.shape, 1)
        mid = None
        for start in range(0, sub, 2 * m):
            picked = jnp.broadcast_to(groups[:, start + m - 1:start + m, :], groups.shape)
            mid = picked if mid is None else jnp.where(row_in_group >= start, picked, mid)
        mid = mid.reshape(rows, width)
    return -jnp.abs(b - mid)


def _hg_kernel(q_ref, f_ref, i_ref, g_ref, sum_ref, lev_ref, llb_ref, l1m_ref, nw_ref, o_ref, st_ref,
               *, chunk, n_lev):
    @pl.when(pl.program_id(1) == 0)
    def _():
        st_ref[...] = jnp.zeros_like(st_ref)

    f = f_ref[...].astype(F32)
    y = jnp.exp(-jnp.abs(f))
    one_plus_y = 1.0 + y
    log_sig = jnp.minimum(f, 0.0) - jnp.log(one_plus_y)
    a = llb_ref[...]
    bb = l1m_ref[...] + log_sig
    log_f = jnp.maximum(a, bb) + jnp.log(1.0 + jnp.exp(-jnp.abs(a - bb)))
    k_all = jnp.exp(l1m_ref[...]) * (jnp.where(f >= 0.0, y, 1.0) / one_plus_y)
    q_all = _silu(q_ref[...].astype(F32))
    hi = log_f.astype(BF16)
    r1 = log_f - hi.astype(F32)
    mid = r1.astype(BF16)
    lo = (r1 - mid.astype(F32)).astype(BF16)
    tri = sum_ref[...]
    b_all = (jnp.dot(tri, hi, preferred_element_type=F32)
             + jnp.dot(tri, mid, preferred_element_type=F32)
             + jnp.dot(tri, lo, preferred_element_type=F32))
    b_all = b_all * LOG2_E
    to_end_all = b_all[chunk - 1:chunk, :] - b_all
    level_decay = [jnp.exp2(_hg_level_exponent(b_all, lev)) for lev in range(n_lev)]
    levmap = lev_ref[...]
    on_diag = levmap == -1
    on_level = [levmap == lev for lev in range(n_lev)]
    v_all = i_ref[...]
    g_all = g_ref[...].astype(F32)
    for h in range(HG_HEADS):
        sl = slice(h * HG_DK, (h + 1) * HG_DK)
        qh = q_all[:, sl]
        kh = k_all[:, sl]
        vh = v_all[:, sl].astype(BF16)
        b_h = b_all[:, sl]
        to_end = to_end_all[:, sl]
        amat = jnp.where(on_diag, _dot_nt(qh.astype(BF16), kh.astype(BF16)), 0.0)
        for lev in range(n_lev):
            e = level_decay[lev][:, sl]
            a_l = _dot_nt((qh * e).astype(BF16), (kh * e).astype(BF16))
            amat = jnp.where(on_level[lev], a_l, amat)
        state_t = st_ref[h]
        o = (jnp.dot(amat.astype(BF16), vh, preferred_element_type=F32)
             + _dot_nt((qh * jnp.exp2(b_h)).astype(BF16), state_t.astype(BF16)))
        k_end = kh * jnp.exp2(to_end)
        st_ref[h] = jnp.exp2(b_h[chunk - 1:chunk, :]) * state_t + _dot_tn(vh, k_end.astype(BF16))
        o = o * lax.rsqrt(jnp.mean(o * o, axis=-1, keepdims=True) + EPS) * nw_ref[...]
        o_ref[:, sl] = (o * _silu(g_all[:, sl])).astype(o_ref.dtype)


def _hgrn2(p3, log_lb_row, log1m_lb_row, norm_w_row):
    b, seq, _ = p3.shape
    c = C_HG
    tri, levmap, n_lev = _hg_tables(c)
    const = lambda shape: pl.BlockSpec(shape, lambda i, j: (0,) * len(shape))
    blk = lambda col: pl.BlockSpec((None, c, BRANCH_W), lambda i, j: (i, j, col // BRANCH_W))
    return pl.pallas_call(
        functools.partial(_hg_kernel, chunk=c, n_lev=n_lev),
        grid=(b, seq // c),
        in_specs=[blk(COL_HQ), blk(COL_HF), blk(COL_HI), blk(COL_HG),
                  const((c, c)), const((c, c)),
                  const((1, BRANCH_W)), const((1, BRANCH_W)), const((1, HG_DK))],
        out_specs=pl.BlockSpec((None, c, BRANCH_W), lambda i, j: (i, j, 0)),
        out_shape=jax.ShapeDtypeStruct((b, seq, BRANCH_W), BF16),
        scratch_shapes=[pltpu.VMEM((HG_HEADS, HG_DK, HG_DK), F32)],
        compiler_params=_cparams(("parallel", "arbitrary")),
        name="hgrn2",
    )(p3, p3, p3, p3, jnp.asarray(tri, BF16), jnp.asarray(levmap), log_lb_row, log1m_lb_row, norm_w_row)


def _expand_block_diag(comp_ref, e_ref, dst_ref, row_div, lane_div, causal=False):
    gq = LANE // S5_CH
    rows, ncols = dst_ref.shape
    step = 512
    for c0 in range(0, ncols, step):
        r1 = min(rows, c0 + step) if causal else rows
        row_grp = (lax.broadcasted_iota(jnp.int32, (r1, step), 0) // row_div) % gq
        lane_grp = ((lax.broadcasted_iota(jnp.int32, (r1, step), 1) + c0) // lane_div) % gq
        full = jnp.dot(comp_ref[0:r1, :], e_ref[:, c0:c0 + step], preferred_element_type=F32)
        dst_ref[0:r1, c0:c0 + step] = jnp.where(row_grp == lane_grp, full, 0.0).astype(dst_ref.dtype)


def _s5_kernel(u_ref, k2_ref, bc_ref, cc_ref, esc_ref, eb_ref, lam_ref, o_ref, tc_scr, tq_ref, bq_ref, cq_ref,
               x_scr, w_scr, s_scr, *, rows):
    nb = S5_BLOCK

    @pl.when(pl.program_id(1) == 0)
    def _():
        k2 = k2_ref[...]
        lane = lax.broadcasted_iota(jnp.int32, k2.shape, 1)
        for t in range(nb):
            shifted = k2 if t == 0 else jnp.where(lane >= t * S5_CH, pltpu.roll(k2, t * S5_CH, 1), 0.0)
            tc_scr[t * LANE:(t + 1) * LANE, :] = shifted.astype(tc_scr.dtype)
        _expand_block_diag(tc_scr, esc_ref, tq_ref, S5_CH, S5_CH, causal=True)
        _expand_block_diag(bc_ref, eb_ref, bq_ref, S5_CH, S5_STATE)
        _expand_block_diag(cc_ref, esc_ref, cq_ref, S5_STATE, S5_CH)

    n_seq = u_ref.shape[0]
    for b in range(n_seq):
        for t in range(nb):
            x_scr[b * rows:(b + 1) * rows, t * LANE:(t + 1) * LANE] = (
                u_ref[b, pl.ds(t, rows, stride=nb), :].astype(x_scr.dtype))
    x = x_scr[...]
    half = w_scr.shape[1] // 2
    w_scr[...] = jnp.dot(x, bq_ref[...], preferred_element_type=F32)
    lam_re = lam_ref[0:1, :]
    lam_im = lam_ref[1:2, :]

    def body(j, carry):
        out = []
        for b in range(n_seq):
            s_re, s_im = carry[2 * b], carry[2 * b + 1]
            r = b * rows + j
            s_scr[pl.ds(r, 1), 0:half] = s_re
            s_scr[pl.ds(r, 1), half:] = s_im
            w_re = w_scr[pl.ds(r, 1), 0:half]
            w_im = w_scr[pl.ds(r, 1), half:]
            out += [lam_re * s_re - lam_im * s_im + w_re, lam_re * s_im + lam_im * s_re + w_im]
        return tuple(out)

    zero = jnp.zeros((1, half), F32)
    lax.fori_loop(0, rows, body, (zero,) * (2 * n_seq))
    s_bf = s_scr[...].astype(BF16)
    pair = 2 * LANE
    for c0 in range(0, nb * LANE, pair):
        k_rows = c0 + pair
        y = (jnp.dot(x[:, :k_rows], tq_ref[0:k_rows, c0:c0 + pair], preferred_element_type=F32)
             + jnp.dot(s_bf, cq_ref[:, c0:c0 + pair], preferred_element_type=F32))
        for b in range(n_seq):
            for t in range(c0 // LANE, (c0 + pair) // LANE):
                o_ref[b, pl.ds(t, rows, stride=nb), :] = y[b * rows:(b + 1) * rows, t * LANE - c0:(t + 1) * LANE - c0]


def _s5_scan(p3, k2, bc, cc, lam16):
    batch, seq, _ = p3.shape
    nb = S5_BLOCK
    nq = BRANCH_W // LANE
    rows = seq // nb
    kdim = nb * LANE
    gq = LANE // S5_CH
    ncol = 2 * gq * S5_STATE
    e_sc = (np.eye(nb)[:, None, :, None, None] * np.eye(S5_CH)[None, :, None, None, :] * np.ones((1, 1, 1, gq, 1)))
    e_sc = e_sc.reshape(nb * S5_CH, nb * gq * S5_CH)
    e_b = (np.eye(2)[:, None, :, None, None] * np.eye(S5_STATE)[None, :, None, None, :] * np.ones((1, 1, 1, gq, 1)))
    e_b = e_b.reshape(2 * S5_STATE, ncol)
    full = lambda shape: pl.BlockSpec(shape, lambda q, b: (0,) * len(shape))
    per_q = lambda r, c: pl.BlockSpec((None, r, c), lambda q, b: (q, 0, 0))
    n_seq = S5_SEQ_PER_STEP if batch % S5_SEQ_PER_STEP == 0 else 1
    return pl.pallas_call(
        functools.partial(_s5_kernel, rows=rows),
        grid=(nq, batch // n_seq),
        in_specs=[pl.BlockSpec((n_seq, seq, LANE), lambda q, b: (b, 0, q)),
                  per_q(LANE, nb * S5_CH), per_q(kdim, 2 * S5_STATE), per_q(ncol, nb * S5_CH),
                  full(e_sc.shape), full(e_b.shape), per_q(2, ncol // 2)],
        out_specs=pl.BlockSpec((n_seq, seq, LANE), lambda q, b: (b, 0, q)),
        out_shape=jax.ShapeDtypeStruct((batch, seq, BRANCH_W), F32),
        scratch_shapes=[pltpu.VMEM((kdim, nb * S5_CH), BF16),
                        pltpu.VMEM((kdim, kdim), BF16), pltpu.VMEM((kdim, ncol), BF16), pltpu.VMEM((ncol, kdim), BF16),
                        pltpu.VMEM((n_seq * rows, kdim), BF16), pltpu.VMEM((n_seq * rows, ncol), F32),
                        pltpu.VMEM((n_seq * rows, ncol), F32)],
        compiler_params=_cparams(("parallel", "arbitrary")),
        name="s5_scan",
    )(p3, k2, bc, cc, jnp.asarray(e_sc, BF16), jnp.asarray(e_b, BF16), lam16)


def _s5_operators(lam_re, lam_im, b_re, b_im, c_re, c_im, d_skip, log_dt):
    nb = S5_BLOCK
    gq = LANE // S5_CH
    nq = S5_GROUPS // gq
    lam = lax.complex(jnp.minimum(lam_re.astype(F32), S5_MAX_REAL), lam_im.astype(F32))
    step = jnp.exp(log_dt.astype(F32))[:, None]
    z = lam * step
    lam_bar = jnp.exp(z)
    b_bar = ((lam_bar - 1.0) / lam)[..., None] * lax.complex(b_re.astype(F32), b_im.astype(F32))
    c_mat = lax.complex(c_re.astype(F32), c_im.astype(F32))
    pw = jnp.exp(z[..., None] * jnp.arange(nb + 1, dtype=F32))
    cp = c_mat[:, None, :, :] * pw[..., :nb].transpose(0, 2, 1)[:, :, None, :]
    cp = jnp.concatenate([cp.real, -cp.imag], axis=-1).reshape(S5_GROUPS, nb * S5_CH, 2 * S5_STATE)
    bri = jnp.concatenate([b_bar.real, b_bar.imag], axis=1)
    kern = jnp.einsum('gnk,gki->gin', cp, bri, precision=HIGHEST)
    skip = (jnp.asarray(np.concatenate([np.eye(S5_CH), np.zeros((S5_CH, (nb - 1) * S5_CH))], axis=1), F32)[None]
            * d_skip.astype(F32).reshape(S5_GROUPS, S5_CH, 1))
    k2 = (kern + skip).reshape(nq, gq * S5_CH, nb * S5_CH)
    pw_rev = jnp.exp(z[..., None] * jnp.asarray(np.arange(nb - 1, -1, -1), F32))
    binc = pw_rev[:, :, :, None] * b_bar[:, :, None, :]
    binc = jnp.stack([binc.real, binc.imag], axis=0).reshape(2, nq, gq, S5_STATE, nb, S5_CH)
    bc = binc.transpose(1, 4, 2, 5, 0, 3).reshape(nq, nb * LANE, 2 * S5_STATE)
    cm = c_mat.transpose(0, 2, 1)[:, :, None, :] * pw[..., 1:][:, :, :, None]
    cm = jnp.stack([cm.real, -cm.imag], axis=0).reshape(2, nq, gq * S5_STATE, nb * S5_CH)
    cc = cm.transpose(1, 0, 2, 3).reshape(nq, 2 * gq * S5_STATE, nb * S5_CH)
    lam_n = pw[..., nb].reshape(nq, gq * S5_STATE)
    lam16 = jnp.stack([lam_n.real, lam_n.imag], axis=1)
    return k2, bc.astype(BF16), cc.astype(BF16), lam16


def _merge_kernel(x_ref, sc_ref, sh_ref, gm_ref, ys5_ref, yhg_ref, yret_ref, ym2_ref,
                  wglu_ref, wbr_ref, wg_ref, bg_ref, wout_ref, o_ref):
    x = x_ref[...]
    d = x.shape[1]
    h = _modulated_norm(x, sc_ref[...], sh_ref[...]).astype(BF16)
    y_s5 = jax.nn.gelu(ys5_ref[...])
    y_s5 = y_s5 * jax.nn.sigmoid(jnp.dot(y_s5.astype(BF16), wglu_ref[...], preferred_element_type=F32))
    acc = jnp.zeros(x.shape, F32)
    for n, y in enumerate((y_s5, yhg_ref[...], yret_ref[...], ym2_ref[...])):
        gate = jax.nn.sigmoid(jnp.dot(h, wg_ref[:, n * d:(n + 1) * d], preferred_element_type=F32)
                              + bg_ref[:, n * d:(n + 1) * d])
        acc = acc + gate * jnp.dot(y.astype(BF16), wbr_ref[n], preferred_element_type=F32)
    o_ref[...] = x + gm_ref[...] * jnp.dot(acc.astype(BF16), wout_ref[...], preferred_element_type=F32)


def _merge(x2, mod3, ys5, yhg, yret, ym2, w_glu, w_branch, w_gate, b_gate, w_out, layer, seq):
    t, d = x2.shape
    tm = TM_PROJ
    tpb = seq // tm
    const = lambda shape: pl.BlockSpec((None,) + shape, lambda i: (layer,) + (0,) * len(shape),
                                       pipeline_mode=pl.Buffered(1))
    modspec = lambda k: pl.BlockSpec((None, 1, d), lambda i: ((i // tpb) * 6 + k, 0, 0))
    yspec = pl.BlockSpec((tm, BRANCH_W), lambda i: (i, 0))
    return pl.pallas_call(
        _merge_kernel,
        grid=(t // tm,),
        in_specs=[pl.BlockSpec((tm, d), lambda i: (i, 0)), modspec(1), modspec(0), modspec(2),
                  yspec, yspec, yspec, yspec,
                  const((BRANCH_W, BRANCH_W)), const((4, BRANCH_W, d)), const((d, 4 * d)), const((1, 4 * d)),
                  const((d, d))],
        out_specs=pl.BlockSpec((tm, d), lambda i: (i, 0)),
        out_shape=jax.ShapeDtypeStruct((t, d), F32),
        compiler_params=_cparams(("parallel",)),
        name="merge",
    )(x2, mod3, mod3, mod3, ys5, yhg, yret, ym2, w_glu, w_branch, w_gate, b_gate, w_out)


def _router_kernel(x_ref, sc_ref, sh_ref, wr_ref, br_ref, tri_ref, h_ref, ids_ref, wts_ref, cnt_ref, carry):
    i = pl.program_id(0)

    @pl.when(i == 0)
    def _():
        carry[...] = jnp.zeros_like(carry)

    h = _modulated_norm(x_ref[...], sc_ref[...], sh_ref[...])
    tm, d = h.shape
    packed = _pack_bf16_pairs(h)
    for k in range(N_SLAB):
        h_ref[k] = packed[:, k * SLAB:(k + 1) * SLAB]
    h_hi = h.astype(BF16)
    h_lo = (h - h_hi.astype(F32)).astype(BF16)
    w_r = wr_ref[...]
    w_hi = w_r.astype(BF16)
    w_lo = (w_r - w_hi.astype(F32)).astype(BF16)
    logits = _dot_nt(w_hi, h_hi) + _dot_nt(w_hi, h_lo) + _dot_nt(w_lo, h_hi) + br_ref[:, 0:1]
    gl = [logits[g:g + 1, :] for g in range(MOE_GROUPS)]
    gmax = gl[0]
    gsel = jnp.zeros((1, tm), jnp.int32)
    for g in range(1, MOE_GROUPS):
        better = gl[g] > gmax
        gsel = jnp.where(better, g, gsel)
        gmax = jnp.where(better, gl[g], gmax)
    gden = gl[0] * 0.0
    for g in range(MOE_GROUPS):
        gden = gden + jnp.exp(gl[g] - gmax)
    g_w = 1.0 / gden
    el = []
    for e in range(MOE_EPG):
        v = logits[MOE_GROUPS + e:MOE_GROUPS + e + 1, :]
        for g in range(1, MOE_GROUPS):
            row = MOE_GROUPS + g * MOE_EPG + e
            v = jnp.where(gsel == g, logits[row:row + 1, :], v)
        el.append(v)
    v1 = el[0]
    i1 = jnp.zeros((1, tm), jnp.int32)
    for e in range(1, MOE_EPG):
        better = el[e] > v1
        i1 = jnp.where(better, e, i1)
        v1 = jnp.where(better, el[e], v1)
    v2 = jnp.full((1, tm), -jnp.inf, F32)
    i2 = jnp.zeros((1, tm), jnp.int32)
    for e in range(MOE_EPG):
        better = (el[e] > v2) & (i1 != e)
        i2 = jnp.where(better, e, i2)
        v2 = jnp.where(better, el[e], v2)
    ex = jnp.exp(v2 - v1)
    p1 = 1.0 / (1.0 + ex)
    e1 = gsel * MOE_EPG + i1
    e2 = gsel * MOE_EPG + i2
    erow = lax.broadcasted_iota(jnp.int32, (MOE_EXPERTS, tm), 0)
    oh1 = (erow == e1).astype(F32)
    oh2 = (erow == e2).astype(F32)
    both = oh1 + oh2
    n_grp = tm // LANE
    stacked = jnp.concatenate([both[:, g * LANE:(g + 1) * LANE] for g in range(n_grp)], axis=0)
    within = jnp.dot(stacked.astype(BF16), tri_ref[...], preferred_element_type=F32)
    grp_count = jnp.sum(stacked, axis=1, keepdims=True)
    running = carry[:, 0:1]
    pieces = []
    for g in range(n_grp):
        pieces.append(within[g * MOE_EXPERTS:(g + 1) * MOE_EXPERTS, :] + running)
        running = running + grp_count[g * MOE_EXPERTS:(g + 1) * MOE_EXPERTS, :]
    prefix = jnp.concatenate(pieces, axis=1)
    rank1 = jnp.sum(oh1 * prefix, axis=0, keepdims=True).astype(jnp.int32)
    rank2 = jnp.sum(oh2 * prefix, axis=0, keepdims=True).astype(jnp.int32)
    carry[...] = jnp.broadcast_to(running, carry.shape)
    zi = jnp.zeros((1, tm), jnp.int32)
    ids_ref[...] = jnp.concatenate([e1, e2, rank1, rank2, zi, zi, zi, zi], axis=0)
    wrow = lax.broadcasted_iota(jnp.int32, (LANE, tm), 0)
    wts_ref[...] = jnp.where(wrow == 0, p1 * g_w, jnp.where(wrow == 1, ex * p1 * g_w, 0.0)).T
    cnt_ref[...] = carry[...]


def _router(x2, mod3, w_route, b_route, tri_excl, seq):
    t, d = x2.shape
    tm = TM_PROJ
    tpb = seq // tm
    nr = w_route.shape[0]
    const = lambda shape: pl.BlockSpec(shape, lambda i: (0,) * len(shape))
    modspec = lambda k: pl.BlockSpec((None, 1, d), lambda i: ((i // tpb) * 6 + k, 0, 0))
    return pl.pallas_call(
        _router_kernel,
        grid=(t // tm,),
        in_specs=[pl.BlockSpec((tm, d), lambda i: (i, 0)), modspec(4), modspec(3),
                  const((nr, d)), const((nr, LANE)), const((LANE, LANE))],
        out_specs=[pl.BlockSpec((N_SLAB, tm, SLAB), lambda i: (0, i, 0)),
                   pl.BlockSpec((8, tm), lambda i: (0, i)),
                   pl.BlockSpec((tm, LANE), lambda i: (i, 0)),
                   const((MOE_EXPERTS, LANE))],
        out_shape=[jax.ShapeDtypeStruct((N_SLAB, t, SLAB), jnp.uint32),
                   jax.ShapeDtypeStruct((8, t), jnp.int32),
                   jax.ShapeDtypeStruct((t, LANE), F32),
                   jax.ShapeDtypeStruct((MOE_EXPERTS, LANE), F32)],
        scratch_shapes=[pltpu.VMEM((MOE_EXPERTS, LANE), F32)],
        compiler_params=_cparams(("arbitrary",)),
        name="moe_router",
    )(x2, mod3, mod3, w_route, b_route, tri_excl)


def _sc_mesh():
    return plsc.VectorSubcoreMesh(core_axis_name="core", subcore_axis_name="subcore")


def _slab_rows(idx, n_rows):
    return (idx[None, :] + (jnp.arange(N_SLAB, dtype=jnp.int32) * n_rows)[:, None]).reshape(-1)


def _dispatch(slot1, slot2, h_slabs):
    n_slab, t, d = h_slabs.shape
    n_out = 2 * t
    xs = _scatter_rows(h_slabs.reshape(n_slab * t, d), _slab_rows(slot1, n_out), _slab_rows(slot2, n_out),
                       n_slab * n_out)
    return xs.reshape(n_slab, n_out, d)


def _scatter_rows(src, idx1, idx2, n_out):
    t, d = src.shape
    win = SC_WINDOW

    @pl.kernel(out_type=jax.ShapeDtypeStruct((n_out, d), src.dtype), mesh=_sc_mesh(), name="moe_dispatch_sc")
    def scatter_rows(x_hbm, i1_hbm, i2_hbm, o_hbm):
        def body(x_vmem, i1_vmem, i2_vmem):
            pltpu.sync_copy(x_vmem, o_hbm.at[i1_vmem.at[0]])
            pltpu.sync_copy(x_vmem, o_hbm.at[i2_vmem.at[0]])

        pltpu.emit_pipeline(
            body,
            grid=(t // win,),
            in_specs=[pl.BlockSpec((win, d), lambda i: (i, 0)),
                      pl.BlockSpec((1, win), lambda i: (0, i)),
                      pl.BlockSpec((1, win), lambda i: (0, i))],
            out_specs=[],
            core_axis_name=("core", "subcore"),
            dimension_semantics=(pltpu.PARALLEL,),
        )(x_hbm, i1_hbm, i2_hbm)

    return scatter_rows(src, idx1.reshape(1, t), idx2.reshape(1, t))


def _gather_rows(src, idx):
    m = idx.shape[0]
    d = src.shape[1]
    win = SC_WINDOW

    @pl.kernel(out_type=jax.ShapeDtypeStruct((m, d), src.dtype), mesh=_sc_mesh(), name="moe_gather_sc")
    def gather(x_hbm, i_hbm, o_hbm):
        def body(i_vmem, o_vmem):
            pltpu.sync_copy(x_hbm.at[i_vmem.at[0]], o_vmem)

        pltpu.emit_pipeline(
            body,
            grid=(m // win,),
            in_specs=[pl.BlockSpec((1, win), lambda i: (0, i))],
            out_specs=[pl.BlockSpec((win, d), lambda i: (i, 0))],
            core_axis_name=("core", "subcore"),
            dimension_semantics=(pltpu.PARALLEL,),
        )(i_hbm, o_hbm)

    return gather(src, idx.reshape(1, m))


def _expert_kernel(tile_ref, exp_ref, lo_ref, hi_ref, xs_ref, w1_ref, w3_ref, w2_ref, ys_ref, w1_scr, w3_scr, w2_scr):
    s = pl.program_id(0)
    prev = jnp.maximum(s - 1, 0)
    new_expert = (s == 0) | (exp_ref[s] != exp_ref[prev])
    new_tile = (s == 0) | (tile_ref[s] != tile_ref[prev])

    @pl.when(new_expert)
    def _():
        w1_scr[...] = w1_ref[...].astype(BF16)
        w3_scr[...] = w3_ref[...].astype(BF16)
        w2_scr[...] = w2_ref[...].astype(BF16)

    x = _unpack_bf16_pairs(jnp.concatenate([xs_ref[k] for k in range(N_SLAB)], axis=-1)).astype(BF16)
    a = jnp.dot(x, w1_scr[...], preferred_element_type=F32)
    b = jnp.dot(x, w3_scr[...], preferred_element_type=F32)
    act = _silu(a) * b
    y = _pack_bf16_pairs(jnp.dot(act.astype(BF16), w2_scr[...], preferred_element_type=F32))
    row = lax.broadcasted_iota(jnp.int32, (y.shape[0], SLAB), 0)
    mine = (row >= lo_ref[s]) & (row < hi_ref[s])

    @pl.when(new_tile)
    def _():
        for k in range(N_SLAB):
            ys_ref[k] = jnp.where(mine, y[:, k * SLAB:(k + 1) * SLAB], jnp.uint32(0))

    @pl.when(jnp.logical_not(new_tile))
    def _():
        for k in range(N_SLAB):
            ys_ref[k] = jnp.where(mine, y[:, k * SLAB:(k + 1) * SLAB], ys_ref[k])


def _experts(step_tile, step_expert, step_lo, step_hi, xs, w1, w3, w2, layer):
    n_slab, ns, slab = xs.shape
    d = w1.shape[1]
    ff = w1.shape[2]
    n_steps = step_tile.shape[0]
    base = layer * MOE_EXPERTS
    grid_spec = pltpu.PrefetchScalarGridSpec(
        num_scalar_prefetch=4,
        grid=(n_steps,),
        in_specs=[pl.BlockSpec((n_slab, TM_X, slab), lambda s, tl, ex, lo, hi: (0, tl[s], 0)),
                  pl.BlockSpec((None, d, ff), lambda s, tl, ex, lo, hi: (base + ex[s], 0, 0)),
                  pl.BlockSpec((None, d, ff), lambda s, tl, ex, lo, hi: (base + ex[s], 0, 0)),
                  pl.BlockSpec((None, ff, d), lambda s, tl, ex, lo, hi: (base + ex[s], 0, 0))],
        out_specs=pl.BlockSpec((n_slab, TM_X, slab), lambda s, tl, ex, lo, hi: (0, tl[s], 0)),
        scratch_shapes=[pltpu.VMEM((d, ff), BF16), pltpu.VMEM((d, ff), BF16), pltpu.VMEM((ff, d), BF16)],
    )
    return pl.pallas_call(
        _expert_kernel,
        grid_spec=grid_spec,
        out_shape=jax.ShapeDtypeStruct((n_slab, ns, slab), xs.dtype),
        compiler_params=_cparams(("arbitrary",)),
        name="moe_experts",
    )(step_tile, step_expert, step_lo, step_hi, xs, w1, w3, w2)


def _combine_kernel(x_ref, gate_ref, fw_ref, wcol_ref, y1_ref, y2_ref, o_ref, *, final):
    w_first = wcol_ref[:, 0:1]
    w_second = wcol_ref[:, 1:2]
    y_first = _unpack_bf16_pairs(jnp.concatenate([y1_ref[k] for k in range(N_SLAB)], axis=-1))
    y_second = _unpack_bf16_pairs(jnp.concatenate([y2_ref[k] for k in range(N_SLAB)], axis=-1))
    moe = w_first * y_first + w_second * y_second
    x = x_ref[...] + gate_ref[...] * moe
    if final:
        x = x * lax.rsqrt(jnp.mean(x * x, axis=-1, keepdims=True) + EPS) * fw_ref[...]
    o_ref[...] = x


def _combine(x2, mod3, final_w_row, wcol, gathered, seq, final):
    t, d = x2.shape
    tm = TM_COMB
    tpb = seq // tm
    nblk = t // tm
    yspec = lambda off: pl.BlockSpec((N_SLAB, tm, SLAB), lambda i: (0, i + off, 0))
    return pl.pallas_call(
        functools.partial(_combine_kernel, final=final),
        grid=(nblk,),
        in_specs=[pl.BlockSpec((tm, d), lambda i: (i, 0)),
                  pl.BlockSpec((None, 1, d), lambda i: ((i // tpb) * 6 + 5, 0, 0)),
                  pl.BlockSpec((1, d), lambda i: (0, 0)),
                  pl.BlockSpec((tm, LANE), lambda i: (i, 0)),
                  yspec(0), yspec(nblk)],
        out_specs=pl.BlockSpec((tm, d), lambda i: (i, 0)),
        out_shape=jax.ShapeDtypeStruct((t, d), F32),
        compiler_params=_cparams(("parallel",)),
        name="moe_combine",
    )(x2, mod3, final_w_row, wcol, gathered, gathered)


def _moe(x2, mod3, final_w_row, w_route, b_route, tri_excl, w1, w3, w2, layer, seq, final):
    t, d = x2.shape
    h3, ids, wcol, counts = _router(x2, mod3, w_route, b_route, tri_excl, seq)
    cnt = counts[:, 0].astype(jnp.int32)
    ends = jnp.cumsum(cnt)
    offs = ends - cnt
    experts = jnp.arange(MOE_EXPERTS, dtype=jnp.int32)
    pick = lambda table, idx: jnp.sum(jnp.where(idx[:, None] == experts[None, :], table[None, :], 0), axis=1)
    slot1 = pick(offs, ids[0]) + ids[2]
    slot2 = pick(offs, ids[1]) + ids[3]
    n_tiles = 2 * t // TM_X
    first_tile = offs // TM_X
    n_vis = jnp.where(cnt > 0, (ends - 1) // TM_X - first_tile + 1, 0)
    cum = jnp.cumsum(n_vis)
    step = jnp.arange(n_tiles + MOE_EXPERTS, dtype=jnp.int32)
    step_expert = jnp.minimum(jnp.sum(step[:, None] >= cum[None, :], axis=1), MOE_EXPERTS - 1).astype(jnp.int32)
    valid = step < cum[-1]
    step_tile = jnp.where(valid, pick(first_tile - (cum - n_vis), step_expert) + step, n_tiles - 1)
    step_lo = jnp.where(valid, jnp.clip(pick(offs, step_expert) - step_tile * TM_X, 0, TM_X), 0)
    step_hi = jnp.where(valid, jnp.clip(pick(ends, step_expert) - step_tile * TM_X, 0, TM_X), 0)
    xs = _dispatch(slot1, slot2, h3)
    ys = _experts(step_tile.astype(jnp.int32), step_expert, step_lo.astype(jnp.int32), step_hi.astype(jnp.int32),
                  xs, w1, w3, w2, layer)
    n_sorted = ys.shape[1]
    gathered = _gather_rows(ys.reshape(N_SLAB * n_sorted, SLAB), _slab_rows(jnp.concatenate([slot1, slot2]), n_sorted))
    gathered = gathered.reshape(N_SLAB, n_sorted, SLAB)
    return _combine(x2, mod3, final_w_row, wcol, gathered, seq, final)


def kernel(x, c, positions, ada_w, ada_b, w_in, s5_lam_re, s5_lam_im, s5_b_re, s5_b_im, s5_c_re, s5_c_im, s5_d, s5_log_dt, s5_w_glu, hg_lb_logits, hg_norm_w, m2_conv_w, m2_conv_b, m2_dt_bias, m2_a_log, m2_d, m2_norm_w, w_branch, w_gate, b_gate, w_out, moe_w_group, moe_b_group, moe_w_expert, moe_b_expert, moe_w1, moe_w3, moe_w2, final_norm_w):
    bsz, seq, d = x.shape
    t = bsz * seq
    depth = ada_w.shape[0]
    assert seq % TM_PROJ == 0 and seq % C_RET == 0 and seq % C_SSD == 0 and seq % C_HG == 0
    x2 = x.reshape(t, d).astype(F32)

    c_pad = jnp.zeros((8, d), F32).at[:bsz].set(c.astype(F32))
    mod_all = _ada_mod(c_pad.T, ada_w.astype(F32), ada_b.astype(F32), bsz)

    half = RET_DK // 2
    inv_freq = ROPE_BASE ** (-jnp.arange(half, dtype=F32) / half)
    invf_col = jnp.broadcast_to(inv_freq[:, None], (half, LANE))
    expand = np.tile(np.eye(half, dtype=np.float32), (1, 2 * RET_HEADS))
    sign = np.tile(np.concatenate([-np.ones(half), np.ones(half)]), RET_HEADS)[None, :].astype(np.float32)
    cos_t, sin_t = _rope_tables(positions.reshape(1, t).astype(jnp.int32), invf_col,
                                jnp.asarray(expand, BF16), jnp.asarray(expand * sign, BF16))
    cos3 = cos_t.reshape(bsz, seq, -1)
    sin3 = sin_t.reshape(bsz, seq, -1)

    lb_cum = jnp.cumsum(jax.nn.softmax(hg_lb_logits.astype(F32), axis=0), axis=0)
    hg_lb = lb_cum - lb_cum[:1]
    tri_ssd = jnp.asarray(np.tril(np.ones((C_SSD, C_SSD), np.float32)), BF16)
    tri_excl = jnp.asarray(np.triu(np.ones((LANE, LANE), np.float32), 1), BF16)
    final_w_row = final_norm_w.astype(F32)[None, :]
    w_pad = jnp.concatenate([w_in.astype(BF16), jnp.zeros((depth, d, IN_W_PAD - IN_W), BF16)], axis=-1)
    w_glu_bf = s5_w_glu.astype(BF16)
    w_branch_bf = w_branch.astype(BF16)
    w_gate_bf = w_gate.astype(BF16)
    w_out_bf = w_out.astype(BF16)
    b_gate3 = b_gate.astype(F32).reshape(depth, 1, -1)
    moe_w1_all = moe_w1.astype(F32).reshape(depth * MOE_EXPERTS, d, MOE_FF)
    moe_w3_all = moe_w3.astype(F32).reshape(depth * MOE_EXPERTS, d, MOE_FF)
    moe_w2_all = moe_w2.astype(F32).reshape(depth * MOE_EXPERTS, MOE_FF, d)

    for layer in range(depth):
        mod3 = mod_all[layer, :bsz].reshape(bsz * 6, 1, d)
        p, u_s5 = _in_proj(x2, mod3, w_pad, layer, seq)
        p3 = p.reshape(bsz, seq, IN_W_PAD)

        ops = _s5_operators(s5_lam_re[layer], s5_lam_im[layer], s5_b_re[layer], s5_b_im[layer],
                            s5_c_re[layer], s5_c_im[layer], s5_d[layer], s5_log_dt[layer])
        y_s5 = _s5_scan(u_s5.reshape(bsz, seq, BRANCH_W), *ops).reshape(t, BRANCH_W)

        lb = hg_lb[layer][None, :]
        y_hg = _hgrn2(p3, jnp.log(lb), jnp.log1p(-lb), hg_norm_w[layer].astype(F32)[None, :]).reshape(t, BRANCH_W)

        y_ret = _retention(p3, cos3, sin3).reshape(t, BRANCH_W)

        pad8 = lambda v: jnp.zeros((1, LANE), F32).at[0, :M2_HEADS].set(v.astype(F32))
        y_m2 = _ssd(p3, tri_ssd, m2_conv_w[layer].astype(F32), m2_conv_b[layer].astype(F32)[None, :],
                    pad8(m2_dt_bias[layer]), pad8(m2_a_log[layer]),
                    jnp.repeat(m2_d[layer].astype(F32), M2_HEADDIM)[None, :],
                    m2_norm_w[layer].astype(F32)[None, :]).reshape(t, BRANCH_W)

        nr = 40
        w_route = jnp.zeros((nr, d), F32).at[:MOE_GROUPS].set(moe_w_group[layer].astype(F32).T)
        w_route = w_route.at[MOE_GROUPS:MOE_GROUPS + MOE_EXPERTS].set(moe_w_expert[layer].astype(F32).T)
        b_route = jnp.zeros((nr, LANE), F32).at[:MOE_GROUPS, 0].set(moe_b_group[layer].astype(F32))
        b_route = b_route.at[MOE_GROUPS:MOE_GROUPS + MOE_EXPERTS, 0].set(moe_b_expert[layer].astype(F32))
        x2 = _merge(x2, mod3, y_s5, y_hg, y_ret, y_m2, w_glu_bf, w_branch_bf, w_gate_bf, b_gate3, w_out_bf,
                    layer, seq)
        x2 = _moe(x2, mod3, final_w_row, w_route, b_route, tri_excl, moe_w1_all, moe_w3_all, moe_w2_all,
                  layer, seq, final=(layer == depth - 1))
    return x2.reshape(bsz, seq, d)
```

```python
import functools
import math

import numpy as np
import jax
import jax.numpy as jnp
from jax import lax
from jax.experimental import pallas as pl
from jax.experimental.pallas import tpu as pltpu
from jax.experimental.pallas import tpu_sc as plsc

F32 = jnp.float32
BF16 = jnp.bfloat16
HIGHEST = lax.Precision.HIGHEST

D_MODEL = 1024
DEPTH = 2
BRANCH_W = 512
EPS = 1e-6
S5_GROUPS = 32
S5_CH = 16
S5_STATE = 64
S5_MAX_REAL = -1e-4
S5_BLOCK = 16
S5_SEQ_PER_STEP = 2
HG_HEADS = 4
HG_DK = 128
RET_HEADS = 4
RET_DK = 64
RET_DV = 128
ROPE_BASE = 10000.0
M2_HEADS = 8
M2_HEADDIM = 64
M2_GROUPS = 2
M2_STATE = 128
M2_CONV = 4
MOE_GROUPS = 4
MOE_EPG = 8
MOE_EXPERTS = MOE_GROUPS * MOE_EPG
MOE_FF = 256

COL_S5, COL_HQ, COL_HF, COL_HI, COL_HG = 0, 512, 1024, 1536, 2048
COL_RQ, COL_RK, COL_RV, COL_RG = 2560, 2816, 3072, 3584
COL_MZ, COL_MXS, COL_MBC, COL_MDT = 4096, 4608, 5120, 5632
IN_W = 5640
IN_W_PAD = 5888

LANE = 128
VMEM_LIMIT = 56 * 1024 * 1024

TM_PROJ = 1024
TN_PROJ = 1024
TM_INPROJ = 512
LOG2_E = 1.4426950408889634
C_RET = 512
C_SSD = 256
C_HG = 128
TM_X = 512
X_SUB = 128
TM_COMB = 1024
SC_WINDOW = 128
SLAB = 256
N_SLAB = D_MODEL // 2 // SLAB


def _cparams(sem):
    return pltpu.CompilerParams(dimension_semantics=sem, vmem_limit_bytes=VMEM_LIMIT)


def _silu(v):
    return v * jax.nn.sigmoid(v)


def _dot_nt(a, b, **kw):
    return lax.dot_general(a, b, (((1,), (1,)), ((), ())), preferred_element_type=F32, **kw)


def _dot_tn(a, b, **kw):
    return lax.dot_general(a, b, (((0,), (0,)), ((), ())), preferred_element_type=F32, **kw)


def _ada_kernel(ct_ref, w_ref, b_ref, o_ref, *, n_rows):
    cond_t = _silu(ct_ref[...])
    w = w_ref[...]
    rows = [jnp.sum(w * cond_t[:, b:b + 1], axis=0, keepdims=True) for b in range(n_rows)]
    rows += [jnp.zeros_like(rows[0])] * (cond_t.shape[1] - n_rows)
    o_ref[...] = jnp.concatenate(rows, axis=0) + b_ref[...]


def _ada_mod(c_pad_t, ada_w, ada_b, n_rows):
    depth, d, n = ada_w.shape
    tn = 1536
    return pl.pallas_call(
        functools.partial(_ada_kernel, n_rows=n_rows),
        grid=(depth, n // tn),
        in_specs=[pl.BlockSpec((d, 8), lambda l, j: (0, 0)),
                  pl.BlockSpec((None, d, tn), lambda l, j: (l, 0, j)),
                  pl.BlockSpec((None, 1, tn), lambda l, j: (l, 0, j))],
        out_specs=pl.BlockSpec((None, 8, tn), lambda l, j: (l, 0, j)),
        out_shape=jax.ShapeDtypeStruct((depth, 8, n), F32),
        compiler_params=_cparams(("parallel", "parallel")),
        name="ada_mod",
    )(c_pad_t, ada_w, ada_b.reshape(depth, 1, n))


def _pack_bf16_pairs(x):
    n = x.shape[1] // 2
    lo = pltpu.bitcast(x[:, :n].astype(BF16).astype(F32), jnp.uint32) >> 16
    hi = pltpu.bitcast(x[:, n:].astype(BF16).astype(F32), jnp.uint32)
    return hi | lo


def _unpack_bf16_pairs(w):
    lo = pltpu.bitcast(w << 16, F32)
    hi = pltpu.bitcast(w & jnp.uint32(0xFFFF0000), F32)
    return jnp.concatenate([lo, hi], axis=-1)


def _modulated_norm(x, scale, shift):
    ms = jnp.mean(x * x, axis=-1, keepdims=True)
    return x * lax.rsqrt(ms + EPS) * (1.0 + scale) + shift


def _inproj_kernel(x_ref, sc_ref, sh_ref, w_ref, o_ref, u_ref):
    h = _modulated_norm(x_ref[...], sc_ref[...], sh_ref[...]).astype(BF16)
    n_total = o_ref.shape[1]
    for n0 in range(0, n_total, TN_PROJ):
        n1 = min(n0 + TN_PROJ, n_total)
        p = jnp.dot(h, w_ref[:, n0:n1], preferred_element_type=F32)
        o_ref[:, n0:n1] = p.astype(o_ref.dtype)
        if n0 == 0:
            u_ref[...] = p[:, COL_S5:COL_S5 + BRANCH_W]


def _in_proj(x2, mod3, w_pad, layer, seq):
    t, d = x2.shape
    tm = TM_INPROJ
    tpb = seq // tm
    assert COL_S5 + BRANCH_W <= TN_PROJ
    return pl.pallas_call(
        _inproj_kernel,
        grid=(t // tm,),
        in_specs=[pl.BlockSpec((tm, d), lambda i: (i, 0)),
                  pl.BlockSpec((None, 1, d), lambda i: ((i // tpb) * 6 + 1, 0, 0)),
                  pl.BlockSpec((None, 1, d), lambda i: ((i // tpb) * 6 + 0, 0, 0)),
                  pl.BlockSpec((None, d, IN_W_PAD), lambda i: (layer, 0, 0), pipeline_mode=pl.Buffered(1))],
        out_specs=[pl.BlockSpec((tm, IN_W_PAD), lambda i: (i, 0)),
                   pl.BlockSpec((tm, BRANCH_W), lambda i: (i, 0))],
        out_shape=[jax.ShapeDtypeStruct((t, IN_W_PAD), BF16), jax.ShapeDtypeStruct((t, BRANCH_W), F32)],
        compiler_params=_cparams(("parallel",)),
        name="in_proj",
    )(x2, mod3, mod3, w_pad)


def _rope_kernel(pos_ref, invf_ref, ecos_ref, esin_ref, cos_ref, sin_ref):
    ang = invf_ref[:, 0:1] * pos_ref[...].astype(F32)
    def spread(values, e_ref):
        hi = values.astype(BF16)
        rest = values - hi.astype(F32)
        mid = rest.astype(BF16)
        lo = (rest - mid.astype(F32)).astype(BF16)
        e = e_ref[...]
        return _dot_tn(hi, e) + _dot_tn(mid, e) + _dot_tn(lo, e)

    cos_ref[...] = spread(jnp.cos(ang), ecos_ref)
    sin_ref[...] = spread(jnp.sin(ang), esin_ref)


def _rope_tables(pos_row, invf_col, expand_cos, expand_sin):
    t = pos_row.shape[1]
    half, w = expand_cos.shape
    tm = 1024
    const = lambda shape: pl.BlockSpec(shape, lambda i: (0, 0))
    return pl.pallas_call(
        _rope_kernel,
        grid=(t // tm,),
        in_specs=[pl.BlockSpec((1, tm), lambda i: (0, i)), const((half, LANE)), const((half, w)), const((half, w))],
        out_specs=[pl.BlockSpec((tm, w), lambda i: (i, 0))] * 2,
        out_shape=[jax.ShapeDtypeStruct((t, w), F32)] * 2,
        compiler_params=_cparams(("parallel",)),
        name="rope_tables",
    )(pos_row, invf_col, expand_cos, expand_sin)


def _ret_kernel(q_ref, k_ref, v_ref, g_ref, cos_ref, sin_ref, o_ref, st_ref, dec_ref, *, chunk):
    @pl.when(pl.program_id(1) == 0)
    def _():
        st_ref[...] = jnp.zeros_like(st_ref)
        ti = lax.broadcasted_iota(jnp.int32, (chunk, chunk), 0)
        si = lax.broadcasted_iota(jnp.int32, (chunk, chunk), 1)
        lag = (ti - si).astype(F32)
        for h in range(RET_HEADS):
            log_gamma = math.log1p(-(2.0 ** (-5.0 - h)))
            dec_ref[h] = jnp.where(ti >= si, jnp.exp(jnp.minimum(lag * log_gamma, 0.0)), 0.0)

    cosf = cos_ref[...]
    sinf = sin_ref[...]
    width = RET_HEADS * RET_DK
    lane = lax.broadcasted_iota(jnp.int32, (chunk, width), 1)
    first_half = (lane % RET_DK) < (RET_DK // 2)

    def rope(t):
        partner = jnp.where(first_half, pltpu.roll(t, width - RET_DK // 2, 1), pltpu.roll(t, RET_DK // 2, 1))
        return t * cosf + partner * sinf

    q = rope(q_ref[...].astype(F32))
    k = rope(k_ref[...].astype(F32)) * (RET_DK ** -0.5)
    v = v_ref[...]
    g = g_ref[...].astype(F32)
    tcol = lax.broadcasted_iota(jnp.int32, (chunk, 1), 0).astype(F32)
    for h in range(RET_HEADS):
        log_gamma = math.log1p(-(2.0 ** (-5.0 - h)))
        qh = q[:, h * RET_DK:(h + 1) * RET_DK]
        kh = k[:, h * RET_DK:(h + 1) * RET_DK]
        vh = v[:, h * RET_DV:(h + 1) * RET_DV].astype(BF16)
        scores = _dot_nt(qh.astype(BF16), kh.astype(BF16)) * dec_ref[h]
        state = st_ref[h]
        q_in = qh * jnp.exp(log_gamma * (tcol + 1.0))
        o = (jnp.dot(scores.astype(BF16), vh, preferred_element_type=F32)
             + jnp.dot(q_in.astype(BF16), state.astype(BF16), preferred_element_type=F32))
        k_out = kh * jnp.exp(log_gamma * (chunk - 1.0 - tcol))
        st_ref[h] = math.exp(log_gamma * chunk) * state + _dot_tn(k_out.astype(BF16), vh)
        o = o * lax.rsqrt(jnp.mean(o * o, axis=-1, keepdims=True) + EPS)
        gh = g[:, h * RET_DV:(h + 1) * RET_DV]
        o_ref[:, h * RET_DV:(h + 1) * RET_DV] = (o * _silu(gh)).astype(o_ref.dtype)


def _retention(p3, cos3, sin3):
    b, seq, _ = p3.shape
    c = C_RET
    qk_w = RET_HEADS * RET_DK
    return pl.pallas_call(
        functools.partial(_ret_kernel, chunk=c),
        grid=(b, seq // c),
        in_specs=[pl.BlockSpec((None, c, qk_w), lambda i, j: (i, j, COL_RQ // qk_w)),
                  pl.BlockSpec((None, c, qk_w), lambda i, j: (i, j, COL_RK // qk_w)),
                  pl.BlockSpec((None, c, BRANCH_W), lambda i, j: (i, j, COL_RV // BRANCH_W)),
                  pl.BlockSpec((None, c, BRANCH_W), lambda i, j: (i, j, COL_RG // BRANCH_W)),
                  pl.BlockSpec((None, c, qk_w), lambda i, j: (i, j, 0)),
                  pl.BlockSpec((None, c, qk_w), lambda i, j: (i, j, 0))],
        out_specs=pl.BlockSpec((None, c, BRANCH_W), lambda i, j: (i, j, 0)),
        out_shape=jax.ShapeDtypeStruct((b, seq, BRANCH_W), BF16),
        scratch_shapes=[pltpu.VMEM((RET_HEADS, RET_DK, RET_DV), F32), pltpu.VMEM((RET_HEADS, c, c), F32)],
        compiler_params=_cparams(("parallel", "arbitrary")),
        name="retention",
    )(p3, p3, p3, p3, cos3, sin3)


def _ssd_kernel(z_ref, xs_ref, bc_ref, dt_ref, tri_ref, cw_ref, cb_ref, dtb_ref, alog_ref, dsk_ref, nw_ref,
                o_ref, xe_scr, st_ref, *, chunk):
    j = pl.program_id(1)
    width = 2 * BRANCH_W

    @pl.when(j == 0)
    def _():
        st_ref[...] = jnp.zeros_like(st_ref)
        xe_scr[0:8, :] = jnp.zeros((8, width), F32)

    @pl.when(j > 0)
    def _():
        xe_scr[0:8, :] = xe_scr[chunk:chunk + 8, :]

    xe_scr[8:, 0:BRANCH_W] = xs_ref[...].astype(F32)
    xe_scr[8:, BRANCH_W:] = bc_ref[...].astype(F32)
    conv = cb_ref[...] + cw_ref[M2_CONV - 1:M2_CONV, :] * xe_scr[8:, :]
    for tap in range(M2_CONV - 1):
        conv = conv + cw_ref[tap:tap + 1, :] * xe_scr[pl.ds(8 - (M2_CONV - 1) + tap, chunk), :]
    conv = _silu(conv)
    xs = conv[:, :BRANCH_W]
    bm = conv[:, BRANCH_W:BRANCH_W + M2_GROUPS * M2_STATE]
    cm = conv[:, BRANCH_W + M2_GROUPS * M2_STATE:]

    dt = jax.nn.softplus(dt_ref[...].astype(F32) + dtb_ref[...])
    da = dt * (-jnp.exp(alog_ref[...]))
    da_hi = da.astype(BF16)
    da_r = da - da_hi.astype(F32)
    da_mid = da_r.astype(BF16)
    da_lo = (da_r - da_mid.astype(F32)).astype(BF16)
    tri = tri_ref[...]
    a_cs = (jnp.dot(tri, da_hi, preferred_element_type=F32) + jnp.dot(tri, da_mid, preferred_element_type=F32)
            + jnp.dot(tri, da_lo, preferred_element_type=F32))
    a_cs = a_cs * LOG2_E
    a_cs_t = a_cs.T
    ti = lax.broadcasted_iota(jnp.int32, (chunk, chunk), 0)
    si = lax.broadcasted_iota(jnp.int32, (chunk, chunk), 1)
    causal = ti >= si
    hpg = M2_HEADS // M2_GROUPS
    ys = []
    for grp in range(M2_GROUPS):
        bm_g = bm[:, grp * M2_STATE:(grp + 1) * M2_STATE]
        cm_g = cm[:, grp * M2_STATE:(grp + 1) * M2_STATE]
        cb = _dot_nt(cm_g.astype(BF16), bm_g.astype(BF16))
        for hh in range(hpg):
            h = grp * hpg + hh
            col = a_cs[:, h:h + 1]
            row = a_cs_t[h:h + 1, :]
            lmat = jnp.where(causal, jnp.exp2(col - row), 0.0)
            xd = xs[:, h * M2_HEADDIM:(h + 1) * M2_HEADDIM] * dt[:, h:h + 1]
            state = st_ref[h]
            y = (jnp.dot((cb * lmat).astype(BF16), xd.astype(BF16), preferred_element_type=F32)
                 + jnp.dot((cm_g * jnp.exp2(col)).astype(BF16), state.astype(BF16), preferred_element_type=F32))
            a_last = a_cs[chunk - 1:chunk, h:h + 1]
            to_end = jnp.exp2(a_last - col)
            st_ref[h] = jnp.exp2(a_last) * state + _dot_tn(bm_g.astype(BF16), (xd * to_end).astype(BF16))
            ys.append(y)
    y = jnp.concatenate(ys, axis=-1) + dsk_ref[...] * xs
    y = y * _silu(z_ref[...].astype(F32))
    o_ref[...] = (y * lax.rsqrt(jnp.mean(y * y, axis=-1, keepdims=True) + EPS) * nw_ref[...]).astype(o_ref.dtype)


def _ssd(p3, tri, conv_w, conv_b, dt_bias_row, a_log_row, d_skip_row, norm_w_row):
    b, seq, _ = p3.shape
    c = C_SSD
    const = lambda shape: pl.BlockSpec(shape, lambda i, j: (0,) * len(shape))
    return pl.pallas_call(
        functools.partial(_ssd_kernel, chunk=c),
        grid=(b, seq // c),
        in_specs=[pl.BlockSpec((None, c, BRANCH_W), lambda i, j: (i, j, COL_MZ // BRANCH_W)),
                  pl.BlockSpec((None, c, BRANCH_W), lambda i, j: (i, j, COL_MXS // BRANCH_W)),
                  pl.BlockSpec((None, c, BRANCH_W), lambda i, j: (i, j, COL_MBC // BRANCH_W)),
                  pl.BlockSpec((None, c, LANE), lambda i, j: (i, j, COL_MDT // LANE)),
                  const((c, c)), const((M2_CONV, 2 * BRANCH_W)), const((1, 2 * BRANCH_W)),
                  const((1, LANE)), const((1, LANE)), const((1, BRANCH_W)), const((1, BRANCH_W))],
        out_specs=pl.BlockSpec((None, c, BRANCH_W), lambda i, j: (i, j, 0)),
        out_shape=jax.ShapeDtypeStruct((b, seq, BRANCH_W), BF16),
        scratch_shapes=[pltpu.VMEM((c + 8, 2 * BRANCH_W), F32),
                        pltpu.VMEM((M2_HEADS, M2_STATE, M2_HEADDIM), F32)],
        compiler_params=_cparams(("parallel", "arbitrary")),
        name="ssd",
    )(p3, p3, p3, p3, tri, conv_w, conv_b, dt_bias_row, a_log_row, d_skip_row, norm_w_row)


def _hg_tables(chunk):
    n_lev = int(math.log2(chunk))
    r = np.arange(chunk)[:, None]
    jj = np.arange(chunk)[None, :]
    tri = (jj <= r).astype(np.float32)
    x = r ^ jj
    levmap = np.where(r > jj, np.floor(np.log2(x + 0.5)), np.where(r == jj, -1, -2)).astype(np.int32)
    return tri, levmap, n_lev


def _hg_level_exponent(b, lev):
    rows, width = b.shape
    m = 1 << lev
    sub = 8
    if 2 * m >= sub:
        blocks = b.reshape(rows // (2 * m), 2 * m, width)
        mid = jnp.broadcast_to(blocks[:, m - 1:m, :], blocks.shape).reshape(rows, width)
    else:
        groups = b.reshape(rows // sub, sub, width)
        row_in_group = lax.broadcasted_iota(jnp.int32, groups.shape, 1)
        mid = None
        for start in range(0, sub, 2 * m):
            picked = jnp.broadcast_to(groups[:, start + m - 1:start + m, :], groups.shape)
            mid = picked if mid is None else jnp.where(row_in_group >= start, picked, mid)
        mid = mid.reshape(rows, width)
    return -jnp.abs(b - mid)


def _hg_kernel(q_ref, f_ref, i_ref, g_ref, sum_ref, lev_ref, llb_ref, l1m_ref, nw_ref, o_ref, st_ref,
               *, chunk, n_lev):
    @pl.when(pl.program_id(1) == 0)
    def _():
        st_ref[...] = jnp.zeros_like(st_ref)

    f = f_ref[...].astype(F32)
    y = jnp.exp(-jnp.abs(f))
    one_plus_y = 1.0 + y
    log_sig = jnp.minimum(f, 0.0) - jnp.log(one_plus_y)
    a = llb_ref[...]
    bb = l1m_ref[...] + log_sig
    log_f = jnp.maximum(a, bb) + jnp.log(1.0 + jnp.exp(-jnp.abs(a - bb)))
    k_all = jnp.exp(l1m_ref[...]) * (jnp.where(f >= 0.0, y, 1.0) / one_plus_y)
    q_all = _silu(q_ref[...].astype(F32))
    hi = log_f.astype(BF16)
    r1 = log_f - hi.astype(F32)
    mid = r1.astype(BF16)
    lo = (r1 - mid.astype(F32)).astype(BF16)
    tri = sum_ref[...]
    b_all = (jnp.dot(tri, hi, preferred_element_type=F32)
             + jnp.dot(tri, mid, preferred_element_type=F32)
             + jnp.dot(tri, lo, preferred_element_type=F32))
    b_all = b_all * LOG2_E
    to_end_all = b_all[chunk - 1:chunk, :] - b_all
    level_decay = [jnp.exp2(_hg_level_exponent(b_all, lev)) for lev in range(n_lev)]
    levmap = lev_ref[...]
    on_diag = levmap == -1
    on_level = [levmap == lev for lev in range(n_lev)]
    v_all = i_ref[...]
    g_all = g_ref[...].astype(F32)
    for h in range(HG_HEADS):
        sl = slice(h * HG_DK, (h + 1) * HG_DK)
        qh = q_all[:, sl]
        kh = k_all[:, sl]
        vh = v_all[:, sl].astype(BF16)
        b_h = b_all[:, sl]
        to_end = to_end_all[:, sl]
        amat = jnp.where(on_diag, _dot_nt(qh.astype(BF16), kh.astype(BF16)), 0.0)
        for lev in range(n_lev):
            e = level_decay[lev][:, sl]
            a_l = _dot_nt((qh * e).astype(BF16), (kh * e).astype(BF16))
            amat = jnp.where(on_level[lev], a_l, amat)
        state_t = st_ref[h]
        o = (jnp.dot(amat.astype(BF16), vh, preferred_element_type=F32)
             + _dot_nt((qh * jnp.exp2(b_h)).astype(BF16), state_t.astype(BF16)))
        k_end = kh * jnp.exp2(to_end)
        st_ref[h] = jnp.exp2(b_h[chunk - 1:chunk, :]) * state_t + _dot_tn(vh, k_end.astype(BF16))
        o = o * lax.rsqrt(jnp.mean(o * o, axis=-1, keepdims=True) + EPS) * nw_ref[...]
        o_ref[:, sl] = (o * _silu(g_all[:, sl])).astype(o_ref.dtype)


def _hgrn2(p3, log_lb_row, log1m_lb_row, norm_w_row):
    b, seq, _ = p3.shape
    c = C_HG
    tri, levmap, n_lev = _hg_tables(c)
    const = lambda shape: pl.BlockSpec(shape, lambda i, j: (0,) * len(shape))
    blk = lambda col: pl.BlockSpec((None, c, BRANCH_W), lambda i, j: (i, j, col // BRANCH_W))
    return pl.pallas_call(
        functools.partial(_hg_kernel, chunk=c, n_lev=n_lev),
        grid=(b, seq // c),
        in_specs=[blk(COL_HQ), blk(COL_HF), blk(COL_HI), blk(COL_HG),
                  const((c, c)), const((c, c)),
                  const((1, BRANCH_W)), const((1, BRANCH_W)), const((1, HG_DK))],
        out_specs=pl.BlockSpec((None, c, BRANCH_W), lambda i, j: (i, j, 0)),
        out_shape=jax.ShapeDtypeStruct((b, seq, BRANCH_W), BF16),
        scratch_shapes=[pltpu.VMEM((HG_HEADS, HG_DK, HG_DK), F32)],
        compiler_params=_cparams(("parallel", "arbitrary")),
        name="hgrn2",
    )(p3, p3, p3, p3, jnp.asarray(tri, BF16), jnp.asarray(levmap), log_lb_row, log1m_lb_row, norm_w_row)


def _expand_block_diag(comp_ref, e_ref, dst_ref, row_div, lane_div, causal=False):
    gq = LANE // S5_CH
    rows, ncols = dst_ref.shape
    step = 512
    for c0 in range(0, ncols, step):
        r1 = min(rows, c0 + step) if causal else rows
        row_grp = (lax.broadcasted_iota(jnp.int32, (r1, step), 0) // row_div) % gq
        lane_grp = ((lax.broadcasted_iota(jnp.int32, (r1, step), 1) + c0) // lane_div) % gq
        full = jnp.dot(comp_ref[0:r1, :], e_ref[:, c0:c0 + step], preferred_element_type=F32)
        dst_ref[0:r1, c0:c0 + step] = jnp.where(row_grp == lane_grp, full, 0.0).astype(dst_ref.dtype)


def _s5_kernel(u_ref, k2_ref, bc_ref, cc_ref, esc_ref, eb_ref, lam_ref, o_ref, tc_scr, tq_ref, bq_ref, cq_ref,
               x_scr, w_scr, s_scr, *, rows):
    nb = S5_BLOCK

    @pl.when(pl.program_id(1) == 0)
    def _():
        k2 = k2_ref[...]
        lane = lax.broadcasted_iota(jnp.int32, k2.shape, 1)
        for t in range(nb):
            shifted = k2 if t == 0 else jnp.where(lane >= t * S5_CH, pltpu.roll(k2, t * S5_CH, 1), 0.0)
            tc_scr[t * LANE:(t + 1) * LANE, :] = shifted.astype(tc_scr.dtype)
        _expand_block_diag(tc_scr, esc_ref, tq_ref, S5_CH, S5_CH, causal=True)
        _expand_block_diag(bc_ref, eb_ref, bq_ref, S5_CH, S5_STATE)
        _expand_block_diag(cc_ref, esc_ref, cq_ref, S5_STATE, S5_CH)

    n_seq = u_ref.shape[0]
    for b in range(n_seq):
        for t in range(nb):
            x_scr[b * rows:(b + 1) * rows, t * LANE:(t + 1) * LANE] = (
                u_ref[b, pl.ds(t, rows, stride=nb), :].astype(x_scr.dtype))
    x = x_scr[...]
    half = w_scr.shape[1] // 2
    w_scr[...] = jnp.dot(x, bq_ref[...], preferred_element_type=F32)
    lam_re = lam_ref[0:1, :]
    lam_im = lam_ref[1:2, :]

    def body(j, carry):
        out = []
        for b in range(n_seq):
            s_re, s_im = carry[2 * b], carry[2 * b + 1]
            r = b * rows + j
            s_scr[pl.ds(r, 1), 0:half] = s_re
            s_scr[pl.ds(r, 1), half:] = s_im
            w_re = w_scr[pl.ds(r, 1), 0:half]
            w_im = w_scr[pl.ds(r, 1), half:]
            out += [lam_re * s_re - lam_im * s_im + w_re, lam_re * s_im + lam_im * s_re + w_im]
        return tuple(out)

    zero = jnp.zeros((1, half), F32)
    lax.fori_loop(0, rows, body, (zero,) * (2 * n_seq))
    s_bf = s_scr[...].astype(BF16)
    pair = 2 * LANE
    for c0 in range(0, nb * LANE, pair):
        k_rows = c0 + pair
        y = (jnp.dot(x[:, :k_rows], tq_ref[0:k_rows, c0:c0 + pair], preferred_element_type=F32)
             + jnp.dot(s_bf, cq_ref[:, c0:c0 + pair], preferred_element_type=F32))
        for b in range(n_seq):
            for t in range(c0 // LANE, (c0 + pair) // LANE):
                o_ref[b, pl.ds(t, rows, stride=nb), :] = y[b * rows:(b + 1) * rows, t * LANE - c0:(t + 1) * LANE - c0]


def _s5_scan(p3, k2, bc, cc, lam16):
    batch, seq, _ = p3.shape
    nb = S5_BLOCK
    nq = BRANCH_W // LANE
    rows = seq // nb
    kdim = nb * LANE
    gq = LANE // S5_CH
    ncol = 2 * gq * S5_STATE
    e_sc = (np.eye(nb)[:, None, :, None, None] * np.eye(S5_CH)[None, :, None, None, :] * np.ones((1, 1, 1, gq, 1)))
    e_sc = e_sc.reshape(nb * S5_CH, nb * gq * S5_CH)
    e_b = (np.eye(2)[:, None, :, None, None] * np.eye(S5_STATE)[None, :, None, None, :] * np.ones((1, 1, 1, gq, 1)))
    e_b = e_b.reshape(2 * S5_STATE, ncol)
    full = lambda shape: pl.BlockSpec(shape, lambda q, b: (0,) * len(shape))
    per_q = lambda r, c: pl.BlockSpec((None, r, c), lambda q, b: (q, 0, 0))
    n_seq = S5_SEQ_PER_STEP if batch % S5_SEQ_PER_STEP == 0 else 1
    return pl.pallas_call(
        functools.partial(_s5_kernel, rows=rows),
        grid=(nq, batch // n_seq),
        in_specs=[pl.BlockSpec((n_seq, seq, LANE), lambda q, b: (b, 0, q)),
                  per_q(LANE, nb * S5_CH), per_q(kdim, 2 * S5_STATE), per_q(ncol, nb * S5_CH),
                  full(e_sc.shape), full(e_b.shape), per_q(2, ncol // 2)],
        out_specs=pl.BlockSpec((n_seq, seq, LANE), lambda q, b: (b, 0, q)),
        out_shape=jax.ShapeDtypeStruct((batch, seq, BRANCH_W), F32),
        scratch_shapes=[pltpu.VMEM((kdim, nb * S5_CH), BF16),
                        pltpu.VMEM((kdim, kdim), BF16), pltpu.VMEM((kdim, ncol), BF16), pltpu.VMEM((ncol, kdim), BF16),
                        pltpu.VMEM((n_seq * rows, kdim), BF16), pltpu.VMEM((n_seq * rows, ncol), F32),
                        pltpu.VMEM((n_seq * rows, ncol), F32)],
        compiler_params=_cparams(("parallel", "arbitrary")),
        name="s5_scan",
    )(p3, k2, bc, cc, jnp.asarray(e_sc, BF16), jnp.asarray(e_b, BF16), lam16)


def _s5_operators(lam_re, lam_im, b_re, b_im, c_re, c_im, d_skip, log_dt):
    nb = S5_BLOCK
    gq = LANE // S5_CH
    nq = S5_GROUPS // gq
    lam = lax.complex(jnp.minimum(lam_re.astype(F32), S5_MAX_REAL), lam_im.astype(F32))
    step = jnp.exp(log_dt.astype(F32))[:, None]
    z = lam * step
    lam_bar = jnp.exp(z)
    b_bar = ((lam_bar - 1.0) / lam)[..., None] * lax.complex(b_re.astype(F32), b_im.astype(F32))
    c_mat = lax.complex(c_re.astype(F32), c_im.astype(F32))
    pw = jnp.exp(z[..., None] * jnp.arange(nb + 1, dtype=F32))
    cp = c_mat[:, None, :, :] * pw[..., :nb].transpose(0, 2, 1)[:, :, None, :]
    cp = jnp.concatenate([cp.real, -cp.imag], axis=-1).reshape(S5_GROUPS, nb * S5_CH, 2 * S5_STATE)
    bri = jnp.concatenate([b_bar.real, b_bar.imag], axis=1)
    kern = jnp.einsum('gnk,gki->gin', cp, bri, precision=HIGHEST)
    skip = (jnp.asarray(np.concatenate([np.eye(S5_CH), np.zeros((S5_CH, (nb - 1) * S5_CH))], axis=1), F32)[None]
            * d_skip.astype(F32).reshape(S5_GROUPS, S5_CH, 1))
    k2 = (kern + skip).reshape(nq, gq * S5_CH, nb * S5_CH)
    pw_rev = jnp.exp(z[..., None] * jnp.asarray(np.arange(nb - 1, -1, -1), F32))
    binc = pw_rev[:, :, :, None] * b_bar[:, :, None, :]
    binc = jnp.stack([binc.real, binc.imag], axis=0).reshape(2, nq, gq, S5_STATE, nb, S5_CH)
    bc = binc.transpose(1, 4, 2, 5, 0, 3).reshape(nq, nb * LANE, 2 * S5_STATE)
    cm = c_mat.transpose(0, 2, 1)[:, :, None, :] * pw[..., 1:][:, :, :, None]
    cm = jnp.stack([cm.real, -cm.imag], axis=0).reshape(2, nq, gq * S5_STATE, nb * S5_CH)
    cc = cm.transpose(1, 0, 2, 3).reshape(nq, 2 * gq * S5_STATE, nb * S5_CH)
    lam_n = pw[..., nb].reshape(nq, gq * S5_STATE)
    lam16 = jnp.stack([lam_n.real, lam_n.imag], axis=1)
    return k2, bc.astype(BF16), cc.astype(BF16), lam16


def _merge_kernel(x_ref, sc_ref, sh_ref, gm_ref, ys5_ref, yhg_ref, yret_ref, ym2_ref,
                  wglu_ref, wbr_ref, wg_ref, bg_ref, wout_ref, o_ref):
    x = x_ref[...]
    d = x.shape[1]
    h = _modulated_norm(x, sc_ref[...], sh_ref[...]).astype(BF16)
    y_s5 = jax.nn.gelu(ys5_ref[...])
    y_s5 = y_s5 * jax.nn.sigmoid(jnp.dot(y_s5.astype(BF16), wglu_ref[...], preferred_element_type=F32))
    acc = jnp.zeros(x.shape, F32)
    for n, y in enumerate((y_s5, yhg_ref[...], yret_ref[...], ym2_ref[...])):
        gate = jax.nn.sigmoid(jnp.dot(h, wg_ref[:, n * d:(n + 1) * d], preferred_element_type=F32)
                              + bg_ref[:, n * d:(n + 1) * d])
        acc = acc + gate * jnp.dot(y.astype(BF16), wbr_ref[n], preferred_element_type=F32)
    o_ref[...] = x + gm_ref[...] * jnp.dot(acc.astype(BF16), wout_ref[...], preferred_element_type=F32)


def _merge(x2, mod3, ys5, yhg, yret, ym2, w_glu, w_branch, w_gate, b_gate, w_out, layer, seq):
    t, d = x2.shape
    tm = TM_PROJ
    tpb = seq // tm
    const = lambda shape: pl.BlockSpec((None,) + shape, lambda i: (layer,) + (0,) * len(shape),
                                       pipeline_mode=pl.Buffered(1))
    modspec = lambda k: pl.BlockSpec((None, 1, d), lambda i: ((i // tpb) * 6 + k, 0, 0))
    yspec = pl.BlockSpec((tm, BRANCH_W), lambda i: (i, 0))
    return pl.pallas_call(
        _merge_kernel,
        grid=(t // tm,),
        in_specs=[pl.BlockSpec((tm, d), lambda i: (i, 0)), modspec(1), modspec(0), modspec(2),
                  yspec, yspec, yspec, yspec,
                  const((BRANCH_W, BRANCH_W)), const((4, BRANCH_W, d)), const((d, 4 * d)), const((1, 4 * d)),
                  const((d, d))],
        out_specs=pl.BlockSpec((tm, d), lambda i: (i, 0)),
        out_shape=jax.ShapeDtypeStruct((t, d), F32),
        compiler_params=_cparams(("parallel",)),
        name="merge",
    )(x2, mod3, mod3, mod3, ys5, yhg, yret, ym2, w_glu, w_branch, w_gate, b_gate, w_out)


def _router_kernel(x_ref, sc_ref, sh_ref, wr_ref, br_ref, tri_ref, h_ref, ids_ref, wts_ref, cnt_ref, carry):
    i = pl.program_id(0)

    @pl.when(i == 0)
    def _():
        carry[...] = jnp.zeros_like(carry)

    h = _modulated_norm(x_ref[...], sc_ref[...], sh_ref[...])
    tm, d = h.shape
    packed = _pack_bf16_pairs(h)
    for k in range(N_SLAB):
        h_ref[k] = packed[:, k * SLAB:(k + 1) * SLAB]
    h_hi = h.astype(BF16)
    h_lo = (h - h_hi.astype(F32)).astype(BF16)
    w_r = wr_ref[...]
    w_hi = w_r.astype(BF16)
    w_lo = (w_r - w_hi.astype(F32)).astype(BF16)
    logits = _dot_nt(w_hi, h_hi) + _dot_nt(w_hi, h_lo) + _dot_nt(w_lo, h_hi) + br_ref[:, 0:1]
    gl = [logits[g:g + 1, :] for g in range(MOE_GROUPS)]
    gmax = gl[0]
    gsel = jnp.zeros((1, tm), jnp.int32)
    for g in range(1, MOE_GROUPS):
        better = gl[g] > gmax
        gsel = jnp.where(better, g, gsel)
        gmax = jnp.where(better, gl[g], gmax)
    gden = gl[0] * 0.0
    for g in range(MOE_GROUPS):
        gden = gden + jnp.exp(gl[g] - gmax)
    g_w = 1.0 / gden
    el = []
    for e in range(MOE_EPG):
        v = logits[MOE_GROUPS + e:MOE_GROUPS + e + 1, :]
        for g in range(1, MOE_GROUPS):
            row = MOE_GROUPS + g * MOE_EPG + e
            v = jnp.where(gsel == g, logits[row:row + 1, :], v)
        el.append(v)
    v1 = el[0]
    i1 = jnp.zeros((1, tm), jnp.int32)
    for e in range(1, MOE_EPG):
        better = el[e] > v1
        i1 = jnp.where(better, e, i1)
        v1 = jnp.where(better, el[e], v1)
    v2 = jnp.full((1, tm), -jnp.inf, F32)
    i2 = jnp.zeros((1, tm), jnp.int32)
    for e in range(MOE_EPG):
        better = (el[e] > v2) & (i1 != e)
        i2 = jnp.where(better, e, i2)
        v2 = jnp.where(better, el[e], v2)
    ex = jnp.exp(v2 - v1)
    p1 = 1.0 / (1.0 + ex)
    e1 = gsel * MOE_EPG + i1
    e2 = gsel * MOE_EPG + i2
    erow = lax.broadcasted_iota(jnp.int32, (MOE_EXPERTS, tm), 0)
    oh1 = (erow == e1).astype(F32)
    oh2 = (erow == e2).astype(F32)
    both = oh1 + oh2
    n_grp = tm // LANE
    stacked = jnp.concatenate([both[:, g * LANE:(g + 1) * LANE] for g in range(n_grp)], axis=0)
    within = jnp.dot(stacked.astype(BF16), tri_ref[...], preferred_element_type=F32)
    grp_count = jnp.sum(stacked, axis=1, keepdims=True)
    running = carry[:, 0:1]
    pieces = []
    for g in range(n_grp):
        pieces.append(within[g * MOE_EXPERTS:(g + 1) * MOE_EXPERTS, :] + running)
        running = running + grp_count[g * MOE_EXPERTS:(g + 1) * MOE_EXPERTS, :]
    prefix = jnp.concatenate(pieces, axis=1)
    rank1 = jnp.sum(oh1 * prefix, axis=0, keepdims=True).astype(jnp.int32)
    rank2 = jnp.sum(oh2 * prefix, axis=0, keepdims=True).astype(jnp.int32)
    carry[...] = jnp.broadcast_to(running, carry.shape)
    zi = jnp.zeros((1, tm), jnp.int32)
    ids_ref[...] = jnp.concatenate([e1, e2, rank1, rank2, zi, zi, zi, zi], axis=0)
    wrow = lax.broadcasted_iota(jnp.int32, (LANE, tm), 0)
    wts_ref[...] = jnp.where(wrow == 0, p1 * g_w, jnp.where(wrow == 1, ex * p1 * g_w, 0.0)).T
    cnt_ref[...] = carry[...]


def _router(x2, mod3, w_route, b_route, tri_excl, seq):
    t, d = x2.shape
    tm = TM_PROJ
    tpb = seq // tm
    nr = w_route.shape[0]
    const = lambda shape: pl.BlockSpec(shape, lambda i: (0,) * len(shape))
    modspec = lambda k: pl.BlockSpec((None, 1, d), lambda i: ((i // tpb) * 6 + k, 0, 0))
    return pl.pallas_call(
        _router_kernel,
        grid=(t // tm,),
        in_specs=[pl.BlockSpec((tm, d), lambda i: (i, 0)), modspec(4), modspec(3),
                  const((nr, d)), const((nr, LANE)), const((LANE, LANE))],
        out_specs=[pl.BlockSpec((N_SLAB, tm, SLAB), lambda i: (0, i, 0)),
                   pl.BlockSpec((8, tm), lambda i: (0, i)),
                   pl.BlockSpec((tm, LANE), lambda i: (i, 0)),
                   const((MOE_EXPERTS, LANE))],
        out_shape=[jax.ShapeDtypeStruct((N_SLAB, t, SLAB), jnp.uint32),
                   jax.ShapeDtypeStruct((8, t), jnp.int32),
                   jax.ShapeDtypeStruct((t, LANE), F32),
                   jax.ShapeDtypeStruct((MOE_EXPERTS, LANE), F32)],
        scratch_shapes=[pltpu.VMEM((MOE_EXPERTS, LANE), F32)],
        compiler_params=_cparams(("arbitrary",)),
        name="moe_router",
    )(x2, mod3, mod3, w_route, b_route, tri_excl)


def _sc_mesh():
    return plsc.VectorSubcoreMesh(core_axis_name="core", subcore_axis_name="subcore")


def _slab_rows(idx, n_rows):
    return (idx[None, :] + (jnp.arange(N_SLAB, dtype=jnp.int32) * n_rows)[:, None]).reshape(-1)


def _dispatch(slot1, slot2, h_slabs):
    n_slab, t, d = h_slabs.shape
    n_out = 2 * t
    xs = _scatter_rows(h_slabs.reshape(n_slab * t, d), _slab_rows(slot1, n_out), _slab_rows(slot2, n_out),
                       n_slab * n_out)
    return xs.reshape(n_slab, n_out, d)


def _scatter_rows(src, idx1, idx2, n_out):
    t, d = src.shape
    win = SC_WINDOW

    @pl.kernel(out_type=jax.ShapeDtypeStruct((n_out, d), src.dtype), mesh=_sc_mesh(), name="moe_dispatch_sc")
    def scatter_rows(x_hbm, i1_hbm, i2_hbm, o_hbm):
        def body(x_vmem, i1_vmem, i2_vmem):
            pltpu.sync_copy(x_vmem, o_hbm.at[i1_vmem.at[0]])
            pltpu.sync_copy(x_vmem, o_hbm.at[i2_vmem.at[0]])

        pltpu.emit_pipeline(
            body,
            grid=(t // win,),
            in_specs=[pl.BlockSpec((win, d), lambda i: (i, 0)),
                      pl.BlockSpec((1, win), lambda i: (0, i)),
                      pl.BlockSpec((1, win), lambda i: (0, i))],
            out_specs=[],
            core_axis_name=("core", "subcore"),
            dimension_semantics=(pltpu.PARALLEL,),
        )(x_hbm, i1_hbm, i2_hbm)

    return scatter_rows(src, idx1.reshape(1, t), idx2.reshape(1, t))


def _gather_rows(src, idx):
    m = idx.shape[0]
    d = src.shape[1]
    win = SC_WINDOW

    @pl.kernel(out_type=jax.ShapeDtypeStruct((m, d), src.dtype), mesh=_sc_mesh(), name="moe_gather_sc")
    def gather(x_hbm, i_hbm, o_hbm):
        def body(i_vmem, o_vmem):
            pltpu.sync_copy(x_hbm.at[i_vmem.at[0]], o_vmem)

        pltpu.emit_pipeline(
            body,
            grid=(m // win,),
            in_specs=[pl.BlockSpec((1, win), lambda i: (0, i))],
            out_specs=[pl.BlockSpec((win, d), lambda i: (i, 0))],
            core_axis_name=("core", "subcore"),
            dimension_semantics=(pltpu.PARALLEL,),
        )(i_hbm, o_hbm)

    return gather(src, idx.reshape(1, m))


def _expert_kernel(tile_ref, exp_ref, lo_ref, hi_ref, xs_ref, w1_ref, w3_ref, w2_ref, ys_ref, w1_scr, w3_scr, w2_scr):
    s = pl.program_id(0)
    prev = jnp.maximum(s - 1, 0)
    new_expert = (s == 0) | (exp_ref[s] != exp_ref[prev])
    new_tile = (s == 0) | (tile_ref[s] != tile_ref[prev])

    @pl.when(new_expert)
    def _():
        w1_scr[...] = w1_ref[...].astype(BF16)
        w3_scr[...] = w3_ref[...].astype(BF16)
        w2_scr[...] = w2_ref[...].astype(BF16)

    lo = lo_ref[s]
    hi = hi_ref[s]
    n_rows = xs_ref.shape[1]

    @pl.when(new_tile)
    def _():
        ys_ref[...] = jnp.zeros_like(ys_ref)

    def run_rows(r0, n):
        x = _unpack_bf16_pairs(jnp.concatenate([xs_ref[k, r0:r0 + n, :] for k in range(N_SLAB)], axis=-1)).astype(BF16)
        a = jnp.dot(x, w1_scr[...], preferred_element_type=F32)
        b = jnp.dot(x, w3_scr[...], preferred_element_type=F32)
        act = _silu(a) * b
        y = _pack_bf16_pairs(jnp.dot(act.astype(BF16), w2_scr[...], preferred_element_type=F32))
        row = lax.broadcasted_iota(jnp.int32, (n, SLAB), 0) + r0
        mine = (row >= lo) & (row < hi)
        for k in range(N_SLAB):
            ys_ref[k, r0:r0 + n, :] = jnp.where(mine, y[:, k * SLAB:(k + 1) * SLAB], ys_ref[k, r0:r0 + n, :])

    groups = (hi + (X_SUB - 1)) // X_SUB - lo // X_SUB
    whole = groups > 2

    @pl.when(whole)
    def _():
        run_rows(0, n_rows)

    for r0 in range(0, n_rows, X_SUB):
        @pl.when(jnp.logical_not(whole) & (lo < r0 + X_SUB) & (hi > r0))
        def _(r0=r0):
            run_rows(r0, X_SUB)


def _experts(step_tile, step_expert, step_lo, step_hi, xs, w1, w3, w2, layer):
    n_slab, ns, slab = xs.shape
    d = w1.shape[1]
    ff = w1.shape[2]
    n_steps = step_tile.shape[0]
    base = layer * MOE_EXPERTS
    grid_spec = pltpu.PrefetchScalarGridSpec(
        num_scalar_prefetch=4,
        grid=(n_steps,),
        in_specs=[pl.BlockSpec((n_slab, TM_X, slab), lambda s, tl, ex, lo, hi: (0, tl[s], 0)),
                  pl.BlockSpec((None, d, ff), lambda s, tl, ex, lo, hi: (base + ex[s], 0, 0)),
                  pl.BlockSpec((None, d, ff), lambda s, tl, ex, lo, hi: (base + ex[s], 0, 0)),
                  pl.BlockSpec((None, ff, d), lambda s, tl, ex, lo, hi: (base + ex[s], 0, 0))],
        out_specs=pl.BlockSpec((n_slab, TM_X, slab), lambda s, tl, ex, lo, hi: (0, tl[s], 0)),
        scratch_shapes=[pltpu.VMEM((d, ff), BF16), pltpu.VMEM((d, ff), BF16), pltpu.VMEM((ff, d), BF16)],
    )
    return pl.pallas_call(
        _expert_kernel,
        grid_spec=grid_spec,
        out_shape=jax.ShapeDtypeStruct((n_slab, ns, slab), xs.dtype),
        compiler_params=_cparams(("arbitrary",)),
        name="moe_experts",
    )(step_tile, step_expert, step_lo, step_hi, xs, w1, w3, w2)


def _combine_kernel(x_ref, gate_ref, fw_ref, wcol_ref, y1_ref, y2_ref, o_ref, *, final):
    w_first = wcol_ref[:, 0:1]
    w_second = wcol_ref[:, 1:2]
    y_first = _unpack_bf16_pairs(jnp.concatenate([y1_ref[k] for k in range(N_SLAB)], axis=-1))
    y_second = _unpack_bf16_pairs(jnp.concatenate([y2_ref[k] for k in range(N_SLAB)], axis=-1))
    moe = w_first * y_first + w_second * y_second
    x = x_ref[...] + gate_ref[...] * moe
    if final:
        x = x * lax.rsqrt(jnp.mean(x * x, axis=-1, keepdims=True) + EPS) * fw_ref[...]
    o_ref[...] = x


def _combine(x2, mod3, final_w_row, wcol, gathered, seq, final):
    t, d = x2.shape
    tm = TM_COMB
    tpb = seq // tm
    nblk = t // tm
    yspec = lambda off: pl.BlockSpec((N_SLAB, tm, SLAB), lambda i: (0, i + off, 0))
    return pl.pallas_call(
        functools.partial(_combine_kernel, final=final),
        grid=(nblk,),
        in_specs=[pl.BlockSpec((tm, d), lambda i: (i, 0)),
                  pl.BlockSpec((None, 1, d), lambda i: ((i // tpb) * 6 + 5, 0, 0)),
                  pl.BlockSpec((1, d), lambda i: (0, 0)),
                  pl.BlockSpec((tm, LANE), lambda i: (i, 0)),
                  yspec(0), yspec(nblk)],
        out_specs=pl.BlockSpec((tm, d), lambda i: (i, 0)),
        out_shape=jax.ShapeDtypeStruct((t, d), F32),
        compiler_params=_cparams(("parallel",)),
        name="moe_combine",
    )(x2, mod3, final_w_row, wcol, gathered, gathered)


def _moe(x2, mod3, final_w_row, w_route, b_route, tri_excl, w1, w3, w2, layer, seq, final):
    t, d = x2.shape
    h3, ids, wcol, counts = _router(x2, mod3, w_route, b_route, tri_excl, seq)
    cnt = counts[:, 0].astype(jnp.int32)
    ends = jnp.cumsum(cnt)
    offs = ends - cnt
    experts = jnp.arange(MOE_EXPERTS, dtype=jnp.int32)
    pick = lambda table, idx: jnp.sum(jnp.where(idx[:, None] == experts[None, :], table[None, :], 0), axis=1)
    slot1 = pick(offs, ids[0]) + ids[2]
    slot2 = pick(offs, ids[1]) + ids[3]
    n_tiles = 2 * t // TM_X
    first_tile = offs // TM_X
    n_vis = jnp.where(cnt > 0, (ends - 1) // TM_X - first_tile + 1, 0)
    cum = jnp.cumsum(n_vis)
    step = jnp.arange(n_tiles + MOE_EXPERTS, dtype=jnp.int32)
    step_expert = jnp.minimum(jnp.sum(step[:, None] >= cum[None, :], axis=1), MOE_EXPERTS - 1).astype(jnp.int32)
    valid = step < cum[-1]
    step_tile = jnp.where(valid, pick(first_tile - (cum - n_vis), step_expert) + step, n_tiles - 1)
    step_lo = jnp.where(valid, jnp.clip(pick(offs, step_expert) - step_tile * TM_X, 0, TM_X), 0)
    step_hi = jnp.where(valid, jnp.clip(pick(ends, step_expert) - step_tile * TM_X, 0, TM_X), 0)
    xs = _dispatch(slot1, slot2, h3)
    ys = _experts(step_tile.astype(jnp.int32), step_expert, step_lo.astype(jnp.int32), step_hi.astype(jnp.int32),
                  xs, w1, w3, w2, layer)
    n_sorted = ys.shape[1]
    gathered = _gather_rows(ys.reshape(N_SLAB * n_sorted, SLAB), _slab_rows(jnp.concatenate([slot1, slot2]), n_sorted))
    gathered = gathered.reshape(N_SLAB, n_sorted, SLAB)
    return _combine(x2, mod3, final_w_row, wcol, gathered, seq, final)


def kernel(x, c, positions, ada_w, ada_b, w_in, s5_lam_re, s5_lam_im, s5_b_re, s5_b_im, s5_c_re, s5_c_im, s5_d, s5_log_dt, s5_w_glu, hg_lb_logits, hg_norm_w, m2_conv_w, m2_conv_b, m2_dt_bias, m2_a_log, m2_d, m2_norm_w, w_branch, w_gate, b_gate, w_out, moe_w_group, moe_b_group, moe_w_expert, moe_b_expert, moe_w1, moe_w3, moe_w2, final_norm_w):
    bsz, seq, d = x.shape
    t = bsz * seq
    depth = ada_w.shape[0]
    assert seq % TM_PROJ == 0 and seq % C_RET == 0 and seq % C_SSD == 0 and seq % C_HG == 0
    x2 = x.reshape(t, d).astype(F32)

    c_pad = jnp.zeros((8, d), F32).at[:bsz].set(c.astype(F32))
    mod_all = _ada_mod(c_pad.T, ada_w.astype(F32), ada_b.astype(F32), bsz)

    half = RET_DK // 2
    inv_freq = ROPE_BASE ** (-jnp.arange(half, dtype=F32) / half)
    invf_col = jnp.broadcast_to(inv_freq[:, None], (half, LANE))
    expand = np.tile(np.eye(half, dtype=np.float32), (1, 2 * RET_HEADS))
    sign = np.tile(np.concatenate([-np.ones(half), np.ones(half)]), RET_HEADS)[None, :].astype(np.float32)
    cos_t, sin_t = _rope_tables(positions.reshape(1, t).astype(jnp.int32), invf_col,
                                jnp.asarray(expand, BF16), jnp.asarray(expand * sign, BF16))
    cos3 = cos_t.reshape(bsz, seq, -1)
    sin3 = sin_t.reshape(bsz, seq, -1)

    lb_cum = jnp.cumsum(jax.nn.softmax(hg_lb_logits.astype(F32), axis=0), axis=0)
    hg_lb = lb_cum - lb_cum[:1]
    tri_ssd = jnp.asarray(np.tril(np.ones((C_SSD, C_SSD), np.float32)), BF16)
    tri_excl = jnp.asarray(np.triu(np.ones((LANE, LANE), np.float32), 1), BF16)
    final_w_row = final_norm_w.astype(F32)[None, :]
    w_pad = jnp.concatenate([w_in.astype(BF16), jnp.zeros((depth, d, IN_W_PAD - IN_W), BF16)], axis=-1)
    w_glu_bf = s5_w_glu.astype(BF16)
    w_branch_bf = w_branch.astype(BF16)
    w_gate_bf = w_gate.astype(BF16)
    w_out_bf = w_out.astype(BF16)
    b_gate3 = b_gate.astype(F32).reshape(depth, 1, -1)
    moe_w1_all = moe_w1.astype(F32).reshape(depth * MOE_EXPERTS, d, MOE_FF)
    moe_w3_all = moe_w3.astype(F32).reshape(depth * MOE_EXPERTS, d, MOE_FF)
    moe_w2_all = moe_w2.astype(F32).reshape(depth * MOE_EXPERTS, MOE_FF, d)

    for layer in range(depth):
        mod3 = mod_all[layer, :bsz].reshape(bsz * 6, 1, d)
        p, u_s5 = _in_proj(x2, mod3, w_pad, layer, seq)
        p3 = p.reshape(bsz, seq, IN_W_PAD)

        ops = _s5_operators(s5_lam_re[layer], s5_lam_im[layer], s5_b_re[layer], s5_b_im[layer],
                            s5_c_re[layer], s5_c_im[layer], s5_d[layer], s5_log_dt[layer])
        y_s5 = _s5_scan(u_s5.reshape(bsz, seq, BRANCH_W), *ops).reshape(t, BRANCH_W)

        lb = hg_lb[layer][None, :]
        y_hg = _hgrn2(p3, jnp.log(lb), jnp.log1p(-lb), hg_norm_w[layer].astype(F32)[None, :]).reshape(t, BRANCH_W)

        y_ret = _retention(p3, cos3, sin3).reshape(t, BRANCH_W)

        pad8 = lambda v: jnp.zeros((1, LANE), F32).at[0, :M2_HEADS].set(v.astype(F32))
        y_m2 = _ssd(p3, tri_ssd, m2_conv_w[layer].astype(F32), m2_conv_b[layer].astype(F32)[None, :],
                    pad8(m2_dt_bias[layer]), pad8(m2_a_log[layer]),
                    jnp.repeat(m2_d[layer].astype(F32), M2_HEADDIM)[None, :],
                    m2_norm_w[layer].astype(F32)[None, :]).reshape(t, BRANCH_W)

        nr = 40
        w_route = jnp.zeros((nr, d), F32).at[:MOE_GROUPS].set(moe_w_group[layer].astype(F32).T)
        w_route = w_route.at[MOE_GROUPS:MOE_GROUPS + MOE_EXPERTS].set(moe_w_expert[layer].astype(F32).T)
        b_route = jnp.zeros((nr, LANE), F32).at[:MOE_GROUPS, 0].set(moe_b_group[layer].astype(F32))
        b_route = b_route.at[MOE_GROUPS:MOE_GROUPS + MOE_EXPERTS, 0].set(moe_b_expert[layer].astype(F32))
        x2 = _merge(x2, mod3, y_s5, y_hg, y_ret, y_m2, w_glu_bf, w_branch_bf, w_gate_bf, b_gate3, w_out_bf,
                    layer, seq)
        x2 = _moe(x2, mod3, final_w_row, w_route, b_route, tri_excl, moe_w1_all, moe_w3_all, moe_w2_all,
                  layer, seq, final=(layer == depth - 1))
    return x2.reshape(bsz, seq, d)
```

```python
import functools
import math

import numpy as np
import jax
import jax.numpy as jnp
from jax import lax
from jax.experimental import pallas as pl
from jax.experimental.pallas import tpu as pltpu
from jax.experimental.pallas import tpu_sc as plsc

F32 = jnp.float32
BF16 = jnp.bfloat16
HIGHEST = lax.Precision.HIGHEST

D_MODEL = 1024
BRANCH_W = 512
EPS = 1e-6
S5_GROUPS = 32
S5_CH = 16
S5_STATE = 64
S5_MAX_REAL = -1e-4
S5_BLOCK = 16
S5_SEQ_PER_STEP = 2
HG_HEADS = 4
HG_DK = 128
RET_HEADS = 4
RET_DK = 64
RET_DV = 128
ROPE_BASE = 10000.0
M2_HEADS = 8
M2_HEADDIM = 64
M2_GROUPS = 2
M2_STATE = 128
M2_CONV = 4
MOE_GROUPS = 4
MOE_EPG = 8
MOE_EXPERTS = MOE_GROUPS * MOE_EPG
MOE_FF = 256

COL_S5, COL_HQ, COL_HF, COL_HI, COL_HG = 0, 512, 1024, 1536, 2048
COL_RQ, COL_RK, COL_RV, COL_RG = 2560, 2816, 3072, 3584
COL_MZ, COL_MXS, COL_MBC, COL_MDT = 4096, 4608, 5120, 5632
IN_W = 5640
IN_W_PAD = 5888

LANE = 128
VMEM_LIMIT = 56 * 1024 * 1024

TM_PROJ = 1024
TN_PROJ = 1024
TM_INPROJ = 512
LOG2_E = 1.4426950408889634
C_RET = 512
C_SSD = 256
C_HG = 128
TM_X = 512
X_SUB = 128
TM_COMB = 1024
SC_WINDOW = 128
SLAB = 256
N_SLAB = D_MODEL // 2 // SLAB


def _cparams(sem):
    return pltpu.CompilerParams(dimension_semantics=sem, vmem_limit_bytes=VMEM_LIMIT)


def _silu(v):
    return v * jax.nn.sigmoid(v)


def _dot_nt(a, b, **kw):
    return lax.dot_general(a, b, (((1,), (1,)), ((), ())), preferred_element_type=F32, **kw)


def _dot_tn(a, b, **kw):
    return lax.dot_general(a, b, (((0,), (0,)), ((), ())), preferred_element_type=F32, **kw)


def _ada_kernel(ct_ref, w_ref, b_ref, o_ref, *, n_rows):
    cond_t = _silu(ct_ref[...])
    w = w_ref[...]
    rows = [jnp.sum(w * cond_t[:, b:b + 1], axis=0, keepdims=True) for b in range(n_rows)]
    rows += [jnp.zeros_like(rows[0])] * (cond_t.shape[1] - n_rows)
    o_ref[...] = jnp.concatenate(rows, axis=0) + b_ref[...]


def _ada_mod(c_pad_t, ada_w, ada_b, n_rows):
    depth, d, n = ada_w.shape
    tn = 1536
    return pl.pallas_call(
        functools.partial(_ada_kernel, n_rows=n_rows),
        grid=(depth, n // tn),
        in_specs=[pl.BlockSpec((d, 8), lambda l, j: (0, 0)),
                  pl.BlockSpec((None, d, tn), lambda l, j: (l, 0, j)),
                  pl.BlockSpec((None, 1, tn), lambda l, j: (l, 0, j))],
        out_specs=pl.BlockSpec((None, 8, tn), lambda l, j: (l, 0, j)),
        out_shape=jax.ShapeDtypeStruct((depth, 8, n), F32),
        compiler_params=_cparams(("parallel", "parallel")),
        name="ada_mod",
    )(c_pad_t, ada_w, ada_b.reshape(depth, 1, n))


def _pack_bf16_pairs(x):
    n = x.shape[1] // 2
    lo = pltpu.bitcast(x[:, :n].astype(BF16).astype(F32), jnp.uint32) >> 16
    hi = pltpu.bitcast(x[:, n:].astype(BF16).astype(F32), jnp.uint32)
    return hi | lo


def _unpack_bf16_pairs(w):
    lo = pltpu.bitcast(w << 16, F32)
    hi = pltpu.bitcast(w & jnp.uint32(0xFFFF0000), F32)
    return jnp.concatenate([lo, hi], axis=-1)


def _modulated_norm(x, scale, shift):
    ms = jnp.mean(x * x, axis=-1, keepdims=True)
    return x * lax.rsqrt(ms + EPS) * (1.0 + scale) + shift


def _inproj_kernel(x_ref, sc_ref, sh_ref, w_ref, o_ref, u_ref):
    h = _modulated_norm(x_ref[...], sc_ref[...], sh_ref[...]).astype(BF16)
    n_total = o_ref.shape[1]
    for n0 in range(0, n_total, TN_PROJ):
        n1 = min(n0 + TN_PROJ, n_total)
        p = jnp.dot(h, w_ref[:, n0:n1], preferred_element_type=F32)
        o_ref[:, n0:n1] = p.astype(o_ref.dtype)
        if n0 == 0:
            u_ref[...] = p[:, COL_S5:COL_S5 + BRANCH_W]


def _in_proj(x2, mod3, w_pad, layer, seq):
    t, d = x2.shape
    tm = TM_INPROJ
    tpb = seq // tm
    assert COL_S5 + BRANCH_W <= TN_PROJ
    return pl.pallas_call(
        _inproj_kernel,
        grid=(t // tm,),
        in_specs=[pl.BlockSpec((tm, d), lambda i: (i, 0)),
                  pl.BlockSpec((None, 1, d), lambda i: ((i // tpb) * 6 + 1, 0, 0)),
                  pl.BlockSpec((None, 1, d), lambda i: ((i // tpb) * 6 + 0, 0, 0)),
                  pl.BlockSpec((None, d, IN_W_PAD), lambda i: (layer, 0, 0), pipeline_mode=pl.Buffered(1))],
        out_specs=[pl.BlockSpec((tm, IN_W_PAD), lambda i: (i, 0)),
                   pl.BlockSpec((tm, BRANCH_W), lambda i: (i, 0))],
        out_shape=[jax.ShapeDtypeStruct((t, IN_W_PAD), BF16), jax.ShapeDtypeStruct((t, BRANCH_W), F32)],
        compiler_params=_cparams(("parallel",)),
        name="in_proj",
    )(x2, mod3, mod3, w_pad)


def _rope_kernel(pos_ref, invf_ref, ecos_ref, esin_ref, cos_ref, sin_ref):
    ang = invf_ref[:, 0:1] * pos_ref[...].astype(F32)
    def spread(values, e_ref):
        hi = values.astype(BF16)
        rest = values - hi.astype(F32)
        mid = rest.astype(BF16)
        lo = (rest - mid.astype(F32)).astype(BF16)
        e = e_ref[...]
        return _dot_tn(hi, e) + _dot_tn(mid, e) + _dot_tn(lo, e)

    cos_ref[...] = spread(jnp.cos(ang), ecos_ref)
    sin_ref[...] = spread(jnp.sin(ang), esin_ref)


def _rope_tables(pos_row, invf_col, expand_cos, expand_sin):
    t = pos_row.shape[1]
    half, w = expand_cos.shape
    tm = 1024
    const = lambda shape: pl.BlockSpec(shape, lambda i: (0, 0))
    return pl.pallas_call(
        _rope_kernel,
        grid=(t // tm,),
        in_specs=[pl.BlockSpec((1, tm), lambda i: (0, i)), const((half, LANE)), const((half, w)), const((half, w))],
        out_specs=[pl.BlockSpec((tm, w), lambda i: (i, 0))] * 2,
        out_shape=[jax.ShapeDtypeStruct((t, w), F32)] * 2,
        compiler_params=_cparams(("parallel",)),
        name="rope_tables",
    )(pos_row, invf_col, expand_cos, expand_sin)


def _ret_kernel(q_ref, k_ref, v_ref, g_ref, cos_ref, sin_ref, o_ref, st_ref, dec_ref, *, chunk):
    @pl.when(pl.program_id(1) == 0)
    def _():
        st_ref[...] = jnp.zeros_like(st_ref)
        ti = lax.broadcasted_iota(jnp.int32, (chunk, chunk), 0)
        si = lax.broadcasted_iota(jnp.int32, (chunk, chunk), 1)
        lag = (ti - si).astype(F32)
        for h in range(RET_HEADS):
            log_gamma = math.log1p(-(2.0 ** (-5.0 - h)))
            dec_ref[h] = jnp.where(ti >= si, jnp.exp(jnp.minimum(lag * log_gamma, 0.0)), 0.0)

    cosf = cos_ref[...]
    sinf = sin_ref[...]
    width = RET_HEADS * RET_DK
    lane = lax.broadcasted_iota(jnp.int32, (chunk, width), 1)
    first_half = (lane % RET_DK) < (RET_DK // 2)

    def rope(t):
        partner = jnp.where(first_half, pltpu.roll(t, width - RET_DK // 2, 1), pltpu.roll(t, RET_DK // 2, 1))
        return t * cosf + partner * sinf

    q = rope(q_ref[...].astype(F32))
    k = rope(k_ref[...].astype(F32)) * (RET_DK ** -0.5)
    v = v_ref[...]
    g = g_ref[...].astype(F32)
    tcol = lax.broadcasted_iota(jnp.int32, (chunk, 1), 0).astype(F32)
    for h in range(RET_HEADS):
        log_gamma = math.log1p(-(2.0 ** (-5.0 - h)))
        qh = q[:, h * RET_DK:(h + 1) * RET_DK]
        kh = k[:, h * RET_DK:(h + 1) * RET_DK]
        vh = v[:, h * RET_DV:(h + 1) * RET_DV].astype(BF16)
        scores = _dot_nt(qh.astype(BF16), kh.astype(BF16)) * dec_ref[h]
        state = st_ref[h]
        q_in = qh * jnp.exp(log_gamma * (tcol + 1.0))
        o = (jnp.dot(scores.astype(BF16), vh, preferred_element_type=F32)
             + jnp.dot(q_in.astype(BF16), state.astype(BF16), preferred_element_type=F32))
        k_out = kh * jnp.exp(log_gamma * (chunk - 1.0 - tcol))
        st_ref[h] = math.exp(log_gamma * chunk) * state + _dot_tn(k_out.astype(BF16), vh)
        o = o * lax.rsqrt(jnp.mean(o * o, axis=-1, keepdims=True) + EPS)
        gh = g[:, h * RET_DV:(h + 1) * RET_DV]
        o_ref[:, h * RET_DV:(h + 1) * RET_DV] = (o * _silu(gh)).astype(o_ref.dtype)


def _retention(p3, cos3, sin3):
    b, seq, _ = p3.shape
    c = C_RET
    qk_w = RET_HEADS * RET_DK
    return pl.pallas_call(
        functools.partial(_ret_kernel, chunk=c),
        grid=(b, seq // c),
        in_specs=[pl.BlockSpec((None, c, qk_w), lambda i, j: (i, j, COL_RQ // qk_w)),
                  pl.BlockSpec((None, c, qk_w), lambda i, j: (i, j, COL_RK // qk_w)),
                  pl.BlockSpec((None, c, BRANCH_W), lambda i, j: (i, j, COL_RV // BRANCH_W)),
                  pl.BlockSpec((None, c, BRANCH_W), lambda i, j: (i, j, COL_RG // BRANCH_W)),
                  pl.BlockSpec((None, c, qk_w), lambda i, j: (i, j, 0)),
                  pl.BlockSpec((None, c, qk_w), lambda i, j: (i, j, 0))],
        out_specs=pl.BlockSpec((None, c, BRANCH_W), lambda i, j: (i, j, 0)),
        out_shape=jax.ShapeDtypeStruct((b, seq, BRANCH_W), BF16),
        scratch_shapes=[pltpu.VMEM((RET_HEADS, RET_DK, RET_DV), F32), pltpu.VMEM((RET_HEADS, c, c), F32)],
        compiler_params=_cparams(("parallel", "arbitrary")),
        name="retention",
    )(p3, p3, p3, p3, cos3, sin3)


def _ssd_kernel(z_ref, xs_ref, bc_ref, dt_ref, tri_ref, cw_ref, cb_ref, dtb_ref, alog_ref, dsk_ref, nw_ref,
                o_ref, xe_scr, st_ref, *, chunk):
    j = pl.program_id(1)
    width = 2 * BRANCH_W

    @pl.when(j == 0)
    def _():
        st_ref[...] = jnp.zeros_like(st_ref)
        xe_scr[0:8, :] = jnp.zeros((8, width), F32)

    @pl.when(j > 0)
    def _():
        xe_scr[0:8, :] = xe_scr[chunk:chunk + 8, :]

    xe_scr[8:, 0:BRANCH_W] = xs_ref[...].astype(F32)
    xe_scr[8:, BRANCH_W:] = bc_ref[...].astype(F32)
    conv = cb_ref[...] + cw_ref[M2_CONV - 1:M2_CONV, :] * xe_scr[8:, :]
    for tap in range(M2_CONV - 1):
        conv = conv + cw_ref[tap:tap + 1, :] * xe_scr[pl.ds(8 - (M2_CONV - 1) + tap, chunk), :]
    conv = _silu(conv)
    xs = conv[:, :BRANCH_W]
    bm = conv[:, BRANCH_W:BRANCH_W + M2_GROUPS * M2_STATE]
    cm = conv[:, BRANCH_W + M2_GROUPS * M2_STATE:]

    dt = jax.nn.softplus(dt_ref[...].astype(F32) + dtb_ref[...])
    da = dt * (-jnp.exp(alog_ref[...]))
    da_hi = da.astype(BF16)
    da_r = da - da_hi.astype(F32)
    da_mid = da_r.astype(BF16)
    da_lo = (da_r - da_mid.astype(F32)).astype(BF16)
    tri = tri_ref[...]
    a_cs = (jnp.dot(tri, da_hi, preferred_element_type=F32) + jnp.dot(tri, da_mid, preferred_element_type=F32)
            + jnp.dot(tri, da_lo, preferred_element_type=F32))
    a_cs = a_cs * LOG2_E
    a_cs_t = a_cs.T
    ti = lax.broadcasted_iota(jnp.int32, (chunk, chunk), 0)
    si = lax.broadcasted_iota(jnp.int32, (chunk, chunk), 1)
    causal = ti >= si
    hpg = M2_HEADS // M2_GROUPS
    ys = []
    for grp in range(M2_GROUPS):
        bm_g = bm[:, grp * M2_STATE:(grp + 1) * M2_STATE]
        cm_g = cm[:, grp * M2_STATE:(grp + 1) * M2_STATE]
        cb = _dot_nt(cm_g.astype(BF16), bm_g.astype(BF16))
        for hh in range(hpg):
            h = grp * hpg + hh
            col = a_cs[:, h:h + 1]
            row = a_cs_t[h:h + 1, :]
            lmat = jnp.where(causal, jnp.exp2(col - row), 0.0)
            xd = xs[:, h * M2_HEADDIM:(h + 1) * M2_HEADDIM] * dt[:, h:h + 1]
            state = st_ref[h]
            y = (jnp.dot((cb * lmat).astype(BF16), xd.astype(BF16), preferred_element_type=F32)
                 + jnp.dot((cm_g * jnp.exp2(col)).astype(BF16), state.astype(BF16), preferred_element_type=F32))
            a_last = a_cs[chunk - 1:chunk, h:h + 1]
            to_end = jnp.exp2(a_last - col)
            st_ref[h] = jnp.exp2(a_last) * state + _dot_tn(bm_g.astype(BF16), (xd * to_end).astype(BF16))
            ys.append(y)
    y = jnp.concatenate(ys, axis=-1) + dsk_ref[...] * xs
    y = y * _silu(z_ref[...].astype(F32))
    o_ref[...] = (y * lax.rsqrt(jnp.mean(y * y, axis=-1, keepdims=True) + EPS) * nw_ref[...]).astype(o_ref.dtype)


def _ssd(p3, tri, conv_w, conv_b, dt_bias_row, a_log_row, d_skip_row, norm_w_row):
    b, seq, _ = p3.shape
    c = C_SSD
    const = lambda shape: pl.BlockSpec(shape, lambda i, j: (0,) * len(shape))
    return pl.pallas_call(
        functools.partial(_ssd_kernel, chunk=c),
        grid=(b, seq // c),
        in_specs=[pl.BlockSpec((None, c, BRANCH_W), lambda i, j: (i, j, COL_MZ // BRANCH_W)),
                  pl.BlockSpec((None, c, BRANCH_W), lambda i, j: (i, j, COL_MXS // BRANCH_W)),
                  pl.BlockSpec((None, c, BRANCH_W), lambda i, j: (i, j, COL_MBC // BRANCH_W)),
                  pl.BlockSpec((None, c, LANE), lambda i, j: (i, j, COL_MDT // LANE)),
                  const((c, c)), const((M2_CONV, 2 * BRANCH_W)), const((1, 2 * BRANCH_W)),
                  const((1, LANE)), const((1, LANE)), const((1, BRANCH_W)), const((1, BRANCH_W))],
        out_specs=pl.BlockSpec((None, c, BRANCH_W), lambda i, j: (i, j, 0)),
        out_shape=jax.ShapeDtypeStruct((b, seq, BRANCH_W), BF16),
        scratch_shapes=[pltpu.VMEM((c + 8, 2 * BRANCH_W), F32),
                        pltpu.VMEM((M2_HEADS, M2_STATE, M2_HEADDIM), F32)],
        compiler_params=_cparams(("parallel", "arbitrary")),
        name="ssd",
    )(p3, p3, p3, p3, tri, conv_w, conv_b, dt_bias_row, a_log_row, d_skip_row, norm_w_row)


def _hg_tables(chunk):
    n_lev = int(math.log2(chunk))
    r = np.arange(chunk)[:, None]
    jj = np.arange(chunk)[None, :]
    tri = (jj <= r).astype(np.float32)
    x = r ^ jj
    levmap = np.where(r > jj, np.floor(np.log2(x + 0.5)), np.where(r == jj, -1, -2)).astype(np.int32)
    return tri, levmap, n_lev


def _hg_level_exponent(b, lev):
    rows, width = b.shape
    m = 1 << lev
    sub = 8
    if 2 * m >= sub:
        blocks = b.reshape(rows // (2 * m), 2 * m, width)
        mid = jnp.broadcast_to(blocks[:, m - 1:m, :], blocks.shape).reshape(rows, width)
    else:
        groups = b.reshape(rows // sub, sub, width)
        row_in_group = lax.broadcasted_iota(jnp.int32, groups.shape, 1)
        mid = None
        for start in range(0, sub, 2 * m):
            picked = jnp.broadcast_to(groups[:, start + m - 1:start + m, :], groups.shape)
            mid = picked if mid is None else jnp.where(row_in_group >= start, picked, mid)
        mid = mid.reshape(rows, width)
    return -jnp.abs(b - mid)


def _hg_kernel(q_ref, f_ref, i_ref, g_ref, sum_ref, lev_ref, llb_ref, l1m_ref, nw_ref, o_ref, st_ref,
               *, chunk, n_lev):
    @pl.when(pl.program_id(1) == 0)
    def _():
        st_ref[...] = jnp.zeros_like(st_ref)

    f = f_ref[...].astype(F32)
    y = jnp.exp(-jnp.abs(f))
    one_plus_y = 1.0 + y
    log_sig = jnp.minimum(f, 0.0) - jnp.log(one_plus_y)
    a = llb_ref[...]
    bb = l1m_ref[...] + log_sig
    log_f = jnp.maximum(a, bb) + jnp.log(1.0 + jnp.exp(-jnp.abs(a - bb)))
    k_all = jnp.exp(l1m_ref[...]) * (jnp.where(f >= 0.0, y, 1.0) / one_plus_y)
    q_all = _silu(q_ref[...].astype(F32))
    hi = log_f.astype(BF16)
    r1 = log_f - hi.astype(F32)
    mid = r1.astype(BF16)
    lo = (r1 - mid.astype(F32)).astype(BF16)
    tri = sum_ref[...]
    b_all = (jnp.dot(tri, hi, preferred_element_type=F32)
             + jnp.dot(tri, mid, preferred_element_type=F32)
             + jnp.dot(tri, lo, preferred_element_type=F32))
    b_all = b_all * LOG2_E
    to_end_all = b_all[chunk - 1:chunk, :] - b_all
    level_decay = [jnp.exp2(_hg_level_exponent(b_all, lev)) for lev in range(n_lev)]
    levmap = lev_ref[...]
    on_diag = levmap == -1
    on_level = [levmap == lev for lev in range(n_lev)]
    v_all = i_ref[...]
    g_all = g_ref[...].astype(F32)
    for h in range(HG_HEADS):
        sl = slice(h * HG_DK, (h + 1) * HG_DK)
        qh = q_all[:, sl]
        kh = k_all[:, sl]
        vh = v_all[:, sl].astype(BF16)
        b_h = b_all[:, sl]
        to_end = to_end_all[:, sl]
        amat = jnp.where(on_diag, _dot_nt(qh.astype(BF16), kh.astype(BF16)), 0.0)
        for lev in range(n_lev):
            e = level_decay[lev][:, sl]
            a_l = _dot_nt((qh * e).astype(BF16), (kh * e).astype(BF16))
            amat = jnp.where(on_level[lev], a_l, amat)
        state_t = st_ref[h]
        o = (jnp.dot(amat.astype(BF16), vh, preferred_element_type=F32)
             + _dot_nt((qh * jnp.exp2(b_h)).astype(BF16), state_t.astype(BF16)))
        k_end = kh * jnp.exp2(to_end)
        st_ref[h] = jnp.exp2(b_h[chunk - 1:chunk, :]) * state_t + _dot_tn(vh, k_end.astype(BF16))
        o = o * lax.rsqrt(jnp.mean(o * o, axis=-1, keepdims=True) + EPS) * nw_ref[...]
        o_ref[:, sl] = (o * _silu(g_all[:, sl])).astype(o_ref.dtype)


def _hgrn2(p3, log_lb_row, log1m_lb_row, norm_w_row):
    b, seq, _ = p3.shape
    c = C_HG
    tri, levmap, n_lev = _hg_tables(c)
    const = lambda shape: pl.BlockSpec(shape, lambda i, j: (0,) * len(shape))
    blk = lambda col: pl.BlockSpec((None, c, BRANCH_W), lambda i, j: (i, j, col // BRANCH_W))
    return pl.pallas_call(
        functools.partial(_hg_kernel, chunk=c, n_lev=n_lev),
        grid=(b, seq // c),
        in_specs=[blk(COL_HQ), blk(COL_HF), blk(COL_HI), blk(COL_HG),
                  const((c, c)), const((c, c)),
                  const((1, BRANCH_W)), const((1, BRANCH_W)), const((1, HG_DK))],
        out_specs=pl.BlockSpec((None, c, BRANCH_W), lambda i, j: (i, j, 0)),
        out_shape=jax.ShapeDtypeStruct((b, seq, BRANCH_W), BF16),
        scratch_shapes=[pltpu.VMEM((HG_HEADS, HG_DK, HG_DK), F32)],
        compiler_params=_cparams(("parallel", "arbitrary")),
        name="hgrn2",
    )(p3, p3, p3, p3, jnp.asarray(tri, BF16), jnp.asarray(levmap), log_lb_row, log1m_lb_row, norm_w_row)


def _expand_block_diag(comp_ref, e_ref, dst_ref, row_div, lane_div, causal=False):
    gq = LANE // S5_CH
    rows, ncols = dst_ref.shape
    step = 512
    for c0 in range(0, ncols, step):
        r1 = min(rows, c0 + step) if causal else rows
        row_grp = (lax.broadcasted_iota(jnp.int32, (r1, step), 0) // row_div) % gq
        lane_grp = ((lax.broadcasted_iota(jnp.int32, (r1, step), 1) + c0) // lane_div) % gq
        full = jnp.dot(comp_ref[0:r1, :], e_ref[:, c0:c0 + step], preferred_element_type=F32)
        dst_ref[0:r1, c0:c0 + step] = jnp.where(row_grp == lane_grp, full, 0.0).astype(dst_ref.dtype)


def _s5_kernel(u_ref, k2_ref, bc_ref, cc_ref, esc_ref, lam_ref, o_ref, tc_scr, tq_ref, bqt_ref, cq_ref,
               x_scr, w_scr, s_scr, *, rows):
    nb = S5_BLOCK

    @pl.when(pl.program_id(1) == 0)
    def _():
        k2 = k2_ref[...]
        lane = lax.broadcasted_iota(jnp.int32, k2.shape, 1)
        for t in range(nb):
            shifted = k2 if t == 0 else jnp.where(lane >= t * S5_CH, pltpu.roll(k2, t * S5_CH, 1), 0.0)
            tc_scr[t * LANE:(t + 1) * LANE, :] = shifted.astype(tc_scr.dtype)
        _expand_block_diag(tc_scr, esc_ref, tq_ref, S5_CH, S5_CH, causal=True)
        _expand_block_diag(bc_ref, esc_ref, bqt_ref, S5_STATE, S5_CH)
        _expand_block_diag(cc_ref, esc_ref, cq_ref, S5_STATE, S5_CH)

    n_seq = u_ref.shape[0]
    for b in range(n_seq):
        for t in range(nb):
            x_scr[b * rows:(b + 1) * rows, t * LANE:(t + 1) * LANE] = (
                u_ref[b, pl.ds(t, rows, stride=nb), :].astype(x_scr.dtype))
    x = x_scr[...]
    half = w_scr.shape[1] // 2
    w_scr[...] = _dot_nt(x, bqt_ref[...])
    lam_re = lam_ref[0:1, :]
    lam_im = lam_ref[1:2, :]

    def body(j, carry):
        out = []
        for b in range(n_seq):
            s_re, s_im = carry[2 * b], carry[2 * b + 1]
            r = b * rows + j
            s_scr[pl.ds(r, 1), 0:half] = s_re
            s_scr[pl.ds(r, 1), half:] = s_im
            w_re = w_scr[pl.ds(r, 1), 0:half]
            w_im = w_scr[pl.ds(r, 1), half:]
            out += [lam_re * s_re - lam_im * s_im + w_re, lam_re * s_im + lam_im * s_re + w_im]
        return tuple(out)

    zero = jnp.zeros((1, half), F32)
    lax.fori_loop(0, rows, body, (zero,) * (2 * n_seq))
    s_bf = s_scr[...].astype(BF16)
    pair = 2 * LANE
    for c0 in range(0, nb * LANE, pair):
        k_rows = c0 + pair
        y = (jnp.dot(x[:, :k_rows], tq_ref[0:k_rows, c0:c0 + pair], preferred_element_type=F32)
             + jnp.dot(s_bf, cq_ref[:, c0:c0 + pair], preferred_element_type=F32))
        for b in range(n_seq):
            for t in range(c0 // LANE, (c0 + pair) // LANE):
                o_ref[b, pl.ds(t, rows, stride=nb), :] = y[b * rows:(b + 1) * rows, t * LANE - c0:(t + 1) * LANE - c0]


def _s5_scan(p3, k2, bc, cc, lam16):
    batch, seq, _ = p3.shape
    nb = S5_BLOCK
    nq = BRANCH_W // LANE
    rows = seq // nb
    kdim = nb * LANE
    gq = LANE // S5_CH
    ncol = 2 * gq * S5_STATE
    e_sc = (np.eye(nb)[:, None, :, None, None] * np.eye(S5_CH)[None, :, None, None, :] * np.ones((1, 1, 1, gq, 1)))
    e_sc = e_sc.reshape(nb * S5_CH, nb * gq * S5_CH)
    full = lambda shape: pl.BlockSpec(shape, lambda q, b: (0,) * len(shape))
    per_q = lambda r, c: pl.BlockSpec((None, r, c), lambda q, b: (q, 0, 0))
    n_seq = S5_SEQ_PER_STEP if batch % S5_SEQ_PER_STEP == 0 else 1
    return pl.pallas_call(
        functools.partial(_s5_kernel, rows=rows),
        grid=(nq, batch // n_seq),
        in_specs=[pl.BlockSpec((n_seq, seq, LANE), lambda q, b: (b, 0, q)),
                  per_q(LANE, nb * S5_CH), per_q(ncol, nb * S5_CH), per_q(ncol, nb * S5_CH),
                  full(e_sc.shape), per_q(2, ncol // 2)],
        out_specs=pl.BlockSpec((n_seq, seq, LANE), lambda q, b: (b, 0, q)),
        out_shape=jax.ShapeDtypeStruct((batch, seq, BRANCH_W), F32),
        scratch_shapes=[pltpu.VMEM((kdim, nb * S5_CH), BF16),
                        pltpu.VMEM((kdim, kdim), BF16), pltpu.VMEM((ncol, kdim), BF16), pltpu.VMEM((ncol, kdim), BF16),
                        pltpu.VMEM((n_seq * rows, kdim), BF16), pltpu.VMEM((n_seq * rows, ncol), F32),
                        pltpu.VMEM((n_seq * rows, ncol), F32)],
        compiler_params=_cparams(("parallel", "arbitrary")),
        name="s5_scan",
    )(p3, k2, bc, cc, jnp.asarray(e_sc, BF16), lam16)


def _s5_operators(lam_re, lam_im, b_re, b_im, c_re, c_im, d_skip, log_dt):
    nb = S5_BLOCK
    gq = LANE // S5_CH
    nq = S5_GROUPS // gq
    lam = lax.complex(jnp.minimum(lam_re.astype(F32), S5_MAX_REAL), lam_im.astype(F32))
    step = jnp.exp(log_dt.astype(F32))[:, None]
    z = lam * step
    lam_bar = jnp.exp(z)
    b_bar = ((lam_bar - 1.0) / lam)[..., None] * lax.complex(b_re.astype(F32), b_im.astype(F32))
    c_mat = lax.complex(c_re.astype(F32), c_im.astype(F32))
    pw = jnp.exp(z[..., None] * jnp.arange(nb + 1, dtype=F32))
    cp = c_mat[:, None, :, :] * pw[..., :nb].transpose(0, 2, 1)[:, :, None, :]
    cp = jnp.concatenate([cp.real, -cp.imag], axis=-1).reshape(S5_GROUPS, nb * S5_CH, 2 * S5_STATE)
    bri = jnp.concatenate([b_bar.real, b_bar.imag], axis=1)
    kern = jnp.einsum('gnk,gki->gin', cp, bri, precision=HIGHEST)
    skip = (jnp.asarray(np.concatenate([np.eye(S5_CH), np.zeros((S5_CH, (nb - 1) * S5_CH))], axis=1), F32)[None]
            * d_skip.astype(F32).reshape(S5_GROUPS, S5_CH, 1))
    k2 = (kern + skip).reshape(nq, gq * S5_CH, nb * S5_CH)
    pw_rev = jnp.exp(z[..., None] * jnp.asarray(np.arange(nb - 1, -1, -1), F32))
    binc = pw_rev[:, :, :, None] * b_bar[:, :, None, :]
    binc = jnp.stack([binc.real, binc.imag], axis=0).reshape(2, nq, gq * S5_STATE, nb * S5_CH)
    bc = binc.transpose(1, 0, 2, 3).reshape(nq, 2 * gq * S5_STATE, nb * S5_CH)
    cm = c_mat.transpose(0, 2, 1)[:, :, None, :] * pw[..., 1:][:, :, :, None]
    cm = jnp.stack([cm.real, -cm.imag], axis=0).reshape(2, nq, gq * S5_STATE, nb * S5_CH)
    cc = cm.transpose(1, 0, 2, 3).reshape(nq, 2 * gq * S5_STATE, nb * S5_CH)
    lam_n = pw[..., nb].reshape(nq, gq * S5_STATE)
    lam16 = jnp.stack([lam_n.real, lam_n.imag], axis=1)
    return k2, bc.astype(BF16), cc.astype(BF16), lam16


def _merge_kernel(x_ref, sc_ref, sh_ref, gm_ref, ys5_ref, yhg_ref, yret_ref, ym2_ref,
                  wglu_ref, wbr_ref, wg_ref, bg_ref, wout_ref, o_ref):
    x = x_ref[...]
    d = x.shape[1]
    h = _modulated_norm(x, sc_ref[...], sh_ref[...]).astype(BF16)
    y_s5 = jax.nn.gelu(ys5_ref[...])
    y_s5 = y_s5 * jax.nn.sigmoid(jnp.dot(y_s5.astype(BF16), wglu_ref[...], preferred_element_type=F32))
    acc = jnp.zeros(x.shape, F32)
    for n, y in enumerate((y_s5, yhg_ref[...], yret_ref[...], ym2_ref[...])):
        gate = jax.nn.sigmoid(jnp.dot(h, wg_ref[:, n * d:(n + 1) * d], preferred_element_type=F32)
                              + bg_ref[:, n * d:(n + 1) * d])
        acc = acc + gate * jnp.dot(y.astype(BF16), wbr_ref[n], preferred_element_type=F32)
    o_ref[...] = x + gm_ref[...] * jnp.dot(acc.astype(BF16), wout_ref[...], preferred_element_type=F32)


def _merge(x2, mod3, ys5, yhg, yret, ym2, w_glu, w_branch, w_gate, b_gate, w_out, layer, seq):
    t, d = x2.shape
    tm = TM_PROJ
    tpb = seq // tm
    const = lambda shape: pl.BlockSpec((None,) + shape, lambda i: (layer,) + (0,) * len(shape),
                                       pipeline_mode=pl.Buffered(1))
    modspec = lambda k: pl.BlockSpec((None, 1, d), lambda i: ((i // tpb) * 6 + k, 0, 0))
    yspec = pl.BlockSpec((tm, BRANCH_W), lambda i: (i, 0))
    return pl.pallas_call(
        _merge_kernel,
        grid=(t // tm,),
        in_specs=[pl.BlockSpec((tm, d), lambda i: (i, 0)), modspec(1), modspec(0), modspec(2),
                  yspec, yspec, yspec, yspec,
                  const((BRANCH_W, BRANCH_W)), const((4, BRANCH_W, d)), const((d, 4 * d)), const((1, 4 * d)),
                  const((d, d))],
        out_specs=pl.BlockSpec((tm, d), lambda i: (i, 0)),
        out_shape=jax.ShapeDtypeStruct((t, d), F32),
        compiler_params=_cparams(("parallel",)),
        name="merge",
    )(x2, mod3, mod3, mod3, ys5, yhg, yret, ym2, w_glu, w_branch, w_gate, b_gate, w_out)


def _router_kernel(x_ref, sc_ref, sh_ref, wr_ref, br_ref, tri_ref, h_ref, ids_ref, wts_ref, cnt_ref, carry):
    i = pl.program_id(0)

    @pl.when(i == 0)
    def _():
        carry[...] = jnp.zeros_like(carry)

    h = _modulated_norm(x_ref[...], sc_ref[...], sh_ref[...])
    tm, d = h.shape
    packed = _pack_bf16_pairs(h)
    for k in range(N_SLAB):
        h_ref[k] = packed[:, k * SLAB:(k + 1) * SLAB]
    h_hi = h.astype(BF16)
    h_lo = (h - h_hi.astype(F32)).astype(BF16)
    w_r = wr_ref[...]
    w_hi = w_r.astype(BF16)
    w_lo = (w_r - w_hi.astype(F32)).astype(BF16)
    logits = _dot_nt(w_hi, h_hi) + _dot_nt(w_hi, h_lo) + _dot_nt(w_lo, h_hi) + br_ref[:, 0:1]
    gl = [logits[g:g + 1, :] for g in range(MOE_GROUPS)]
    gmax = gl[0]
    gsel = jnp.zeros((1, tm), jnp.int32)
    for g in range(1, MOE_GROUPS):
        better = gl[g] > gmax
        gsel = jnp.where(better, g, gsel)
        gmax = jnp.where(better, gl[g], gmax)
    gden = gl[0] * 0.0
    for g in range(MOE_GROUPS):
        gden = gden + jnp.exp(gl[g] - gmax)
    g_w = 1.0 / gden
    el = []
    for e in range(MOE_EPG):
        v = logits[MOE_GROUPS + e:MOE_GROUPS + e + 1, :]
        for g in range(1, MOE_GROUPS):
            row = MOE_GROUPS + g * MOE_EPG + e
            v = jnp.where(gsel == g, logits[row:row + 1, :], v)
        el.append(v)
    v1 = el[0]
    i1 = jnp.zeros((1, tm), jnp.int32)
    for e in range(1, MOE_EPG):
        better = el[e] > v1
        i1 = jnp.where(better, e, i1)
        v1 = jnp.where(better, el[e], v1)
    v2 = jnp.full((1, tm), -jnp.inf, F32)
    i2 = jnp.zeros((1, tm), jnp.int32)
    for e in range(MOE_EPG):
        better = (el[e] > v2) & (i1 != e)
        i2 = jnp.where(better, e, i2)
        v2 = jnp.where(better, el[e], v2)
    ex = jnp.exp(v2 - v1)
    p1 = 1.0 / (1.0 + ex)
    e1 = gsel * MOE_EPG + i1
    e2 = gsel * MOE_EPG + i2
    erow = lax.broadcasted_iota(jnp.int32, (MOE_EXPERTS, tm), 0)
    oh1 = (erow == e1).astype(F32)
    oh2 = (erow == e2).astype(F32)
    both = oh1 + oh2
    n_grp = tm // LANE
    stacked = jnp.concatenate([both[:, g * LANE:(g + 1) * LANE] for g in range(n_grp)], axis=0)
    within = jnp.dot(stacked.astype(BF16), tri_ref[...], preferred_element_type=F32)
    grp_count = jnp.sum(stacked, axis=1, keepdims=True)
    running = carry[:, 0:1]
    pieces = []
    for g in range(n_grp):
        pieces.append(within[g * MOE_EXPERTS:(g + 1) * MOE_EXPERTS, :] + running)
        running = running + grp_count[g * MOE_EXPERTS:(g + 1) * MOE_EXPERTS, :]
    prefix = jnp.concatenate(pieces, axis=1)
    rank1 = jnp.sum(oh1 * prefix, axis=0, keepdims=True).astype(jnp.int32)
    rank2 = jnp.sum(oh2 * prefix, axis=0, keepdims=True).astype(jnp.int32)
    carry[...] = jnp.broadcast_to(running, carry.shape)
    zi = jnp.zeros((1, tm), jnp.int32)
    ids_ref[...] = jnp.concatenate([e1, e2, rank1, rank2, zi, zi, zi, zi], axis=0)
    wrow = lax.broadcasted_iota(jnp.int32, (LANE, tm), 0)
    wts_ref[...] = jnp.where(wrow == 0, p1 * g_w, jnp.where(wrow == 1, ex * p1 * g_w, 0.0)).T
    cnt_ref[...] = carry[...]


def _router(x2, mod3, w_route, b_route, tri_excl, seq):
    t, d = x2.shape
    tm = TM_PROJ
    tpb = seq // tm
    nr = w_route.shape[0]
    const = lambda shape: pl.BlockSpec(shape, lambda i: (0,) * len(shape))
    modspec = lambda k: pl.BlockSpec((None, 1, d), lambda i: ((i // tpb) * 6 + k, 0, 0))
    return pl.pallas_call(
        _router_kernel,
        grid=(t // tm,),
        in_specs=[pl.BlockSpec((tm, d), lambda i: (i, 0)), modspec(4), modspec(3),
                  const((nr, d)), const((nr, LANE)), const((LANE, LANE))],
        out_specs=[pl.BlockSpec((N_SLAB, tm, SLAB), lambda i: (0, i, 0)),
                   pl.BlockSpec((8, tm), lambda i: (0, i)),
                   pl.BlockSpec((tm, LANE), lambda i: (i, 0)),
                   const((MOE_EXPERTS, LANE))],
        out_shape=[jax.ShapeDtypeStruct((N_SLAB, t, SLAB), jnp.uint32),
                   jax.ShapeDtypeStruct((8, t), jnp.int32),
                   jax.ShapeDtypeStruct((t, LANE), F32),
                   jax.ShapeDtypeStruct((MOE_EXPERTS, LANE), F32)],
        scratch_shapes=[pltpu.VMEM((MOE_EXPERTS, LANE), F32)],
        compiler_params=_cparams(("arbitrary",)),
        name="moe_router",
    )(x2, mod3, mod3, w_route, b_route, tri_excl)


def _sc_mesh():
    return plsc.VectorSubcoreMesh(core_axis_name="core", subcore_axis_name="subcore")


def _slab_rows(idx, n_rows):
    return (idx[None, :] + (jnp.arange(N_SLAB, dtype=jnp.int32) * n_rows)[:, None]).reshape(-1)


def _dispatch(slot1, slot2, h_slabs):
    n_slab, t, d = h_slabs.shape
    n_out = 2 * t
    xs = _scatter_rows(h_slabs.reshape(n_slab * t, d), _slab_rows(slot1, n_out), _slab_rows(slot2, n_out),
                       n_slab * n_out)
    return xs.reshape(n_slab, n_out, d)


def _scatter_rows(src, idx1, idx2, n_out):
    t, d = src.shape
    win = SC_WINDOW

    @pl.kernel(out_type=jax.ShapeDtypeStruct((n_out, d), src.dtype), mesh=_sc_mesh(), name="moe_dispatch_sc")
    def scatter_rows(x_hbm, i1_hbm, i2_hbm, o_hbm):
        def body(x_vmem, i1_vmem, i2_vmem):
            pltpu.sync_copy(x_vmem, o_hbm.at[i1_vmem.at[0]])
            pltpu.sync_copy(x_vmem, o_hbm.at[i2_vmem.at[0]])

        pltpu.emit_pipeline(
            body,
            grid=(t // win,),
            in_specs=[pl.BlockSpec((win, d), lambda i: (i, 0)),
                      pl.BlockSpec((1, win), lambda i: (0, i)),
                      pl.BlockSpec((1, win), lambda i: (0, i))],
            out_specs=[],
            core_axis_name=("core", "subcore"),
            dimension_semantics=(pltpu.PARALLEL,),
        )(x_hbm, i1_hbm, i2_hbm)

    return scatter_rows(src, idx1.reshape(1, t), idx2.reshape(1, t))


def _gather_rows(src, idx):
    m = idx.shape[0]
    d = src.shape[1]
    win = SC_WINDOW

    @pl.kernel(out_type=jax.ShapeDtypeStruct((m, d), src.dtype), mesh=_sc_mesh(), name="moe_gather_sc")
    def gather(x_hbm, i_hbm, o_hbm):
        def body(i_vmem, o_vmem):
            pltpu.sync_copy(x_hbm.at[i_vmem.at[0]], o_vmem)

        pltpu.emit_pipeline(
            body,
            grid=(m // win,),
            in_specs=[pl.BlockSpec((1, win), lambda i: (0, i))],
            out_specs=[pl.BlockSpec((win, d), lambda i: (i, 0))],
            core_axis_name=("core", "subcore"),
            dimension_semantics=(pltpu.PARALLEL,),
        )(i_hbm, o_hbm)

    return gather(src, idx.reshape(1, m))


def _expert_kernel(tile_ref, exp_ref, lo_ref, hi_ref, xs_ref, w1_ref, w3_ref, w2_ref, ys_ref, w1_scr, w3_scr, w2_scr):
    s = pl.program_id(0)
    prev = jnp.maximum(s - 1, 0)
    new_expert = (s == 0) | (exp_ref[s] != exp_ref[prev])
    new_tile = (s == 0) | (tile_ref[s] != tile_ref[prev])

    @pl.when(new_expert)
    def _():
        w1_scr[...] = w1_ref[...].astype(BF16)
        w3_scr[...] = w3_ref[...].astype(BF16)
        w2_scr[...] = w2_ref[...].astype(BF16)

    lo = lo_ref[s]
    hi = hi_ref[s]
    n_rows = xs_ref.shape[1]

    @pl.when(new_tile)
    def _():
        ys_ref[...] = jnp.zeros_like(ys_ref)

    def run_rows(r0, n):
        x = _unpack_bf16_pairs(jnp.concatenate([xs_ref[k, r0:r0 + n, :] for k in range(N_SLAB)], axis=-1)).astype(BF16)
        a = jnp.dot(x, w1_scr[...], preferred_element_type=F32)
        b = jnp.dot(x, w3_scr[...], preferred_element_type=F32)
        act = _silu(a) * b
        y = _pack_bf16_pairs(jnp.dot(act.astype(BF16), w2_scr[...], preferred_element_type=F32))
        row = lax.broadcasted_iota(jnp.int32, (n, SLAB), 0) + r0
        mine = (row >= lo) & (row < hi)
        for k in range(N_SLAB):
            ys_ref[k, r0:r0 + n, :] = jnp.where(mine, y[:, k * SLAB:(k + 1) * SLAB], ys_ref[k, r0:r0 + n, :])

    groups = (hi + (X_SUB - 1)) // X_SUB - lo // X_SUB
    whole = groups > 2

    @pl.when(whole)
    def _():
        run_rows(0, n_rows)

    for r0 in range(0, n_rows, X_SUB):
        @pl.when(jnp.logical_not(whole) & (lo < r0 + X_SUB) & (hi > r0))
        def _(r0=r0):
            run_rows(r0, X_SUB)


def _experts(step_tile, step_expert, step_lo, step_hi, xs, w1, w3, w2, layer):
    n_slab, ns, slab = xs.shape
    d = w1.shape[1]
    ff = w1.shape[2]
    n_steps = step_tile.shape[0]
    base = layer * MOE_EXPERTS
    grid_spec = pltpu.PrefetchScalarGridSpec(
        num_scalar_prefetch=4,
        grid=(n_steps,),
        in_specs=[pl.BlockSpec((n_slab, TM_X, slab), lambda s, tl, ex, lo, hi: (0, tl[s], 0)),
                  pl.BlockSpec((None, d, ff), lambda s, tl, ex, lo, hi: (base + ex[s], 0, 0)),
                  pl.BlockSpec((None, d, ff), lambda s, tl, ex, lo, hi: (base + ex[s], 0, 0)),
                  pl.BlockSpec((None, ff, d), lambda s, tl, ex, lo, hi: (base + ex[s], 0, 0))],
        out_specs=pl.BlockSpec((n_slab, TM_X, slab), lambda s, tl, ex, lo, hi: (0, tl[s], 0)),
        scratch_shapes=[pltpu.VMEM((d, ff), BF16), pltpu.VMEM((d, ff), BF16), pltpu.VMEM((ff, d), BF16)],
    )
    return pl.pallas_call(
        _expert_kernel,
        grid_spec=grid_spec,
        out_shape=jax.ShapeDtypeStruct((n_slab, ns, slab), xs.dtype),
        compiler_params=_cparams(("arbitrary",)),
        name="moe_experts",
    )(step_tile, step_expert, step_lo, step_hi, xs, w1, w3, w2)


def _combine_kernel(x_ref, gate_ref, fw_ref, wcol_ref, y1_ref, y2_ref, o_ref, *, final):
    w_first = wcol_ref[:, 0:1]
    w_second = wcol_ref[:, 1:2]
    y_first = _unpack_bf16_pairs(jnp.concatenate([y1_ref[k] for k in range(N_SLAB)], axis=-1))
    y_second = _unpack_bf16_pairs(jnp.concatenate([y2_ref[k] for k in range(N_SLAB)], axis=-1))
    moe = w_first * y_first + w_second * y_second
    x = x_ref[...] + gate_ref[...] * moe
    if final:
        x = x * lax.rsqrt(jnp.mean(x * x, axis=-1, keepdims=True) + EPS) * fw_ref[...]
    o_ref[...] = x


def _combine(x2, mod3, final_w_row, wcol, gathered, seq, final):
    t, d = x2.shape
    tm = TM_COMB
    tpb = seq // tm
    nblk = t // tm
    yspec = lambda off: pl.BlockSpec((N_SLAB, tm, SLAB), lambda i: (0, i + off, 0))
    return pl.pallas_call(
        functools.partial(_combine_kernel, final=final),
        grid=(nblk,),
        in_specs=[pl.BlockSpec((tm, d), lambda i: (i, 0)),
                  pl.BlockSpec((None, 1, d), lambda i: ((i // tpb) * 6 + 5, 0, 0)),
                  pl.BlockSpec((1, d), lambda i: (0, 0)),
                  pl.BlockSpec((tm, LANE), lambda i: (i, 0)),
                  yspec(0), yspec(nblk)],
        out_specs=pl.BlockSpec((tm, d), lambda i: (i, 0)),
        out_shape=jax.ShapeDtypeStruct((t, d), F32),
        compiler_params=_cparams(("parallel",)),
        name="moe_combine",
    )(x2, mod3, final_w_row, wcol, gathered, gathered)


def _moe(x2, mod3, final_w_row, w_route, b_route, tri_excl, w1, w3, w2, layer, seq, final):
    t, d = x2.shape
    h3, ids, wcol, counts = _router(x2, mod3, w_route, b_route, tri_excl, seq)
    cnt = counts[:, 0].astype(jnp.int32)
    ends = jnp.cumsum(cnt)
    offs = ends - cnt
    experts = jnp.arange(MOE_EXPERTS, dtype=jnp.int32)
    pick = lambda table, idx: jnp.sum(jnp.where(idx[:, None] == experts[None, :], table[None, :], 0), axis=1)
    slot1 = pick(offs, ids[0]) + ids[2]
    slot2 = pick(offs, ids[1]) + ids[3]
    n_tiles = 2 * t // TM_X
    first_tile = offs // TM_X
    n_vis = jnp.where(cnt > 0, (ends - 1) // TM_X - first_tile + 1, 0)
    cum = jnp.cumsum(n_vis)
    step = jnp.arange(n_tiles + MOE_EXPERTS, dtype=jnp.int32)
    step_expert = jnp.minimum(jnp.sum(step[:, None] >= cum[None, :], axis=1), MOE_EXPERTS - 1).astype(jnp.int32)
    valid = step < cum[-1]
    step_tile = jnp.where(valid, pick(first_tile - (cum - n_vis), step_expert) + step, n_tiles - 1)
    step_lo = jnp.where(valid, jnp.clip(pick(offs, step_expert) - step_tile * TM_X, 0, TM_X), 0)
    step_hi = jnp.where(valid, jnp.clip(pick(ends, step_expert) - step_tile * TM_X, 0, TM_X), 0)
    xs = _dispatch(slot1, slot2, h3)
    ys = _experts(step_tile.astype(jnp.int32), step_expert, step_lo.astype(jnp.int32), step_hi.astype(jnp.int32),
                  xs, w1, w3, w2, layer)
    n_sorted = ys.shape[1]
    gathered = _gather_rows(ys.reshape(N_SLAB * n_sorted, SLAB), _slab_rows(jnp.concatenate([slot1, slot2]), n_sorted))
    gathered = gathered.reshape(N_SLAB, n_sorted, SLAB)
    return _combine(x2, mod3, final_w_row, wcol, gathered, seq, final)


def kernel(x, c, positions, ada_w, ada_b, w_in, s5_lam_re, s5_lam_im, s5_b_re, s5_b_im, s5_c_re, s5_c_im, s5_d, s5_log_dt, s5_w_glu, hg_lb_logits, hg_norm_w, m2_conv_w, m2_conv_b, m2_dt_bias, m2_a_log, m2_d, m2_norm_w, w_branch, w_gate, b_gate, w_out, moe_w_group, moe_b_group, moe_w_expert, moe_b_expert, moe_w1, moe_w3, moe_w2, final_norm_w):
    bsz, seq, d = x.shape
    t = bsz * seq
    depth = ada_w.shape[0]
    assert seq % TM_PROJ == 0 and seq % C_RET == 0 and seq % C_SSD == 0 and seq % C_HG == 0
    x2 = x.reshape(t, d).astype(F32)

    c_pad = jnp.zeros((8, d), F32).at[:bsz].set(c.astype(F32))
    mod_all = _ada_mod(c_pad.T, ada_w.astype(F32), ada_b.astype(F32), bsz)

    half = RET_DK // 2
    inv_freq = ROPE_BASE ** (-jnp.arange(half, dtype=F32) / half)
    invf_col = jnp.broadcast_to(inv_freq[:, None], (half, LANE))
    expand = np.tile(np.eye(half, dtype=np.float32), (1, 2 * RET_HEADS))
    sign = np.tile(np.concatenate([-np.ones(half), np.ones(half)]), RET_HEADS)[None, :].astype(np.float32)
    cos_t, sin_t = _rope_tables(positions.reshape(1, t).astype(jnp.int32), invf_col,
                                jnp.asarray(expand, BF16), jnp.asarray(expand * sign, BF16))
    cos3 = cos_t.reshape(bsz, seq, -1)
    sin3 = sin_t.reshape(bsz, seq, -1)

    lb_cum = jnp.cumsum(jax.nn.softmax(hg_lb_logits.astype(F32), axis=0), axis=0)
    hg_lb = lb_cum - lb_cum[:1]
    tri_ssd = jnp.asarray(np.tril(np.ones((C_SSD, C_SSD), np.float32)), BF16)
    tri_excl = jnp.asarray(np.triu(np.ones((LANE, LANE), np.float32), 1), BF16)
    final_w_row = final_norm_w.astype(F32)[None, :]
    w_pad = jnp.concatenate([w_in.astype(BF16), jnp.zeros((depth, d, IN_W_PAD - IN_W), BF16)], axis=-1)
    w_glu_bf = s5_w_glu.astype(BF16)
    w_branch_bf = w_branch.astype(BF16)
    w_gate_bf = w_gate.astype(BF16)
    w_out_bf = w_out.astype(BF16)
    b_gate3 = b_gate.astype(F32).reshape(depth, 1, -1)
    moe_w1_all = moe_w1.astype(F32).reshape(depth * MOE_EXPERTS, d, MOE_FF)
    moe_w3_all = moe_w3.astype(F32).reshape(depth * MOE_EXPERTS, d, MOE_FF)
    moe_w2_all = moe_w2.astype(F32).reshape(depth * MOE_EXPERTS, MOE_FF, d)

    for layer in range(depth):
        mod3 = mod_all[layer, :bsz].reshape(bsz * 6, 1, d)
        p, u_s5 = _in_proj(x2, mod3, w_pad, layer, seq)
        p3 = p.reshape(bsz, seq, IN_W_PAD)

        ops = _s5_operators(s5_lam_re[layer], s5_lam_im[layer], s5_b_re[layer], s5_b_im[layer],
                            s5_c_re[layer], s5_c_im[layer], s5_d[layer], s5_log_dt[layer])
        y_s5 = _s5_scan(u_s5.reshape(bsz, seq, BRANCH_W), *ops).reshape(t, BRANCH_W)

        lb = hg_lb[layer][None, :]
        y_hg = _hgrn2(p3, jnp.log(lb), jnp.log1p(-lb), hg_norm_w[layer].astype(F32)[None, :]).reshape(t, BRANCH_W)

        y_ret = _retention(p3, cos3, sin3).reshape(t, BRANCH_W)

        pad8 = lambda v: jnp.zeros((1, LANE), F32).at[0, :M2_HEADS].set(v.astype(F32))
        y_m2 = _ssd(p3, tri_ssd, m2_conv_w[layer].astype(F32), m2_conv_b[layer].astype(F32)[None, :],
                    pad8(m2_dt_bias[layer]), pad8(m2_a_log[layer]),
                    jnp.repeat(m2_d[layer].astype(F32), M2_HEADDIM)[None, :],
                    m2_norm_w[layer].astype(F32)[None, :]).reshape(t, BRANCH_W)

        nr = 40
        w_route = jnp.zeros((nr, d), F32).at[:MOE_GROUPS].set(moe_w_group[layer].astype(F32).T)
        w_route = w_route.at[MOE_GROUPS:MOE_GROUPS + MOE_EXPERTS].set(moe_w_expert[layer].astype(F32).T)
        b_route = jnp.zeros((nr, LANE), F32).at[:MOE_GROUPS, 0].set(moe_b_group[layer].astype(F32))
        b_route = b_route.at[MOE_GROUPS:MOE_GROUPS + MOE_EXPERTS, 0].set(moe_b_expert[layer].astype(F32))
        x2 = _merge(x2, mod3, y_s5, y_hg, y_ret, y_m2, w_glu_bf, w_branch_bf, w_gate_bf, b_gate3, w_out_bf,
                    layer, seq)
        x2 = _moe(x2, mod3, final_w_row, w_route, b_route, tri_excl, moe_w1_all, moe_w3_all, moe_w2_all,
                  layer, seq, final=(layer == depth - 1))
    return x2.reshape(bsz, seq, d)
```

```python
import functools
import math

import numpy as np
import jax
import jax.numpy as jnp
from jax import lax
from jax.experimental import pallas as pl
from jax.experimental.pallas import tpu as pltpu
from jax.experimental.pallas import tpu_sc as plsc

F32 = jnp.float32
BF16 = jnp.bfloat16
HIGHEST = lax.Precision.HIGHEST

D_MODEL = 1024
BRANCH_W = 512
EPS = 1e-6
S5_GROUPS = 32
S5_CH = 16
S5_STATE = 64
S5_MAX_REAL = -1e-4
S5_BLOCK = 16
S5_SEQ_PER_STEP = 2
HG_HEADS = 4
HG_DK = 128
RET_HEADS = 4
RET_DK = 64
RET_DV = 128
ROPE_BASE = 10000.0
M2_HEADS = 8
M2_HEADDIM = 64
M2_GROUPS = 2
M2_STATE = 128
M2_CONV = 4
MOE_GROUPS = 4
MOE_EPG = 8
MOE_EXPERTS = MOE_GROUPS * MOE_EPG
MOE_FF = 256

COL_S5, COL_HQ, COL_HF, COL_HI, COL_HG = 0, 512, 1024, 1536, 2048
COL_RQ, COL_RK, COL_RV, COL_RG = 2560, 2816, 3072, 3584
COL_MZ, COL_MXS, COL_MBC, COL_MDT = 4096, 4608, 5120, 5632
IN_W = 5640
IN_W_PAD = 5760

LANE = 128
VMEM_LIMIT = 56 * 1024 * 1024

TM_PROJ = 1024
TN_PROJ = 1024
TM_INPROJ = 512
LOG2_E = 1.4426950408889634
C_RET = 512
C_SSD = 256
C_HG = 128
TM_X = 512
X_SUB = 128
TM_COMB = 1024
SC_WINDOW = 128
SLAB = 256
N_SLAB = D_MODEL // 2 // SLAB


def _cparams(sem):
    return pltpu.CompilerParams(dimension_semantics=sem, vmem_limit_bytes=VMEM_LIMIT)


def _silu(v):
    return v * jax.nn.sigmoid(v)


def _dot_nt(a, b, **kw):
    return lax.dot_general(a, b, (((1,), (1,)), ((), ())), preferred_element_type=F32, **kw)


def _dot_tn(a, b, **kw):
    return lax.dot_general(a, b, (((0,), (0,)), ((), ())), preferred_element_type=F32, **kw)


def _ada_kernel(ct_ref, w_ref, b_ref, o_ref, *, n_rows):
    cond_t = _silu(ct_ref[...])
    w = w_ref[...]
    rows = [jnp.sum(w * cond_t[:, b:b + 1], axis=0, keepdims=True) for b in range(n_rows)]
    rows += [jnp.zeros_like(rows[0])] * (cond_t.shape[1] - n_rows)
    o_ref[...] = jnp.concatenate(rows, axis=0) + b_ref[...]


def _ada_mod(c_pad_t, ada_w, ada_b, n_rows):
    depth, d, n = ada_w.shape
    tn = 1536
    return pl.pallas_call(
        functools.partial(_ada_kernel, n_rows=n_rows),
        grid=(depth, n // tn),
        in_specs=[pl.BlockSpec((d, 8), lambda l, j: (0, 0)),
                  pl.BlockSpec((None, d, tn), lambda l, j: (l, 0, j)),
                  pl.BlockSpec((None, 1, tn), lambda l, j: (l, 0, j))],
        out_specs=pl.BlockSpec((None, 8, tn), lambda l, j: (l, 0, j)),
        out_shape=jax.ShapeDtypeStruct((depth, 8, n), F32),
        compiler_params=_cparams(("parallel", "parallel")),
        name="ada_mod",
    )(c_pad_t, ada_w, ada_b.reshape(depth, 1, n))


def _pack_bf16_pairs(x):
    n = x.shape[1] // 2
    lo = pltpu.bitcast(x[:, :n].astype(BF16).astype(F32), jnp.uint32) >> 16
    hi = pltpu.bitcast(x[:, n:].astype(BF16).astype(F32), jnp.uint32)
    return hi | lo


def _unpack_bf16_pairs(w):
    lo = pltpu.bitcast(w << 16, F32)
    hi = pltpu.bitcast(w & jnp.uint32(0xFFFF0000), F32)
    return jnp.concatenate([lo, hi], axis=-1)


def _modulated_norm(x, scale, shift):
    ms = jnp.mean(x * x, axis=-1, keepdims=True)
    return x * lax.rsqrt(ms + EPS) * (1.0 + scale) + shift


def _inproj_kernel(x_ref, sc_ref, sh_ref, w_ref, wtail_ref, o_ref, u_ref):
    h = _modulated_norm(x_ref[...], sc_ref[...], sh_ref[...]).astype(BF16)
    n_main = w_ref.shape[1]
    for n0 in range(0, n_main, TN_PROJ):
        n1 = min(n0 + TN_PROJ, n_main)
        p = jnp.dot(h, w_ref[:, n0:n1], preferred_element_type=F32)
        o_ref[:, n0:n1] = p.astype(o_ref.dtype)
        if n0 == 0:
            u_ref[...] = p[:, COL_S5:COL_S5 + BRANCH_W]
    o_ref[:, n_main:] = jnp.dot(h, wtail_ref[...], preferred_element_type=F32).astype(o_ref.dtype)


def _in_proj(x2, mod3, w_main, w_tail, layer, seq):
    t, d = x2.shape
    tm = TM_INPROJ
    tpb = seq // tm
    n_main = w_main.shape[2]
    assert COL_S5 + BRANCH_W <= TN_PROJ and n_main + w_tail.shape[2] == IN_W_PAD
    return pl.pallas_call(
        _inproj_kernel,
        grid=(t // tm,),
        in_specs=[pl.BlockSpec((tm, d), lambda i: (i, 0)),
                  pl.BlockSpec((None, 1, d), lambda i: ((i // tpb) * 6 + 1, 0, 0)),
                  pl.BlockSpec((None, 1, d), lambda i: ((i // tpb) * 6 + 0, 0, 0)),
                  pl.BlockSpec((None, d, n_main), lambda i: (layer, 0, 0), pipeline_mode=pl.Buffered(1)),
                  pl.BlockSpec((None, d, IN_W_PAD - n_main), lambda i: (layer, 0, 0), pipeline_mode=pl.Buffered(1))],
        out_specs=[pl.BlockSpec((tm, IN_W_PAD), lambda i: (i, 0)),
                   pl.BlockSpec((tm, BRANCH_W), lambda i: (i, 0))],
        out_shape=[jax.ShapeDtypeStruct((t, IN_W_PAD), BF16), jax.ShapeDtypeStruct((t, BRANCH_W), F32)],
        compiler_params=_cparams(("parallel",)),
        name="in_proj",
    )(x2, mod3, mod3, w_main, w_tail)


def _rope_kernel(pos_ref, invf_ref, ecos_ref, esin_ref, cos_ref, sin_ref):
    ang = invf_ref[:, 0:1] * pos_ref[...].astype(F32)
    def spread(values, e_ref):
        hi = values.astype(BF16)
        rest = values - hi.astype(F32)
        mid = rest.astype(BF16)
        lo = (rest - mid.astype(F32)).astype(BF16)
        e = e_ref[...]
        return _dot_tn(hi, e) + _dot_tn(mid, e) + _dot_tn(lo, e)

    cos_ref[...] = spread(jnp.cos(ang), ecos_ref)
    sin_ref[...] = spread(jnp.sin(ang), esin_ref)


def _rope_tables(pos_row, invf_col, expand_cos, expand_sin):
    t = pos_row.shape[1]
    half, w = expand_cos.shape
    tm = 1024
    const = lambda shape: pl.BlockSpec(shape, lambda i: (0, 0))
    return pl.pallas_call(
        _rope_kernel,
        grid=(t // tm,),
        in_specs=[pl.BlockSpec((1, tm), lambda i: (0, i)), const((half, LANE)), const((half, w)), const((half, w))],
        out_specs=[pl.BlockSpec((tm, w), lambda i: (i, 0))] * 2,
        out_shape=[jax.ShapeDtypeStruct((t, w), F32)] * 2,
        compiler_params=_cparams(("parallel",)),
        name="rope_tables",
    )(pos_row, invf_col, expand_cos, expand_sin)


def _ret_kernel(q_ref, k_ref, v_ref, g_ref, cos_ref, sin_ref, o_ref, st_ref, dec_ref, *, chunk):
    @pl.when(pl.program_id(1) == 0)
    def _():
        st_ref[...] = jnp.zeros_like(st_ref)
        ti = lax.broadcasted_iota(jnp.int32, (chunk, chunk), 0)
        si = lax.broadcasted_iota(jnp.int32, (chunk, chunk), 1)
        lag = (ti - si).astype(F32)
        for h in range(RET_HEADS):
            log_gamma = math.log1p(-(2.0 ** (-5.0 - h)))
            dec_ref[h] = jnp.where(ti >= si, jnp.exp(jnp.minimum(lag * log_gamma, 0.0)), 0.0)

    cosf = cos_ref[...]
    sinf = sin_ref[...]
    width = RET_HEADS * RET_DK
    lane = lax.broadcasted_iota(jnp.int32, (chunk, width), 1)
    first_half = (lane % RET_DK) < (RET_DK // 2)

    def rope(t):
        partner = jnp.where(first_half, pltpu.roll(t, width - RET_DK // 2, 1), pltpu.roll(t, RET_DK // 2, 1))
        return t * cosf + partner * sinf

    q = rope(q_ref[...].astype(F32))
    k = rope(k_ref[...].astype(F32)) * (RET_DK ** -0.5)
    v = v_ref[...]
    g = g_ref[...].astype(F32)
    tcol = lax.broadcasted_iota(jnp.int32, (chunk, 1), 0).astype(F32)
    for h in range(RET_HEADS):
        log_gamma = math.log1p(-(2.0 ** (-5.0 - h)))
        qh = q[:, h * RET_DK:(h + 1) * RET_DK]
        kh = k[:, h * RET_DK:(h + 1) * RET_DK]
        vh = v[:, h * RET_DV:(h + 1) * RET_DV].astype(BF16)
        scores = _dot_nt(qh.astype(BF16), kh.astype(BF16)) * dec_ref[h]
        state = st_ref[h]
        q_in = qh * jnp.exp(log_gamma * (tcol + 1.0))
        o = (jnp.dot(scores.astype(BF16), vh, preferred_element_type=F32)
             + jnp.dot(q_in.astype(BF16), state.astype(BF16), preferred_element_type=F32))
        k_out = kh * jnp.exp(log_gamma * (chunk - 1.0 - tcol))
        st_ref[h] = math.exp(log_gamma * chunk) * state + _dot_tn(k_out.astype(BF16), vh)
        o = o * lax.rsqrt(jnp.mean(o * o, axis=-1, keepdims=True) + EPS)
        gh = g[:, h * RET_DV:(h + 1) * RET_DV]
        o_ref[:, h * RET_DV:(h + 1) * RET_DV] = (o * _silu(gh)).astype(o_ref.dtype)


def _retention(p3, cos3, sin3):
    b, seq, _ = p3.shape
    c = C_RET
    qk_w = RET_HEADS * RET_DK
    return pl.pallas_call(
        functools.partial(_ret_kernel, chunk=c),
        grid=(b, seq // c),
        in_specs=[pl.BlockSpec((None, c, qk_w), lambda i, j: (i, j, COL_RQ // qk_w)),
                  pl.BlockSpec((None, c, qk_w), lambda i, j: (i, j, COL_RK // qk_w)),
                  pl.BlockSpec((None, c, BRANCH_W), lambda i, j: (i, j, COL_RV // BRANCH_W)),
                  pl.BlockSpec((None, c, BRANCH_W), lambda i, j: (i, j, COL_RG // BRANCH_W)),
                  pl.BlockSpec((None, c, qk_w), lambda i, j: (i, j, 0)),
                  pl.BlockSpec((None, c, qk_w), lambda i, j: (i, j, 0))],
        out_specs=pl.BlockSpec((None, c, BRANCH_W), lambda i, j: (i, j, 0)),
        out_shape=jax.ShapeDtypeStruct((b, seq, BRANCH_W), BF16),
        scratch_shapes=[pltpu.VMEM((RET_HEADS, RET_DK, RET_DV), F32), pltpu.VMEM((RET_HEADS, c, c), F32)],
        compiler_params=_cparams(("parallel", "arbitrary")),
        name="retention",
    )(p3, p3, p3, p3, cos3, sin3)


def _ssd_kernel(z_ref, xs_ref, bc_ref, dt_ref, tri_ref, cw_ref, cb_ref, dtb_ref, alog_ref, dsk_ref, nw_ref,
                o_ref, xe_scr, st_ref, *, chunk):
    j = pl.program_id(1)
    width = 2 * BRANCH_W

    @pl.when(j == 0)
    def _():
        st_ref[...] = jnp.zeros_like(st_ref)
        xe_scr[0:8, :] = jnp.zeros((8, width), F32)

    @pl.when(j > 0)
    def _():
        xe_scr[0:8, :] = xe_scr[chunk:chunk + 8, :]

    xe_scr[8:, 0:BRANCH_W] = xs_ref[...].astype(F32)
    xe_scr[8:, BRANCH_W:] = bc_ref[...].astype(F32)
    conv = cb_ref[...] + cw_ref[M2_CONV - 1:M2_CONV, :] * xe_scr[8:, :]
    for tap in range(M2_CONV - 1):
        conv = conv + cw_ref[tap:tap + 1, :] * xe_scr[pl.ds(8 - (M2_CONV - 1) + tap, chunk), :]
    conv = _silu(conv)
    xs = conv[:, :BRANCH_W]
    bm = conv[:, BRANCH_W:BRANCH_W + M2_GROUPS * M2_STATE]
    cm = conv[:, BRANCH_W + M2_GROUPS * M2_STATE:]

    dt = jax.nn.softplus(dt_ref[...].astype(F32) + dtb_ref[...])
    da = dt * (-jnp.exp(alog_ref[...]))
    da_hi = da.astype(BF16)
    da_r = da - da_hi.astype(F32)
    da_mid = da_r.astype(BF16)
    da_lo = (da_r - da_mid.astype(F32)).astype(BF16)
    tri = tri_ref[...]
    a_cs = (jnp.dot(tri, da_hi, preferred_element_type=F32) + jnp.dot(tri, da_mid, preferred_element_type=F32)
            + jnp.dot(tri, da_lo, preferred_element_type=F32))
    a_cs = a_cs * LOG2_E
    a_cs_t = a_cs.T
    ti = lax.broadcasted_iota(jnp.int32, (chunk, chunk), 0)
    si = lax.broadcasted_iota(jnp.int32, (chunk, chunk), 1)
    causal = ti >= si
    hpg = M2_HEADS // M2_GROUPS
    ys = []
    for grp in range(M2_GROUPS):
        bm_g = bm[:, grp * M2_STATE:(grp + 1) * M2_STATE]
        cm_g = cm[:, grp * M2_STATE:(grp + 1) * M2_STATE]
        cb = _dot_nt(cm_g.astype(BF16), bm_g.astype(BF16))
        for hh in range(hpg):
            h = grp * hpg + hh
            col = a_cs[:, h:h + 1]
            row = a_cs_t[h:h + 1, :]
            lmat = jnp.where(causal, jnp.exp2(col - row), 0.0)
            xd = xs[:, h * M2_HEADDIM:(h + 1) * M2_HEADDIM] * dt[:, h:h + 1]
            state = st_ref[h]
            y = (jnp.dot((cb * lmat).astype(BF16), xd.astype(BF16), preferred_element_type=F32)
                 + jnp.dot((cm_g * jnp.exp2(col)).astype(BF16), state.astype(BF16), preferred_element_type=F32))
            a_last = a_cs[chunk - 1:chunk, h:h + 1]
            to_end = jnp.exp2(a_last - col)
            st_ref[h] = jnp.exp2(a_last) * state + _dot_tn(bm_g.astype(BF16), (xd * to_end).astype(BF16))
            ys.append(y)
    y = jnp.concatenate(ys, axis=-1) + dsk_ref[...] * xs
    y = y * _silu(z_ref[...].astype(F32))
    o_ref[...] = (y * lax.rsqrt(jnp.mean(y * y, axis=-1, keepdims=True) + EPS) * nw_ref[...]).astype(o_ref.dtype)


def _ssd(p3, tri, conv_w, conv_b, dt_bias_row, a_log_row, d_skip_row, norm_w_row):
    b, seq, _ = p3.shape
    c = C_SSD
    const = lambda shape: pl.BlockSpec(shape, lambda i, j: (0,) * len(shape))
    return pl.pallas_call(
        functools.partial(_ssd_kernel, chunk=c),
        grid=(b, seq // c),
        in_specs=[pl.BlockSpec((None, c, BRANCH_W), lambda i, j: (i, j, COL_MZ // BRANCH_W)),
                  pl.BlockSpec((None, c, BRANCH_W), lambda i, j: (i, j, COL_MXS // BRANCH_W)),
                  pl.BlockSpec((None, c, BRANCH_W), lambda i, j: (i, j, COL_MBC // BRANCH_W)),
                  pl.BlockSpec((None, c, LANE), lambda i, j: (i, j, COL_MDT // LANE)),
                  const((c, c)), const((M2_CONV, 2 * BRANCH_W)), const((1, 2 * BRANCH_W)),
                  const((1, LANE)), const((1, LANE)), const((1, BRANCH_W)), const((1, BRANCH_W))],
        out_specs=pl.BlockSpec((None, c, BRANCH_W), lambda i, j: (i, j, 0)),
        out_shape=jax.ShapeDtypeStruct((b, seq, BRANCH_W), BF16),
        scratch_shapes=[pltpu.VMEM((c + 8, 2 * BRANCH_W), F32),
                        pltpu.VMEM((M2_HEADS, M2_STATE, M2_HEADDIM), F32)],
        compiler_params=_cparams(("parallel", "arbitrary")),
        name="ssd",
    )(p3, p3, p3, p3, tri, conv_w, conv_b, dt_bias_row, a_log_row, d_skip_row, norm_w_row)


def _hg_tables(chunk):
    n_lev = int(math.log2(chunk))
    r = np.arange(chunk)[:, None]
    jj = np.arange(chunk)[None, :]
    tri = (jj <= r).astype(np.float32)
    x = r ^ jj
    levmap = np.where(r > jj, np.floor(np.log2(x + 0.5)), np.where(r == jj, -1, -2)).astype(np.int32)
    return tri, levmap, n_lev


def _hg_level_exponent(b, lev):
    rows, width = b.shape
    m = 1 << lev
    sub = 8
    if 2 * m >= sub:
        blocks = b.reshape(rows // (2 * m), 2 * m, width)
        mid = jnp.broadcast_to(blocks[:, m - 1:m, :], blocks.shape).reshape(rows, width)
    else:
        groups = b.reshape(rows // sub, sub, width)
        row_in_group = lax.broadcasted_iota(jnp.int32, groups.shape, 1)
        mid = None
        for start in range(0, sub, 2 * m):
            picked = jnp.broadcast_to(groups[:, start + m - 1:start + m, :], groups.shape)
            mid = picked if mid is None else jnp.where(row_in_group >= start, picked, mid)
        mid = mid.reshape(rows, width)
    return -jnp.abs(b - mid)


def _hg_kernel(q_ref, f_ref, i_ref, g_ref, sum_ref, lev_ref, llb_ref, l1m_ref, nw_ref, o_ref, st_ref,
               *, chunk, n_lev):
    @pl.when(pl.program_id(1) == 0)
    def _():
        st_ref[...] = jnp.zeros_like(st_ref)

    f = f_ref[...].astype(F32)
    y = jnp.exp(-jnp.abs(f))
    one_plus_y = 1.0 + y
    log_sig = jnp.minimum(f, 0.0) - jnp.log(one_plus_y)
    a = llb_ref[...]
    bb = l1m_ref[...] + log_sig
    log_f = jnp.maximum(a, bb) + jnp.log(1.0 + jnp.exp(-jnp.abs(a - bb)))
    k_all = jnp.exp(l1m_ref[...]) * (jnp.where(f >= 0.0, y, 1.0) / one_plus_y)
    q_all = _silu(q_ref[...].astype(F32))
    hi = log_f.astype(BF16)
    r1 = log_f - hi.astype(F32)
    mid = r1.astype(BF16)
    lo = (r1 - mid.astype(F32)).astype(BF16)
    tri = sum_ref[...]
    b_all = (jnp.dot(tri, hi, preferred_element_type=F32)
             + jnp.dot(tri, mid, preferred_element_type=F32)
             + jnp.dot(tri, lo, preferred_element_type=F32))
    b_all = b_all * LOG2_E
    to_end_all = b_all[chunk - 1:chunk, :] - b_all
    level_decay = [jnp.exp2(_hg_level_exponent(b_all, lev)) for lev in range(n_lev)]
    levmap = lev_ref[...]
    on_diag = levmap == -1
    on_level = [levmap == lev for lev in range(n_lev)]
    v_all = i_ref[...]
    g_all = g_ref[...].astype(F32)
    for h in range(HG_HEADS):
        sl = slice(h * HG_DK, (h + 1) * HG_DK)
        qh = q_all[:, sl]
        kh = k_all[:, sl]
        vh = v_all[:, sl].astype(BF16)
        b_h = b_all[:, sl]
        to_end = to_end_all[:, sl]
        amat = jnp.where(on_diag, _dot_nt(qh.astype(BF16), kh.astype(BF16)), 0.0)
        for lev in range(n_lev):
            e = level_decay[lev][:, sl]
            a_l = _dot_nt((qh * e).astype(BF16), (kh * e).astype(BF16))
            amat = jnp.where(on_level[lev], a_l, amat)
        state_t = st_ref[h]
        o = (jnp.dot(amat.astype(BF16), vh, preferred_element_type=F32)
             + _dot_nt((qh * jnp.exp2(b_h)).astype(BF16), state_t.astype(BF16)))
        k_end = kh * jnp.exp2(to_end)
        st_ref[h] = jnp.exp2(b_h[chunk - 1:chunk, :]) * state_t + _dot_tn(vh, k_end.astype(BF16))
        o = o * lax.rsqrt(jnp.mean(o * o, axis=-1, keepdims=True) + EPS) * nw_ref[...]
        o_ref[:, sl] = (o * _silu(g_all[:, sl])).astype(o_ref.dtype)


def _hgrn2(p3, log_lb_row, log1m_lb_row, norm_w_row):
    b, seq, _ = p3.shape
    c = C_HG
    tri, levmap, n_lev = _hg_tables(c)
    const = lambda shape: pl.BlockSpec(shape, lambda i, j: (0,) * len(shape))
    blk = lambda col: pl.BlockSpec((None, c, BRANCH_W), lambda i, j: (i, j, col // BRANCH_W))
    return pl.pallas_call(
        functools.partial(_hg_kernel, chunk=c, n_lev=n_lev),
        grid=(b, seq // c),
        in_specs=[blk(COL_HQ), blk(COL_HF), blk(COL_HI), blk(COL_HG),
                  const((c, c)), const((c, c)),
                  const((1, BRANCH_W)), const((1, BRANCH_W)), const((1, HG_DK))],
        out_specs=pl.BlockSpec((None, c, BRANCH_W), lambda i, j: (i, j, 0)),
        out_shape=jax.ShapeDtypeStruct((b, seq, BRANCH_W), BF16),
        scratch_shapes=[pltpu.VMEM((HG_HEADS, HG_DK, HG_DK), F32)],
        compiler_params=_cparams(("parallel", "arbitrary")),
        name="hgrn2",
    )(p3, p3, p3, p3, jnp.asarray(tri, BF16), jnp.asarray(levmap), log_lb_row, log1m_lb_row, norm_w_row)


def _expand_block_diag(comp_ref, e_ref, dst_ref, row_div, lane_div, causal=False):
    gq = LANE // S5_CH
    rows, ncols = dst_ref.shape
    step = 512
    for c0 in range(0, ncols, step):
        r1 = min(rows, c0 + step) if causal else rows
        row_grp = (lax.broadcasted_iota(jnp.int32, (r1, step), 0) // row_div) % gq
        lane_grp = ((lax.broadcasted_iota(jnp.int32, (r1, step), 1) + c0) // lane_div) % gq
        full = jnp.dot(comp_ref[0:r1, :], e_ref[:, c0:c0 + step], preferred_element_type=F32)
        dst_ref[0:r1, c0:c0 + step] = jnp.where(row_grp == lane_grp, full, 0.0).astype(dst_ref.dtype)


def _s5_kernel(u_ref, k2_ref, bc_ref, cc_ref, esc_ref, lam_ref, o_ref, tc_scr, tq_ref, bqt_ref, cq_ref,
               x_scr, w_scr, s_scr, *, rows):
    nb = S5_BLOCK

    @pl.when(pl.program_id(1) == 0)
    def _():
        k2 = k2_ref[...]
        lane = lax.broadcasted_iota(jnp.int32, k2.shape, 1)
        for t in range(nb):
            shifted = k2 if t == 0 else jnp.where(lane >= t * S5_CH, pltpu.roll(k2, t * S5_CH, 1), 0.0)
            tc_scr[t * LANE:(t + 1) * LANE, :] = shifted.astype(tc_scr.dtype)
        _expand_block_diag(tc_scr, esc_ref, tq_ref, S5_CH, S5_CH, causal=True)
        _expand_block_diag(bc_ref, esc_ref, bqt_ref, S5_STATE, S5_CH)
        _expand_block_diag(cc_ref, esc_ref, cq_ref, S5_STATE, S5_CH)

    n_seq = u_ref.shape[0]
    for b in range(n_seq):
        for t in range(nb):
            x_scr[b * rows:(b + 1) * rows, t * LANE:(t + 1) * LANE] = (
                u_ref[b, pl.ds(t, rows, stride=nb), :].astype(x_scr.dtype))
    x = x_scr[...]
    half = w_scr.shape[1] // 2
    w_scr[...] = _dot_nt(x, bqt_ref[...])
    lam_re = lam_ref[0:1, :]
    lam_im = lam_ref[1:2, :]

    def body(j, carry):
        out = []
        for b in range(n_seq):
            s_re, s_im = carry[2 * b], carry[2 * b + 1]
            r = b * rows + j
            s_scr[pl.ds(r, 1), 0:half] = s_re
            s_scr[pl.ds(r, 1), half:] = s_im
            w_re = w_scr[pl.ds(r, 1), 0:half]
            w_im = w_scr[pl.ds(r, 1), half:]
            out += [lam_re * s_re - lam_im * s_im + w_re, lam_re * s_im + lam_im * s_re + w_im]
        return tuple(out)

    zero = jnp.zeros((1, half), F32)
    lax.fori_loop(0, rows, body, (zero,) * (2 * n_seq))
    s_bf = s_scr[...].astype(BF16)
    pair = 2 * LANE
    for c0 in range(0, nb * LANE, pair):
        k_rows = c0 + pair
        y = (jnp.dot(x[:, :k_rows], tq_ref[0:k_rows, c0:c0 + pair], preferred_element_type=F32)
             + jnp.dot(s_bf, cq_ref[:, c0:c0 + pair], preferred_element_type=F32))
        for b in range(n_seq):
            for t in range(c0 // LANE, (c0 + pair) // LANE):
                o_ref[b, pl.ds(t, rows, stride=nb), :] = y[b * rows:(b + 1) * rows, t * LANE - c0:(t + 1) * LANE - c0]


def _s5_scan(p3, k2, bc, cc, lam16):
    batch, seq, _ = p3.shape
    nb = S5_BLOCK
    nq = BRANCH_W // LANE
    rows = seq // nb
    kdim = nb * LANE
    gq = LANE // S5_CH
    ncol = 2 * gq * S5_STATE
    e_sc = (np.eye(nb)[:, None, :, None, None] * np.eye(S5_CH)[None, :, None, None, :] * np.ones((1, 1, 1, gq, 1)))
    e_sc = e_sc.reshape(nb * S5_CH, nb * gq * S5_CH)
    full = lambda shape: pl.BlockSpec(shape, lambda q, b: (0,) * len(shape))
    per_q = lambda r, c: pl.BlockSpec((None, r, c), lambda q, b: (q, 0, 0))
    n_seq = S5_SEQ_PER_STEP if batch % S5_SEQ_PER_STEP == 0 else 1
    return pl.pallas_call(
        functools.partial(_s5_kernel, rows=rows),
        grid=(nq, batch // n_seq),
        in_specs=[pl.BlockSpec((n_seq, seq, LANE), lambda q, b: (b, 0, q)),
                  per_q(LANE, nb * S5_CH), per_q(ncol, nb * S5_CH), per_q(ncol, nb * S5_CH),
                  full(e_sc.shape), per_q(2, ncol // 2)],
        out_specs=pl.BlockSpec((n_seq, seq, LANE), lambda q, b: (b, 0, q)),
        out_shape=jax.ShapeDtypeStruct((batch, seq, BRANCH_W), F32),
        scratch_shapes=[pltpu.VMEM((kdim, nb * S5_CH), BF16),
                        pltpu.VMEM((kdim, kdim), BF16), pltpu.VMEM((ncol, kdim), BF16), pltpu.VMEM((ncol, kdim), BF16),
                        pltpu.VMEM((n_seq * rows, kdim), BF16), pltpu.VMEM((n_seq * rows, ncol), F32),
                        pltpu.VMEM((n_seq * rows, ncol), F32)],
        compiler_params=_cparams(("parallel", "arbitrary")),
        name="s5_scan",
    )(p3, k2, bc, cc, jnp.asarray(e_sc, BF16), lam16)


def _s5_operators(lam_re, lam_im, b_re, b_im, c_re, c_im, d_skip, log_dt):
    nb = S5_BLOCK
    gq = LANE // S5_CH
    nq = S5_GROUPS // gq
    lam = lax.complex(jnp.minimum(lam_re.astype(F32), S5_MAX_REAL), lam_im.astype(F32))
    step = jnp.exp(log_dt.astype(F32))[:, None]
    z = lam * step
    lam_bar = jnp.exp(z)
    b_bar = ((lam_bar - 1.0) / lam)[..., None] * lax.complex(b_re.astype(F32), b_im.astype(F32))
    c_mat = lax.complex(c_re.astype(F32), c_im.astype(F32))
    pw = jnp.exp(z[..., None] * jnp.arange(nb + 1, dtype=F32))
    cp = c_mat[:, None, :, :] * pw[..., :nb].transpose(0, 2, 1)[:, :, None, :]
    cp = jnp.concatenate([cp.real, -cp.imag], axis=-1).reshape(S5_GROUPS, nb * S5_CH, 2 * S5_STATE)
    bri = jnp.concatenate([b_bar.real, b_bar.imag], axis=1)
    kern = jnp.einsum('gnk,gki->gin', cp, bri, precision=HIGHEST)
    skip = (jnp.asarray(np.concatenate([np.eye(S5_CH), np.zeros((S5_CH, (nb - 1) * S5_CH))], axis=1), F32)[None]
            * d_skip.astype(F32).reshape(S5_GROUPS, S5_CH, 1))
    k2 = (kern + skip).reshape(nq, gq * S5_CH, nb * S5_CH)
    pw_rev = jnp.exp(z[..., None] * jnp.asarray(np.arange(nb - 1, -1, -1), F32))
    binc = pw_rev[:, :, :, None] * b_bar[:, :, None, :]
    binc = jnp.stack([binc.real, binc.imag], axis=0).reshape(2, nq, gq * S5_STATE, nb * S5_CH)
    bc = binc.transpose(1, 0, 2, 3).reshape(nq, 2 * gq * S5_STATE, nb * S5_CH)
    cm = c_mat.transpose(0, 2, 1)[:, :, None, :] * pw[..., 1:][:, :, :, None]
    cm = jnp.stack([cm.real, -cm.imag], axis=0).reshape(2, nq, gq * S5_STATE, nb * S5_CH)
    cc = cm.transpose(1, 0, 2, 3).reshape(nq, 2 * gq * S5_STATE, nb * S5_CH)
    lam_n = pw[..., nb].reshape(nq, gq * S5_STATE)
    lam16 = jnp.stack([lam_n.real, lam_n.imag], axis=1)
    return k2, bc.astype(BF16), cc.astype(BF16), lam16


def _merge_kernel(x_ref, sc_ref, sh_ref, gm_ref, ys5_ref, yhg_ref, yret_ref, ym2_ref,
                  wglu_ref, wbr_ref, wg_ref, bg_ref, wout_ref, o_ref):
    x = x_ref[...]
    d = x.shape[1]
    h = _modulated_norm(x, sc_ref[...], sh_ref[...]).astype(BF16)
    y_s5 = jax.nn.gelu(ys5_ref[...])
    y_s5 = y_s5 * jax.nn.sigmoid(jnp.dot(y_s5.astype(BF16), wglu_ref[...], preferred_element_type=F32))
    acc = jnp.zeros(x.shape, F32)
    for n, y in enumerate((y_s5, yhg_ref[...], yret_ref[...], ym2_ref[...])):
        gate = jax.nn.sigmoid(jnp.dot(h, wg_ref[:, n * d:(n + 1) * d], preferred_element_type=F32)
                              + bg_ref[:, n * d:(n + 1) * d])
        acc = acc + gate * jnp.dot(y.astype(BF16), wbr_ref[n], preferred_element_type=F32)
    o_ref[...] = x + gm_ref[...] * jnp.dot(acc.astype(BF16), wout_ref[...], preferred_element_type=F32)


def _merge(x2, mod3, ys5, yhg, yret, ym2, w_glu, w_branch, w_gate, b_gate, w_out, layer, seq):
    t, d = x2.shape
    tm = TM_PROJ
    tpb = seq // tm
    const = lambda shape: pl.BlockSpec((None,) + shape, lambda i: (layer,) + (0,) * len(shape),
                                       pipeline_mode=pl.Buffered(1))
    modspec = lambda k: pl.BlockSpec((None, 1, d), lambda i: ((i // tpb) * 6 + k, 0, 0))
    yspec = pl.BlockSpec((tm, BRANCH_W), lambda i: (i, 0))
    return pl.pallas_call(
        _merge_kernel,
        grid=(t // tm,),
        in_specs=[pl.BlockSpec((tm, d), lambda i: (i, 0)), modspec(1), modspec(0), modspec(2),
                  yspec, yspec, yspec, yspec,
                  const((BRANCH_W, BRANCH_W)), const((4, BRANCH_W, d)), const((d, 4 * d)), const((1, 4 * d)),
                  const((d, d))],
        out_specs=pl.BlockSpec((tm, d), lambda i: (i, 0)),
        out_shape=jax.ShapeDtypeStruct((t, d), F32),
        compiler_params=_cparams(("parallel",)),
        name="merge",
    )(x2, mod3, mod3, mod3, ys5, yhg, yret, ym2, w_glu, w_branch, w_gate, b_gate, w_out)


def _router_kernel(x_ref, sc_ref, sh_ref, wr_ref, br_ref, tri_ref, h_ref, ids_ref, wts_ref, cnt_ref, carry):
    i = pl.program_id(0)

    @pl.when(i == 0)
    def _():
        carry[...] = jnp.zeros_like(carry)

    h = _modulated_norm(x_ref[...], sc_ref[...], sh_ref[...])
    tm, d = h.shape
    packed = _pack_bf16_pairs(h)
    for k in range(N_SLAB):
        h_ref[k] = packed[:, k * SLAB:(k + 1) * SLAB]
    h_hi = h.astype(BF16)
    h_lo = (h - h_hi.astype(F32)).astype(BF16)
    w_r = wr_ref[...]
    w_hi = w_r.astype(BF16)
    w_lo = (w_r - w_hi.astype(F32)).astype(BF16)
    logits = _dot_nt(w_hi, h_hi) + _dot_nt(w_hi, h_lo) + _dot_nt(w_lo, h_hi) + br_ref[:, 0:1]
    gl = [logits[g:g + 1, :] for g in range(MOE_GROUPS)]
    gmax = gl[0]
    gsel = jnp.zeros((1, tm), jnp.int32)
    for g in range(1, MOE_GROUPS):
        better = gl[g] > gmax
        gsel = jnp.where(better, g, gsel)
        gmax = jnp.where(better, gl[g], gmax)
    gden = gl[0] * 0.0
    for g in range(MOE_GROUPS):
        gden = gden + jnp.exp(gl[g] - gmax)
    g_w = 1.0 / gden
    el = []
    for e in range(MOE_EPG):
        v = logits[MOE_GROUPS + e:MOE_GROUPS + e + 1, :]
        for g in range(1, MOE_GROUPS):
            row = MOE_GROUPS + g * MOE_EPG + e
            v = jnp.where(gsel == g, logits[row:row + 1, :], v)
        el.append(v)
    v1 = el[0]
    i1 = jnp.zeros((1, tm), jnp.int32)
    for e in range(1, MOE_EPG):
        better = el[e] > v1
        i1 = jnp.where(better, e, i1)
        v1 = jnp.where(better, el[e], v1)
    v2 = jnp.full((1, tm), -jnp.inf, F32)
    i2 = jnp.zeros((1, tm), jnp.int32)
    for e in range(MOE_EPG):
        better = (el[e] > v2) & (i1 != e)
        i2 = jnp.where(better, e, i2)
        v2 = jnp.where(better, el[e], v2)
    ex = jnp.exp(v2 - v1)
    p1 = 1.0 / (1.0 + ex)
    e1 = gsel * MOE_EPG + i1
    e2 = gsel * MOE_EPG + i2
    erow = lax.broadcasted_iota(jnp.int32, (MOE_EXPERTS, tm), 0)
    oh1 = (erow == e1).astype(F32)
    oh2 = (erow == e2).astype(F32)
    both = oh1 + oh2
    n_grp = tm // LANE
    stacked = jnp.concatenate([both[:, g * LANE:(g + 1) * LANE] for g in range(n_grp)], axis=0)
    within = jnp.dot(stacked.astype(BF16), tri_ref[...], preferred_element_type=F32)
    grp_count = jnp.sum(stacked, axis=1, keepdims=True)
    running = carry[:, 0:1]
    pieces = []
    for g in range(n_grp):
        pieces.append(within[g * MOE_EXPERTS:(g + 1) * MOE_EXPERTS, :] + running)
        running = running + grp_count[g * MOE_EXPERTS:(g + 1) * MOE_EXPERTS, :]
    prefix = jnp.concatenate(pieces, axis=1)
    rank1 = jnp.sum(oh1 * prefix, axis=0, keepdims=True).astype(jnp.int32)
    rank2 = jnp.sum(oh2 * prefix, axis=0, keepdims=True).astype(jnp.int32)
    carry[...] = jnp.broadcast_to(running, carry.shape)
    zi = jnp.zeros((1, tm), jnp.int32)
    ids_ref[...] = jnp.concatenate([e1, e2, rank1, rank2, zi, zi, zi, zi], axis=0)
    wrow = lax.broadcasted_iota(jnp.int32, (LANE, tm), 0)
    wts_ref[...] = jnp.where(wrow == 0, p1 * g_w, jnp.where(wrow == 1, ex * p1 * g_w, 0.0)).T
    cnt_ref[...] = carry[...]


def _router(x2, mod3, w_route, b_route, tri_excl, seq):
    t, d = x2.shape
    tm = TM_PROJ
    tpb = seq // tm
    nr = w_route.shape[0]
    const = lambda shape: pl.BlockSpec(shape, lambda i: (0,) * len(shape))
    modspec = lambda k: pl.BlockSpec((None, 1, d), lambda i: ((i // tpb) * 6 + k, 0, 0))
    return pl.pallas_call(
        _router_kernel,
        grid=(t // tm,),
        in_specs=[pl.BlockSpec((tm, d), lambda i: (i, 0)), modspec(4), modspec(3),
                  const((nr, d)), const((nr, LANE)), const((LANE, LANE))],
        out_specs=[pl.BlockSpec((N_SLAB, tm, SLAB), lambda i: (0, i, 0)),
                   pl.BlockSpec((8, tm), lambda i: (0, i)),
                   pl.BlockSpec((tm, LANE), lambda i: (i, 0)),
                   const((MOE_EXPERTS, LANE))],
        out_shape=[jax.ShapeDtypeStruct((N_SLAB, t, SLAB), jnp.uint32),
                   jax.ShapeDtypeStruct((8, t), jnp.int32),
                   jax.ShapeDtypeStruct((t, LANE), F32),
                   jax.ShapeDtypeStruct((MOE_EXPERTS, LANE), F32)],
        scratch_shapes=[pltpu.VMEM((MOE_EXPERTS, LANE), F32)],
        compiler_params=_cparams(("arbitrary",)),
        name="moe_router",
    )(x2, mod3, mod3, w_route, b_route, tri_excl)


def _sc_mesh():
    return plsc.VectorSubcoreMesh(core_axis_name="core", subcore_axis_name="subcore")


def _slab_rows(idx, n_rows):
    return (idx[None, :] + (jnp.arange(N_SLAB, dtype=jnp.int32) * n_rows)[:, None]).reshape(-1)


def _dispatch(slot1, slot2, h_slabs):
    n_slab, t, d = h_slabs.shape
    n_out = 2 * t
    xs = _scatter_rows(h_slabs.reshape(n_slab * t, d), _slab_rows(slot1, n_out), _slab_rows(slot2, n_out),
                       n_slab * n_out)
    return xs.reshape(n_slab, n_out, d)


def _scatter_rows(src, idx1, idx2, n_out):
    t, d = src.shape
    win = SC_WINDOW

    @pl.kernel(out_type=jax.ShapeDtypeStruct((n_out, d), src.dtype), mesh=_sc_mesh(), name="moe_dispatch_sc")
    def scatter_rows(x_hbm, i1_hbm, i2_hbm, o_hbm):
        def body(x_vmem, i1_vmem, i2_vmem):
            pltpu.sync_copy(x_vmem, o_hbm.at[i1_vmem.at[0]])
            pltpu.sync_copy(x_vmem, o_hbm.at[i2_vmem.at[0]])

        pltpu.emit_pipeline(
            body,
            grid=(t // win,),
            in_specs=[pl.BlockSpec((win, d), lambda i: (i, 0)),
                      pl.BlockSpec((1, win), lambda i: (0, i)),
                      pl.BlockSpec((1, win), lambda i: (0, i))],
            out_specs=[],
            core_axis_name=("core", "subcore"),
            dimension_semantics=(pltpu.PARALLEL,),
        )(x_hbm, i1_hbm, i2_hbm)

    return scatter_rows(src, idx1.reshape(1, t), idx2.reshape(1, t))


def _gather_rows(src, idx):
    m = idx.shape[0]
    d = src.shape[1]
    win = SC_WINDOW

    @pl.kernel(out_type=jax.ShapeDtypeStruct((m, d), src.dtype), mesh=_sc_mesh(), name="moe_gather_sc")
    def gather(x_hbm, i_hbm, o_hbm):
        def body(i_vmem, o_vmem):
            pltpu.sync_copy(x_hbm.at[i_vmem.at[0]], o_vmem)

        pltpu.emit_pipeline(
            body,
            grid=(m // win,),
            in_specs=[pl.BlockSpec((1, win), lambda i: (0, i))],
            out_specs=[pl.BlockSpec((win, d), lambda i: (i, 0))],
            core_axis_name=("core", "subcore"),
            dimension_semantics=(pltpu.PARALLEL,),
        )(i_hbm, o_hbm)

    return gather(src, idx.reshape(1, m))


def _expert_kernel(tile_ref, exp_ref, lo_ref, hi_ref, xs_ref, w1_ref, w3_ref, w2_ref, ys_ref, w1_scr, w3_scr, w2_scr):
    s = pl.program_id(0)
    prev = jnp.maximum(s - 1, 0)
    new_expert = (s == 0) | (exp_ref[s] != exp_ref[prev])
    new_tile = (s == 0) | (tile_ref[s] != tile_ref[prev])

    @pl.when(new_expert)
    def _():
        w1_scr[...] = w1_ref[...].astype(BF16)
        w3_scr[...] = w3_ref[...].astype(BF16)
        w2_scr[...] = w2_ref[...].astype(BF16)

    lo = lo_ref[s]
    hi = hi_ref[s]
    n_rows = xs_ref.shape[1]

    @pl.when(new_tile)
    def _():
        ys_ref[...] = jnp.zeros_like(ys_ref)

    def run_rows(r0, n):
        x = _unpack_bf16_pairs(jnp.concatenate([xs_ref[k, r0:r0 + n, :] for k in range(N_SLAB)], axis=-1)).astype(BF16)
        a = jnp.dot(x, w1_scr[...], preferred_element_type=F32)
        b = jnp.dot(x, w3_scr[...], preferred_element_type=F32)
        act = _silu(a) * b
        y = _pack_bf16_pairs(jnp.dot(act.astype(BF16), w2_scr[...], preferred_element_type=F32))
        row = lax.broadcasted_iota(jnp.int32, (n, SLAB), 0) + r0
        mine = (row >= lo) & (row < hi)
        for k in range(N_SLAB):
            ys_ref[k, r0:r0 + n, :] = jnp.where(mine, y[:, k * SLAB:(k + 1) * SLAB], ys_ref[k, r0:r0 + n, :])

    groups = (hi + (X_SUB - 1)) // X_SUB - lo // X_SUB
    whole = groups > 2

    @pl.when(whole)
    def _():
        run_rows(0, n_rows)

    for r0 in range(0, n_rows, X_SUB):
        @pl.when(jnp.logical_not(whole) & (lo < r0 + X_SUB) & (hi > r0))
        def _(r0=r0):
            run_rows(r0, X_SUB)


def _experts(step_tile, step_expert, step_lo, step_hi, xs, w1, w3, w2, layer):
    n_slab, ns, slab = xs.shape
    d = w1.shape[1]
    ff = w1.shape[2]
    n_steps = step_tile.shape[0]
    base = layer * MOE_EXPERTS
    grid_spec = pltpu.PrefetchScalarGridSpec(
        num_scalar_prefetch=4,
        grid=(n_steps,),
        in_specs=[pl.BlockSpec((n_slab, TM_X, slab), lambda s, tl, ex, lo, hi: (0, tl[s], 0)),
                  pl.BlockSpec((None, d, ff), lambda s, tl, ex, lo, hi: (base + ex[s], 0, 0)),
                  pl.BlockSpec((None, d, ff), lambda s, tl, ex, lo, hi: (base + ex[s], 0, 0)),
                  pl.BlockSpec((None, ff, d), lambda s, tl, ex, lo, hi: (base + ex[s], 0, 0))],
        out_specs=pl.BlockSpec((n_slab, TM_X, slab), lambda s, tl, ex, lo, hi: (0, tl[s], 0)),
        scratch_shapes=[pltpu.VMEM((d, ff), BF16), pltpu.VMEM((d, ff), BF16), pltpu.VMEM((ff, d), BF16)],
    )
    return pl.pallas_call(
        _expert_kernel,
        grid_spec=grid_spec,
        out_shape=jax.ShapeDtypeStruct((n_slab, ns, slab), xs.dtype),
        compiler_params=_cparams(("arbitrary",)),
        name="moe_experts",
    )(step_tile, step_expert, step_lo, step_hi, xs, w1, w3, w2)


def _combine_kernel(x_ref, gate_ref, fw_ref, wcol_ref, y1_ref, y2_ref, o_ref, *, final):
    w_first = wcol_ref[:, 0:1]
    w_second = wcol_ref[:, 1:2]
    y_first = _unpack_bf16_pairs(jnp.concatenate([y1_ref[k] for k in range(N_SLAB)], axis=-1))
    y_second = _unpack_bf16_pairs(jnp.concatenate([y2_ref[k] for k in range(N_SLAB)], axis=-1))
    moe = w_first * y_first + w_second * y_second
    x = x_ref[...] + gate_ref[...] * moe
    if final:
        x = x * lax.rsqrt(jnp.mean(x * x, axis=-1, keepdims=True) + EPS) * fw_ref[...]
    o_ref[...] = x


def _combine(x2, mod3, final_w_row, wcol, gathered, seq, final):
    t, d = x2.shape
    tm = TM_COMB
    tpb = seq // tm
    nblk = t // tm
    yspec = lambda off: pl.BlockSpec((N_SLAB, tm, SLAB), lambda i: (0, i + off, 0))
    return pl.pallas_call(
        functools.partial(_combine_kernel, final=final),
        grid=(nblk,),
        in_specs=[pl.BlockSpec((tm, d), lambda i: (i, 0)),
                  pl.BlockSpec((None, 1, d), lambda i: ((i // tpb) * 6 + 5, 0, 0)),
                  pl.BlockSpec((1, d), lambda i: (0, 0)),
                  pl.BlockSpec((tm, LANE), lambda i: (i, 0)),
                  yspec(0), yspec(nblk)],
        out_specs=pl.BlockSpec((tm, d), lambda i: (i, 0)),
        out_shape=jax.ShapeDtypeStruct((t, d), F32),
        compiler_params=_cparams(("parallel",)),
        name="moe_combine",
    )(x2, mod3, final_w_row, wcol, gathered, gathered)


def _moe(x2, mod3, final_w_row, w_route, b_route, tri_excl, w1, w3, w2, layer, seq, final):
    t, d = x2.shape
    h3, ids, wcol, counts = _router(x2, mod3, w_route, b_route, tri_excl, seq)
    cnt = counts[:, 0].astype(jnp.int32)
    ends = jnp.cumsum(cnt)
    offs = ends - cnt
    experts = jnp.arange(MOE_EXPERTS, dtype=jnp.int32)
    pick = lambda table, idx: jnp.sum(jnp.where(idx[:, None] == experts[None, :], table[None, :], 0), axis=1)
    slot1 = pick(offs, ids[0]) + ids[2]
    slot2 = pick(offs, ids[1]) + ids[3]
    n_tiles = 2 * t // TM_X
    first_tile = offs // TM_X
    n_vis = jnp.where(cnt > 0, (ends - 1) // TM_X - first_tile + 1, 0)
    cum = jnp.cumsum(n_vis)
    step = jnp.arange(n_tiles + MOE_EXPERTS, dtype=jnp.int32)
    step_expert = jnp.minimum(jnp.sum(step[:, None] >= cum[None, :], axis=1), MOE_EXPERTS - 1).astype(jnp.int32)
    valid = step < cum[-1]
    step_tile = jnp.where(valid, pick(first_tile - (cum - n_vis), step_expert) + step, n_tiles - 1)
    step_lo = jnp.where(valid, jnp.clip(pick(offs, step_expert) - step_tile * TM_X, 0, TM_X), 0)
    step_hi = jnp.where(valid, jnp.clip(pick(ends, step_expert) - step_tile * TM_X, 0, TM_X), 0)
    xs = _dispatch(slot1, slot2, h3)
    ys = _experts(step_tile.astype(jnp.int32), step_expert, step_lo.astype(jnp.int32), step_hi.astype(jnp.int32),
                  xs, w1, w3, w2, layer)
    n_sorted = ys.shape[1]
    gathered = _gather_rows(ys.reshape(N_SLAB * n_sorted, SLAB), _slab_rows(jnp.concatenate([slot1, slot2]), n_sorted))
    gathered = gathered.reshape(N_SLAB, n_sorted, SLAB)
    return _combine(x2, mod3, final_w_row, wcol, gathered, seq, final)


def kernel(x, c, positions, ada_w, ada_b, w_in, s5_lam_re, s5_lam_im, s5_b_re, s5_b_im, s5_c_re, s5_c_im, s5_d, s5_log_dt, s5_w_glu, hg_lb_logits, hg_norm_w, m2_conv_w, m2_conv_b, m2_dt_bias, m2_a_log, m2_d, m2_norm_w, w_branch, w_gate, b_gate, w_out, moe_w_group, moe_b_group, moe_w_expert, moe_b_expert, moe_w1, moe_w3, moe_w2, final_norm_w):
    bsz, seq, d = x.shape
    t = bsz * seq
    depth = ada_w.shape[0]
    assert seq % TM_PROJ == 0 and seq % C_RET == 0 and seq % C_SSD == 0 and seq % C_HG == 0
    x2 = x.reshape(t, d).astype(F32)

    c_pad = jnp.zeros((8, d), F32).at[:bsz].set(c.astype(F32))
    mod_all = _ada_mod(c_pad.T, ada_w.astype(F32), ada_b.astype(F32), bsz)

    half = RET_DK // 2
    inv_freq = ROPE_BASE ** (-jnp.arange(half, dtype=F32) / half)
    invf_col = jnp.broadcast_to(inv_freq[:, None], (half, LANE))
    expand = np.tile(np.eye(half, dtype=np.float32), (1, 2 * RET_HEADS))
    sign = np.tile(np.concatenate([-np.ones(half), np.ones(half)]), RET_HEADS)[None, :].astype(np.float32)
    cos_t, sin_t = _rope_tables(positions.reshape(1, t).astype(jnp.int32), invf_col,
                                jnp.asarray(expand, BF16), jnp.asarray(expand * sign, BF16))
    cos3 = cos_t.reshape(bsz, seq, -1)
    sin3 = sin_t.reshape(bsz, seq, -1)

    lb_cum = jnp.cumsum(jax.nn.softmax(hg_lb_logits.astype(F32), axis=0), axis=0)
    hg_lb = lb_cum - lb_cum[:1]
    tri_ssd = jnp.asarray(np.tril(np.ones((C_SSD, C_SSD), np.float32)), BF16)
    tri_excl = jnp.asarray(np.triu(np.ones((LANE, LANE), np.float32), 1), BF16)
    final_w_row = final_norm_w.astype(F32)[None, :]
    n_main = IN_W // LANE * LANE
    w_main = w_in[:, :, :n_main].astype(BF16)
    w_tail = jnp.zeros((depth, d, IN_W_PAD - n_main), BF16).at[:, :, :IN_W - n_main].set(w_in[:, :, n_main:].astype(BF16))
    w_glu_bf = s5_w_glu.astype(BF16)
    w_branch_bf = w_branch.astype(BF16)
    w_gate_bf = w_gate.astype(BF16)
    w_out_bf = w_out.astype(BF16)
    b_gate3 = b_gate.astype(F32).reshape(depth, 1, -1)
    moe_w1_all = moe_w1.astype(F32).reshape(depth * MOE_EXPERTS, d, MOE_FF)
    moe_w3_all = moe_w3.astype(F32).reshape(depth * MOE_EXPERTS, d, MOE_FF)
    moe_w2_all = moe_w2.astype(F32).reshape(depth * MOE_EXPERTS, MOE_FF, d)

    for layer in range(depth):
        mod3 = mod_all[layer, :bsz].reshape(bsz * 6, 1, d)
        p, u_s5 = _in_proj(x2, mod3, w_main, w_tail, layer, seq)
        p3 = p.reshape(bsz, seq, IN_W_PAD)

        ops = _s5_operators(s5_lam_re[layer], s5_lam_im[layer], s5_b_re[layer], s5_b_im[layer],
                            s5_c_re[layer], s5_c_im[layer], s5_d[layer], s5_log_dt[layer])
        y_s5 = _s5_scan(u_s5.reshape(bsz, seq, BRANCH_W), *ops).reshape(t, BRANCH_W)

        lb = hg_lb[layer][None, :]
        y_hg = _hgrn2(p3, jnp.log(lb), jnp.log1p(-lb), hg_norm_w[layer].astype(F32)[None, :]).reshape(t, BRANCH_W)

        y_ret = _retention(p3, cos3, sin3).reshape(t, BRANCH_W)

        pad8 = lambda v: jnp.zeros((1, LANE), F32).at[0, :M2_HEADS].set(v.astype(F32))
        y_m2 = _ssd(p3, tri_ssd, m2_conv_w[layer].astype(F32), m2_conv_b[layer].astype(F32)[None, :],
                    pad8(m2_dt_bias[layer]), pad8(m2_a_log[layer]),
                    jnp.repeat(m2_d[layer].astype(F32), M2_HEADDIM)[None, :],
                    m2_norm_w[layer].astype(F32)[None, :]).reshape(t, BRANCH_W)

        nr = 40
        w_route = jnp.zeros((nr, d), F32).at[:MOE_GROUPS].set(moe_w_group[layer].astype(F32).T)
        w_route = w_route.at[MOE_GROUPS:MOE_GROUPS + MOE_EXPERTS].set(moe_w_expert[layer].astype(F32).T)
        b_route = jnp.zeros((nr, LANE), F32).at[:MOE_GROUPS, 0].set(moe_b_group[layer].astype(F32))
        b_route = b_route.at[MOE_GROUPS:MOE_GROUPS + MOE_EXPERTS, 0].set(moe_b_expert[layer].astype(F32))
        x2 = _merge(x2, mod3, y_s5, y_hg, y_ret, y_m2, w_glu_bf, w_branch_bf, w_gate_bf, b_gate3, w_out_bf,
                    layer, seq)
        x2 = _moe(x2, mod3, final_w_row, w_route, b_route, tri_excl, moe_w1_all, moe_w3_all, moe_w2_all,
                  layer, seq, final=(layer == depth - 1))
    return x2.reshape(bsz, seq, d)
```

```python
import functools
import math

import numpy as np
import jax
import jax.numpy as jnp
from jax import lax
from jax.experimental import pallas as pl
from jax.experimental.pallas import tpu as pltpu
from jax.experimental.pallas import tpu_sc as plsc

F32 = jnp.float32
BF16 = jnp.bfloat16
HIGHEST = lax.Precision.HIGHEST

D_MODEL = 1024
BRANCH_W = 512
EPS = 1e-6
S5_GROUPS = 32
S5_CH = 16
S5_STATE = 64
S5_MAX_REAL = -1e-4
S5_BLOCK = 16
S5_SEQ_PER_STEP = 2
HG_HEADS = 4
HG_DK = 128
RET_HEADS = 4
RET_DK = 64
RET_DV = 128
ROPE_BASE = 10000.0
M2_HEADS = 8
M2_HEADDIM = 64
M2_GROUPS = 2
M2_STATE = 128
M2_CONV = 4
MOE_GROUPS = 4
MOE_EPG = 8
MOE_EXPERTS = MOE_GROUPS * MOE_EPG
MOE_FF = 256

COL_S5, COL_HQ, COL_HF, COL_HI, COL_HG = 0, 512, 1024, 1536, 2048
COL_RQ, COL_RK, COL_RV, COL_RG = 2560, 2816, 3072, 3584
COL_MZ, COL_MXS, COL_MBC, COL_MDT = 4096, 4608, 5120, 5632
IN_W = 5640
IN_W_PAD = 5760

LANE = 128
VMEM_LIMIT = 56 * 1024 * 1024

TM_PROJ = 1024
TN_PROJ = 1024
TM_INPROJ = 512
LOG2_E = 1.4426950408889634
C_RET = 512
C_SSD = 256
C_HG = 128
TM_X = 512
X_SUB = 128
TM_COMB = 1024
SC_WINDOW = 128
SLAB = 256
N_SLAB = D_MODEL // 2 // SLAB


def _cparams(sem):
    return pltpu.CompilerParams(dimension_semantics=sem, vmem_limit_bytes=VMEM_LIMIT)


def _silu(v):
    return v * jax.nn.sigmoid(v)


def _dot_nt(a, b, **kw):
    return lax.dot_general(a, b, (((1,), (1,)), ((), ())), preferred_element_type=F32, **kw)


def _dot_tn(a, b, **kw):
    return lax.dot_general(a, b, (((0,), (0,)), ((), ())), preferred_element_type=F32, **kw)


def _ada_kernel(ct_ref, w_ref, b_ref, o_ref, *, n_rows):
    cond_t = _silu(ct_ref[...])
    w = w_ref[...]
    rows = [jnp.sum(w * cond_t[:, b:b + 1], axis=0, keepdims=True) for b in range(n_rows)]
    rows += [jnp.zeros_like(rows[0])] * (cond_t.shape[1] - n_rows)
    o_ref[...] = jnp.concatenate(rows, axis=0) + b_ref[...]


def _ada_mod(c_pad_t, ada_w, ada_b, n_rows):
    depth, d, n = ada_w.shape
    tn = 1536
    return pl.pallas_call(
        functools.partial(_ada_kernel, n_rows=n_rows),
        grid=(depth, n // tn),
        in_specs=[pl.BlockSpec((d, 8), lambda l, j: (0, 0)),
                  pl.BlockSpec((None, d, tn), lambda l, j: (l, 0, j)),
                  pl.BlockSpec((None, 1, tn), lambda l, j: (l, 0, j))],
        out_specs=pl.BlockSpec((None, 8, tn), lambda l, j: (l, 0, j)),
        out_shape=jax.ShapeDtypeStruct((depth, 8, n), F32),
        compiler_params=_cparams(("parallel", "parallel")),
        name="ada_mod",
    )(c_pad_t, ada_w, ada_b.reshape(depth, 1, n))


def _pack_bf16_pairs(x):
    n = x.shape[1] // 2
    lo = pltpu.bitcast(x[:, :n].astype(BF16).astype(F32), jnp.uint32) >> 16
    hi = pltpu.bitcast(x[:, n:].astype(BF16).astype(F32), jnp.uint32)
    return hi | lo


def _unpack_bf16_pairs(w):
    lo = pltpu.bitcast(w << 16, F32)
    hi = pltpu.bitcast(w & jnp.uint32(0xFFFF0000), F32)
    return jnp.concatenate([lo, hi], axis=-1)


def _modulated_norm(x, scale, shift):
    ms = jnp.mean(x * x, axis=-1, keepdims=True)
    return x * lax.rsqrt(ms + EPS) * (1.0 + scale) + shift


def _inproj_kernel(x_ref, sc_ref, sh_ref, w_ref, wtail_ref, o_ref, u_ref):
    h = _modulated_norm(x_ref[...], sc_ref[...], sh_ref[...]).astype(BF16)
    n_main = w_ref.shape[1]
    for n0 in range(0, n_main, TN_PROJ):
        n1 = min(n0 + TN_PROJ, n_main)
        p = jnp.dot(h, w_ref[:, n0:n1], preferred_element_type=F32)
        o_ref[:, n0:n1] = p.astype(o_ref.dtype)
        if n0 == 0:
            u_ref[...] = p[:, COL_S5:COL_S5 + BRANCH_W]
    o_ref[:, n_main:] = jnp.dot(h, wtail_ref[...], preferred_element_type=F32).astype(o_ref.dtype)


def _in_proj(x2, mod3, w_main, w_tail, layer, seq):
    t, d = x2.shape
    tm = TM_INPROJ
    tpb = seq // tm
    n_main = w_main.shape[2]
    assert COL_S5 + BRANCH_W <= TN_PROJ and n_main + w_tail.shape[2] == IN_W_PAD
    return pl.pallas_call(
        _inproj_kernel,
        grid=(t // tm,),
        in_specs=[pl.BlockSpec((tm, d), lambda i: (i, 0)),
                  pl.BlockSpec((None, 1, d), lambda i: ((i // tpb) * 6 + 1, 0, 0)),
                  pl.BlockSpec((None, 1, d), lambda i: ((i // tpb) * 6 + 0, 0, 0)),
                  pl.BlockSpec((None, d, n_main), lambda i: (layer, 0, 0), pipeline_mode=pl.Buffered(1)),
                  pl.BlockSpec((None, d, IN_W_PAD - n_main), lambda i: (layer, 0, 0), pipeline_mode=pl.Buffered(1))],
        out_specs=[pl.BlockSpec((tm, IN_W_PAD), lambda i: (i, 0)),
                   pl.BlockSpec((tm, BRANCH_W), lambda i: (i, 0))],
        out_shape=[jax.ShapeDtypeStruct((t, IN_W_PAD), BF16), jax.ShapeDtypeStruct((t, BRANCH_W), F32)],
        compiler_params=_cparams(("parallel",)),
        name="in_proj",
    )(x2, mod3, mod3, w_main, w_tail)


def _rope_kernel(pos_ref, invf_ref, ecos_ref, esin_ref, cos_ref, sin_ref):
    ang = invf_ref[:, 0:1] * pos_ref[...].astype(F32)
    def spread(values, e_ref):
        hi = values.astype(BF16)
        rest = values - hi.astype(F32)
        mid = rest.astype(BF16)
        lo = (rest - mid.astype(F32)).astype(BF16)
        e = e_ref[...]
        return _dot_tn(hi, e) + _dot_tn(mid, e) + _dot_tn(lo, e)

    cos_ref[...] = spread(jnp.cos(ang), ecos_ref)
    sin_ref[...] = spread(jnp.sin(ang), esin_ref)


def _rope_tables(pos_row, invf_col, expand_cos, expand_sin):
    t = pos_row.shape[1]
    half, w = expand_cos.shape
    tm = 1024
    const = lambda shape: pl.BlockSpec(shape, lambda i: (0, 0))
    return pl.pallas_call(
        _rope_kernel,
        grid=(t // tm,),
        in_specs=[pl.BlockSpec((1, tm), lambda i: (0, i)), const((half, LANE)), const((half, w)), const((half, w))],
        out_specs=[pl.BlockSpec((tm, w), lambda i: (i, 0))] * 2,
        out_shape=[jax.ShapeDtypeStruct((t, w), F32)] * 2,
        compiler_params=_cparams(("parallel",)),
        name="rope_tables",
    )(pos_row, invf_col, expand_cos, expand_sin)


def _ret_kernel(q_ref, k_ref, v_ref, g_ref, cos_ref, sin_ref, o_ref, st_ref, dec_ref, *, chunk):
    @pl.when(pl.program_id(1) == 0)
    def _():
        st_ref[...] = jnp.zeros_like(st_ref)
        ti = lax.broadcasted_iota(jnp.int32, (chunk, chunk), 0)
        si = lax.broadcasted_iota(jnp.int32, (chunk, chunk), 1)
        lag = (ti - si).astype(F32)
        for h in range(RET_HEADS):
            log_gamma = math.log1p(-(2.0 ** (-5.0 - h)))
            dec_ref[h] = jnp.where(ti >= si, jnp.exp(jnp.minimum(lag * log_gamma, 0.0)), 0.0)

    cosf = cos_ref[...]
    sinf = sin_ref[...]
    width = RET_HEADS * RET_DK
    lane = lax.broadcasted_iota(jnp.int32, (chunk, width), 1)
    first_half = (lane % RET_DK) < (RET_DK // 2)

    def rope(t):
        partner = jnp.where(first_half, pltpu.roll(t, width - RET_DK // 2, 1), pltpu.roll(t, RET_DK // 2, 1))
        return t * cosf + partner * sinf

    q = rope(q_ref[...].astype(F32))
    k = rope(k_ref[...].astype(F32)) * (RET_DK ** -0.5)
    v = v_ref[...]
    g = g_ref[...].astype(F32)
    tcol = lax.broadcasted_iota(jnp.int32, (chunk, 1), 0).astype(F32)
    for h in range(RET_HEADS):
        log_gamma = math.log1p(-(2.0 ** (-5.0 - h)))
        qh = q[:, h * RET_DK:(h + 1) * RET_DK]
        kh = k[:, h * RET_DK:(h + 1) * RET_DK]
        vh = v[:, h * RET_DV:(h + 1) * RET_DV].astype(BF16)
        scores = _dot_nt(qh.astype(BF16), kh.astype(BF16)) * dec_ref[h]
        state = st_ref[h]
        q_in = qh * jnp.exp(log_gamma * (tcol + 1.0))
        o = (jnp.dot(scores.astype(BF16), vh, preferred_element_type=F32)
             + jnp.dot(q_in.astype(BF16), state.astype(BF16), preferred_element_type=F32))
        k_out = kh * jnp.exp(log_gamma * (chunk - 1.0 - tcol))
        st_ref[h] = math.exp(log_gamma * chunk) * state + _dot_tn(k_out.astype(BF16), vh)
        o = o * lax.rsqrt(jnp.mean(o * o, axis=-1, keepdims=True) + EPS)
        gh = g[:, h * RET_DV:(h + 1) * RET_DV]
        o_ref[:, h * RET_DV:(h + 1) * RET_DV] = (o * _silu(gh)).astype(o_ref.dtype)


def _retention(p3, cos3, sin3):
    b, seq, _ = p3.shape
    c = C_RET
    qk_w = RET_HEADS * RET_DK
    return pl.pallas_call(
        functools.partial(_ret_kernel, chunk=c),
        grid=(b, seq // c),
        in_specs=[pl.BlockSpec((None, c, qk_w), lambda i, j: (i, j, COL_RQ // qk_w)),
                  pl.BlockSpec((None, c, qk_w), lambda i, j: (i, j, COL_RK // qk_w)),
                  pl.BlockSpec((None, c, BRANCH_W), lambda i, j: (i, j, COL_RV // BRANCH_W)),
                  pl.BlockSpec((None, c, BRANCH_W), lambda i, j: (i, j, COL_RG // BRANCH_W)),
                  pl.BlockSpec((None, c, qk_w), lambda i, j: (i, j, 0)),
                  pl.BlockSpec((None, c, qk_w), lambda i, j: (i, j, 0))],
        out_specs=pl.BlockSpec((None, c, BRANCH_W), lambda i, j: (i, j, 0)),
        out_shape=jax.ShapeDtypeStruct((b, seq, BRANCH_W), BF16),
        scratch_shapes=[pltpu.VMEM((RET_HEADS, RET_DK, RET_DV), F32), pltpu.VMEM((RET_HEADS, c, c), F32)],
        compiler_params=_cparams(("parallel", "arbitrary")),
        name="retention",
    )(p3, p3, p3, p3, cos3, sin3)


def _ssd_kernel(z_ref, xs_ref, bc_ref, dt_ref, tri_ref, cw_ref, cb_ref, dtb_ref, alog_ref, dsk_ref, nw_ref,
                o_ref, xe_scr, st_ref, *, chunk):
    j = pl.program_id(1)
    width = 2 * BRANCH_W

    @pl.when(j == 0)
    def _():
        st_ref[...] = jnp.zeros_like(st_ref)
        xe_scr[0:8, :] = jnp.zeros((8, width), F32)

    @pl.when(j > 0)
    def _():
        xe_scr[0:8, :] = xe_scr[chunk:chunk + 8, :]

    xe_scr[8:, 0:BRANCH_W] = xs_ref[...].astype(F32)
    xe_scr[8:, BRANCH_W:] = bc_ref[...].astype(F32)
    conv = cb_ref[...] + cw_ref[M2_CONV - 1:M2_CONV, :] * xe_scr[8:, :]
    for tap in range(M2_CONV - 1):
        conv = conv + cw_ref[tap:tap + 1, :] * xe_scr[pl.ds(8 - (M2_CONV - 1) + tap, chunk), :]
    conv = _silu(conv)
    xs = conv[:, :BRANCH_W]
    bm = conv[:, BRANCH_W:BRANCH_W + M2_GROUPS * M2_STATE]
    cm = conv[:, BRANCH_W + M2_GROUPS * M2_STATE:]

    dt = jax.nn.softplus(dt_ref[...].astype(F32) + dtb_ref[...])
    da = dt * (-jnp.exp(alog_ref[...]))
    da_hi = da.astype(BF16)
    da_r = da - da_hi.astype(F32)
    da_mid = da_r.astype(BF16)
    da_lo = (da_r - da_mid.astype(F32)).astype(BF16)
    tri = tri_ref[...]
    a_cs = (jnp.dot(tri, da_hi, preferred_element_type=F32) + jnp.dot(tri, da_mid, preferred_element_type=F32)
            + jnp.dot(tri, da_lo, preferred_element_type=F32))
    a_cs = a_cs * LOG2_E
    a_cs_t = a_cs.T
    ti = lax.broadcasted_iota(jnp.int32, (chunk, chunk), 0)
    si = lax.broadcasted_iota(jnp.int32, (chunk, chunk), 1)
    causal = ti >= si
    hpg = M2_HEADS // M2_GROUPS
    pair_w = 2 * M2_HEADDIM
    upper = lax.broadcasted_iota(jnp.int32, (chunk, pair_w), 1) >= M2_HEADDIM
    upper_state = lax.broadcasted_iota(jnp.int32, (M2_STATE, pair_w), 1) >= M2_HEADDIM
    ys = []
    for grp in range(M2_GROUPS):
        bm_g = bm[:, grp * M2_STATE:(grp + 1) * M2_STATE]
        cm_g = cm[:, grp * M2_STATE:(grp + 1) * M2_STATE]
        bm_bf = bm_g.astype(BF16)
        cb = _dot_nt(cm_g.astype(BF16), bm_bf)
        for pp in range(hpg // 2):
            h0 = grp * hpg + 2 * pp
            pair = h0 // 2
            xd = xs[:, pair * pair_w:(pair + 1) * pair_w] * jnp.where(upper, dt[:, h0 + 1:h0 + 2], dt[:, h0:h0 + 1])
            xd_bf = xd.astype(BF16)
            state = st_ref[pair]
            state_bf = state.astype(BF16)
            y_heads, state_heads = [], []
            for h in (h0, h0 + 1):
                col = a_cs[:, h:h + 1]
                row = a_cs_t[h:h + 1, :]
                lmat = jnp.where(causal, jnp.exp2(col - row), 0.0)
                y_heads.append(jnp.dot((cb * lmat).astype(BF16), xd_bf, preferred_element_type=F32)
                               + jnp.dot((cm_g * jnp.exp2(col)).astype(BF16), state_bf, preferred_element_type=F32))
                a_last = a_cs[chunk - 1:chunk, h:h + 1]
                to_end = jnp.exp2(a_last - col)
                state_heads.append(jnp.exp2(a_last) * state + _dot_tn(bm_bf, (xd * to_end).astype(BF16)))
            ys.append(jnp.where(upper, y_heads[1], y_heads[0]))
            st_ref[pair] = jnp.where(upper_state, state_heads[1], state_heads[0])
    y = jnp.concatenate(ys, axis=-1) + dsk_ref[...] * xs
    y = y * _silu(z_ref[...].astype(F32))
    o_ref[...] = (y * lax.rsqrt(jnp.mean(y * y, axis=-1, keepdims=True) + EPS) * nw_ref[...]).astype(o_ref.dtype)


def _ssd(p3, tri, conv_w, conv_b, dt_bias_row, a_log_row, d_skip_row, norm_w_row):
    b, seq, _ = p3.shape
    c = C_SSD
    const = lambda shape: pl.BlockSpec(shape, lambda i, j: (0,) * len(shape))
    return pl.pallas_call(
        functools.partial(_ssd_kernel, chunk=c),
        grid=(b, seq // c),
        in_specs=[pl.BlockSpec((None, c, BRANCH_W), lambda i, j: (i, j, COL_MZ // BRANCH_W)),
                  pl.BlockSpec((None, c, BRANCH_W), lambda i, j: (i, j, COL_MXS // BRANCH_W)),
                  pl.BlockSpec((None, c, BRANCH_W), lambda i, j: (i, j, COL_MBC // BRANCH_W)),
                  pl.BlockSpec((None, c, LANE), lambda i, j: (i, j, COL_MDT // LANE)),
                  const((c, c)), const((M2_CONV, 2 * BRANCH_W)), const((1, 2 * BRANCH_W)),
                  const((1, LANE)), const((1, LANE)), const((1, BRANCH_W)), const((1, BRANCH_W))],
        out_specs=pl.BlockSpec((None, c, BRANCH_W), lambda i, j: (i, j, 0)),
        out_shape=jax.ShapeDtypeStruct((b, seq, BRANCH_W), BF16),
        scratch_shapes=[pltpu.VMEM((c + 8, 2 * BRANCH_W), F32),
                        pltpu.VMEM((M2_HEADS // 2, M2_STATE, 2 * M2_HEADDIM), F32)],
        compiler_params=_cparams(("parallel", "arbitrary")),
        name="ssd",
    )(p3, p3, p3, p3, tri, conv_w, conv_b, dt_bias_row, a_log_row, d_skip_row, norm_w_row)


def _hg_tables(chunk):
    n_lev = int(math.log2(chunk))
    r = np.arange(chunk)[:, None]
    jj = np.arange(chunk)[None, :]
    tri = (jj <= r).astype(np.float32)
    x = r ^ jj
    levmap = np.where(r > jj, np.floor(np.log2(x + 0.5)), np.where(r == jj, -1, -2)).astype(np.int32)
    return tri, levmap, n_lev


def _hg_level_exponent(b, lev):
    rows, width = b.shape
    m = 1 << lev
    sub = 8
    if 2 * m >= sub:
        blocks = b.reshape(rows // (2 * m), 2 * m, width)
        mid = jnp.broadcast_to(blocks[:, m - 1:m, :], blocks.shape).reshape(rows, width)
    else:
        groups = b.reshape(rows // sub, sub, width)
        row_in_group = lax.broadcasted_iota(jnp.int32, groups.shape, 1)
        mid = None
        for start in range(0, sub, 2 * m):
            picked = jnp.broadcast_to(groups[:, start + m - 1:start + m, :], groups.shape)
            mid = picked if mid is None else jnp.where(row_in_group >= start, picked, mid)
        mid = mid.reshape(rows, width)
    return -jnp.abs(b - mid)


def _hg_kernel(q_ref, f_ref, i_ref, g_ref, sum_ref, lev_ref, llb_ref, l1m_ref, nw_ref, o_ref, st_ref,
               *, chunk, n_lev):
    @pl.when(pl.program_id(1) == 0)
    def _():
        st_ref[...] = jnp.zeros_like(st_ref)

    f = f_ref[...].astype(F32)
    y = jnp.exp(-jnp.abs(f))
    one_plus_y = 1.0 + y
    log_sig = jnp.minimum(f, 0.0) - jnp.log(one_plus_y)
    a = llb_ref[...]
    bb = l1m_ref[...] + log_sig
    log_f = jnp.maximum(a, bb) + jnp.log(1.0 + jnp.exp(-jnp.abs(a - bb)))
    k_all = jnp.exp(l1m_ref[...]) * (jnp.where(f >= 0.0, y, 1.0) / one_plus_y)
    q_all = _silu(q_ref[...].astype(F32))
    hi = log_f.astype(BF16)
    r1 = log_f - hi.astype(F32)
    mid = r1.astype(BF16)
    lo = (r1 - mid.astype(F32)).astype(BF16)
    tri = sum_ref[...]
    b_all = (jnp.dot(tri, hi, preferred_element_type=F32)
             + jnp.dot(tri, mid, preferred_element_type=F32)
             + jnp.dot(tri, lo, preferred_element_type=F32))
    b_all = b_all * LOG2_E
    to_end_all = b_all[chunk - 1:chunk, :] - b_all
    level_decay = [jnp.exp2(_hg_level_exponent(b_all, lev)) for lev in range(n_lev)]
    levmap = lev_ref[...]
    on_diag = levmap == -1
    on_level = [levmap == lev for lev in range(n_lev)]
    v_all = i_ref[...]
    g_all = g_ref[...].astype(F32)
    for h in range(HG_HEADS):
        sl = slice(h * HG_DK, (h + 1) * HG_DK)
        qh = q_all[:, sl]
        kh = k_all[:, sl]
        vh = v_all[:, sl].astype(BF16)
        b_h = b_all[:, sl]
        to_end = to_end_all[:, sl]
        amat = jnp.where(on_diag, _dot_nt(qh.astype(BF16), kh.astype(BF16)), 0.0)
        for lev in range(n_lev):
            e = level_decay[lev][:, sl]
            a_l = _dot_nt((qh * e).astype(BF16), (kh * e).astype(BF16))
            amat = jnp.where(on_level[lev], a_l, amat)
        state_t = st_ref[h]
        o = (jnp.dot(amat.astype(BF16), vh, preferred_element_type=F32)
             + _dot_nt((qh * jnp.exp2(b_h)).astype(BF16), state_t.astype(BF16)))
        k_end = kh * jnp.exp2(to_end)
        st_ref[h] = jnp.exp2(b_h[chunk - 1:chunk, :]) * state_t + _dot_tn(vh, k_end.astype(BF16))
        o = o * lax.rsqrt(jnp.mean(o * o, axis=-1, keepdims=True) + EPS) * nw_ref[...]
        o_ref[:, sl] = (o * _silu(g_all[:, sl])).astype(o_ref.dtype)


def _hgrn2(p3, log_lb_row, log1m_lb_row, norm_w_row):
    b, seq, _ = p3.shape
    c = C_HG
    tri, levmap, n_lev = _hg_tables(c)
    const = lambda shape: pl.BlockSpec(shape, lambda i, j: (0,) * len(shape))
    blk = lambda col: pl.BlockSpec((None, c, BRANCH_W), lambda i, j: (i, j, col // BRANCH_W))
    return pl.pallas_call(
        functools.partial(_hg_kernel, chunk=c, n_lev=n_lev),
        grid=(b, seq // c),
        in_specs=[blk(COL_HQ), blk(COL_HF), blk(COL_HI), blk(COL_HG),
                  const((c, c)), const((c, c)),
                  const((1, BRANCH_W)), const((1, BRANCH_W)), const((1, HG_DK))],
        out_specs=pl.BlockSpec((None, c, BRANCH_W), lambda i, j: (i, j, 0)),
        out_shape=jax.ShapeDtypeStruct((b, seq, BRANCH_W), BF16),
        scratch_shapes=[pltpu.VMEM((HG_HEADS, HG_DK, HG_DK), F32)],
        compiler_params=_cparams(("parallel", "arbitrary")),
        name="hgrn2",
    )(p3, p3, p3, p3, jnp.asarray(tri, BF16), jnp.asarray(levmap), log_lb_row, log1m_lb_row, norm_w_row)


def _expand_block_diag(comp_ref, e_ref, dst_ref, row_div, lane_div, causal=False):
    gq = LANE // S5_CH
    rows, ncols = dst_ref.shape
    step = 512
    for c0 in range(0, ncols, step):
        r1 = min(rows, c0 + step) if causal else rows
        row_grp = (lax.broadcasted_iota(jnp.int32, (r1, step), 0) // row_div) % gq
        lane_grp = ((lax.broadcasted_iota(jnp.int32, (r1, step), 1) + c0) // lane_div) % gq
        full = jnp.dot(comp_ref[0:r1, :], e_ref[:, c0:c0 + step], preferred_element_type=F32)
        dst_ref[0:r1, c0:c0 + step] = jnp.where(row_grp == lane_grp, full, 0.0).astype(dst_ref.dtype)


def _s5_kernel(u_ref, k2_ref, bc_ref, cc_ref, esc_ref, lam_ref, o_ref, tc_scr, tq_ref, bqt_ref, cq_ref,
               x_scr, w_scr, s_scr, *, rows):
    nb = S5_BLOCK

    @pl.when(pl.program_id(1) == 0)
    def _():
        k2 = k2_ref[...]
        lane = lax.broadcasted_iota(jnp.int32, k2.shape, 1)
        for t in range(nb):
            shifted = k2 if t == 0 else jnp.where(lane >= t * S5_CH, pltpu.roll(k2, t * S5_CH, 1), 0.0)
            tc_scr[t * LANE:(t + 1) * LANE, :] = shifted.astype(tc_scr.dtype)
        _expand_block_diag(tc_scr, esc_ref, tq_ref, S5_CH, S5_CH, causal=True)
        _expand_block_diag(bc_ref, esc_ref, bqt_ref, S5_STATE, S5_CH)
        _expand_block_diag(cc_ref, esc_ref, cq_ref, S5_STATE, S5_CH)

    n_seq = u_ref.shape[0]
    for b in range(n_seq):
        for t in range(nb):
            x_scr[b * rows:(b + 1) * rows, t * LANE:(t + 1) * LANE] = (
                u_ref[b, pl.ds(t, rows, stride=nb), :].astype(x_scr.dtype))
    x = x_scr[...]
    half = w_scr.shape[1] // 2
    w_scr[...] = _dot_nt(x, bqt_ref[...])
    lam_re = lam_ref[0:1, :]
    lam_im = lam_ref[1:2, :]

    def body(j, carry):
        out = []
        for b in range(n_seq):
            s_re, s_im = carry[2 * b], carry[2 * b + 1]
            r = b * rows + j
            s_scr[pl.ds(r, 1), 0:half] = s_re
            s_scr[pl.ds(r, 1), half:] = s_im
            w_re = w_scr[pl.ds(r, 1), 0:half]
            w_im = w_scr[pl.ds(r, 1), half:]
            out += [lam_re * s_re - lam_im * s_im + w_re, lam_re * s_im + lam_im * s_re + w_im]
        return tuple(out)

    zero = jnp.zeros((1, half), F32)
    lax.fori_loop(0, rows, body, (zero,) * (2 * n_seq))
    s_bf = s_scr[...].astype(BF16)
    pair = 2 * LANE
    for c0 in range(0, nb * LANE, pair):
        k_rows = c0 + pair
        y = (jnp.dot(x[:, :k_rows], tq_ref[0:k_rows, c0:c0 + pair], preferred_element_type=F32)
             + jnp.dot(s_bf, cq_ref[:, c0:c0 + pair], preferred_element_type=F32))
        for b in range(n_seq):
            for t in range(c0 // LANE, (c0 + pair) // LANE):
                o_ref[b, pl.ds(t, rows, stride=nb), :] = y[b * rows:(b + 1) * rows, t * LANE - c0:(t + 1) * LANE - c0]


def _s5_scan(p3, k2, bc, cc, lam16):
    batch, seq, _ = p3.shape
    nb = S5_BLOCK
    nq = BRANCH_W // LANE
    rows = seq // nb
    kdim = nb * LANE
    gq = LANE // S5_CH
    ncol = 2 * gq * S5_STATE
    e_sc = (np.eye(nb)[:, None, :, None, None] * np.eye(S5_CH)[None, :, None, None, :] * np.ones((1, 1, 1, gq, 1)))
    e_sc = e_sc.reshape(nb * S5_CH, nb * gq * S5_CH)
    full = lambda shape: pl.BlockSpec(shape, lambda q, b: (0,) * len(shape))
    per_q = lambda r, c: pl.BlockSpec((None, r, c), lambda q, b: (q, 0, 0))
    n_seq = S5_SEQ_PER_STEP if batch % S5_SEQ_PER_STEP == 0 else 1
    return pl.pallas_call(
        functools.partial(_s5_kernel, rows=rows),
        grid=(nq, batch // n_seq),
        in_specs=[pl.BlockSpec((n_seq, seq, LANE), lambda q, b: (b, 0, q)),
                  per_q(LANE, nb * S5_CH), per_q(ncol, nb * S5_CH), per_q(ncol, nb * S5_CH),
                  full(e_sc.shape), per_q(2, ncol // 2)],
        out_specs=pl.BlockSpec((n_seq, seq, LANE), lambda q, b: (b, 0, q)),
        out_shape=jax.ShapeDtypeStruct((batch, seq, BRANCH_W), F32),
        scratch_shapes=[pltpu.VMEM((kdim, nb * S5_CH), BF16),
                        pltpu.VMEM((kdim, kdim), BF16), pltpu.VMEM((ncol, kdim), BF16), pltpu.VMEM((ncol, kdim), BF16),
                        pltpu.VMEM((n_seq * rows, kdim), BF16), pltpu.VMEM((n_seq * rows, ncol), F32),
                        pltpu.VMEM((n_seq * rows, ncol), F32)],
        compiler_params=_cparams(("parallel", "arbitrary")),
        name="s5_scan",
    )(p3, k2, bc, cc, jnp.asarray(e_sc, BF16), lam16)


def _s5_operators(lam_re, lam_im, b_re, b_im, c_re, c_im, d_skip, log_dt):
    nb = S5_BLOCK
    gq = LANE // S5_CH
    nq = S5_GROUPS // gq
    lam = lax.complex(jnp.minimum(lam_re.astype(F32), S5_MAX_REAL), lam_im.astype(F32))
    step = jnp.exp(log_dt.astype(F32))[:, None]
    z = lam * step
    lam_bar = jnp.exp(z)
    b_bar = ((lam_bar - 1.0) / lam)[..., None] * lax.complex(b_re.astype(F32), b_im.astype(F32))
    c_mat = lax.complex(c_re.astype(F32), c_im.astype(F32))
    pw = jnp.exp(z[..., None] * jnp.arange(nb + 1, dtype=F32))
    cp = c_mat[:, None, :, :] * pw[..., :nb].transpose(0, 2, 1)[:, :, None, :]
    cp = jnp.concatenate([cp.real, -cp.imag], axis=-1).reshape(S5_GROUPS, nb * S5_CH, 2 * S5_STATE)
    bri = jnp.concatenate([b_bar.real, b_bar.imag], axis=1)
    kern = jnp.einsum('gnk,gki->gin', cp, bri, precision=HIGHEST)
    skip = (jnp.asarray(np.concatenate([np.eye(S5_CH), np.zeros((S5_CH, (nb - 1) * S5_CH))], axis=1), F32)[None]
            * d_skip.astype(F32).reshape(S5_GROUPS, S5_CH, 1))
    k2 = (kern + skip).reshape(nq, gq * S5_CH, nb * S5_CH)
    pw_rev = jnp.exp(z[..., None] * jnp.asarray(np.arange(nb - 1, -1, -1), F32))
    binc = pw_rev[:, :, :, None] * b_bar[:, :, None, :]
    binc = jnp.stack([binc.real, binc.imag], axis=0).reshape(2, nq, gq * S5_STATE, nb * S5_CH)
    bc = binc.transpose(1, 0, 2, 3).reshape(nq, 2 * gq * S5_STATE, nb * S5_CH)
    cm = c_mat.transpose(0, 2, 1)[:, :, None, :] * pw[..., 1:][:, :, :, None]
    cm = jnp.stack([cm.real, -cm.imag], axis=0).reshape(2, nq, gq * S5_STATE, nb * S5_CH)
    cc = cm.transpose(1, 0, 2, 3).reshape(nq, 2 * gq * S5_STATE, nb * S5_CH)
    lam_n = pw[..., nb].reshape(nq, gq * S5_STATE)
    lam16 = jnp.stack([lam_n.real, lam_n.imag], axis=1)
    return k2, bc.astype(BF16), cc.astype(BF16), lam16


def _merge_kernel(x_ref, sc_ref, sh_ref, gm_ref, ys5_ref, yhg_ref, yret_ref, ym2_ref,
                  wglu_ref, wbr_ref, wg_ref, bg_ref, wout_ref, o_ref):
    x = x_ref[...]
    d = x.shape[1]
    h = _modulated_norm(x, sc_ref[...], sh_ref[...]).astype(BF16)
    y_s5 = jax.nn.gelu(ys5_ref[...])
    y_s5 = y_s5 * jax.nn.sigmoid(jnp.dot(y_s5.astype(BF16), wglu_ref[...], preferred_element_type=F32))
    acc = jnp.zeros(x.shape, F32)
    for n, y in enumerate((y_s5, yhg_ref[...], yret_ref[...], ym2_ref[...])):
        gate = jax.nn.sigmoid(jnp.dot(h, wg_ref[:, n * d:(n + 1) * d], preferred_element_type=F32)
                              + bg_ref[:, n * d:(n + 1) * d])
        acc = acc + gate * jnp.dot(y.astype(BF16), wbr_ref[n], preferred_element_type=F32)
    o_ref[...] = x + gm_ref[...] * jnp.dot(acc.astype(BF16), wout_ref[...], preferred_element_type=F32)


def _merge(x2, mod3, ys5, yhg, yret, ym2, w_glu, w_branch, w_gate, b_gate, w_out, layer, seq):
    t, d = x2.shape
    tm = TM_PROJ
    tpb = seq // tm
    const = lambda shape: pl.BlockSpec((None,) + shape, lambda i: (layer,) + (0,) * len(shape),
                                       pipeline_mode=pl.Buffered(1))
    modspec = lambda k: pl.BlockSpec((None, 1, d), lambda i: ((i // tpb) * 6 + k, 0, 0))
    yspec = pl.BlockSpec((tm, BRANCH_W), lambda i: (i, 0))
    return pl.pallas_call(
        _merge_kernel,
        grid=(t // tm,),
        in_specs=[pl.BlockSpec((tm, d), lambda i: (i, 0)), modspec(1), modspec(0), modspec(2),
                  yspec, yspec, yspec, yspec,
                  const((BRANCH_W, BRANCH_W)), const((4, BRANCH_W, d)), const((d, 4 * d)), const((1, 4 * d)),
                  const((d, d))],
        out_specs=pl.BlockSpec((tm, d), lambda i: (i, 0)),
        out_shape=jax.ShapeDtypeStruct((t, d), F32),
        compiler_params=_cparams(("parallel",)),
        name="merge",
    )(x2, mod3, mod3, mod3, ys5, yhg, yret, ym2, w_glu, w_branch, w_gate, b_gate, w_out)


def _router_kernel(x_ref, sc_ref, sh_ref, wr_ref, br_ref, tri_ref, h_ref, ids_ref, wts_ref, cnt_ref, carry):
    i = pl.program_id(0)

    @pl.when(i == 0)
    def _():
        carry[...] = jnp.zeros_like(carry)

    h = _modulated_norm(x_ref[...], sc_ref[...], sh_ref[...])
    tm, d = h.shape
    packed = _pack_bf16_pairs(h)
    for k in range(N_SLAB):
        h_ref[k] = packed[:, k * SLAB:(k + 1) * SLAB]
    h_hi = h.astype(BF16)
    h_lo = (h - h_hi.astype(F32)).astype(BF16)
    w_r = wr_ref[...]
    w_hi = w_r.astype(BF16)
    w_lo = (w_r - w_hi.astype(F32)).astype(BF16)
    logits = _dot_nt(w_hi, h_hi) + _dot_nt(w_hi, h_lo) + _dot_nt(w_lo, h_hi) + br_ref[:, 0:1]
    gl = [logits[g:g + 1, :] for g in range(MOE_GROUPS)]
    gmax = gl[0]
    gsel = jnp.zeros((1, tm), jnp.int32)
    for g in range(1, MOE_GROUPS):
        better = gl[g] > gmax
        gsel = jnp.where(better, g, gsel)
        gmax = jnp.where(better, gl[g], gmax)
    gden = gl[0] * 0.0
    for g in range(MOE_GROUPS):
        gden = gden + jnp.exp(gl[g] - gmax)
    g_w = 1.0 / gden
    el = []
    for e in range(MOE_EPG):
        v = logits[MOE_GROUPS + e:MOE_GROUPS + e + 1, :]
        for g in range(1, MOE_GROUPS):
            row = MOE_GROUPS + g * MOE_EPG + e
            v = jnp.where(gsel == g, logits[row:row + 1, :], v)
        el.append(v)
    v1 = el[0]
    i1 = jnp.zeros((1, tm), jnp.int32)
    for e in range(1, MOE_EPG):
        better = el[e] > v1
        i1 = jnp.where(better, e, i1)
        v1 = jnp.where(better, el[e], v1)
    v2 = jnp.full((1, tm), -jnp.inf, F32)
    i2 = jnp.zeros((1, tm), jnp.int32)
    for e in range(MOE_EPG):
        better = (el[e] > v2) & (i1 != e)
        i2 = jnp.where(better, e, i2)
        v2 = jnp.where(better, el[e], v2)
    ex = jnp.exp(v2 - v1)
    p1 = 1.0 / (1.0 + ex)
    e1 = gsel * MOE_EPG + i1
    e2 = gsel * MOE_EPG + i2
    erow = lax.broadcasted_iota(jnp.int32, (MOE_EXPERTS, tm), 0)
    oh1 = (erow == e1).astype(F32)
    oh2 = (erow == e2).astype(F32)
    both = oh1 + oh2
    n_grp = tm // LANE
    stacked = jnp.concatenate([both[:, g * LANE:(g + 1) * LANE] for g in range(n_grp)], axis=0)
    within = jnp.dot(stacked.astype(BF16), tri_ref[...], preferred_element_type=F32)
    grp_count = jnp.sum(stacked, axis=1, keepdims=True)
    running = carry[:, 0:1]
    pieces = []
    for g in range(n_grp):
        pieces.append(within[g * MOE_EXPERTS:(g + 1) * MOE_EXPERTS, :] + running)
        running = running + grp_count[g * MOE_EXPERTS:(g + 1) * MOE_EXPERTS, :]
    prefix = jnp.concatenate(pieces, axis=1)
    rank1 = jnp.sum(oh1 * prefix, axis=0, keepdims=True).astype(jnp.int32)
    rank2 = jnp.sum(oh2 * prefix, axis=0, keepdims=True).astype(jnp.int32)
    carry[...] = jnp.broadcast_to(running, carry.shape)
    zi = jnp.zeros((1, tm), jnp.int32)
    ids_ref[...] = jnp.concatenate([e1, e2, rank1, rank2, zi, zi, zi, zi], axis=0)
    wrow = lax.broadcasted_iota(jnp.int32, (LANE, tm), 0)
    wts_ref[...] = jnp.where(wrow == 0, p1 * g_w, jnp.where(wrow == 1, ex * p1 * g_w, 0.0)).T
    cnt_ref[...] = carry[...]


def _router(x2, mod3, w_route, b_route, tri_excl, seq):
    t, d = x2.shape
    tm = TM_PROJ
    tpb = seq // tm
    nr = w_route.shape[0]
    const = lambda shape: pl.BlockSpec(shape, lambda i: (0,) * len(shape))
    modspec = lambda k: pl.BlockSpec((None, 1, d), lambda i: ((i // tpb) * 6 + k, 0, 0))
    return pl.pallas_call(
        _router_kernel,
        grid=(t // tm,),
        in_specs=[pl.BlockSpec((tm, d), lambda i: (i, 0)), modspec(4), modspec(3),
                  const((nr, d)), const((nr, LANE)), const((LANE, LANE))],
        out_specs=[pl.BlockSpec((N_SLAB, tm, SLAB), lambda i: (0, i, 0)),
                   pl.BlockSpec((8, tm), lambda i: (0, i)),
                   pl.BlockSpec((tm, LANE), lambda i: (i, 0)),
                   const((MOE_EXPERTS, LANE))],
        out_shape=[jax.ShapeDtypeStruct((N_SLAB, t, SLAB), jnp.uint32),
                   jax.ShapeDtypeStruct((8, t), jnp.int32),
                   jax.ShapeDtypeStruct((t, LANE), F32),
                   jax.ShapeDtypeStruct((MOE_EXPERTS, LANE), F32)],
        scratch_shapes=[pltpu.VMEM((MOE_EXPERTS, LANE), F32)],
        compiler_params=_cparams(("arbitrary",)),
        name="moe_router",
    )(x2, mod3, mod3, w_route, b_route, tri_excl)


def _sc_mesh():
    return plsc.VectorSubcoreMesh(core_axis_name="core", subcore_axis_name="subcore")


def _slab_rows(idx, n_rows):
    return (idx[None, :] + (jnp.arange(N_SLAB, dtype=jnp.int32) * n_rows)[:, None]).reshape(-1)


def _dispatch(slot1, slot2, h_slabs):
    n_slab, t, d = h_slabs.shape
    n_out = 2 * t
    xs = _scatter_rows(h_slabs.reshape(n_slab * t, d), _slab_rows(slot1, n_out), _slab_rows(slot2, n_out),
                       n_slab * n_out)
    return xs.reshape(n_slab, n_out, d)


def _scatter_rows(src, idx1, idx2, n_out):
    t, d = src.shape
    win = SC_WINDOW

    @pl.kernel(out_type=jax.ShapeDtypeStruct((n_out, d), src.dtype), mesh=_sc_mesh(), name="moe_dispatch_sc")
    def scatter_rows(x_hbm, i1_hbm, i2_hbm, o_hbm):
        def body(x_vmem, i1_vmem, i2_vmem):
            pltpu.sync_copy(x_vmem, o_hbm.at[i1_vmem.at[0]])
            pltpu.sync_copy(x_vmem, o_hbm.at[i2_vmem.at[0]])

        pltpu.emit_pipeline(
            body,
            grid=(t // win,),
            in_specs=[pl.BlockSpec((win, d), lambda i: (i, 0)),
                      pl.BlockSpec((1, win), lambda i: (0, i)),
                      pl.BlockSpec((1, win), lambda i: (0, i))],
            out_specs=[],
            core_axis_name=("core", "subcore"),
            dimension_semantics=(pltpu.PARALLEL,),
        )(x_hbm, i1_hbm, i2_hbm)

    return scatter_rows(src, idx1.reshape(1, t), idx2.reshape(1, t))


def _gather_rows(src, idx):
    m = idx.shape[0]
    d = src.shape[1]
    win = SC_WINDOW

    @pl.kernel(out_type=jax.ShapeDtypeStruct((m, d), src.dtype), mesh=_sc_mesh(), name="moe_gather_sc")
    def gather(x_hbm, i_hbm, o_hbm):
        def body(i_vmem, o_vmem):
            pltpu.sync_copy(x_hbm.at[i_vmem.at[0]], o_vmem)

        pltpu.emit_pipeline(
            body,
            grid=(m // win,),
            in_specs=[pl.BlockSpec((1, win), lambda i: (0, i))],
            out_specs=[pl.BlockSpec((win, d), lambda i: (i, 0))],
            core_axis_name=("core", "subcore"),
            dimension_semantics=(pltpu.PARALLEL,),
        )(i_hbm, o_hbm)

    return gather(src, idx.reshape(1, m))


def _expert_kernel(tile_ref, exp_ref, lo_ref, hi_ref, xs_ref, w1_ref, w3_ref, w2_ref, ys_ref, w1_scr, w3_scr, w2_scr):
    s = pl.program_id(0)
    prev = jnp.maximum(s - 1, 0)
    new_expert = (s == 0) | (exp_ref[s] != exp_ref[prev])
    new_tile = (s == 0) | (tile_ref[s] != tile_ref[prev])

    @pl.when(new_expert)
    def _():
        w1_scr[...] = w1_ref[...].astype(BF16)
        w3_scr[...] = w3_ref[...].astype(BF16)
        w2_scr[...] = w2_ref[...].astype(BF16)

    lo = lo_ref[s]
    hi = hi_ref[s]
    n_rows = xs_ref.shape[1]

    @pl.when(new_tile)
    def _():
        ys_ref[...] = jnp.zeros_like(ys_ref)

    def run_rows(r0, n):
        x = _unpack_bf16_pairs(jnp.concatenate([xs_ref[k, r0:r0 + n, :] for k in range(N_SLAB)], axis=-1)).astype(BF16)
        a = jnp.dot(x, w1_scr[...], preferred_element_type=F32)
        b = jnp.dot(x, w3_scr[...], preferred_element_type=F32)
        act = _silu(a) * b
        y = _pack_bf16_pairs(jnp.dot(act.astype(BF16), w2_scr[...], preferred_element_type=F32))
        row = lax.broadcasted_iota(jnp.int32, (n, SLAB), 0) + r0
        mine = (row >= lo) & (row < hi)
        for k in range(N_SLAB):
            ys_ref[k, r0:r0 + n, :] = jnp.where(mine, y[:, k * SLAB:(k + 1) * SLAB], ys_ref[k, r0:r0 + n, :])

    groups = (hi + (X_SUB - 1)) // X_SUB - lo // X_SUB
    whole = groups > 2

    @pl.when(whole)
    def _():
        run_rows(0, n_rows)

    for r0 in range(0, n_rows, X_SUB):
        @pl.when(jnp.logical_not(whole) & (lo < r0 + X_SUB) & (hi > r0))
        def _(r0=r0):
            run_rows(r0, X_SUB)


def _experts(step_tile, step_expert, step_lo, step_hi, xs, w1, w3, w2, layer):
    n_slab, ns, slab = xs.shape
    d = w1.shape[1]
    ff = w1.shape[2]
    n_steps = step_tile.shape[0]
    base = layer * MOE_EXPERTS
    grid_spec = pltpu.PrefetchScalarGridSpec(
        num_scalar_prefetch=4,
        grid=(n_steps,),
        in_specs=[pl.BlockSpec((n_slab, TM_X, slab), lambda s, tl, ex, lo, hi: (0, tl[s], 0)),
                  pl.BlockSpec((None, d, ff), lambda s, tl, ex, lo, hi: (base + ex[s], 0, 0)),
                  pl.BlockSpec((None, d, ff), lambda s, tl, ex, lo, hi: (base + ex[s], 0, 0)),
                  pl.BlockSpec((None, ff, d), lambda s, tl, ex, lo, hi: (base + ex[s], 0, 0))],
        out_specs=pl.BlockSpec((n_slab, TM_X, slab), lambda s, tl, ex, lo, hi: (0, tl[s], 0)),
        scratch_shapes=[pltpu.VMEM((d, ff), BF16), pltpu.VMEM((d, ff), BF16), pltpu.VMEM((ff, d), BF16)],
    )
    return pl.pallas_call(
        _expert_kernel,
        grid_spec=grid_spec,
        out_shape=jax.ShapeDtypeStruct((n_slab, ns, slab), xs.dtype),
        compiler_params=_cparams(("arbitrary",)),
        name="moe_experts",
    )(step_tile, step_expert, step_lo, step_hi, xs, w1, w3, w2)


def _combine_kernel(x_ref, gate_ref, fw_ref, wcol_ref, y1_ref, y2_ref, o_ref, *, final):
    w_first = wcol_ref[:, 0:1]
    w_second = wcol_ref[:, 1:2]
    y_first = _unpack_bf16_pairs(jnp.concatenate([y1_ref[k] for k in range(N_SLAB)], axis=-1))
    y_second = _unpack_bf16_pairs(jnp.concatenate([y2_ref[k] for k in range(N_SLAB)], axis=-1))
    moe = w_first * y_first + w_second * y_second
    x = x_ref[...] + gate_ref[...] * moe
    if final:
        x = x * lax.rsqrt(jnp.mean(x * x, axis=-1, keepdims=True) + EPS) * fw_ref[...]
    o_ref[...] = x


def _combine(x2, mod3, final_w_row, wcol, gathered, seq, final):
    t, d = x2.shape
    tm = TM_COMB
    tpb = seq // tm
    nblk = t // tm
    yspec = lambda off: pl.BlockSpec((N_SLAB, tm, SLAB), lambda i: (0, i + off, 0))
    return pl.pallas_call(
        functools.partial(_combine_kernel, final=final),
        grid=(nblk,),
        in_specs=[pl.BlockSpec((tm, d), lambda i: (i, 0)),
                  pl.BlockSpec((None, 1, d), lambda i: ((i // tpb) * 6 + 5, 0, 0)),
                  pl.BlockSpec((1, d), lambda i: (0, 0)),
                  pl.BlockSpec((tm, LANE), lambda i: (i, 0)),
                  yspec(0), yspec(nblk)],
        out_specs=pl.BlockSpec((tm, d), lambda i: (i, 0)),
        out_shape=jax.ShapeDtypeStruct((t, d), F32),
        compiler_params=_cparams(("parallel",)),
        name="moe_combine",
    )(x2, mod3, final_w_row, wcol, gathered, gathered)


def _moe(x2, mod3, final_w_row, w_route, b_route, tri_excl, w1, w3, w2, layer, seq, final):
    t, d = x2.shape
    h3, ids, wcol, counts = _router(x2, mod3, w_route, b_route, tri_excl, seq)
    cnt = counts[:, 0].astype(jnp.int32)
    ends = jnp.cumsum(cnt)
    offs = ends - cnt
    experts = jnp.arange(MOE_EXPERTS, dtype=jnp.int32)
    pick = lambda table, idx: jnp.sum(jnp.where(idx[:, None] == experts[None, :], table[None, :], 0), axis=1)
    slot1 = pick(offs, ids[0]) + ids[2]
    slot2 = pick(offs, ids[1]) + ids[3]
    n_tiles = 2 * t // TM_X
    first_tile = offs // TM_X
    n_vis = jnp.where(cnt > 0, (ends - 1) // TM_X - first_tile + 1, 0)
    cum = jnp.cumsum(n_vis)
    step = jnp.arange(n_tiles + MOE_EXPERTS, dtype=jnp.int32)
    step_expert = jnp.minimum(jnp.sum(step[:, None] >= cum[None, :], axis=1), MOE_EXPERTS - 1).astype(jnp.int32)
    valid = step < cum[-1]
    step_tile = jnp.where(valid, pick(first_tile - (cum - n_vis), step_expert) + step, n_tiles - 1)
    step_lo = jnp.where(valid, jnp.clip(pick(offs, step_expert) - step_tile * TM_X, 0, TM_X), 0)
    step_hi = jnp.where(valid, jnp.clip(pick(ends, step_expert) - step_tile * TM_X, 0, TM_X), 0)
    xs = _dispatch(slot1, slot2, h3)
    ys = _experts(step_tile.astype(jnp.int32), step_expert, step_lo.astype(jnp.int32), step_hi.astype(jnp.int32),
                  xs, w1, w3, w2, layer)
    n_sorted = ys.shape[1]
    gathered = _gather_rows(ys.reshape(N_SLAB * n_sorted, SLAB), _slab_rows(jnp.concatenate([slot1, slot2]), n_sorted))
    gathered = gathered.reshape(N_SLAB, n_sorted, SLAB)
    return _combine(x2, mod3, final_w_row, wcol, gathered, seq, final)


def kernel(x, c, positions, ada_w, ada_b, w_in, s5_lam_re, s5_lam_im, s5_b_re, s5_b_im, s5_c_re, s5_c_im, s5_d, s5_log_dt, s5_w_glu, hg_lb_logits, hg_norm_w, m2_conv_w, m2_conv_b, m2_dt_bias, m2_a_log, m2_d, m2_norm_w, w_branch, w_gate, b_gate, w_out, moe_w_group, moe_b_group, moe_w_expert, moe_b_expert, moe_w1, moe_w3, moe_w2, final_norm_w):
    bsz, seq, d = x.shape
    t = bsz * seq
    depth = ada_w.shape[0]
    assert seq % TM_PROJ == 0 and seq % C_RET == 0 and seq % C_SSD == 0 and seq % C_HG == 0
    x2 = x.reshape(t, d).astype(F32)

    c_pad = jnp.zeros((8, d), F32).at[:bsz].set(c.astype(F32))
    mod_all = _ada_mod(c_pad.T, ada_w.astype(F32), ada_b.astype(F32), bsz)

    half = RET_DK // 2
    inv_freq = ROPE_BASE ** (-jnp.arange(half, dtype=F32) / half)
    invf_col = jnp.broadcast_to(inv_freq[:, None], (half, LANE))
    expand = np.tile(np.eye(half, dtype=np.float32), (1, 2 * RET_HEADS))
    sign = np.tile(np.concatenate([-np.ones(half), np.ones(half)]), RET_HEADS)[None, :].astype(np.float32)
    cos_t, sin_t = _rope_tables(positions.reshape(1, t).astype(jnp.int32), invf_col,
                                jnp.asarray(expand, BF16), jnp.asarray(expand * sign, BF16))
    cos3 = cos_t.reshape(bsz, seq, -1)
    sin3 = sin_t.reshape(bsz, seq, -1)

    lb_cum = jnp.cumsum(jax.nn.softmax(hg_lb_logits.astype(F32), axis=0), axis=0)
    hg_lb = lb_cum - lb_cum[:1]
    tri_ssd = jnp.asarray(np.tril(np.ones((C_SSD, C_SSD), np.float32)), BF16)
    tri_excl = jnp.asarray(np.triu(np.ones((LANE, LANE), np.float32), 1), BF16)
    final_w_row = final_norm_w.astype(F32)[None, :]
    n_main = IN_W // LANE * LANE
    w_main = w_in[:, :, :n_main].astype(BF16)
    w_tail = jnp.zeros((depth, d, IN_W_PAD - n_main), BF16).at[:, :, :IN_W - n_main].set(w_in[:, :, n_main:].astype(BF16))
    w_glu_bf = s5_w_glu.astype(BF16)
    w_branch_bf = w_branch.astype(BF16)
    w_gate_bf = w_gate.astype(BF16)
    w_out_bf = w_out.astype(BF16)
    b_gate3 = b_gate.astype(F32).reshape(depth, 1, -1)
    moe_w1_all = moe_w1.astype(F32).reshape(depth * MOE_EXPERTS, d, MOE_FF)
    moe_w3_all = moe_w3.astype(F32).reshape(depth * MOE_EXPERTS, d, MOE_FF)
    moe_w2_all = moe_w2.astype(F32).reshape(depth * MOE_EXPERTS, MOE_FF, d)

    for layer in range(depth):
        mod3 = mod_all[layer, :bsz].reshape(bsz * 6, 1, d)
        p, u_s5 = _in_proj(x2, mod3, w_main, w_tail, layer, seq)
        p3 = p.reshape(bsz, seq, IN_W_PAD)

        ops = _s5_operators(s5_lam_re[layer], s5_lam_im[layer], s5_b_re[layer], s5_b_im[layer],
                            s5_c_re[layer], s5_c_im[layer], s5_d[layer], s5_log_dt[layer])
        y_s5 = _s5_scan(u_s5.reshape(bsz, seq, BRANCH_W), *ops).reshape(t, BRANCH_W)

        lb = hg_lb[layer][None, :]
        y_hg = _hgrn2(p3, jnp.log(lb), jnp.log1p(-lb), hg_norm_w[layer].astype(F32)[None, :]).reshape(t, BRANCH_W)

        y_ret = _retention(p3, cos3, sin3).reshape(t, BRANCH_W)

        pad8 = lambda v: jnp.zeros((1, LANE), F32).at[0, :M2_HEADS].set(v.astype(F32))
        y_m2 = _ssd(p3, tri_ssd, m2_conv_w[layer].astype(F32), m2_conv_b[layer].astype(F32)[None, :],
                    pad8(m2_dt_bias[layer]), pad8(m2_a_log[layer]),
                    jnp.repeat(m2_d[layer].astype(F32), M2_HEADDIM)[None, :],
                    m2_norm_w[layer].astype(F32)[None, :]).reshape(t, BRANCH_W)

        nr = 40
        w_route = jnp.zeros((nr, d), F32).at[:MOE_GROUPS].set(moe_w_group[layer].astype(F32).T)
        w_route = w_route.at[MOE_GROUPS:MOE_GROUPS + MOE_EXPERTS].set(moe_w_expert[layer].astype(F32).T)
        b_route = jnp.zeros((nr, LANE), F32).at[:MOE_GROUPS, 0].set(moe_b_group[layer].astype(F32))
        b_route = b_route.at[MOE_GROUPS:MOE_GROUPS + MOE_EXPERTS, 0].set(moe_b_expert[layer].astype(F32))
        x2 = _merge(x2, mod3, y_s5, y_hg, y_ret, y_m2, w_glu_bf, w_branch_bf, w_gate_bf, b_gate3, w_out_bf,
                    layer, seq)
        x2 = _moe(x2, mod3, final_w_row, w_route, b_route, tri_excl, moe_w1_all, moe_w3_all, moe_w2_all,
                  layer, seq, final=(layer == depth - 1))
    return x2.reshape(bsz, seq, d)
```

```python
import functools
import math

import numpy as np
import jax
import jax.numpy as jnp
from jax import lax
from jax.experimental import pallas as pl
from jax.experimental.pallas import tpu as pltpu
from jax.experimental.pallas import tpu_sc as plsc

F32 = jnp.float32
BF16 = jnp.bfloat16
HIGHEST = lax.Precision.HIGHEST

D_MODEL = 1024
BRANCH_W = 512
EPS = 1e-6
S5_GROUPS = 32
S5_CH = 16
S5_STATE = 64
S5_MAX_REAL = -1e-4
S5_BLOCK = 16
S5_SEQ_PER_STEP = 2
HG_HEADS = 4
HG_DK = 128
RET_HEADS = 4
RET_DK = 64
RET_DV = 128
ROPE_BASE = 10000.0
M2_HEADS = 8
M2_HEADDIM = 64
M2_GROUPS = 2
M2_STATE = 128
M2_CONV = 4
MOE_GROUPS = 4
MOE_EPG = 8
MOE_EXPERTS = MOE_GROUPS * MOE_EPG
MOE_FF = 256

COL_S5, COL_HQ, COL_HF, COL_HI, COL_HG = 0, 512, 1024, 1536, 2048
COL_RQ, COL_RK, COL_RV, COL_RG = 2560, 2816, 3072, 3584
COL_MZ, COL_MXS, COL_MBC, COL_MDT = 4096, 4608, 5120, 5632
IN_W = 5640
IN_W_PAD = 5760

LANE = 128
VMEM_LIMIT = 56 * 1024 * 1024

TM_PROJ = 1024
TN_PROJ = 1024
TM_INPROJ = 1024
LOG2_E = 1.4426950408889634
C_RET = 512
C_SSD = 256
C_HG = 128
TM_X = 512
X_SUB = 128
TM_COMB = 1024
SC_WINDOW = 128
SLAB = 256
N_SLAB = D_MODEL // 2 // SLAB


def _cparams(sem):
    return pltpu.CompilerParams(dimension_semantics=sem, vmem_limit_bytes=VMEM_LIMIT)


def _silu(v):
    return v * jax.nn.sigmoid(v)


def _dot_nt(a, b, **kw):
    return lax.dot_general(a, b, (((1,), (1,)), ((), ())), preferred_element_type=F32, **kw)


def _dot_tn(a, b, **kw):
    return lax.dot_general(a, b, (((0,), (0,)), ((), ())), preferred_element_type=F32, **kw)


def _ada_kernel(ct_ref, w_ref, b_ref, o_ref, *, n_rows):
    cond_t = _silu(ct_ref[...])
    w = w_ref[...]
    rows = [jnp.sum(w * cond_t[:, b:b + 1], axis=0, keepdims=True) for b in range(n_rows)]
    rows += [jnp.zeros_like(rows[0])] * (cond_t.shape[1] - n_rows)
    o_ref[...] = jnp.concatenate(rows, axis=0) + b_ref[...]


def _ada_mod(c_pad_t, ada_w, ada_b, n_rows):
    depth, d, n = ada_w.shape
    tn = 1536
    return pl.pallas_call(
        functools.partial(_ada_kernel, n_rows=n_rows),
        grid=(depth, n // tn),
        in_specs=[pl.BlockSpec((d, 8), lambda l, j: (0, 0)),
                  pl.BlockSpec((None, d, tn), lambda l, j: (l, 0, j)),
                  pl.BlockSpec((None, 1, tn), lambda l, j: (l, 0, j))],
        out_specs=pl.BlockSpec((None, 8, tn), lambda l, j: (l, 0, j)),
        out_shape=jax.ShapeDtypeStruct((depth, 8, n), F32),
        compiler_params=_cparams(("parallel", "parallel")),
        name="ada_mod",
    )(c_pad_t, ada_w, ada_b.reshape(depth, 1, n))


def _pack_bf16_pairs(x):
    n = x.shape[1] // 2
    lo = pltpu.bitcast(x[:, :n].astype(BF16).astype(F32), jnp.uint32) >> 16
    hi = pltpu.bitcast(x[:, n:].astype(BF16).astype(F32), jnp.uint32)
    return hi | lo


def _unpack_bf16_pairs(w):
    lo = pltpu.bitcast(w << 16, F32)
    hi = pltpu.bitcast(w & jnp.uint32(0xFFFF0000), F32)
    return jnp.concatenate([lo, hi], axis=-1)


def _modulated_norm(x, scale, shift):
    ms = jnp.mean(x * x, axis=-1, keepdims=True)
    return x * lax.rsqrt(ms + EPS) * (1.0 + scale) + shift


def _inproj_kernel(x_ref, sc_ref, sh_ref, w_ref, wtail_ref, o_ref, u_ref):
    h = _modulated_norm(x_ref[...], sc_ref[...], sh_ref[...]).astype(BF16)
    n_main = w_ref.shape[1]
    for n0 in range(0, n_main, TN_PROJ):
        n1 = min(n0 + TN_PROJ, n_main)
        p = jnp.dot(h, w_ref[:, n0:n1], preferred_element_type=F32)
        o_ref[:, n0:n1] = p.astype(o_ref.dtype)
        if n0 == 0:
            u_ref[...] = p[:, COL_S5:COL_S5 + BRANCH_W]
    o_ref[:, n_main:] = jnp.dot(h, wtail_ref[...], preferred_element_type=F32).astype(o_ref.dtype)


def _in_proj(x2, mod3, w_main, w_tail, layer, seq):
    t, d = x2.shape
    tm = TM_INPROJ
    tpb = seq // tm
    n_main = w_main.shape[2]
    assert COL_S5 + BRANCH_W <= TN_PROJ and n_main + w_tail.shape[2] == IN_W_PAD
    return pl.pallas_call(
        _inproj_kernel,
        grid=(t // tm,),
        in_specs=[pl.BlockSpec((tm, d), lambda i: (i, 0)),
                  pl.BlockSpec((None, 1, d), lambda i: ((i // tpb) * 6 + 1, 0, 0)),
                  pl.BlockSpec((None, 1, d), lambda i: ((i // tpb) * 6 + 0, 0, 0)),
                  pl.BlockSpec((None, d, n_main), lambda i: (layer, 0, 0), pipeline_mode=pl.Buffered(1)),
                  pl.BlockSpec((None, d, IN_W_PAD - n_main), lambda i: (layer, 0, 0), pipeline_mode=pl.Buffered(1))],
        out_specs=[pl.BlockSpec((tm, IN_W_PAD), lambda i: (i, 0)),
                   pl.BlockSpec((tm, BRANCH_W), lambda i: (i, 0))],
        out_shape=[jax.ShapeDtypeStruct((t, IN_W_PAD), BF16), jax.ShapeDtypeStruct((t, BRANCH_W), F32)],
        compiler_params=_cparams(("parallel",)),
        name="in_proj",
    )(x2, mod3, mod3, w_main, w_tail)


def _rope_kernel(pos_ref, invf_ref, ecos_ref, esin_ref, cos_ref, sin_ref):
    ang = invf_ref[:, 0:1] * pos_ref[...].astype(F32)
    def spread(values, e_ref):
        hi = values.astype(BF16)
        rest = values - hi.astype(F32)
        mid = rest.astype(BF16)
        lo = (rest - mid.astype(F32)).astype(BF16)
        e = e_ref[...]
        return _dot_tn(hi, e) + _dot_tn(mid, e) + _dot_tn(lo, e)

    cos_ref[...] = spread(jnp.cos(ang), ecos_ref)
    sin_ref[...] = spread(jnp.sin(ang), esin_ref)


def _rope_tables(pos_row, invf_col, expand_cos, expand_sin):
    t = pos_row.shape[1]
    half, w = expand_cos.shape
    tm = 1024
    const = lambda shape: pl.BlockSpec(shape, lambda i: (0, 0))
    return pl.pallas_call(
        _rope_kernel,
        grid=(t // tm,),
        in_specs=[pl.BlockSpec((1, tm), lambda i: (0, i)), const((half, LANE)), const((half, w)), const((half, w))],
        out_specs=[pl.BlockSpec((tm, w), lambda i: (i, 0))] * 2,
        out_shape=[jax.ShapeDtypeStruct((t, w), F32)] * 2,
        compiler_params=_cparams(("parallel",)),
        name="rope_tables",
    )(pos_row, invf_col, expand_cos, expand_sin)


def _ret_kernel(q_ref, k_ref, v_ref, g_ref, cos_ref, sin_ref, o_ref, st_ref, dec_ref, *, chunk):
    @pl.when(pl.program_id(1) == 0)
    def _():
        st_ref[...] = jnp.zeros_like(st_ref)
        ti = lax.broadcasted_iota(jnp.int32, (chunk, chunk), 0)
        si = lax.broadcasted_iota(jnp.int32, (chunk, chunk), 1)
        lag = (ti - si).astype(F32)
        for h in range(RET_HEADS):
            log_gamma = math.log1p(-(2.0 ** (-5.0 - h)))
            dec_ref[h] = jnp.where(ti >= si, jnp.exp(jnp.minimum(lag * log_gamma, 0.0)), 0.0)

    cosf = cos_ref[...]
    sinf = sin_ref[...]
    width = RET_HEADS * RET_DK
    lane = lax.broadcasted_iota(jnp.int32, (chunk, width), 1)
    first_half = (lane % RET_DK) < (RET_DK // 2)

    def rope(t):
        partner = jnp.where(first_half, pltpu.roll(t, width - RET_DK // 2, 1), pltpu.roll(t, RET_DK // 2, 1))
        return t * cosf + partner * sinf

    q = rope(q_ref[...].astype(F32))
    k = rope(k_ref[...].astype(F32)) * (RET_DK ** -0.5)
    v = v_ref[...]
    g = g_ref[...].astype(F32)
    tcol = lax.broadcasted_iota(jnp.int32, (chunk, 1), 0).astype(F32)
    for h in range(RET_HEADS):
        log_gamma = math.log1p(-(2.0 ** (-5.0 - h)))
        qh = q[:, h * RET_DK:(h + 1) * RET_DK]
        kh = k[:, h * RET_DK:(h + 1) * RET_DK]
        vh = v[:, h * RET_DV:(h + 1) * RET_DV].astype(BF16)
        scores = _dot_nt(qh.astype(BF16), kh.astype(BF16)) * dec_ref[h]
        state = st_ref[h]
        q_in = qh * jnp.exp(log_gamma * (tcol + 1.0))
        o = (jnp.dot(scores.astype(BF16), vh, preferred_element_type=F32)
             + jnp.dot(q_in.astype(BF16), state.astype(BF16), preferred_element_type=F32))
        k_out = kh * jnp.exp(log_gamma * (chunk - 1.0 - tcol))
        st_ref[h] = math.exp(log_gamma * chunk) * state + _dot_tn(k_out.astype(BF16), vh)
        o = o * lax.rsqrt(jnp.mean(o * o, axis=-1, keepdims=True) + EPS)
        gh = g[:, h * RET_DV:(h + 1) * RET_DV]
        o_ref[:, h * RET_DV:(h + 1) * RET_DV] = (o * _silu(gh)).astype(o_ref.dtype)


def _retention(p3, cos3, sin3):
    b, seq, _ = p3.shape
    c = C_RET
    qk_w = RET_HEADS * RET_DK
    return pl.pallas_call(
        functools.partial(_ret_kernel, chunk=c),
        grid=(b, seq // c),
        in_specs=[pl.BlockSpec((None, c, qk_w), lambda i, j: (i, j, COL_RQ // qk_w)),
                  pl.BlockSpec((None, c, qk_w), lambda i, j: (i, j, COL_RK // qk_w)),
                  pl.BlockSpec((None, c, BRANCH_W), lambda i, j: (i, j, COL_RV // BRANCH_W)),
                  pl.BlockSpec((None, c, BRANCH_W), lambda i, j: (i, j, COL_RG // BRANCH_W)),
                  pl.BlockSpec((None, c, qk_w), lambda i, j: (i, j, 0)),
                  pl.BlockSpec((None, c, qk_w), lambda i, j: (i, j, 0))],
        out_specs=pl.BlockSpec((None, c, BRANCH_W), lambda i, j: (i, j, 0)),
        out_shape=jax.ShapeDtypeStruct((b, seq, BRANCH_W), BF16),
        scratch_shapes=[pltpu.VMEM((RET_HEADS, RET_DK, RET_DV), F32), pltpu.VMEM((RET_HEADS, c, c), F32)],
        compiler_params=_cparams(("parallel", "arbitrary")),
        name="retention",
    )(p3, p3, p3, p3, cos3, sin3)


def _ssd_kernel(z_ref, xs_ref, bc_ref, dt_ref, tri_ref, cw_ref, cb_ref, dtb_ref, alog_ref, dsk_ref, nw_ref,
                o_ref, xe_scr, st_ref, *, chunk):
    j = pl.program_id(1)
    width = 2 * BRANCH_W

    @pl.when(j == 0)
    def _():
        st_ref[...] = jnp.zeros_like(st_ref)
        xe_scr[0:8, :] = jnp.zeros((8, width), F32)

    @pl.when(j > 0)
    def _():
        xe_scr[0:8, :] = xe_scr[chunk:chunk + 8, :]

    xe_scr[8:, 0:BRANCH_W] = xs_ref[...].astype(F32)
    xe_scr[8:, BRANCH_W:] = bc_ref[...].astype(F32)
    conv = cb_ref[...] + cw_ref[M2_CONV - 1:M2_CONV, :] * xe_scr[8:, :]
    for tap in range(M2_CONV - 1):
        conv = conv + cw_ref[tap:tap + 1, :] * xe_scr[pl.ds(8 - (M2_CONV - 1) + tap, chunk), :]
    conv = _silu(conv)
    xs = conv[:, :BRANCH_W]
    bm = conv[:, BRANCH_W:BRANCH_W + M2_GROUPS * M2_STATE]
    cm = conv[:, BRANCH_W + M2_GROUPS * M2_STATE:]

    dt = jax.nn.softplus(dt_ref[...].astype(F32) + dtb_ref[...])
    da = dt * (-jnp.exp(alog_ref[...]))
    da_hi = da.astype(BF16)
    da_r = da - da_hi.astype(F32)
    da_mid = da_r.astype(BF16)
    da_lo = (da_r - da_mid.astype(F32)).astype(BF16)
    tri = tri_ref[...]
    a_cs = (jnp.dot(tri, da_hi, preferred_element_type=F32) + jnp.dot(tri, da_mid, preferred_element_type=F32)
            + jnp.dot(tri, da_lo, preferred_element_type=F32))
    a_cs = a_cs * LOG2_E
    a_cs_t = a_cs.T
    ti = lax.broadcasted_iota(jnp.int32, (chunk, chunk), 0)
    si = lax.broadcasted_iota(jnp.int32, (chunk, chunk), 1)
    causal = ti >= si
    hpg = M2_HEADS // M2_GROUPS
    pair_w = 2 * M2_HEADDIM
    upper = lax.broadcasted_iota(jnp.int32, (chunk, pair_w), 1) >= M2_HEADDIM
    upper_state = lax.broadcasted_iota(jnp.int32, (M2_STATE, pair_w), 1) >= M2_HEADDIM
    ys = []
    for grp in range(M2_GROUPS):
        bm_g = bm[:, grp * M2_STATE:(grp + 1) * M2_STATE]
        cm_g = cm[:, grp * M2_STATE:(grp + 1) * M2_STATE]
        bm_bf = bm_g.astype(BF16)
        cb = _dot_nt(cm_g.astype(BF16), bm_bf)
        for pp in range(hpg // 2):
            h0 = grp * hpg + 2 * pp
            pair = h0 // 2
            xd = xs[:, pair * pair_w:(pair + 1) * pair_w] * jnp.where(upper, dt[:, h0 + 1:h0 + 2], dt[:, h0:h0 + 1])
            xd_bf = xd.astype(BF16)
            state = st_ref[pair]
            state_bf = state.astype(BF16)
            y_heads, state_heads = [], []
            for h in (h0, h0 + 1):
                col = a_cs[:, h:h + 1]
                row = a_cs_t[h:h + 1, :]
                lmat = jnp.where(causal, jnp.exp2(col - row), 0.0)
                y_heads.append(jnp.dot((cb * lmat).astype(BF16), xd_bf, preferred_element_type=F32)
                               + jnp.dot((cm_g * jnp.exp2(col)).astype(BF16), state_bf, preferred_element_type=F32))
                a_last = a_cs[chunk - 1:chunk, h:h + 1]
                to_end = jnp.exp2(a_last - col)
                state_heads.append(jnp.exp2(a_last) * state + _dot_tn(bm_bf, (xd * to_end).astype(BF16)))
            ys.append(jnp.where(upper, y_heads[1], y_heads[0]))
            st_ref[pair] = jnp.where(upper_state, state_heads[1], state_heads[0])
    y = jnp.concatenate(ys, axis=-1) + dsk_ref[...] * xs
    y = y * _silu(z_ref[...].astype(F32))
    o_ref[...] = (y * lax.rsqrt(jnp.mean(y * y, axis=-1, keepdims=True) + EPS) * nw_ref[...]).astype(o_ref.dtype)


def _ssd(p3, tri, conv_w, conv_b, dt_bias_row, a_log_row, d_skip_row, norm_w_row):
    b, seq, _ = p3.shape
    c = C_SSD
    const = lambda shape: pl.BlockSpec(shape, lambda i, j: (0,) * len(shape))
    return pl.pallas_call(
        functools.partial(_ssd_kernel, chunk=c),
        grid=(b, seq // c),
        in_specs=[pl.BlockSpec((None, c, BRANCH_W), lambda i, j: (i, j, COL_MZ // BRANCH_W)),
                  pl.BlockSpec((None, c, BRANCH_W), lambda i, j: (i, j, COL_MXS // BRANCH_W)),
                  pl.BlockSpec((None, c, BRANCH_W), lambda i, j: (i, j, COL_MBC // BRANCH_W)),
                  pl.BlockSpec((None, c, LANE), lambda i, j: (i, j, COL_MDT // LANE)),
                  const((c, c)), const((M2_CONV, 2 * BRANCH_W)), const((1, 2 * BRANCH_W)),
                  const((1, LANE)), const((1, LANE)), const((1, BRANCH_W)), const((1, BRANCH_W))],
        out_specs=pl.BlockSpec((None, c, BRANCH_W), lambda i, j: (i, j, 0)),
        out_shape=jax.ShapeDtypeStruct((b, seq, BRANCH_W), BF16),
        scratch_shapes=[pltpu.VMEM((c + 8, 2 * BRANCH_W), F32),
                        pltpu.VMEM((M2_HEADS // 2, M2_STATE, 2 * M2_HEADDIM), F32)],
        compiler_params=_cparams(("parallel", "arbitrary")),
        name="ssd",
    )(p3, p3, p3, p3, tri, conv_w, conv_b, dt_bias_row, a_log_row, d_skip_row, norm_w_row)


def _hg_tables(chunk):
    n_lev = int(math.log2(chunk))
    r = np.arange(chunk)[:, None]
    jj = np.arange(chunk)[None, :]
    tri = (jj <= r).astype(np.float32)
    x = r ^ jj
    levmap = np.where(r > jj, np.floor(np.log2(x + 0.5)), np.where(r == jj, -1, -2)).astype(np.int32)
    return tri, levmap, n_lev


def _hg_level_exponent(b, lev):
    rows, width = b.shape
    m = 1 << lev
    sub = 8
    if 2 * m >= sub:
        blocks = b.reshape(rows // (2 * m), 2 * m, width)
        mid = jnp.broadcast_to(blocks[:, m - 1:m, :], blocks.shape).reshape(rows, width)
    else:
        groups = b.reshape(rows // sub, sub, width)
        row_in_group = lax.broadcasted_iota(jnp.int32, groups.shape, 1)
        mid = None
        for start in range(0, sub, 2 * m):
            picked = jnp.broadcast_to(groups[:, start + m - 1:start + m, :], groups.shape)
            mid = picked if mid is None else jnp.where(row_in_group >= start, picked, mid)
        mid = mid.reshape(rows, width)
    return -jnp.abs(b - mid)


def _hg_kernel(q_ref, f_ref, i_ref, g_ref, sum_ref, lev_ref, llb_ref, l1m_ref, nw_ref, o_ref, st_ref,
               *, chunk, n_lev):
    @pl.when(pl.program_id(1) == 0)
    def _():
        st_ref[...] = jnp.zeros_like(st_ref)

    f = f_ref[...].astype(F32)
    y = jnp.exp(-jnp.abs(f))
    one_plus_y = 1.0 + y
    log_sig = jnp.minimum(f, 0.0) - jnp.log(one_plus_y)
    a = llb_ref[...]
    bb = l1m_ref[...] + log_sig
    log_f = jnp.maximum(a, bb) + jnp.log(1.0 + jnp.exp(-jnp.abs(a - bb)))
    k_all = jnp.exp(l1m_ref[...]) * (jnp.where(f >= 0.0, y, 1.0) / one_plus_y)
    q_all = _silu(q_ref[...].astype(F32))
    hi = log_f.astype(BF16)
    r1 = log_f - hi.astype(F32)
    mid = r1.astype(BF16)
    lo = (r1 - mid.astype(F32)).astype(BF16)
    tri = sum_ref[...]
    b_all = (jnp.dot(tri, hi, preferred_element_type=F32)
             + jnp.dot(tri, mid, preferred_element_type=F32)
             + jnp.dot(tri, lo, preferred_element_type=F32))
    b_all = b_all * LOG2_E
    to_end_all = b_all[chunk - 1:chunk, :] - b_all
    level_decay = [jnp.exp2(_hg_level_exponent(b_all, lev)) for lev in range(n_lev)]
    levmap = lev_ref[...]
    on_diag = levmap == -1
    on_level = [levmap == lev for lev in range(n_lev)]
    v_all = i_ref[...]
    g_all = g_ref[...].astype(F32)
    for h in range(HG_HEADS):
        sl = slice(h * HG_DK, (h + 1) * HG_DK)
        qh = q_all[:, sl]
        kh = k_all[:, sl]
        vh = v_all[:, sl].astype(BF16)
        b_h = b_all[:, sl]
        to_end = to_end_all[:, sl]
        amat = jnp.where(on_diag, _dot_nt(qh.astype(BF16), kh.astype(BF16)), 0.0)
        for lev in range(n_lev):
            e = level_decay[lev][:, sl]
            a_l = _dot_nt((qh * e).astype(BF16), (kh * e).astype(BF16))
            amat = jnp.where(on_level[lev], a_l, amat)
        state_t = st_ref[h]
        o = (jnp.dot(amat.astype(BF16), vh, preferred_element_type=F32)
             + _dot_nt((qh * jnp.exp2(b_h)).astype(BF16), state_t.astype(BF16)))
        k_end = kh * jnp.exp2(to_end)
        st_ref[h] = jnp.exp2(b_h[chunk - 1:chunk, :]) * state_t + _dot_tn(vh, k_end.astype(BF16))
        o = o * lax.rsqrt(jnp.mean(o * o, axis=-1, keepdims=True) + EPS) * nw_ref[...]
        o_ref[:, sl] = (o * _silu(g_all[:, sl])).astype(o_ref.dtype)


def _hgrn2(p3, log_lb_row, log1m_lb_row, norm_w_row):
    b, seq, _ = p3.shape
    c = C_HG
    tri, levmap, n_lev = _hg_tables(c)
    const = lambda shape: pl.BlockSpec(shape, lambda i, j: (0,) * len(shape))
    blk = lambda col: pl.BlockSpec((None, c, BRANCH_W), lambda i, j: (i, j, col // BRANCH_W))
    return pl.pallas_call(
        functools.partial(_hg_kernel, chunk=c, n_lev=n_lev),
        grid=(b, seq // c),
        in_specs=[blk(COL_HQ), blk(COL_HF), blk(COL_HI), blk(COL_HG),
                  const((c, c)), const((c, c)),
                  const((1, BRANCH_W)), const((1, BRANCH_W)), const((1, HG_DK))],
        out_specs=pl.BlockSpec((None, c, BRANCH_W), lambda i, j: (i, j, 0)),
        out_shape=jax.ShapeDtypeStruct((b, seq, BRANCH_W), BF16),
        scratch_shapes=[pltpu.VMEM((HG_HEADS, HG_DK, HG_DK), F32)],
        compiler_params=_cparams(("parallel", "arbitrary")),
        name="hgrn2",
    )(p3, p3, p3, p3, jnp.asarray(tri, BF16), jnp.asarray(levmap), log_lb_row, log1m_lb_row, norm_w_row)


def _expand_block_diag(comp_ref, e_ref, dst_ref, row_div, lane_div, causal=False):
    gq = LANE // S5_CH
    rows, ncols = dst_ref.shape
    step = 512
    for c0 in range(0, ncols, step):
        r1 = min(rows, c0 + step) if causal else rows
        row_grp = (lax.broadcasted_iota(jnp.int32, (r1, step), 0) // row_div) % gq
        lane_grp = ((lax.broadcasted_iota(jnp.int32, (r1, step), 1) + c0) // lane_div) % gq
        full = jnp.dot(comp_ref[0:r1, :], e_ref[:, c0:c0 + step], preferred_element_type=F32)
        dst_ref[0:r1, c0:c0 + step] = jnp.where(row_grp == lane_grp, full, 0.0).astype(dst_ref.dtype)


def _s5_kernel(u_ref, k2_ref, bc_ref, cc_ref, esc_ref, lam_ref, o_ref, tc_scr, tq_ref, bqt_ref, cq_ref,
               x_scr, w_scr, s_scr, *, rows):
    nb = S5_BLOCK

    @pl.when(pl.program_id(1) == 0)
    def _():
        k2 = k2_ref[...]
        lane = lax.broadcasted_iota(jnp.int32, k2.shape, 1)
        for t in range(nb):
            shifted = k2 if t == 0 else jnp.where(lane >= t * S5_CH, pltpu.roll(k2, t * S5_CH, 1), 0.0)
            tc_scr[t * LANE:(t + 1) * LANE, :] = shifted.astype(tc_scr.dtype)
        _expand_block_diag(tc_scr, esc_ref, tq_ref, S5_CH, S5_CH, causal=True)
        _expand_block_diag(bc_ref, esc_ref, bqt_ref, S5_STATE, S5_CH)
        _expand_block_diag(cc_ref, esc_ref, cq_ref, S5_STATE, S5_CH)

    n_seq = u_ref.shape[0]
    for b in range(n_seq):
        for t in range(nb):
            x_scr[b * rows:(b + 1) * rows, t * LANE:(t + 1) * LANE] = (
                u_ref[b, pl.ds(t, rows, stride=nb), :].astype(x_scr.dtype))
    x = x_scr[...]
    half = w_scr.shape[1] // 2
    w_scr[...] = _dot_nt(x, bqt_ref[...])
    lam_re = lam_ref[0:1, :]
    lam_im = lam_ref[1:2, :]

    def body(j, carry):
        out = []
        for b in range(n_seq):
            s_re, s_im = carry[2 * b], carry[2 * b + 1]
            r = b * rows + j
            s_scr[pl.ds(r, 1), 0:half] = s_re
            s_scr[pl.ds(r, 1), half:] = s_im
            w_re = w_scr[pl.ds(r, 1), 0:half]
            w_im = w_scr[pl.ds(r, 1), half:]
            out += [lam_re * s_re - lam_im * s_im + w_re, lam_re * s_im + lam_im * s_re + w_im]
        return tuple(out)

    zero = jnp.zeros((1, half), F32)
    lax.fori_loop(0, rows, body, (zero,) * (2 * n_seq))
    s_bf = s_scr[...].astype(BF16)
    pair = 2 * LANE
    for c0 in range(0, nb * LANE, pair):
        k_rows = c0 + pair
        y = (jnp.dot(x[:, :k_rows], tq_ref[0:k_rows, c0:c0 + pair], preferred_element_type=F32)
             + jnp.dot(s_bf, cq_ref[:, c0:c0 + pair], preferred_element_type=F32))
        for b in range(n_seq):
            for t in range(c0 // LANE, (c0 + pair) // LANE):
                o_ref[b, pl.ds(t, rows, stride=nb), :] = y[b * rows:(b + 1) * rows, t * LANE - c0:(t + 1) * LANE - c0]


def _s5_scan(p3, k2, bc, cc, lam16):
    batch, seq, _ = p3.shape
    nb = S5_BLOCK
    nq = BRANCH_W // LANE
    rows = seq // nb
    kdim = nb * LANE
    gq = LANE // S5_CH
    ncol = 2 * gq * S5_STATE
    e_sc = (np.eye(nb)[:, None, :, None, None] * np.eye(S5_CH)[None, :, None, None, :] * np.ones((1, 1, 1, gq, 1)))
    e_sc = e_sc.reshape(nb * S5_CH, nb * gq * S5_CH)
    full = lambda shape: pl.BlockSpec(shape, lambda q, b: (0,) * len(shape))
    per_q = lambda r, c: pl.BlockSpec((None, r, c), lambda q, b: (q, 0, 0))
    n_seq = S5_SEQ_PER_STEP if batch % S5_SEQ_PER_STEP == 0 else 1
    return pl.pallas_call(
        functools.partial(_s5_kernel, rows=rows),
        grid=(nq, batch // n_seq),
        in_specs=[pl.BlockSpec((n_seq, seq, LANE), lambda q, b: (b, 0, q)),
                  per_q(LANE, nb * S5_CH), per_q(ncol, nb * S5_CH), per_q(ncol, nb * S5_CH),
                  full(e_sc.shape), per_q(2, ncol // 2)],
        out_specs=pl.BlockSpec((n_seq, seq, LANE), lambda q, b: (b, 0, q)),
        out_shape=jax.ShapeDtypeStruct((batch, seq, BRANCH_W), F32),
        scratch_shapes=[pltpu.VMEM((kdim, nb * S5_CH), BF16),
                        pltpu.VMEM((kdim, kdim), BF16), pltpu.VMEM((ncol, kdim), BF16), pltpu.VMEM((ncol, kdim), BF16),
                        pltpu.VMEM((n_seq * rows, kdim), BF16), pltpu.VMEM((n_seq * rows, ncol), F32),
                        pltpu.VMEM((n_seq * rows, ncol), F32)],
        compiler_params=_cparams(("parallel", "arbitrary")),
        name="s5_scan",
    )(p3, k2, bc, cc, jnp.asarray(e_sc, BF16), lam16)


def _s5_operators(lam_re, lam_im, b_re, b_im, c_re, c_im, d_skip, log_dt):
    nb = S5_BLOCK
    gq = LANE // S5_CH
    nq = S5_GROUPS // gq
    lam = lax.complex(jnp.minimum(lam_re.astype(F32), S5_MAX_REAL), lam_im.astype(F32))
    step = jnp.exp(log_dt.astype(F32))[:, None]
    z = lam * step
    lam_bar = jnp.exp(z)
    b_bar = ((lam_bar - 1.0) / lam)[..., None] * lax.complex(b_re.astype(F32), b_im.astype(F32))
    c_mat = lax.complex(c_re.astype(F32), c_im.astype(F32))
    pw = jnp.exp(z[..., None] * jnp.arange(nb + 1, dtype=F32))
    cp = c_mat[:, None, :, :] * pw[..., :nb].transpose(0, 2, 1)[:, :, None, :]
    cp = jnp.concatenate([cp.real, -cp.imag], axis=-1).reshape(S5_GROUPS, nb * S5_CH, 2 * S5_STATE)
    bri = jnp.concatenate([b_bar.real, b_bar.imag], axis=1)
    kern = jnp.einsum('gnk,gki->gin', cp, bri, precision=HIGHEST)
    skip = (jnp.asarray(np.concatenate([np.eye(S5_CH), np.zeros((S5_CH, (nb - 1) * S5_CH))], axis=1), F32)[None]
            * d_skip.astype(F32).reshape(S5_GROUPS, S5_CH, 1))
    k2 = (kern + skip).reshape(nq, gq * S5_CH, nb * S5_CH)
    pw_rev = jnp.exp(z[..., None] * jnp.asarray(np.arange(nb - 1, -1, -1), F32))
    binc = pw_rev[:, :, :, None] * b_bar[:, :, None, :]
    binc = jnp.stack([binc.real, binc.imag], axis=0).reshape(2, nq, gq * S5_STATE, nb * S5_CH)
    bc = binc.transpose(1, 0, 2, 3).reshape(nq, 2 * gq * S5_STATE, nb * S5_CH)
    cm = c_mat.transpose(0, 2, 1)[:, :, None, :] * pw[..., 1:][:, :, :, None]
    cm = jnp.stack([cm.real, -cm.imag], axis=0).reshape(2, nq, gq * S5_STATE, nb * S5_CH)
    cc = cm.transpose(1, 0, 2, 3).reshape(nq, 2 * gq * S5_STATE, nb * S5_CH)
    lam_n = pw[..., nb].reshape(nq, gq * S5_STATE)
    lam16 = jnp.stack([lam_n.real, lam_n.imag], axis=1)
    return k2, bc.astype(BF16), cc.astype(BF16), lam16


def _merge_kernel(x_ref, sc_ref, sh_ref, gm_ref, ys5_ref, yhg_ref, yret_ref, ym2_ref,
                  wglu_ref, wbr_ref, wg_ref, bg_ref, wout_ref, o_ref):
    x = x_ref[...]
    d = x.shape[1]
    h = _modulated_norm(x, sc_ref[...], sh_ref[...]).astype(BF16)
    y_s5 = jax.nn.gelu(ys5_ref[...])
    y_s5 = y_s5 * jax.nn.sigmoid(jnp.dot(y_s5.astype(BF16), wglu_ref[...], preferred_element_type=F32))
    acc = jnp.zeros(x.shape, F32)
    for n, y in enumerate((y_s5, yhg_ref[...], yret_ref[...], ym2_ref[...])):
        gate = jax.nn.sigmoid(jnp.dot(h, wg_ref[:, n * d:(n + 1) * d], preferred_element_type=F32)
                              + bg_ref[:, n * d:(n + 1) * d])
        acc = acc + gate * jnp.dot(y.astype(BF16), wbr_ref[n], preferred_element_type=F32)
    o_ref[...] = x + gm_ref[...] * jnp.dot(acc.astype(BF16), wout_ref[...], preferred_element_type=F32)


def _merge(x2, mod3, ys5, yhg, yret, ym2, w_glu, w_branch, w_gate, b_gate, w_out, layer, seq):
    t, d = x2.shape
    tm = TM_PROJ
    tpb = seq // tm
    const = lambda shape: pl.BlockSpec((None,) + shape, lambda i: (layer,) + (0,) * len(shape),
                                       pipeline_mode=pl.Buffered(1))
    modspec = lambda k: pl.BlockSpec((None, 1, d), lambda i: ((i // tpb) * 6 + k, 0, 0))
    yspec = pl.BlockSpec((tm, BRANCH_W), lambda i: (i, 0))
    return pl.pallas_call(
        _merge_kernel,
        grid=(t // tm,),
        in_specs=[pl.BlockSpec((tm, d), lambda i: (i, 0)), modspec(1), modspec(0), modspec(2),
                  yspec, yspec, yspec, yspec,
                  const((BRANCH_W, BRANCH_W)), const((4, BRANCH_W, d)), const((d, 4 * d)), const((1, 4 * d)),
                  const((d, d))],
        out_specs=pl.BlockSpec((tm, d), lambda i: (i, 0)),
        out_shape=jax.ShapeDtypeStruct((t, d), F32),
        compiler_params=_cparams(("parallel",)),
        name="merge",
    )(x2, mod3, mod3, mod3, ys5, yhg, yret, ym2, w_glu, w_branch, w_gate, b_gate, w_out)


def _router_kernel(x_ref, sc_ref, sh_ref, wr_ref, br_ref, tri_ref, h_ref, ids_ref, wts_ref, cnt_ref, carry):
    i = pl.program_id(0)

    @pl.when(i == 0)
    def _():
        carry[...] = jnp.zeros_like(carry)

    h = _modulated_norm(x_ref[...], sc_ref[...], sh_ref[...])
    tm, d = h.shape
    packed = _pack_bf16_pairs(h)
    for k in range(N_SLAB):
        h_ref[k] = packed[:, k * SLAB:(k + 1) * SLAB]
    h_hi = h.astype(BF16)
    h_lo = (h - h_hi.astype(F32)).astype(BF16)
    w_r = wr_ref[...]
    w_hi = w_r.astype(BF16)
    w_lo = (w_r - w_hi.astype(F32)).astype(BF16)
    logits = _dot_nt(w_hi, h_hi) + _dot_nt(w_hi, h_lo) + _dot_nt(w_lo, h_hi) + br_ref[:, 0:1]
    gl = [logits[g:g + 1, :] for g in range(MOE_GROUPS)]
    gmax = gl[0]
    gsel = jnp.zeros((1, tm), jnp.int32)
    for g in range(1, MOE_GROUPS):
        better = gl[g] > gmax
        gsel = jnp.where(better, g, gsel)
        gmax = jnp.where(better, gl[g], gmax)
    gden = gl[0] * 0.0
    for g in range(MOE_GROUPS):
        gden = gden + jnp.exp(gl[g] - gmax)
    g_w = 1.0 / gden
    el = []
    for e in range(MOE_EPG):
        v = logits[MOE_GROUPS + e:MOE_GROUPS + e + 1, :]
        for g in range(1, MOE_GROUPS):
            row = MOE_GROUPS + g * MOE_EPG + e
            v = jnp.where(gsel == g, logits[row:row + 1, :], v)
        el.append(v)
    v1 = el[0]
    i1 = jnp.zeros((1, tm), jnp.int32)
    for e in range(1, MOE_EPG):
        better = el[e] > v1
        i1 = jnp.where(better, e, i1)
        v1 = jnp.where(better, el[e], v1)
    v2 = jnp.full((1, tm), -jnp.inf, F32)
    i2 = jnp.zeros((1, tm), jnp.int32)
    for e in range(MOE_EPG):
        better = (el[e] > v2) & (i1 != e)
        i2 = jnp.where(better, e, i2)
        v2 = jnp.where(better, el[e], v2)
    ex = jnp.exp(v2 - v1)
    p1 = 1.0 / (1.0 + ex)
    e1 = gsel * MOE_EPG + i1
    e2 = gsel * MOE_EPG + i2
    erow = lax.broadcasted_iota(jnp.int32, (MOE_EXPERTS, tm), 0)
    oh1 = (erow == e1).astype(F32)
    oh2 = (erow == e2).astype(F32)
    both = oh1 + oh2
    n_grp = tm // LANE
    stacked = jnp.concatenate([both[:, g * LANE:(g + 1) * LANE] for g in range(n_grp)], axis=0)
    within = jnp.dot(stacked.astype(BF16), tri_ref[...], preferred_element_type=F32)
    grp_count = jnp.sum(stacked, axis=1, keepdims=True)
    running = carry[:, 0:1]
    pieces = []
    for g in range(n_grp):
        pieces.append(within[g * MOE_EXPERTS:(g + 1) * MOE_EXPERTS, :] + running)
        running = running + grp_count[g * MOE_EXPERTS:(g + 1) * MOE_EXPERTS, :]
    prefix = jnp.concatenate(pieces, axis=1)
    rank1 = jnp.sum(oh1 * prefix, axis=0, keepdims=True).astype(jnp.int32)
    rank2 = jnp.sum(oh2 * prefix, axis=0, keepdims=True).astype(jnp.int32)
    carry[...] = jnp.broadcast_to(running, carry.shape)
    zi = jnp.zeros((1, tm), jnp.int32)
    ids_ref[...] = jnp.concatenate([e1, e2, rank1, rank2, zi, zi, zi, zi], axis=0)
    wrow = lax.broadcasted_iota(jnp.int32, (LANE, tm), 0)
    wts_ref[...] = jnp.where(wrow == 0, p1 * g_w, jnp.where(wrow == 1, ex * p1 * g_w, 0.0)).T
    cnt_ref[...] = carry[...]


def _router(x2, mod3, w_route, b_route, tri_excl, seq):
    t, d = x2.shape
    tm = TM_PROJ
    tpb = seq // tm
    nr = w_route.shape[0]
    const = lambda shape: pl.BlockSpec(shape, lambda i: (0,) * len(shape))
    modspec = lambda k: pl.BlockSpec((None, 1, d), lambda i: ((i // tpb) * 6 + k, 0, 0))
    return pl.pallas_call(
        _router_kernel,
        grid=(t // tm,),
        in_specs=[pl.BlockSpec((tm, d), lambda i: (i, 0)), modspec(4), modspec(3),
                  const((nr, d)), const((nr, LANE)), const((LANE, LANE))],
        out_specs=[pl.BlockSpec((N_SLAB, tm, SLAB), lambda i: (0, i, 0)),
                   pl.BlockSpec((8, tm), lambda i: (0, i)),
                   pl.BlockSpec((tm, LANE), lambda i: (i, 0)),
                   const((MOE_EXPERTS, LANE))],
        out_shape=[jax.ShapeDtypeStruct((N_SLAB, t, SLAB), jnp.uint32),
                   jax.ShapeDtypeStruct((8, t), jnp.int32),
                   jax.ShapeDtypeStruct((t, LANE), F32),
                   jax.ShapeDtypeStruct((MOE_EXPERTS, LANE), F32)],
        scratch_shapes=[pltpu.VMEM((MOE_EXPERTS, LANE), F32)],
        compiler_params=_cparams(("arbitrary",)),
        name="moe_router",
    )(x2, mod3, mod3, w_route, b_route, tri_excl)


def _sc_mesh():
    return plsc.VectorSubcoreMesh(core_axis_name="core", subcore_axis_name="subcore")


def _slab_rows(idx, n_rows):
    return (idx[None, :] + (jnp.arange(N_SLAB, dtype=jnp.int32) * n_rows)[:, None]).reshape(-1)


def _dispatch(slot1, slot2, h_slabs):
    n_slab, t, d = h_slabs.shape
    n_out = 2 * t
    xs = _scatter_rows(h_slabs.reshape(n_slab * t, d), _slab_rows(slot1, n_out), _slab_rows(slot2, n_out),
                       n_slab * n_out)
    return xs.reshape(n_slab, n_out, d)


def _scatter_rows(src, idx1, idx2, n_out):
    t, d = src.shape
    win = SC_WINDOW

    @pl.kernel(out_type=jax.ShapeDtypeStruct((n_out, d), src.dtype), mesh=_sc_mesh(), name="moe_dispatch_sc")
    def scatter_rows(x_hbm, i1_hbm, i2_hbm, o_hbm):
        def body(x_vmem, i1_vmem, i2_vmem):
            pltpu.sync_copy(x_vmem, o_hbm.at[i1_vmem.at[0]])
            pltpu.sync_copy(x_vmem, o_hbm.at[i2_vmem.at[0]])

        pltpu.emit_pipeline(
            body,
            grid=(t // win,),
            in_specs=[pl.BlockSpec((win, d), lambda i: (i, 0)),
                      pl.BlockSpec((1, win), lambda i: (0, i)),
                      pl.BlockSpec((1, win), lambda i: (0, i))],
            out_specs=[],
            core_axis_name=("core", "subcore"),
            dimension_semantics=(pltpu.PARALLEL,),
        )(x_hbm, i1_hbm, i2_hbm)

    return scatter_rows(src, idx1.reshape(1, t), idx2.reshape(1, t))


def _gather_rows(src, idx):
    m = idx.shape[0]
    d = src.shape[1]
    win = SC_WINDOW

    @pl.kernel(out_type=jax.ShapeDtypeStruct((m, d), src.dtype), mesh=_sc_mesh(), name="moe_gather_sc")
    def gather(x_hbm, i_hbm, o_hbm):
        def body(i_vmem, o_vmem):
            pltpu.sync_copy(x_hbm.at[i_vmem.at[0]], o_vmem)

        pltpu.emit_pipeline(
            body,
            grid=(m // win,),
            in_specs=[pl.BlockSpec((1, win), lambda i: (0, i))],
            out_specs=[pl.BlockSpec((win, d), lambda i: (i, 0))],
            core_axis_name=("core", "subcore"),
            dimension_semantics=(pltpu.PARALLEL,),
        )(i_hbm, o_hbm)

    return gather(src, idx.reshape(1, m))


def _expert_kernel(tile_ref, exp_ref, lo_ref, hi_ref, xs_ref, w1_ref, w3_ref, w2_ref, ys_ref, w1_scr, w3_scr, w2_scr):
    s = pl.program_id(0)
    prev = jnp.maximum(s - 1, 0)
    new_expert = (s == 0) | (exp_ref[s] != exp_ref[prev])
    new_tile = (s == 0) | (tile_ref[s] != tile_ref[prev])

    @pl.when(new_expert)
    def _():
        w1_scr[...] = w1_ref[...].astype(BF16)
        w3_scr[...] = w3_ref[...].astype(BF16)
        w2_scr[...] = w2_ref[...].astype(BF16)

    lo = lo_ref[s]
    hi = hi_ref[s]
    n_rows = xs_ref.shape[1]

    @pl.when(new_tile)
    def _():
        ys_ref[...] = jnp.zeros_like(ys_ref)

    def run_rows(r0, n):
        x = _unpack_bf16_pairs(jnp.concatenate([xs_ref[k, r0:r0 + n, :] for k in range(N_SLAB)], axis=-1)).astype(BF16)
        a = jnp.dot(x, w1_scr[...], preferred_element_type=F32)
        b = jnp.dot(x, w3_scr[...], preferred_element_type=F32)
        act = _silu(a) * b
        y = _pack_bf16_pairs(jnp.dot(act.astype(BF16), w2_scr[...], preferred_element_type=F32))
        row = lax.broadcasted_iota(jnp.int32, (n, SLAB), 0) + r0
        mine = (row >= lo) & (row < hi)
        for k in range(N_SLAB):
            ys_ref[k, r0:r0 + n, :] = jnp.where(mine, y[:, k * SLAB:(k + 1) * SLAB], ys_ref[k, r0:r0 + n, :])

    groups = (hi + (X_SUB - 1)) // X_SUB - lo // X_SUB
    whole = groups > 2

    @pl.when(whole)
    def _():
        run_rows(0, n_rows)

    for r0 in range(0, n_rows, X_SUB):
        @pl.when(jnp.logical_not(whole) & (lo < r0 + X_SUB) & (hi > r0))
        def _(r0=r0):
            run_rows(r0, X_SUB)


def _experts(step_tile, step_expert, step_lo, step_hi, xs, w1, w3, w2, layer):
    n_slab, ns, slab = xs.shape
    d = w1.shape[1]
    ff = w1.shape[2]
    n_steps = step_tile.shape[0]
    base = layer * MOE_EXPERTS
    grid_spec = pltpu.PrefetchScalarGridSpec(
        num_scalar_prefetch=4,
        grid=(n_steps,),
        in_specs=[pl.BlockSpec((n_slab, TM_X, slab), lambda s, tl, ex, lo, hi: (0, tl[s], 0)),
                  pl.BlockSpec((None, d, ff), lambda s, tl, ex, lo, hi: (base + ex[s], 0, 0)),
                  pl.BlockSpec((None, d, ff), lambda s, tl, ex, lo, hi: (base + ex[s], 0, 0)),
                  pl.BlockSpec((None, ff, d), lambda s, tl, ex, lo, hi: (base + ex[s], 0, 0))],
        out_specs=pl.BlockSpec((n_slab, TM_X, slab), lambda s, tl, ex, lo, hi: (0, tl[s], 0)),
        scratch_shapes=[pltpu.VMEM((d, ff), BF16), pltpu.VMEM((d, ff), BF16), pltpu.VMEM((ff, d), BF16)],
    )
    return pl.pallas_call(
        _expert_kernel,
        grid_spec=grid_spec,
        out_shape=jax.ShapeDtypeStruct((n_slab, ns, slab), xs.dtype),
        compiler_params=_cparams(("arbitrary",)),
        name="moe_experts",
    )(step_tile, step_expert, step_lo, step_hi, xs, w1, w3, w2)


def _combine_kernel(x_ref, gate_ref, fw_ref, wcol_ref, y1_ref, y2_ref, o_ref, *, final):
    w_first = wcol_ref[:, 0:1]
    w_second = wcol_ref[:, 1:2]
    y_first = _unpack_bf16_pairs(jnp.concatenate([y1_ref[k] for k in range(N_SLAB)], axis=-1))
    y_second = _unpack_bf16_pairs(jnp.concatenate([y2_ref[k] for k in range(N_SLAB)], axis=-1))
    moe = w_first * y_first + w_second * y_second
    x = x_ref[...] + gate_ref[...] * moe
    if final:
        x = x * lax.rsqrt(jnp.mean(x * x, axis=-1, keepdims=True) + EPS) * fw_ref[...]
    o_ref[...] = x


def _combine(x2, mod3, final_w_row, wcol, gathered, seq, final):
    t, d = x2.shape
    tm = TM_COMB
    tpb = seq // tm
    nblk = t // tm
    yspec = lambda off: pl.BlockSpec((N_SLAB, tm, SLAB), lambda i: (0, i + off, 0))
    return pl.pallas_call(
        functools.partial(_combine_kernel, final=final),
        grid=(nblk,),
        in_specs=[pl.BlockSpec((tm, d), lambda i: (i, 0)),
                  pl.BlockSpec((None, 1, d), lambda i: ((i // tpb) * 6 + 5, 0, 0)),
                  pl.BlockSpec((1, d), lambda i: (0, 0)),
                  pl.BlockSpec((tm, LANE), lambda i: (i, 0)),
                  yspec(0), yspec(nblk)],
        out_specs=pl.BlockSpec((tm, d), lambda i: (i, 0)),
        out_shape=jax.ShapeDtypeStruct((t, d), F32),
        compiler_params=_cparams(("parallel",)),
        name="moe_combine",
    )(x2, mod3, final_w_row, wcol, gathered, gathered)


def _moe(x2, mod3, final_w_row, w_route, b_route, tri_excl, w1, w3, w2, layer, seq, final):
    t, d = x2.shape
    h3, ids, wcol, counts = _router(x2, mod3, w_route, b_route, tri_excl, seq)
    cnt = counts[:, 0].astype(jnp.int32)
    ends = jnp.cumsum(cnt)
    offs = ends - cnt
    experts = jnp.arange(MOE_EXPERTS, dtype=jnp.int32)
    pick = lambda table, idx: jnp.sum(jnp.where(idx[:, None] == experts[None, :], table[None, :], 0), axis=1)
    slot1 = pick(offs, ids[0]) + ids[2]
    slot2 = pick(offs, ids[1]) + ids[3]
    n_tiles = 2 * t // TM_X
    first_tile = offs // TM_X
    n_vis = jnp.where(cnt > 0, (ends - 1) // TM_X - first_tile + 1, 0)
    cum = jnp.cumsum(n_vis)
    step = jnp.arange(n_tiles + MOE_EXPERTS, dtype=jnp.int32)
    step_expert = jnp.minimum(jnp.sum(step[:, None] >= cum[None, :], axis=1), MOE_EXPERTS - 1).astype(jnp.int32)
    valid = step < cum[-1]
    step_tile = jnp.where(valid, pick(first_tile - (cum - n_vis), step_expert) + step, n_tiles - 1)
    step_lo = jnp.where(valid, jnp.clip(pick(offs, step_expert) - step_tile * TM_X, 0, TM_X), 0)
    step_hi = jnp.where(valid, jnp.clip(pick(ends, step_expert) - step_tile * TM_X, 0, TM_X), 0)
    xs = _dispatch(slot1, slot2, h3)
    ys = _experts(step_tile.astype(jnp.int32), step_expert, step_lo.astype(jnp.int32), step_hi.astype(jnp.int32),
                  xs, w1, w3, w2, layer)
    n_sorted = ys.shape[1]
    gathered = _gather_rows(ys.reshape(N_SLAB * n_sorted, SLAB), _slab_rows(jnp.concatenate([slot1, slot2]), n_sorted))
    gathered = gathered.reshape(N_SLAB, n_sorted, SLAB)
    return _combine(x2, mod3, final_w_row, wcol, gathered, seq, final)


def kernel(x, c, positions, ada_w, ada_b, w_in, s5_lam_re, s5_lam_im, s5_b_re, s5_b_im, s5_c_re, s5_c_im, s5_d, s5_log_dt, s5_w_glu, hg_lb_logits, hg_norm_w, m2_conv_w, m2_conv_b, m2_dt_bias, m2_a_log, m2_d, m2_norm_w, w_branch, w_gate, b_gate, w_out, moe_w_group, moe_b_group, moe_w_expert, moe_b_expert, moe_w1, moe_w3, moe_w2, final_norm_w):
    bsz, seq, d = x.shape
    t = bsz * seq
    depth = ada_w.shape[0]
    assert seq % TM_PROJ == 0 and seq % C_RET == 0 and seq % C_SSD == 0 and seq % C_HG == 0
    x2 = x.reshape(t, d).astype(F32)

    c_pad = jnp.zeros((8, d), F32).at[:bsz].set(c.astype(F32))
    mod_all = _ada_mod(c_pad.T, ada_w.astype(F32), ada_b.astype(F32), bsz)

    half = RET_DK // 2
    inv_freq = ROPE_BASE ** (-jnp.arange(half, dtype=F32) / half)
    invf_col = jnp.broadcast_to(inv_freq[:, None], (half, LANE))
    expand = np.tile(np.eye(half, dtype=np.float32), (1, 2 * RET_HEADS))
    sign = np.tile(np.concatenate([-np.ones(half), np.ones(half)]), RET_HEADS)[None, :].astype(np.float32)
    cos_t, sin_t = _rope_tables(positions.reshape(1, t).astype(jnp.int32), invf_col,
                                jnp.asarray(expand, BF16), jnp.asarray(expand * sign, BF16))
    cos3 = cos_t.reshape(bsz, seq, -1)
    sin3 = sin_t.reshape(bsz, seq, -1)

    lb_cum = jnp.cumsum(jax.nn.softmax(hg_lb_logits.astype(F32), axis=0), axis=0)
    hg_lb = lb_cum - lb_cum[:1]
    tri_ssd = jnp.asarray(np.tril(np.ones((C_SSD, C_SSD), np.float32)), BF16)
    tri_excl = jnp.asarray(np.triu(np.ones((LANE, LANE), np.float32), 1), BF16)
    final_w_row = final_norm_w.astype(F32)[None, :]
    n_main = IN_W // LANE * LANE
    w_main = w_in[:, :, :n_main].astype(BF16)
    w_tail = jnp.zeros((depth, d, IN_W_PAD - n_main), BF16).at[:, :, :IN_W - n_main].set(w_in[:, :, n_main:].astype(BF16))
    w_glu_bf = s5_w_glu.astype(BF16)
    w_branch_bf = w_branch.astype(BF16)
    w_gate_bf = w_gate.astype(BF16)
    w_out_bf = w_out.astype(BF16)
    b_gate3 = b_gate.astype(F32).reshape(depth, 1, -1)
    moe_w1_all = moe_w1.astype(F32).reshape(depth * MOE_EXPERTS, d, MOE_FF)
    moe_w3_all = moe_w3.astype(F32).reshape(depth * MOE_EXPERTS, d, MOE_FF)
    moe_w2_all = moe_w2.astype(F32).reshape(depth * MOE_EXPERTS, MOE_FF, d)

    for layer in range(depth):
        mod3 = mod_all[layer, :bsz].reshape(bsz * 6, 1, d)
        p, u_s5 = _in_proj(x2, mod3, w_main, w_tail, layer, seq)
        p3 = p.reshape(bsz, seq, IN_W_PAD)

        ops = _s5_operators(s5_lam_re[layer], s5_lam_im[layer], s5_b_re[layer], s5_b_im[layer],
                            s5_c_re[layer], s5_c_im[layer], s5_d[layer], s5_log_dt[layer])
        y_s5 = _s5_scan(u_s5.reshape(bsz, seq, BRANCH_W), *ops).reshape(t, BRANCH_W)

        lb = hg_lb[layer][None, :]
        y_hg = _hgrn2(p3, jnp.log(lb), jnp.log1p(-lb), hg_norm_w[layer].astype(F32)[None, :]).reshape(t, BRANCH_W)

        y_ret = _retention(p3, cos3, sin3).reshape(t, BRANCH_W)

        pad8 = lambda v: jnp.zeros((1, LANE), F32).at[0, :M2_HEADS].set(v.astype(F32))
        y_m2 = _ssd(p3, tri_ssd, m2_conv_w[layer].astype(F32), m2_conv_b[layer].astype(F32)[None, :],
                    pad8(m2_dt_bias[layer]), pad8(m2_a_log[layer]),
                    jnp.repeat(m2_d[layer].astype(F32), M2_HEADDIM)[None, :],
                    m2_norm_w[layer].astype(F32)[None, :]).reshape(t, BRANCH_W)

        nr = 40
        w_route = jnp.zeros((nr, d), F32).at[:MOE_GROUPS].set(moe_w_group[layer].astype(F32).T)
        w_route = w_route.at[MOE_GROUPS:MOE_GROUPS + MOE_EXPERTS].set(moe_w_expert[layer].astype(F32).T)
        b_route = jnp.zeros((nr, LANE), F32).at[:MOE_GROUPS, 0].set(moe_b_group[layer].astype(F32))
        b_route = b_route.at[MOE_GROUPS:MOE_GROUPS + MOE_EXPERTS, 0].set(moe_b_expert[layer].astype(F32))
        x2 = _merge(x2, mod3, y_s5, y_hg, y_ret, y_m2, w_glu_bf, w_branch_bf, w_gate_bf, b_gate3, w_out_bf,
                    layer, seq)
        x2 = _moe(x2, mod3, final_w_row, w_route, b_route, tri_excl, moe_w1_all, moe_w3_all, moe_w2_all,
                  layer, seq, final=(layer == depth - 1))
    return x2.reshape(bsz, seq, d)
```

```python
import functools
import math

import numpy as np
import jax
import jax.numpy as jnp
from jax import lax
from jax.experimental import pallas as pl
from jax.experimental.pallas import tpu as pltpu
from jax.experimental.pallas import tpu_sc as plsc

F32 = jnp.float32
BF16 = jnp.bfloat16
HIGHEST = lax.Precision.HIGHEST

D_MODEL = 1024
BRANCH_W = 512
EPS = 1e-6
S5_GROUPS = 32
S5_CH = 16
S5_STATE = 64
S5_MAX_REAL = -1e-4
S5_BLOCK = 8
S5_SEQ_PER_STEP = 2
HG_HEADS = 4
HG_DK = 128
RET_HEADS = 4
RET_DK = 64
RET_DV = 128
ROPE_BASE = 10000.0
M2_HEADS = 8
M2_HEADDIM = 64
M2_GROUPS = 2
M2_STATE = 128
M2_CONV = 4
MOE_GROUPS = 4
MOE_EPG = 8
MOE_EXPERTS = MOE_GROUPS * MOE_EPG
MOE_FF = 256

COL_S5, COL_HQ, COL_HF, COL_HI, COL_HG = 0, 512, 1024, 1536, 2048
COL_RQ, COL_RK, COL_RV, COL_RG = 2560, 2816, 3072, 3584
COL_MZ, COL_MXS, COL_MBC, COL_MDT = 4096, 4608, 5120, 5632
IN_W = 5640
IN_W_PAD = 5760

LANE = 128
VMEM_LIMIT = 56 * 1024 * 1024

TM_PROJ = 1024
TN_PROJ = 1024
TM_INPROJ = 512
LOG2_E = 1.4426950408889634
C_RET = 512
C_SSD = 256
C_HG = 128
TM_X = 512
X_SUB = 128
TM_COMB = 1024
SC_WINDOW = 128
SLAB = 256
N_SLAB = D_MODEL // 2 // SLAB


def _cparams(sem):
    return pltpu.CompilerParams(dimension_semantics=sem, vmem_limit_bytes=VMEM_LIMIT)


def _silu(v):
    return v * jax.nn.sigmoid(v)


def _dot_nt(a, b, **kw):
    return lax.dot_general(a, b, (((1,), (1,)), ((), ())), preferred_element_type=F32, **kw)


def _dot_tn(a, b, **kw):
    return lax.dot_general(a, b, (((0,), (0,)), ((), ())), preferred_element_type=F32, **kw)


def _ada_kernel(ct_ref, w_ref, b_ref, o_ref, *, n_rows):
    cond_t = _silu(ct_ref[...])
    w = w_ref[...]
    rows = [jnp.sum(w * cond_t[:, b:b + 1], axis=0, keepdims=True) for b in range(n_rows)]
    rows += [jnp.zeros_like(rows[0])] * (cond_t.shape[1] - n_rows)
    o_ref[...] = jnp.concatenate(rows, axis=0) + b_ref[...]


def _ada_mod(c_pad_t, ada_w, ada_b, n_rows):
    depth, d, n = ada_w.shape
    tn = 1536
    return pl.pallas_call(
        functools.partial(_ada_kernel, n_rows=n_rows),
        grid=(depth, n // tn),
        in_specs=[pl.BlockSpec((d, 8), lambda l, j: (0, 0)),
                  pl.BlockSpec((None, d, tn), lambda l, j: (l, 0, j)),
                  pl.BlockSpec((None, 1, tn), lambda l, j: (l, 0, j))],
        out_specs=pl.BlockSpec((None, 8, tn), lambda l, j: (l, 0, j)),
        out_shape=jax.ShapeDtypeStruct((depth, 8, n), F32),
        compiler_params=_cparams(("parallel", "parallel")),
        name="ada_mod",
    )(c_pad_t, ada_w, ada_b.reshape(depth, 1, n))


def _pack_bf16_pairs(x):
    n = x.shape[1] // 2
    lo = pltpu.bitcast(x[:, :n].astype(BF16).astype(F32), jnp.uint32) >> 16
    hi = pltpu.bitcast(x[:, n:].astype(BF16).astype(F32), jnp.uint32)
    return hi | lo


def _unpack_bf16_pairs(w):
    lo = pltpu.bitcast(w << 16, F32)
    hi = pltpu.bitcast(w & jnp.uint32(0xFFFF0000), F32)
    return jnp.concatenate([lo, hi], axis=-1)


def _modulated_norm(x, scale, shift):
    ms = jnp.mean(x * x, axis=-1, keepdims=True)
    return x * lax.rsqrt(ms + EPS) * (1.0 + scale) + shift


def _inproj_kernel(x_ref, sc_ref, sh_ref, w_ref, wtail_ref, o_ref, u_ref):
    h = _modulated_norm(x_ref[...], sc_ref[...], sh_ref[...]).astype(BF16)
    n_main = w_ref.shape[1]
    for n0 in range(0, n_main, TN_PROJ):
        n1 = min(n0 + TN_PROJ, n_main)
        p = jnp.dot(h, w_ref[:, n0:n1], preferred_element_type=F32)
        o_ref[:, n0:n1] = p.astype(o_ref.dtype)
        if n0 == 0:
            u_ref[...] = p[:, COL_S5:COL_S5 + BRANCH_W]
    o_ref[:, n_main:] = jnp.dot(h, wtail_ref[...], preferred_element_type=F32).astype(o_ref.dtype)


def _in_proj(x2, mod3, w_main, w_tail, layer, seq):
    t, d = x2.shape
    tm = TM_INPROJ
    tpb = seq // tm
    n_main = w_main.shape[2]
    assert COL_S5 + BRANCH_W <= TN_PROJ and n_main + w_tail.shape[2] == IN_W_PAD
    return pl.pallas_call(
        _inproj_kernel,
        grid=(t // tm,),
        in_specs=[pl.BlockSpec((tm, d), lambda i: (i, 0)),
                  pl.BlockSpec((None, 1, d), lambda i: ((i // tpb) * 6 + 1, 0, 0)),
                  pl.BlockSpec((None, 1, d), lambda i: ((i // tpb) * 6 + 0, 0, 0)),
                  pl.BlockSpec((None, d, n_main), lambda i: (layer, 0, 0), pipeline_mode=pl.Buffered(1)),
                  pl.BlockSpec((None, d, IN_W_PAD - n_main), lambda i: (layer, 0, 0), pipeline_mode=pl.Buffered(1))],
        out_specs=[pl.BlockSpec((tm, IN_W_PAD), lambda i: (i, 0)),
                   pl.BlockSpec((tm, BRANCH_W), lambda i: (i, 0))],
        out_shape=[jax.ShapeDtypeStruct((t, IN_W_PAD), BF16), jax.ShapeDtypeStruct((t, BRANCH_W), F32)],
        compiler_params=_cparams(("parallel",)),
        name="in_proj",
    )(x2, mod3, mod3, w_main, w_tail)


def _rope_kernel(pos_ref, invf_ref, ecos_ref, esin_ref, cos_ref, sin_ref):
    ang = invf_ref[:, 0:1] * pos_ref[...].astype(F32)
    def spread(values, e_ref):
        hi = values.astype(BF16)
        rest = values - hi.astype(F32)
        mid = rest.astype(BF16)
        lo = (rest - mid.astype(F32)).astype(BF16)
        e = e_ref[...]
        return _dot_tn(hi, e) + _dot_tn(mid, e) + _dot_tn(lo, e)

    cos_ref[...] = spread(jnp.cos(ang), ecos_ref)
    sin_ref[...] = spread(jnp.sin(ang), esin_ref)


def _rope_tables(pos_row, invf_col, expand_cos, expand_sin):
    t = pos_row.shape[1]
    half, w = expand_cos.shape
    tm = 1024
    const = lambda shape: pl.BlockSpec(shape, lambda i: (0, 0))
    return pl.pallas_call(
        _rope_kernel,
        grid=(t // tm,),
        in_specs=[pl.BlockSpec((1, tm), lambda i: (0, i)), const((half, LANE)), const((half, w)), const((half, w))],
        out_specs=[pl.BlockSpec((tm, w), lambda i: (i, 0))] * 2,
        out_shape=[jax.ShapeDtypeStruct((t, w), F32)] * 2,
        compiler_params=_cparams(("parallel",)),
        name="rope_tables",
    )(pos_row, invf_col, expand_cos, expand_sin)


def _ret_kernel(q_ref, k_ref, v_ref, g_ref, cos_ref, sin_ref, o_ref, st_ref, dec_ref, *, chunk):
    @pl.when(pl.program_id(1) == 0)
    def _():
        st_ref[...] = jnp.zeros_like(st_ref)
        ti = lax.broadcasted_iota(jnp.int32, (chunk, chunk), 0)
        si = lax.broadcasted_iota(jnp.int32, (chunk, chunk), 1)
        lag = (ti - si).astype(F32)
        for h in range(RET_HEADS):
            log_gamma = math.log1p(-(2.0 ** (-5.0 - h)))
            dec_ref[h] = jnp.where(ti >= si, jnp.exp(jnp.minimum(lag * log_gamma, 0.0)), 0.0)

    cosf = cos_ref[...]
    sinf = sin_ref[...]
    width = RET_HEADS * RET_DK
    lane = lax.broadcasted_iota(jnp.int32, (chunk, width), 1)
    first_half = (lane % RET_DK) < (RET_DK // 2)

    def rope(t):
        partner = jnp.where(first_half, pltpu.roll(t, width - RET_DK // 2, 1), pltpu.roll(t, RET_DK // 2, 1))
        return t * cosf + partner * sinf

    q = rope(q_ref[...].astype(F32))
    k = rope(k_ref[...].astype(F32)) * (RET_DK ** -0.5)
    v = v_ref[...]
    g = g_ref[...].astype(F32)
    tcol = lax.broadcasted_iota(jnp.int32, (chunk, 1), 0).astype(F32)
    for h in range(RET_HEADS):
        log_gamma = math.log1p(-(2.0 ** (-5.0 - h)))
        qh = q[:, h * RET_DK:(h + 1) * RET_DK]
        kh = k[:, h * RET_DK:(h + 1) * RET_DK]
        vh = v[:, h * RET_DV:(h + 1) * RET_DV].astype(BF16)
        scores = _dot_nt(qh.astype(BF16), kh.astype(BF16)) * dec_ref[h]
        state = st_ref[h]
        q_in = qh * jnp.exp(log_gamma * (tcol + 1.0))
        o = (jnp.dot(scores.astype(BF16), vh, preferred_element_type=F32)
             + jnp.dot(q_in.astype(BF16), state.astype(BF16), preferred_element_type=F32))
        k_out = kh * jnp.exp(log_gamma * (chunk - 1.0 - tcol))
        st_ref[h] = math.exp(log_gamma * chunk) * state + _dot_tn(k_out.astype(BF16), vh)
        o = o * lax.rsqrt(jnp.mean(o * o, axis=-1, keepdims=True) + EPS)
        gh = g[:, h * RET_DV:(h + 1) * RET_DV]
        o_ref[:, h * RET_DV:(h + 1) * RET_DV] = (o * _silu(gh)).astype(o_ref.dtype)


def _retention(p3, cos3, sin3):
    b, seq, _ = p3.shape
    c = C_RET
    qk_w = RET_HEADS * RET_DK
    return pl.pallas_call(
        functools.partial(_ret_kernel, chunk=c),
        grid=(b, seq // c),
        in_specs=[pl.BlockSpec((None, c, qk_w), lambda i, j: (i, j, COL_RQ // qk_w)),
                  pl.BlockSpec((None, c, qk_w), lambda i, j: (i, j, COL_RK // qk_w)),
                  pl.BlockSpec((None, c, BRANCH_W), lambda i, j: (i, j, COL_RV // BRANCH_W)),
                  pl.BlockSpec((None, c, BRANCH_W), lambda i, j: (i, j, COL_RG // BRANCH_W)),
                  pl.BlockSpec((None, c, qk_w), lambda i, j: (i, j, 0)),
                  pl.BlockSpec((None, c, qk_w), lambda i, j: (i, j, 0))],
        out_specs=pl.BlockSpec((None, c, BRANCH_W), lambda i, j: (i, j, 0)),
        out_shape=jax.ShapeDtypeStruct((b, seq, BRANCH_W), BF16),
        scratch_shapes=[pltpu.VMEM((RET_HEADS, RET_DK, RET_DV), F32), pltpu.VMEM((RET_HEADS, c, c), F32)],
        compiler_params=_cparams(("parallel", "arbitrary")),
        name="retention",
    )(p3, p3, p3, p3, cos3, sin3)


def _ssd_kernel(z_ref, xs_ref, bc_ref, dt_ref, tri_ref, cw_ref, cb_ref, dtb_ref, alog_ref, dsk_ref, nw_ref,
                o_ref, xe_scr, st_ref, *, chunk):
    j = pl.program_id(1)
    width = 2 * BRANCH_W

    @pl.when(j == 0)
    def _():
        st_ref[...] = jnp.zeros_like(st_ref)
        xe_scr[0:8, :] = jnp.zeros((8, width), F32)

    @pl.when(j > 0)
    def _():
        xe_scr[0:8, :] = xe_scr[chunk:chunk + 8, :]

    xe_scr[8:, 0:BRANCH_W] = xs_ref[...].astype(F32)
    xe_scr[8:, BRANCH_W:] = bc_ref[...].astype(F32)
    conv = cb_ref[...] + cw_ref[M2_CONV - 1:M2_CONV, :] * xe_scr[8:, :]
    for tap in range(M2_CONV - 1):
        conv = conv + cw_ref[tap:tap + 1, :] * xe_scr[pl.ds(8 - (M2_CONV - 1) + tap, chunk), :]
    conv = _silu(conv)
    xs = conv[:, :BRANCH_W]
    bm = conv[:, BRANCH_W:BRANCH_W + M2_GROUPS * M2_STATE]
    cm = conv[:, BRANCH_W + M2_GROUPS * M2_STATE:]

    dt = jax.nn.softplus(dt_ref[...].astype(F32) + dtb_ref[...])
    da = dt * (-jnp.exp(alog_ref[...]))
    da_hi = da.astype(BF16)
    da_r = da - da_hi.astype(F32)
    da_mid = da_r.astype(BF16)
    da_lo = (da_r - da_mid.astype(F32)).astype(BF16)
    tri = tri_ref[...]
    a_cs = (jnp.dot(tri, da_hi, preferred_element_type=F32) + jnp.dot(tri, da_mid, preferred_element_type=F32)
            + jnp.dot(tri, da_lo, preferred_element_type=F32))
    a_cs = a_cs * LOG2_E
    a_cs_t = a_cs.T
    ti = lax.broadcasted_iota(jnp.int32, (chunk, chunk), 0)
    si = lax.broadcasted_iota(jnp.int32, (chunk, chunk), 1)
    causal = ti >= si
    hpg = M2_HEADS // M2_GROUPS
    pair_w = 2 * M2_HEADDIM
    upper = lax.broadcasted_iota(jnp.int32, (chunk, pair_w), 1) >= M2_HEADDIM
    upper_state = lax.broadcasted_iota(jnp.int32, (M2_STATE, pair_w), 1) >= M2_HEADDIM
    ys = []
    for grp in range(M2_GROUPS):
        bm_g = bm[:, grp * M2_STATE:(grp + 1) * M2_STATE]
        cm_g = cm[:, grp * M2_STATE:(grp + 1) * M2_STATE]
        bm_bf = bm_g.astype(BF16)
        cb = _dot_nt(cm_g.astype(BF16), bm_bf)
        for pp in range(hpg // 2):
            h0 = grp * hpg + 2 * pp
            pair = h0 // 2
            xd = xs[:, pair * pair_w:(pair + 1) * pair_w] * jnp.where(upper, dt[:, h0 + 1:h0 + 2], dt[:, h0:h0 + 1])
            xd_bf = xd.astype(BF16)
            state = st_ref[pair]
            state_bf = state.astype(BF16)
            y_heads, state_heads = [], []
            for h in (h0, h0 + 1):
                col = a_cs[:, h:h + 1]
                row = a_cs_t[h:h + 1, :]
                lmat = jnp.where(causal, jnp.exp2(col - row), 0.0)
                y_heads.append(jnp.dot((cb * lmat).astype(BF16), xd_bf, preferred_element_type=F32)
                               + jnp.dot((cm_g * jnp.exp2(col)).astype(BF16), state_bf, preferred_element_type=F32))
                a_last = a_cs[chunk - 1:chunk, h:h + 1]
                to_end = jnp.exp2(a_last - col)
                state_heads.append(jnp.exp2(a_last) * state + _dot_tn(bm_bf, (xd * to_end).astype(BF16)))
            ys.append(jnp.where(upper, y_heads[1], y_heads[0]))
            st_ref[pair] = jnp.where(upper_state, state_heads[1], state_heads[0])
    y = jnp.concatenate(ys, axis=-1) + dsk_ref[...] * xs
    y = y * _silu(z_ref[...].astype(F32))
    o_ref[...] = (y * lax.rsqrt(jnp.mean(y * y, axis=-1, keepdims=True) + EPS) * nw_ref[...]).astype(o_ref.dtype)


def _ssd(p3, tri, conv_w, conv_b, dt_bias_row, a_log_row, d_skip_row, norm_w_row):
    b, seq, _ = p3.shape
    c = C_SSD
    const = lambda shape: pl.BlockSpec(shape, lambda i, j: (0,) * len(shape))
    return pl.pallas_call(
        functools.partial(_ssd_kernel, chunk=c),
        grid=(b, seq // c),
        in_specs=[pl.BlockSpec((None, c, BRANCH_W), lambda i, j: (i, j, COL_MZ // BRANCH_W)),
                  pl.BlockSpec((None, c, BRANCH_W), lambda i, j: (i, j, COL_MXS // BRANCH_W)),
                  pl.BlockSpec((None, c, BRANCH_W), lambda i, j: (i, j, COL_MBC // BRANCH_W)),
                  pl.BlockSpec((None, c, LANE), lambda i, j: (i, j, COL_MDT // LANE)),
                  const((c, c)), const((M2_CONV, 2 * BRANCH_W)), const((1, 2 * BRANCH_W)),
                  const((1, LANE)), const((1, LANE)), const((1, BRANCH_W)), const((1, BRANCH_W))],
        out_specs=pl.BlockSpec((None, c, BRANCH_W), lambda i, j: (i, j, 0)),
        out_shape=jax.ShapeDtypeStruct((b, seq, BRANCH_W), BF16),
        scratch_shapes=[pltpu.VMEM((c + 8, 2 * BRANCH_W), F32),
                        pltpu.VMEM((M2_HEADS // 2, M2_STATE, 2 * M2_HEADDIM), F32)],
        compiler_params=_cparams(("parallel", "arbitrary")),
        name="ssd",
    )(p3, p3, p3, p3, tri, conv_w, conv_b, dt_bias_row, a_log_row, d_skip_row, norm_w_row)


def _hg_tables(chunk):
    n_lev = int(math.log2(chunk))
    r = np.arange(chunk)[:, None]
    jj = np.arange(chunk)[None, :]
    tri = (jj <= r).astype(np.float32)
    x = r ^ jj
    levmap = np.where(r > jj, np.floor(np.log2(x + 0.5)), np.where(r == jj, -1, -2)).astype(np.int32)
    return tri, levmap, n_lev


def _hg_level_exponent(b, lev):
    rows, width = b.shape
    m = 1 << lev
    sub = 8
    if 2 * m >= sub:
        blocks = b.reshape(rows // (2 * m), 2 * m, width)
        mid = jnp.broadcast_to(blocks[:, m - 1:m, :], blocks.shape).reshape(rows, width)
    else:
        groups = b.reshape(rows // sub, sub, width)
        row_in_group = lax.broadcasted_iota(jnp.int32, groups.shape, 1)
        mid = None
        for start in range(0, sub, 2 * m):
            picked = jnp.broadcast_to(groups[:, start + m - 1:start + m, :], groups.shape)
            mid = picked if mid is None else jnp.where(row_in_group >= start, picked, mid)
        mid = mid.reshape(rows, width)
    return -jnp.abs(b - mid)


def _hg_kernel(q_ref, f_ref, i_ref, g_ref, sum_ref, lev_ref, llb_ref, l1m_ref, nw_ref, o_ref, st_ref,
               *, chunk, n_lev):
    @pl.when(pl.program_id(1) == 0)
    def _():
        st_ref[...] = jnp.zeros_like(st_ref)

    f = f_ref[...].astype(F32)
    y = jnp.exp(-jnp.abs(f))
    one_plus_y = 1.0 + y
    log_sig = jnp.minimum(f, 0.0) - jnp.log(one_plus_y)
    a = llb_ref[...]
    bb = l1m_ref[...] + log_sig
    log_f = jnp.maximum(a, bb) + jnp.log(1.0 + jnp.exp(-jnp.abs(a - bb)))
    k_all = jnp.exp(l1m_ref[...]) * (jnp.where(f >= 0.0, y, 1.0) / one_plus_y)
    q_all = _silu(q_ref[...].astype(F32))
    hi = log_f.astype(BF16)
    r1 = log_f - hi.astype(F32)
    mid = r1.astype(BF16)
    lo = (r1 - mid.astype(F32)).astype(BF16)
    tri = sum_ref[...]
    b_all = (jnp.dot(tri, hi, preferred_element_type=F32)
             + jnp.dot(tri, mid, preferred_element_type=F32)
             + jnp.dot(tri, lo, preferred_element_type=F32))
    b_all = b_all * LOG2_E
    to_end_all = b_all[chunk - 1:chunk, :] - b_all
    level_decay = [jnp.exp2(_hg_level_exponent(b_all, lev)) for lev in range(n_lev)]
    levmap = lev_ref[...]
    on_diag = levmap == -1
    on_level = [levmap == lev for lev in range(n_lev)]
    v_all = i_ref[...]
    g_all = g_ref[...].astype(F32)
    for h in range(HG_HEADS):
        sl = slice(h * HG_DK, (h + 1) * HG_DK)
        qh = q_all[:, sl]
        kh = k_all[:, sl]
        vh = v_all[:, sl].astype(BF16)
        b_h = b_all[:, sl]
        to_end = to_end_all[:, sl]
        amat = jnp.where(on_diag, _dot_nt(qh.astype(BF16), kh.astype(BF16)), 0.0)
        for lev in range(n_lev):
            e = level_decay[lev][:, sl]
            a_l = _dot_nt((qh * e).astype(BF16), (kh * e).astype(BF16))
            amat = jnp.where(on_level[lev], a_l, amat)
        state_t = st_ref[h]
        o = (jnp.dot(amat.astype(BF16), vh, preferred_element_type=F32)
             + _dot_nt((qh * jnp.exp2(b_h)).astype(BF16), state_t.astype(BF16)))
        k_end = kh * jnp.exp2(to_end)
        st_ref[h] = jnp.exp2(b_h[chunk - 1:chunk, :]) * state_t + _dot_tn(vh, k_end.astype(BF16))
        o = o * lax.rsqrt(jnp.mean(o * o, axis=-1, keepdims=True) + EPS) * nw_ref[...]
        o_ref[:, sl] = (o * _silu(g_all[:, sl])).astype(o_ref.dtype)


def _hgrn2(p3, log_lb_row, log1m_lb_row, norm_w_row):
    b, seq, _ = p3.shape
    c = C_HG
    tri, levmap, n_lev = _hg_tables(c)
    const = lambda shape: pl.BlockSpec(shape, lambda i, j: (0,) * len(shape))
    blk = lambda col: pl.BlockSpec((None, c, BRANCH_W), lambda i, j: (i, j, col // BRANCH_W))
    return pl.pallas_call(
        functools.partial(_hg_kernel, chunk=c, n_lev=n_lev),
        grid=(b, seq // c),
        in_specs=[blk(COL_HQ), blk(COL_HF), blk(COL_HI), blk(COL_HG),
                  const((c, c)), const((c, c)),
                  const((1, BRANCH_W)), const((1, BRANCH_W)), const((1, HG_DK))],
        out_specs=pl.BlockSpec((None, c, BRANCH_W), lambda i, j: (i, j, 0)),
        out_shape=jax.ShapeDtypeStruct((b, seq, BRANCH_W), BF16),
        scratch_shapes=[pltpu.VMEM((HG_HEADS, HG_DK, HG_DK), F32)],
        compiler_params=_cparams(("parallel", "arbitrary")),
        name="hgrn2",
    )(p3, p3, p3, p3, jnp.asarray(tri, BF16), jnp.asarray(levmap), log_lb_row, log1m_lb_row, norm_w_row)


def _expand_block_diag(comp_ref, e_ref, dst_ref, row_div, lane_div, causal=False):
    gq = LANE // S5_CH
    rows, ncols = dst_ref.shape
    step = 512
    for c0 in range(0, ncols, step):
        r1 = min(rows, c0 + step) if causal else rows
        row_grp = (lax.broadcasted_iota(jnp.int32, (r1, step), 0) // row_div) % gq
        lane_grp = ((lax.broadcasted_iota(jnp.int32, (r1, step), 1) + c0) // lane_div) % gq
        full = jnp.dot(comp_ref[0:r1, :], e_ref[:, c0:c0 + step], preferred_element_type=F32)
        dst_ref[0:r1, c0:c0 + step] = jnp.where(row_grp == lane_grp, full, 0.0).astype(dst_ref.dtype)


def _s5_kernel(u_ref, k2_ref, bc_ref, cc_ref, esc_ref, lam_ref, o_ref, tc_scr, tq_ref, bqt_ref, cq_ref,
               x_scr, w_scr, s_scr, *, rows):
    nb = S5_BLOCK

    @pl.when(pl.program_id(1) == 0)
    def _():
        k2 = k2_ref[...]
        lane = lax.broadcasted_iota(jnp.int32, k2.shape, 1)
        for t in range(nb):
            shifted = k2 if t == 0 else jnp.where(lane >= t * S5_CH, pltpu.roll(k2, t * S5_CH, 1), 0.0)
            tc_scr[t * LANE:(t + 1) * LANE, :] = shifted.astype(tc_scr.dtype)
        _expand_block_diag(tc_scr, esc_ref, tq_ref, S5_CH, S5_CH, causal=True)
        _expand_block_diag(bc_ref, esc_ref, bqt_ref, S5_STATE, S5_CH)
        _expand_block_diag(cc_ref, esc_ref, cq_ref, S5_STATE, S5_CH)

    n_seq = u_ref.shape[0]
    for b in range(n_seq):
        for t in range(nb):
            x_scr[b * rows:(b + 1) * rows, t * LANE:(t + 1) * LANE] = (
                u_ref[b, pl.ds(t, rows, stride=nb), :].astype(x_scr.dtype))
    x = x_scr[...]
    half = w_scr.shape[1] // 2
    w_scr[...] = _dot_nt(x, bqt_ref[...])
    lam_re = lam_ref[0:1, :]
    lam_im = lam_ref[1:2, :]

    def body(j, carry):
        out = []
        for b in range(n_seq):
            s_re, s_im = carry[2 * b], carry[2 * b + 1]
            r = b * rows + j
            s_scr[pl.ds(r, 1), 0:half] = s_re
            s_scr[pl.ds(r, 1), half:] = s_im
            w_re = w_scr[pl.ds(r, 1), 0:half]
            w_im = w_scr[pl.ds(r, 1), half:]
            out += [lam_re * s_re - lam_im * s_im + w_re, lam_re * s_im + lam_im * s_re + w_im]
        return tuple(out)

    zero = jnp.zeros((1, half), F32)
    lax.fori_loop(0, rows, body, (zero,) * (2 * n_seq))
    s_bf = s_scr[...].astype(BF16)
    pair = 2 * LANE
    for c0 in range(0, nb * LANE, pair):
        k_rows = c0 + pair
        y = (jnp.dot(x[:, :k_rows], tq_ref[0:k_rows, c0:c0 + pair], preferred_element_type=F32)
             + jnp.dot(s_bf, cq_ref[:, c0:c0 + pair], preferred_element_type=F32))
        for b in range(n_seq):
            for t in range(c0 // LANE, (c0 + pair) // LANE):
                o_ref[b, pl.ds(t, rows, stride=nb), :] = y[b * rows:(b + 1) * rows, t * LANE - c0:(t + 1) * LANE - c0]


def _s5_scan(p3, k2, bc, cc, lam16):
    batch, seq, _ = p3.shape
    nb = S5_BLOCK
    nq = BRANCH_W // LANE
    rows = seq // nb
    kdim = nb * LANE
    gq = LANE // S5_CH
    ncol = 2 * gq * S5_STATE
    e_sc = (np.eye(nb)[:, None, :, None, None] * np.eye(S5_CH)[None, :, None, None, :] * np.ones((1, 1, 1, gq, 1)))
    e_sc = e_sc.reshape(nb * S5_CH, nb * gq * S5_CH)
    full = lambda shape: pl.BlockSpec(shape, lambda q, b: (0,) * len(shape))
    per_q = lambda r, c: pl.BlockSpec((None, r, c), lambda q, b: (q, 0, 0))
    n_seq = S5_SEQ_PER_STEP if batch % S5_SEQ_PER_STEP == 0 else 1
    return pl.pallas_call(
        functools.partial(_s5_kernel, rows=rows),
        grid=(nq, batch // n_seq),
        in_specs=[pl.BlockSpec((n_seq, seq, LANE), lambda q, b: (b, 0, q)),
                  per_q(LANE, nb * S5_CH), per_q(ncol, nb * S5_CH), per_q(ncol, nb * S5_CH),
                  full(e_sc.shape), per_q(2, ncol // 2)],
        out_specs=pl.BlockSpec((n_seq, seq, LANE), lambda q, b: (b, 0, q)),
        out_shape=jax.ShapeDtypeStruct((batch, seq, BRANCH_W), F32),
        scratch_shapes=[pltpu.VMEM((kdim, nb * S5_CH), BF16),
                        pltpu.VMEM((kdim, kdim), BF16), pltpu.VMEM((ncol, kdim), BF16), pltpu.VMEM((ncol, kdim), BF16),
                        pltpu.VMEM((n_seq * rows, kdim), BF16), pltpu.VMEM((n_seq * rows, ncol), F32),
                        pltpu.VMEM((n_seq * rows, ncol), F32)],
        compiler_params=_cparams(("parallel", "arbitrary")),
        name="s5_scan",
    )(p3, k2, bc, cc, jnp.asarray(e_sc, BF16), lam16)


def _s5_operators(lam_re, lam_im, b_re, b_im, c_re, c_im, d_skip, log_dt):
    nb = S5_BLOCK
    gq = LANE // S5_CH
    nq = S5_GROUPS // gq
    lam = lax.complex(jnp.minimum(lam_re.astype(F32), S5_MAX_REAL), lam_im.astype(F32))
    step = jnp.exp(log_dt.astype(F32))[:, None]
    z = lam * step
    lam_bar = jnp.exp(z)
    b_bar = ((lam_bar - 1.0) / lam)[..., None] * lax.complex(b_re.astype(F32), b_im.astype(F32))
    c_mat = lax.complex(c_re.astype(F32), c_im.astype(F32))
    pw = jnp.exp(z[..., None] * jnp.arange(nb + 1, dtype=F32))
    cp = c_mat[:, None, :, :] * pw[..., :nb].transpose(0, 2, 1)[:, :, None, :]
    cp = jnp.concatenate([cp.real, -cp.imag], axis=-1).reshape(S5_GROUPS, nb * S5_CH, 2 * S5_STATE)
    bri = jnp.concatenate([b_bar.real, b_bar.imag], axis=1)
    kern = jnp.einsum('gnk,gki->gin', cp, bri, precision=HIGHEST)
    skip = (jnp.asarray(np.concatenate([np.eye(S5_CH), np.zeros((S5_CH, (nb - 1) * S5_CH))], axis=1), F32)[None]
            * d_skip.astype(F32).reshape(S5_GROUPS, S5_CH, 1))
    k2 = (kern + skip).reshape(nq, gq * S5_CH, nb * S5_CH)
    pw_rev = jnp.exp(z[..., None] * jnp.asarray(np.arange(nb - 1, -1, -1), F32))
    binc = pw_rev[:, :, :, None] * b_bar[:, :, None, :]
    binc = jnp.stack([binc.real, binc.imag], axis=0).reshape(2, nq, gq * S5_STATE, nb * S5_CH)
    bc = binc.transpose(1, 0, 2, 3).reshape(nq, 2 * gq * S5_STATE, nb * S5_CH)
    cm = c_mat.transpose(0, 2, 1)[:, :, None, :] * pw[..., 1:][:, :, :, None]
    cm = jnp.stack([cm.real, -cm.imag], axis=0).reshape(2, nq, gq * S5_STATE, nb * S5_CH)
    cc = cm.transpose(1, 0, 2, 3).reshape(nq, 2 * gq * S5_STATE, nb * S5_CH)
    lam_n = pw[..., nb].reshape(nq, gq * S5_STATE)
    lam16 = jnp.stack([lam_n.real, lam_n.imag], axis=1)
    return k2, bc.astype(BF16), cc.astype(BF16), lam16


def _merge_kernel(x_ref, sc_ref, sh_ref, gm_ref, ys5_ref, yhg_ref, yret_ref, ym2_ref,
                  wglu_ref, wbr_ref, wg_ref, bg_ref, wout_ref, o_ref):
    x = x_ref[...]
    d = x.shape[1]
    h = _modulated_norm(x, sc_ref[...], sh_ref[...]).astype(BF16)
    y_s5 = jax.nn.gelu(ys5_ref[...])
    y_s5 = y_s5 * jax.nn.sigmoid(jnp.dot(y_s5.astype(BF16), wglu_ref[...], preferred_element_type=F32))
    acc = jnp.zeros(x.shape, F32)
    for n, y in enumerate((y_s5, yhg_ref[...], yret_ref[...], ym2_ref[...])):
        gate = jax.nn.sigmoid(jnp.dot(h, wg_ref[:, n * d:(n + 1) * d], preferred_element_type=F32)
                              + bg_ref[:, n * d:(n + 1) * d])
        acc = acc + gate * jnp.dot(y.astype(BF16), wbr_ref[n], preferred_element_type=F32)
    o_ref[...] = x + gm_ref[...] * jnp.dot(acc.astype(BF16), wout_ref[...], preferred_element_type=F32)


def _merge(x2, mod3, ys5, yhg, yret, ym2, w_glu, w_branch, w_gate, b_gate, w_out, layer, seq):
    t, d = x2.shape
    tm = TM_PROJ
    tpb = seq // tm
    const = lambda shape: pl.BlockSpec((None,) + shape, lambda i: (layer,) + (0,) * len(shape),
                                       pipeline_mode=pl.Buffered(1))
    modspec = lambda k: pl.BlockSpec((None, 1, d), lambda i: ((i // tpb) * 6 + k, 0, 0))
    yspec = pl.BlockSpec((tm, BRANCH_W), lambda i: (i, 0))
    return pl.pallas_call(
        _merge_kernel,
        grid=(t // tm,),
        in_specs=[pl.BlockSpec((tm, d), lambda i: (i, 0)), modspec(1), modspec(0), modspec(2),
                  yspec, yspec, yspec, yspec,
                  const((BRANCH_W, BRANCH_W)), const((4, BRANCH_W, d)), const((d, 4 * d)), const((1, 4 * d)),
                  const((d, d))],
        out_specs=pl.BlockSpec((tm, d), lambda i: (i, 0)),
        out_shape=jax.ShapeDtypeStruct((t, d), F32),
        compiler_params=_cparams(("parallel",)),
        name="merge",
    )(x2, mod3, mod3, mod3, ys5, yhg, yret, ym2, w_glu, w_branch, w_gate, b_gate, w_out)


def _router_kernel(x_ref, sc_ref, sh_ref, wr_ref, br_ref, tri_ref, h_ref, ids_ref, wts_ref, cnt_ref, carry):
    i = pl.program_id(0)

    @pl.when(i == 0)
    def _():
        carry[...] = jnp.zeros_like(carry)

    h = _modulated_norm(x_ref[...], sc_ref[...], sh_ref[...])
    tm, d = h.shape
    packed = _pack_bf16_pairs(h)
    for k in range(N_SLAB):
        h_ref[k] = packed[:, k * SLAB:(k + 1) * SLAB]
    h_hi = h.astype(BF16)
    h_lo = (h - h_hi.astype(F32)).astype(BF16)
    w_r = wr_ref[...]
    w_hi = w_r.astype(BF16)
    w_lo = (w_r - w_hi.astype(F32)).astype(BF16)
    logits = _dot_nt(w_hi, h_hi) + _dot_nt(w_hi, h_lo) + _dot_nt(w_lo, h_hi) + br_ref[:, 0:1]
    gl = [logits[g:g + 1, :] for g in range(MOE_GROUPS)]
    gmax = gl[0]
    gsel = jnp.zeros((1, tm), jnp.int32)
    for g in range(1, MOE_GROUPS):
        better = gl[g] > gmax
        gsel = jnp.where(better, g, gsel)
        gmax = jnp.where(better, gl[g], gmax)
    gden = gl[0] * 0.0
    for g in range(MOE_GROUPS):
        gden = gden + jnp.exp(gl[g] - gmax)
    g_w = 1.0 / gden
    el = []
    for e in range(MOE_EPG):
        v = logits[MOE_GROUPS + e:MOE_GROUPS + e + 1, :]
        for g in range(1, MOE_GROUPS):
            row = MOE_GROUPS + g * MOE_EPG + e
            v = jnp.where(gsel == g, logits[row:row + 1, :], v)
        el.append(v)
    v1 = el[0]
    i1 = jnp.zeros((1, tm), jnp.int32)
    for e in range(1, MOE_EPG):
        better = el[e] > v1
        i1 = jnp.where(better, e, i1)
        v1 = jnp.where(better, el[e], v1)
    v2 = jnp.full((1, tm), -jnp.inf, F32)
    i2 = jnp.zeros((1, tm), jnp.int32)
    for e in range(MOE_EPG):
        better = (el[e] > v2) & (i1 != e)
        i2 = jnp.where(better, e, i2)
        v2 = jnp.where(better, el[e], v2)
    ex = jnp.exp(v2 - v1)
    p1 = 1.0 / (1.0 + ex)
    e1 = gsel * MOE_EPG + i1
    e2 = gsel * MOE_EPG + i2
    erow = lax.broadcasted_iota(jnp.int32, (MOE_EXPERTS, tm), 0)
    oh1 = (erow == e1).astype(F32)
    oh2 = (erow == e2).astype(F32)
    both = oh1 + oh2
    n_grp = tm // LANE
    stacked = jnp.concatenate([both[:, g * LANE:(g + 1) * LANE] for g in range(n_grp)], axis=0)
    within = jnp.dot(stacked.astype(BF16), tri_ref[...], preferred_element_type=F32)
    grp_count = jnp.sum(stacked, axis=1, keepdims=True)
    running = carry[:, 0:1]
    pieces = []
    for g in range(n_grp):
        pieces.append(within[g * MOE_EXPERTS:(g + 1) * MOE_EXPERTS, :] + running)
        running = running + grp_count[g * MOE_EXPERTS:(g + 1) * MOE_EXPERTS, :]
    prefix = jnp.concatenate(pieces, axis=1)
    rank1 = jnp.sum(oh1 * prefix, axis=0, keepdims=True).astype(jnp.int32)
    rank2 = jnp.sum(oh2 * prefix, axis=0, keepdims=True).astype(jnp.int32)
    carry[...] = jnp.broadcast_to(running, carry.shape)
    zi = jnp.zeros((1, tm), jnp.int32)
    ids_ref[...] = jnp.concatenate([e1, e2, rank1, rank2, zi, zi, zi, zi], axis=0)
    wrow = lax.broadcasted_iota(jnp.int32, (LANE, tm), 0)
    wts_ref[...] = jnp.where(wrow == 0, p1 * g_w, jnp.where(wrow == 1, ex * p1 * g_w, 0.0)).T
    cnt_ref[...] = carry[...]


def _router(x2, mod3, w_route, b_route, tri_excl, seq):
    t, d = x2.shape
    tm = TM_PROJ
    tpb = seq // tm
    nr = w_route.shape[0]
    const = lambda shape: pl.BlockSpec(shape, lambda i: (0,) * len(shape))
    modspec = lambda k: pl.BlockSpec((None, 1, d), lambda i: ((i // tpb) * 6 + k, 0, 0))
    return pl.pallas_call(
        _router_kernel,
        grid=(t // tm,),
        in_specs=[pl.BlockSpec((tm, d), lambda i: (i, 0)), modspec(4), modspec(3),
                  const((nr, d)), const((nr, LANE)), const((LANE, LANE))],
        out_specs=[pl.BlockSpec((N_SLAB, tm, SLAB), lambda i: (0, i, 0)),
                   pl.BlockSpec((8, tm), lambda i: (0, i)),
                   pl.BlockSpec((tm, LANE), lambda i: (i, 0)),
                   const((MOE_EXPERTS, LANE))],
        out_shape=[jax.ShapeDtypeStruct((N_SLAB, t, SLAB), jnp.uint32),
                   jax.ShapeDtypeStruct((8, t), jnp.int32),
                   jax.ShapeDtypeStruct((t, LANE), F32),
                   jax.ShapeDtypeStruct((MOE_EXPERTS, LANE), F32)],
        scratch_shapes=[pltpu.VMEM((MOE_EXPERTS, LANE), F32)],
        compiler_params=_cparams(("arbitrary",)),
        name="moe_router",
    )(x2, mod3, mod3, w_route, b_route, tri_excl)


def _sc_mesh():
    return plsc.VectorSubcoreMesh(core_axis_name="core", subcore_axis_name="subcore")


def _slab_rows(idx, n_rows):
    return (idx[None, :] + (jnp.arange(N_SLAB, dtype=jnp.int32) * n_rows)[:, None]).reshape(-1)


def _dispatch(slot1, slot2, h_slabs):
    n_slab, t, d = h_slabs.shape
    n_out = 2 * t
    xs = _scatter_rows(h_slabs.reshape(n_slab * t, d), _slab_rows(slot1, n_out), _slab_rows(slot2, n_out),
                       n_slab * n_out)
    return xs.reshape(n_slab, n_out, d)


def _scatter_rows(src, idx1, idx2, n_out):
    t, d = src.shape
    win = SC_WINDOW

    @pl.kernel(out_type=jax.ShapeDtypeStruct((n_out, d), src.dtype), mesh=_sc_mesh(), name="moe_dispatch_sc")
    def scatter_rows(x_hbm, i1_hbm, i2_hbm, o_hbm):
        def body(x_vmem, i1_vmem, i2_vmem):
            pltpu.sync_copy(x_vmem, o_hbm.at[i1_vmem.at[0]])
            pltpu.sync_copy(x_vmem, o_hbm.at[i2_vmem.at[0]])

        pltpu.emit_pipeline(
            body,
            grid=(t // win,),
            in_specs=[pl.BlockSpec((win, d), lambda i: (i, 0)),
                      pl.BlockSpec((1, win), lambda i: (0, i)),
                      pl.BlockSpec((1, win), lambda i: (0, i))],
            out_specs=[],
            core_axis_name=("core", "subcore"),
            dimension_semantics=(pltpu.PARALLEL,),
        )(x_hbm, i1_hbm, i2_hbm)

    return scatter_rows(src, idx1.reshape(1, t), idx2.reshape(1, t))


def _gather_rows(src, idx):
    m = idx.shape[0]
    d = src.shape[1]
    win = SC_WINDOW

    @pl.kernel(out_type=jax.ShapeDtypeStruct((m, d), src.dtype), mesh=_sc_mesh(), name="moe_gather_sc")
    def gather(x_hbm, i_hbm, o_hbm):
        def body(i_vmem, o_vmem):
            pltpu.sync_copy(x_hbm.at[i_vmem.at[0]], o_vmem)

        pltpu.emit_pipeline(
            body,
            grid=(m // win,),
            in_specs=[pl.BlockSpec((1, win), lambda i: (0, i))],
            out_specs=[pl.BlockSpec((win, d), lambda i: (i, 0))],
            core_axis_name=("core", "subcore"),
            dimension_semantics=(pltpu.PARALLEL,),
        )(i_hbm, o_hbm)

    return gather(src, idx.reshape(1, m))


def _expert_kernel(tile_ref, exp_ref, lo_ref, hi_ref, xs_ref, w1_ref, w3_ref, w2_ref, ys_ref, w1_scr, w3_scr, w2_scr):
    s = pl.program_id(0)
    prev = jnp.maximum(s - 1, 0)
    new_expert = (s == 0) | (exp_ref[s] != exp_ref[prev])
    new_tile = (s == 0) | (tile_ref[s] != tile_ref[prev])

    @pl.when(new_expert)
    def _():
        w1_scr[...] = w1_ref[...].astype(BF16)
        w3_scr[...] = w3_ref[...].astype(BF16)
        w2_scr[...] = w2_ref[...].astype(BF16)

    lo = lo_ref[s]
    hi = hi_ref[s]
    n_rows = xs_ref.shape[1]

    @pl.when(new_tile)
    def _():
        ys_ref[...] = jnp.zeros_like(ys_ref)

    def run_rows(r0, n):
        x = _unpack_bf16_pairs(jnp.concatenate([xs_ref[k, r0:r0 + n, :] for k in range(N_SLAB)], axis=-1)).astype(BF16)
        a = jnp.dot(x, w1_scr[...], preferred_element_type=F32)
        b = jnp.dot(x, w3_scr[...], preferred_element_type=F32)
        act = _silu(a) * b
        y = _pack_bf16_pairs(jnp.dot(act.astype(BF16), w2_scr[...], preferred_element_type=F32))
        row = lax.broadcasted_iota(jnp.int32, (n, SLAB), 0) + r0
        mine = (row >= lo) & (row < hi)
        for k in range(N_SLAB):
            ys_ref[k, r0:r0 + n, :] = jnp.where(mine, y[:, k * SLAB:(k + 1) * SLAB], ys_ref[k, r0:r0 + n, :])

    groups = (hi + (X_SUB - 1)) // X_SUB - lo // X_SUB
    whole = groups > 2

    @pl.when(whole)
    def _():
        run_rows(0, n_rows)

    for r0 in range(0, n_rows, X_SUB):
        @pl.when(jnp.logical_not(whole) & (lo < r0 + X_SUB) & (hi > r0))
        def _(r0=r0):
            run_rows(r0, X_SUB)


def _experts(step_tile, step_expert, step_lo, step_hi, xs, w1, w3, w2, layer):
    n_slab, ns, slab = xs.shape
    d = w1.shape[1]
    ff = w1.shape[2]
    n_steps = step_tile.shape[0]
    base = layer * MOE_EXPERTS
    grid_spec = pltpu.PrefetchScalarGridSpec(
        num_scalar_prefetch=4,
        grid=(n_steps,),
        in_specs=[pl.BlockSpec((n_slab, TM_X, slab), lambda s, tl, ex, lo, hi: (0, tl[s], 0)),
                  pl.BlockSpec((None, d, ff), lambda s, tl, ex, lo, hi: (base + ex[s], 0, 0)),
                  pl.BlockSpec((None, d, ff), lambda s, tl, ex, lo, hi: (base + ex[s], 0, 0)),
                  pl.BlockSpec((None, ff, d), lambda s, tl, ex, lo, hi: (base + ex[s], 0, 0))],
        out_specs=pl.BlockSpec((n_slab, TM_X, slab), lambda s, tl, ex, lo, hi: (0, tl[s], 0)),
        scratch_shapes=[pltpu.VMEM((d, ff), BF16), pltpu.VMEM((d, ff), BF16), pltpu.VMEM((ff, d), BF16)],
    )
    return pl.pallas_call(
        _expert_kernel,
        grid_spec=grid_spec,
        out_shape=jax.ShapeDtypeStruct((n_slab, ns, slab), xs.dtype),
        compiler_params=_cparams(("arbitrary",)),
        name="moe_experts",
    )(step_tile, step_expert, step_lo, step_hi, xs, w1, w3, w2)


def _combine_kernel(x_ref, gate_ref, fw_ref, wcol_ref, y1_ref, y2_ref, o_ref, *, final):
    w_first = wcol_ref[:, 0:1]
    w_second = wcol_ref[:, 1:2]
    y_first = _unpack_bf16_pairs(jnp.concatenate([y1_ref[k] for k in range(N_SLAB)], axis=-1))
    y_second = _unpack_bf16_pairs(jnp.concatenate([y2_ref[k] for k in range(N_SLAB)], axis=-1))
    moe = w_first * y_first + w_second * y_second
    x = x_ref[...] + gate_ref[...] * moe
    if final:
        x = x * lax.rsqrt(jnp.mean(x * x, axis=-1, keepdims=True) + EPS) * fw_ref[...]
    o_ref[...] = x


def _combine(x2, mod3, final_w_row, wcol, gathered, seq, final):
    t, d = x2.shape
    tm = TM_COMB
    tpb = seq // tm
    nblk = t // tm
    yspec = lambda off: pl.BlockSpec((N_SLAB, tm, SLAB), lambda i: (0, i + off, 0))
    return pl.pallas_call(
        functools.partial(_combine_kernel, final=final),
        grid=(nblk,),
        in_specs=[pl.BlockSpec((tm, d), lambda i: (i, 0)),
                  pl.BlockSpec((None, 1, d), lambda i: ((i // tpb) * 6 + 5, 0, 0)),
                  pl.BlockSpec((1, d), lambda i: (0, 0)),
                  pl.BlockSpec((tm, LANE), lambda i: (i, 0)),
                  yspec(0), yspec(nblk)],
        out_specs=pl.BlockSpec((tm, d), lambda i: (i, 0)),
        out_shape=jax.ShapeDtypeStruct((t, d), F32),
        compiler_params=_cparams(("parallel",)),
        name="moe_combine",
    )(x2, mod3, final_w_row, wcol, gathered, gathered)


def _moe(x2, mod3, final_w_row, w_route, b_route, tri_excl, w1, w3, w2, layer, seq, final):
    t, d = x2.shape
    h3, ids, wcol, counts = _router(x2, mod3, w_route, b_route, tri_excl, seq)
    cnt = counts[:, 0].astype(jnp.int32)
    ends = jnp.cumsum(cnt)
    offs = ends - cnt
    experts = jnp.arange(MOE_EXPERTS, dtype=jnp.int32)
    pick = lambda table, idx: jnp.sum(jnp.where(idx[:, None] == experts[None, :], table[None, :], 0), axis=1)
    slot1 = pick(offs, ids[0]) + ids[2]
    slot2 = pick(offs, ids[1]) + ids[3]
    n_tiles = 2 * t // TM_X
    first_tile = offs // TM_X
    n_vis = jnp.where(cnt > 0, (ends - 1) // TM_X - first_tile + 1, 0)
    cum = jnp.cumsum(n_vis)
    step = jnp.arange(n_tiles + MOE_EXPERTS, dtype=jnp.int32)
    step_expert = jnp.minimum(jnp.sum(step[:, None] >= cum[None, :], axis=1), MOE_EXPERTS - 1).astype(jnp.int32)
    valid = step < cum[-1]
    step_tile = jnp.where(valid, pick(first_tile - (cum - n_vis), step_expert) + step, n_tiles - 1)
    step_lo = jnp.where(valid, jnp.clip(pick(offs, step_expert) - step_tile * TM_X, 0, TM_X), 0)
    step_hi = jnp.where(valid, jnp.clip(pick(ends, step_expert) - step_tile * TM_X, 0, TM_X), 0)
    xs = _dispatch(slot1, slot2, h3)
    ys = _experts(step_tile.astype(jnp.int32), step_expert, step_lo.astype(jnp.int32), step_hi.astype(jnp.int32),
                  xs, w1, w3, w2, layer)
    n_sorted = ys.shape[1]
    gathered = _gather_rows(ys.reshape(N_SLAB * n_sorted, SLAB), _slab_rows(jnp.concatenate([slot1, slot2]), n_sorted))
    gathered = gathered.reshape(N_SLAB, n_sorted, SLAB)
    return _combine(x2, mod3, final_w_row, wcol, gathered, seq, final)


def kernel(x, c, positions, ada_w, ada_b, w_in, s5_lam_re, s5_lam_im, s5_b_re, s5_b_im, s5_c_re, s5_c_im, s5_d, s5_log_dt, s5_w_glu, hg_lb_logits, hg_norm_w, m2_conv_w, m2_conv_b, m2_dt_bias, m2_a_log, m2_d, m2_norm_w, w_branch, w_gate, b_gate, w_out, moe_w_group, moe_b_group, moe_w_expert, moe_b_expert, moe_w1, moe_w3, moe_w2, final_norm_w):
    bsz, seq, d = x.shape
    t = bsz * seq
    depth = ada_w.shape[0]
    assert seq % TM_PROJ == 0 and seq % C_RET == 0 and seq % C_SSD == 0 and seq % C_HG == 0
    x2 = x.reshape(t, d).astype(F32)

    c_pad = jnp.zeros((8, d), F32).at[:bsz].set(c.astype(F32))
    mod_all = _ada_mod(c_pad.T, ada_w.astype(F32), ada_b.astype(F32), bsz)

    half = RET_DK // 2
    inv_freq = ROPE_BASE ** (-jnp.arange(half, dtype=F32) / half)
    invf_col = jnp.broadcast_to(inv_freq[:, None], (half, LANE))
    expand = np.tile(np.eye(half, dtype=np.float32), (1, 2 * RET_HEADS))
    sign = np.tile(np.concatenate([-np.ones(half), np.ones(half)]), RET_HEADS)[None, :].astype(np.float32)
    cos_t, sin_t = _rope_tables(positions.reshape(1, t).astype(jnp.int32), invf_col,
                                jnp.asarray(expand, BF16), jnp.asarray(expand * sign, BF16))
    cos3 = cos_t.reshape(bsz, seq, -1)
    sin3 = sin_t.reshape(bsz, seq, -1)

    lb_cum = jnp.cumsum(jax.nn.softmax(hg_lb_logits.astype(F32), axis=0), axis=0)
    hg_lb = lb_cum - lb_cum[:1]
    tri_ssd = jnp.asarray(np.tril(np.ones((C_SSD, C_SSD), np.float32)), BF16)
    tri_excl = jnp.asarray(np.triu(np.ones((LANE, LANE), np.float32), 1), BF16)
    final_w_row = final_norm_w.astype(F32)[None, :]
    n_main = IN_W // LANE * LANE
    w_main = w_in[:, :, :n_main].astype(BF16)
    w_tail = jnp.zeros((depth, d, IN_W_PAD - n_main), BF16).at[:, :, :IN_W - n_main].set(w_in[:, :, n_main:].astype(BF16))
    w_glu_bf = s5_w_glu.astype(BF16)
    w_branch_bf = w_branch.astype(BF16)
    w_gate_bf = w_gate.astype(BF16)
    w_out_bf = w_out.astype(BF16)
    b_gate3 = b_gate.astype(F32).reshape(depth, 1, -1)
    moe_w1_all = moe_w1.astype(F32).reshape(depth * MOE_EXPERTS, d, MOE_FF)
    moe_w3_all = moe_w3.astype(F32).reshape(depth * MOE_EXPERTS, d, MOE_FF)
    moe_w2_all = moe_w2.astype(F32).reshape(depth * MOE_EXPERTS, MOE_FF, d)

    for layer in range(depth):
        mod3 = mod_all[layer, :bsz].reshape(bsz * 6, 1, d)
        p, u_s5 = _in_proj(x2, mod3, w_main, w_tail, layer, seq)
        p3 = p.reshape(bsz, seq, IN_W_PAD)

        ops = _s5_operators(s5_lam_re[layer], s5_lam_im[layer], s5_b_re[layer], s5_b_im[layer],
                            s5_c_re[layer], s5_c_im[layer], s5_d[layer], s5_log_dt[layer])
        y_s5 = _s5_scan(u_s5.reshape(bsz, seq, BRANCH_W), *ops).reshape(t, BRANCH_W)

        lb = hg_lb[layer][None, :]
        y_hg = _hgrn2(p3, jnp.log(lb), jnp.log1p(-lb), hg_norm_w[layer].astype(F32)[None, :]).reshape(t, BRANCH_W)

        y_ret = _retention(p3, cos3, sin3).reshape(t, BRANCH_W)

        pad8 = lambda v: jnp.zeros((1, LANE), F32).at[0, :M2_HEADS].set(v.astype(F32))
        y_m2 = _ssd(p3, tri_ssd, m2_conv_w[layer].astype(F32), m2_conv_b[layer].astype(F32)[None, :],
                    pad8(m2_dt_bias[layer]), pad8(m2_a_log[layer]),
                    jnp.repeat(m2_d[layer].astype(F32), M2_HEADDIM)[None, :],
                    m2_norm_w[layer].astype(F32)[None, :]).reshape(t, BRANCH_W)

        nr = 40
        w_route = jnp.zeros((nr, d), F32).at[:MOE_GROUPS].set(moe_w_group[layer].astype(F32).T)
        w_route = w_route.at[MOE_GROUPS:MOE_GROUPS + MOE_EXPERTS].set(moe_w_expert[layer].astype(F32).T)
        b_route = jnp.zeros((nr, LANE), F32).at[:MOE_GROUPS, 0].set(moe_b_group[layer].astype(F32))
        b_route = b_route.at[MOE_GROUPS:MOE_GROUPS + MOE_EXPERTS, 0].set(moe_b_expert[layer].astype(F32))
        x2 = _merge(x2, mod3, y_s5, y_hg, y_ret, y_m2, w_glu_bf, w_branch_bf, w_gate_bf, b_gate3, w_out_bf,
                    layer, seq)
        x2 = _moe(x2, mod3, final_w_row, w_route, b_route, tri_excl, moe_w1_all, moe_w3_all, moe_w2_all,
                  layer, seq, final=(layer == depth - 1))
    return x2.reshape(bsz, seq, d)
```

```python
import functools
import math

import numpy as np
import jax
import jax.numpy as jnp
from jax import lax
from jax.experimental import pallas as pl
from jax.experimental.pallas import tpu as pltpu
from jax.experimental.pallas import tpu_sc as plsc

F32 = jnp.float32
BF16 = jnp.bfloat16
HIGHEST = lax.Precision.HIGHEST

D_MODEL = 1024
BRANCH_W = 512
EPS = 1e-6
S5_GROUPS = 32
S5_CH = 16
S5_STATE = 64
S5_MAX_REAL = -1e-4
S5_BLOCK = 8
S5_SEQ_PER_STEP = 2
HG_HEADS = 4
HG_DK = 128
RET_HEADS = 4
RET_DK = 64
RET_DV = 128
ROPE_BASE = 10000.0
M2_HEADS = 8
M2_HEADDIM = 64
M2_GROUPS = 2
M2_STATE = 128
M2_CONV = 4
MOE_GROUPS = 4
MOE_EPG = 8
MOE_EXPERTS = MOE_GROUPS * MOE_EPG
MOE_FF = 256

COL_S5, COL_HQ, COL_HF, COL_HI, COL_HG = 0, 512, 1024, 1536, 2048
COL_RQ, COL_RK, COL_RV, COL_RG = 2560, 2816, 3072, 3584
COL_MZ, COL_MXS, COL_MBC, COL_MDT = 4096, 4608, 5120, 5632
IN_W = 5640
IN_W_PAD = 5760

LANE = 128
VMEM_LIMIT = 56 * 1024 * 1024

TM_PROJ = 1024
TN_PROJ = 1024
TM_INPROJ = 512
LOG2_E = 1.4426950408889634
C_RET = 512
C_SSD = 256
C_HG = 128
TM_X = 512
X_SUB = 128
TM_COMB = 1024
SC_WINDOW = 128
SLAB = 256
N_SLAB = D_MODEL // 2 // SLAB


def _cparams(sem):
    return pltpu.CompilerParams(dimension_semantics=sem, vmem_limit_bytes=VMEM_LIMIT)


def _silu(v):
    return v * jax.nn.sigmoid(v)


def _dot_nt(a, b, **kw):
    return lax.dot_general(a, b, (((1,), (1,)), ((), ())), preferred_element_type=F32, **kw)


def _dot_tn(a, b, **kw):
    return lax.dot_general(a, b, (((0,), (0,)), ((), ())), preferred_element_type=F32, **kw)


def _ada_kernel(ct_ref, w_ref, b_ref, o_ref, *, n_rows):
    cond_t = _silu(ct_ref[...])
    w = w_ref[...]
    rows = [jnp.sum(w * cond_t[:, b:b + 1], axis=0, keepdims=True) for b in range(n_rows)]
    rows += [jnp.zeros_like(rows[0])] * (cond_t.shape[1] - n_rows)
    o_ref[...] = jnp.concatenate(rows, axis=0) + b_ref[...]


def _ada_mod(c_pad_t, ada_w, ada_b, n_rows):
    depth, d, n = ada_w.shape
    tn = 1536
    return pl.pallas_call(
        functools.partial(_ada_kernel, n_rows=n_rows),
        grid=(depth, n // tn),
        in_specs=[pl.BlockSpec((d, 8), lambda l, j: (0, 0)),
                  pl.BlockSpec((None, d, tn), lambda l, j: (l, 0, j)),
                  pl.BlockSpec((None, 1, tn), lambda l, j: (l, 0, j))],
        out_specs=pl.BlockSpec((None, 8, tn), lambda l, j: (l, 0, j)),
        out_shape=jax.ShapeDtypeStruct((depth, 8, n), F32),
        compiler_params=_cparams(("parallel", "parallel")),
        name="ada_mod",
    )(c_pad_t, ada_w, ada_b.reshape(depth, 1, n))


def _pack_bf16_pairs(x):
    n = x.shape[1] // 2
    lo = pltpu.bitcast(x[:, :n].astype(BF16).astype(F32), jnp.uint32) >> 16
    hi = pltpu.bitcast(x[:, n:].astype(BF16).astype(F32), jnp.uint32)
    return hi | lo


def _unpack_bf16_pairs(w):
    lo = pltpu.bitcast(w << 16, F32)
    hi = pltpu.bitcast(w & jnp.uint32(0xFFFF0000), F32)
    return jnp.concatenate([lo, hi], axis=-1)


def _modulated_norm(x, scale, shift):
    ms = jnp.mean(x * x, axis=-1, keepdims=True)
    return x * lax.rsqrt(ms + EPS) * (1.0 + scale) + shift


def _inproj_kernel(x_ref, sc_ref, sh_ref, w_ref, wtail_ref, o_ref, u_ref):
    h = _modulated_norm(x_ref[...], sc_ref[...], sh_ref[...]).astype(BF16)
    n_main = w_ref.shape[1]
    for n0 in range(0, n_main, TN_PROJ):
        n1 = min(n0 + TN_PROJ, n_main)
        p = jnp.dot(h, w_ref[:, n0:n1], preferred_element_type=F32)
        o_ref[:, n0:n1] = p.astype(o_ref.dtype)
        if n0 == 0:
            u_ref[...] = p[:, COL_S5:COL_S5 + BRANCH_W]
    o_ref[:, n_main:] = jnp.dot(h, wtail_ref[...], preferred_element_type=F32).astype(o_ref.dtype)


def _in_proj(x2, mod3, w_main, w_tail, layer, seq):
    t, d = x2.shape
    tm = TM_INPROJ
    tpb = seq // tm
    n_main = w_main.shape[2]
    assert COL_S5 + BRANCH_W <= TN_PROJ and n_main + w_tail.shape[2] == IN_W_PAD
    return pl.pallas_call(
        _inproj_kernel,
        grid=(t // tm,),
        in_specs=[pl.BlockSpec((tm, d), lambda i: (i, 0)),
                  pl.BlockSpec((None, 1, d), lambda i: ((i // tpb) * 6 + 1, 0, 0)),
                  pl.BlockSpec((None, 1, d), lambda i: ((i // tpb) * 6 + 0, 0, 0)),
                  pl.BlockSpec((None, d, n_main), lambda i: (layer, 0, 0), pipeline_mode=pl.Buffered(1)),
                  pl.BlockSpec((None, d, IN_W_PAD - n_main), lambda i: (layer, 0, 0), pipeline_mode=pl.Buffered(1))],
        out_specs=[pl.BlockSpec((tm, IN_W_PAD), lambda i: (i, 0)),
                   pl.BlockSpec((tm, BRANCH_W), lambda i: (i, 0))],
        out_shape=[jax.ShapeDtypeStruct((t, IN_W_PAD), BF16), jax.ShapeDtypeStruct((t, BRANCH_W), F32)],
        compiler_params=_cparams(("parallel",)),
        name="in_proj",
    )(x2, mod3, mod3, w_main, w_tail)


def _rope_kernel(pos_ref, invf_ref, ecos_ref, esin_ref, cos_ref, sin_ref):
    ang = invf_ref[:, 0:1] * pos_ref[...].astype(F32)
    def spread(values, e_ref):
        hi = values.astype(BF16)
        rest = values - hi.astype(F32)
        mid = rest.astype(BF16)
        lo = (rest - mid.astype(F32)).astype(BF16)
        e = e_ref[...]
        return _dot_tn(hi, e) + _dot_tn(mid, e) + _dot_tn(lo, e)

    cos_ref[...] = spread(jnp.cos(ang), ecos_ref)
    sin_ref[...] = spread(jnp.sin(ang), esin_ref)


def _rope_tables(pos_row, invf_col, expand_cos, expand_sin):
    t = pos_row.shape[1]
    half, w = expand_cos.shape
    tm = 1024
    const = lambda shape: pl.BlockSpec(shape, lambda i: (0, 0))
    return pl.pallas_call(
        _rope_kernel,
        grid=(t // tm,),
        in_specs=[pl.BlockSpec((1, tm), lambda i: (0, i)), const((half, LANE)), const((half, w)), const((half, w))],
        out_specs=[pl.BlockSpec((tm, w), lambda i: (i, 0))] * 2,
        out_shape=[jax.ShapeDtypeStruct((t, w), F32)] * 2,
        compiler_params=_cparams(("parallel",)),
        name="rope_tables",
    )(pos_row, invf_col, expand_cos, expand_sin)


def _ret_kernel(q_ref, k_ref, v_ref, g_ref, cos_ref, sin_ref, o_ref, st_ref, dec_ref, *, chunk):
    @pl.when(pl.program_id(1) == 0)
    def _():
        st_ref[...] = jnp.zeros_like(st_ref)
        ti = lax.broadcasted_iota(jnp.int32, (chunk, chunk), 0)
        si = lax.broadcasted_iota(jnp.int32, (chunk, chunk), 1)
        lag = (ti - si).astype(F32)
        for h in range(RET_HEADS):
            log_gamma = math.log1p(-(2.0 ** (-5.0 - h)))
            dec_ref[h] = jnp.where(ti >= si, jnp.exp(jnp.minimum(lag * log_gamma, 0.0)), 0.0)

    cosf = cos_ref[...]
    sinf = sin_ref[...]
    width = RET_HEADS * RET_DK
    lane = lax.broadcasted_iota(jnp.int32, (chunk, width), 1)
    first_half = (lane % RET_DK) < (RET_DK // 2)

    def rope(t):
        partner = jnp.where(first_half, pltpu.roll(t, width - RET_DK // 2, 1), pltpu.roll(t, RET_DK // 2, 1))
        return t * cosf + partner * sinf

    q = rope(q_ref[...].astype(F32))
    k = rope(k_ref[...].astype(F32)) * (RET_DK ** -0.5)
    v = v_ref[...]
    g = g_ref[...].astype(F32)
    tcol = lax.broadcasted_iota(jnp.int32, (chunk, 1), 0).astype(F32)
    for h in range(RET_HEADS):
        log_gamma = math.log1p(-(2.0 ** (-5.0 - h)))
        qh = q[:, h * RET_DK:(h + 1) * RET_DK]
        kh = k[:, h * RET_DK:(h + 1) * RET_DK]
        vh = v[:, h * RET_DV:(h + 1) * RET_DV].astype(BF16)
        scores = _dot_nt(qh.astype(BF16), kh.astype(BF16)) * dec_ref[h]
        state = st_ref[h]
        q_in = qh * jnp.exp(log_gamma * (tcol + 1.0))
        o = (jnp.dot(scores.astype(BF16), vh, preferred_element_type=F32)
             + jnp.dot(q_in.astype(BF16), state.astype(BF16), preferred_element_type=F32))
        k_out = kh * jnp.exp(log_gamma * (chunk - 1.0 - tcol))
        st_ref[h] = math.exp(log_gamma * chunk) * state + _dot_tn(k_out.astype(BF16), vh)
        o = o * lax.rsqrt(jnp.mean(o * o, axis=-1, keepdims=True) + EPS)
        gh = g[:, h * RET_DV:(h + 1) * RET_DV]
        o_ref[:, h * RET_DV:(h + 1) * RET_DV] = (o * _silu(gh)).astype(o_ref.dtype)


def _retention(p3, cos3, sin3):
    b, seq, _ = p3.shape
    c = C_RET
    qk_w = RET_HEADS * RET_DK
    return pl.pallas_call(
        functools.partial(_ret_kernel, chunk=c),
        grid=(b, seq // c),
        in_specs=[pl.BlockSpec((None, c, qk_w), lambda i, j: (i, j, COL_RQ // qk_w)),
                  pl.BlockSpec((None, c, qk_w), lambda i, j: (i, j, COL_RK // qk_w)),
                  pl.BlockSpec((None, c, BRANCH_W), lambda i, j: (i, j, COL_RV // BRANCH_W)),
                  pl.BlockSpec((None, c, BRANCH_W), lambda i, j: (i, j, COL_RG // BRANCH_W)),
                  pl.BlockSpec((None, c, qk_w), lambda i, j: (i, j, 0)),
                  pl.BlockSpec((None, c, qk_w), lambda i, j: (i, j, 0))],
        out_specs=pl.BlockSpec((None, c, BRANCH_W), lambda i, j: (i, j, 0)),
        out_shape=jax.ShapeDtypeStruct((b, seq, BRANCH_W), BF16),
        scratch_shapes=[pltpu.VMEM((RET_HEADS, RET_DK, RET_DV), F32), pltpu.VMEM((RET_HEADS, c, c), F32)],
        compiler_params=_cparams(("parallel", "arbitrary")),
        name="retention",
    )(p3, p3, p3, p3, cos3, sin3)


def _ssd_kernel(z_ref, xs_ref, bc_ref, dt_ref, tri_ref, cw_ref, cb_ref, dtb_ref, alog_ref, dsk_ref, nw_ref,
                o_ref, xe_scr, st_ref, *, chunk):
    j = pl.program_id(1)
    width = 2 * BRANCH_W

    @pl.when(j == 0)
    def _():
        st_ref[...] = jnp.zeros_like(st_ref)
        xe_scr[0:8, :] = jnp.zeros((8, width), F32)

    @pl.when(j > 0)
    def _():
        xe_scr[0:8, :] = xe_scr[chunk:chunk + 8, :]

    xe_scr[8:, 0:BRANCH_W] = xs_ref[...].astype(F32)
    xe_scr[8:, BRANCH_W:] = bc_ref[...].astype(F32)
    conv = cb_ref[...] + cw_ref[M2_CONV - 1:M2_CONV, :] * xe_scr[8:, :]
    for tap in range(M2_CONV - 1):
        conv = conv + cw_ref[tap:tap + 1, :] * xe_scr[pl.ds(8 - (M2_CONV - 1) + tap, chunk), :]
    conv = _silu(conv)
    xs = conv[:, :BRANCH_W]
    bm = conv[:, BRANCH_W:BRANCH_W + M2_GROUPS * M2_STATE]
    cm = conv[:, BRANCH_W + M2_GROUPS * M2_STATE:]

    dt = jax.nn.softplus(dt_ref[...].astype(F32) + dtb_ref[...])
    da = dt * (-jnp.exp(alog_ref[...]))
    da_hi = da.astype(BF16)
    da_r = da - da_hi.astype(F32)
    da_mid = da_r.astype(BF16)
    da_lo = (da_r - da_mid.astype(F32)).astype(BF16)
    tri = tri_ref[...]
    a_cs = (jnp.dot(tri, da_hi, preferred_element_type=F32) + jnp.dot(tri, da_mid, preferred_element_type=F32)
            + jnp.dot(tri, da_lo, preferred_element_type=F32))
    a_cs = a_cs * LOG2_E
    a_cs_t = a_cs.T
    ti = lax.broadcasted_iota(jnp.int32, (chunk, chunk), 0)
    si = lax.broadcasted_iota(jnp.int32, (chunk, chunk), 1)
    causal = ti >= si
    hpg = M2_HEADS // M2_GROUPS
    pair_w = 2 * M2_HEADDIM
    upper = lax.broadcasted_iota(jnp.int32, (chunk, pair_w), 1) >= M2_HEADDIM
    upper_state = lax.broadcasted_iota(jnp.int32, (M2_STATE, pair_w), 1) >= M2_HEADDIM
    ys = []
    for grp in range(M2_GROUPS):
        bm_g = bm[:, grp * M2_STATE:(grp + 1) * M2_STATE]
        cm_g = cm[:, grp * M2_STATE:(grp + 1) * M2_STATE]
        bm_bf = bm_g.astype(BF16)
        cb = _dot_nt(cm_g.astype(BF16), bm_bf)
        for pp in range(hpg // 2):
            h0 = grp * hpg + 2 * pp
            pair = h0 // 2
            xd = xs[:, pair * pair_w:(pair + 1) * pair_w] * jnp.where(upper, dt[:, h0 + 1:h0 + 2], dt[:, h0:h0 + 1])
            xd_bf = xd.astype(BF16)
            state = st_ref[pair]
            state_bf = state.astype(BF16)
            y_heads, state_heads = [], []
            for h in (h0, h0 + 1):
                col = a_cs[:, h:h + 1]
                row = a_cs_t[h:h + 1, :]
                lmat = jnp.where(causal, jnp.exp2(col - row), 0.0)
                y_heads.append(jnp.dot((cb * lmat).astype(BF16), xd_bf, preferred_element_type=F32)
                               + jnp.dot((cm_g * jnp.exp2(col)).astype(BF16), state_bf, preferred_element_type=F32))
                a_last = a_cs[chunk - 1:chunk, h:h + 1]
                to_end = jnp.exp2(a_last - col)
                state_heads.append(jnp.exp2(a_last) * state + _dot_tn(bm_bf, (xd * to_end).astype(BF16)))
            ys.append(jnp.where(upper, y_heads[1], y_heads[0]))
            st_ref[pair] = jnp.where(upper_state, state_heads[1], state_heads[0])
    y = jnp.concatenate(ys, axis=-1) + dsk_ref[...] * xs
    y = y * _silu(z_ref[...].astype(F32))
    o_ref[...] = (y * lax.rsqrt(jnp.mean(y * y, axis=-1, keepdims=True) + EPS) * nw_ref[...]).astype(o_ref.dtype)


def _ssd(p3, tri, conv_w, conv_b, dt_bias_row, a_log_row, d_skip_row, norm_w_row):
    b, seq, _ = p3.shape
    c = C_SSD
    const = lambda shape: pl.BlockSpec(shape, lambda i, j: (0,) * len(shape))
    return pl.pallas_call(
        functools.partial(_ssd_kernel, chunk=c),
        grid=(b, seq // c),
        in_specs=[pl.BlockSpec((None, c, BRANCH_W), lambda i, j: (i, j, COL_MZ // BRANCH_W)),
                  pl.BlockSpec((None, c, BRANCH_W), lambda i, j: (i, j, COL_MXS // BRANCH_W)),
                  pl.BlockSpec((None, c, BRANCH_W), lambda i, j: (i, j, COL_MBC // BRANCH_W)),
                  pl.BlockSpec((None, c, LANE), lambda i, j: (i, j, COL_MDT // LANE)),
                  const((c, c)), const((M2_CONV, 2 * BRANCH_W)), const((1, 2 * BRANCH_W)),
                  const((1, LANE)), const((1, LANE)), const((1, BRANCH_W)), const((1, BRANCH_W))],
        out_specs=pl.BlockSpec((None, c, BRANCH_W), lambda i, j: (i, j, 0)),
        out_shape=jax.ShapeDtypeStruct((b, seq, BRANCH_W), BF16),
        scratch_shapes=[pltpu.VMEM((c + 8, 2 * BRANCH_W), F32),
                        pltpu.VMEM((M2_HEADS // 2, M2_STATE, 2 * M2_HEADDIM), F32)],
        compiler_params=_cparams(("parallel", "arbitrary")),
        name="ssd",
    )(p3, p3, p3, p3, tri, conv_w, conv_b, dt_bias_row, a_log_row, d_skip_row, norm_w_row)


def _hg_tables(chunk):
    n_lev = int(math.log2(chunk))
    r = np.arange(chunk)[:, None]
    jj = np.arange(chunk)[None, :]
    tri = (jj <= r).astype(np.float32)
    x = r ^ jj
    levmap = np.where(r > jj, np.floor(np.log2(x + 0.5)), np.where(r == jj, -1, -2)).astype(np.int32)
    return tri, levmap, n_lev


def _hg_level_exponent(b, lev):
    rows, width = b.shape
    m = 1 << lev
    sub = 8
    if 2 * m >= sub:
        blocks = b.reshape(rows // (2 * m), 2 * m, width)
        mid = jnp.broadcast_to(blocks[:, m - 1:m, :], blocks.shape).reshape(rows, width)
    else:
        groups = b.reshape(rows // sub, sub, width)
        row_in_group = lax.broadcasted_iota(jnp.int32, groups.shape, 1)
        mid = None
        for start in range(0, sub, 2 * m):
            picked = jnp.broadcast_to(groups[:, start + m - 1:start + m, :], groups.shape)
            mid = picked if mid is None else jnp.where(row_in_group >= start, picked, mid)
        mid = mid.reshape(rows, width)
    return -jnp.abs(b - mid)


def _hg_kernel(q_ref, f_ref, i_ref, g_ref, sum_ref, lev_ref, llb_ref, l1m_ref, nw_ref, o_ref, st_ref,
               *, chunk, n_lev):
    @pl.when(pl.program_id(1) == 0)
    def _():
        st_ref[...] = jnp.zeros_like(st_ref)

    f = f_ref[...].astype(F32)
    y = jnp.exp(-jnp.abs(f))
    one_plus_y = 1.0 + y
    log_sig = jnp.minimum(f, 0.0) - jnp.log(one_plus_y)
    a = llb_ref[...]
    bb = l1m_ref[...] + log_sig
    log_f = jnp.maximum(a, bb) + jnp.log(1.0 + jnp.exp(-jnp.abs(a - bb)))
    k_all = jnp.exp(l1m_ref[...]) * (jnp.where(f >= 0.0, y, 1.0) / one_plus_y)
    q_all = _silu(q_ref[...].astype(F32))
    hi = log_f.astype(BF16)
    r1 = log_f - hi.astype(F32)
    mid = r1.astype(BF16)
    lo = (r1 - mid.astype(F32)).astype(BF16)
    tri = sum_ref[...]
    b_all = (jnp.dot(tri, hi, preferred_element_type=F32)
             + jnp.dot(tri, mid, preferred_element_type=F32)
             + jnp.dot(tri, lo, preferred_element_type=F32))
    b_all = b_all * LOG2_E
    to_end_all = b_all[chunk - 1:chunk, :] - b_all
    level_decay = [jnp.exp2(_hg_level_exponent(b_all, lev)) for lev in range(n_lev)]
    levmap = lev_ref[...]
    on_diag = levmap == -1
    on_level = [levmap == lev for lev in range(n_lev)]
    v_all = i_ref[...]
    g_all = g_ref[...].astype(F32)
    for h in range(HG_HEADS):
        sl = slice(h * HG_DK, (h + 1) * HG_DK)
        qh = q_all[:, sl]
        kh = k_all[:, sl]
        vh = v_all[:, sl].astype(BF16)
        b_h = b_all[:, sl]
        to_end = to_end_all[:, sl]
        amat = jnp.where(on_diag, _dot_nt(qh.astype(BF16), kh.astype(BF16)), 0.0)
        for lev in range(n_lev):
            e = level_decay[lev][:, sl]
            a_l = _dot_nt((qh * e).astype(BF16), (kh * e).astype(BF16))
            amat = jnp.where(on_level[lev], a_l, amat)
        state_t = st_ref[h]
        o = (jnp.dot(amat.astype(BF16), vh, preferred_element_type=F32)
             + _dot_nt((qh * jnp.exp2(b_h)).astype(BF16), state_t.astype(BF16)))
        k_end = kh * jnp.exp2(to_end)
        st_ref[h] = jnp.exp2(b_h[chunk - 1:chunk, :]) * state_t + _dot_tn(vh, k_end.astype(BF16))
        o = o * lax.rsqrt(jnp.mean(o * o, axis=-1, keepdims=True) + EPS) * nw_ref[...]
        o_ref[:, sl] = (o * _silu(g_all[:, sl])).astype(o_ref.dtype)


def _hgrn2(p3, log_lb_row, log1m_lb_row, norm_w_row):
    b, seq, _ = p3.shape
    c = C_HG
    tri, levmap, n_lev = _hg_tables(c)
    const = lambda shape: pl.BlockSpec(shape, lambda i, j: (0,) * len(shape))
    blk = lambda col: pl.BlockSpec((None, c, BRANCH_W), lambda i, j: (i, j, col // BRANCH_W))
    return pl.pallas_call(
        functools.partial(_hg_kernel, chunk=c, n_lev=n_lev),
        grid=(b, seq // c),
        in_specs=[blk(COL_HQ), blk(COL_HF), blk(COL_HI), blk(COL_HG),
                  const((c, c)), const((c, c)),
                  const((1, BRANCH_W)), const((1, BRANCH_W)), const((1, HG_DK))],
        out_specs=pl.BlockSpec((None, c, BRANCH_W), lambda i, j: (i, j, 0)),
        out_shape=jax.ShapeDtypeStruct((b, seq, BRANCH_W), BF16),
        scratch_shapes=[pltpu.VMEM((HG_HEADS, HG_DK, HG_DK), F32)],
        compiler_params=_cparams(("parallel", "arbitrary")),
        name="hgrn2",
    )(p3, p3, p3, p3, jnp.asarray(tri, BF16), jnp.asarray(levmap), log_lb_row, log1m_lb_row, norm_w_row)


def _expand_block_diag(comp_ref, e_ref, dst_ref, row_div, lane_div, causal=False):
    gq = LANE // S5_CH
    rows, ncols = dst_ref.shape
    step = 512
    for c0 in range(0, ncols, step):
        r1 = min(rows, c0 + step) if causal else rows
        row_grp = (lax.broadcasted_iota(jnp.int32, (r1, step), 0) // row_div) % gq
        lane_grp = ((lax.broadcasted_iota(jnp.int32, (r1, step), 1) + c0) // lane_div) % gq
        full = jnp.dot(comp_ref[0:r1, :], e_ref[:, c0:c0 + step], preferred_element_type=F32)
        dst_ref[0:r1, c0:c0 + step] = jnp.where(row_grp == lane_grp, full, 0.0).astype(dst_ref.dtype)


def _s5_kernel(u_ref, k2_ref, bc_ref, cc_ref, esc_ref, lam_ref, o_ref, tc_scr, tq_ref, bqt_ref, cq_ref,
               x_scr, w_scr, s_scr, *, rows):
    nb = S5_BLOCK

    @pl.when(pl.program_id(1) == 0)
    def _():
        k2 = k2_ref[...]
        lane = lax.broadcasted_iota(jnp.int32, k2.shape, 1)
        for t in range(nb):
            shifted = k2 if t == 0 else jnp.where(lane >= t * S5_CH, pltpu.roll(k2, t * S5_CH, 1), 0.0)
            tc_scr[t * LANE:(t + 1) * LANE, :] = shifted.astype(tc_scr.dtype)
        _expand_block_diag(tc_scr, esc_ref, tq_ref, S5_CH, S5_CH, causal=True)
        _expand_block_diag(bc_ref, esc_ref, bqt_ref, S5_STATE, S5_CH)
        _expand_block_diag(cc_ref, esc_ref, cq_ref, S5_STATE, S5_CH)

    n_seq = u_ref.shape[0]
    for b in range(n_seq):
        for t in range(nb):
            x_scr[b * rows:(b + 1) * rows, t * LANE:(t + 1) * LANE] = (
                u_ref[b, pl.ds(t, rows, stride=nb), :].astype(x_scr.dtype))
    x = x_scr[...]
    half = w_scr.shape[1] // 2
    w_scr[...] = _dot_nt(x, bqt_ref[...])
    lam_re = lam_ref[0:1, :]
    lam_im = lam_ref[1:2, :]

    def body(j, carry):
        out = []
        for b in range(n_seq):
            s_re, s_im = carry[2 * b], carry[2 * b + 1]
            r = b * rows + j
            s_scr[pl.ds(r, 1), 0:half] = s_re
            s_scr[pl.ds(r, 1), half:] = s_im
            w_re = w_scr[pl.ds(r, 1), 0:half]
            w_im = w_scr[pl.ds(r, 1), half:]
            out += [lam_re * s_re - lam_im * s_im + w_re, lam_re * s_im + lam_im * s_re + w_im]
        return tuple(out)

    zero = jnp.zeros((1, half), F32)
    lax.fori_loop(0, rows, body, (zero,) * (2 * n_seq), unroll=4)
    s_bf = s_scr[...].astype(BF16)
    pair = 2 * LANE
    for c0 in range(0, nb * LANE, pair):
        k_rows = c0 + pair
        y = (jnp.dot(x[:, :k_rows], tq_ref[0:k_rows, c0:c0 + pair], preferred_element_type=F32)
             + jnp.dot(s_bf, cq_ref[:, c0:c0 + pair], preferred_element_type=F32))
        for b in range(n_seq):
            for t in range(c0 // LANE, (c0 + pair) // LANE):
                o_ref[b, pl.ds(t, rows, stride=nb), :] = y[b * rows:(b + 1) * rows, t * LANE - c0:(t + 1) * LANE - c0]


def _s5_scan(p3, k2, bc, cc, lam16):
    batch, seq, _ = p3.shape
    nb = S5_BLOCK
    nq = BRANCH_W // LANE
    rows = seq // nb
    kdim = nb * LANE
    gq = LANE // S5_CH
    ncol = 2 * gq * S5_STATE
    e_sc = (np.eye(nb)[:, None, :, None, None] * np.eye(S5_CH)[None, :, None, None, :] * np.ones((1, 1, 1, gq, 1)))
    e_sc = e_sc.reshape(nb * S5_CH, nb * gq * S5_CH)
    full = lambda shape: pl.BlockSpec(shape, lambda q, b: (0,) * len(shape))
    per_q = lambda r, c: pl.BlockSpec((None, r, c), lambda q, b: (q, 0, 0))
    n_seq = S5_SEQ_PER_STEP if batch % S5_SEQ_PER_STEP == 0 else 1
    return pl.pallas_call(
        functools.partial(_s5_kernel, rows=rows),
        grid=(nq, batch // n_seq),
        in_specs=[pl.BlockSpec((n_seq, seq, LANE), lambda q, b: (b, 0, q)),
                  per_q(LANE, nb * S5_CH), per_q(ncol, nb * S5_CH), per_q(ncol, nb * S5_CH),
                  full(e_sc.shape), per_q(2, ncol // 2)],
        out_specs=pl.BlockSpec((n_seq, seq, LANE), lambda q, b: (b, 0, q)),
        out_shape=jax.ShapeDtypeStruct((batch, seq, BRANCH_W), F32),
        scratch_shapes=[pltpu.VMEM((kdim, nb * S5_CH), BF16),
                        pltpu.VMEM((kdim, kdim), BF16), pltpu.VMEM((ncol, kdim), BF16), pltpu.VMEM((ncol, kdim), BF16),
                        pltpu.VMEM((n_seq * rows, kdim), BF16), pltpu.VMEM((n_seq * rows, ncol), F32),
                        pltpu.VMEM((n_seq * rows, ncol), F32)],
        compiler_params=_cparams(("parallel", "arbitrary")),
        name="s5_scan",
    )(p3, k2, bc, cc, jnp.asarray(e_sc, BF16), lam16)


def _s5_operators(lam_re, lam_im, b_re, b_im, c_re, c_im, d_skip, log_dt):
    nb = S5_BLOCK
    gq = LANE // S5_CH
    nq = S5_GROUPS // gq
    lam = lax.complex(jnp.minimum(lam_re.astype(F32), S5_MAX_REAL), lam_im.astype(F32))
    step = jnp.exp(log_dt.astype(F32))[:, None]
    z = lam * step
    lam_bar = jnp.exp(z)
    b_bar = ((lam_bar - 1.0) / lam)[..., None] * lax.complex(b_re.astype(F32), b_im.astype(F32))
    c_mat = lax.complex(c_re.astype(F32), c_im.astype(F32))
    pw = jnp.exp(z[..., None] * jnp.arange(nb + 1, dtype=F32))
    cp = c_mat[:, None, :, :] * pw[..., :nb].transpose(0, 2, 1)[:, :, None, :]
    cp = jnp.concatenate([cp.real, -cp.imag], axis=-1).reshape(S5_GROUPS, nb * S5_CH, 2 * S5_STATE)
    bri = jnp.concatenate([b_bar.real, b_bar.imag], axis=1)
    kern = jnp.einsum('gnk,gki->gin', cp, bri, precision=HIGHEST)
    skip = (jnp.asarray(np.concatenate([np.eye(S5_CH), np.zeros((S5_CH, (nb - 1) * S5_CH))], axis=1), F32)[None]
            * d_skip.astype(F32).reshape(S5_GROUPS, S5_CH, 1))
    k2 = (kern + skip).reshape(nq, gq * S5_CH, nb * S5_CH)
    pw_rev = jnp.exp(z[..., None] * jnp.asarray(np.arange(nb - 1, -1, -1), F32))
    binc = pw_rev[:, :, :, None] * b_bar[:, :, None, :]
    binc = jnp.stack([binc.real, binc.imag], axis=0).reshape(2, nq, gq * S5_STATE, nb * S5_CH)
    bc = binc.transpose(1, 0, 2, 3).reshape(nq, 2 * gq * S5_STATE, nb * S5_CH)
    cm = c_mat.transpose(0, 2, 1)[:, :, None, :] * pw[..., 1:][:, :, :, None]
    cm = jnp.stack([cm.real, -cm.imag], axis=0).reshape(2, nq, gq * S5_STATE, nb * S5_CH)
    cc = cm.transpose(1, 0, 2, 3).reshape(nq, 2 * gq * S5_STATE, nb * S5_CH)
    lam_n = pw[..., nb].reshape(nq, gq * S5_STATE)
    lam16 = jnp.stack([lam_n.real, lam_n.imag], axis=1)
    return k2, bc.astype(BF16), cc.astype(BF16), lam16


def _merge_kernel(x_ref, sc_ref, sh_ref, gm_ref, ys5_ref, yhg_ref, yret_ref, ym2_ref,
                  wglu_ref, wbr_ref, wg_ref, bg_ref, wout_ref, o_ref):
    x = x_ref[...]
    d = x.shape[1]
    h = _modulated_norm(x, sc_ref[...], sh_ref[...]).astype(BF16)
    y_s5 = jax.nn.gelu(ys5_ref[...])
    y_s5 = y_s5 * jax.nn.sigmoid(jnp.dot(y_s5.astype(BF16), wglu_ref[...], preferred_element_type=F32))
    acc = jnp.zeros(x.shape, F32)
    for n, y in enumerate((y_s5, yhg_ref[...], yret_ref[...], ym2_ref[...])):
        gate = jax.nn.sigmoid(jnp.dot(h, wg_ref[:, n * d:(n + 1) * d], preferred_element_type=F32)
                              + bg_ref[:, n * d:(n + 1) * d])
        acc = acc + gate * jnp.dot(y.astype(BF16), wbr_ref[n], preferred_element_type=F32)
    o_ref[...] = x + gm_ref[...] * jnp.dot(acc.astype(BF16), wout_ref[...], preferred_element_type=F32)


def _merge(x2, mod3, ys5, yhg, yret, ym2, w_glu, w_branch, w_gate, b_gate, w_out, layer, seq):
    t, d = x2.shape
    tm = TM_PROJ
    tpb = seq // tm
    const = lambda shape: pl.BlockSpec((None,) + shape, lambda i: (layer,) + (0,) * len(shape),
                                       pipeline_mode=pl.Buffered(1))
    modspec = lambda k: pl.BlockSpec((None, 1, d), lambda i: ((i // tpb) * 6 + k, 0, 0))
    yspec = pl.BlockSpec((tm, BRANCH_W), lambda i: (i, 0))
    return pl.pallas_call(
        _merge_kernel,
        grid=(t // tm,),
        in_specs=[pl.BlockSpec((tm, d), lambda i: (i, 0)), modspec(1), modspec(0), modspec(2),
                  yspec, yspec, yspec, yspec,
                  const((BRANCH_W, BRANCH_W)), const((4, BRANCH_W, d)), const((d, 4 * d)), const((1, 4 * d)),
                  const((d, d))],
        out_specs=pl.BlockSpec((tm, d), lambda i: (i, 0)),
        out_shape=jax.ShapeDtypeStruct((t, d), F32),
        compiler_params=_cparams(("parallel",)),
        name="merge",
    )(x2, mod3, mod3, mod3, ys5, yhg, yret, ym2, w_glu, w_branch, w_gate, b_gate, w_out)


def _router_kernel(x_ref, sc_ref, sh_ref, wr_ref, br_ref, tri_ref, h_ref, ids_ref, wts_ref, cnt_ref, carry):
    i = pl.program_id(0)

    @pl.when(i == 0)
    def _():
        carry[...] = jnp.zeros_like(carry)

    h = _modulated_norm(x_ref[...], sc_ref[...], sh_ref[...])
    tm, d = h.shape
    packed = _pack_bf16_pairs(h)
    for k in range(N_SLAB):
        h_ref[k] = packed[:, k * SLAB:(k + 1) * SLAB]
    h_hi = h.astype(BF16)
    h_lo = (h - h_hi.astype(F32)).astype(BF16)
    w_r = wr_ref[...]
    w_hi = w_r.astype(BF16)
    w_lo = (w_r - w_hi.astype(F32)).astype(BF16)
    logits = _dot_nt(w_hi, h_hi) + _dot_nt(w_hi, h_lo) + _dot_nt(w_lo, h_hi) + br_ref[:, 0:1]
    gl = [logits[g:g + 1, :] for g in range(MOE_GROUPS)]
    gmax = gl[0]
    gsel = jnp.zeros((1, tm), jnp.int32)
    for g in range(1, MOE_GROUPS):
        better = gl[g] > gmax
        gsel = jnp.where(better, g, gsel)
        gmax = jnp.where(better, gl[g], gmax)
    gden = gl[0] * 0.0
    for g in range(MOE_GROUPS):
        gden = gden + jnp.exp(gl[g] - gmax)
    g_w = 1.0 / gden
    el = []
    for e in range(MOE_EPG):
        v = logits[MOE_GROUPS + e:MOE_GROUPS + e + 1, :]
        for g in range(1, MOE_GROUPS):
            row = MOE_GROUPS + g * MOE_EPG + e
            v = jnp.where(gsel == g, logits[row:row + 1, :], v)
        el.append(v)
    v1 = el[0]
    i1 = jnp.zeros((1, tm), jnp.int32)
    for e in range(1, MOE_EPG):
        better = el[e] > v1
        i1 = jnp.where(better, e, i1)
        v1 = jnp.where(better, el[e], v1)
    v2 = jnp.full((1, tm), -jnp.inf, F32)
    i2 = jnp.zeros((1, tm), jnp.int32)
    for e in range(MOE_EPG):
        better = (el[e] > v2) & (i1 != e)
        i2 = jnp.where(better, e, i2)
        v2 = jnp.where(better, el[e], v2)
    ex = jnp.exp(v2 - v1)
    p1 = 1.0 / (1.0 + ex)
    e1 = gsel * MOE_EPG + i1
    e2 = gsel * MOE_EPG + i2
    erow = lax.broadcasted_iota(jnp.int32, (MOE_EXPERTS, tm), 0)
    oh1 = (erow == e1).astype(F32)
    oh2 = (erow == e2).astype(F32)
    both = oh1 + oh2
    n_grp = tm // LANE
    stacked = jnp.concatenate([both[:, g * LANE:(g + 1) * LANE] for g in range(n_grp)], axis=0)
    within = jnp.dot(stacked.astype(BF16), tri_ref[...], preferred_element_type=F32)
    grp_count = jnp.sum(stacked, axis=1, keepdims=True)
    running = carry[:, 0:1]
    pieces = []
    for g in range(n_grp):
        pieces.append(within[g * MOE_EXPERTS:(g + 1) * MOE_EXPERTS, :] + running)
        running = running + grp_count[g * MOE_EXPERTS:(g + 1) * MOE_EXPERTS, :]
    prefix = jnp.concatenate(pieces, axis=1)
    rank1 = jnp.sum(oh1 * prefix, axis=0, keepdims=True).astype(jnp.int32)
    rank2 = jnp.sum(oh2 * prefix, axis=0, keepdims=True).astype(jnp.int32)
    carry[...] = jnp.broadcast_to(running, carry.shape)
    zi = jnp.zeros((1, tm), jnp.int32)
    ids_ref[...] = jnp.concatenate([e1, e2, rank1, rank2, zi, zi, zi, zi], axis=0)
    wrow = lax.broadcasted_iota(jnp.int32, (LANE, tm), 0)
    wts_ref[...] = jnp.where(wrow == 0, p1 * g_w, jnp.where(wrow == 1, ex * p1 * g_w, 0.0)).T
    cnt_ref[...] = carry[...]


def _router(x2, mod3, w_route, b_route, tri_excl, seq):
    t, d = x2.shape
    tm = TM_PROJ
    tpb = seq // tm
    nr = w_route.shape[0]
    const = lambda shape: pl.BlockSpec(shape, lambda i: (0,) * len(shape))
    modspec = lambda k: pl.BlockSpec((None, 1, d), lambda i: ((i // tpb) * 6 + k, 0, 0))
    return pl.pallas_call(
        _router_kernel,
        grid=(t // tm,),
        in_specs=[pl.BlockSpec((tm, d), lambda i: (i, 0)), modspec(4), modspec(3),
                  const((nr, d)), const((nr, LANE)), const((LANE, LANE))],
        out_specs=[pl.BlockSpec((N_SLAB, tm, SLAB), lambda i: (0, i, 0)),
                   pl.BlockSpec((8, tm), lambda i: (0, i)),
                   pl.BlockSpec((tm, LANE), lambda i: (i, 0)),
                   const((MOE_EXPERTS, LANE))],
        out_shape=[jax.ShapeDtypeStruct((N_SLAB, t, SLAB), jnp.uint32),
                   jax.ShapeDtypeStruct((8, t), jnp.int32),
                   jax.ShapeDtypeStruct((t, LANE), F32),
                   jax.ShapeDtypeStruct((MOE_EXPERTS, LANE), F32)],
        scratch_shapes=[pltpu.VMEM((MOE_EXPERTS, LANE), F32)],
        compiler_params=_cparams(("arbitrary",)),
        name="moe_router",
    )(x2, mod3, mod3, w_route, b_route, tri_excl)


def _sc_mesh():
    return plsc.VectorSubcoreMesh(core_axis_name="core", subcore_axis_name="subcore")


def _slab_rows(idx, n_rows):
    return (idx[None, :] + (jnp.arange(N_SLAB, dtype=jnp.int32) * n_rows)[:, None]).reshape(-1)


def _dispatch(slot1, slot2, h_slabs):
    n_slab, t, d = h_slabs.shape
    n_out = 2 * t
    xs = _scatter_rows(h_slabs.reshape(n_slab * t, d), _slab_rows(slot1, n_out), _slab_rows(slot2, n_out),
                       n_slab * n_out)
    return xs.reshape(n_slab, n_out, d)


def _scatter_rows(src, idx1, idx2, n_out):
    t, d = src.shape
    win = SC_WINDOW

    @pl.kernel(out_type=jax.ShapeDtypeStruct((n_out, d), src.dtype), mesh=_sc_mesh(), name="moe_dispatch_sc")
    def scatter_rows(x_hbm, i1_hbm, i2_hbm, o_hbm):
        def body(x_vmem, i1_vmem, i2_vmem):
            pltpu.sync_copy(x_vmem, o_hbm.at[i1_vmem.at[0]])
            pltpu.sync_copy(x_vmem, o_hbm.at[i2_vmem.at[0]])

        pltpu.emit_pipeline(
            body,
            grid=(t // win,),
            in_specs=[pl.BlockSpec((win, d), lambda i: (i, 0)),
                      pl.BlockSpec((1, win), lambda i: (0, i)),
                      pl.BlockSpec((1, win), lambda i: (0, i))],
            out_specs=[],
            core_axis_name=("core", "subcore"),
            dimension_semantics=(pltpu.PARALLEL,),
        )(x_hbm, i1_hbm, i2_hbm)

    return scatter_rows(src, idx1.reshape(1, t), idx2.reshape(1, t))


def _gather_rows(src, idx):
    m = idx.shape[0]
    d = src.shape[1]
    win = SC_WINDOW

    @pl.kernel(out_type=jax.ShapeDtypeStruct((m, d), src.dtype), mesh=_sc_mesh(), name="moe_gather_sc")
    def gather(x_hbm, i_hbm, o_hbm):
        def body(i_vmem, o_vmem):
            pltpu.sync_copy(x_hbm.at[i_vmem.at[0]], o_vmem)

        pltpu.emit_pipeline(
            body,
            grid=(m // win,),
            in_specs=[pl.BlockSpec((1, win), lambda i: (0, i))],
            out_specs=[pl.BlockSpec((win, d), lambda i: (i, 0))],
            core_axis_name=("core", "subcore"),
            dimension_semantics=(pltpu.PARALLEL,),
        )(i_hbm, o_hbm)

    return gather(src, idx.reshape(1, m))


def _expert_kernel(tile_ref, exp_ref, lo_ref, hi_ref, xs_ref, w1_ref, w3_ref, w2_ref, ys_ref, w1_scr, w3_scr, w2_scr):
    s = pl.program_id(0)
    prev = jnp.maximum(s - 1, 0)
    new_expert = (s == 0) | (exp_ref[s] != exp_ref[prev])
    new_tile = (s == 0) | (tile_ref[s] != tile_ref[prev])

    @pl.when(new_expert)
    def _():
        w1_scr[...] = w1_ref[...].astype(BF16)
        w3_scr[...] = w3_ref[...].astype(BF16)
        w2_scr[...] = w2_ref[...].astype(BF16)

    lo = lo_ref[s]
    hi = hi_ref[s]
    n_rows = xs_ref.shape[1]

    @pl.when(new_tile)
    def _():
        ys_ref[...] = jnp.zeros_like(ys_ref)

    def run_rows(r0, n):
        x = _unpack_bf16_pairs(jnp.concatenate([xs_ref[k, r0:r0 + n, :] for k in range(N_SLAB)], axis=-1)).astype(BF16)
        a = jnp.dot(x, w1_scr[...], preferred_element_type=F32)
        b = jnp.dot(x, w3_scr[...], preferred_element_type=F32)
        act = _silu(a) * b
        y = _pack_bf16_pairs(jnp.dot(act.astype(BF16), w2_scr[...], preferred_element_type=F32))
        row = lax.broadcasted_iota(jnp.int32, (n, SLAB), 0) + r0
        mine = (row >= lo) & (row < hi)
        for k in range(N_SLAB):
            ys_ref[k, r0:r0 + n, :] = jnp.where(mine, y[:, k * SLAB:(k + 1) * SLAB], ys_ref[k, r0:r0 + n, :])

    groups = (hi + (X_SUB - 1)) // X_SUB - lo // X_SUB
    whole = groups > 2

    @pl.when(whole)
    def _():
        run_rows(0, n_rows)

    for r0 in range(0, n_rows, X_SUB):
        @pl.when(jnp.logical_not(whole) & (lo < r0 + X_SUB) & (hi > r0))
        def _(r0=r0):
            run_rows(r0, X_SUB)


def _experts(step_tile, step_expert, step_lo, step_hi, xs, w1, w3, w2, layer):
    n_slab, ns, slab = xs.shape
    d = w1.shape[1]
    ff = w1.shape[2]
    n_steps = step_tile.shape[0]
    base = layer * MOE_EXPERTS
    grid_spec = pltpu.PrefetchScalarGridSpec(
        num_scalar_prefetch=4,
        grid=(n_steps,),
        in_specs=[pl.BlockSpec((n_slab, TM_X, slab), lambda s, tl, ex, lo, hi: (0, tl[s], 0)),
                  pl.BlockSpec((None, d, ff), lambda s, tl, ex, lo, hi: (base + ex[s], 0, 0)),
                  pl.BlockSpec((None, d, ff), lambda s, tl, ex, lo, hi: (base + ex[s], 0, 0)),
                  pl.BlockSpec((None, ff, d), lambda s, tl, ex, lo, hi: (base + ex[s], 0, 0))],
        out_specs=pl.BlockSpec((n_slab, TM_X, slab), lambda s, tl, ex, lo, hi: (0, tl[s], 0)),
        scratch_shapes=[pltpu.VMEM((d, ff), BF16), pltpu.VMEM((d, ff), BF16), pltpu.VMEM((ff, d), BF16)],
    )
    return pl.pallas_call(
        _expert_kernel,
        grid_spec=grid_spec,
        out_shape=jax.ShapeDtypeStruct((n_slab, ns, slab), xs.dtype),
        compiler_params=_cparams(("arbitrary",)),
        name="moe_experts",
    )(step_tile, step_expert, step_lo, step_hi, xs, w1, w3, w2)


def _combine_kernel(x_ref, gate_ref, fw_ref, wcol_ref, y1_ref, y2_ref, o_ref, *, final):
    w_first = wcol_ref[:, 0:1]
    w_second = wcol_ref[:, 1:2]
    y_first = _unpack_bf16_pairs(jnp.concatenate([y1_ref[k] for k in range(N_SLAB)], axis=-1))
    y_second = _unpack_bf16_pairs(jnp.concatenate([y2_ref[k] for k in range(N_SLAB)], axis=-1))
    moe = w_first * y_first + w_second * y_second
    x = x_ref[...] + gate_ref[...] * moe
    if final:
        x = x * lax.rsqrt(jnp.mean(x * x, axis=-1, keepdims=True) + EPS) * fw_ref[...]
    o_ref[...] = x


def _combine(x2, mod3, final_w_row, wcol, gathered, seq, final):
    t, d = x2.shape
    tm = TM_COMB
    tpb = seq // tm
    nblk = t // tm
    yspec = lambda off: pl.BlockSpec((N_SLAB, tm, SLAB), lambda i: (0, i + off, 0))
    return pl.pallas_call(
        functools.partial(_combine_kernel, final=final),
        grid=(nblk,),
        in_specs=[pl.BlockSpec((tm, d), lambda i: (i, 0)),
                  pl.BlockSpec((None, 1, d), lambda i: ((i // tpb) * 6 + 5, 0, 0)),
                  pl.BlockSpec((1, d), lambda i: (0, 0)),
                  pl.BlockSpec((tm, LANE), lambda i: (i, 0)),
                  yspec(0), yspec(nblk)],
        out_specs=pl.BlockSpec((tm, d), lambda i: (i, 0)),
        out_shape=jax.ShapeDtypeStruct((t, d), F32),
        compiler_params=_cparams(("parallel",)),
        name="moe_combine",
    )(x2, mod3, final_w_row, wcol, gathered, gathered)


def _moe(x2, mod3, final_w_row, w_route, b_route, tri_excl, w1, w3, w2, layer, seq, final):
    t, d = x2.shape
    h3, ids, wcol, counts = _router(x2, mod3, w_route, b_route, tri_excl, seq)
    cnt = counts[:, 0].astype(jnp.int32)
    ends = jnp.cumsum(cnt)
    offs = ends - cnt
    experts = jnp.arange(MOE_EXPERTS, dtype=jnp.int32)
    pick = lambda table, idx: jnp.sum(jnp.where(idx[:, None] == experts[None, :], table[None, :], 0), axis=1)
    slot1 = pick(offs, ids[0]) + ids[2]
    slot2 = pick(offs, ids[1]) + ids[3]
    n_tiles = 2 * t // TM_X
    first_tile = offs // TM_X
    n_vis = jnp.where(cnt > 0, (ends - 1) // TM_X - first_tile + 1, 0)
    cum = jnp.cumsum(n_vis)
    step = jnp.arange(n_tiles + MOE_EXPERTS, dtype=jnp.int32)
    step_expert = jnp.minimum(jnp.sum(step[:, None] >= cum[None, :], axis=1), MOE_EXPERTS - 1).astype(jnp.int32)
    valid = step < cum[-1]
    step_tile = jnp.where(valid, pick(first_tile - (cum - n_vis), step_expert) + step, n_tiles - 1)
    step_lo = jnp.where(valid, jnp.clip(pick(offs, step_expert) - step_tile * TM_X, 0, TM_X), 0)
    step_hi = jnp.where(valid, jnp.clip(pick(ends, step_expert) - step_tile * TM_X, 0, TM_X), 0)
    xs = _dispatch(slot1, slot2, h3)
    ys = _experts(step_tile.astype(jnp.int32), step_expert, step_lo.astype(jnp.int32), step_hi.astype(jnp.int32),
                  xs, w1, w3, w2, layer)
    n_sorted = ys.shape[1]
    gathered = _gather_rows(ys.reshape(N_SLAB * n_sorted, SLAB), _slab_rows(jnp.concatenate([slot1, slot2]), n_sorted))
    gathered = gathered.reshape(N_SLAB, n_sorted, SLAB)
    return _combine(x2, mod3, final_w_row, wcol, gathered, seq, final)


def kernel(x, c, positions, ada_w, ada_b, w_in, s5_lam_re, s5_lam_im, s5_b_re, s5_b_im, s5_c_re, s5_c_im, s5_d, s5_log_dt, s5_w_glu, hg_lb_logits, hg_norm_w, m2_conv_w, m2_conv_b, m2_dt_bias, m2_a_log, m2_d, m2_norm_w, w_branch, w_gate, b_gate, w_out, moe_w_group, moe_b_group, moe_w_expert, moe_b_expert, moe_w1, moe_w3, moe_w2, final_norm_w):
    bsz, seq, d = x.shape
    t = bsz * seq
    depth = ada_w.shape[0]
    assert seq % TM_PROJ == 0 and seq % C_RET == 0 and seq % C_SSD == 0 and seq % C_HG == 0
    x2 = x.reshape(t, d).astype(F32)

    c_pad = jnp.zeros((8, d), F32).at[:bsz].set(c.astype(F32))
    mod_all = _ada_mod(c_pad.T, ada_w.astype(F32), ada_b.astype(F32), bsz)

    half = RET_DK // 2
    inv_freq = ROPE_BASE ** (-jnp.arange(half, dtype=F32) / half)
    invf_col = jnp.broadcast_to(inv_freq[:, None], (half, LANE))
    expand = np.tile(np.eye(half, dtype=np.float32), (1, 2 * RET_HEADS))
    sign = np.tile(np.concatenate([-np.ones(half), np.ones(half)]), RET_HEADS)[None, :].astype(np.float32)
    cos_t, sin_t = _rope_tables(positions.reshape(1, t).astype(jnp.int32), invf_col,
                                jnp.asarray(expand, BF16), jnp.asarray(expand * sign, BF16))
    cos3 = cos_t.reshape(bsz, seq, -1)
    sin3 = sin_t.reshape(bsz, seq, -1)

    lb_cum = jnp.cumsum(jax.nn.softmax(hg_lb_logits.astype(F32), axis=0), axis=0)
    hg_lb = lb_cum - lb_cum[:1]
    tri_ssd = jnp.asarray(np.tril(np.ones((C_SSD, C_SSD), np.float32)), BF16)
    tri_excl = jnp.asarray(np.triu(np.ones((LANE, LANE), np.float32), 1), BF16)
    final_w_row = final_norm_w.astype(F32)[None, :]
    n_main = IN_W // LANE * LANE
    w_main = w_in[:, :, :n_main].astype(BF16)
    w_tail = jnp.zeros((depth, d, IN_W_PAD - n_main), BF16).at[:, :, :IN_W - n_main].set(w_in[:, :, n_main:].astype(BF16))
    w_glu_bf = s5_w_glu.astype(BF16)
    w_branch_bf = w_branch.astype(BF16)
    w_gate_bf = w_gate.astype(BF16)
    w_out_bf = w_out.astype(BF16)
    b_gate3 = b_gate.astype(F32).reshape(depth, 1, -1)
    moe_w1_all = moe_w1.astype(F32).reshape(depth * MOE_EXPERTS, d, MOE_FF)
    moe_w3_all = moe_w3.astype(F32).reshape(depth * MOE_EXPERTS, d, MOE_FF)
    moe_w2_all = moe_w2.astype(F32).reshape(depth * MOE_EXPERTS, MOE_FF, d)

    for layer in range(depth):
        mod3 = mod_all[layer, :bsz].reshape(bsz * 6, 1, d)
        p, u_s5 = _in_proj(x2, mod3, w_main, w_tail, layer, seq)
        p3 = p.reshape(bsz, seq, IN_W_PAD)

        ops = _s5_operators(s5_lam_re[layer], s5_lam_im[layer], s5_b_re[layer], s5_b_im[layer],
                            s5_c_re[layer], s5_c_im[layer], s5_d[layer], s5_log_dt[layer])
        y_s5 = _s5_scan(u_s5.reshape(bsz, seq, BRANCH_W), *ops).reshape(t, BRANCH_W)

        lb = hg_lb[layer][None, :]
        y_hg = _hgrn2(p3, jnp.log(lb), jnp.log1p(-lb), hg_norm_w[layer].astype(F32)[None, :]).reshape(t, BRANCH_W)

        y_ret = _retention(p3, cos3, sin3).reshape(t, BRANCH_W)

        pad8 = lambda v: jnp.zeros((1, LANE), F32).at[0, :M2_HEADS].set(v.astype(F32))
        y_m2 = _ssd(p3, tri_ssd, m2_conv_w[layer].astype(F32), m2_conv_b[layer].astype(F32)[None, :],
                    pad8(m2_dt_bias[layer]), pad8(m2_a_log[layer]),
                    jnp.repeat(m2_d[layer].astype(F32), M2_HEADDIM)[None, :],
                    m2_norm_w[layer].astype(F32)[None, :]).reshape(t, BRANCH_W)

        nr = 40
        w_route = jnp.zeros((nr, d), F32).at[:MOE_GROUPS].set(moe_w_group[layer].astype(F32).T)
        w_route = w_route.at[MOE_GROUPS:MOE_GROUPS + MOE_EXPERTS].set(moe_w_expert[layer].astype(F32).T)
        b_route = jnp.zeros((nr, LANE), F32).at[:MOE_GROUPS, 0].set(moe_b_group[layer].astype(F32))
        b_route = b_route.at[MOE_GROUPS:MOE_GROUPS + MOE_EXPERTS, 0].set(moe_b_expert[layer].astype(F32))
        x2 = _merge(x2, mod3, y_s5, y_hg, y_ret, y_m2, w_glu_bf, w_branch_bf, w_gate_bf, b_gate3, w_out_bf,
                    layer, seq)
        x2 = _moe(x2, mod3, final_w_row, w_route, b_route, tri_excl, moe_w1_all, moe_w3_all, moe_w2_all,
                  layer, seq, final=(layer == depth - 1))
    return x2.reshape(bsz, seq, d)
```

```python
import functools
import math

import numpy as np
import jax
import jax.numpy as jnp
from jax import lax
from jax.experimental import pallas as pl
from jax.experimental.pallas import tpu as pltpu
from jax.experimental.pallas import tpu_sc as plsc

F32 = jnp.float32
BF16 = jnp.bfloat16
HIGHEST = lax.Precision.HIGHEST

D_MODEL = 1024
BRANCH_W = 512
EPS = 1e-6
S5_GROUPS = 32
S5_CH = 16
S5_STATE = 64
S5_MAX_REAL = -1e-4
S5_BLOCK = 8
S5_SEQ_PER_STEP = 2
HG_HEADS = 4
HG_DK = 128
RET_HEADS = 4
RET_DK = 64
RET_DV = 128
ROPE_BASE = 10000.0
M2_HEADS = 8
M2_HEADDIM = 64
M2_GROUPS = 2
M2_STATE = 128
M2_CONV = 4
MOE_GROUPS = 4
MOE_EPG = 8
MOE_EXPERTS = MOE_GROUPS * MOE_EPG
MOE_FF = 256

COL_S5, COL_HQ, COL_HF, COL_HI, COL_HG = 0, 512, 1024, 1536, 2048
COL_RQ, COL_RK, COL_RV, COL_RG = 2560, 2816, 3072, 3584
COL_MZ, COL_MXS, COL_MBC, COL_MDT = 4096, 4608, 5120, 5632
IN_W = 5640
IN_W_PAD = 5760

LANE = 128
VMEM_LIMIT = 56 * 1024 * 1024

TM_PROJ = 1024
TN_PROJ = 1024
TM_INPROJ = 512
LOG2_E = 1.4426950408889634
C_RET = 512
C_SSD = 256
C_HG = 128
TM_X = 512
X_SUB = 128
TM_COMB = 1024
SC_WINDOW = 128
SLAB = 256
N_SLAB = D_MODEL // 2 // SLAB


def _cparams(sem):
    return pltpu.CompilerParams(dimension_semantics=sem, vmem_limit_bytes=VMEM_LIMIT)


def _silu(v):
    return v * jax.nn.sigmoid(v)


def _dot_nt(a, b, **kw):
    return lax.dot_general(a, b, (((1,), (1,)), ((), ())), preferred_element_type=F32, **kw)


def _dot_tn(a, b, **kw):
    return lax.dot_general(a, b, (((0,), (0,)), ((), ())), preferred_element_type=F32, **kw)


def _ada_kernel(ct_ref, w_ref, b_ref, o_ref, *, n_rows):
    cond_t = _silu(ct_ref[...])
    w = w_ref[...]
    rows = [jnp.sum(w * cond_t[:, b:b + 1], axis=0, keepdims=True) for b in range(n_rows)]
    rows += [jnp.zeros_like(rows[0])] * (cond_t.shape[1] - n_rows)
    o_ref[...] = jnp.concatenate(rows, axis=0) + b_ref[...]


def _ada_mod(c_pad_t, ada_w, ada_b, n_rows):
    depth, d, n = ada_w.shape
    tn = 1536
    return pl.pallas_call(
        functools.partial(_ada_kernel, n_rows=n_rows),
        grid=(depth, n // tn),
        in_specs=[pl.BlockSpec((d, 8), lambda l, j: (0, 0)),
                  pl.BlockSpec((None, d, tn), lambda l, j: (l, 0, j)),
                  pl.BlockSpec((None, 1, tn), lambda l, j: (l, 0, j))],
        out_specs=pl.BlockSpec((None, 8, tn), lambda l, j: (l, 0, j)),
        out_shape=jax.ShapeDtypeStruct((depth, 8, n), F32),
        compiler_params=_cparams(("parallel", "parallel")),
        name="ada_mod",
    )(c_pad_t, ada_w, ada_b.reshape(depth, 1, n))


def _pack_bf16_pairs(x):
    n = x.shape[1] // 2
    lo = pltpu.bitcast(x[:, :n].astype(BF16).astype(F32), jnp.uint32) >> 16
    hi = pltpu.bitcast(x[:, n:].astype(BF16).astype(F32), jnp.uint32)
    return hi | lo


def _unpack_bf16_pairs(w):
    lo = pltpu.bitcast(w << 16, F32)
    hi = pltpu.bitcast(w & jnp.uint32(0xFFFF0000), F32)
    return jnp.concatenate([lo, hi], axis=-1)


def _modulated_norm(x, scale, shift):
    ms = jnp.mean(x * x, axis=-1, keepdims=True)
    return x * lax.rsqrt(ms + EPS) * (1.0 + scale) + shift


def _inproj_kernel(x_ref, sc_ref, sh_ref, w_ref, wtail_ref, o_ref, u_ref):
    h = _modulated_norm(x_ref[...], sc_ref[...], sh_ref[...]).astype(BF16)
    n_main = w_ref.shape[1]
    for n0 in range(0, n_main, TN_PROJ):
        n1 = min(n0 + TN_PROJ, n_main)
        p = jnp.dot(h, w_ref[:, n0:n1], preferred_element_type=F32)
        o_ref[:, n0:n1] = p.astype(o_ref.dtype)
        if n0 == 0:
            u_ref[...] = p[:, COL_S5:COL_S5 + BRANCH_W]
    o_ref[:, n_main:] = jnp.dot(h, wtail_ref[...], preferred_element_type=F32).astype(o_ref.dtype)


def _in_proj(x2, mod3, w_main, w_tail, layer, seq):
    t, d = x2.shape
    tm = TM_INPROJ
    tpb = seq // tm
    n_main = w_main.shape[2]
    assert COL_S5 + BRANCH_W <= TN_PROJ and n_main + w_tail.shape[2] == IN_W_PAD
    return pl.pallas_call(
        _inproj_kernel,
        grid=(t // tm,),
        in_specs=[pl.BlockSpec((tm, d), lambda i: (i, 0)),
                  pl.BlockSpec((None, 1, d), lambda i: ((i // tpb) * 6 + 1, 0, 0)),
                  pl.BlockSpec((None, 1, d), lambda i: ((i // tpb) * 6 + 0, 0, 0)),
                  pl.BlockSpec((None, d, n_main), lambda i: (layer, 0, 0), pipeline_mode=pl.Buffered(1)),
                  pl.BlockSpec((None, d, IN_W_PAD - n_main), lambda i: (layer, 0, 0), pipeline_mode=pl.Buffered(1))],
        out_specs=[pl.BlockSpec((tm, IN_W_PAD), lambda i: (i, 0)),
                   pl.BlockSpec((tm, BRANCH_W), lambda i: (i, 0))],
        out_shape=[jax.ShapeDtypeStruct((t, IN_W_PAD), BF16), jax.ShapeDtypeStruct((t, BRANCH_W), F32)],
        compiler_params=_cparams(("parallel",)),
        name="in_proj",
    )(x2, mod3, mod3, w_main, w_tail)


def _rope_kernel(pos_ref, invf_ref, ecos_ref, esin_ref, cos_ref, sin_ref):
    ang = invf_ref[:, 0:1] * pos_ref[...].astype(F32)
    def spread(values, e_ref):
        hi = values.astype(BF16)
        rest = values - hi.astype(F32)
        mid = rest.astype(BF16)
        lo = (rest - mid.astype(F32)).astype(BF16)
        e = e_ref[...]
        return _dot_tn(hi, e) + _dot_tn(mid, e) + _dot_tn(lo, e)

    cos_ref[...] = spread(jnp.cos(ang), ecos_ref)
    sin_ref[...] = spread(jnp.sin(ang), esin_ref)


def _rope_tables(pos_row, invf_col, expand_cos, expand_sin):
    t = pos_row.shape[1]
    half, w = expand_cos.shape
    tm = 1024
    const = lambda shape: pl.BlockSpec(shape, lambda i: (0, 0))
    return pl.pallas_call(
        _rope_kernel,
        grid=(t // tm,),
        in_specs=[pl.BlockSpec((1, tm), lambda i: (0, i)), const((half, LANE)), const((half, w)), const((half, w))],
        out_specs=[pl.BlockSpec((tm, w), lambda i: (i, 0))] * 2,
        out_shape=[jax.ShapeDtypeStruct((t, w), F32)] * 2,
        compiler_params=_cparams(("parallel",)),
        name="rope_tables",
    )(pos_row, invf_col, expand_cos, expand_sin)


def _ret_kernel(q_ref, k_ref, v_ref, g_ref, cos_ref, sin_ref, o_ref, st_ref, dec_ref, *, chunk):
    @pl.when(pl.program_id(1) == 0)
    def _():
        st_ref[...] = jnp.zeros_like(st_ref)
        ti = lax.broadcasted_iota(jnp.int32, (chunk, chunk), 0)
        si = lax.broadcasted_iota(jnp.int32, (chunk, chunk), 1)
        lag = (ti - si).astype(F32)
        for h in range(RET_HEADS):
            log_gamma = math.log1p(-(2.0 ** (-5.0 - h)))
            dec_ref[h] = jnp.where(ti >= si, jnp.exp(jnp.minimum(lag * log_gamma, 0.0)), 0.0)

    cosf = cos_ref[...]
    sinf = sin_ref[...]
    width = RET_HEADS * RET_DK
    lane = lax.broadcasted_iota(jnp.int32, (chunk, width), 1)
    first_half = (lane % RET_DK) < (RET_DK // 2)

    def rope(t):
        partner = jnp.where(first_half, pltpu.roll(t, width - RET_DK // 2, 1), pltpu.roll(t, RET_DK // 2, 1))
        return t * cosf + partner * sinf

    q = rope(q_ref[...].astype(F32))
    k = rope(k_ref[...].astype(F32)) * (RET_DK ** -0.5)
    v = v_ref[...]
    g = g_ref[...].astype(F32)
    tcol = lax.broadcasted_iota(jnp.int32, (chunk, 1), 0).astype(F32)
    for h in range(RET_HEADS):
        log_gamma = math.log1p(-(2.0 ** (-5.0 - h)))
        qh = q[:, h * RET_DK:(h + 1) * RET_DK]
        kh = k[:, h * RET_DK:(h + 1) * RET_DK]
        vh = v[:, h * RET_DV:(h + 1) * RET_DV].astype(BF16)
        scores = _dot_nt(qh.astype(BF16), kh.astype(BF16)) * dec_ref[h]
        state = st_ref[h]
        q_in = qh * jnp.exp(log_gamma * (tcol + 1.0))
        o = (jnp.dot(scores.astype(BF16), vh, preferred_element_type=F32)
             + jnp.dot(q_in.astype(BF16), state.astype(BF16), preferred_element_type=F32))
        k_out = kh * jnp.exp(log_gamma * (chunk - 1.0 - tcol))
        st_ref[h] = math.exp(log_gamma * chunk) * state + _dot_tn(k_out.astype(BF16), vh)
        o = o * lax.rsqrt(jnp.mean(o * o, axis=-1, keepdims=True) + EPS)
        gh = g[:, h * RET_DV:(h + 1) * RET_DV]
        o_ref[:, h * RET_DV:(h + 1) * RET_DV] = (o * _silu(gh)).astype(o_ref.dtype)


def _retention(p3, cos3, sin3):
    b, seq, _ = p3.shape
    c = C_RET
    qk_w = RET_HEADS * RET_DK
    return pl.pallas_call(
        functools.partial(_ret_kernel, chunk=c),
        grid=(b, seq // c),
        in_specs=[pl.BlockSpec((None, c, qk_w), lambda i, j: (i, j, COL_RQ // qk_w)),
                  pl.BlockSpec((None, c, qk_w), lambda i, j: (i, j, COL_RK // qk_w)),
                  pl.BlockSpec((None, c, BRANCH_W), lambda i, j: (i, j, COL_RV // BRANCH_W)),
                  pl.BlockSpec((None, c, BRANCH_W), lambda i, j: (i, j, COL_RG // BRANCH_W)),
                  pl.BlockSpec((None, c, qk_w), lambda i, j: (i, j, 0)),
                  pl.BlockSpec((None, c, qk_w), lambda i, j: (i, j, 0))],
        out_specs=pl.BlockSpec((None, c, BRANCH_W), lambda i, j: (i, j, 0)),
        out_shape=jax.ShapeDtypeStruct((b, seq, BRANCH_W), BF16),
        scratch_shapes=[pltpu.VMEM((RET_HEADS, RET_DK, RET_DV), F32), pltpu.VMEM((RET_HEADS, c, c), F32)],
        compiler_params=_cparams(("parallel", "arbitrary")),
        name="retention",
    )(p3, p3, p3, p3, cos3, sin3)


def _ssd_kernel(z_ref, xs_ref, bc_ref, dt_ref, tri_ref, cw_ref, cb_ref, dtb_ref, alog_ref, dsk_ref, nw_ref,
                o_ref, xe_scr, st_ref, *, chunk):
    j = pl.program_id(1)
    width = 2 * BRANCH_W

    @pl.when(j == 0)
    def _():
        st_ref[...] = jnp.zeros_like(st_ref)
        xe_scr[0:8, :] = jnp.zeros((8, width), F32)

    @pl.when(j > 0)
    def _():
        xe_scr[0:8, :] = xe_scr[chunk:chunk + 8, :]

    xe_scr[8:, 0:BRANCH_W] = xs_ref[...].astype(F32)
    xe_scr[8:, BRANCH_W:] = bc_ref[...].astype(F32)
    conv = cb_ref[...] + cw_ref[M2_CONV - 1:M2_CONV, :] * xe_scr[8:, :]
    for tap in range(M2_CONV - 1):
        conv = conv + cw_ref[tap:tap + 1, :] * xe_scr[pl.ds(8 - (M2_CONV - 1) + tap, chunk), :]
    conv = _silu(conv)
    xs = conv[:, :BRANCH_W]
    bm = conv[:, BRANCH_W:BRANCH_W + M2_GROUPS * M2_STATE]
    cm = conv[:, BRANCH_W + M2_GROUPS * M2_STATE:]

    dt = jax.nn.softplus(dt_ref[...].astype(F32) + dtb_ref[...])
    da = dt * (-jnp.exp(alog_ref[...]))
    da_hi = da.astype(BF16)
    da_r = da - da_hi.astype(F32)
    da_mid = da_r.astype(BF16)
    da_lo = (da_r - da_mid.astype(F32)).astype(BF16)
    tri = tri_ref[...]
    a_cs = (jnp.dot(tri, da_hi, preferred_element_type=F32) + jnp.dot(tri, da_mid, preferred_element_type=F32)
            + jnp.dot(tri, da_lo, preferred_element_type=F32))
    a_cs = a_cs * LOG2_E
    a_cs_t = a_cs.T
    ti = lax.broadcasted_iota(jnp.int32, (chunk, chunk), 0)
    si = lax.broadcasted_iota(jnp.int32, (chunk, chunk), 1)
    causal = ti >= si
    hpg = M2_HEADS // M2_GROUPS
    pair_w = 2 * M2_HEADDIM
    upper = lax.broadcasted_iota(jnp.int32, (chunk, pair_w), 1) >= M2_HEADDIM
    upper_state = lax.broadcasted_iota(jnp.int32, (M2_STATE, pair_w), 1) >= M2_HEADDIM
    ys = []
    for grp in range(M2_GROUPS):
        bm_g = bm[:, grp * M2_STATE:(grp + 1) * M2_STATE]
        cm_g = cm[:, grp * M2_STATE:(grp + 1) * M2_STATE]
        bm_bf = bm_g.astype(BF16)
        cb = _dot_nt(cm_g.astype(BF16), bm_bf)
        for pp in range(hpg // 2):
            h0 = grp * hpg + 2 * pp
            pair = h0 // 2
            xd = xs[:, pair * pair_w:(pair + 1) * pair_w] * jnp.where(upper, dt[:, h0 + 1:h0 + 2], dt[:, h0:h0 + 1])
            xd_bf = xd.astype(BF16)
            state = st_ref[pair]
            state_bf = state.astype(BF16)
            y_heads, state_heads = [], []
            for h in (h0, h0 + 1):
                col = a_cs[:, h:h + 1]
                row = a_cs_t[h:h + 1, :]
                lmat = jnp.where(causal, jnp.exp2(col - row), 0.0)
                y_heads.append(jnp.dot((cb * lmat).astype(BF16), xd_bf, preferred_element_type=F32)
                               + jnp.dot((cm_g * jnp.exp2(col)).astype(BF16), state_bf, preferred_element_type=F32))
                a_last = a_cs[chunk - 1:chunk, h:h + 1]
                to_end = jnp.exp2(a_last - col)
                state_heads.append(jnp.exp2(a_last) * state + _dot_tn(bm_bf, (xd * to_end).astype(BF16)))
            ys.append(jnp.where(upper, y_heads[1], y_heads[0]))
            st_ref[pair] = jnp.where(upper_state, state_heads[1], state_heads[0])
    y = jnp.concatenate(ys, axis=-1) + dsk_ref[...] * xs
    y = y * _silu(z_ref[...].astype(F32))
    o_ref[...] = (y * lax.rsqrt(jnp.mean(y * y, axis=-1, keepdims=True) + EPS) * nw_ref[...]).astype(o_ref.dtype)


def _ssd(p3, tri, conv_w, conv_b, dt_bias_row, a_log_row, d_skip_row, norm_w_row):
    b, seq, _ = p3.shape
    c = C_SSD
    const = lambda shape: pl.BlockSpec(shape, lambda i, j: (0,) * len(shape))
    return pl.pallas_call(
        functools.partial(_ssd_kernel, chunk=c),
        grid=(b, seq // c),
        in_specs=[pl.BlockSpec((None, c, BRANCH_W), lambda i, j: (i, j, COL_MZ // BRANCH_W)),
                  pl.BlockSpec((None, c, BRANCH_W), lambda i, j: (i, j, COL_MXS // BRANCH_W)),
                  pl.BlockSpec((None, c, BRANCH_W), lambda i, j: (i, j, COL_MBC // BRANCH_W)),
                  pl.BlockSpec((None, c, LANE), lambda i, j: (i, j, COL_MDT // LANE)),
                  const((c, c)), const((M2_CONV, 2 * BRANCH_W)), const((1, 2 * BRANCH_W)),
                  const((1, LANE)), const((1, LANE)), const((1, BRANCH_W)), const((1, BRANCH_W))],
        out_specs=pl.BlockSpec((None, c, BRANCH_W), lambda i, j: (i, j, 0)),
        out_shape=jax.ShapeDtypeStruct((b, seq, BRANCH_W), BF16),
        scratch_shapes=[pltpu.VMEM((c + 8, 2 * BRANCH_W), F32),
                        pltpu.VMEM((M2_HEADS // 2, M2_STATE, 2 * M2_HEADDIM), F32)],
        compiler_params=_cparams(("parallel", "arbitrary")),
        name="ssd",
    )(p3, p3, p3, p3, tri, conv_w, conv_b, dt_bias_row, a_log_row, d_skip_row, norm_w_row)


def _hg_tables(chunk):
    n_lev = int(math.log2(chunk))
    r = np.arange(chunk)[:, None]
    jj = np.arange(chunk)[None, :]
    tri = (jj <= r).astype(np.float32)
    x = r ^ jj
    levmap = np.where(r > jj, np.floor(np.log2(x + 0.5)), np.where(r == jj, -1, -2)).astype(np.int32)
    return tri, levmap, n_lev


def _hg_level_exponent(b, lev):
    rows, width = b.shape
    m = 1 << lev
    sub = 8
    if 2 * m >= sub:
        blocks = b.reshape(rows // (2 * m), 2 * m, width)
        mid = jnp.broadcast_to(blocks[:, m - 1:m, :], blocks.shape).reshape(rows, width)
    else:
        groups = b.reshape(rows // sub, sub, width)
        row_in_group = lax.broadcasted_iota(jnp.int32, groups.shape, 1)
        mid = None
        for start in range(0, sub, 2 * m):
            picked = jnp.broadcast_to(groups[:, start + m - 1:start + m, :], groups.shape)
            mid = picked if mid is None else jnp.where(row_in_group >= start, picked, mid)
        mid = mid.reshape(rows, width)
    return -jnp.abs(b - mid)


def _hg_kernel(q_ref, f_ref, i_ref, g_ref, sum_ref, lev_ref, llb_ref, l1m_ref, nw_ref, o_ref, st_ref,
               *, chunk, n_lev):
    @pl.when(pl.program_id(1) == 0)
    def _():
        st_ref[...] = jnp.zeros_like(st_ref)

    f = f_ref[...].astype(F32)
    y = jnp.exp(-jnp.abs(f))
    one_plus_y = 1.0 + y
    log_sig = jnp.minimum(f, 0.0) - jnp.log(one_plus_y)
    a = llb_ref[...]
    bb = l1m_ref[...] + log_sig
    log_f = jnp.maximum(a, bb) + jnp.log(1.0 + jnp.exp(-jnp.abs(a - bb)))
    k_all = jnp.exp(l1m_ref[...]) * (jnp.where(f >= 0.0, y, 1.0) / one_plus_y)
    q_all = _silu(q_ref[...].astype(F32))
    hi = log_f.astype(BF16)
    r1 = log_f - hi.astype(F32)
    mid = r1.astype(BF16)
    lo = (r1 - mid.astype(F32)).astype(BF16)
    tri = sum_ref[...]
    b_all = (jnp.dot(tri, hi, preferred_element_type=F32)
             + jnp.dot(tri, mid, preferred_element_type=F32)
             + jnp.dot(tri, lo, preferred_element_type=F32))
    b_all = b_all * LOG2_E
    to_end_all = b_all[chunk - 1:chunk, :] - b_all
    level_decay = [jnp.exp2(_hg_level_exponent(b_all, lev)) for lev in range(n_lev)]
    levmap = lev_ref[...]
    on_diag = levmap == -1
    on_level = [levmap == lev for lev in range(n_lev)]
    v_all = i_ref[...]
    g_all = g_ref[...].astype(F32)
    for h in range(HG_HEADS):
        sl = slice(h * HG_DK, (h + 1) * HG_DK)
        qh = q_all[:, sl]
        kh = k_all[:, sl]
        vh = v_all[:, sl].astype(BF16)
        b_h = b_all[:, sl]
        to_end = to_end_all[:, sl]
        amat = jnp.where(on_diag, _dot_nt(qh.astype(BF16), kh.astype(BF16)), 0.0)
        for lev in range(n_lev):
            e = level_decay[lev][:, sl]
            a_l = _dot_nt((qh * e).astype(BF16), (kh * e).astype(BF16))
            amat = jnp.where(on_level[lev], a_l, amat)
        state_t = st_ref[h]
        o = (jnp.dot(amat.astype(BF16), vh, preferred_element_type=F32)
             + _dot_nt((qh * jnp.exp2(b_h)).astype(BF16), state_t.astype(BF16)))
        k_end = kh * jnp.exp2(to_end)
        st_ref[h] = jnp.exp2(b_h[chunk - 1:chunk, :]) * state_t + _dot_tn(vh, k_end.astype(BF16))
        o = o * lax.rsqrt(jnp.mean(o * o, axis=-1, keepdims=True) + EPS) * nw_ref[...]
        o_ref[:, sl] = (o * _silu(g_all[:, sl])).astype(o_ref.dtype)


def _hgrn2(p3, log_lb_row, log1m_lb_row, norm_w_row):
    b, seq, _ = p3.shape
    c = C_HG
    tri, levmap, n_lev = _hg_tables(c)
    const = lambda shape: pl.BlockSpec(shape, lambda i, j: (0,) * len(shape))
    blk = lambda col: pl.BlockSpec((None, c, BRANCH_W), lambda i, j: (i, j, col // BRANCH_W))
    return pl.pallas_call(
        functools.partial(_hg_kernel, chunk=c, n_lev=n_lev),
        grid=(b, seq // c),
        in_specs=[blk(COL_HQ), blk(COL_HF), blk(COL_HI), blk(COL_HG),
                  const((c, c)), const((c, c)),
                  const((1, BRANCH_W)), const((1, BRANCH_W)), const((1, HG_DK))],
        out_specs=pl.BlockSpec((None, c, BRANCH_W), lambda i, j: (i, j, 0)),
        out_shape=jax.ShapeDtypeStruct((b, seq, BRANCH_W), BF16),
        scratch_shapes=[pltpu.VMEM((HG_HEADS, HG_DK, HG_DK), F32)],
        compiler_params=_cparams(("parallel", "arbitrary")),
        name="hgrn2",
    )(p3, p3, p3, p3, jnp.asarray(tri, BF16), jnp.asarray(levmap), log_lb_row, log1m_lb_row, norm_w_row)


def _expand_block_diag(comp_ref, e_ref, dst_ref, row_div, lane_div, causal=False):
    gq = LANE // S5_CH
    rows, ncols = dst_ref.shape
    step = 512
    for c0 in range(0, ncols, step):
        r1 = min(rows, c0 + step) if causal else rows
        row_grp = (lax.broadcasted_iota(jnp.int32, (r1, step), 0) // row_div) % gq
        lane_grp = ((lax.broadcasted_iota(jnp.int32, (r1, step), 1) + c0) // lane_div) % gq
        full = jnp.dot(comp_ref[0:r1, :], e_ref[:, c0:c0 + step], preferred_element_type=F32)
        dst_ref[0:r1, c0:c0 + step] = jnp.where(row_grp == lane_grp, full, 0.0).astype(dst_ref.dtype)


def _s5_kernel(u_ref, k2_ref, bc_ref, cc_ref, esc_ref, lam_ref, o_ref, tc_scr, tq_ref, bqt_ref, cq_ref,
               x_scr, w_scr, s_scr, *, rows):
    nb = S5_BLOCK

    @pl.when(pl.program_id(1) == 0)
    def _():
        k2 = k2_ref[...]
        lane = lax.broadcasted_iota(jnp.int32, k2.shape, 1)
        for t in range(nb):
            shifted = k2 if t == 0 else jnp.where(lane >= t * S5_CH, pltpu.roll(k2, t * S5_CH, 1), 0.0)
            tc_scr[t * LANE:(t + 1) * LANE, :] = shifted.astype(tc_scr.dtype)
        _expand_block_diag(tc_scr, esc_ref, tq_ref, S5_CH, S5_CH, causal=True)
        _expand_block_diag(bc_ref, esc_ref, bqt_ref, S5_STATE, S5_CH)
        _expand_block_diag(cc_ref, esc_ref, cq_ref, S5_STATE, S5_CH)

    n_seq = u_ref.shape[0]
    for b in range(n_seq):
        for t in range(nb):
            x_scr[b * rows:(b + 1) * rows, t * LANE:(t + 1) * LANE] = (
                u_ref[b, pl.ds(t, rows, stride=nb), :].astype(x_scr.dtype))
    x = x_scr[...]
    half = w_scr.shape[1] // 2
    w_scr[...] = _dot_nt(x, bqt_ref[...])
    lam_re = lam_ref[0:1, :]
    lam_im = lam_ref[1:2, :]

    def body(j, carry):
        out = []
        for b in range(n_seq):
            s_re, s_im = carry[2 * b], carry[2 * b + 1]
            r = b * rows + j
            s_scr[pl.ds(r, 1), 0:half] = s_re
            s_scr[pl.ds(r, 1), half:] = s_im
            w_re = w_scr[pl.ds(r, 1), 0:half]
            w_im = w_scr[pl.ds(r, 1), half:]
            out += [lam_re * s_re - lam_im * s_im + w_re, lam_re * s_im + lam_im * s_re + w_im]
        return tuple(out)

    zero = jnp.zeros((1, half), F32)
    lax.fori_loop(0, rows, body, (zero,) * (2 * n_seq), unroll=8)
    s_bf = s_scr[...].astype(BF16)
    pair = 2 * LANE
    for c0 in range(0, nb * LANE, pair):
        k_rows = c0 + pair
        y = (jnp.dot(x[:, :k_rows], tq_ref[0:k_rows, c0:c0 + pair], preferred_element_type=F32)
             + jnp.dot(s_bf, cq_ref[:, c0:c0 + pair], preferred_element_type=F32))
        for b in range(n_seq):
            for t in range(c0 // LANE, (c0 + pair) // LANE):
                o_ref[b, pl.ds(t, rows, stride=nb), :] = y[b * rows:(b + 1) * rows, t * LANE - c0:(t + 1) * LANE - c0]


def _s5_scan(p3, k2, bc, cc, lam16):
    batch, seq, _ = p3.shape
    nb = S5_BLOCK
    nq = BRANCH_W // LANE
    rows = seq // nb
    kdim = nb * LANE
    gq = LANE // S5_CH
    ncol = 2 * gq * S5_STATE
    e_sc = (np.eye(nb)[:, None, :, None, None] * np.eye(S5_CH)[None, :, None, None, :] * np.ones((1, 1, 1, gq, 1)))
    e_sc = e_sc.reshape(nb * S5_CH, nb * gq * S5_CH)
    full = lambda shape: pl.BlockSpec(shape, lambda q, b: (0,) * len(shape))
    per_q = lambda r, c: pl.BlockSpec((None, r, c), lambda q, b: (q, 0, 0))
    n_seq = S5_SEQ_PER_STEP if batch % S5_SEQ_PER_STEP == 0 else 1
    return pl.pallas_call(
        functools.partial(_s5_kernel, rows=rows),
        grid=(nq, batch // n_seq),
        in_specs=[pl.BlockSpec((n_seq, seq, LANE), lambda q, b: (b, 0, q)),
                  per_q(LANE, nb * S5_CH), per_q(ncol, nb * S5_CH), per_q(ncol, nb * S5_CH),
                  full(e_sc.shape), per_q(2, ncol // 2)],
        out_specs=pl.BlockSpec((n_seq, seq, LANE), lambda q, b: (b, 0, q)),
        out_shape=jax.ShapeDtypeStruct((batch, seq, BRANCH_W), F32),
        scratch_shapes=[pltpu.VMEM((kdim, nb * S5_CH), BF16),
                        pltpu.VMEM((kdim, kdim), BF16), pltpu.VMEM((ncol, kdim), BF16), pltpu.VMEM((ncol, kdim), BF16),
                        pltpu.VMEM((n_seq * rows, kdim), BF16), pltpu.VMEM((n_seq * rows, ncol), F32),
                        pltpu.VMEM((n_seq * rows, ncol), F32)],
        compiler_params=_cparams(("parallel", "arbitrary")),
        name="s5_scan",
    )(p3, k2, bc, cc, jnp.asarray(e_sc, BF16), lam16)


def _s5_operators(lam_re, lam_im, b_re, b_im, c_re, c_im, d_skip, log_dt):
    nb = S5_BLOCK
    gq = LANE // S5_CH
    nq = S5_GROUPS // gq
    lam = lax.complex(jnp.minimum(lam_re.astype(F32), S5_MAX_REAL), lam_im.astype(F32))
    step = jnp.exp(log_dt.astype(F32))[:, None]
    z = lam * step
    lam_bar = jnp.exp(z)
    b_bar = ((lam_bar - 1.0) / lam)[..., None] * lax.complex(b_re.astype(F32), b_im.astype(F32))
    c_mat = lax.complex(c_re.astype(F32), c_im.astype(F32))
    pw = jnp.exp(z[..., None] * jnp.arange(nb + 1, dtype=F32))
    cp = c_mat[:, None, :, :] * pw[..., :nb].transpose(0, 2, 1)[:, :, None, :]
    cp = jnp.concatenate([cp.real, -cp.imag], axis=-1).reshape(S5_GROUPS, nb * S5_CH, 2 * S5_STATE)
    bri = jnp.concatenate([b_bar.real, b_bar.imag], axis=1)
    kern = jnp.einsum('gnk,gki->gin', cp, bri, precision=HIGHEST)
    skip = (jnp.asarray(np.concatenate([np.eye(S5_CH), np.zeros((S5_CH, (nb - 1) * S5_CH))], axis=1), F32)[None]
            * d_skip.astype(F32).reshape(S5_GROUPS, S5_CH, 1))
    k2 = (kern + skip).reshape(nq, gq * S5_CH, nb * S5_CH)
    pw_rev = jnp.exp(z[..., None] * jnp.asarray(np.arange(nb - 1, -1, -1), F32))
    binc = pw_rev[:, :, :, None] * b_bar[:, :, None, :]
    binc = jnp.stack([binc.real, binc.imag], axis=0).reshape(2, nq, gq * S5_STATE, nb * S5_CH)
    bc = binc.transpose(1, 0, 2, 3).reshape(nq, 2 * gq * S5_STATE, nb * S5_CH)
    cm = c_mat.transpose(0, 2, 1)[:, :, None, :] * pw[..., 1:][:, :, :, None]
    cm = jnp.stack([cm.real, -cm.imag], axis=0).reshape(2, nq, gq * S5_STATE, nb * S5_CH)
    cc = cm.transpose(1, 0, 2, 3).reshape(nq, 2 * gq * S5_STATE, nb * S5_CH)
    lam_n = pw[..., nb].reshape(nq, gq * S5_STATE)
    lam16 = jnp.stack([lam_n.real, lam_n.imag], axis=1)
    return k2, bc.astype(BF16), cc.astype(BF16), lam16


def _merge_kernel(x_ref, sc_ref, sh_ref, gm_ref, ys5_ref, yhg_ref, yret_ref, ym2_ref,
                  wglu_ref, wbr_ref, wg_ref, bg_ref, wout_ref, o_ref):
    x = x_ref[...]
    d = x.shape[1]
    h = _modulated_norm(x, sc_ref[...], sh_ref[...]).astype(BF16)
    y_s5 = jax.nn.gelu(ys5_ref[...])
    y_s5 = y_s5 * jax.nn.sigmoid(jnp.dot(y_s5.astype(BF16), wglu_ref[...], preferred_element_type=F32))
    acc = jnp.zeros(x.shape, F32)
    for n, y in enumerate((y_s5, yhg_ref[...], yret_ref[...], ym2_ref[...])):
        gate = jax.nn.sigmoid(jnp.dot(h, wg_ref[:, n * d:(n + 1) * d], preferred_element_type=F32)
                              + bg_ref[:, n * d:(n + 1) * d])
        acc = acc + gate * jnp.dot(y.astype(BF16), wbr_ref[n], preferred_element_type=F32)
    o_ref[...] = x + gm_ref[...] * jnp.dot(acc.astype(BF16), wout_ref[...], preferred_element_type=F32)


def _merge(x2, mod3, ys5, yhg, yret, ym2, w_glu, w_branch, w_gate, b_gate, w_out, layer, seq):
    t, d = x2.shape
    tm = TM_PROJ
    tpb = seq // tm
    const = lambda shape: pl.BlockSpec((None,) + shape, lambda i: (layer,) + (0,) * len(shape),
                                       pipeline_mode=pl.Buffered(1))
    modspec = lambda k: pl.BlockSpec((None, 1, d), lambda i: ((i // tpb) * 6 + k, 0, 0))
    yspec = pl.BlockSpec((tm, BRANCH_W), lambda i: (i, 0))
    return pl.pallas_call(
        _merge_kernel,
        grid=(t // tm,),
        in_specs=[pl.BlockSpec((tm, d), lambda i: (i, 0)), modspec(1), modspec(0), modspec(2),
                  yspec, yspec, yspec, yspec,
                  const((BRANCH_W, BRANCH_W)), const((4, BRANCH_W, d)), const((d, 4 * d)), const((1, 4 * d)),
                  const((d, d))],
        out_specs=pl.BlockSpec((tm, d), lambda i: (i, 0)),
        out_shape=jax.ShapeDtypeStruct((t, d), F32),
        compiler_params=_cparams(("parallel",)),
        name="merge",
    )(x2, mod3, mod3, mod3, ys5, yhg, yret, ym2, w_glu, w_branch, w_gate, b_gate, w_out)


def _router_kernel(x_ref, sc_ref, sh_ref, wr_ref, br_ref, tri_ref, h_ref, ids_ref, wts_ref, cnt_ref, carry):
    i = pl.program_id(0)

    @pl.when(i == 0)
    def _():
        carry[...] = jnp.zeros_like(carry)

    h = _modulated_norm(x_ref[...], sc_ref[...], sh_ref[...])
    tm, d = h.shape
    packed = _pack_bf16_pairs(h)
    for k in range(N_SLAB):
        h_ref[k] = packed[:, k * SLAB:(k + 1) * SLAB]
    h_hi = h.astype(BF16)
    h_lo = (h - h_hi.astype(F32)).astype(BF16)
    w_r = wr_ref[...]
    w_hi = w_r.astype(BF16)
    w_lo = (w_r - w_hi.astype(F32)).astype(BF16)
    logits = _dot_nt(w_hi, h_hi) + _dot_nt(w_hi, h_lo) + _dot_nt(w_lo, h_hi) + br_ref[:, 0:1]
    gl = [logits[g:g + 1, :] for g in range(MOE_GROUPS)]
    gmax = gl[0]
    gsel = jnp.zeros((1, tm), jnp.int32)
    for g in range(1, MOE_GROUPS):
        better = gl[g] > gmax
        gsel = jnp.where(better, g, gsel)
        gmax = jnp.where(better, gl[g], gmax)
    gden = gl[0] * 0.0
    for g in range(MOE_GROUPS):
        gden = gden + jnp.exp(gl[g] - gmax)
    g_w = 1.0 / gden
    el = []
    for e in range(MOE_EPG):
        v = logits[MOE_GROUPS + e:MOE_GROUPS + e + 1, :]
        for g in range(1, MOE_GROUPS):
            row = MOE_GROUPS + g * MOE_EPG + e
            v = jnp.where(gsel == g, logits[row:row + 1, :], v)
        el.append(v)
    v1 = el[0]
    i1 = jnp.zeros((1, tm), jnp.int32)
    for e in range(1, MOE_EPG):
        better = el[e] > v1
        i1 = jnp.where(better, e, i1)
        v1 = jnp.where(better, el[e], v1)
    v2 = jnp.full((1, tm), -jnp.inf, F32)
    i2 = jnp.zeros((1, tm), jnp.int32)
    for e in range(MOE_EPG):
        better = (el[e] > v2) & (i1 != e)
        i2 = jnp.where(better, e, i2)
        v2 = jnp.where(better, el[e], v2)
    ex = jnp.exp(v2 - v1)
    p1 = 1.0 / (1.0 + ex)
    e1 = gsel * MOE_EPG + i1
    e2 = gsel * MOE_EPG + i2
    erow = lax.broadcasted_iota(jnp.int32, (MOE_EXPERTS, tm), 0)
    oh1 = (erow == e1).astype(F32)
    oh2 = (erow == e2).astype(F32)
    both = oh1 + oh2
    n_grp = tm // LANE
    stacked = jnp.concatenate([both[:, g * LANE:(g + 1) * LANE] for g in range(n_grp)], axis=0)
    within = jnp.dot(stacked.astype(BF16), tri_ref[...], preferred_element_type=F32)
    grp_count = jnp.sum(stacked, axis=1, keepdims=True)
    running = carry[:, 0:1]
    pieces = []
    for g in range(n_grp):
        pieces.append(within[g * MOE_EXPERTS:(g + 1) * MOE_EXPERTS, :] + running)
        running = running + grp_count[g * MOE_EXPERTS:(g + 1) * MOE_EXPERTS, :]
    prefix = jnp.concatenate(pieces, axis=1)
    rank1 = jnp.sum(oh1 * prefix, axis=0, keepdims=True).astype(jnp.int32)
    rank2 = jnp.sum(oh2 * prefix, axis=0, keepdims=True).astype(jnp.int32)
    carry[...] = jnp.broadcast_to(running, carry.shape)
    zi = jnp.zeros((1, tm), jnp.int32)
    ids_ref[...] = jnp.concatenate([e1, e2, rank1, rank2, zi, zi, zi, zi], axis=0)
    wrow = lax.broadcasted_iota(jnp.int32, (LANE, tm), 0)
    wts_ref[...] = jnp.where(wrow == 0, p1 * g_w, jnp.where(wrow == 1, ex * p1 * g_w, 0.0)).T
    cnt_ref[...] = carry[...]


def _router(x2, mod3, w_route, b_route, tri_excl, seq):
    t, d = x2.shape
    tm = TM_PROJ
    tpb = seq // tm
    nr = w_route.shape[0]
    const = lambda shape: pl.BlockSpec(shape, lambda i: (0,) * len(shape))
    modspec = lambda k: pl.BlockSpec((None, 1, d), lambda i: ((i // tpb) * 6 + k, 0, 0))
    return pl.pallas_call(
        _router_kernel,
        grid=(t // tm,),
        in_specs=[pl.BlockSpec((tm, d), lambda i: (i, 0)), modspec(4), modspec(3),
                  const((nr, d)), const((nr, LANE)), const((LANE, LANE))],
        out_specs=[pl.BlockSpec((N_SLAB, tm, SLAB), lambda i: (0, i, 0)),
                   pl.BlockSpec((8, tm), lambda i: (0, i)),
                   pl.BlockSpec((tm, LANE), lambda i: (i, 0)),
                   const((MOE_EXPERTS, LANE))],
        out_shape=[jax.ShapeDtypeStruct((N_SLAB, t, SLAB), jnp.uint32),
                   jax.ShapeDtypeStruct((8, t), jnp.int32),
                   jax.ShapeDtypeStruct((t, LANE), F32),
                   jax.ShapeDtypeStruct((MOE_EXPERTS, LANE), F32)],
        scratch_shapes=[pltpu.VMEM((MOE_EXPERTS, LANE), F32)],
        compiler_params=_cparams(("arbitrary",)),
        name="moe_router",
    )(x2, mod3, mod3, w_route, b_route, tri_excl)


def _sc_mesh():
    return plsc.VectorSubcoreMesh(core_axis_name="core", subcore_axis_name="subcore")


def _slab_rows(idx, n_rows):
    return (idx[None, :] + (jnp.arange(N_SLAB, dtype=jnp.int32) * n_rows)[:, None]).reshape(-1)


def _dispatch(slot1, slot2, h_slabs):
    n_slab, t, d = h_slabs.shape
    n_out = 2 * t
    xs = _scatter_rows(h_slabs.reshape(n_slab * t, d), _slab_rows(slot1, n_out), _slab_rows(slot2, n_out),
                       n_slab * n_out)
    return xs.reshape(n_slab, n_out, d)


def _scatter_rows(src, idx1, idx2, n_out):
    t, d = src.shape
    win = SC_WINDOW

    @pl.kernel(out_type=jax.ShapeDtypeStruct((n_out, d), src.dtype), mesh=_sc_mesh(), name="moe_dispatch_sc")
    def scatter_rows(x_hbm, i1_hbm, i2_hbm, o_hbm):
        def body(x_vmem, i1_vmem, i2_vmem):
            pltpu.sync_copy(x_vmem, o_hbm.at[i1_vmem.at[0]])
            pltpu.sync_copy(x_vmem, o_hbm.at[i2_vmem.at[0]])

        pltpu.emit_pipeline(
            body,
            grid=(t // win,),
            in_specs=[pl.BlockSpec((win, d), lambda i: (i, 0)),
                      pl.BlockSpec((1, win), lambda i: (0, i)),
                      pl.BlockSpec((1, win), lambda i: (0, i))],
            out_specs=[],
            core_axis_name=("core", "subcore"),
            dimension_semantics=(pltpu.PARALLEL,),
        )(x_hbm, i1_hbm, i2_hbm)

    return scatter_rows(src, idx1.reshape(1, t), idx2.reshape(1, t))


def _gather_rows(src, idx):
    m = idx.shape[0]
    d = src.shape[1]
    win = SC_WINDOW

    @pl.kernel(out_type=jax.ShapeDtypeStruct((m, d), src.dtype), mesh=_sc_mesh(), name="moe_gather_sc")
    def gather(x_hbm, i_hbm, o_hbm):
        def body(i_vmem, o_vmem):
            pltpu.sync_copy(x_hbm.at[i_vmem.at[0]], o_vmem)

        pltpu.emit_pipeline(
            body,
            grid=(m // win,),
            in_specs=[pl.BlockSpec((1, win), lambda i: (0, i))],
            out_specs=[pl.BlockSpec((win, d), lambda i: (i, 0))],
            core_axis_name=("core", "subcore"),
            dimension_semantics=(pltpu.PARALLEL,),
        )(i_hbm, o_hbm)

    return gather(src, idx.reshape(1, m))


def _expert_kernel(tile_ref, exp_ref, lo_ref, hi_ref, xs_ref, w1_ref, w3_ref, w2_ref, ys_ref, w1_scr, w3_scr, w2_scr):
    s = pl.program_id(0)
    prev = jnp.maximum(s - 1, 0)
    new_expert = (s == 0) | (exp_ref[s] != exp_ref[prev])
    new_tile = (s == 0) | (tile_ref[s] != tile_ref[prev])

    @pl.when(new_expert)
    def _():
        w1_scr[...] = w1_ref[...].astype(BF16)
        w3_scr[...] = w3_ref[...].astype(BF16)
        w2_scr[...] = w2_ref[...].astype(BF16)

    lo = lo_ref[s]
    hi = hi_ref[s]
    n_rows = xs_ref.shape[1]

    @pl.when(new_tile)
    def _():
        ys_ref[...] = jnp.zeros_like(ys_ref)

    def run_rows(r0, n):
        x = _unpack_bf16_pairs(jnp.concatenate([xs_ref[k, r0:r0 + n, :] for k in range(N_SLAB)], axis=-1)).astype(BF16)
        a = jnp.dot(x, w1_scr[...], preferred_element_type=F32)
        b = jnp.dot(x, w3_scr[...], preferred_element_type=F32)
        act = _silu(a) * b
        y = _pack_bf16_pairs(jnp.dot(act.astype(BF16), w2_scr[...], preferred_element_type=F32))
        row = lax.broadcasted_iota(jnp.int32, (n, SLAB), 0) + r0
        mine = (row >= lo) & (row < hi)
        for k in range(N_SLAB):
            ys_ref[k, r0:r0 + n, :] = jnp.where(mine, y[:, k * SLAB:(k + 1) * SLAB], ys_ref[k, r0:r0 + n, :])

    groups = (hi + (X_SUB - 1)) // X_SUB - lo // X_SUB
    whole = groups > 2

    @pl.when(whole)
    def _():
        run_rows(0, n_rows)

    for r0 in range(0, n_rows, X_SUB):
        @pl.when(jnp.logical_not(whole) & (lo < r0 + X_SUB) & (hi > r0))
        def _(r0=r0):
            run_rows(r0, X_SUB)


def _experts(step_tile, step_expert, step_lo, step_hi, xs, w1, w3, w2, layer):
    n_slab, ns, slab = xs.shape
    d = w1.shape[1]
    ff = w1.shape[2]
    n_steps = step_tile.shape[0]
    base = layer * MOE_EXPERTS
    grid_spec = pltpu.PrefetchScalarGridSpec(
        num_scalar_prefetch=4,
        grid=(n_steps,),
        in_specs=[pl.BlockSpec((n_slab, TM_X, slab), lambda s, tl, ex, lo, hi: (0, tl[s], 0)),
                  pl.BlockSpec((None, d, ff), lambda s, tl, ex, lo, hi: (base + ex[s], 0, 0)),
                  pl.BlockSpec((None, d, ff), lambda s, tl, ex, lo, hi: (base + ex[s], 0, 0)),
                  pl.BlockSpec((None, ff, d), lambda s, tl, ex, lo, hi: (base + ex[s], 0, 0))],
        out_specs=pl.BlockSpec((n_slab, TM_X, slab), lambda s, tl, ex, lo, hi: (0, tl[s], 0)),
        scratch_shapes=[pltpu.VMEM((d, ff), BF16), pltpu.VMEM((d, ff), BF16), pltpu.VMEM((ff, d), BF16)],
    )
    return pl.pallas_call(
        _expert_kernel,
        grid_spec=grid_spec,
        out_shape=jax.ShapeDtypeStruct((n_slab, ns, slab), xs.dtype),
        compiler_params=_cparams(("arbitrary",)),
        name="moe_experts",
    )(step_tile, step_expert, step_lo, step_hi, xs, w1, w3, w2)


def _combine_kernel(x_ref, gate_ref, fw_ref, wcol_ref, y1_ref, y2_ref, o_ref, *, final):
    w_first = wcol_ref[:, 0:1]
    w_second = wcol_ref[:, 1:2]
    y_first = _unpack_bf16_pairs(jnp.concatenate([y1_ref[k] for k in range(N_SLAB)], axis=-1))
    y_second = _unpack_bf16_pairs(jnp.concatenate([y2_ref[k] for k in range(N_SLAB)], axis=-1))
    moe = w_first * y_first + w_second * y_second
    x = x_ref[...] + gate_ref[...] * moe
    if final:
        x = x * lax.rsqrt(jnp.mean(x * x, axis=-1, keepdims=True) + EPS) * fw_ref[...]
    o_ref[...] = x


def _combine(x2, mod3, final_w_row, wcol, gathered, seq, final):
    t, d = x2.shape
    tm = TM_COMB
    tpb = seq // tm
    nblk = t // tm
    yspec = lambda off: pl.BlockSpec((N_SLAB, tm, SLAB), lambda i: (0, i + off, 0))
    return pl.pallas_call(
        functools.partial(_combine_kernel, final=final),
        grid=(nblk,),
        in_specs=[pl.BlockSpec((tm, d), lambda i: (i, 0)),
                  pl.BlockSpec((None, 1, d), lambda i: ((i // tpb) * 6 + 5, 0, 0)),
                  pl.BlockSpec((1, d), lambda i: (0, 0)),
                  pl.BlockSpec((tm, LANE), lambda i: (i, 0)),
                  yspec(0), yspec(nblk)],
        out_specs=pl.BlockSpec((tm, d), lambda i: (i, 0)),
        out_shape=jax.ShapeDtypeStruct((t, d), F32),
        compiler_params=_cparams(("parallel",)),
        name="moe_combine",
    )(x2, mod3, final_w_row, wcol, gathered, gathered)


def _moe(x2, mod3, final_w_row, w_route, b_route, tri_excl, w1, w3, w2, layer, seq, final):
    t, d = x2.shape
    h3, ids, wcol, counts = _router(x2, mod3, w_route, b_route, tri_excl, seq)
    cnt = counts[:, 0].astype(jnp.int32)
    ends = jnp.cumsum(cnt)
    offs = ends - cnt
    experts = jnp.arange(MOE_EXPERTS, dtype=jnp.int32)
    pick = lambda table, idx: jnp.sum(jnp.where(idx[:, None] == experts[None, :], table[None, :], 0), axis=1)
    slot1 = pick(offs, ids[0]) + ids[2]
    slot2 = pick(offs, ids[1]) + ids[3]
    n_tiles = 2 * t // TM_X
    first_tile = offs // TM_X
    n_vis = jnp.where(cnt > 0, (ends - 1) // TM_X - first_tile + 1, 0)
    cum = jnp.cumsum(n_vis)
    step = jnp.arange(n_tiles + MOE_EXPERTS, dtype=jnp.int32)
    step_expert = jnp.minimum(jnp.sum(step[:, None] >= cum[None, :], axis=1), MOE_EXPERTS - 1).astype(jnp.int32)
    valid = step < cum[-1]
    step_tile = jnp.where(valid, pick(first_tile - (cum - n_vis), step_expert) + step, n_tiles - 1)
    step_lo = jnp.where(valid, jnp.clip(pick(offs, step_expert) - step_tile * TM_X, 0, TM_X), 0)
    step_hi = jnp.where(valid, jnp.clip(pick(ends, step_expert) - step_tile * TM_X, 0, TM_X), 0)
    xs = _dispatch(slot1, slot2, h3)
    ys = _experts(step_tile.astype(jnp.int32), step_expert, step_lo.astype(jnp.int32), step_hi.astype(jnp.int32),
                  xs, w1, w3, w2, layer)
    n_sorted = ys.shape[1]
    gathered = _gather_rows(ys.reshape(N_SLAB * n_sorted, SLAB), _slab_rows(jnp.concatenate([slot1, slot2]), n_sorted))
    gathered = gathered.reshape(N_SLAB, n_sorted, SLAB)
    return _combine(x2, mod3, final_w_row, wcol, gathered, seq, final)


def kernel(x, c, positions, ada_w, ada_b, w_in, s5_lam_re, s5_lam_im, s5_b_re, s5_b_im, s5_c_re, s5_c_im, s5_d, s5_log_dt, s5_w_glu, hg_lb_logits, hg_norm_w, m2_conv_w, m2_conv_b, m2_dt_bias, m2_a_log, m2_d, m2_norm_w, w_branch, w_gate, b_gate, w_out, moe_w_group, moe_b_group, moe_w_expert, moe_b_expert, moe_w1, moe_w3, moe_w2, final_norm_w):
    bsz, seq, d = x.shape
    t = bsz * seq
    depth = ada_w.shape[0]
    assert seq % TM_PROJ == 0 and seq % C_RET == 0 and seq % C_SSD == 0 and seq % C_HG == 0
    x2 = x.reshape(t, d).astype(F32)

    c_pad = jnp.zeros((8, d), F32).at[:bsz].set(c.astype(F32))
    mod_all = _ada_mod(c_pad.T, ada_w.astype(F32), ada_b.astype(F32), bsz)

    half = RET_DK // 2
    inv_freq = ROPE_BASE ** (-jnp.arange(half, dtype=F32) / half)
    invf_col = jnp.broadcast_to(inv_freq[:, None], (half, LANE))
    expand = np.tile(np.eye(half, dtype=np.float32), (1, 2 * RET_HEADS))
    sign = np.tile(np.concatenate([-np.ones(half), np.ones(half)]), RET_HEADS)[None, :].astype(np.float32)
    cos_t, sin_t = _rope_tables(positions.reshape(1, t).astype(jnp.int32), invf_col,
                                jnp.asarray(expand, BF16), jnp.asarray(expand * sign, BF16))
    cos3 = cos_t.reshape(bsz, seq, -1)
    sin3 = sin_t.reshape(bsz, seq, -1)

    lb_cum = jnp.cumsum(jax.nn.softmax(hg_lb_logits.astype(F32), axis=0), axis=0)
    hg_lb = lb_cum - lb_cum[:1]
    tri_ssd = jnp.asarray(np.tril(np.ones((C_SSD, C_SSD), np.float32)), BF16)
    tri_excl = jnp.asarray(np.triu(np.ones((LANE, LANE), np.float32), 1), BF16)
    final_w_row = final_norm_w.astype(F32)[None, :]
    n_main = IN_W // LANE * LANE
    w_main = w_in[:, :, :n_main].astype(BF16)
    w_tail = jnp.zeros((depth, d, IN_W_PAD - n_main), BF16).at[:, :, :IN_W - n_main].set(w_in[:, :, n_main:].astype(BF16))
    w_glu_bf = s5_w_glu.astype(BF16)
    w_branch_bf = w_branch.astype(BF16)
    w_gate_bf = w_gate.astype(BF16)
    w_out_bf = w_out.astype(BF16)
    b_gate3 = b_gate.astype(F32).reshape(depth, 1, -1)
    moe_w1_all = moe_w1.astype(F32).reshape(depth * MOE_EXPERTS, d, MOE_FF)
    moe_w3_all = moe_w3.astype(F32).reshape(depth * MOE_EXPERTS, d, MOE_FF)
    moe_w2_all = moe_w2.astype(F32).reshape(depth * MOE_EXPERTS, MOE_FF, d)

    for layer in range(depth):
        mod3 = mod_all[layer, :bsz].reshape(bsz * 6, 1, d)
        p, u_s5 = _in_proj(x2, mod3, w_main, w_tail, layer, seq)
        p3 = p.reshape(bsz, seq, IN_W_PAD)

        ops = _s5_operators(s5_lam_re[layer], s5_lam_im[layer], s5_b_re[layer], s5_b_im[layer],
                            s5_c_re[layer], s5_c_im[layer], s5_d[layer], s5_log_dt[layer])
        y_s5 = _s5_scan(u_s5.reshape(bsz, seq, BRANCH_W), *ops).reshape(t, BRANCH_W)

        lb = hg_lb[layer][None, :]
        y_hg = _hgrn2(p3, jnp.log(lb), jnp.log1p(-lb), hg_norm_w[layer].astype(F32)[None, :]).reshape(t, BRANCH_W)

        y_ret = _retention(p3, cos3, sin3).reshape(t, BRANCH_W)

        pad8 = lambda v: jnp.zeros((1, LANE), F32).at[0, :M2_HEADS].set(v.astype(F32))
        y_m2 = _ssd(p3, tri_ssd, m2_conv_w[layer].astype(F32), m2_conv_b[layer].astype(F32)[None, :],
                    pad8(m2_dt_bias[layer]), pad8(m2_a_log[layer]),
                    jnp.repeat(m2_d[layer].astype(F32), M2_HEADDIM)[None, :],
                    m2_norm_w[layer].astype(F32)[None, :]).reshape(t, BRANCH_W)

        nr = 40
        w_route = jnp.zeros((nr, d), F32).at[:MOE_GROUPS].set(moe_w_group[layer].astype(F32).T)
        w_route = w_route.at[MOE_GROUPS:MOE_GROUPS + MOE_EXPERTS].set(moe_w_expert[layer].astype(F32).T)
        b_route = jnp.zeros((nr, LANE), F32).at[:MOE_GROUPS, 0].set(moe_b_group[layer].astype(F32))
        b_route = b_route.at[MOE_GROUPS:MOE_GROUPS + MOE_EXPERTS, 0].set(moe_b_expert[layer].astype(F32))
        x2 = _merge(x2, mod3, y_s5, y_hg, y_ret, y_m2, w_glu_bf, w_branch_bf, w_gate_bf, b_gate3, w_out_bf,
                    layer, seq)
        x2 = _moe(x2, mod3, final_w_row, w_route, b_route, tri_excl, moe_w1_all, moe_w3_all, moe_w2_all,
                  layer, seq, final=(layer == depth - 1))
    return x2.reshape(bsz, seq, d)
```

```python
import functools
import math

import numpy as np
import jax
import jax.numpy as jnp
from jax import lax
from jax.experimental import pallas as pl
from jax.experimental.pallas import tpu as pltpu
from jax.experimental.pallas import tpu_sc as plsc

F32 = jnp.float32
BF16 = jnp.bfloat16
HIGHEST = lax.Precision.HIGHEST

D_MODEL = 1024
BRANCH_W = 512
EPS = 1e-6
S5_GROUPS = 32
S5_CH = 16
S5_STATE = 64
S5_MAX_REAL = -1e-4
S5_BLOCK = 8
S5_SEQ_PER_STEP = 2
HG_HEADS = 4
HG_DK = 128
RET_HEADS = 4
RET_DK = 64
RET_DV = 128
ROPE_BASE = 10000.0
M2_HEADS = 8
M2_HEADDIM = 64
M2_GROUPS = 2
M2_STATE = 128
M2_CONV = 4
MOE_GROUPS = 4
MOE_EPG = 8
MOE_EXPERTS = MOE_GROUPS * MOE_EPG
MOE_FF = 256

COL_S5, COL_HQ, COL_HF, COL_HI, COL_HG = 0, 512, 1024, 1536, 2048
COL_RQ, COL_RK, COL_RV, COL_RG = 2560, 2816, 3072, 3584
COL_MZ, COL_MXS, COL_MBC, COL_MDT = 4096, 4608, 5120, 5632
IN_W = 5640
IN_W_PAD = 5760

LANE = 128
VMEM_LIMIT = 56 * 1024 * 1024

TM_PROJ = 1024
TN_PROJ = 1024
TM_INPROJ = 512
LOG2_E = 1.4426950408889634
C_RET = 512
C_SSD = 256
C_HG = 128
TM_X = 512
X_SUB = 128
TM_COMB = 1024
SC_WINDOW = 128
SLAB = 256
N_SLAB = D_MODEL // 2 // SLAB


def _cparams(sem):
    return pltpu.CompilerParams(dimension_semantics=sem, vmem_limit_bytes=VMEM_LIMIT)


def _silu(v):
    return v * jax.nn.sigmoid(v)


def _dot_nt(a, b, **kw):
    return lax.dot_general(a, b, (((1,), (1,)), ((), ())), preferred_element_type=F32, **kw)


def _dot_tn(a, b, **kw):
    return lax.dot_general(a, b, (((0,), (0,)), ((), ())), preferred_element_type=F32, **kw)


def _ada_kernel(ct_ref, w_ref, b_ref, o_ref, *, n_rows):
    cond_t = _silu(ct_ref[...])
    w = w_ref[...]
    rows = [jnp.sum(w * cond_t[:, b:b + 1], axis=0, keepdims=True) for b in range(n_rows)]
    rows += [jnp.zeros_like(rows[0])] * (cond_t.shape[1] - n_rows)
    o_ref[...] = jnp.concatenate(rows, axis=0) + b_ref[...]


def _ada_mod(c_pad_t, ada_w, ada_b, n_rows):
    depth, d, n = ada_w.shape
    tn = 1536
    return pl.pallas_call(
        functools.partial(_ada_kernel, n_rows=n_rows),
        grid=(depth, n // tn),
        in_specs=[pl.BlockSpec((d, 8), lambda l, j: (0, 0)),
                  pl.BlockSpec((None, d, tn), lambda l, j: (l, 0, j)),
                  pl.BlockSpec((None, 1, tn), lambda l, j: (l, 0, j))],
        out_specs=pl.BlockSpec((None, 8, tn), lambda l, j: (l, 0, j)),
        out_shape=jax.ShapeDtypeStruct((depth, 8, n), F32),
        compiler_params=_cparams(("parallel", "parallel")),
        name="ada_mod",
    )(c_pad_t, ada_w, ada_b.reshape(depth, 1, n))


def _pack_bf16_pairs(x):
    n = x.shape[1] // 2
    lo = pltpu.bitcast(x[:, :n].astype(BF16).astype(F32), jnp.uint32) >> 16
    hi = pltpu.bitcast(x[:, n:].astype(BF16).astype(F32), jnp.uint32)
    return hi | lo


def _unpack_bf16_pairs(w):
    lo = pltpu.bitcast(w << 16, F32)
    hi = pltpu.bitcast(w & jnp.uint32(0xFFFF0000), F32)
    return jnp.concatenate([lo, hi], axis=-1)


def _modulated_norm(x, scale, shift):
    ms = jnp.mean(x * x, axis=-1, keepdims=True)
    return x * lax.rsqrt(ms + EPS) * (1.0 + scale) + shift


def _inproj_kernel(x_ref, sc_ref, sh_ref, w_ref, wtail_ref, o_ref, u_ref):
    h = _modulated_norm(x_ref[...], sc_ref[...], sh_ref[...]).astype(BF16)
    n_main = w_ref.shape[1]
    for n0 in range(0, n_main, TN_PROJ):
        n1 = min(n0 + TN_PROJ, n_main)
        p = jnp.dot(h, w_ref[:, n0:n1], preferred_element_type=F32)
        o_ref[:, n0:n1] = p.astype(o_ref.dtype)
        if n0 == 0:
            u_ref[...] = p[:, COL_S5:COL_S5 + BRANCH_W]
    o_ref[:, n_main:] = jnp.dot(h, wtail_ref[...], preferred_element_type=F32).astype(o_ref.dtype)


def _in_proj(x2, mod3, w_main, w_tail, layer, seq):
    t, d = x2.shape
    tm = TM_INPROJ
    tpb = seq // tm
    n_main = w_main.shape[2]
    assert COL_S5 + BRANCH_W <= TN_PROJ and n_main + w_tail.shape[2] == IN_W_PAD
    return pl.pallas_call(
        _inproj_kernel,
        grid=(t // tm,),
        in_specs=[pl.BlockSpec((tm, d), lambda i: (i, 0)),
                  pl.BlockSpec((None, 1, d), lambda i: ((i // tpb) * 6 + 1, 0, 0)),
                  pl.BlockSpec((None, 1, d), lambda i: ((i // tpb) * 6 + 0, 0, 0)),
                  pl.BlockSpec((None, d, n_main), lambda i: (layer, 0, 0), pipeline_mode=pl.Buffered(1)),
                  pl.BlockSpec((None, d, IN_W_PAD - n_main), lambda i: (layer, 0, 0), pipeline_mode=pl.Buffered(1))],
        out_specs=[pl.BlockSpec((tm, IN_W_PAD), lambda i: (i, 0)),
                   pl.BlockSpec((tm, BRANCH_W), lambda i: (i, 0))],
        out_shape=[jax.ShapeDtypeStruct((t, IN_W_PAD), BF16), jax.ShapeDtypeStruct((t, BRANCH_W), F32)],
        compiler_params=_cparams(("parallel",)),
        name="in_proj",
    )(x2, mod3, mod3, w_main, w_tail)


def _rope_kernel(pos_ref, invf_ref, ecos_ref, esin_ref, cos_ref, sin_ref):
    ang = invf_ref[:, 0:1] * pos_ref[...].astype(F32)
    def spread(values, e_ref):
        hi = values.astype(BF16)
        rest = values - hi.astype(F32)
        mid = rest.astype(BF16)
        lo = (rest - mid.astype(F32)).astype(BF16)
        e = e_ref[...]
        return _dot_tn(hi, e) + _dot_tn(mid, e) + _dot_tn(lo, e)

    cos_ref[...] = spread(jnp.cos(ang), ecos_ref)
    sin_ref[...] = spread(jnp.sin(ang), esin_ref)


def _rope_tables(pos_row, invf_col, expand_cos, expand_sin):
    t = pos_row.shape[1]
    half, w = expand_cos.shape
    tm = 1024
    const = lambda shape: pl.BlockSpec(shape, lambda i: (0, 0))
    return pl.pallas_call(
        _rope_kernel,
        grid=(t // tm,),
        in_specs=[pl.BlockSpec((1, tm), lambda i: (0, i)), const((half, LANE)), const((half, w)), const((half, w))],
        out_specs=[pl.BlockSpec((tm, w), lambda i: (i, 0))] * 2,
        out_shape=[jax.ShapeDtypeStruct((t, w), F32)] * 2,
        compiler_params=_cparams(("parallel",)),
        name="rope_tables",
    )(pos_row, invf_col, expand_cos, expand_sin)


def _ret_kernel(q_ref, k_ref, v_ref, g_ref, cos_ref, sin_ref, o_ref, st_ref, dec_ref, *, chunk):
    @pl.when(pl.program_id(1) == 0)
    def _():
        st_ref[...] = jnp.zeros_like(st_ref)
        ti = lax.broadcasted_iota(jnp.int32, (chunk, chunk), 0)
        si = lax.broadcasted_iota(jnp.int32, (chunk, chunk), 1)
        lag = (ti - si).astype(F32)
        for h in range(RET_HEADS):
            log_gamma = math.log1p(-(2.0 ** (-5.0 - h)))
            dec_ref[h] = jnp.where(ti >= si, jnp.exp(jnp.minimum(lag * log_gamma, 0.0)), 0.0)

    cosf = cos_ref[...]
    sinf = sin_ref[...]
    width = RET_HEADS * RET_DK
    lane = lax.broadcasted_iota(jnp.int32, (chunk, width), 1)
    first_half = (lane % RET_DK) < (RET_DK // 2)

    def rope(t):
        partner = jnp.where(first_half, pltpu.roll(t, width - RET_DK // 2, 1), pltpu.roll(t, RET_DK // 2, 1))
        return t * cosf + partner * sinf

    q = rope(q_ref[...].astype(F32))
    k = rope(k_ref[...].astype(F32)) * (RET_DK ** -0.5)
    v = v_ref[...]
    g = g_ref[...].astype(F32)
    tcol = lax.broadcasted_iota(jnp.int32, (chunk, 1), 0).astype(F32)
    for h in range(RET_HEADS):
        log_gamma = math.log1p(-(2.0 ** (-5.0 - h)))
        qh = q[:, h * RET_DK:(h + 1) * RET_DK]
        kh = k[:, h * RET_DK:(h + 1) * RET_DK]
        vh = v[:, h * RET_DV:(h + 1) * RET_DV].astype(BF16)
        scores = _dot_nt(qh.astype(BF16), kh.astype(BF16)) * dec_ref[h]
        state = st_ref[h]
        q_in = qh * jnp.exp(log_gamma * (tcol + 1.0))
        o = (jnp.dot(scores.astype(BF16), vh, preferred_element_type=F32)
             + jnp.dot(q_in.astype(BF16), state.astype(BF16), preferred_element_type=F32))
        k_out = kh * jnp.exp(log_gamma * (chunk - 1.0 - tcol))
        st_ref[h] = math.exp(log_gamma * chunk) * state + _dot_tn(k_out.astype(BF16), vh)
        o = o * lax.rsqrt(jnp.mean(o * o, axis=-1, keepdims=True) + EPS)
        gh = g[:, h * RET_DV:(h + 1) * RET_DV]
        o_ref[:, h * RET_DV:(h + 1) * RET_DV] = (o * _silu(gh)).astype(o_ref.dtype)


def _retention(p3, cos3, sin3):
    b, seq, _ = p3.shape
    c = C_RET
    qk_w = RET_HEADS * RET_DK
    return pl.pallas_call(
        functools.partial(_ret_kernel, chunk=c),
        grid=(b, seq // c),
        in_specs=[pl.BlockSpec((None, c, qk_w), lambda i, j: (i, j, COL_RQ // qk_w)),
                  pl.BlockSpec((None, c, qk_w), lambda i, j: (i, j, COL_RK // qk_w)),
                  pl.BlockSpec((None, c, BRANCH_W), lambda i, j: (i, j, COL_RV // BRANCH_W)),
                  pl.BlockSpec((None, c, BRANCH_W), lambda i, j: (i, j, COL_RG // BRANCH_W)),
                  pl.BlockSpec((None, c, qk_w), lambda i, j: (i, j, 0)),
                  pl.BlockSpec((None, c, qk_w), lambda i, j: (i, j, 0))],
        out_specs=pl.BlockSpec((None, c, BRANCH_W), lambda i, j: (i, j, 0)),
        out_shape=jax.ShapeDtypeStruct((b, seq, BRANCH_W), BF16),
        scratch_shapes=[pltpu.VMEM((RET_HEADS, RET_DK, RET_DV), F32), pltpu.VMEM((RET_HEADS, c, c), F32)],
        compiler_params=_cparams(("parallel", "arbitrary")),
        name="retention",
    )(p3, p3, p3, p3, cos3, sin3)


def _ssd_kernel(z_ref, xs_ref, bc_ref, dt_ref, tri_ref, cw_ref, cb_ref, dtb_ref, alog_ref, dsk_ref, nw_ref,
                o_ref, xe_scr, st_ref, *, chunk):
    j = pl.program_id(1)
    width = 2 * BRANCH_W

    @pl.when(j == 0)
    def _():
        st_ref[...] = jnp.zeros_like(st_ref)
        xe_scr[0:8, :] = jnp.zeros((8, width), F32)

    @pl.when(j > 0)
    def _():
        xe_scr[0:8, :] = xe_scr[chunk:chunk + 8, :]

    xe_scr[8:, 0:BRANCH_W] = xs_ref[...].astype(F32)
    xe_scr[8:, BRANCH_W:] = bc_ref[...].astype(F32)
    conv = cb_ref[...] + cw_ref[M2_CONV - 1:M2_CONV, :] * xe_scr[8:, :]
    for tap in range(M2_CONV - 1):
        conv = conv + cw_ref[tap:tap + 1, :] * xe_scr[pl.ds(8 - (M2_CONV - 1) + tap, chunk), :]
    conv = _silu(conv)
    xs = conv[:, :BRANCH_W]
    bm = conv[:, BRANCH_W:BRANCH_W + M2_GROUPS * M2_STATE]
    cm = conv[:, BRANCH_W + M2_GROUPS * M2_STATE:]

    dt = jax.nn.softplus(dt_ref[...].astype(F32) + dtb_ref[...])
    da = dt * (-jnp.exp(alog_ref[...]))
    da_hi = da.astype(BF16)
    da_r = da - da_hi.astype(F32)
    da_mid = da_r.astype(BF16)
    da_lo = (da_r - da_mid.astype(F32)).astype(BF16)
    tri = tri_ref[...]
    a_cs = (jnp.dot(tri, da_hi, preferred_element_type=F32) + jnp.dot(tri, da_mid, preferred_element_type=F32)
            + jnp.dot(tri, da_lo, preferred_element_type=F32))
    a_cs = a_cs * LOG2_E
    a_cs_t = a_cs.T
    ti = lax.broadcasted_iota(jnp.int32, (chunk, chunk), 0)
    si = lax.broadcasted_iota(jnp.int32, (chunk, chunk), 1)
    causal = ti >= si
    hpg = M2_HEADS // M2_GROUPS
    pair_w = 2 * M2_HEADDIM
    upper = lax.broadcasted_iota(jnp.int32, (chunk, pair_w), 1) >= M2_HEADDIM
    upper_state = lax.broadcasted_iota(jnp.int32, (M2_STATE, pair_w), 1) >= M2_HEADDIM
    ys = []
    for grp in range(M2_GROUPS):
        bm_g = bm[:, grp * M2_STATE:(grp + 1) * M2_STATE]
        cm_g = cm[:, grp * M2_STATE:(grp + 1) * M2_STATE]
        bm_bf = bm_g.astype(BF16)
        cb = _dot_nt(cm_g.astype(BF16), bm_bf)
        for pp in range(hpg // 2):
            h0 = grp * hpg + 2 * pp
            pair = h0 // 2
            xd = xs[:, pair * pair_w:(pair + 1) * pair_w] * jnp.where(upper, dt[:, h0 + 1:h0 + 2], dt[:, h0:h0 + 1])
            xd_bf = xd.astype(BF16)
            state = st_ref[pair]
            state_bf = state.astype(BF16)
            y_heads, state_heads = [], []
            for h in (h0, h0 + 1):
                col = a_cs[:, h:h + 1]
                row = a_cs_t[h:h + 1, :]
                lmat = jnp.where(causal, jnp.exp2(col - row), 0.0)
                y_heads.append(jnp.dot((cb * lmat).astype(BF16), xd_bf, preferred_element_type=F32)
                               + jnp.dot((cm_g * jnp.exp2(col)).astype(BF16), state_bf, preferred_element_type=F32))
                a_last = a_cs[chunk - 1:chunk, h:h + 1]
                to_end = jnp.exp2(a_last - col)
                state_heads.append(jnp.exp2(a_last) * state + _dot_tn(bm_bf, (xd * to_end).astype(BF16)))
            ys.append(jnp.where(upper, y_heads[1], y_heads[0]))
            st_ref[pair] = jnp.where(upper_state, state_heads[1], state_heads[0])
    y = jnp.concatenate(ys, axis=-1) + dsk_ref[...] * xs
    y = y * _silu(z_ref[...].astype(F32))
    o_ref[...] = (y * lax.rsqrt(jnp.mean(y * y, axis=-1, keepdims=True) + EPS) * nw_ref[...]).astype(o_ref.dtype)


def _ssd(p3, tri, conv_w, conv_b, dt_bias_row, a_log_row, d_skip_row, norm_w_row):
    b, seq, _ = p3.shape
    c = C_SSD
    const = lambda shape: pl.BlockSpec(shape, lambda i, j: (0,) * len(shape))
    return pl.pallas_call(
        functools.partial(_ssd_kernel, chunk=c),
        grid=(b, seq // c),
        in_specs=[pl.BlockSpec((None, c, BRANCH_W), lambda i, j: (i, j, COL_MZ // BRANCH_W)),
                  pl.BlockSpec((None, c, BRANCH_W), lambda i, j: (i, j, COL_MXS // BRANCH_W)),
                  pl.BlockSpec((None, c, BRANCH_W), lambda i, j: (i, j, COL_MBC // BRANCH_W)),
                  pl.BlockSpec((None, c, LANE), lambda i, j: (i, j, COL_MDT // LANE)),
                  const((c, c)), const((M2_CONV, 2 * BRANCH_W)), const((1, 2 * BRANCH_W)),
                  const((1, LANE)), const((1, LANE)), const((1, BRANCH_W)), const((1, BRANCH_W))],
        out_specs=pl.BlockSpec((None, c, BRANCH_W), lambda i, j: (i, j, 0)),
        out_shape=jax.ShapeDtypeStruct((b, seq, BRANCH_W), BF16),
        scratch_shapes=[pltpu.VMEM((c + 8, 2 * BRANCH_W), F32),
                        pltpu.VMEM((M2_HEADS // 2, M2_STATE, 2 * M2_HEADDIM), F32)],
        compiler_params=_cparams(("parallel", "arbitrary")),
        name="ssd",
    )(p3, p3, p3, p3, tri, conv_w, conv_b, dt_bias_row, a_log_row, d_skip_row, norm_w_row)


def _hg_tables(chunk):
    n_lev = int(math.log2(chunk))
    r = np.arange(chunk)[:, None]
    jj = np.arange(chunk)[None, :]
    tri = (jj <= r).astype(np.float32)
    x = r ^ jj
    levmap = np.where(r > jj, np.floor(np.log2(x + 0.5)), np.where(r == jj, -1, -2)).astype(np.int32)
    return tri, levmap, n_lev


def _hg_level_exponent(b, lev):
    rows, width = b.shape
    m = 1 << lev
    sub = 8
    if 2 * m >= sub:
        blocks = b.reshape(rows // (2 * m), 2 * m, width)
        mid = jnp.broadcast_to(blocks[:, m - 1:m, :], blocks.shape).reshape(rows, width)
    else:
        groups = b.reshape(rows // sub, sub, width)
        row_in_group = lax.broadcasted_iota(jnp.int32, groups.shape, 1)
        mid = None
        for start in range(0, sub, 2 * m):
            picked = jnp.broadcast_to(groups[:, start + m - 1:start + m, :], groups.shape)
            mid = picked if mid is None else jnp.where(row_in_group >= start, picked, mid)
        mid = mid.reshape(rows, width)
    return -jnp.abs(b - mid)


def _hg_kernel(q_ref, f_ref, i_ref, g_ref, sum_ref, lev_ref, llb_ref, l1m_ref, nw_ref, o_ref, st_ref,
               *, chunk, n_lev):
    @pl.when(pl.program_id(1) == 0)
    def _():
        st_ref[...] = jnp.zeros_like(st_ref)

    f = f_ref[...].astype(F32)
    y = jnp.exp(-jnp.abs(f))
    one_plus_y = 1.0 + y
    log_sig = jnp.minimum(f, 0.0) - jnp.log(one_plus_y)
    a = llb_ref[...]
    bb = l1m_ref[...] + log_sig
    log_f = jnp.maximum(a, bb) + jnp.log(1.0 + jnp.exp(-jnp.abs(a - bb)))
    k_all = jnp.exp(l1m_ref[...]) * (jnp.where(f >= 0.0, y, 1.0) / one_plus_y)
    q_all = _silu(q_ref[...].astype(F32))
    hi = log_f.astype(BF16)
    r1 = log_f - hi.astype(F32)
    mid = r1.astype(BF16)
    lo = (r1 - mid.astype(F32)).astype(BF16)
    tri = sum_ref[...]
    b_all = (jnp.dot(tri, hi, preferred_element_type=F32)
             + jnp.dot(tri, mid, preferred_element_type=F32)
             + jnp.dot(tri, lo, preferred_element_type=F32))
    b_all = b_all * LOG2_E
    to_end_all = b_all[chunk - 1:chunk, :] - b_all
    level_decay = [jnp.exp2(_hg_level_exponent(b_all, lev)) for lev in range(n_lev)]
    levmap = lev_ref[...]
    on_diag = levmap == -1
    on_level = [levmap == lev for lev in range(n_lev)]
    v_all = i_ref[...]
    g_all = g_ref[...].astype(F32)
    for h in range(HG_HEADS):
        sl = slice(h * HG_DK, (h + 1) * HG_DK)
        qh = q_all[:, sl]
        kh = k_all[:, sl]
        vh = v_all[:, sl].astype(BF16)
        b_h = b_all[:, sl]
        to_end = to_end_all[:, sl]
        amat = jnp.where(on_diag, _dot_nt(qh.astype(BF16), kh.astype(BF16)), 0.0)
        for lev in range(n_lev):
            e = level_decay[lev][:, sl]
            a_l = _dot_nt((qh * e).astype(BF16), (kh * e).astype(BF16))
            amat = jnp.where(on_level[lev], a_l, amat)
        state_t = st_ref[h]
        o = (jnp.dot(amat.astype(BF16), vh, preferred_element_type=F32)
             + _dot_nt((qh * jnp.exp2(b_h)).astype(BF16), state_t.astype(BF16)))
        k_end = kh * jnp.exp2(to_end)
        st_ref[h] = jnp.exp2(b_h[chunk - 1:chunk, :]) * state_t + _dot_tn(vh, k_end.astype(BF16))
        o = o * lax.rsqrt(jnp.mean(o * o, axis=-1, keepdims=True) + EPS) * nw_ref[...]
        o_ref[:, sl] = (o * _silu(g_all[:, sl])).astype(o_ref.dtype)


def _hgrn2(p3, log_lb_row, log1m_lb_row, norm_w_row):
    b, seq, _ = p3.shape
    c = C_HG
    tri, levmap, n_lev = _hg_tables(c)
    const = lambda shape: pl.BlockSpec(shape, lambda i, j: (0,) * len(shape))
    blk = lambda col: pl.BlockSpec((None, c, BRANCH_W), lambda i, j: (i, j, col // BRANCH_W))
    return pl.pallas_call(
        functools.partial(_hg_kernel, chunk=c, n_lev=n_lev),
        grid=(b, seq // c),
        in_specs=[blk(COL_HQ), blk(COL_HF), blk(COL_HI), blk(COL_HG),
                  const((c, c)), const((c, c)),
                  const((1, BRANCH_W)), const((1, BRANCH_W)), const((1, HG_DK))],
        out_specs=pl.BlockSpec((None, c, BRANCH_W), lambda i, j: (i, j, 0)),
        out_shape=jax.ShapeDtypeStruct((b, seq, BRANCH_W), BF16),
        scratch_shapes=[pltpu.VMEM((HG_HEADS, HG_DK, HG_DK), F32)],
        compiler_params=_cparams(("parallel", "arbitrary")),
        name="hgrn2",
    )(p3, p3, p3, p3, jnp.asarray(tri, BF16), jnp.asarray(levmap), log_lb_row, log1m_lb_row, norm_w_row)


def _expand_block_diag(comp_ref, e_ref, dst_ref, row_div, lane_div, causal=False):
    gq = LANE // S5_CH
    rows, ncols = dst_ref.shape
    step = 512
    for c0 in range(0, ncols, step):
        r1 = min(rows, c0 + step) if causal else rows
        row_grp = (lax.broadcasted_iota(jnp.int32, (r1, step), 0) // row_div) % gq
        lane_grp = ((lax.broadcasted_iota(jnp.int32, (r1, step), 1) + c0) // lane_div) % gq
        full = jnp.dot(comp_ref[0:r1, :], e_ref[:, c0:c0 + step], preferred_element_type=F32)
        dst_ref[0:r1, c0:c0 + step] = jnp.where(row_grp == lane_grp, full, 0.0).astype(dst_ref.dtype)


def _s5_kernel(u_ref, k2_ref, bc_ref, cc_ref, esc_ref, lam_ref, o_ref, tc_scr, tq_ref, bqt_ref, cq_ref,
               x_scr, w_scr, s_scr, *, rows):
    nb = S5_BLOCK

    @pl.when(pl.program_id(1) == 0)
    def _():
        k2 = k2_ref[...]
        lane = lax.broadcasted_iota(jnp.int32, k2.shape, 1)
        for t in range(nb):
            shifted = k2 if t == 0 else jnp.where(lane >= t * S5_CH, pltpu.roll(k2, t * S5_CH, 1), 0.0)
            tc_scr[t * LANE:(t + 1) * LANE, :] = shifted.astype(tc_scr.dtype)
        _expand_block_diag(tc_scr, esc_ref, tq_ref, S5_CH, S5_CH, causal=True)
        _expand_block_diag(bc_ref, esc_ref, bqt_ref, S5_STATE, S5_CH)
        _expand_block_diag(cc_ref, esc_ref, cq_ref, S5_STATE, S5_CH)

    n_seq = u_ref.shape[0]
    for b in range(n_seq):
        for t in range(nb):
            x_scr[b * rows:(b + 1) * rows, t * LANE:(t + 1) * LANE] = (
                u_ref[b, pl.ds(t, rows, stride=nb), :].astype(x_scr.dtype))
    x = x_scr[...]
    half = w_scr.shape[1] // 2
    w_scr[...] = _dot_nt(x, bqt_ref[...])
    lam_re = lam_ref[0:1, :]
    lam_im = lam_ref[1:2, :]

    def body(j, carry):
        out = []
        for b in range(n_seq):
            s_re, s_im = carry[2 * b], carry[2 * b + 1]
            r = b * rows + j
            s_scr[pl.ds(r, 1), 0:half] = s_re
            s_scr[pl.ds(r, 1), half:] = s_im
            w_re = w_scr[pl.ds(r, 1), 0:half]
            w_im = w_scr[pl.ds(r, 1), half:]
            out += [lam_re * s_re - lam_im * s_im + w_re, lam_re * s_im + lam_im * s_re + w_im]
        return tuple(out)

    zero = jnp.zeros((1, half), F32)
    lax.fori_loop(0, rows, body, (zero,) * (2 * n_seq), unroll=16)
    s_bf = s_scr[...].astype(BF16)
    pair = 2 * LANE
    for c0 in range(0, nb * LANE, pair):
        k_rows = c0 + pair
        y = (jnp.dot(x[:, :k_rows], tq_ref[0:k_rows, c0:c0 + pair], preferred_element_type=F32)
             + jnp.dot(s_bf, cq_ref[:, c0:c0 + pair], preferred_element_type=F32))
        for b in range(n_seq):
            for t in range(c0 // LANE, (c0 + pair) // LANE):
                o_ref[b, pl.ds(t, rows, stride=nb), :] = y[b * rows:(b + 1) * rows, t * LANE - c0:(t + 1) * LANE - c0]


def _s5_scan(p3, k2, bc, cc, lam16):
    batch, seq, _ = p3.shape
    nb = S5_BLOCK
    nq = BRANCH_W // LANE
    rows = seq // nb
    kdim = nb * LANE
    gq = LANE // S5_CH
    ncol = 2 * gq * S5_STATE
    e_sc = (np.eye(nb)[:, None, :, None, None] * np.eye(S5_CH)[None, :, None, None, :] * np.ones((1, 1, 1, gq, 1)))
    e_sc = e_sc.reshape(nb * S5_CH, nb * gq * S5_CH)
    full = lambda shape: pl.BlockSpec(shape, lambda q, b: (0,) * len(shape))
    per_q = lambda r, c: pl.BlockSpec((None, r, c), lambda q, b: (q, 0, 0))
    n_seq = S5_SEQ_PER_STEP if batch % S5_SEQ_PER_STEP == 0 else 1
    return pl.pallas_call(
        functools.partial(_s5_kernel, rows=rows),
        grid=(nq, batch // n_seq),
        in_specs=[pl.BlockSpec((n_seq, seq, LANE), lambda q, b: (b, 0, q)),
                  per_q(LANE, nb * S5_CH), per_q(ncol, nb * S5_CH), per_q(ncol, nb * S5_CH),
                  full(e_sc.shape), per_q(2, ncol // 2)],
        out_specs=pl.BlockSpec((n_seq, seq, LANE), lambda q, b: (b, 0, q)),
        out_shape=jax.ShapeDtypeStruct((batch, seq, BRANCH_W), F32),
        scratch_shapes=[pltpu.VMEM((kdim, nb * S5_CH), BF16),
                        pltpu.VMEM((kdim, kdim), BF16), pltpu.VMEM((ncol, kdim), BF16), pltpu.VMEM((ncol, kdim), BF16),
                        pltpu.VMEM((n_seq * rows, kdim), BF16), pltpu.VMEM((n_seq * rows, ncol), F32),
                        pltpu.VMEM((n_seq * rows, ncol), F32)],
        compiler_params=_cparams(("parallel", "arbitrary")),
        name="s5_scan",
    )(p3, k2, bc, cc, jnp.asarray(e_sc, BF16), lam16)


def _s5_operators(lam_re, lam_im, b_re, b_im, c_re, c_im, d_skip, log_dt):
    nb = S5_BLOCK
    gq = LANE // S5_CH
    nq = S5_GROUPS // gq
    lam = lax.complex(jnp.minimum(lam_re.astype(F32), S5_MAX_REAL), lam_im.astype(F32))
    step = jnp.exp(log_dt.astype(F32))[:, None]
    z = lam * step
    lam_bar = jnp.exp(z)
    b_bar = ((lam_bar - 1.0) / lam)[..., None] * lax.complex(b_re.astype(F32), b_im.astype(F32))
    c_mat = lax.complex(c_re.astype(F32), c_im.astype(F32))
    pw = jnp.exp(z[..., None] * jnp.arange(nb + 1, dtype=F32))
    cp = c_mat[:, None, :, :] * pw[..., :nb].transpose(0, 2, 1)[:, :, None, :]
    cp = jnp.concatenate([cp.real, -cp.imag], axis=-1).reshape(S5_GROUPS, nb * S5_CH, 2 * S5_STATE)
    bri = jnp.concatenate([b_bar.real, b_bar.imag], axis=1)
    kern = jnp.einsum('gnk,gki->gin', cp, bri, precision=HIGHEST)
    skip = (jnp.asarray(np.concatenate([np.eye(S5_CH), np.zeros((S5_CH, (nb - 1) * S5_CH))], axis=1), F32)[None]
            * d_skip.astype(F32).reshape(S5_GROUPS, S5_CH, 1))
    k2 = (kern + skip).reshape(nq, gq * S5_CH, nb * S5_CH)
    pw_rev = jnp.exp(z[..., None] * jnp.asarray(np.arange(nb - 1, -1, -1), F32))
    binc = pw_rev[:, :, :, None] * b_bar[:, :, None, :]
    binc = jnp.stack([binc.real, binc.imag], axis=0).reshape(2, nq, gq * S5_STATE, nb * S5_CH)
    bc = binc.transpose(1, 0, 2, 3).reshape(nq, 2 * gq * S5_STATE, nb * S5_CH)
    cm = c_mat.transpose(0, 2, 1)[:, :, None, :] * pw[..., 1:][:, :, :, None]
    cm = jnp.stack([cm.real, -cm.imag], axis=0).reshape(2, nq, gq * S5_STATE, nb * S5_CH)
    cc = cm.transpose(1, 0, 2, 3).reshape(nq, 2 * gq * S5_STATE, nb * S5_CH)
    lam_n = pw[..., nb].reshape(nq, gq * S5_STATE)
    lam16 = jnp.stack([lam_n.real, lam_n.imag], axis=1)
    return k2, bc.astype(BF16), cc.astype(BF16), lam16


def _merge_kernel(x_ref, sc_ref, sh_ref, gm_ref, ys5_ref, yhg_ref, yret_ref, ym2_ref,
                  wglu_ref, wbr_ref, wg_ref, bg_ref, wout_ref, o_ref):
    x = x_ref[...]
    d = x.shape[1]
    h = _modulated_norm(x, sc_ref[...], sh_ref[...]).astype(BF16)
    y_s5 = jax.nn.gelu(ys5_ref[...])
    y_s5 = y_s5 * jax.nn.sigmoid(jnp.dot(y_s5.astype(BF16), wglu_ref[...], preferred_element_type=F32))
    acc = jnp.zeros(x.shape, F32)
    for n, y in enumerate((y_s5, yhg_ref[...], yret_ref[...], ym2_ref[...])):
        gate = jax.nn.sigmoid(jnp.dot(h, wg_ref[:, n * d:(n + 1) * d], preferred_element_type=F32)
                              + bg_ref[:, n * d:(n + 1) * d])
        acc = acc + gate * jnp.dot(y.astype(BF16), wbr_ref[n], preferred_element_type=F32)
    o_ref[...] = x + gm_ref[...] * jnp.dot(acc.astype(BF16), wout_ref[...], preferred_element_type=F32)


def _merge(x2, mod3, ys5, yhg, yret, ym2, w_glu, w_branch, w_gate, b_gate, w_out, layer, seq):
    t, d = x2.shape
    tm = TM_PROJ
    tpb = seq // tm
    const = lambda shape: pl.BlockSpec((None,) + shape, lambda i: (layer,) + (0,) * len(shape),
                                       pipeline_mode=pl.Buffered(1))
    modspec = lambda k: pl.BlockSpec((None, 1, d), lambda i: ((i // tpb) * 6 + k, 0, 0))
    yspec = pl.BlockSpec((tm, BRANCH_W), lambda i: (i, 0))
    return pl.pallas_call(
        _merge_kernel,
        grid=(t // tm,),
        in_specs=[pl.BlockSpec((tm, d), lambda i: (i, 0)), modspec(1), modspec(0), modspec(2),
                  yspec, yspec, yspec, yspec,
                  const((BRANCH_W, BRANCH_W)), const((4, BRANCH_W, d)), const((d, 4 * d)), const((1, 4 * d)),
                  const((d, d))],
        out_specs=pl.BlockSpec((tm, d), lambda i: (i, 0)),
        out_shape=jax.ShapeDtypeStruct((t, d), F32),
        compiler_params=_cparams(("parallel",)),
        name="merge",
    )(x2, mod3, mod3, mod3, ys5, yhg, yret, ym2, w_glu, w_branch, w_gate, b_gate, w_out)


def _router_kernel(x_ref, sc_ref, sh_ref, wr_ref, br_ref, tri_ref, h_ref, ids_ref, wts_ref, cnt_ref, carry):
    i = pl.program_id(0)

    @pl.when(i == 0)
    def _():
        carry[...] = jnp.zeros_like(carry)

    h = _modulated_norm(x_ref[...], sc_ref[...], sh_ref[...])
    tm, d = h.shape
    packed = _pack_bf16_pairs(h)
    for k in range(N_SLAB):
        h_ref[k] = packed[:, k * SLAB:(k + 1) * SLAB]
    h_hi = h.astype(BF16)
    h_lo = (h - h_hi.astype(F32)).astype(BF16)
    w_r = wr_ref[...]
    w_hi = w_r.astype(BF16)
    w_lo = (w_r - w_hi.astype(F32)).astype(BF16)
    logits = _dot_nt(w_hi, h_hi) + _dot_nt(w_hi, h_lo) + _dot_nt(w_lo, h_hi) + br_ref[:, 0:1]
    gl = [logits[g:g + 1, :] for g in range(MOE_GROUPS)]
    gmax = gl[0]
    gsel = jnp.zeros((1, tm), jnp.int32)
    for g in range(1, MOE_GROUPS):
        better = gl[g] > gmax
        gsel = jnp.where(better, g, gsel)
        gmax = jnp.where(better, gl[g], gmax)
    gden = gl[0] * 0.0
    for g in range(MOE_GROUPS):
        gden = gden + jnp.exp(gl[g] - gmax)
    g_w = 1.0 / gden
    el = []
    for e in range(MOE_EPG):
        v = logits[MOE_GROUPS + e:MOE_GROUPS + e + 1, :]
        for g in range(1, MOE_GROUPS):
            row = MOE_GROUPS + g * MOE_EPG + e
            v = jnp.where(gsel == g, logits[row:row + 1, :], v)
        el.append(v)
    v1 = el[0]
    i1 = jnp.zeros((1, tm), jnp.int32)
    for e in range(1, MOE_EPG):
        better = el[e] > v1
        i1 = jnp.where(better, e, i1)
        v1 = jnp.where(better, el[e], v1)
    v2 = jnp.full((1, tm), -jnp.inf, F32)
    i2 = jnp.zeros((1, tm), jnp.int32)
    for e in range(MOE_EPG):
        better = (el[e] > v2) & (i1 != e)
        i2 = jnp.where(better, e, i2)
        v2 = jnp.where(better, el[e], v2)
    ex = jnp.exp(v2 - v1)
    p1 = 1.0 / (1.0 + ex)
    e1 = gsel * MOE_EPG + i1
    e2 = gsel * MOE_EPG + i2
    erow = lax.broadcasted_iota(jnp.int32, (MOE_EXPERTS, tm), 0)
    oh1 = (erow == e1).astype(F32)
    oh2 = (erow == e2).astype(F32)
    both = oh1 + oh2
    n_grp = tm // LANE
    stacked = jnp.concatenate([both[:, g * LANE:(g + 1) * LANE] for g in range(n_grp)], axis=0)
    within = jnp.dot(stacked.astype(BF16), tri_ref[...], preferred_element_type=F32)
    grp_count = jnp.sum(stacked, axis=1, keepdims=True)
    running = carry[:, 0:1]
    pieces = []
    for g in range(n_grp):
        pieces.append(within[g * MOE_EXPERTS:(g + 1) * MOE_EXPERTS, :] + running)
        running = running + grp_count[g * MOE_EXPERTS:(g + 1) * MOE_EXPERTS, :]
    prefix = jnp.concatenate(pieces, axis=1)
    rank1 = jnp.sum(oh1 * prefix, axis=0, keepdims=True).astype(jnp.int32)
    rank2 = jnp.sum(oh2 * prefix, axis=0, keepdims=True).astype(jnp.int32)
    carry[...] = jnp.broadcast_to(running, carry.shape)
    zi = jnp.zeros((1, tm), jnp.int32)
    ids_ref[...] = jnp.concatenate([e1, e2, rank1, rank2, zi, zi, zi, zi], axis=0)
    wrow = lax.broadcasted_iota(jnp.int32, (LANE, tm), 0)
    wts_ref[...] = jnp.where(wrow == 0, p1 * g_w, jnp.where(wrow == 1, ex * p1 * g_w, 0.0)).T
    cnt_ref[...] = carry[...]


def _router(x2, mod3, w_route, b_route, tri_excl, seq):
    t, d = x2.shape
    tm = TM_PROJ
    tpb = seq // tm
    nr = w_route.shape[0]
    const = lambda shape: pl.BlockSpec(shape, lambda i: (0,) * len(shape))
    modspec = lambda k: pl.BlockSpec((None, 1, d), lambda i: ((i // tpb) * 6 + k, 0, 0))
    return pl.pallas_call(
        _router_kernel,
        grid=(t // tm,),
        in_specs=[pl.BlockSpec((tm, d), lambda i: (i, 0)), modspec(4), modspec(3),
                  const((nr, d)), const((nr, LANE)), const((LANE, LANE))],
        out_specs=[pl.BlockSpec((N_SLAB, tm, SLAB), lambda i: (0, i, 0)),
                   pl.BlockSpec((8, tm), lambda i: (0, i)),
                   pl.BlockSpec((tm, LANE), lambda i: (i, 0)),
                   const((MOE_EXPERTS, LANE))],
        out_shape=[jax.ShapeDtypeStruct((N_SLAB, t, SLAB), jnp.uint32),
                   jax.ShapeDtypeStruct((8, t), jnp.int32),
                   jax.ShapeDtypeStruct((t, LANE), F32),
                   jax.ShapeDtypeStruct((MOE_EXPERTS, LANE), F32)],
        scratch_shapes=[pltpu.VMEM((MOE_EXPERTS, LANE), F32)],
        compiler_params=_cparams(("arbitrary",)),
        name="moe_router",
    )(x2, mod3, mod3, w_route, b_route, tri_excl)


def _sc_mesh():
    return plsc.VectorSubcoreMesh(core_axis_name="core", subcore_axis_name="subcore")


def _slab_rows(idx, n_rows):
    return (idx[None, :] + (jnp.arange(N_SLAB, dtype=jnp.int32) * n_rows)[:, None]).reshape(-1)


def _dispatch(slot1, slot2, h_slabs):
    n_slab, t, d = h_slabs.shape
    n_out = 2 * t
    xs = _scatter_rows(h_slabs.reshape(n_slab * t, d), _slab_rows(slot1, n_out), _slab_rows(slot2, n_out),
                       n_slab * n_out)
    return xs.reshape(n_slab, n_out, d)


def _scatter_rows(src, idx1, idx2, n_out):
    t, d = src.shape
    win = SC_WINDOW

    @pl.kernel(out_type=jax.ShapeDtypeStruct((n_out, d), src.dtype), mesh=_sc_mesh(), name="moe_dispatch_sc")
    def scatter_rows(x_hbm, i1_hbm, i2_hbm, o_hbm):
        def body(x_vmem, i1_vmem, i2_vmem):
            pltpu.sync_copy(x_vmem, o_hbm.at[i1_vmem.at[0]])
            pltpu.sync_copy(x_vmem, o_hbm.at[i2_vmem.at[0]])

        pltpu.emit_pipeline(
            body,
            grid=(t // win,),
            in_specs=[pl.BlockSpec((win, d), lambda i: (i, 0)),
                      pl.BlockSpec((1, win), lambda i: (0, i)),
                      pl.BlockSpec((1, win), lambda i: (0, i))],
            out_specs=[],
            core_axis_name=("core", "subcore"),
            dimension_semantics=(pltpu.PARALLEL,),
        )(x_hbm, i1_hbm, i2_hbm)

    return scatter_rows(src, idx1.reshape(1, t), idx2.reshape(1, t))


def _gather_rows(src, idx):
    m = idx.shape[0]
    d = src.shape[1]
    win = SC_WINDOW

    @pl.kernel(out_type=jax.ShapeDtypeStruct((m, d), src.dtype), mesh=_sc_mesh(), name="moe_gather_sc")
    def gather(x_hbm, i_hbm, o_hbm):
        def body(i_vmem, o_vmem):
            pltpu.sync_copy(x_hbm.at[i_vmem.at[0]], o_vmem)

        pltpu.emit_pipeline(
            body,
            grid=(m // win,),
            in_specs=[pl.BlockSpec((1, win), lambda i: (0, i))],
            out_specs=[pl.BlockSpec((win, d), lambda i: (i, 0))],
            core_axis_name=("core", "subcore"),
            dimension_semantics=(pltpu.PARALLEL,),
        )(i_hbm, o_hbm)

    return gather(src, idx.reshape(1, m))


def _expert_kernel(tile_ref, exp_ref, lo_ref, hi_ref, xs_ref, w1_ref, w3_ref, w2_ref, ys_ref, w1_scr, w3_scr, w2_scr):
    s = pl.program_id(0)
    prev = jnp.maximum(s - 1, 0)
    new_expert = (s == 0) | (exp_ref[s] != exp_ref[prev])
    new_tile = (s == 0) | (tile_ref[s] != tile_ref[prev])

    @pl.when(new_expert)
    def _():
        w1_scr[...] = w1_ref[...].astype(BF16)
        w3_scr[...] = w3_ref[...].astype(BF16)
        w2_scr[...] = w2_ref[...].astype(BF16)

    lo = lo_ref[s]
    hi = hi_ref[s]
    n_rows = xs_ref.shape[1]

    @pl.when(new_tile)
    def _():
        ys_ref[...] = jnp.zeros_like(ys_ref)

    def run_rows(r0, n):
        x = _unpack_bf16_pairs(jnp.concatenate([xs_ref[k, r0:r0 + n, :] for k in range(N_SLAB)], axis=-1)).astype(BF16)
        a = jnp.dot(x, w1_scr[...], preferred_element_type=F32)
        b = jnp.dot(x, w3_scr[...], preferred_element_type=F32)
        act = _silu(a) * b
        y = _pack_bf16_pairs(jnp.dot(act.astype(BF16), w2_scr[...], preferred_element_type=F32))
        row = lax.broadcasted_iota(jnp.int32, (n, SLAB), 0) + r0
        mine = (row >= lo) & (row < hi)
        for k in range(N_SLAB):
            ys_ref[k, r0:r0 + n, :] = jnp.where(mine, y[:, k * SLAB:(k + 1) * SLAB], ys_ref[k, r0:r0 + n, :])

    groups = (hi + (X_SUB - 1)) // X_SUB - lo // X_SUB
    whole = groups > 2

    @pl.when(whole)
    def _():
        run_rows(0, n_rows)

    for r0 in range(0, n_rows, X_SUB):
        @pl.when(jnp.logical_not(whole) & (lo < r0 + X_SUB) & (hi > r0))
        def _(r0=r0):
            run_rows(r0, X_SUB)


def _experts(step_tile, step_expert, step_lo, step_hi, xs, w1, w3, w2, layer):
    n_slab, ns, slab = xs.shape
    d = w1.shape[1]
    ff = w1.shape[2]
    n_steps = step_tile.shape[0]
    base = layer * MOE_EXPERTS
    grid_spec = pltpu.PrefetchScalarGridSpec(
        num_scalar_prefetch=4,
        grid=(n_steps,),
        in_specs=[pl.BlockSpec((n_slab, TM_X, slab), lambda s, tl, ex, lo, hi: (0, tl[s], 0)),
                  pl.BlockSpec((None, d, ff), lambda s, tl, ex, lo, hi: (base + ex[s], 0, 0)),
                  pl.BlockSpec((None, d, ff), lambda s, tl, ex, lo, hi: (base + ex[s], 0, 0)),
                  pl.BlockSpec((None, ff, d), lambda s, tl, ex, lo, hi: (base + ex[s], 0, 0))],
        out_specs=pl.BlockSpec((n_slab, TM_X, slab), lambda s, tl, ex, lo, hi: (0, tl[s], 0)),
        scratch_shapes=[pltpu.VMEM((d, ff), BF16), pltpu.VMEM((d, ff), BF16), pltpu.VMEM((ff, d), BF16)],
    )
    return pl.pallas_call(
        _expert_kernel,
        grid_spec=grid_spec,
        out_shape=jax.ShapeDtypeStruct((n_slab, ns, slab), xs.dtype),
        compiler_params=_cparams(("arbitrary",)),
        name="moe_experts",
    )(step_tile, step_expert, step_lo, step_hi, xs, w1, w3, w2)


def _combine_kernel(x_ref, gate_ref, fw_ref, wcol_ref, y1_ref, y2_ref, o_ref, *, final):
    w_first = wcol_ref[:, 0:1]
    w_second = wcol_ref[:, 1:2]
    y_first = _unpack_bf16_pairs(jnp.concatenate([y1_ref[k] for k in range(N_SLAB)], axis=-1))
    y_second = _unpack_bf16_pairs(jnp.concatenate([y2_ref[k] for k in range(N_SLAB)], axis=-1))
    moe = w_first * y_first + w_second * y_second
    x = x_ref[...] + gate_ref[...] * moe
    if final:
        x = x * lax.rsqrt(jnp.mean(x * x, axis=-1, keepdims=True) + EPS) * fw_ref[...]
    o_ref[...] = x


def _combine(x2, mod3, final_w_row, wcol, gathered, seq, final):
    t, d = x2.shape
    tm = TM_COMB
    tpb = seq // tm
    nblk = t // tm
    yspec = lambda off: pl.BlockSpec((N_SLAB, tm, SLAB), lambda i: (0, i + off, 0))
    return pl.pallas_call(
        functools.partial(_combine_kernel, final=final),
        grid=(nblk,),
        in_specs=[pl.BlockSpec((tm, d), lambda i: (i, 0)),
                  pl.BlockSpec((None, 1, d), lambda i: ((i // tpb) * 6 + 5, 0, 0)),
                  pl.BlockSpec((1, d), lambda i: (0, 0)),
                  pl.BlockSpec((tm, LANE), lambda i: (i, 0)),
                  yspec(0), yspec(nblk)],
        out_specs=pl.BlockSpec((tm, d), lambda i: (i, 0)),
        out_shape=jax.ShapeDtypeStruct((t, d), F32),
        compiler_params=_cparams(("parallel",)),
        name="moe_combine",
    )(x2, mod3, final_w_row, wcol, gathered, gathered)


def _moe(x2, mod3, final_w_row, w_route, b_route, tri_excl, w1, w3, w2, layer, seq, final):
    t, d = x2.shape
    h3, ids, wcol, counts = _router(x2, mod3, w_route, b_route, tri_excl, seq)
    cnt = counts[:, 0].astype(jnp.int32)
    ends = jnp.cumsum(cnt)
    offs = ends - cnt
    experts = jnp.arange(MOE_EXPERTS, dtype=jnp.int32)
    pick = lambda table, idx: jnp.sum(jnp.where(idx[:, None] == experts[None, :], table[None, :], 0), axis=1)
    slot1 = pick(offs, ids[0]) + ids[2]
    slot2 = pick(offs, ids[1]) + ids[3]
    n_tiles = 2 * t // TM_X
    first_tile = offs // TM_X
    n_vis = jnp.where(cnt > 0, (ends - 1) // TM_X - first_tile + 1, 0)
    cum = jnp.cumsum(n_vis)
    step = jnp.arange(n_tiles + MOE_EXPERTS, dtype=jnp.int32)
    step_expert = jnp.minimum(jnp.sum(step[:, None] >= cum[None, :], axis=1), MOE_EXPERTS - 1).astype(jnp.int32)
    valid = step < cum[-1]
    step_tile = jnp.where(valid, pick(first_tile - (cum - n_vis), step_expert) + step, n_tiles - 1)
    step_lo = jnp.where(valid, jnp.clip(pick(offs, step_expert) - step_tile * TM_X, 0, TM_X), 0)
    step_hi = jnp.where(valid, jnp.clip(pick(ends, step_expert) - step_tile * TM_X, 0, TM_X), 0)
    xs = _dispatch(slot1, slot2, h3)
    ys = _experts(step_tile.astype(jnp.int32), step_expert, step_lo.astype(jnp.int32), step_hi.astype(jnp.int32),
                  xs, w1, w3, w2, layer)
    n_sorted = ys.shape[1]
    gathered = _gather_rows(ys.reshape(N_SLAB * n_sorted, SLAB), _slab_rows(jnp.concatenate([slot1, slot2]), n_sorted))
    gathered = gathered.reshape(N_SLAB, n_sorted, SLAB)
    return _combine(x2, mod3, final_w_row, wcol, gathered, seq, final)


def kernel(x, c, positions, ada_w, ada_b, w_in, s5_lam_re, s5_lam_im, s5_b_re, s5_b_im, s5_c_re, s5_c_im, s5_d, s5_log_dt, s5_w_glu, hg_lb_logits, hg_norm_w, m2_conv_w, m2_conv_b, m2_dt_bias, m2_a_log, m2_d, m2_norm_w, w_branch, w_gate, b_gate, w_out, moe_w_group, moe_b_group, moe_w_expert, moe_b_expert, moe_w1, moe_w3, moe_w2, final_norm_w):
    bsz, seq, d = x.shape
    t = bsz * seq
    depth = ada_w.shape[0]
    assert seq % TM_PROJ == 0 and seq % C_RET == 0 and seq % C_SSD == 0 and seq % C_HG == 0
    x2 = x.reshape(t, d).astype(F32)

    c_pad = jnp.zeros((8, d), F32).at[:bsz].set(c.astype(F32))
    mod_all = _ada_mod(c_pad.T, ada_w.astype(F32), ada_b.astype(F32), bsz)

    half = RET_DK // 2
    inv_freq = ROPE_BASE ** (-jnp.arange(half, dtype=F32) / half)
    invf_col = jnp.broadcast_to(inv_freq[:, None], (half, LANE))
    expand = np.tile(np.eye(half, dtype=np.float32), (1, 2 * RET_HEADS))
    sign = np.tile(np.concatenate([-np.ones(half), np.ones(half)]), RET_HEADS)[None, :].astype(np.float32)
    cos_t, sin_t = _rope_tables(positions.reshape(1, t).astype(jnp.int32), invf_col,
                                jnp.asarray(expand, BF16), jnp.asarray(expand * sign, BF16))
    cos3 = cos_t.reshape(bsz, seq, -1)
    sin3 = sin_t.reshape(bsz, seq, -1)

    lb_cum = jnp.cumsum(jax.nn.softmax(hg_lb_logits.astype(F32), axis=0), axis=0)
    hg_lb = lb_cum - lb_cum[:1]
    tri_ssd = jnp.asarray(np.tril(np.ones((C_SSD, C_SSD), np.float32)), BF16)
    tri_excl = jnp.asarray(np.triu(np.ones((LANE, LANE), np.float32), 1), BF16)
    final_w_row = final_norm_w.astype(F32)[None, :]
    n_main = IN_W // LANE * LANE
    w_main = w_in[:, :, :n_main].astype(BF16)
    w_tail = jnp.zeros((depth, d, IN_W_PAD - n_main), BF16).at[:, :, :IN_W - n_main].set(w_in[:, :, n_main:].astype(BF16))
    w_glu_bf = s5_w_glu.astype(BF16)
    w_branch_bf = w_branch.astype(BF16)
    w_gate_bf = w_gate.astype(BF16)
    w_out_bf = w_out.astype(BF16)
    b_gate3 = b_gate.astype(F32).reshape(depth, 1, -1)
    moe_w1_all = moe_w1.astype(F32).reshape(depth * MOE_EXPERTS, d, MOE_FF)
    moe_w3_all = moe_w3.astype(F32).reshape(depth * MOE_EXPERTS, d, MOE_FF)
    moe_w2_all = moe_w2.astype(F32).reshape(depth * MOE_EXPERTS, MOE_FF, d)

    for layer in range(depth):
        mod3 = mod_all[layer, :bsz].reshape(bsz * 6, 1, d)
        p, u_s5 = _in_proj(x2, mod3, w_main, w_tail, layer, seq)
        p3 = p.reshape(bsz, seq, IN_W_PAD)

        ops = _s5_operators(s5_lam_re[layer], s5_lam_im[layer], s5_b_re[layer], s5_b_im[layer],
                            s5_c_re[layer], s5_c_im[layer], s5_d[layer], s5_log_dt[layer])
        y_s5 = _s5_scan(u_s5.reshape(bsz, seq, BRANCH_W), *ops).reshape(t, BRANCH_W)

        lb = hg_lb[layer][None, :]
        y_hg = _hgrn2(p3, jnp.log(lb), jnp.log1p(-lb), hg_norm_w[layer].astype(F32)[None, :]).reshape(t, BRANCH_W)

        y_ret = _retention(p3, cos3, sin3).reshape(t, BRANCH_W)

        pad8 = lambda v: jnp.zeros((1, LANE), F32).at[0, :M2_HEADS].set(v.astype(F32))
        y_m2 = _ssd(p3, tri_ssd, m2_conv_w[layer].astype(F32), m2_conv_b[layer].astype(F32)[None, :],
                    pad8(m2_dt_bias[layer]), pad8(m2_a_log[layer]),
                    jnp.repeat(m2_d[layer].astype(F32), M2_HEADDIM)[None, :],
                    m2_norm_w[layer].astype(F32)[None, :]).reshape(t, BRANCH_W)

        nr = 40
        w_route = jnp.zeros((nr, d), F32).at[:MOE_GROUPS].set(moe_w_group[layer].astype(F32).T)
        w_route = w_route.at[MOE_GROUPS:MOE_GROUPS + MOE_EXPERTS].set(moe_w_expert[layer].astype(F32).T)
        b_route = jnp.zeros((nr, LANE), F32).at[:MOE_GROUPS, 0].set(moe_b_group[layer].astype(F32))
        b_route = b_route.at[MOE_GROUPS:MOE_GROUPS + MOE_EXPERTS, 0].set(moe_b_expert[layer].astype(F32))
        x2 = _merge(x2, mod3, y_s5, y_hg, y_ret, y_m2, w_glu_bf, w_branch_bf, w_gate_bf, b_gate3, w_out_bf,
                    layer, seq)
        x2 = _moe(x2, mod3, final_w_row, w_route, b_route, tri_excl, moe_w1_all, moe_w3_all, moe_w2_all,
                  layer, seq, final=(layer == depth - 1))
    return x2.reshape(bsz, seq, d)
```
